```python
import jax, jax.numpy as jnp
from jax import lax
import numpy as np

D_MODEL = 1024
BATCH = 8
SEQ = 8192
DEPTH = 2

N_MIXERS = 2
EPS = 1e-6

A_HEADS = 8
A_HEAD_DIM = D_MODEL // A_HEADS
A_KEY_DIM = A_HEADS * A_HEAD_DIM
A_VAL_DIM = A_HEADS * A_HEAD_DIM
A_QKV_DIM = 2 * A_KEY_DIM + A_VAL_DIM
A_IN_DIM = A_QKV_DIM + A_VAL_DIM + 2 * A_HEADS
CONV_WIDTH = 4
CHUNK = 64

B_HEADS = 8
B_HEAD_DIM = D_MODEL // B_HEADS
B_WIDTH = B_HEADS * B_HEAD_DIM
B_IN_DIM = 4 * B_WIDTH
Q_BLOCK = 128

N_A_LAYERS = (DEPTH + 1) // 2
N_B_LAYERS = DEPTH // 2

kernel_name = 'hybrid_gated_deltanet_stick_breaking'


def rmsnorm(x, w):
    xf = x.astype(jnp.float32)
    y = xf * lax.rsqrt(jnp.mean(xf * xf, axis=-1, keepdims=True) + EPS) * w.astype(jnp.float32)
    return y.astype(x.dtype)


def l2norm(x):
    xf = x.astype(jnp.float32)
    return xf * lax.rsqrt(jnp.sum(xf * xf, axis=-1, keepdims=True) + EPS)


def causal_depthwise_conv(x, w):
    K, C = w.shape
    return lax.conv_general_dilated(
        x, w[:, None, :].astype(x.dtype), window_strides=(1,), padding=[(K - 1, 0)],
        dimension_numbers=('NWC', 'WIO', 'NWC'), feature_group_count=C)


def chunk_gated_delta_rule(q, k, v, g, beta):
    Bsz, T, H, Dk = q.shape
    Dv = v.shape[-1]
    N = T // CHUNK

    def to_chunks(a):
        a = a.reshape((Bsz, N, CHUNK, H) + a.shape[3:])
        return jnp.moveaxis(a, 3, 2)

    q, k, v, g, beta = map(to_chunks, (q, k, v, g, beta))
    g = jnp.cumsum(g, axis=-1)
    idx = jnp.arange(CHUNK)
    causal = idx[:, None] >= idx[None, :]
    strict = idx[:, None] > idx[None, :]
    decay = jnp.exp(jnp.where(causal, g[..., :, None] - g[..., None, :], -jnp.inf))

    kb = k * beta[..., None]
    vb = v * beta[..., None]
    lower = jnp.where(strict, jnp.einsum('bnhik,bnhjk->bnhij', kb, k) * decay, 0.0)
    tri = lower + jnp.eye(CHUNK, dtype=jnp.float32)
    rhs = jnp.concatenate([vb, kb * jnp.exp(g)[..., None]], axis=-1)
    sol = lax.linalg.triangular_solve(tri, rhs, left_side=True, lower=True)
    u = sol[..., :Dv]
    w = sol[..., Dv:]

    attn_intra = jnp.einsum('bnhik,bnhjk->bnhij', q, k) * decay
    q_g = q * jnp.exp(g)[..., None]
    k_tail = k * jnp.exp(g[..., -1:] - g)[..., None]
    g_last = jnp.exp(g[..., -1])

    xs = tuple(jnp.moveaxis(a, 1, 0) for a in (u, w, attn_intra, q_g, k_tail, g_last))

    def step(S, inp):
        u_i, w_i, a_i, qg_i, kt_i, gl_i = inp
        v_new = u_i - jnp.einsum('bhck,bhkv->bhcv', w_i, S)
        o_i = jnp.einsum('bhck,bhkv->bhcv', qg_i, S) + jnp.einsum('bhij,bhjv->bhiv', a_i, v_new)
        S = S * gl_i[..., None, None] + jnp.einsum('bhck,bhcv->bhkv', kt_i, v_new)
        return S, o_i

    S0 = jnp.zeros((Bsz, H, Dk, Dv), jnp.float32)
    _, o = lax.scan(step, S0, xs)
    return o.transpose(1, 0, 3, 2, 4).reshape(Bsz, T, H, Dv)


def gated_deltanet_branch(h, w_in, conv_w, a_log, dt_bias, o_norm, w_out):
    Bsz, T, _ = h.shape
    proj = h @ w_in
    qkv, z, b_logit, a_logit = jnp.split(
        proj, [A_QKV_DIM, A_QKV_DIM + A_VAL_DIM, A_QKV_DIM + A_VAL_DIM + A_HEADS], axis=-1)
    qkv = jax.nn.silu(causal_depthwise_conv(qkv, conv_w))
    q, k, v = jnp.split(qkv, [A_KEY_DIM, 2 * A_KEY_DIM], axis=-1)
    q = l2norm(q.reshape(Bsz, T, A_HEADS, A_HEAD_DIM)) * (A_HEAD_DIM ** -0.5)
    k = l2norm(k.reshape(Bsz, T, A_HEADS, A_HEAD_DIM))
    v = v.reshape(Bsz, T, A_HEADS, A_HEAD_DIM).astype(jnp.float32)
    beta = jax.nn.sigmoid(b_logit.astype(jnp.float32))
    g = -jnp.exp(a_log.astype(jnp.float32)) * jax.nn.softplus(
        a_logit.astype(jnp.float32) + dt_bias.astype(jnp.float32))
    o = chunk_gated_delta_rule(q, k, v, g, beta)
    o = rmsnorm(o, o_norm) * jax.nn.silu(z.reshape(Bsz, T, A_HEADS, A_HEAD_DIM).astype(jnp.float32))
    return o.reshape(Bsz, T, A_VAL_DIM).astype(h.dtype) @ w_out


def stick_breaking_branch(h, w_in, w_out):
    Bsz, T, _ = h.shape
    proj = h @ w_in
    q, k, v, gate = jnp.split(proj, 4, axis=-1)

    def heads(t):
        return t.reshape(Bsz, T, B_HEADS, B_HEAD_DIM).transpose(0, 2, 1, 3)

    q, k, v = heads(q), heads(k), heads(v)
    nb = T // Q_BLOCK
    q_blocks = q.reshape(Bsz, B_HEADS, nb, Q_BLOCK, B_HEAD_DIM).transpose(2, 0, 1, 3, 4)
    key_pos = jnp.arange(T)
    scale = B_HEAD_DIM ** -0.5

    def block(args):
        q_blk, start = args
        qpos = start + jnp.arange(Q_BLOCK)
        mask = key_pos[None, :] < qpos[:, None]
        z = jnp.einsum('bhqd,bhsd->bhqs', q_blk, k).astype(jnp.float32) * scale
        log_1m = jnp.where(mask, jax.nn.log_sigmoid(-z), 0.0)
        tail = lax.cumsum(log_1m, axis=3, reverse=True) - log_1m
        A = jnp.where(mask, jnp.exp(jax.nn.log_sigmoid(z) + tail), 0.0)
        return jnp.einsum('bhqs,bhsd->bhqd', A.astype(v.dtype), v)

    o = lax.map(block, (q_blocks, jnp.arange(nb) * Q_BLOCK))
    o = o.transpose(1, 0, 3, 2, 4).reshape(Bsz, T, B_WIDTH)
    return (o * jax.nn.silu(gate)) @ w_out


def _fwd_setup_inputs(seed: int = 0) -> dict:
    key = jax.random.key(seed)
    ks = jax.random.split(key, 14)
    f32 = jnp.float32
    x = jax.random.normal(ks[0], (BATCH, SEQ, D_MODEL), f32)
    norm_w = 1.0 + 0.02 * jax.random.normal(ks[1], (DEPTH, D_MODEL), f32)
    a_w_in = jax.random.normal(ks[2], (N_A_LAYERS, D_MODEL, A_IN_DIM), f32) * D_MODEL ** -0.5
    a_conv_w = jax.random.normal(ks[3], (N_A_LAYERS, CONV_WIDTH, A_QKV_DIM), f32) * CONV_WIDTH ** -0.5
    a_a_log = jnp.log(jax.random.uniform(ks[4], (N_A_LAYERS, A_HEADS), f32, 1.0, 16.0))
    dt = jnp.exp(jax.random.uniform(ks[5], (N_A_LAYERS, A_HEADS), f32, np.log(1e-3), np.log(1e-1)))
    a_dt_bias = dt + jnp.log(-jnp.expm1(-dt))
    a_o_norm = 1.0 + 0.02 * jax.random.normal(ks[6], (N_A_LAYERS, A_HEAD_DIM), f32)
    a_w_out = jax.random.normal(ks[7], (N_A_LAYERS, A_VAL_DIM, D_MODEL), f32) * A_VAL_DIM ** -0.5
    b_w_in = jax.random.normal(ks[8], (N_B_LAYERS, D_MODEL, B_IN_DIM), f32) * D_MODEL ** -0.5
    b_w_out = jax.random.normal(ks[9], (N_B_LAYERS, B_WIDTH, D_MODEL), f32) * B_WIDTH ** -0.5
    final_norm_w = 1.0 + 0.02 * jax.random.normal(ks[10], (D_MODEL,), f32)
    return {'x': x, 'norm_w': norm_w, 'a_w_in': a_w_in, 'a_conv_w': a_conv_w, 'a_a_log': a_a_log,
            'a_dt_bias': a_dt_bias, 'a_o_norm': a_o_norm, 'a_w_out': a_w_out,
            'b_w_in': b_w_in, 'b_w_out': b_w_out, 'final_norm_w': final_norm_w}


def _fwd_reference(x, norm_w, a_w_in, a_conv_w, a_a_log, a_dt_bias, a_o_norm, a_w_out, b_w_in, b_w_out, final_norm_w):
    h = x
    for i in range(DEPTH):
        u = rmsnorm(h, norm_w[i])
        j = i // N_MIXERS
        if i % N_MIXERS == 0:
            h = h + gated_deltanet_branch(u, a_w_in[j], a_conv_w[j], a_a_log[j], a_dt_bias[j],
                                          a_o_norm[j], a_w_out[j])
        else:
            h = h + stick_breaking_branch(u, b_w_in[j], b_w_out[j])
    return rmsnorm(h, final_norm_w)


import jax as _jax
import jax.numpy as _jnp

TWIN_FORMAT = 'train_step'
FWD_PARAMS = ['x', 'norm_w', 'a_w_in', 'a_conv_w', 'a_a_log', 'a_dt_bias', 'a_o_norm', 'a_w_out', 'b_w_in', 'b_w_out', 'final_norm_w']
TWIN_WEIGHTS = ['norm_w', 'a_w_in', 'a_conv_w', 'a_a_log', 'a_dt_bias', 'a_o_norm', 'a_w_out', 'b_w_in', 'b_w_out', 'final_norm_w']
TWIN_DIFF_INPUT = 'x'
TWIN_INPUTS = ['x', 'norm_w', 'a_w_in', 'a_conv_w', 'a_a_log', 'a_dt_bias', 'a_o_norm', 'a_w_out', 'b_w_in', 'b_w_out', 'final_norm_w', 'loss_target', 'm_norm_w', 'm_a_w_in', 'm_a_conv_w', 'm_a_a_log', 'm_a_dt_bias', 'm_a_o_norm', 'm_a_w_out', 'm_b_w_in', 'm_b_w_out', 'm_final_norm_w', 'v_norm_w', 'v_a_w_in', 'v_a_conv_w', 'v_a_a_log', 'v_a_dt_bias', 'v_a_o_norm', 'v_a_w_out', 'v_b_w_in', 'v_b_w_out', 'v_final_norm_w']
TWIN_OUTPUTS = ['loss', 'grad_x', 'grad_norm_w', 'grad_a_w_in', 'grad_a_conv_w', 'grad_a_a_log', 'grad_a_dt_bias', 'grad_a_o_norm', 'grad_a_w_out', 'grad_b_w_in', 'grad_b_w_out', 'grad_final_norm_w', 'delta_norm_w', 'delta_a_w_in', 'delta_a_conv_w', 'delta_a_a_log', 'delta_a_dt_bias', 'delta_a_o_norm', 'delta_a_w_out', 'delta_b_w_in', 'delta_b_w_out', 'delta_final_norm_w', 'new_m_norm_w', 'new_m_a_w_in', 'new_m_a_conv_w', 'new_m_a_a_log', 'new_m_a_dt_bias', 'new_m_a_o_norm', 'new_m_a_w_out', 'new_m_b_w_in', 'new_m_b_w_out', 'new_m_final_norm_w', 'new_v_norm_w', 'new_v_a_w_in', 'new_v_a_conv_w', 'new_v_a_a_log', 'new_v_a_dt_bias', 'new_v_a_o_norm', 'new_v_a_w_out', 'new_v_b_w_in', 'new_v_b_w_out', 'new_v_final_norm_w']
TWIN_LEAF_KINDS = {'loss': 'loss', 'grad_x': 'grad_x', 'grad_norm_w': 'grad_w', 'grad_a_w_in': 'grad_w', 'grad_a_conv_w': 'grad_w', 'grad_a_a_log': 'grad_w', 'grad_a_dt_bias': 'grad_w', 'grad_a_o_norm': 'grad_w', 'grad_a_w_out': 'grad_w', 'grad_b_w_in': 'grad_w', 'grad_b_w_out': 'grad_w', 'grad_final_norm_w': 'grad_w', 'delta_norm_w': 'delta_w', 'delta_a_w_in': 'delta_w', 'delta_a_conv_w': 'delta_w', 'delta_a_a_log': 'delta_w', 'delta_a_dt_bias': 'delta_w', 'delta_a_o_norm': 'delta_w', 'delta_a_w_out': 'delta_w', 'delta_b_w_in': 'delta_w', 'delta_b_w_out': 'delta_w', 'delta_final_norm_w': 'delta_w', 'new_m_norm_w': 'new_m', 'new_m_a_w_in': 'new_m', 'new_m_a_conv_w': 'new_m', 'new_m_a_a_log': 'new_m', 'new_m_a_dt_bias': 'new_m', 'new_m_a_o_norm': 'new_m', 'new_m_a_w_out': 'new_m', 'new_m_b_w_in': 'new_m', 'new_m_b_w_out': 'new_m', 'new_m_final_norm_w': 'new_m', 'new_v_norm_w': 'new_v', 'new_v_a_w_in': 'new_v', 'new_v_a_conv_w': 'new_v', 'new_v_a_a_log': 'new_v', 'new_v_a_dt_bias': 'new_v', 'new_v_a_o_norm': 'new_v', 'new_v_a_w_out': 'new_v', 'new_v_b_w_in': 'new_v', 'new_v_b_w_out': 'new_v', 'new_v_final_norm_w': 'new_v'}


def _forward(args):
    return _fwd_reference(*[args[k] for k in FWD_PARAMS])


def _output_shape():
    out = _jax.eval_shape(lambda: _forward(_fwd_setup_inputs(0)))
    return out.shape, out.dtype

N_MICROBATCH = 1
ADAM_LR = 0.001
ADAM_B1 = 0.9
ADAM_B2 = 0.999
ADAM_EPS = 1e-08
ADAM_WD = 0.01
ADAM_STEP = 10
PER_EXAMPLE_BATCH_AXIS = {'x': 0, 'loss_target': 0}
SHARED_INPUTS = []
_WEIGHT_DTYPES = {'norm_w': _jnp.float32, 'a_w_in': _jnp.float32, 'a_conv_w': _jnp.float32, 'a_a_log': _jnp.float32, 'a_dt_bias': _jnp.float32, 'a_o_norm': _jnp.float32, 'a_w_out': _jnp.float32, 'b_w_in': _jnp.float32, 'b_w_out': _jnp.float32, 'final_norm_w': _jnp.float32}
MOMENT_SCALE = {'norm_w': 1.835641e-01, 'a_w_in': 1.135890e-01, 'a_conv_w': 1.042526e-01, 'a_a_log': 5.906082e-01, 'a_dt_bias': 5.693260e-01, 'a_o_norm': 3.825889e-01, 'a_w_out': 1.358987e-01, 'b_w_in': 6.362081e-02, 'b_w_out': 8.105982e-02, 'final_norm_w': 6.397200e+01}


def _to_microbatches(a, axis):
    t = _jnp.moveaxis(a, axis, 0)
    t = t.reshape((N_MICROBATCH, t.shape[0] // N_MICROBATCH) + t.shape[1:])
    return _jnp.moveaxis(t, 1, axis + 1)


def setup_inputs(seed: int = 0) -> dict:
    inp = _fwd_setup_inputs(seed)
    key = _jax.random.fold_in(_jax.random.key(seed), 7919)
    shape, _ = _output_shape()
    out = dict(inp)
    out["loss_target"] = _jax.random.normal(_jax.random.fold_in(key, 0), shape, _jnp.float32)
    for i, name in enumerate(TWIN_WEIGHTS):
        w = inp[name].astype(_jnp.float32)
        if MOMENT_SCALE is None:
            s = _jnp.sqrt(_jnp.mean(_jnp.square(w)) + 1e-30)
        else:
            s = MOMENT_SCALE[name]
        km, kv = _jax.random.split(_jax.random.fold_in(key, i + 1))
        out[name] = w
        out["m_" + name] = s * _jax.random.normal(km, w.shape, _jnp.float32)
        out["v_" + name] = (s * s) * _jax.random.uniform(kv, w.shape, _jnp.float32, 0.5, 1.5)
    if N_MICROBATCH > 1:
        for name, axis in PER_EXAMPLE_BATCH_AXIS.items():
            out[name] = _to_microbatches(out[name], axis)
    return {'x': out['x'], 'norm_w': out['norm_w'], 'a_w_in': out['a_w_in'], 'a_conv_w': out['a_conv_w'], 'a_a_log': out['a_a_log'], 'a_dt_bias': out['a_dt_bias'], 'a_o_norm': out['a_o_norm'], 'a_w_out': out['a_w_out'], 'b_w_in': out['b_w_in'], 'b_w_out': out['b_w_out'], 'final_norm_w': out['final_norm_w'], 'loss_target': out['loss_target'], 'm_norm_w': out['m_norm_w'], 'm_a_w_in': out['m_a_w_in'], 'm_a_conv_w': out['m_a_conv_w'], 'm_a_a_log': out['m_a_a_log'], 'm_a_dt_bias': out['m_a_dt_bias'], 'm_a_o_norm': out['m_a_o_norm'], 'm_a_w_out': out['m_a_w_out'], 'm_b_w_in': out['m_b_w_in'], 'm_b_w_out': out['m_b_w_out'], 'm_final_norm_w': out['m_final_norm_w'], 'v_norm_w': out['v_norm_w'], 'v_a_w_in': out['v_a_w_in'], 'v_a_conv_w': out['v_a_conv_w'], 'v_a_a_log': out['v_a_a_log'], 'v_a_dt_bias': out['v_a_dt_bias'], 'v_a_o_norm': out['v_a_o_norm'], 'v_a_w_out': out['v_a_w_out'], 'v_b_w_in': out['v_b_w_in'], 'v_b_w_out': out['v_b_w_out'], 'v_final_norm_w': out['v_final_norm_w']}


def _loss(weights, diff, rest, loss_target):
    with _jax.named_scope("forward"):
        args = {**rest, TWIN_DIFF_INPUT: diff, **{k: w.astype(_WEIGHT_DTYPES[k]) for k, w in weights.items()}}
        y = _forward(args)
    with _jax.named_scope("loss_head"):
        err = _jnp.square(y.astype(_jnp.float32) - loss_target)
        return 0.5 * _jnp.sum(_jnp.mean(err, axis=-1)) if err.ndim else 0.5 * err


def _adamw(w, g, m, v):
    m = ADAM_B1 * m + (1.0 - ADAM_B1) * g
    v = ADAM_B2 * v + (1.0 - ADAM_B2) * _jnp.square(g)
    m_hat = m / (1.0 - ADAM_B1 ** ADAM_STEP)
    v_hat = v / (1.0 - ADAM_B2 ** ADAM_STEP)
    delta = -ADAM_LR * (m_hat / (_jnp.sqrt(v_hat) + ADAM_EPS) + ADAM_WD * w)
    return delta, m, v


def reference(x, norm_w, a_w_in, a_conv_w, a_a_log, a_dt_bias, a_o_norm, a_w_out, b_w_in, b_w_out, final_norm_w, loss_target, m_norm_w, m_a_w_in, m_a_conv_w, m_a_a_log, m_a_dt_bias, m_a_o_norm, m_a_w_out, m_b_w_in, m_b_w_out, m_final_norm_w, v_norm_w, v_a_w_in, v_a_conv_w, v_a_a_log, v_a_dt_bias, v_a_o_norm, v_a_w_out, v_b_w_in, v_b_w_out, v_final_norm_w):
    given = dict(x=x, norm_w=norm_w, a_w_in=a_w_in, a_conv_w=a_conv_w, a_a_log=a_a_log, a_dt_bias=a_dt_bias, a_o_norm=a_o_norm, a_w_out=a_w_out, b_w_in=b_w_in, b_w_out=b_w_out, final_norm_w=final_norm_w, loss_target=loss_target, m_norm_w=m_norm_w, m_a_w_in=m_a_w_in, m_a_conv_w=m_a_conv_w, m_a_a_log=m_a_a_log, m_a_dt_bias=m_a_dt_bias, m_a_o_norm=m_a_o_norm, m_a_w_out=m_a_w_out, m_b_w_in=m_b_w_in, m_b_w_out=m_b_w_out, m_final_norm_w=m_final_norm_w, v_norm_w=v_norm_w, v_a_w_in=v_a_w_in, v_a_conv_w=v_a_conv_w, v_a_a_log=v_a_a_log, v_a_dt_bias=v_a_dt_bias, v_a_o_norm=v_a_o_norm, v_a_w_out=v_a_w_out, v_b_w_in=v_b_w_in, v_b_w_out=v_b_w_out, v_final_norm_w=v_final_norm_w)
    weights = {n: given[n] for n in TWIN_WEIGHTS}
    shared = {n: given[n] for n in SHARED_INPUTS}
    per_example = {n: given[n] for n in ['x']}
    grad_fn = _jax.value_and_grad(_loss, argnums=(0, 1))

    def one_microbatch(ex, loss_target):
        ex = dict(ex)
        diff = ex.pop(TWIN_DIFF_INPUT)
        return grad_fn(weights, diff, {**shared, **ex}, loss_target)

    if N_MICROBATCH == 1:
        loss, (grad_w, grad_x) = one_microbatch(per_example, given["loss_target"])
    else:
        def body(carry, xs):
            loss_sum, grad_sum = carry
            l_k, (gw_k, gx_k) = one_microbatch(xs[0], xs[1])
            with _jax.named_scope("update"):
                return (loss_sum + l_k, _jax.tree.map(_jnp.add, grad_sum, gw_k)), gx_k

        init = (_jnp.zeros((), _jnp.float32), _jax.tree.map(_jnp.zeros_like, weights))
        (loss, grad_w), grad_x = _jax.lax.scan(body, init, (per_example, given["loss_target"]))
    with _jax.named_scope("update"):
        delta_w, new_m, new_v = {}, {}, {}
        for n in TWIN_WEIGHTS:
            delta_w[n], new_m[n], new_v[n] = _adamw(weights[n], grad_w[n], given["m_" + n], given["v_" + n])
    return (loss, grad_x, *[grad_w[n] for n in TWIN_WEIGHTS], *[delta_w[n] for n in TWIN_WEIGHTS],
            *[new_m[n] for n in TWIN_WEIGHTS], *[new_v[n] for n in TWIN_WEIGHTS])
```

```python
import functools

import jax
import jax.numpy as jnp
from jax import lax
from jax.experimental import pallas as pl
from jax.experimental.pallas import tpu as pltpu

F32 = jnp.float32
BF16 = jnp.bfloat16
EPS = 1e-6
HEAD_DIM = 128
CHUNK = 64
ATTN_BLOCK = 128
LANES = 128
N_DEV = 8
VMEM_LIMIT_BYTES = 48 * 1024 * 1024
ADAM_LR, ADAM_B1, ADAM_B2, ADAM_EPS, ADAM_WD, ADAM_STEP = 0.001, 0.9, 0.999, 1e-08, 0.01, 10
MESH_ID = pl.DeviceIdType.MESH


def _pick(n, candidates):
    for c in candidates:
        if n % c == 0:
            return c
    raise ValueError(f"no tile for {n} in {candidates}")


def _params(n_grid_axes):
    return pltpu.CompilerParams(dimension_semantics=("arbitrary",) * n_grid_axes, vmem_limit_bytes=VMEM_LIMIT_BYTES)


def _dot(a, b):
    return jnp.dot(a.astype(BF16), b.astype(BF16), preferred_element_type=F32)


def _dot_nt(a, b):
    return lax.dot_general(a.astype(BF16), b.astype(BF16), (((1,), (1,)), ((), ())), preferred_element_type=F32)


def _dot_tn(a, b):
    return lax.dot_general(a.astype(BF16), b.astype(BF16), (((0,), (0,)), ((), ())), preferred_element_type=F32)


def _split2(x):
    hi = x.astype(BF16)
    lo = (x - hi.astype(F32)).astype(BF16)
    return hi, lo


def _split3(x):
    hi = x.astype(BF16)
    r = x - hi.astype(F32)
    mid = r.astype(BF16)
    lo = (r - mid.astype(F32)).astype(BF16)
    return hi, mid, lo


def _dot3(a, b):
    a_hi, a_lo = _split2(a)
    b_hi, b_lo = _split2(b)
    d = functools.partial(jnp.dot, preferred_element_type=F32)
    return d(a_hi, b_hi) + (d(a_hi, b_lo) + d(a_lo, b_hi))


def _silu(x):
    return x * jax.nn.sigmoid(x)


def _softplus(x):
    return jnp.maximum(x, 0.0) + jnp.log1p(jnp.exp(-jnp.abs(x)))


def _iota2(shape, axis):
    return lax.broadcasted_iota(jnp.int32, shape, axis)


def _rms_bwd_math(x, w, dy):
    r = lax.rsqrt(jnp.mean(x * x, axis=-1, keepdims=True) + EPS)
    xhat = x * r
    dxhat = dy * w
    dx = r * (dxhat - xhat * jnp.mean(dxhat * xhat, axis=-1, keepdims=True))
    dw = jnp.sum(dy * xhat, axis=0, keepdims=True)
    return dx, dw


def _rmsnorm_fwd(x, w, name):
    T, D = x.shape
    tm = _pick(T, (512, 256, 128))

    def body(x_ref, w_ref, o_ref):
        xf = x_ref[...]
        r = lax.rsqrt(jnp.mean(xf * xf, axis=-1, keepdims=True) + EPS)
        o_ref[...] = (xf * r * w_ref[...]).astype(BF16)

    return pl.pallas_call(
        body, name=name, grid=(T // tm,),
        in_specs=[pl.BlockSpec((tm, D), lambda i: (i, 0)), pl.BlockSpec((1, D), lambda i: (0, 0))],
        out_specs=pl.BlockSpec((tm, D), lambda i: (i, 0)),
        out_shape=jax.ShapeDtypeStruct((T, D), BF16), compiler_params=_params(1),
    )(x, w)


def _rmsnorm_bwd(x, w, du, dres, name):
    T, D = x.shape
    tm = _pick(T, (512, 256, 128))

    def body(x_ref, w_ref, du_ref, dres_ref, dx_ref, dw_ref):
        dx, dw = _rms_bwd_math(x_ref[...], w_ref[...], du_ref[...].astype(F32))
        dx_ref[...] = dres_ref[...] + dx

        @pl.when(pl.program_id(0) == 0)
        def _():
            dw_ref[...] = jnp.zeros_like(dw_ref)

        dw_ref[...] += jnp.broadcast_to(dw, dw_ref.shape)

    return pl.pallas_call(
        body, name=name, grid=(T // tm,),
        in_specs=[pl.BlockSpec((tm, D), lambda i: (i, 0)), pl.BlockSpec((1, D), lambda i: (0, 0)),
                  pl.BlockSpec((tm, D), lambda i: (i, 0)), pl.BlockSpec((tm, D), lambda i: (i, 0))],
        out_specs=[pl.BlockSpec((tm, D), lambda i: (i, 0)), pl.BlockSpec((8, D), lambda i: (0, 0))],
        out_shape=[jax.ShapeDtypeStruct((T, D), F32), jax.ShapeDtypeStruct((8, D), F32)],
        compiler_params=_params(1),
    )(x, w, du, dres)


def _mm_nn(a, b, name, add=None, out_dtype=F32):
    M, K = a.shape
    _, N = b.shape
    tm = _pick(M, (512, 256, 128))
    tn = _pick(N, (512, 256, 128))

    def body(*refs):
        a_ref, b_ref = refs[0], refs[1]
        o_ref = refs[-1]
        acc = _dot(a_ref[...], b_ref[...])
        if add is not None:
            acc = acc + refs[2][...]
        o_ref[...] = acc.astype(out_dtype)

    in_specs = [pl.BlockSpec((tm, K), lambda i, j: (i, 0)), pl.BlockSpec((K, tn), lambda i, j: (0, j))]
    args = [a, b]
    if add is not None:
        in_specs.append(pl.BlockSpec((tm, tn), lambda i, j: (i, j)))
        args.append(add)
    return pl.pallas_call(
        body, name=name, grid=(M // tm, N // tn), in_specs=in_specs,
        out_specs=pl.BlockSpec((tm, tn), lambda i, j: (i, j)),
        out_shape=jax.ShapeDtypeStruct((M, N), out_dtype), compiler_params=_params(2),
    )(*args)


def _mm_nt(a, b, name, out_dtype=F32):
    M, K = a.shape
    N, _ = b.shape
    tm = _pick(M, (512, 256, 128))
    tn = _pick(N, (512, 256, 128))

    def body(a_ref, b_ref, o_ref):
        o_ref[...] = _dot_nt(a_ref[...], b_ref[...]).astype(out_dtype)

    return pl.pallas_call(
        body, name=name, grid=(M // tm, N // tn),
        in_specs=[pl.BlockSpec((tm, K), lambda i, j: (i, 0)), pl.BlockSpec((tn, K), lambda i, j: (j, 0))],
        out_specs=pl.BlockSpec((tm, tn), lambda i, j: (i, j)),
        out_shape=jax.ShapeDtypeStruct((M, N), out_dtype), compiler_params=_params(2),
    )(a, b)


def _mm_tn(a, b, name):
    R, M = a.shape
    _, N = b.shape
    tm = _pick(M, (1024, 512, 256, 128))
    tn = _pick(N, (512, 256, 128))
    tr = _pick(R, (512, 256, 128))
    n_r = R // tr

    def body(a_ref, b_ref, o_ref, acc_ref):
        r = pl.program_id(2)

        @pl.when(r == 0)
        def _():
            acc_ref[...] = jnp.zeros_like(acc_ref)

        acc_ref[...] += _dot_tn(a_ref[...], b_ref[...])

        @pl.when(r == n_r - 1)
        def _():
            o_ref[...] = acc_ref[...]

    return pl.pallas_call(
        body, name=name, grid=(M // tm, N // tn, n_r),
        in_specs=[pl.BlockSpec((tr, tm), lambda i, j, r: (r, i)), pl.BlockSpec((tr, tn), lambda i, j, r: (r, j))],
        out_specs=pl.BlockSpec((tm, tn), lambda i, j, r: (i, j)),
        out_shape=jax.ShapeDtypeStruct((M, N), F32),
        scratch_shapes=[pltpu.VMEM((tm, tn), F32)], compiler_params=_params(3),
    )(a, b)


def _qkv_post(c, j, n_heads):
    s = _silu(c)
    parts = []
    for h in range(n_heads):
        sh = s[:, h * HEAD_DIM:(h + 1) * HEAD_DIM]
        parts.append(sh * lax.rsqrt(jnp.sum(sh * sh, axis=-1, keepdims=True) + EPS))
    n = jnp.concatenate(parts, axis=-1)
    is_q = (j == 0).astype(F32)
    is_v = (j == 2).astype(F32)
    n = n * (1.0 + is_q * (HEAD_DIM ** -0.5 - 1.0))
    return n * (1.0 - is_v) + s * is_v


def _conv_taps(cur, halo_prev):
    tm = cur.shape[0]
    ext = jnp.concatenate([halo_prev, cur], axis=0)
    taps = [pltpu.roll(ext, s, 0)[8:8 + tm] for s in (3, 2, 1)]
    return taps + [cur]


def _conv_fwd(p, conv_w, d_model, name):
    T = p.shape[0]
    D = d_model
    H = D // HEAD_DIM
    tm = _pick(T, (256, 128, 64))

    def body(cur_ref, prev_ref, w_ref, o_ref):
        i, j = pl.program_id(0), pl.program_id(1)
        prev = prev_ref[...] * (i > 0).astype(F32)
        taps = _conv_taps(cur_ref[...], prev)
        w = w_ref[...]
        c = sum(taps[k] * w[k:k + 1, :] for k in range(4))
        o_ref[...] = _qkv_post(c, j, H)

    return pl.pallas_call(
        body, name=name, grid=(T // tm, 3),
        in_specs=[pl.BlockSpec((tm, D), lambda i, j: (i, j)),
                  pl.BlockSpec((8, D), lambda i, j: (jnp.maximum(i * (tm // 8) - 1, 0), j)),
                  pl.BlockSpec((4, D), lambda i, j: (0, j))],
        out_specs=pl.BlockSpec((tm, D), lambda i, j: (i, j)),
        out_shape=jax.ShapeDtypeStruct((T, 3 * D), F32), compiler_params=_params(2),
    )(p, p, conv_w)


def _chunk_tri(tm, upper):
    r, c = _iota2((tm, tm), 0), _iota2((tm, tm), 1)
    same = (r // CHUNK) == (c // CHUNK)
    tri = (c >= r) if upper else (c <= r)
    return jnp.where(same & tri, 1.0, 0.0).astype(BF16)


def _dot_mask(mask_bf16, x):
    hi, mid, lo = _split3(x)
    d = functools.partial(jnp.dot, preferred_element_type=F32)
    return d(mask_bf16, hi) + (d(mask_bf16, mid) + d(mask_bf16, lo))


def _gates_math(pb, pa, a_log, dt_bias):
    beta = jax.nn.sigmoid(pb)
    g = -jnp.exp(a_log) * _softplus(pa + dt_bias)
    return beta, g


def _gates_fwd(p, gate_params, col0, name):
    T = p.shape[0]
    tm = _pick(T, (256, 128, 64))

    def body(pb_ref, pa_ref, gp_ref, beta_ref, gc_ref):
        gp = gp_ref[...]
        beta, g = _gates_math(pb_ref[...], pa_ref[...], gp[0:1, :], gp[1:2, :])
        beta_ref[...] = beta
        gc_ref[...] = _dot_mask(_chunk_tri(tm, upper=False), g)

    return pl.pallas_call(
        body, name=name, grid=(T // tm,),
        in_specs=[pl.BlockSpec((tm, LANES), lambda i: (i, col0)), pl.BlockSpec((tm, LANES), lambda i: (i, col0 + 1)),
                  pl.BlockSpec((8, LANES), lambda i: (0, 0))],
        out_specs=[pl.BlockSpec((tm, LANES), lambda i: (i, 0))] * 2,
        out_shape=[jax.ShapeDtypeStruct((T, LANES), F32)] * 2, compiler_params=_params(1),
    )(p, p, gate_params)


def _col_to_row(col):
    C = col.shape[0]
    eye = _iota2((C, C), 0) == _iota2((C, C), 1)
    return jnp.sum(jnp.where(eye, col, 0.0), axis=0, keepdims=True)


def _unit_lower_inverse(low):
    C = low.shape[0]
    eye = (_iota2((C, C), 0) == _iota2((C, C), 1)).astype(F32)
    t = eye - low
    p = _dot3(low, low)
    n = 2
    while True:
        t = t + _dot3(t, p)
        n *= 2
        if n >= C:
            return t
        p = _dot3(p, p)


def _chunk_head_fwd(q, k, v, gc, beta, s_in):
    C = q.shape[0]
    r, c = _iota2((C, C), 0), _iota2((C, C), 1)
    causal, strict = r >= c, r > c
    decay = jnp.where(causal, jnp.exp(jnp.where(causal, gc - _col_to_row(gc), 0.0)), 0.0)
    kb, vb = k * beta, v * beta
    low = jnp.where(strict, _dot_nt(kb, k) * decay, 0.0)
    t_inv = _unit_lower_inverse(low)
    eg = jnp.exp(gc)
    u = _dot(t_inv, vb)
    w = _dot(t_inv, kb * eg)
    attn = _dot_nt(q, k) * decay
    g_last = gc[C - 1:C, :]
    k_tail = k * jnp.exp(g_last - gc)
    v_new = u - _dot(w, s_in)
    o = _dot(q * eg, s_in) + _dot(attn, v_new)
    s_out = s_in * jnp.exp(g_last) + _dot_tn(k_tail, v_new)
    return o, s_out, t_inv


def _chunk_fwd(qkv, beta, gc, d_model, name):
    T = qkv.shape[0]
    D = d_model
    H = D // HEAD_DIM
    N = T // CHUNK

    def body(q_ref, k_ref, v_ref, beta_ref, gc_ref, o_ref, s_all_ref, t_all_ref, s_ref):
        @pl.when(pl.program_id(0) == 0)
        def _():
            s_ref[...] = jnp.zeros_like(s_ref)

        for h in range(H):
            hs = slice(h * HEAD_DIM, (h + 1) * HEAD_DIM)
            s_in = s_ref[h]
            s_all_ref[0, h] = s_in
            o, s_out, t_inv = _chunk_head_fwd(q_ref[:, hs], k_ref[:, hs], v_ref[:, hs],
                                              gc_ref[:, h:h + 1], beta_ref[:, h:h + 1], s_in)
            o_ref[:, hs] = o
            s_ref[h] = s_out
            t_all_ref[0, h] = t_inv

    return pl.pallas_call(
        body, name=name, grid=(N,),
        in_specs=[pl.BlockSpec((CHUNK, D), lambda n: (n, 0)), pl.BlockSpec((CHUNK, D), lambda n: (n, 1)),
                  pl.BlockSpec((CHUNK, D), lambda n: (n, 2)),
                  pl.BlockSpec((CHUNK, LANES), lambda n: (n, 0)), pl.BlockSpec((CHUNK, LANES), lambda n: (n, 0))],
        out_specs=[pl.BlockSpec((CHUNK, D), lambda n: (n, 0)),
                   pl.BlockSpec((1, H, HEAD_DIM, HEAD_DIM), lambda n: (n, 0, 0, 0)),
                   pl.BlockSpec((1, H, CHUNK, CHUNK), lambda n: (n, 0, 0, 0))],
        out_shape=[jax.ShapeDtypeStruct((T, D), F32), jax.ShapeDtypeStruct((N, H, HEAD_DIM, HEAD_DIM), F32),
                   jax.ShapeDtypeStruct((N, H, CHUNK, CHUNK), F32)],
        scratch_shapes=[pltpu.VMEM((H, HEAD_DIM, HEAD_DIM), F32)], compiler_params=_params(1),
    )(qkv, qkv, qkv, beta, gc)


def _onorm_gate_math(o, z, w, n_heads):
    parts = []
    for h in range(n_heads):
        hs = slice(h * HEAD_DIM, (h + 1) * HEAD_DIM)
        oh = o[:, hs]
        y = oh * lax.rsqrt(jnp.mean(oh * oh, axis=-1, keepdims=True) + EPS) * w
        parts.append(y * _silu(z[:, hs]))
    return jnp.concatenate(parts, axis=-1)


def _onorm_gate_fwd(o, p, z_col, o_norm, name):
    T, D = o.shape
    H = D // HEAD_DIM
    tm = _pick(T, (256, 128, 64))

    def body(o_ref, z_ref, w_ref, y_ref):
        y_ref[...] = _onorm_gate_math(o_ref[...], z_ref[...], w_ref[...], H).astype(BF16)

    return pl.pallas_call(
        body, name=name, grid=(T // tm,),
        in_specs=[pl.BlockSpec((tm, D), lambda i: (i, 0)), pl.BlockSpec((tm, D), lambda i: (i, z_col)),
                  pl.BlockSpec((1, HEAD_DIM), lambda i: (0, 0))],
        out_specs=pl.BlockSpec((tm, D), lambda i: (i, 0)),
        out_shape=jax.ShapeDtypeStruct((T, D), BF16), compiler_params=_params(1),
    )(o, p, o_norm)


def _log_one_minus(z):
    return jnp.minimum(-z, 0.0) - jnp.log1p(jnp.exp(-jnp.abs(z)))


def _incl_lower(n):
    return jnp.where(_iota2((n, n), 0) >= _iota2((n, n), 1), 1.0, 0.0).astype(BF16)


def _incl_upper(n):
    return jnp.where(_iota2((n, n), 0) <= _iota2((n, n), 1), 1.0, 0.0).astype(BF16)


def _dot_cum(x, tri_bf16):
    hi, lo = _split2(x)
    d = functools.partial(jnp.dot, preferred_element_type=F32)
    return d(hi, tri_bf16) + d(lo, tri_bf16)


def _sb_fwd(qkv, d_model, name):
    T = qkv.shape[0]
    D = d_model
    H = D // HEAD_DIM
    B = ATTN_BLOCK
    nq = T // B
    scale = HEAD_DIM ** -0.5

    def body(q_ref, k_ref, v_ref, o_ref, r_ref):
        i = pl.program_id(1)
        q = q_ref[...]
        tri = _incl_lower(B)
        strict = _iota2((B, B), 0) > _iota2((B, B), 1)

        def block(j, masked, acc, carry):
            start = pl.multiple_of(j * B, B)
            kj = k_ref[pl.ds(start, B), :]
            vj = v_ref[pl.ds(start, B), :]
            z = _dot_nt(q, kj) * scale
            lm = _log_one_minus(z)
            if masked:
                lm = jnp.where(strict, lm, 0.0)
            cum = _dot_cum(lm, tri)
            a = jnp.exp(z + cum + carry)
            if masked:
                a = jnp.where(strict, a, 0.0)
            return acc + _dot(a, vj), carry + cum[:, 0:1]

        acc, carry = block(i, True, jnp.zeros((B, HEAD_DIM), F32), jnp.zeros((B, 1), F32))
        acc, carry = lax.fori_loop(0, i, lambda it, ac: block(i - 1 - it, False, *ac), (acc, carry))
        o_ref[...] = acc
        r_ref[0] = jnp.broadcast_to(carry, (B, LANES))

    return pl.pallas_call(
        body, name=name, grid=(H, nq),
        in_specs=[pl.BlockSpec((B, HEAD_DIM), lambda h, i: (i, h)),
                  pl.BlockSpec((T, HEAD_DIM), lambda h, i: (0, H + h)),
                  pl.BlockSpec((T, HEAD_DIM), lambda h, i: (0, 2 * H + h))],
        out_specs=[pl.BlockSpec((B, HEAD_DIM), lambda h, i: (i, h)),
                   pl.BlockSpec((1, B, LANES), lambda h, i: (h, i, 0))],
        out_shape=[jax.ShapeDtypeStruct((T, D), F32), jax.ShapeDtypeStruct((H, T, LANES), F32)],
        compiler_params=_params(2),
    )(qkv, qkv, qkv)


def _gate_mul_fwd(o, gate, name):
    T, D = o.shape
    tm = _pick(T, (512, 256, 128))

    def body(o_ref, g_ref, y_ref):
        y_ref[...] = (o_ref[...] * _silu(g_ref[...])).astype(BF16)

    spec = pl.BlockSpec((tm, D), lambda i: (i, 0))
    return pl.pallas_call(body, name=name, grid=(T // tm,), in_specs=[spec, spec], out_specs=spec,
                          out_shape=jax.ShapeDtypeStruct((T, D), BF16), compiler_params=_params(1))(o, gate)


def _final_loss(h, w, target, name):
    T, D = h.shape
    tm = _pick(T, (512, 256, 128))

    def body(h_ref, w_ref, t_ref, dh_ref, loss_ref, dw_ref):
        x, w = h_ref[...], w_ref[...]
        r = lax.rsqrt(jnp.mean(x * x, axis=-1, keepdims=True) + EPS)
        err = x * r * w - t_ref[...]
        part = 0.5 * jnp.sum(jnp.mean(err * err, axis=-1, keepdims=True), axis=0, keepdims=True)
        dx, dw = _rms_bwd_math(x, w, err * (1.0 / D))
        dh_ref[...] = dx

        @pl.when(pl.program_id(0) == 0)
        def _():
            loss_ref[...] = jnp.zeros_like(loss_ref)
            dw_ref[...] = jnp.zeros_like(dw_ref)

        loss_ref[...] += jnp.broadcast_to(part, loss_ref.shape)
        dw_ref[...] += jnp.broadcast_to(dw, dw_ref.shape)

    return pl.pallas_call(
        body, name=name, grid=(T // tm,),
        in_specs=[pl.BlockSpec((tm, D), lambda i: (i, 0)), pl.BlockSpec((1, D), lambda i: (0, 0)),
                  pl.BlockSpec((tm, D), lambda i: (i, 0))],
        out_specs=[pl.BlockSpec((tm, D), lambda i: (i, 0)), pl.BlockSpec((8, LANES), lambda i: (0, 0)),
                   pl.BlockSpec((8, D), lambda i: (0, 0))],
        out_shape=[jax.ShapeDtypeStruct((T, D), F32), jax.ShapeDtypeStruct((8, LANES), F32),
                   jax.ShapeDtypeStruct((8, D), F32)],
        compiler_params=_params(1),
    )(h, w, target)


def _gate_mul_bwd(dy, o, gate, name):
    T, D = o.shape
    tm = _pick(T, (512, 256, 128))

    def body(dy_ref, o_ref, g_ref, do_ref, dg_ref):
        dy, g = dy_ref[...], g_ref[...]
        s = jax.nn.sigmoid(g)
        do_ref[...] = dy * (g * s)
        dg_ref[...] = (dy * o_ref[...] * (s + g * s * (1.0 - s))).astype(BF16)

    spec = pl.BlockSpec((tm, D), lambda i: (i, 0))
    return pl.pallas_call(body, name=name, grid=(T // tm,), in_specs=[spec] * 3, out_specs=[spec] * 2,
                          out_shape=[jax.ShapeDtypeStruct((T, D), F32), jax.ShapeDtypeStruct((T, D), BF16)],
                          compiler_params=_params(1))(dy, o, gate)


def _sb_bwd(qkv, do, r_tot, d_model, name):
    T = qkv.shape[0]
    D = d_model
    H = D // HEAD_DIM
    B = ATTN_BLOCK
    nq = T // B
    scale = HEAD_DIM ** -0.5

    def body(q_ref, k_ref, v_ref, do_ref, r_ref, dq_ref, dk_ref, dv_ref, dk_acc, dv_acc):
        i = pl.program_id(1)

        @pl.when(i == 0)
        def _():
            dk_acc[...] = jnp.zeros_like(dk_acc)
            dv_acc[...] = jnp.zeros_like(dv_acc)

        q = q_ref[...]
        do_blk = do_ref[...].astype(BF16)
        row_total = r_ref[0][:, 0:1]
        tri_rev = _incl_lower(B)
        tri_fwd = _incl_upper(B)
        strict = _iota2((B, B), 0) > _iota2((B, B), 1)

        def block(j, masked, dq, left_l, left_p):
            start = pl.multiple_of(j * B, B)
            kj = k_ref[pl.ds(start, B), :]
            vj = v_ref[pl.ds(start, B), :]
            z = _dot_nt(q, kj) * scale
            lm = _log_one_minus(z)
            if masked:
                lm = jnp.where(strict, lm, 0.0)
            cum = _dot_cum(lm, tri_rev)
            block_l = cum[:, 0:1]
            a = jnp.exp(z + cum + (row_total - left_l - block_l))
            if masked:
                a = jnp.where(strict, a, 0.0)
            p = _dot_nt(do_blk, vj) * a
            cum_p = _dot_cum(p, tri_fwd) + left_p
            dz = p - jax.nn.sigmoid(z) * cum_p
            if masked:
                dz = jnp.where(strict, dz, 0.0)
            dz = (dz * scale).astype(BF16)
            dk_acc[pl.ds(start, B), :] += _dot_tn(dz, q)
            dv_acc[pl.ds(start, B), :] += _dot_tn(a, do_blk)
            return dq + _dot(dz, kj), left_l + block_l, cum_p[:, B - 1:B]

        zero_col = jnp.zeros((B, 1), F32)
        carry = lax.fori_loop(0, i, lambda j, c: block(j, False, *c), (jnp.zeros((B, HEAD_DIM), F32), zero_col, zero_col))
        dq, _, _ = block(i, True, *carry)
        dq_ref[...] = dq.astype(BF16)

        @pl.when(i == nq - 1)
        def _():
            dk_ref[...] = dk_acc[...].astype(BF16)
            dv_ref[...] = dv_acc[...].astype(BF16)

    return pl.pallas_call(
        body, name=name, grid=(H, nq),
        in_specs=[pl.BlockSpec((B, HEAD_DIM), lambda h, i: (i, h)),
                  pl.BlockSpec((T, HEAD_DIM), lambda h, i: (0, H + h)),
                  pl.BlockSpec((T, HEAD_DIM), lambda h, i: (0, 2 * H + h)),
                  pl.BlockSpec((B, HEAD_DIM), lambda h, i: (i, h)),
                  pl.BlockSpec((1, B, LANES), lambda h, i: (h, i, 0))],
        out_specs=[pl.BlockSpec((B, HEAD_DIM), lambda h, i: (i, h)),
                   pl.BlockSpec((T, HEAD_DIM), lambda h, i: (0, h)),
                   pl.BlockSpec((T, HEAD_DIM), lambda h, i: (0, h))],
        out_shape=[jax.ShapeDtypeStruct((T, D), BF16)] * 3,
        scratch_shapes=[pltpu.VMEM((T, HEAD_DIM), F32), pltpu.VMEM((T, HEAD_DIM), F32)],
        compiler_params=_params(2),
    )(qkv, qkv, qkv, do, r_tot)


def _onorm_gate_bwd(dy, o, p, z_col, o_norm, name):
    T, D = o.shape
    H = D // HEAD_DIM
    tm = _pick(T, (256, 128, 64))

    def body(dy_ref, o_ref, z_ref, w_ref, do_ref, dz_ref, dw_ref):
        _, vjp = jax.vjp(functools.partial(_onorm_gate_math, n_heads=H), o_ref[...], z_ref[...], w_ref[...])
        do, dz, dw = vjp(dy_ref[...])
        do_ref[...] = do
        dz_ref[...] = dz.astype(BF16)

        @pl.when(pl.program_id(0) == 0)
        def _():
            dw_ref[...] = jnp.zeros_like(dw_ref)

        dw_ref[...] += jnp.broadcast_to(dw, dw_ref.shape)

    return pl.pallas_call(
        body, name=name, grid=(T // tm,),
        in_specs=[pl.BlockSpec((tm, D), lambda i: (i, 0)), pl.BlockSpec((tm, D), lambda i: (i, 0)),
                  pl.BlockSpec((tm, D), lambda i: (i, z_col)), pl.BlockSpec((1, HEAD_DIM), lambda i: (0, 0))],
        out_specs=[pl.BlockSpec((tm, D), lambda i: (i, 0)), pl.BlockSpec((tm, D), lambda i: (i, 0)),
                   pl.BlockSpec((8, HEAD_DIM), lambda i: (0, 0))],
        out_shape=[jax.ShapeDtypeStruct((T, D), F32), jax.ShapeDtypeStruct((T, D), BF16),
                   jax.ShapeDtypeStruct((8, HEAD_DIM), F32)],
        compiler_params=_params(1),
    )(dy, o, p, o_norm)


def _row_to_col(row):
    C = row.shape[1]
    eye = _iota2((C, C), 0) == _iota2((C, C), 1)
    return jnp.sum(jnp.where(eye, row, 0.0), axis=1, keepdims=True)


def _lane_sum(x):
    return jnp.sum(x, axis=-1, keepdims=True)


def _chunk_head_bwd(q, k, v, gc, beta, s_in, t_inv, do, ds_out):
    C = q.shape[0]
    r, c = _iota2((C, C), 0), _iota2((C, C), 1)
    causal, strict = r >= c, r > c
    decay = jnp.where(causal, jnp.exp(jnp.where(causal, gc - _col_to_row(gc), 0.0)), 0.0)
    kb, vb = k * beta, v * beta
    kk = _dot_nt(kb, k)
    low = jnp.where(strict, kk * decay, 0.0)
    eg = jnp.exp(gc)
    kbg = kb * eg
    u = _dot(t_inv, vb)
    w = _dot(t_inv, kbg)
    qk = _dot_nt(q, k)
    attn = qk * decay
    g_last = gc[C - 1:C, :]
    e_tail = jnp.exp(g_last - gc)
    k_tail = k * e_tail
    gl = jnp.exp(g_last)
    qg = q * eg
    v_new = u - _dot(w, s_in)

    d_vnew = _dot_tn(attn, do) + _dot(k_tail, ds_out)
    d_ktail = _dot_nt(v_new, ds_out)
    d_gl = jnp.sum(_lane_sum(s_in * ds_out), axis=0, keepdims=True)
    d_qg = _dot_nt(do, s_in)
    d_attn = jnp.where(causal, _dot_nt(do, v_new), 0.0)
    d_w = -_dot_nt(d_vnew, s_in)
    ds_in = ds_out * gl + _dot_tn(qg, do) - _dot_tn(w, d_vnew)
    d_vb = _dot_tn(t_inv, d_vnew)
    d_kbg = _dot_tn(t_inv, d_w)
    d_tinv = _dot_nt(d_vnew, vb) + _dot_nt(d_w, kbg)
    d_low = jnp.where(strict, -_dot_nt(_dot_tn(t_inv, d_tinv), t_inv), 0.0)
    d_kk = d_low * decay
    d_qk = d_attn * decay
    d_kb = _dot(d_kk, k) + d_kbg * eg
    dq = _dot(d_qk, k) + d_qg * eg
    dk = _dot_tn(d_kk, kb) + _dot_tn(d_qk, q) + d_ktail * e_tail + d_kb * beta
    dv = d_vb * beta
    dbeta = _lane_sum(d_kb * k) + _lane_sum(d_vb * v)
    m = d_low * low + d_attn * attn
    tail_term = _lane_sum(d_ktail * k_tail)
    d_g_last = d_gl * gl + jnp.sum(tail_term, axis=0, keepdims=True)
    dgc = (_lane_sum(m) - _row_to_col(jnp.sum(m, axis=0, keepdims=True))
           + _lane_sum(d_qg * qg) + _lane_sum(d_kbg * kbg) - tail_term)
    dgc = dgc + jnp.where(_iota2((C, 1), 0) == C - 1, d_g_last, 0.0)
    return dq, dk, dv, dgc, dbeta, ds_in


def _chunk_bwd(qkv, beta, gc, s_all, t_all, do, d_model, name):
    T = qkv.shape[0]
    D = d_model
    H = D // HEAD_DIM
    N = T // CHUNK

    def body(q_ref, k_ref, v_ref, beta_ref, gc_ref, s_ref, t_ref, do_ref, dqkv_ref, dbeta_ref, dg_ref, ds_ref):
        @pl.when(pl.program_id(0) == 0)
        def _():
            ds_ref[...] = jnp.zeros_like(ds_ref)

        lane = _iota2((CHUNK, LANES), 1)
        dgc_all = jnp.zeros((CHUNK, LANES), F32)
        dbeta_all = jnp.zeros((CHUNK, LANES), F32)
        for h in range(H):
            hs = slice(h * HEAD_DIM, (h + 1) * HEAD_DIM)
            dq, dk, dv, dgc, dbeta, ds_in = _chunk_head_bwd(
                q_ref[:, hs], k_ref[:, hs], v_ref[:, hs], gc_ref[:, h:h + 1], beta_ref[:, h:h + 1],
                s_ref[0, h], t_ref[0, h], do_ref[:, hs], ds_ref[h])
            ds_ref[h] = ds_in
            dqkv_ref[:, h * HEAD_DIM:(h + 1) * HEAD_DIM] = dq
            dqkv_ref[:, D + h * HEAD_DIM:D + (h + 1) * HEAD_DIM] = dk
            dqkv_ref[:, 2 * D + h * HEAD_DIM:2 * D + (h + 1) * HEAD_DIM] = dv
            dgc_all = jnp.where(lane == h, dgc, dgc_all)
            dbeta_all = jnp.where(lane == h, dbeta, dbeta_all)
        dbeta_ref[...] = dbeta_all
        dg_ref[...] = _dot_mask(_chunk_tri(CHUNK, upper=True), dgc_all)

    rev = lambda n: N - 1 - n
    return pl.pallas_call(
        body, name=name, grid=(N,),
        in_specs=[pl.BlockSpec((CHUNK, D), lambda n: (rev(n), 0)), pl.BlockSpec((CHUNK, D), lambda n: (rev(n), 1)),
                  pl.BlockSpec((CHUNK, D), lambda n: (rev(n), 2)),
                  pl.BlockSpec((CHUNK, LANES), lambda n: (rev(n), 0)), pl.BlockSpec((CHUNK, LANES), lambda n: (rev(n), 0)),
                  pl.BlockSpec((1, H, HEAD_DIM, HEAD_DIM), lambda n: (rev(n), 0, 0, 0)),
                  pl.BlockSpec((1, H, CHUNK, CHUNK), lambda n: (rev(n), 0, 0, 0)),
                  pl.BlockSpec((CHUNK, D), lambda n: (rev(n), 0))],
        out_specs=[pl.BlockSpec((CHUNK, 3 * D), lambda n: (rev(n), 0)),
                   pl.BlockSpec((CHUNK, LANES), lambda n: (rev(n), 0)), pl.BlockSpec((CHUNK, LANES), lambda n: (rev(n), 0))],
        out_shape=[jax.ShapeDtypeStruct((T, 3 * D), F32), jax.ShapeDtypeStruct((T, LANES), F32),
                   jax.ShapeDtypeStruct((T, LANES), F32)],
        scratch_shapes=[pltpu.VMEM((H, HEAD_DIM, HEAD_DIM), F32)], compiler_params=_params(1),
    )(qkv, qkv, qkv, beta, gc, s_all, t_all, do)


def _gates_bwd(p, gate_params, col0, dbeta, dg, name):
    T = p.shape[0]
    tm = _pick(T, (256, 128, 64))

    def body(pb_ref, pa_ref, gp_ref, dbeta_ref, dg_ref, dp_ref, dgp_ref):
        gp = gp_ref[...]
        _, vjp = jax.vjp(_gates_math, pb_ref[...], pa_ref[...], gp[0:1, :], gp[1:2, :])
        dpb, dpa, d_alog, d_dt = vjp((dbeta_ref[...], dg_ref[...]))
        dp_ref[:, 0:LANES] = dpb.astype(BF16)
        dp_ref[:, LANES:2 * LANES] = dpa.astype(BF16)

        @pl.when(pl.program_id(0) == 0)
        def _():
            dgp_ref[...] = jnp.zeros_like(dgp_ref)

        dgp_ref[0:1, :] += d_alog
        dgp_ref[1:2, :] += d_dt

    return pl.pallas_call(
        body, name=name, grid=(T // tm,),
        in_specs=[pl.BlockSpec((tm, LANES), lambda i: (i, col0)), pl.BlockSpec((tm, LANES), lambda i: (i, col0 + 1)),
                  pl.BlockSpec((8, LANES), lambda i: (0, 0)),
                  pl.BlockSpec((tm, LANES), lambda i: (i, 0)), pl.BlockSpec((tm, LANES), lambda i: (i, 0))],
        out_specs=[pl.BlockSpec((tm, 2 * LANES), lambda i: (i, 0)), pl.BlockSpec((8, LANES), lambda i: (0, 0))],
        out_shape=[jax.ShapeDtypeStruct((T, 2 * LANES), BF16), jax.ShapeDtypeStruct((8, LANES), F32)],
        compiler_params=_params(1),
    )(p, p, gate_params, dbeta, dg)


def _conv_bwd_act(p, conv_w, dqkv, d_model, name):
    T = p.shape[0]
    D = d_model
    H = D // HEAD_DIM
    tm = _pick(T, (256, 128, 64))

    def body(cur_ref, prev_ref, w_ref, dout_ref, dc_ref, dw_ref):
        j, i = pl.program_id(0), pl.program_id(1)
        prev = prev_ref[...] * (i > 0).astype(F32)
        taps = _conv_taps(cur_ref[...], prev)
        w = w_ref[...]
        c = sum(taps[k] * w[k:k + 1, :] for k in range(4))
        _, vjp = jax.vjp(lambda cc: _qkv_post(cc, j, H), c)
        (dc,) = vjp(dout_ref[...])
        dc_ref[...] = dc

        @pl.when(i == 0)
        def _():
            dw_ref[...] = jnp.zeros_like(dw_ref)

        for k in range(4):
            dw_ref[k:k + 1, :] += jnp.sum(dc * taps[k], axis=0, keepdims=True)

    return pl.pallas_call(
        body, name=name, grid=(3, T // tm),
        in_specs=[pl.BlockSpec((tm, D), lambda j, i: (i, j)),
                  pl.BlockSpec((8, D), lambda j, i: (jnp.maximum(i * (tm // 8) - 1, 0), j)),
                  pl.BlockSpec((4, D), lambda j, i: (0, j)),
                  pl.BlockSpec((tm, D), lambda j, i: (i, j))],
        out_specs=[pl.BlockSpec((tm, D), lambda j, i: (i, j)), pl.BlockSpec((4, D), lambda j, i: (0, j))],
        out_shape=[jax.ShapeDtypeStruct((T, 3 * D), F32), jax.ShapeDtypeStruct((4, 3 * D), F32)],
        compiler_params=_params(2),
    )(p, p, conv_w, dqkv)


def _conv_bwd_input(dc, conv_w, name):
    T, D3 = dc.shape
    D = D3 // 3
    tm = _pick(T, (256, 128, 64))
    n_t = T // tm

    def body(cur_ref, next_ref, w_ref, dp_ref):
        i = pl.program_id(0)
        cur = cur_ref[...]
        nxt = next_ref[...] * (i < n_t - 1).astype(F32)
        ext = jnp.concatenate([cur, nxt], axis=0)
        w = w_ref[...]
        acc = cur * w[3:4, :]
        for s in (1, 2, 3):
            acc = acc + pltpu.roll(ext, tm + 8 - s, 0)[0:tm] * w[3 - s:4 - s, :]
        dp_ref[...] = acc.astype(BF16)

    return pl.pallas_call(
        body, name=name, grid=(n_t, 3),
        in_specs=[pl.BlockSpec((tm, D), lambda i, j: (i, j)),
                  pl.BlockSpec((8, D), lambda i, j: (jnp.minimum((i + 1) * (tm // 8), T // 8 - 1), j)),
                  pl.BlockSpec((4, D), lambda i, j: (0, j))],
        out_specs=pl.BlockSpec((tm, D), lambda i, j: (i, j)),
        out_shape=jax.ShapeDtypeStruct((T, D3), BF16), compiler_params=_params(2),
    )(dc, dc, conv_w)


def _exchange(arrays, gather, name):
    n = len(arrays)

    def body(*refs):
        ins, outs = refs[:n], refs[n:2 * n]
        send_sems, recv_sems, local_sems = refs[2 * n:]
        x, y, c = lax.axis_index("x"), lax.axis_index("y"), lax.axis_index("c")
        me = 4 * x + 2 * y + c

        def peer(k):
            px = 1 - x if k & 4 else x
            py = 1 - y if k & 2 else y
            pc = 1 - c if k & 1 else c
            return (px, py, pc), 4 * px + 2 * py + pc

        def remote(a, k):
            dev, idx = peer(k)
            src = ins[a] if gather else ins[a].at[idx]
            return pltpu.make_async_remote_copy(
                src_ref=src, dst_ref=outs[a].at[me], send_sem=send_sems.at[a * 7 + k - 1],
                recv_sem=recv_sems.at[a * 7 + k - 1], device_id=dev, device_id_type=MESH_ID)

        def arrival(a, k):
            dev, idx = peer(k)
            src = ins[a] if gather else ins[a].at[idx]
            return pltpu.make_async_remote_copy(
                src_ref=src, dst_ref=outs[a].at[idx], send_sem=send_sems.at[a * 7 + k - 1],
                recv_sem=recv_sems.at[a * 7 + k - 1], device_id=dev, device_id_type=MESH_ID)

        local = [pltpu.make_async_copy(ins[a] if gather else ins[a].at[me], outs[a].at[me], local_sems.at[a])
                 for a in range(n)]
        sends = [remote(a, k) for k in range(1, 8) for a in range(n)]
        for cp in local + sends:
            cp.start()
        for k in range(1, 8):
            for a in range(n):
                arrival(a, k).wait_recv()
        for cp in sends:
            cp.wait_send()
        for cp in local:
            cp.wait()

    out_shape = [jax.ShapeDtypeStruct((N_DEV,) + a.shape if gather else a.shape, a.dtype) for a in arrays]
    any_spec = pl.BlockSpec(memory_space=pl.ANY)
    return pl.pallas_call(
        body, name=name, in_specs=[any_spec] * n, out_specs=[any_spec] * n, out_shape=out_shape,
        scratch_shapes=[pltpu.SemaphoreType.DMA((7 * n,)), pltpu.SemaphoreType.DMA((7 * n,)), pltpu.SemaphoreType.DMA((n,))],
        compiler_params=pltpu.CompilerParams(has_side_effects=True),
    )(*arrays)


def _reduce_adamw(recv, w, m, v, name):
    _, R, C = recv.shape
    tr = next((t for t in (256, 128, 64, 32, 16, 8) if R % t == 0), R)
    c1 = 1.0 - ADAM_B1 ** ADAM_STEP
    c2 = 1.0 - ADAM_B2 ** ADAM_STEP

    def body(r_ref, w_ref, m_ref, v_ref, g_ref, d_ref, nm_ref, nv_ref):
        g = r_ref[0]
        for s in range(1, N_DEV):
            g = g + r_ref[s]
        nm = ADAM_B1 * m_ref[...] + (1.0 - ADAM_B1) * g
        nv = ADAM_B2 * v_ref[...] + (1.0 - ADAM_B2) * (g * g)
        g_ref[...] = g
        nm_ref[...] = nm
        nv_ref[...] = nv
        d_ref[...] = -ADAM_LR * ((nm / c1) / (jnp.sqrt(nv / c2) + ADAM_EPS) + ADAM_WD * w_ref[...])

    spec = pl.BlockSpec((tr, C), lambda i: (i, 0))
    return pl.pallas_call(
        body, name=name, grid=(R // tr,),
        in_specs=[pl.BlockSpec((N_DEV, tr, C), lambda i: (0, i, 0)), spec, spec, spec], out_specs=[spec] * 4,
        out_shape=[jax.ShapeDtypeStruct((R, C), F32)] * 4, compiler_params=_params(1),
    )(recv, w, m, v)


def _forward_local(x, target, nw0, nw1, fw, wa_in, conv_w, gate_params, o_norm, wa_out, wb_in, wb_out):
    T, D = x.shape
    nD = D // LANES
    sv = {}
    sv["u0"] = _rmsnorm_fwd(x, nw0, "a_norm_fwd")
    sv["pa"] = _mm_nn(sv["u0"], wa_in, "a_in_proj")
    sv["qkv_a"] = _conv_fwd(sv["pa"], conv_w, D, "a_conv_fwd")
    sv["beta"], sv["gc"] = _gates_fwd(sv["pa"], gate_params, 4 * nD, "a_gates_fwd")
    sv["o_a"], sv["s_all"], sv["t_all"] = _chunk_fwd(sv["qkv_a"], sv["beta"], sv["gc"], D, "a_chunk_fwd")
    sv["y_a"] = _onorm_gate_fwd(sv["o_a"], sv["pa"], 3, o_norm, "a_onorm_fwd")
    sv["h1"] = _mm_nn(sv["y_a"], wa_out, "a_out_proj", add=x)
    sv["u1"] = _rmsnorm_fwd(sv["h1"], nw1, "b_norm_fwd")
    sv["qkv_b"] = _mm_nn(sv["u1"], wb_in[:, :3 * D], "b_in_proj_qkv", out_dtype=BF16)
    sv["gate_b"] = _mm_nn(sv["u1"], wb_in[:, 3 * D:], "b_in_proj_gate")
    sv["o_b"], sv["r_b"] = _sb_fwd(sv["qkv_b"], D, "b_attn_fwd")
    sv["y_b"] = _gate_mul_fwd(sv["o_b"], sv["gate_b"], "b_gate_fwd")
    sv["h2"] = _mm_nn(sv["y_b"], wb_out, "b_out_proj", add=sv["h1"])
    sv["dh2"], sv["loss"], sv["dfw"] = _final_loss(sv["h2"], fw, target, "final_loss")
    return sv


def _backward_local(sv, x, nw0, nw1, wa_in, conv_w, gate_params, o_norm, wa_out, wb_in, wb_out):
    T, D = x.shape
    nD = D // LANES
    g = {}
    dh2 = sv["dh2"]
    g["wb_out"] = _mm_tn(sv["y_b"], dh2, "b_out_proj_dw")
    dy_b = _mm_nt(dh2, wb_out, "b_out_proj_dx")
    do_b, dgate_b = _gate_mul_bwd(dy_b, sv["o_b"], sv["gate_b"], "b_gate_bwd")
    dq_b, dk_b, dv_b = _sb_bwd(sv["qkv_b"], do_b, sv["r_b"], D, "b_attn_bwd")
    dp_b = jnp.concatenate([dq_b, dk_b, dv_b, dgate_b], axis=1)
    g["wb_in"] = _mm_tn(sv["u1"], dp_b, "b_in_proj_dw")
    du1 = _mm_nt(dp_b, wb_in, "b_in_proj_dx")
    dh1, g["nw1"] = _rmsnorm_bwd(sv["h1"], nw1, du1, dh2, "b_norm_bwd")
    g["wa_out"] = _mm_tn(sv["y_a"], dh1, "a_out_proj_dw")
    dy_a = _mm_nt(dh1, wa_out, "a_out_proj_dx")
    do_a, dz_a, g["o_norm"] = _onorm_gate_bwd(dy_a, sv["o_a"], sv["pa"], 3, o_norm, "a_onorm_bwd")
    dqkv_a, dbeta, dg = _chunk_bwd(sv["qkv_a"], sv["beta"], sv["gc"], sv["s_all"], sv["t_all"], do_a, D, "a_chunk_bwd")
    dp_gates, g["gate_params"] = _gates_bwd(sv["pa"], gate_params, 4 * nD, dbeta, dg, "a_gates_bwd")
    dc, g["conv_w"] = _conv_bwd_act(sv["pa"], conv_w, dqkv_a, D, "a_conv_bwd_act")
    dp_qkv = _conv_bwd_input(dc, conv_w, "a_conv_bwd_input")
    dp_a = jnp.concatenate([dp_qkv, dz_a, dp_gates], axis=1)
    g["wa_in"] = _mm_tn(sv["u0"], dp_a, "a_in_proj_dw")
    du0 = _mm_nt(dp_a, wa_in, "a_in_proj_dx")
    g["x"], g["nw0"] = _rmsnorm_bwd(x, nw0, du0, dh1, "a_norm_bwd")
    g["fw"] = sv["dfw"]
    return g


def kernel(x, norm_w, a_w_in, a_conv_w, a_a_log, a_dt_bias, a_o_norm, a_w_out, b_w_in, b_w_out, final_norm_w, loss_target, m_norm_w, m_a_w_in, m_a_conv_w, m_a_a_log, m_a_dt_bias, m_a_o_norm, m_a_w_out, m_b_w_in, m_b_w_out, m_final_norm_w, v_norm_w, v_a_w_in, v_a_conv_w, v_a_a_log, v_a_dt_bias, v_a_o_norm, v_a_w_out, v_b_w_in, v_b_w_out, v_final_norm_w):
    D = x.shape[-1]
    H = D // HEAD_DIM
    shards = [a_w_in[0].astype(BF16), a_w_out[0].astype(BF16), b_w_in[0].astype(BF16), b_w_out[0].astype(BF16), a_conv_w[0]]
    ga_in, ga_out, gb_in, gb_out, g_conv = _exchange(shards, gather=True, name="weights_gather")
    wa = ga_in.transpose(1, 0, 2).reshape(D, -1)
    pad = lambda w: jnp.pad(w, ((0, 0), (0, LANES - w.shape[1])))
    wa_in = jnp.concatenate([wa[:, :4 * D], pad(wa[:, 4 * D:4 * D + H]), pad(wa[:, 4 * D + H:])], axis=1)
    wa_out = ga_out.reshape(D, D)
    wb_in = gb_in.transpose(1, 0, 2).reshape(D, 4 * D)
    wb_out = gb_out.reshape(D, D)
    conv_w = g_conv.transpose(1, 0, 2).reshape(4, 3 * D)
    gate_params = jnp.zeros((8, LANES), F32).at[0, :H].set(a_a_log[0]).at[1, :H].set(a_dt_bias[0])
    nw0, nw1, fw = norm_w[0:1], norm_w[1:2], final_norm_w[None]

    sv = _forward_local(x[0], loss_target[0], nw0, nw1, fw, wa_in, conv_w, gate_params, a_o_norm, wa_out, wb_in, wb_out)
    g = _backward_local(sv, x[0], nw0, nw1, wa_in, conv_w, gate_params, a_o_norm, wa_out, wb_in, wb_out)

    gwa = g["wa_in"]
    gwa = jnp.concatenate([gwa[:, :4 * D], gwa[:, 4 * D:4 * D + H], gwa[:, 4 * D + LANES:4 * D + LANES + H]], axis=1)
    row = lambda v: jnp.pad(v.reshape(1, -1), ((0, 0), (0, D - v.size)))
    small = jnp.concatenate([g["nw0"][0:1], g["nw1"][0:1], g["fw"][0:1], row(g["gate_params"][0, :H]),
                             row(g["gate_params"][1, :H]), row(g["o_norm"][0]), row(sv["loss"][0, 0:1]),
                             jnp.zeros((1, D), F32)], axis=0)
    contribs = [gwa.reshape(D, N_DEV, -1).transpose(1, 0, 2), g["wa_out"].reshape(N_DEV, D // N_DEV, D),
                g["wb_in"].reshape(D, N_DEV, -1).transpose(1, 0, 2), g["wb_out"].reshape(N_DEV, D // N_DEV, D),
                g["conv_w"].reshape(4, N_DEV, -1).transpose(1, 0, 2), jnp.broadcast_to(small[None], (N_DEV, 8, D))]
    ra_in, ra_out, rb_in, rb_out, r_conv, r_small = _exchange(contribs, gather=False, name="grads_exchange")

    outs = {}
    for nm, recv, w, m, v in (("a_w_in", ra_in, a_w_in, m_a_w_in, v_a_w_in), ("a_w_out", ra_out, a_w_out, m_a_w_out, v_a_w_out),
                              ("b_w_in", rb_in, b_w_in, m_b_w_in, v_b_w_in), ("b_w_out", rb_out, b_w_out, m_b_w_out, v_b_w_out),
                              ("a_conv_w", r_conv, a_conv_w, m_a_conv_w, v_a_conv_w)):
        outs[nm] = tuple(o[None] for o in _reduce_adamw(recv, w[0], m[0], v[0], "adamw_" + nm))

    def pack(nw, alog, dt, onorm, fnw):
        return jnp.concatenate([nw, fnw.reshape(1, D), row(alog), row(dt), row(onorm), jnp.zeros((2, D), F32)], axis=0)

    s_g, s_d, s_m, s_v = _reduce_adamw(
        r_small, pack(norm_w, a_a_log, a_dt_bias, a_o_norm, final_norm_w),
        pack(m_norm_w, m_a_a_log, m_a_dt_bias, m_a_o_norm, m_final_norm_w),
        pack(v_norm_w, v_a_a_log, v_a_dt_bias, v_a_o_norm, v_final_norm_w), "adamw_small")
    loss = s_g[6, 0]
    for i, s in enumerate((s_g, s_d, s_m, s_v)):
        outs.setdefault("norm_w", [None] * 4)[i] = s[0:2]
        outs.setdefault("final_norm_w", [None] * 4)[i] = s[2]
        outs.setdefault("a_a_log", [None] * 4)[i] = s[3:4, :H]
        outs.setdefault("a_dt_bias", [None] * 4)[i] = s[4:5, :H]
        outs.setdefault("a_o_norm", [None] * 4)[i] = s[5:6, :HEAD_DIM]
    names = ("norm_w", "a_w_in", "a_conv_w", "a_a_log", "a_dt_bias", "a_o_norm", "a_w_out", "b_w_in", "b_w_out", "final_norm_w")
    return (loss, g["x"][None]) + tuple(outs[n][i] for i in range(4) for n in names)
```

```python
import functools

import jax
import jax.numpy as jnp
from jax import lax
from jax.experimental import pallas as pl
from jax.experimental.pallas import tpu as pltpu

F32 = jnp.float32
BF16 = jnp.bfloat16
EPS = 1e-6
HEAD_DIM = 128
CHUNK = 64
ATTN_Q_BLOCKS = (512, 256)
ATTN_K_BLOCK = 128
LANES = 128
N_DEV = 8
VMEM_LIMIT_BYTES = 48 * 1024 * 1024
ADAM_LR, ADAM_B1, ADAM_B2, ADAM_EPS, ADAM_WD, ADAM_STEP = 0.001, 0.9, 0.999, 1e-08, 0.01, 10
MESH_ID = pl.DeviceIdType.MESH


def _pick(n, candidates):
    for c in candidates:
        if n % c == 0:
            return c
    raise ValueError(f"no tile for {n} in {candidates}")


def _params(n_grid_axes):
    return pltpu.CompilerParams(dimension_semantics=("arbitrary",) * n_grid_axes, vmem_limit_bytes=VMEM_LIMIT_BYTES)


def _dot(a, b):
    return jnp.dot(a.astype(BF16), b.astype(BF16), preferred_element_type=F32)


def _dot_nt(a, b):
    return lax.dot_general(a.astype(BF16), b.astype(BF16), (((1,), (1,)), ((), ())), preferred_element_type=F32)


def _dot_tn(a, b):
    return lax.dot_general(a.astype(BF16), b.astype(BF16), (((0,), (0,)), ((), ())), preferred_element_type=F32)


def _split2(x):
    hi = x.astype(BF16)
    lo = (x - hi.astype(F32)).astype(BF16)
    return hi, lo


def _split3(x):
    hi = x.astype(BF16)
    r = x - hi.astype(F32)
    mid = r.astype(BF16)
    lo = (r - mid.astype(F32)).astype(BF16)
    return hi, mid, lo


def _dot3(a, b):
    a_hi, a_lo = _split2(a)
    b_hi, b_lo = _split2(b)
    d = functools.partial(jnp.dot, preferred_element_type=F32)
    return d(a_hi, b_hi) + (d(a_hi, b_lo) + d(a_lo, b_hi))


def _silu(x):
    return x * jax.nn.sigmoid(x)


def _softplus(x):
    return jnp.maximum(x, 0.0) + jnp.log1p(jnp.exp(-jnp.abs(x)))


def _iota2(shape, axis):
    return lax.broadcasted_iota(jnp.int32, shape, axis)


def _rms_bwd_math(x, w, dy):
    r = lax.rsqrt(jnp.mean(x * x, axis=-1, keepdims=True) + EPS)
    xhat = x * r
    dxhat = dy * w
    dx = r * (dxhat - xhat * jnp.mean(dxhat * xhat, axis=-1, keepdims=True))
    dw = jnp.sum(dy * xhat, axis=0, keepdims=True)
    return dx, dw


def _rmsnorm_fwd(x, w, name):
    T, D = x.shape
    tm = _pick(T, (512, 256, 128))

    def body(x_ref, w_ref, o_ref):
        xf = x_ref[...]
        r = lax.rsqrt(jnp.mean(xf * xf, axis=-1, keepdims=True) + EPS)
        o_ref[...] = (xf * r * w_ref[...]).astype(BF16)

    return pl.pallas_call(
        body, name=name, grid=(T // tm,),
        in_specs=[pl.BlockSpec((tm, D), lambda i: (i, 0)), pl.BlockSpec((1, D), lambda i: (0, 0))],
        out_specs=pl.BlockSpec((tm, D), lambda i: (i, 0)),
        out_shape=jax.ShapeDtypeStruct((T, D), BF16), compiler_params=_params(1),
    )(x, w)


def _rmsnorm_bwd(x, w, du, dres, name):
    T, D = x.shape
    tm = _pick(T, (512, 256, 128))

    def body(x_ref, w_ref, du_ref, dres_ref, dx_ref, dw_ref):
        dx, dw = _rms_bwd_math(x_ref[...], w_ref[...], du_ref[...].astype(F32))
        dx_ref[...] = dres_ref[...] + dx

        @pl.when(pl.program_id(0) == 0)
        def _():
            dw_ref[...] = jnp.zeros_like(dw_ref)

        dw_ref[...] += jnp.broadcast_to(dw, dw_ref.shape)

    return pl.pallas_call(
        body, name=name, grid=(T // tm,),
        in_specs=[pl.BlockSpec((tm, D), lambda i: (i, 0)), pl.BlockSpec((1, D), lambda i: (0, 0)),
                  pl.BlockSpec((tm, D), lambda i: (i, 0)), pl.BlockSpec((tm, D), lambda i: (i, 0))],
        out_specs=[pl.BlockSpec((tm, D), lambda i: (i, 0)), pl.BlockSpec((8, D), lambda i: (0, 0))],
        out_shape=[jax.ShapeDtypeStruct((T, D), F32), jax.ShapeDtypeStruct((8, D), F32)],
        compiler_params=_params(1),
    )(x, w, du, dres)


def _mm_nn(a, b, name, add=None, out_dtype=F32):
    M, K = a.shape
    _, N = b.shape
    tm = _pick(M, (512, 256, 128))
    tn = _pick(N, (512, 256, 128))

    def body(*refs):
        a_ref, b_ref = refs[0], refs[1]
        o_ref = refs[-1]
        acc = _dot(a_ref[...], b_ref[...])
        if add is not None:
            acc = acc + refs[2][...]
        o_ref[...] = acc.astype(out_dtype)

    in_specs = [pl.BlockSpec((tm, K), lambda i, j: (i, 0)), pl.BlockSpec((K, tn), lambda i, j: (0, j))]
    args = [a, b]
    if add is not None:
        in_specs.append(pl.BlockSpec((tm, tn), lambda i, j: (i, j)))
        args.append(add)
    return pl.pallas_call(
        body, name=name, grid=(M // tm, N // tn), in_specs=in_specs,
        out_specs=pl.BlockSpec((tm, tn), lambda i, j: (i, j)),
        out_shape=jax.ShapeDtypeStruct((M, N), out_dtype), compiler_params=_params(2),
    )(*args)


def _mm_nt(a, b, name, out_dtype=F32):
    M, K = a.shape
    N, _ = b.shape
    tm = _pick(M, (512, 256, 128))
    tn = _pick(N, (512, 256, 128))

    def body(a_ref, b_ref, o_ref):
        o_ref[...] = _dot_nt(a_ref[...], b_ref[...]).astype(out_dtype)

    return pl.pallas_call(
        body, name=name, grid=(M // tm, N // tn),
        in_specs=[pl.BlockSpec((tm, K), lambda i, j: (i, 0)), pl.BlockSpec((tn, K), lambda i, j: (j, 0))],
        out_specs=pl.BlockSpec((tm, tn), lambda i, j: (i, j)),
        out_shape=jax.ShapeDtypeStruct((M, N), out_dtype), compiler_params=_params(2),
    )(a, b)


def _mm_tn(a, b, name):
    R, M = a.shape
    _, N = b.shape
    tm = _pick(M, (1024, 512, 256, 128))
    tn = _pick(N, (512, 256, 128))
    tr = _pick(R, (512, 256, 128))
    n_r = R // tr

    def body(a_ref, b_ref, o_ref, acc_ref):
        r = pl.program_id(2)

        @pl.when(r == 0)
        def _():
            acc_ref[...] = jnp.zeros_like(acc_ref)

        acc_ref[...] += _dot_tn(a_ref[...], b_ref[...])

        @pl.when(r == n_r - 1)
        def _():
            o_ref[...] = acc_ref[...]

    return pl.pallas_call(
        body, name=name, grid=(M // tm, N // tn, n_r),
        in_specs=[pl.BlockSpec((tr, tm), lambda i, j, r: (r, i)), pl.BlockSpec((tr, tn), lambda i, j, r: (r, j))],
        out_specs=pl.BlockSpec((tm, tn), lambda i, j, r: (i, j)),
        out_shape=jax.ShapeDtypeStruct((M, N), F32),
        scratch_shapes=[pltpu.VMEM((tm, tn), F32)], compiler_params=_params(3),
    )(a, b)


def _qkv_post(c, j, n_heads):
    s = _silu(c)
    parts = []
    for h in range(n_heads):
        sh = s[:, h * HEAD_DIM:(h + 1) * HEAD_DIM]
        parts.append(sh * lax.rsqrt(jnp.sum(sh * sh, axis=-1, keepdims=True) + EPS))
    n = jnp.concatenate(parts, axis=-1)
    is_q = (j == 0).astype(F32)
    is_v = (j == 2).astype(F32)
    n = n * (1.0 + is_q * (HEAD_DIM ** -0.5 - 1.0))
    return n * (1.0 - is_v) + s * is_v


def _conv_taps(cur, halo_prev):
    tm = cur.shape[0]
    ext = jnp.concatenate([halo_prev, cur], axis=0)
    taps = [pltpu.roll(ext, s, 0)[8:8 + tm] for s in (3, 2, 1)]
    return taps + [cur]


def _conv_fwd(p, conv_w, d_model, name):
    T = p.shape[0]
    D = d_model
    H = D // HEAD_DIM
    tm = _pick(T, (256, 128, 64))

    def body(cur_ref, prev_ref, w_ref, o_ref):
        i, j = pl.program_id(0), pl.program_id(1)
        prev = prev_ref[...] * (i > 0).astype(F32)
        taps = _conv_taps(cur_ref[...], prev)
        w = w_ref[...]
        c = sum(taps[k] * w[k:k + 1, :] for k in range(4))
        o_ref[...] = _qkv_post(c, j, H)

    return pl.pallas_call(
        body, name=name, grid=(T // tm, 3),
        in_specs=[pl.BlockSpec((tm, D), lambda i, j: (i, j)),
                  pl.BlockSpec((8, D), lambda i, j: (jnp.maximum(i * (tm // 8) - 1, 0), j)),
                  pl.BlockSpec((4, D), lambda i, j: (0, j))],
        out_specs=pl.BlockSpec((tm, D), lambda i, j: (i, j)),
        out_shape=jax.ShapeDtypeStruct((T, 3 * D), F32), compiler_params=_params(2),
    )(p, p, conv_w)


def _chunk_tri(tm, upper):
    r, c = _iota2((tm, tm), 0), _iota2((tm, tm), 1)
    same = (r // CHUNK) == (c // CHUNK)
    tri = (c >= r) if upper else (c <= r)
    return jnp.where(same & tri, 1.0, 0.0).astype(BF16)


def _dot_mask(mask_bf16, x):
    hi, mid, lo = _split3(x)
    d = functools.partial(jnp.dot, preferred_element_type=F32)
    return d(mask_bf16, hi) + (d(mask_bf16, mid) + d(mask_bf16, lo))


def _gates_math(pb, pa, a_log, dt_bias):
    beta = jax.nn.sigmoid(pb)
    g = -jnp.exp(a_log) * _softplus(pa + dt_bias)
    return beta, g


def _gates_fwd(p, gate_params, col0, name):
    T = p.shape[0]
    tm = _pick(T, (256, 128, 64))

    def body(pb_ref, pa_ref, gp_ref, beta_ref, gc_ref):
        gp = gp_ref[...]
        beta, g = _gates_math(pb_ref[...], pa_ref[...], gp[0:1, :], gp[1:2, :])
        beta_ref[...] = beta
        gc_ref[...] = _dot_mask(_chunk_tri(tm, upper=False), g)

    return pl.pallas_call(
        body, name=name, grid=(T // tm,),
        in_specs=[pl.BlockSpec((tm, LANES), lambda i: (i, col0)), pl.BlockSpec((tm, LANES), lambda i: (i, col0 + 1)),
                  pl.BlockSpec((8, LANES), lambda i: (0, 0))],
        out_specs=[pl.BlockSpec((tm, LANES), lambda i: (i, 0))] * 2,
        out_shape=[jax.ShapeDtypeStruct((T, LANES), F32)] * 2, compiler_params=_params(1),
    )(p, p, gate_params)


def _col_to_row(col):
    C = col.shape[0]
    eye = _iota2((C, C), 0) == _iota2((C, C), 1)
    return jnp.sum(jnp.where(eye, col, 0.0), axis=0, keepdims=True)


def _unit_lower_inverse(low):
    C = low.shape[0]
    eye = (_iota2((C, C), 0) == _iota2((C, C), 1)).astype(F32)
    t = eye - low
    p = _dot3(low, low)
    n = 2
    while True:
        t = t + _dot3(t, p)
        n *= 2
        if n >= C:
            return t
        p = _dot3(p, p)


def _chunk_head_fwd(q, k, v, gc, beta, s_in):
    C = q.shape[0]
    r, c = _iota2((C, C), 0), _iota2((C, C), 1)
    causal, strict = r >= c, r > c
    decay = jnp.where(causal, jnp.exp(jnp.where(causal, gc - _col_to_row(gc), 0.0)), 0.0)
    kb, vb = k * beta, v * beta
    low = jnp.where(strict, _dot_nt(kb, k) * decay, 0.0)
    t_inv = _unit_lower_inverse(low)
    eg = jnp.exp(gc)
    u = _dot(t_inv, vb)
    w = _dot(t_inv, kb * eg)
    attn = _dot_nt(q, k) * decay
    g_last = gc[C - 1:C, :]
    k_tail = k * jnp.exp(g_last - gc)
    v_new = u - _dot(w, s_in)
    o = _dot(q * eg, s_in) + _dot(attn, v_new)
    s_out = s_in * jnp.exp(g_last) + _dot_tn(k_tail, v_new)
    return o, s_out, t_inv


def _chunk_fwd(qkv, beta, gc, d_model, name):
    T = qkv.shape[0]
    D = d_model
    H = D // HEAD_DIM
    N = T // CHUNK

    def body(q_ref, k_ref, v_ref, beta_ref, gc_ref, o_ref, s_all_ref, t_all_ref, s_ref):
        @pl.when(pl.program_id(0) == 0)
        def _():
            s_ref[...] = jnp.zeros_like(s_ref)

        for h in range(H):
            hs = slice(h * HEAD_DIM, (h + 1) * HEAD_DIM)
            s_in = s_ref[h]
            s_all_ref[0, h] = s_in
            o, s_out, t_inv = _chunk_head_fwd(q_ref[:, hs], k_ref[:, hs], v_ref[:, hs],
                                              gc_ref[:, h:h + 1], beta_ref[:, h:h + 1], s_in)
            o_ref[:, hs] = o
            s_ref[h] = s_out
            t_all_ref[0, h] = t_inv

    return pl.pallas_call(
        body, name=name, grid=(N,),
        in_specs=[pl.BlockSpec((CHUNK, D), lambda n: (n, 0)), pl.BlockSpec((CHUNK, D), lambda n: (n, 1)),
                  pl.BlockSpec((CHUNK, D), lambda n: (n, 2)),
                  pl.BlockSpec((CHUNK, LANES), lambda n: (n, 0)), pl.BlockSpec((CHUNK, LANES), lambda n: (n, 0))],
        out_specs=[pl.BlockSpec((CHUNK, D), lambda n: (n, 0)),
                   pl.BlockSpec((1, H, HEAD_DIM, HEAD_DIM), lambda n: (n, 0, 0, 0)),
                   pl.BlockSpec((1, H, CHUNK, CHUNK), lambda n: (n, 0, 0, 0))],
        out_shape=[jax.ShapeDtypeStruct((T, D), F32), jax.ShapeDtypeStruct((N, H, HEAD_DIM, HEAD_DIM), F32),
                   jax.ShapeDtypeStruct((N, H, CHUNK, CHUNK), F32)],
        scratch_shapes=[pltpu.VMEM((H, HEAD_DIM, HEAD_DIM), F32)], compiler_params=_params(1),
    )(qkv, qkv, qkv, beta, gc)


def _onorm_gate_math(o, z, w, n_heads):
    parts = []
    for h in range(n_heads):
        hs = slice(h * HEAD_DIM, (h + 1) * HEAD_DIM)
        oh = o[:, hs]
        y = oh * lax.rsqrt(jnp.mean(oh * oh, axis=-1, keepdims=True) + EPS) * w
        parts.append(y * _silu(z[:, hs]))
    return jnp.concatenate(parts, axis=-1)


def _onorm_gate_fwd(o, p, z_col, o_norm, name):
    T, D = o.shape
    H = D // HEAD_DIM
    tm = _pick(T, (256, 128, 64))

    def body(o_ref, z_ref, w_ref, y_ref):
        y_ref[...] = _onorm_gate_math(o_ref[...], z_ref[...], w_ref[...], H).astype(BF16)

    return pl.pallas_call(
        body, name=name, grid=(T // tm,),
        in_specs=[pl.BlockSpec((tm, D), lambda i: (i, 0)), pl.BlockSpec((tm, D), lambda i: (i, z_col)),
                  pl.BlockSpec((1, HEAD_DIM), lambda i: (0, 0))],
        out_specs=pl.BlockSpec((tm, D), lambda i: (i, 0)),
        out_shape=jax.ShapeDtypeStruct((T, D), BF16), compiler_params=_params(1),
    )(o, p, o_norm)


def _softplus_sigmoid(z):
    t = jnp.exp(-jnp.abs(z))
    u = 1.0 + t
    r = 1.0 / u
    return jnp.maximum(z, 0.0) + jnp.log(u), jnp.where(z >= 0.0, r, t * r)


def _attn_softplus(z):
    return jnp.maximum(z, 0.0) + jnp.log(1.0 + jnp.exp(-jnp.abs(z)))


def _diag_mask(qb, kb, d):
    return _iota2((qb, kb), 0) > _iota2((qb, kb), 1) + d * kb


def _incl_lower(n):
    return jnp.where(_iota2((n, n), 0) >= _iota2((n, n), 1), 1.0, 0.0).astype(BF16)


def _incl_upper(n):
    return jnp.where(_iota2((n, n), 0) <= _iota2((n, n), 1), 1.0, 0.0).astype(BF16)


def _dot_cum(x, tri_bf16):
    hi, lo = _split2(x)
    d = functools.partial(jnp.dot, preferred_element_type=F32)
    return d(hi, tri_bf16) + d(lo, tri_bf16)


def _sb_fwd(qkv, d_model, name):
    T = qkv.shape[0]
    D = d_model
    H = D // HEAD_DIM
    QB = _pick(T, ATTN_Q_BLOCKS)
    KB = ATTN_K_BLOCK
    nd = QB // KB
    nq = T // QB
    scale = HEAD_DIM ** -0.5

    def body(q_ref, k_ref, v_ref, o_ref, r_ref):
        i = pl.program_id(1)
        q = q_ref[...]
        tri = _incl_lower(KB)

        def block(start, mask, acc, carry):
            start = pl.multiple_of(start, KB)
            kj = k_ref[pl.ds(start, KB), :]
            vj = v_ref[pl.ds(start, KB), :]
            z = _dot_nt(q, kj) * scale
            sp = _attn_softplus(z)
            if mask is not None:
                sp = jnp.where(mask, sp, 0.0)
            cum = _dot_cum(sp, tri)
            a = jnp.exp(z - cum - carry)
            if mask is not None:
                a = jnp.where(mask, a, 0.0)
            return acc + _dot(a, vj), carry + cum[:, 0:1]

        state = (jnp.zeros((QB, HEAD_DIM), F32), jnp.zeros((QB, 1), F32))
        for d in reversed(range(nd)):
            state = block(i * QB + d * KB, _diag_mask(QB, KB, d), *state)

        def two_blocks(it, st):
            j = i * nd - 1 - 2 * it
            return block((j - 1) * KB, None, *block(j * KB, None, *st))

        acc, carry = lax.fori_loop(0, (i * nd) // 2, two_blocks, state)
        o_ref[...] = acc
        r_ref[0] = jnp.broadcast_to(carry, (QB, LANES))

    return pl.pallas_call(
        body, name=name, grid=(H, nq),
        in_specs=[pl.BlockSpec((QB, HEAD_DIM), lambda h, i: (i, h)),
                  pl.BlockSpec((T, HEAD_DIM), lambda h, i: (0, H + h)),
                  pl.BlockSpec((T, HEAD_DIM), lambda h, i: (0, 2 * H + h))],
        out_specs=[pl.BlockSpec((QB, HEAD_DIM), lambda h, i: (i, h)),
                   pl.BlockSpec((1, QB, LANES), lambda h, i: (h, i, 0))],
        out_shape=[jax.ShapeDtypeStruct((T, D), F32), jax.ShapeDtypeStruct((H, T, LANES), F32)],
        compiler_params=_params(2),
    )(qkv, qkv, qkv)


def _gate_mul_fwd(o, gate, name):
    T, D = o.shape
    tm = _pick(T, (512, 256, 128))

    def body(o_ref, g_ref, y_ref):
        y_ref[...] = (o_ref[...] * _silu(g_ref[...])).astype(BF16)

    spec = pl.BlockSpec((tm, D), lambda i: (i, 0))
    return pl.pallas_call(body, name=name, grid=(T // tm,), in_specs=[spec, spec], out_specs=spec,
                          out_shape=jax.ShapeDtypeStruct((T, D), BF16), compiler_params=_params(1))(o, gate)


def _final_loss(h, w, target, name):
    T, D = h.shape
    tm = _pick(T, (512, 256, 128))

    def body(h_ref, w_ref, t_ref, dh_ref, loss_ref, dw_ref):
        x, w = h_ref[...], w_ref[...]
        r = lax.rsqrt(jnp.mean(x * x, axis=-1, keepdims=True) + EPS)
        err = x * r * w - t_ref[...]
        part = 0.5 * jnp.sum(jnp.mean(err * err, axis=-1, keepdims=True), axis=0, keepdims=True)
        dx, dw = _rms_bwd_math(x, w, err * (1.0 / D))
        dh_ref[...] = dx

        @pl.when(pl.program_id(0) == 0)
        def _():
            loss_ref[...] = jnp.zeros_like(loss_ref)
            dw_ref[...] = jnp.zeros_like(dw_ref)

        loss_ref[...] += jnp.broadcast_to(part, loss_ref.shape)
        dw_ref[...] += jnp.broadcast_to(dw, dw_ref.shape)

    return pl.pallas_call(
        body, name=name, grid=(T // tm,),
        in_specs=[pl.BlockSpec((tm, D), lambda i: (i, 0)), pl.BlockSpec((1, D), lambda i: (0, 0)),
                  pl.BlockSpec((tm, D), lambda i: (i, 0))],
        out_specs=[pl.BlockSpec((tm, D), lambda i: (i, 0)), pl.BlockSpec((8, LANES), lambda i: (0, 0)),
                   pl.BlockSpec((8, D), lambda i: (0, 0))],
        out_shape=[jax.ShapeDtypeStruct((T, D), F32), jax.ShapeDtypeStruct((8, LANES), F32),
                   jax.ShapeDtypeStruct((8, D), F32)],
        compiler_params=_params(1),
    )(h, w, target)


def _gate_mul_bwd(dy, o, gate, name):
    T, D = o.shape
    tm = _pick(T, (512, 256, 128))

    def body(dy_ref, o_ref, g_ref, do_ref, dg_ref):
        dy, g = dy_ref[...], g_ref[...]
        s = jax.nn.sigmoid(g)
        do_ref[...] = dy * (g * s)
        dg_ref[...] = (dy * o_ref[...] * (s + g * s * (1.0 - s))).astype(BF16)

    spec = pl.BlockSpec((tm, D), lambda i: (i, 0))
    return pl.pallas_call(body, name=name, grid=(T // tm,), in_specs=[spec] * 3, out_specs=[spec] * 2,
                          out_shape=[jax.ShapeDtypeStruct((T, D), F32), jax.ShapeDtypeStruct((T, D), BF16)],
                          compiler_params=_params(1))(dy, o, gate)


def _sb_bwd(qkv, do, r_tot, d_model, name):
    T = qkv.shape[0]
    D = d_model
    H = D // HEAD_DIM
    QB = _pick(T, ATTN_Q_BLOCKS)
    KB = ATTN_K_BLOCK
    nd = QB // KB
    nq = T // QB
    scale = HEAD_DIM ** -0.5

    def body(q_ref, k_ref, v_ref, do_ref, r_ref, dq_ref, dk_ref, dv_ref, dk_acc, dv_acc):
        i = pl.program_id(1)

        @pl.when(i == 0)
        def _():
            dk_acc[...] = jnp.zeros_like(dk_acc)
            dv_acc[...] = jnp.zeros_like(dv_acc)

        q = q_ref[...]
        do_blk = do_ref[...].astype(BF16)
        row_total = r_ref[0][:, 0:1]
        tri_rev = _incl_lower(KB)
        tri_fwd = _incl_upper(KB)

        def block(start, mask, dq, left_sp, left_p):
            start = pl.multiple_of(start, KB)
            kj = k_ref[pl.ds(start, KB), :]
            vj = v_ref[pl.ds(start, KB), :]
            z = _dot_nt(q, kj) * scale
            sp, sig = _softplus_sigmoid(z)
            if mask is not None:
                sp = jnp.where(mask, sp, 0.0)
            cum = _dot_cum(sp, tri_rev)
            block_sp = cum[:, 0:1]
            a = jnp.exp(z - cum - (row_total - left_sp - block_sp))
            if mask is not None:
                a = jnp.where(mask, a, 0.0)
            p = _dot_nt(do_blk, vj) * a
            cum_p = _dot_cum(p, tri_fwd) + left_p
            dz = p - sig * cum_p
            if mask is not None:
                dz = jnp.where(mask, dz, 0.0)
            dz = (dz * scale).astype(BF16)
            dk_acc[pl.ds(start, KB), :] += _dot_tn(dz, q)
            dv_acc[pl.ds(start, KB), :] += _dot_tn(a, do_blk)
            return dq + _dot(dz, kj), left_sp + block_sp, cum_p[:, KB - 1:KB]

        def two_blocks(it, st):
            return block((2 * it + 1) * KB, None, *block(2 * it * KB, None, *st))

        zero_col = jnp.zeros((QB, 1), F32)
        state = lax.fori_loop(0, (i * nd) // 2, two_blocks, (jnp.zeros((QB, HEAD_DIM), F32), zero_col, zero_col))
        for d in range(nd):
            state = block(i * QB + d * KB, _diag_mask(QB, KB, d), *state)
        dq_ref[...] = state[0].astype(BF16)

        @pl.when(i == nq - 1)
        def _():
            dk_ref[...] = dk_acc[...].astype(BF16)
            dv_ref[...] = dv_acc[...].astype(BF16)

    return pl.pallas_call(
        body, name=name, grid=(H, nq),
        in_specs=[pl.BlockSpec((QB, HEAD_DIM), lambda h, i: (i, h)),
                  pl.BlockSpec((T, HEAD_DIM), lambda h, i: (0, H + h)),
                  pl.BlockSpec((T, HEAD_DIM), lambda h, i: (0, 2 * H + h)),
                  pl.BlockSpec((QB, HEAD_DIM), lambda h, i: (i, h)),
                  pl.BlockSpec((1, QB, LANES), lambda h, i: (h, i, 0))],
        out_specs=[pl.BlockSpec((QB, HEAD_DIM), lambda h, i: (i, h)),
                   pl.BlockSpec((T, HEAD_DIM), lambda h, i: (0, h)),
                   pl.BlockSpec((T, HEAD_DIM), lambda h, i: (0, h))],
        out_shape=[jax.ShapeDtypeStruct((T, D), BF16)] * 3,
        scratch_shapes=[pltpu.VMEM((T, HEAD_DIM), F32), pltpu.VMEM((T, HEAD_DIM), F32)],
        compiler_params=_params(2),
    )(qkv, qkv, qkv, do, r_tot)


def _onorm_gate_bwd(dy, o, p, z_col, o_norm, name):
    T, D = o.shape
    H = D // HEAD_DIM
    tm = _pick(T, (256, 128, 64))

    def body(dy_ref, o_ref, z_ref, w_ref, do_ref, dz_ref, dw_ref):
        _, vjp = jax.vjp(functools.partial(_onorm_gate_math, n_heads=H), o_ref[...], z_ref[...], w_ref[...])
        do, dz, dw = vjp(dy_ref[...])
        do_ref[...] = do
        dz_ref[...] = dz.astype(BF16)

        @pl.when(pl.program_id(0) == 0)
        def _():
            dw_ref[...] = jnp.zeros_like(dw_ref)

        dw_ref[...] += jnp.broadcast_to(dw, dw_ref.shape)

    return pl.pallas_call(
        body, name=name, grid=(T // tm,),
        in_specs=[pl.BlockSpec((tm, D), lambda i: (i, 0)), pl.BlockSpec((tm, D), lambda i: (i, 0)),
                  pl.BlockSpec((tm, D), lambda i: (i, z_col)), pl.BlockSpec((1, HEAD_DIM), lambda i: (0, 0))],
        out_specs=[pl.BlockSpec((tm, D), lambda i: (i, 0)), pl.BlockSpec((tm, D), lambda i: (i, 0)),
                   pl.BlockSpec((8, HEAD_DIM), lambda i: (0, 0))],
        out_shape=[jax.ShapeDtypeStruct((T, D), F32), jax.ShapeDtypeStruct((T, D), BF16),
                   jax.ShapeDtypeStruct((8, HEAD_DIM), F32)],
        compiler_params=_params(1),
    )(dy, o, p, o_norm)


def _row_to_col(row):
    C = row.shape[1]
    eye = _iota2((C, C), 0) == _iota2((C, C), 1)
    return jnp.sum(jnp.where(eye, row, 0.0), axis=1, keepdims=True)


def _lane_sum(x):
    return jnp.sum(x, axis=-1, keepdims=True)


def _chunk_head_bwd(q, k, v, gc, beta, s_in, t_inv, do, ds_out):
    C = q.shape[0]
    r, c = _iota2((C, C), 0), _iota2((C, C), 1)
    causal, strict = r >= c, r > c
    decay = jnp.where(causal, jnp.exp(jnp.where(causal, gc - _col_to_row(gc), 0.0)), 0.0)
    kb, vb = k * beta, v * beta
    kk = _dot_nt(kb, k)
    low = jnp.where(strict, kk * decay, 0.0)
    eg = jnp.exp(gc)
    kbg = kb * eg
    u = _dot(t_inv, vb)
    w = _dot(t_inv, kbg)
    qk = _dot_nt(q, k)
    attn = qk * decay
    g_last = gc[C - 1:C, :]
    e_tail = jnp.exp(g_last - gc)
    k_tail = k * e_tail
    gl = jnp.exp(g_last)
    qg = q * eg
    v_new = u - _dot(w, s_in)

    d_vnew = _dot_tn(attn, do) + _dot(k_tail, ds_out)
    d_ktail = _dot_nt(v_new, ds_out)
    d_gl = jnp.sum(_lane_sum(s_in * ds_out), axis=0, keepdims=True)
    d_qg = _dot_nt(do, s_in)
    d_attn = jnp.where(causal, _dot_nt(do, v_new), 0.0)
    d_w = -_dot_nt(d_vnew, s_in)
    ds_in = ds_out * gl + _dot_tn(qg, do) - _dot_tn(w, d_vnew)
    d_vb = _dot_tn(t_inv, d_vnew)
    d_kbg = _dot_tn(t_inv, d_w)
    d_tinv = _dot_nt(d_vnew, vb) + _dot_nt(d_w, kbg)
    d_low = jnp.where(strict, -_dot_nt(_dot_tn(t_inv, d_tinv), t_inv), 0.0)
    d_kk = d_low * decay
    d_qk = d_attn * decay
    d_kb = _dot(d_kk, k) + d_kbg * eg
    dq = _dot(d_qk, k) + d_qg * eg
    dk = _dot_tn(d_kk, kb) + _dot_tn(d_qk, q) + d_ktail * e_tail + d_kb * beta
    dv = d_vb * beta
    dbeta = _lane_sum(d_kb * k) + _lane_sum(d_vb * v)
    m = d_low * low + d_attn * attn
    tail_term = _lane_sum(d_ktail * k_tail)
    d_g_last = d_gl * gl + jnp.sum(tail_term, axis=0, keepdims=True)
    dgc = (_lane_sum(m) - _row_to_col(jnp.sum(m, axis=0, keepdims=True))
           + _lane_sum(d_qg * qg) + _lane_sum(d_kbg * kbg) - tail_term)
    dgc = dgc + jnp.where(_iota2((C, 1), 0) == C - 1, d_g_last, 0.0)
    return dq, dk, dv, dgc, dbeta, ds_in


def _chunk_bwd(qkv, beta, gc, s_all, t_all, do, d_model, name):
    T = qkv.shape[0]
    D = d_model
    H = D // HEAD_DIM
    N = T // CHUNK

    def body(q_ref, k_ref, v_ref, beta_ref, gc_ref, s_ref, t_ref, do_ref, dqkv_ref, dbeta_ref, dg_ref, ds_ref):
        @pl.when(pl.program_id(0) == 0)
        def _():
            ds_ref[...] = jnp.zeros_like(ds_ref)

        lane = _iota2((CHUNK, LANES), 1)
        dgc_all = jnp.zeros((CHUNK, LANES), F32)
        dbeta_all = jnp.zeros((CHUNK, LANES), F32)
        for h in range(H):
            hs = slice(h * HEAD_DIM, (h + 1) * HEAD_DIM)
            dq, dk, dv, dgc, dbeta, ds_in = _chunk_head_bwd(
                q_ref[:, hs], k_ref[:, hs], v_ref[:, hs], gc_ref[:, h:h + 1], beta_ref[:, h:h + 1],
                s_ref[0, h], t_ref[0, h], do_ref[:, hs], ds_ref[h])
            ds_ref[h] = ds_in
            dqkv_ref[:, h * HEAD_DIM:(h + 1) * HEAD_DIM] = dq
            dqkv_ref[:, D + h * HEAD_DIM:D + (h + 1) * HEAD_DIM] = dk
            dqkv_ref[:, 2 * D + h * HEAD_DIM:2 * D + (h + 1) * HEAD_DIM] = dv
            dgc_all = jnp.where(lane == h, dgc, dgc_all)
            dbeta_all = jnp.where(lane == h, dbeta, dbeta_all)
        dbeta_ref[...] = dbeta_all
        dg_ref[...] = _dot_mask(_chunk_tri(CHUNK, upper=True), dgc_all)

    rev = lambda n: N - 1 - n
    return pl.pallas_call(
        body, name=name, grid=(N,),
        in_specs=[pl.BlockSpec((CHUNK, D), lambda n: (rev(n), 0)), pl.BlockSpec((CHUNK, D), lambda n: (rev(n), 1)),
                  pl.BlockSpec((CHUNK, D), lambda n: (rev(n), 2)),
                  pl.BlockSpec((CHUNK, LANES), lambda n: (rev(n), 0)), pl.BlockSpec((CHUNK, LANES), lambda n: (rev(n), 0)),
                  pl.BlockSpec((1, H, HEAD_DIM, HEAD_DIM), lambda n: (rev(n), 0, 0, 0)),
                  pl.BlockSpec((1, H, CHUNK, CHUNK), lambda n: (rev(n), 0, 0, 0)),
                  pl.BlockSpec((CHUNK, D), lambda n: (rev(n), 0))],
        out_specs=[pl.BlockSpec((CHUNK, 3 * D), lambda n: (rev(n), 0)),
                   pl.BlockSpec((CHUNK, LANES), lambda n: (rev(n), 0)), pl.BlockSpec((CHUNK, LANES), lambda n: (rev(n), 0))],
        out_shape=[jax.ShapeDtypeStruct((T, 3 * D), F32), jax.ShapeDtypeStruct((T, LANES), F32),
                   jax.ShapeDtypeStruct((T, LANES), F32)],
        scratch_shapes=[pltpu.VMEM((H, HEAD_DIM, HEAD_DIM), F32)], compiler_params=_params(1),
    )(qkv, qkv, qkv, beta, gc, s_all, t_all, do)


def _gates_bwd(p, gate_params, col0, dbeta, dg, name):
    T = p.shape[0]
    tm = _pick(T, (256, 128, 64))

    def body(pb_ref, pa_ref, gp_ref, dbeta_ref, dg_ref, dp_ref, dgp_ref):
        gp = gp_ref[...]
        _, vjp = jax.vjp(_gates_math, pb_ref[...], pa_ref[...], gp[0:1, :], gp[1:2, :])
        dpb, dpa, d_alog, d_dt = vjp((dbeta_ref[...], dg_ref[...]))
        dp_ref[:, 0:LANES] = dpb.astype(BF16)
        dp_ref[:, LANES:2 * LANES] = dpa.astype(BF16)

        @pl.when(pl.program_id(0) == 0)
        def _():
            dgp_ref[...] = jnp.zeros_like(dgp_ref)

        dgp_ref[0:1, :] += d_alog
        dgp_ref[1:2, :] += d_dt

    return pl.pallas_call(
        body, name=name, grid=(T // tm,),
        in_specs=[pl.BlockSpec((tm, LANES), lambda i: (i, col0)), pl.BlockSpec((tm, LANES), lambda i: (i, col0 + 1)),
                  pl.BlockSpec((8, LANES), lambda i: (0, 0)),
                  pl.BlockSpec((tm, LANES), lambda i: (i, 0)), pl.BlockSpec((tm, LANES), lambda i: (i, 0))],
        out_specs=[pl.BlockSpec((tm, 2 * LANES), lambda i: (i, 0)), pl.BlockSpec((8, LANES), lambda i: (0, 0))],
        out_shape=[jax.ShapeDtypeStruct((T, 2 * LANES), BF16), jax.ShapeDtypeStruct((8, LANES), F32)],
        compiler_params=_params(1),
    )(p, p, gate_params, dbeta, dg)


def _conv_bwd_act(p, conv_w, dqkv, d_model, name):
    T = p.shape[0]
    D = d_model
    H = D // HEAD_DIM
    tm = _pick(T, (256, 128, 64))

    def body(cur_ref, prev_ref, w_ref, dout_ref, dc_ref, dw_ref):
        j, i = pl.program_id(0), pl.program_id(1)
        prev = prev_ref[...] * (i > 0).astype(F32)
        taps = _conv_taps(cur_ref[...], prev)
        w = w_ref[...]
        c = sum(taps[k] * w[k:k + 1, :] for k in range(4))
        _, vjp = jax.vjp(lambda cc: _qkv_post(cc, j, H), c)
        (dc,) = vjp(dout_ref[...])
        dc_ref[...] = dc

        @pl.when(i == 0)
        def _():
            dw_ref[...] = jnp.zeros_like(dw_ref)

        for k in range(4):
            dw_ref[k:k + 1, :] += jnp.sum(dc * taps[k], axis=0, keepdims=True)

    return pl.pallas_call(
        body, name=name, grid=(3, T // tm),
        in_specs=[pl.BlockSpec((tm, D), lambda j, i: (i, j)),
                  pl.BlockSpec((8, D), lambda j, i: (jnp.maximum(i * (tm // 8) - 1, 0), j)),
                  pl.BlockSpec((4, D), lambda j, i: (0, j)),
                  pl.BlockSpec((tm, D), lambda j, i: (i, j))],
        out_specs=[pl.BlockSpec((tm, D), lambda j, i: (i, j)), pl.BlockSpec((4, D), lambda j, i: (0, j))],
        out_shape=[jax.ShapeDtypeStruct((T, 3 * D), F32), jax.ShapeDtypeStruct((4, 3 * D), F32)],
        compiler_params=_params(2),
    )(p, p, conv_w, dqkv)


def _conv_bwd_input(dc, conv_w, name):
    T, D3 = dc.shape
    D = D3 // 3
    tm = _pick(T, (256, 128, 64))
    n_t = T // tm

    def body(cur_ref, next_ref, w_ref, dp_ref):
        i = pl.program_id(0)
        cur = cur_ref[...]
        nxt = next_ref[...] * (i < n_t - 1).astype(F32)
        ext = jnp.concatenate([cur, nxt], axis=0)
        w = w_ref[...]
        acc = cur * w[3:4, :]
        for s in (1, 2, 3):
            acc = acc + pltpu.roll(ext, tm + 8 - s, 0)[0:tm] * w[3 - s:4 - s, :]
        dp_ref[...] = acc.astype(BF16)

    return pl.pallas_call(
        body, name=name, grid=(n_t, 3),
        in_specs=[pl.BlockSpec((tm, D), lambda i, j: (i, j)),
                  pl.BlockSpec((8, D), lambda i, j: (jnp.minimum((i + 1) * (tm // 8), T // 8 - 1), j)),
                  pl.BlockSpec((4, D), lambda i, j: (0, j))],
        out_specs=pl.BlockSpec((tm, D), lambda i, j: (i, j)),
        out_shape=jax.ShapeDtypeStruct((T, D3), BF16), compiler_params=_params(2),
    )(dc, dc, conv_w)


def _exchange(arrays, gather, name):
    n = len(arrays)

    def body(*refs):
        ins, outs = refs[:n], refs[n:2 * n]
        send_sems, recv_sems, local_sems = refs[2 * n:]
        x, y, c = lax.axis_index("x"), lax.axis_index("y"), lax.axis_index("c")
        me = 4 * x + 2 * y + c

        def peer(k):
            px = 1 - x if k & 4 else x
            py = 1 - y if k & 2 else y
            pc = 1 - c if k & 1 else c
            return (px, py, pc), 4 * px + 2 * py + pc

        def remote(a, k):
            dev, idx = peer(k)
            src = ins[a] if gather else ins[a].at[idx]
            return pltpu.make_async_remote_copy(
                src_ref=src, dst_ref=outs[a].at[me], send_sem=send_sems.at[a * 7 + k - 1],
                recv_sem=recv_sems.at[a * 7 + k - 1], device_id=dev, device_id_type=MESH_ID)

        def arrival(a, k):
            dev, idx = peer(k)
            src = ins[a] if gather else ins[a].at[idx]
            return pltpu.make_async_remote_copy(
                src_ref=src, dst_ref=outs[a].at[idx], send_sem=send_sems.at[a * 7 + k - 1],
                recv_sem=recv_sems.at[a * 7 + k - 1], device_id=dev, device_id_type=MESH_ID)

        local = [pltpu.make_async_copy(ins[a] if gather else ins[a].at[me], outs[a].at[me], local_sems.at[a])
                 for a in range(n)]
        sends = [remote(a, k) for k in range(1, 8) for a in range(n)]
        for cp in local + sends:
            cp.start()
        for k in range(1, 8):
            for a in range(n):
                arrival(a, k).wait_recv()
        for cp in sends:
            cp.wait_send()
        for cp in local:
            cp.wait()

    out_shape = [jax.ShapeDtypeStruct((N_DEV,) + a.shape if gather else a.shape, a.dtype) for a in arrays]
    any_spec = pl.BlockSpec(memory_space=pl.ANY)
    return pl.pallas_call(
        body, name=name, in_specs=[any_spec] * n, out_specs=[any_spec] * n, out_shape=out_shape,
        scratch_shapes=[pltpu.SemaphoreType.DMA((7 * n,)), pltpu.SemaphoreType.DMA((7 * n,)), pltpu.SemaphoreType.DMA((n,))],
        compiler_params=pltpu.CompilerParams(has_side_effects=True),
    )(*arrays)


def _reduce_adamw(recv, w, m, v, name):
    _, R, C = recv.shape
    tr = next((t for t in (256, 128, 64, 32, 16, 8) if R % t == 0), R)
    c1 = 1.0 - ADAM_B1 ** ADAM_STEP
    c2 = 1.0 - ADAM_B2 ** ADAM_STEP

    def body(r_ref, w_ref, m_ref, v_ref, g_ref, d_ref, nm_ref, nv_ref):
        g = r_ref[0]
        for s in range(1, N_DEV):
            g = g + r_ref[s]
        nm = ADAM_B1 * m_ref[...] + (1.0 - ADAM_B1) * g
        nv = ADAM_B2 * v_ref[...] + (1.0 - ADAM_B2) * (g * g)
        g_ref[...] = g
        nm_ref[...] = nm
        nv_ref[...] = nv
        d_ref[...] = -ADAM_LR * ((nm / c1) / (jnp.sqrt(nv / c2) + ADAM_EPS) + ADAM_WD * w_ref[...])

    spec = pl.BlockSpec((tr, C), lambda i: (i, 0))
    return pl.pallas_call(
        body, name=name, grid=(R // tr,),
        in_specs=[pl.BlockSpec((N_DEV, tr, C), lambda i: (0, i, 0)), spec, spec, spec], out_specs=[spec] * 4,
        out_shape=[jax.ShapeDtypeStruct((R, C), F32)] * 4, compiler_params=_params(1),
    )(recv, w, m, v)


def _forward_local(x, target, nw0, nw1, fw, wa_in, conv_w, gate_params, o_norm, wa_out, wb_in, wb_out):
    T, D = x.shape
    nD = D // LANES
    sv = {}
    sv["u0"] = _rmsnorm_fwd(x, nw0, "a_norm_fwd")
    sv["pa"] = _mm_nn(sv["u0"], wa_in, "a_in_proj")
    sv["qkv_a"] = _conv_fwd(sv["pa"], conv_w, D, "a_conv_fwd")
    sv["beta"], sv["gc"] = _gates_fwd(sv["pa"], gate_params, 4 * nD, "a_gates_fwd")
    sv["o_a"], sv["s_all"], sv["t_all"] = _chunk_fwd(sv["qkv_a"], sv["beta"], sv["gc"], D, "a_chunk_fwd")
    sv["y_a"] = _onorm_gate_fwd(sv["o_a"], sv["pa"], 3, o_norm, "a_onorm_fwd")
    sv["h1"] = _mm_nn(sv["y_a"], wa_out, "a_out_proj", add=x)
    sv["u1"] = _rmsnorm_fwd(sv["h1"], nw1, "b_norm_fwd")
    sv["qkv_b"] = _mm_nn(sv["u1"], wb_in[:, :3 * D], "b_in_proj_qkv", out_dtype=BF16)
    sv["gate_b"] = _mm_nn(sv["u1"], wb_in[:, 3 * D:], "b_in_proj_gate")
    sv["o_b"], sv["r_b"] = _sb_fwd(sv["qkv_b"], D, "b_attn_fwd")
    sv["y_b"] = _gate_mul_fwd(sv["o_b"], sv["gate_b"], "b_gate_fwd")
    sv["h2"] = _mm_nn(sv["y_b"], wb_out, "b_out_proj", add=sv["h1"])
    sv["dh2"], sv["loss"], sv["dfw"] = _final_loss(sv["h2"], fw, target, "final_loss")
    return sv


def _backward_local(sv, x, nw0, nw1, wa_in, conv_w, gate_params, o_norm, wa_out, wb_in, wb_out):
    T, D = x.shape
    nD = D // LANES
    g = {}
    dh2 = sv["dh2"]
    g["wb_out"] = _mm_tn(sv["y_b"], dh2, "b_out_proj_dw")
    dy_b = _mm_nt(dh2, wb_out, "b_out_proj_dx")
    do_b, dgate_b = _gate_mul_bwd(dy_b, sv["o_b"], sv["gate_b"], "b_gate_bwd")
    dq_b, dk_b, dv_b = _sb_bwd(sv["qkv_b"], do_b, sv["r_b"], D, "b_attn_bwd")
    dp_b = jnp.concatenate([dq_b, dk_b, dv_b, dgate_b], axis=1)
    g["wb_in"] = _mm_tn(sv["u1"], dp_b, "b_in_proj_dw")
    du1 = _mm_nt(dp_b, wb_in, "b_in_proj_dx")
    dh1, g["nw1"] = _rmsnorm_bwd(sv["h1"], nw1, du1, dh2, "b_norm_bwd")
    g["wa_out"] = _mm_tn(sv["y_a"], dh1, "a_out_proj_dw")
    dy_a = _mm_nt(dh1, wa_out, "a_out_proj_dx")
    do_a, dz_a, g["o_norm"] = _onorm_gate_bwd(dy_a, sv["o_a"], sv["pa"], 3, o_norm, "a_onorm_bwd")
    dqkv_a, dbeta, dg = _chunk_bwd(sv["qkv_a"], sv["beta"], sv["gc"], sv["s_all"], sv["t_all"], do_a, D, "a_chunk_bwd")
    dp_gates, g["gate_params"] = _gates_bwd(sv["pa"], gate_params, 4 * nD, dbeta, dg, "a_gates_bwd")
    dc, g["conv_w"] = _conv_bwd_act(sv["pa"], conv_w, dqkv_a, D, "a_conv_bwd_act")
    dp_qkv = _conv_bwd_input(dc, conv_w, "a_conv_bwd_input")
    dp_a = jnp.concatenate([dp_qkv, dz_a, dp_gates], axis=1)
    g["wa_in"] = _mm_tn(sv["u0"], dp_a, "a_in_proj_dw")
    du0 = _mm_nt(dp_a, wa_in, "a_in_proj_dx")
    g["x"], g["nw0"] = _rmsnorm_bwd(x, nw0, du0, dh1, "a_norm_bwd")
    g["fw"] = sv["dfw"]
    return g


def kernel(x, norm_w, a_w_in, a_conv_w, a_a_log, a_dt_bias, a_o_norm, a_w_out, b_w_in, b_w_out, final_norm_w, loss_target, m_norm_w, m_a_w_in, m_a_conv_w, m_a_a_log, m_a_dt_bias, m_a_o_norm, m_a_w_out, m_b_w_in, m_b_w_out, m_final_norm_w, v_norm_w, v_a_w_in, v_a_conv_w, v_a_a_log, v_a_dt_bias, v_a_o_norm, v_a_w_out, v_b_w_in, v_b_w_out, v_final_norm_w):
    D = x.shape[-1]
    H = D // HEAD_DIM
    shards = [a_w_in[0].astype(BF16), a_w_out[0].astype(BF16), b_w_in[0].astype(BF16), b_w_out[0].astype(BF16), a_conv_w[0]]
    ga_in, ga_out, gb_in, gb_out, g_conv = _exchange(shards, gather=True, name="weights_gather")
    wa = ga_in.transpose(1, 0, 2).reshape(D, -1)
    pad = lambda w: jnp.pad(w, ((0, 0), (0, LANES - w.shape[1])))
    wa_in = jnp.concatenate([wa[:, :4 * D], pad(wa[:, 4 * D:4 * D + H]), pad(wa[:, 4 * D + H:])], axis=1)
    wa_out = ga_out.reshape(D, D)
    wb_in = gb_in.transpose(1, 0, 2).reshape(D, 4 * D)
    wb_out = gb_out.reshape(D, D)
    conv_w = g_conv.transpose(1, 0, 2).reshape(4, 3 * D)
    gate_params = jnp.zeros((8, LANES), F32).at[0, :H].set(a_a_log[0]).at[1, :H].set(a_dt_bias[0])
    nw0, nw1, fw = norm_w[0:1], norm_w[1:2], final_norm_w[None]

    sv = _forward_local(x[0], loss_target[0], nw0, nw1, fw, wa_in, conv_w, gate_params, a_o_norm, wa_out, wb_in, wb_out)
    g = _backward_local(sv, x[0], nw0, nw1, wa_in, conv_w, gate_params, a_o_norm, wa_out, wb_in, wb_out)

    gwa = g["wa_in"]
    gwa = jnp.concatenate([gwa[:, :4 * D], gwa[:, 4 * D:4 * D + H], gwa[:, 4 * D + LANES:4 * D + LANES + H]], axis=1)
    row = lambda v: jnp.pad(v.reshape(1, -1), ((0, 0), (0, D - v.size)))
    small = jnp.concatenate([g["nw0"][0:1], g["nw1"][0:1], g["fw"][0:1], row(g["gate_params"][0, :H]),
                             row(g["gate_params"][1, :H]), row(g["o_norm"][0]), row(sv["loss"][0, 0:1]),
                             jnp.zeros((1, D), F32)], axis=0)
    contribs = [gwa.reshape(D, N_DEV, -1).transpose(1, 0, 2), g["wa_out"].reshape(N_DEV, D // N_DEV, D),
                g["wb_in"].reshape(D, N_DEV, -1).transpose(1, 0, 2), g["wb_out"].reshape(N_DEV, D // N_DEV, D),
                g["conv_w"].reshape(4, N_DEV, -1).transpose(1, 0, 2), jnp.broadcast_to(small[None], (N_DEV, 8, D))]
    ra_in, ra_out, rb_in, rb_out, r_conv, r_small = _exchange(contribs, gather=False, name="grads_exchange")

    outs = {}
    for nm, recv, w, m, v in (("a_w_in", ra_in, a_w_in, m_a_w_in, v_a_w_in), ("a_w_out", ra_out, a_w_out, m_a_w_out, v_a_w_out),
                              ("b_w_in", rb_in, b_w_in, m_b_w_in, v_b_w_in), ("b_w_out", rb_out, b_w_out, m_b_w_out, v_b_w_out),
                              ("a_conv_w", r_conv, a_conv_w, m_a_conv_w, v_a_conv_w)):
        outs[nm] = tuple(o[None] for o in _reduce_adamw(recv, w[0], m[0], v[0], "adamw_" + nm))

    def pack(nw, alog, dt, onorm, fnw):
        return jnp.concatenate([nw, fnw.reshape(1, D), row(alog), row(dt), row(onorm), jnp.zeros((2, D), F32)], axis=0)

    s_g, s_d, s_m, s_v = _reduce_adamw(
        r_small, pack(norm_w, a_a_log, a_dt_bias, a_o_norm, final_norm_w),
        pack(m_norm_w, m_a_a_log, m_a_dt_bias, m_a_o_norm, m_final_norm_w),
        pack(v_norm_w, v_a_a_log, v_a_dt_bias, v_a_o_norm, v_final_norm_w), "adamw_small")
    loss = s_g[6, 0]
    for i, s in enumerate((s_g, s_d, s_m, s_v)):
        outs.setdefault("norm_w", [None] * 4)[i] = s[0:2]
        outs.setdefault("final_norm_w", [None] * 4)[i] = s[2]
        outs.setdefault("a_a_log", [None] * 4)[i] = s[3:4, :H]
        outs.setdefault("a_dt_bias", [None] * 4)[i] = s[4:5, :H]
        outs.setdefault("a_o_norm", [None] * 4)[i] = s[5:6, :HEAD_DIM]
    names = ("norm_w", "a_w_in", "a_conv_w", "a_a_log", "a_dt_bias", "a_o_norm", "a_w_out", "b_w_in", "b_w_out", "final_norm_w")
    return (loss, g["x"][None]) + tuple(outs[n][i] for i in range(4) for n in names)
```

```python
import functools

import jax
import jax.numpy as jnp
from jax import lax
from jax.experimental import pallas as pl
from jax.experimental.pallas import tpu as pltpu

F32 = jnp.float32
BF16 = jnp.bfloat16
EPS = 1e-6
LOG2_E = 1.4426950408889634
MASKED_SCORE = -1e30
HEAD_DIM = 128
CHUNK = 64
ATTN_Q_BLOCKS = (512, 256)
ATTN_K_BLOCK = 128
LANES = 128
N_DEV = 8
VMEM_LIMIT_BYTES = 48 * 1024 * 1024
ADAM_LR, ADAM_B1, ADAM_B2, ADAM_EPS, ADAM_WD, ADAM_STEP = 0.001, 0.9, 0.999, 1e-08, 0.01, 10
MESH_ID = pl.DeviceIdType.MESH


def _pick(n, candidates):
    for c in candidates:
        if n % c == 0:
            return c
    raise ValueError(f"no tile for {n} in {candidates}")


def _params(n_grid_axes):
    return pltpu.CompilerParams(dimension_semantics=("arbitrary",) * n_grid_axes, vmem_limit_bytes=VMEM_LIMIT_BYTES)


def _dot(a, b):
    return jnp.dot(a.astype(BF16), b.astype(BF16), preferred_element_type=F32)


def _dot_nt(a, b):
    return lax.dot_general(a.astype(BF16), b.astype(BF16), (((1,), (1,)), ((), ())), preferred_element_type=F32)


def _dot_tn(a, b):
    return lax.dot_general(a.astype(BF16), b.astype(BF16), (((0,), (0,)), ((), ())), preferred_element_type=F32)


def _split2(x):
    hi = x.astype(BF16)
    lo = (x - hi.astype(F32)).astype(BF16)
    return hi, lo


def _split3(x):
    hi = x.astype(BF16)
    r = x - hi.astype(F32)
    mid = r.astype(BF16)
    lo = (r - mid.astype(F32)).astype(BF16)
    return hi, mid, lo


def _dot3(a, b):
    a_hi, a_lo = _split2(a)
    b_hi, b_lo = _split2(b)
    d = functools.partial(jnp.dot, preferred_element_type=F32)
    return d(a_hi, b_hi) + (d(a_hi, b_lo) + d(a_lo, b_hi))


def _silu(x):
    return x * jax.nn.sigmoid(x)


def _softplus(x):
    return jnp.maximum(x, 0.0) + jnp.log1p(jnp.exp(-jnp.abs(x)))


def _iota2(shape, axis):
    return lax.broadcasted_iota(jnp.int32, shape, axis)


def _rms_bwd_math(x, w, dy):
    r = lax.rsqrt(jnp.mean(x * x, axis=-1, keepdims=True) + EPS)
    xhat = x * r
    dxhat = dy * w
    dx = r * (dxhat - xhat * jnp.mean(dxhat * xhat, axis=-1, keepdims=True))
    dw = jnp.sum(dy * xhat, axis=0, keepdims=True)
    return dx, dw


def _rmsnorm_fwd(x, w, name):
    T, D = x.shape
    tm = _pick(T, (512, 256, 128))

    def body(x_ref, w_ref, o_ref):
        xf = x_ref[...]
        r = lax.rsqrt(jnp.mean(xf * xf, axis=-1, keepdims=True) + EPS)
        o_ref[...] = (xf * r * w_ref[...]).astype(BF16)

    return pl.pallas_call(
        body, name=name, grid=(T // tm,),
        in_specs=[pl.BlockSpec((tm, D), lambda i: (i, 0)), pl.BlockSpec((1, D), lambda i: (0, 0))],
        out_specs=pl.BlockSpec((tm, D), lambda i: (i, 0)),
        out_shape=jax.ShapeDtypeStruct((T, D), BF16), compiler_params=_params(1),
    )(x, w)


def _rmsnorm_bwd(x, w, du, dres, name):
    T, D = x.shape
    tm = _pick(T, (512, 256, 128))

    def body(x_ref, w_ref, du_ref, dres_ref, dx_ref, dw_ref):
        dx, dw = _rms_bwd_math(x_ref[...], w_ref[...], du_ref[...].astype(F32))
        dx_ref[...] = dres_ref[...] + dx

        @pl.when(pl.program_id(0) == 0)
        def _():
            dw_ref[...] = jnp.zeros_like(dw_ref)

        dw_ref[...] += jnp.broadcast_to(dw, dw_ref.shape)

    return pl.pallas_call(
        body, name=name, grid=(T // tm,),
        in_specs=[pl.BlockSpec((tm, D), lambda i: (i, 0)), pl.BlockSpec((1, D), lambda i: (0, 0)),
                  pl.BlockSpec((tm, D), lambda i: (i, 0)), pl.BlockSpec((tm, D), lambda i: (i, 0))],
        out_specs=[pl.BlockSpec((tm, D), lambda i: (i, 0)), pl.BlockSpec((8, D), lambda i: (0, 0))],
        out_shape=[jax.ShapeDtypeStruct((T, D), F32), jax.ShapeDtypeStruct((8, D), F32)],
        compiler_params=_params(1),
    )(x, w, du, dres)


def _mm_nn(a, b, name, add=None, out_dtype=F32):
    M, K = a.shape
    _, N = b.shape
    tm = _pick(M, (512, 256, 128))
    tn = _pick(N, (512, 256, 128))

    def body(*refs):
        a_ref, b_ref = refs[0], refs[1]
        o_ref = refs[-1]
        acc = _dot(a_ref[...], b_ref[...])
        if add is not None:
            acc = acc + refs[2][...]
        o_ref[...] = acc.astype(out_dtype)

    in_specs = [pl.BlockSpec((tm, K), lambda i, j: (i, 0)), pl.BlockSpec((K, tn), lambda i, j: (0, j))]
    args = [a, b]
    if add is not None:
        in_specs.append(pl.BlockSpec((tm, tn), lambda i, j: (i, j)))
        args.append(add)
    return pl.pallas_call(
        body, name=name, grid=(M // tm, N // tn), in_specs=in_specs,
        out_specs=pl.BlockSpec((tm, tn), lambda i, j: (i, j)),
        out_shape=jax.ShapeDtypeStruct((M, N), out_dtype), compiler_params=_params(2),
    )(*args)


def _mm_nt(a, b, name, out_dtype=F32):
    M, K = a.shape
    N, _ = b.shape
    tm = _pick(M, (512, 256, 128))
    tn = _pick(N, (512, 256, 128))

    def body(a_ref, b_ref, o_ref):
        o_ref[...] = _dot_nt(a_ref[...], b_ref[...]).astype(out_dtype)

    return pl.pallas_call(
        body, name=name, grid=(M // tm, N // tn),
        in_specs=[pl.BlockSpec((tm, K), lambda i, j: (i, 0)), pl.BlockSpec((tn, K), lambda i, j: (j, 0))],
        out_specs=pl.BlockSpec((tm, tn), lambda i, j: (i, j)),
        out_shape=jax.ShapeDtypeStruct((M, N), out_dtype), compiler_params=_params(2),
    )(a, b)


def _mm_tn(a, b, name):
    R, M = a.shape
    _, N = b.shape
    tm = _pick(M, (1024, 512, 256, 128))
    tn = _pick(N, (512, 256, 128))
    tr = _pick(R, (512, 256, 128))
    n_r = R // tr

    def body(a_ref, b_ref, o_ref, acc_ref):
        r = pl.program_id(2)

        @pl.when(r == 0)
        def _():
            acc_ref[...] = jnp.zeros_like(acc_ref)

        acc_ref[...] += _dot_tn(a_ref[...], b_ref[...])

        @pl.when(r == n_r - 1)
        def _():
            o_ref[...] = acc_ref[...]

    return pl.pallas_call(
        body, name=name, grid=(M // tm, N // tn, n_r),
        in_specs=[pl.BlockSpec((tr, tm), lambda i, j, r: (r, i)), pl.BlockSpec((tr, tn), lambda i, j, r: (r, j))],
        out_specs=pl.BlockSpec((tm, tn), lambda i, j, r: (i, j)),
        out_shape=jax.ShapeDtypeStruct((M, N), F32),
        scratch_shapes=[pltpu.VMEM((tm, tn), F32)], compiler_params=_params(3),
    )(a, b)


def _qkv_post(c, j, n_heads):
    s = _silu(c)
    parts = []
    for h in range(n_heads):
        sh = s[:, h * HEAD_DIM:(h + 1) * HEAD_DIM]
        parts.append(sh * lax.rsqrt(jnp.sum(sh * sh, axis=-1, keepdims=True) + EPS))
    n = jnp.concatenate(parts, axis=-1)
    is_q = (j == 0).astype(F32)
    is_v = (j == 2).astype(F32)
    n = n * (1.0 + is_q * (HEAD_DIM ** -0.5 - 1.0))
    return n * (1.0 - is_v) + s * is_v


def _conv_taps(cur, halo_prev):
    tm = cur.shape[0]
    ext = jnp.concatenate([halo_prev, cur], axis=0)
    taps = [pltpu.roll(ext, s, 0)[8:8 + tm] for s in (3, 2, 1)]
    return taps + [cur]


def _conv_fwd(p, conv_w, d_model, name):
    T = p.shape[0]
    D = d_model
    H = D // HEAD_DIM
    tm = _pick(T, (256, 128, 64))

    def body(cur_ref, prev_ref, w_ref, o_ref):
        i, j = pl.program_id(0), pl.program_id(1)
        prev = prev_ref[...] * (i > 0).astype(F32)
        taps = _conv_taps(cur_ref[...], prev)
        w = w_ref[...]
        c = sum(taps[k] * w[k:k + 1, :] for k in range(4))
        o_ref[...] = _qkv_post(c, j, H)

    return pl.pallas_call(
        body, name=name, grid=(T // tm, 3),
        in_specs=[pl.BlockSpec((tm, D), lambda i, j: (i, j)),
                  pl.BlockSpec((8, D), lambda i, j: (jnp.maximum(i * (tm // 8) - 1, 0), j)),
                  pl.BlockSpec((4, D), lambda i, j: (0, j))],
        out_specs=pl.BlockSpec((tm, D), lambda i, j: (i, j)),
        out_shape=jax.ShapeDtypeStruct((T, 3 * D), F32), compiler_params=_params(2),
    )(p, p, conv_w)


def _chunk_tri(tm, upper):
    r, c = _iota2((tm, tm), 0), _iota2((tm, tm), 1)
    same = (r // CHUNK) == (c // CHUNK)
    tri = (c >= r) if upper else (c <= r)
    return jnp.where(same & tri, 1.0, 0.0).astype(BF16)


def _dot_mask(mask_bf16, x):
    hi, mid, lo = _split3(x)
    d = functools.partial(jnp.dot, preferred_element_type=F32)
    return d(mask_bf16, hi) + (d(mask_bf16, mid) + d(mask_bf16, lo))


def _gates_math(pb, pa, a_log, dt_bias):
    beta = jax.nn.sigmoid(pb)
    g = -jnp.exp(a_log) * _softplus(pa + dt_bias)
    return beta, g


def _gates_fwd(p, gate_params, col0, name):
    T = p.shape[0]
    tm = _pick(T, (256, 128, 64))

    def body(pb_ref, pa_ref, gp_ref, beta_ref, gc_ref):
        gp = gp_ref[...]
        beta, g = _gates_math(pb_ref[...], pa_ref[...], gp[0:1, :], gp[1:2, :])
        beta_ref[...] = beta
        gc_ref[...] = _dot_mask(_chunk_tri(tm, upper=False), g)

    return pl.pallas_call(
        body, name=name, grid=(T // tm,),
        in_specs=[pl.BlockSpec((tm, LANES), lambda i: (i, col0)), pl.BlockSpec((tm, LANES), lambda i: (i, col0 + 1)),
                  pl.BlockSpec((8, LANES), lambda i: (0, 0))],
        out_specs=[pl.BlockSpec((tm, LANES), lambda i: (i, 0))] * 2,
        out_shape=[jax.ShapeDtypeStruct((T, LANES), F32)] * 2, compiler_params=_params(1),
    )(p, p, gate_params)


def _col_to_row(col):
    C = col.shape[0]
    eye = _iota2((C, C), 0) == _iota2((C, C), 1)
    return jnp.sum(jnp.where(eye, col, 0.0), axis=0, keepdims=True)


def _unit_lower_inverse(low):
    C = low.shape[0]
    eye = (_iota2((C, C), 0) == _iota2((C, C), 1)).astype(F32)
    t = eye - low
    p = _dot3(low, low)
    n = 2
    while True:
        t = t + _dot3(t, p)
        n *= 2
        if n >= C:
            return t
        p = _dot3(p, p)


def _chunk_head_fwd(q, k, v, gc, beta, s_in):
    C = q.shape[0]
    r, c = _iota2((C, C), 0), _iota2((C, C), 1)
    causal, strict = r >= c, r > c
    decay = jnp.where(causal, jnp.exp(jnp.where(causal, gc - _col_to_row(gc), 0.0)), 0.0)
    kb, vb = k * beta, v * beta
    low = jnp.where(strict, _dot_nt(kb, k) * decay, 0.0)
    t_inv = _unit_lower_inverse(low)
    eg = jnp.exp(gc)
    u = _dot(t_inv, vb)
    w = _dot(t_inv, kb * eg)
    attn = _dot_nt(q, k) * decay
    g_last = gc[C - 1:C, :]
    k_tail = k * jnp.exp(g_last - gc)
    v_new = u - _dot(w, s_in)
    o = _dot(q * eg, s_in) + _dot(attn, v_new)
    s_out = s_in * jnp.exp(g_last) + _dot_tn(k_tail, v_new)
    return o, s_out, t_inv


def _chunk_fwd(qkv, beta, gc, d_model, name):
    T = qkv.shape[0]
    D = d_model
    H = D // HEAD_DIM
    N = T // CHUNK

    def body(q_ref, k_ref, v_ref, beta_ref, gc_ref, o_ref, s_all_ref, t_all_ref, s_ref):
        @pl.when(pl.program_id(0) == 0)
        def _():
            s_ref[...] = jnp.zeros_like(s_ref)

        for h in range(H):
            hs = slice(h * HEAD_DIM, (h + 1) * HEAD_DIM)
            s_in = s_ref[h]
            s_all_ref[0, h] = s_in
            o, s_out, t_inv = _chunk_head_fwd(q_ref[:, hs], k_ref[:, hs], v_ref[:, hs],
                                              gc_ref[:, h:h + 1], beta_ref[:, h:h + 1], s_in)
            o_ref[:, hs] = o
            s_ref[h] = s_out
            t_all_ref[0, h] = t_inv

    return pl.pallas_call(
        body, name=name, grid=(N,),
        in_specs=[pl.BlockSpec((CHUNK, D), lambda n: (n, 0)), pl.BlockSpec((CHUNK, D), lambda n: (n, 1)),
                  pl.BlockSpec((CHUNK, D), lambda n: (n, 2)),
                  pl.BlockSpec((CHUNK, LANES), lambda n: (n, 0)), pl.BlockSpec((CHUNK, LANES), lambda n: (n, 0))],
        out_specs=[pl.BlockSpec((CHUNK, D), lambda n: (n, 0)),
                   pl.BlockSpec((1, H, HEAD_DIM, HEAD_DIM), lambda n: (n, 0, 0, 0)),
                   pl.BlockSpec((1, H, CHUNK, CHUNK), lambda n: (n, 0, 0, 0))],
        out_shape=[jax.ShapeDtypeStruct((T, D), F32), jax.ShapeDtypeStruct((N, H, HEAD_DIM, HEAD_DIM), F32),
                   jax.ShapeDtypeStruct((N, H, CHUNK, CHUNK), F32)],
        scratch_shapes=[pltpu.VMEM((H, HEAD_DIM, HEAD_DIM), F32)], compiler_params=_params(1),
    )(qkv, qkv, qkv, beta, gc)


def _onorm_gate_math(o, z, w, n_heads):
    parts = []
    for h in range(n_heads):
        hs = slice(h * HEAD_DIM, (h + 1) * HEAD_DIM)
        oh = o[:, hs]
        y = oh * lax.rsqrt(jnp.mean(oh * oh, axis=-1, keepdims=True) + EPS) * w
        parts.append(y * _silu(z[:, hs]))
    return jnp.concatenate(parts, axis=-1)


def _onorm_gate_fwd(o, p, z_col, o_norm, name):
    T, D = o.shape
    H = D // HEAD_DIM
    tm = _pick(T, (256, 128, 64))

    def body(o_ref, z_ref, w_ref, y_ref):
        y_ref[...] = _onorm_gate_math(o_ref[...], z_ref[...], w_ref[...], H).astype(BF16)

    return pl.pallas_call(
        body, name=name, grid=(T // tm,),
        in_specs=[pl.BlockSpec((tm, D), lambda i: (i, 0)), pl.BlockSpec((tm, D), lambda i: (i, z_col)),
                  pl.BlockSpec((1, HEAD_DIM), lambda i: (0, 0))],
        out_specs=pl.BlockSpec((tm, D), lambda i: (i, 0)),
        out_shape=jax.ShapeDtypeStruct((T, D), BF16), compiler_params=_params(1),
    )(o, p, o_norm)


def _diag_mask(qb, kb, d):
    return _iota2((qb, kb), 0) > _iota2((qb, kb), 1) + d * kb


def _fill_score_masks(mask_buf, qb, kb, ns):
    mask_buf[0] = jnp.zeros(mask_buf.shape[1:], F32)
    for d in range(ns):
        for half in range(2):
            mask_buf[d + 1, :, half * kb:(half + 1) * kb] = jnp.where(_diag_mask(qb, kb, 2 * d + half), 0.0, MASKED_SCORE)


def _softplus_bits(w):
    u = 1.0 + jnp.exp2(jnp.minimum(w, 64.0))
    return jnp.maximum(w, jnp.log2(u)), 1.0 / u


def _incl_lower(n):
    return jnp.where((_iota2((2 * n, n), 0) & (n - 1)) >= _iota2((2 * n, n), 1), 1.0, 0.0).astype(BF16)


def _incl_upper(n):
    return jnp.where((_iota2((2 * n, n), 0) & (n - 1)) <= _iota2((2 * n, n), 1), 1.0, 0.0).astype(BF16)


def _dot_cum(x, tri_bf16):
    hi, lo = _split2(x)
    return jnp.dot(jnp.concatenate([hi, lo], axis=1), tri_bf16, preferred_element_type=F32)


def _sb_fwd(qkv, d_model, name):
    T = qkv.shape[0]
    D = d_model
    H = D // HEAD_DIM
    QB = _pick(T, ATTN_Q_BLOCKS)
    KB = ATTN_K_BLOCK
    KS = 2 * KB
    ns = QB // KS
    nq = T // QB
    scale = HEAD_DIM ** -0.5

    def body(q_ref, k_ref, v_ref, o_ref, r_ref, w_buf, cum_buf, mask_buf):
        i = pl.program_id(1)

        @pl.when(i == 0)
        def _():
            _fill_score_masks(mask_buf, QB, KB, ns)

        q = q_ref[...]
        tri = _incl_lower(KB)
        n_tot = (i + 1) * ns

        def key_step(m):
            return jnp.maximum(n_tot - 1 - m, 0)

        def rows(ref, s):
            return ref[pl.ds(pl.multiple_of(s * KS, KS), KS), :]

        def scores(s):
            return _dot_nt(q, rows(k_ref, s)) * (scale * LOG2_E) + mask_buf[jnp.maximum(s - i * ns + 1, 0)]

        def cums(w):
            sp = _softplus_bits(w)[0]
            return jnp.concatenate([_dot_cum(sp[:, :KB], tri), _dot_cum(sp[:, KB:], tri)], axis=1)

        def weights(w, cum, carry):
            a_r = jnp.exp2(w[:, KB:] - cum[:, KB:] - carry)
            carry = carry + cum[:, KB:KB + 1]
            a_l = jnp.exp2(w[:, :KB] - cum[:, :KB] - carry)
            return jnp.concatenate([a_l, a_r], axis=1).astype(BF16), carry + cum[:, 0:1]

        def trip(m, carry):
            w_new = scores(key_step(m + 2))
            a, carry = weights(w_buf[m % 3], cum_buf[m % 2], carry)
            o_ref[...] += _dot(a, rows(v_ref, key_step(m)))
            cum_buf[(m + 1) % 2] = cums(w_buf[(m + 1) % 3])
            w_buf[(m + 2) % 3] = w_new
            return carry

        o_ref[...] = jnp.zeros_like(o_ref)
        w_buf[0] = scores(key_step(0))
        w_buf[1] = scores(key_step(1))
        cum_buf[0] = cums(w_buf[0])
        carry = lax.fori_loop(0, n_tot, trip, jnp.zeros((QB, 1), F32))
        r_ref[0] = jnp.broadcast_to(carry, (QB, LANES))

    return pl.pallas_call(
        body, name=name, grid=(H, nq),
        in_specs=[pl.BlockSpec((QB, HEAD_DIM), lambda h, i: (i, h)),
                  pl.BlockSpec((T, HEAD_DIM), lambda h, i: (0, H + h)),
                  pl.BlockSpec((T, HEAD_DIM), lambda h, i: (0, 2 * H + h))],
        out_specs=[pl.BlockSpec((QB, HEAD_DIM), lambda h, i: (i, h)),
                   pl.BlockSpec((1, QB, LANES), lambda h, i: (h, i, 0))],
        out_shape=[jax.ShapeDtypeStruct((T, D), F32), jax.ShapeDtypeStruct((H, T, LANES), F32)],
        scratch_shapes=[pltpu.VMEM((3, QB, KS), F32), pltpu.VMEM((2, QB, KS), F32), pltpu.VMEM((ns + 1, QB, KS), F32)],
        compiler_params=_params(2),
    )(qkv, qkv, qkv)


def _gate_mul_fwd(o, gate, name):
    T, D = o.shape
    tm = _pick(T, (512, 256, 128))

    def body(o_ref, g_ref, y_ref):
        y_ref[...] = (o_ref[...] * _silu(g_ref[...])).astype(BF16)

    spec = pl.BlockSpec((tm, D), lambda i: (i, 0))
    return pl.pallas_call(body, name=name, grid=(T // tm,), in_specs=[spec, spec], out_specs=spec,
                          out_shape=jax.ShapeDtypeStruct((T, D), BF16), compiler_params=_params(1))(o, gate)


def _final_loss(h, w, target, name):
    T, D = h.shape
    tm = _pick(T, (512, 256, 128))

    def body(h_ref, w_ref, t_ref, dh_ref, loss_ref, dw_ref):
        x, w = h_ref[...], w_ref[...]
        r = lax.rsqrt(jnp.mean(x * x, axis=-1, keepdims=True) + EPS)
        err = x * r * w - t_ref[...]
        part = 0.5 * jnp.sum(jnp.mean(err * err, axis=-1, keepdims=True), axis=0, keepdims=True)
        dx, dw = _rms_bwd_math(x, w, err * (1.0 / D))
        dh_ref[...] = dx

        @pl.when(pl.program_id(0) == 0)
        def _():
            loss_ref[...] = jnp.zeros_like(loss_ref)
            dw_ref[...] = jnp.zeros_like(dw_ref)

        loss_ref[...] += jnp.broadcast_to(part, loss_ref.shape)
        dw_ref[...] += jnp.broadcast_to(dw, dw_ref.shape)

    return pl.pallas_call(
        body, name=name, grid=(T // tm,),
        in_specs=[pl.BlockSpec((tm, D), lambda i: (i, 0)), pl.BlockSpec((1, D), lambda i: (0, 0)),
                  pl.BlockSpec((tm, D), lambda i: (i, 0))],
        out_specs=[pl.BlockSpec((tm, D), lambda i: (i, 0)), pl.BlockSpec((8, LANES), lambda i: (0, 0)),
                   pl.BlockSpec((8, D), lambda i: (0, 0))],
        out_shape=[jax.ShapeDtypeStruct((T, D), F32), jax.ShapeDtypeStruct((8, LANES), F32),
                   jax.ShapeDtypeStruct((8, D), F32)],
        compiler_params=_params(1),
    )(h, w, target)


def _gate_mul_bwd(dy, o, gate, name):
    T, D = o.shape
    tm = _pick(T, (512, 256, 128))

    def body(dy_ref, o_ref, g_ref, do_ref, dg_ref):
        dy, g = dy_ref[...], g_ref[...]
        s = jax.nn.sigmoid(g)
        do_ref[...] = dy * (g * s)
        dg_ref[...] = (dy * o_ref[...] * (s + g * s * (1.0 - s))).astype(BF16)

    spec = pl.BlockSpec((tm, D), lambda i: (i, 0))
    return pl.pallas_call(body, name=name, grid=(T // tm,), in_specs=[spec] * 3, out_specs=[spec] * 2,
                          out_shape=[jax.ShapeDtypeStruct((T, D), F32), jax.ShapeDtypeStruct((T, D), BF16)],
                          compiler_params=_params(1))(dy, o, gate)


def _sb_bwd(qkv, do, r_tot, d_model, name):
    T = qkv.shape[0]
    D = d_model
    H = D // HEAD_DIM
    QB = _pick(T, ATTN_Q_BLOCKS)
    KB = ATTN_K_BLOCK
    KS = 2 * KB
    ns = QB // KS
    nq = T // QB
    n_key_steps = T // KS
    scale = HEAD_DIM ** -0.5

    def body(q_ref, k_ref, v_ref, do_ref, r_ref, dq_ref, dk_ref, dv_ref,
             dkt_acc, dvt_acc, dq_acc, w_buf, da_buf, cum_buf, sig_buf, mask_buf):
        i = pl.program_id(1)

        @pl.when(i == 0)
        def _():
            dkt_acc[...] = jnp.zeros_like(dkt_acc)
            dvt_acc[...] = jnp.zeros_like(dvt_acc)
            _fill_score_masks(mask_buf, QB, KB, ns)

        q = q_ref[...]
        do_blk = do_ref[...].astype(BF16)
        q_t = q.astype(F32).T.astype(BF16)
        do_t = do_ref[...].T.astype(BF16)
        row_total = r_ref[0][:, 0:1]
        tri_rev = _incl_lower(KB)
        tri_fwd = jnp.where(_iota2((KB, KB), 0) <= _iota2((KB, KB), 1), 1.0, 0.0).astype(BF16)
        n_tot = (i + 1) * ns

        def step_rows(ref, s):
            return ref[pl.ds(pl.multiple_of(s * KS, KS), KS), :]

        def scores(s):
            w = _dot_nt(q, step_rows(k_ref, s)) * (scale * LOG2_E) + mask_buf[jnp.maximum(s - i * ns + 1, 0)]
            return w, _dot_nt(do_blk, step_rows(v_ref, s))

        def softplus_sums(w):
            sp, one_minus_sig = _softplus_bits(w)
            cum = jnp.concatenate([_dot_cum(sp[:, :KB], tri_rev), _dot_cum(sp[:, KB:], tri_rev)], axis=1)
            return cum, 1.0 - one_minus_sig

        def weights(w, cum, da, left_sp):
            right_l = row_total - left_sp - cum[:, 0:1]
            right_r = right_l - cum[:, KB:KB + 1]
            a = jnp.concatenate([jnp.exp2(w[:, :KB] - cum[:, :KB] - right_l),
                                 jnp.exp2(w[:, KB:] - cum[:, KB:] - right_r)], axis=1)
            p = da * a
            cp = jnp.concatenate([_dot(p[:, :KB], tri_fwd), _dot(p[:, KB:], tri_fwd)], axis=1)
            return a.astype(BF16), p, cp, row_total - right_r

        def score_grads(p, cp, sig, left_p):
            cum_l = cp[:, :KB] + left_p
            cum_r = cp[:, KB:] + cum_l[:, KB - 1:KB]
            dz = p - sig * jnp.concatenate([cum_l, cum_r], axis=1)
            return dz.astype(BF16), cum_r[:, KB - 1:KB]

        def trip(m, st):
            left_sp, left_p = st
            s2 = jnp.minimum(m + 2, n_tot - 1)
            s1 = jnp.minimum(m + 1, n_tot - 1)
            w_new, da_new = scores(s2)
            a, p, cp, left_sp = weights(w_buf[m % 3], cum_buf[m % 2], da_buf[m % 3], left_sp)
            cum_new, sig_new = softplus_sums(w_buf[s1 % 3])
            dz, left_p = score_grads(p, cp, sig_buf[m % 2], left_p)
            dq_acc[...] += _dot(dz, step_rows(k_ref, m))
            dkt_acc[m] += jnp.dot(q_t, dz, preferred_element_type=F32) * scale
            dvt_acc[m] += jnp.dot(do_t, a, preferred_element_type=F32)
            cum_buf[(m + 1) % 2] = cum_new
            sig_buf[(m + 1) % 2] = sig_new
            w_buf[(m + 2) % 3] = w_new
            da_buf[(m + 2) % 3] = da_new
            return left_sp, left_p

        dq_acc[...] = jnp.zeros_like(dq_acc)
        w_buf[0], da_buf[0] = scores(0)
        w_buf[1], da_buf[1] = scores(jnp.minimum(1, n_tot - 1))
        cum_buf[0], sig_buf[0] = softplus_sums(w_buf[0])
        zero_col = jnp.zeros((QB, 1), F32)
        lax.fori_loop(0, n_tot, trip, (zero_col, zero_col))
        dq_ref[...] = (dq_acc[...] * scale).astype(BF16)

        @pl.when(i == nq - 1)
        def _():
            for s in range(n_key_steps):
                dk_ref[s * KS:(s + 1) * KS, :] = dkt_acc[s].T.astype(BF16)
                dv_ref[s * KS:(s + 1) * KS, :] = dvt_acc[s].T.astype(BF16)

    return pl.pallas_call(
        body, name=name, grid=(H, nq),
        in_specs=[pl.BlockSpec((QB, HEAD_DIM), lambda h, i: (i, h)),
                  pl.BlockSpec((T, HEAD_DIM), lambda h, i: (0, H + h)),
                  pl.BlockSpec((T, HEAD_DIM), lambda h, i: (0, 2 * H + h)),
                  pl.BlockSpec((QB, HEAD_DIM), lambda h, i: (i, h)),
                  pl.BlockSpec((1, QB, LANES), lambda h, i: (h, i, 0))],
        out_specs=[pl.BlockSpec((QB, HEAD_DIM), lambda h, i: (i, h)),
                   pl.BlockSpec((T, HEAD_DIM), lambda h, i: (0, h)),
                   pl.BlockSpec((T, HEAD_DIM), lambda h, i: (0, h))],
        out_shape=[jax.ShapeDtypeStruct((T, D), BF16)] * 3,
        scratch_shapes=[pltpu.VMEM((n_key_steps, HEAD_DIM, KS), F32), pltpu.VMEM((n_key_steps, HEAD_DIM, KS), F32),
                        pltpu.VMEM((QB, HEAD_DIM), F32), pltpu.VMEM((3, QB, KS), F32), pltpu.VMEM((3, QB, KS), F32),
                        pltpu.VMEM((2, QB, KS), F32), pltpu.VMEM((2, QB, KS), F32), pltpu.VMEM((ns + 1, QB, KS), F32)],
        compiler_params=_params(2),
    )(qkv, qkv, qkv, do, r_tot)


def _onorm_gate_bwd(dy, o, p, z_col, o_norm, name):
    T, D = o.shape
    H = D // HEAD_DIM
    tm = _pick(T, (256, 128, 64))

    def body(dy_ref, o_ref, z_ref, w_ref, do_ref, dz_ref, dw_ref):
        _, vjp = jax.vjp(functools.partial(_onorm_gate_math, n_heads=H), o_ref[...], z_ref[...], w_ref[...])
        do, dz, dw = vjp(dy_ref[...])
        do_ref[...] = do
        dz_ref[...] = dz.astype(BF16)

        @pl.when(pl.program_id(0) == 0)
        def _():
            dw_ref[...] = jnp.zeros_like(dw_ref)

        dw_ref[...] += jnp.broadcast_to(dw, dw_ref.shape)

    return pl.pallas_call(
        body, name=name, grid=(T // tm,),
        in_specs=[pl.BlockSpec((tm, D), lambda i: (i, 0)), pl.BlockSpec((tm, D), lambda i: (i, 0)),
                  pl.BlockSpec((tm, D), lambda i: (i, z_col)), pl.BlockSpec((1, HEAD_DIM), lambda i: (0, 0))],
        out_specs=[pl.BlockSpec((tm, D), lambda i: (i, 0)), pl.BlockSpec((tm, D), lambda i: (i, 0)),
                   pl.BlockSpec((8, HEAD_DIM), lambda i: (0, 0))],
        out_shape=[jax.ShapeDtypeStruct((T, D), F32), jax.ShapeDtypeStruct((T, D), BF16),
                   jax.ShapeDtypeStruct((8, HEAD_DIM), F32)],
        compiler_params=_params(1),
    )(dy, o, p, o_norm)


def _row_to_col(row):
    C = row.shape[1]
    eye = _iota2((C, C), 0) == _iota2((C, C), 1)
    return jnp.sum(jnp.where(eye, row, 0.0), axis=1, keepdims=True)


def _lane_sum(x):
    return jnp.sum(x, axis=-1, keepdims=True)


def _chunk_head_bwd(q, k, v, gc, beta, s_in, t_inv, do, ds_out):
    C = q.shape[0]
    r, c = _iota2((C, C), 0), _iota2((C, C), 1)
    causal, strict = r >= c, r > c
    decay = jnp.where(causal, jnp.exp(jnp.where(causal, gc - _col_to_row(gc), 0.0)), 0.0)
    kb, vb = k * beta, v * beta
    kk = _dot_nt(kb, k)
    low = jnp.where(strict, kk * decay, 0.0)
    eg = jnp.exp(gc)
    kbg = kb * eg
    u = _dot(t_inv, vb)
    w = _dot(t_inv, kbg)
    qk = _dot_nt(q, k)
    attn = qk * decay
    g_last = gc[C - 1:C, :]
    e_tail = jnp.exp(g_last - gc)
    k_tail = k * e_tail
    gl = jnp.exp(g_last)
    qg = q * eg
    v_new = u - _dot(w, s_in)

    d_vnew = _dot_tn(attn, do) + _dot(k_tail, ds_out)
    d_ktail = _dot_nt(v_new, ds_out)
    d_gl = jnp.sum(_lane_sum(s_in * ds_out), axis=0, keepdims=True)
    d_qg = _dot_nt(do, s_in)
    d_attn = jnp.where(causal, _dot_nt(do, v_new), 0.0)
    d_w = -_dot_nt(d_vnew, s_in)
    ds_in = ds_out * gl + _dot_tn(qg, do) - _dot_tn(w, d_vnew)
    d_vb = _dot_tn(t_inv, d_vnew)
    d_kbg = _dot_tn(t_inv, d_w)
    d_tinv = _dot_nt(d_vnew, vb) + _dot_nt(d_w, kbg)
    d_low = jnp.where(strict, -_dot_nt(_dot_tn(t_inv, d_tinv), t_inv), 0.0)
    d_kk = d_low * decay
    d_qk = d_attn * decay
    d_kb = _dot(d_kk, k) + d_kbg * eg
    dq = _dot(d_qk, k) + d_qg * eg
    dk = _dot_tn(d_kk, kb) + _dot_tn(d_qk, q) + d_ktail * e_tail + d_kb * beta
    dv = d_vb * beta
    dbeta = _lane_sum(d_kb * k) + _lane_sum(d_vb * v)
    m = d_low * low + d_attn * attn
    tail_term = _lane_sum(d_ktail * k_tail)
    d_g_last = d_gl * gl + jnp.sum(tail_term, axis=0, keepdims=True)
    dgc = (_lane_sum(m) - _row_to_col(jnp.sum(m, axis=0, keepdims=True))
           + _lane_sum(d_qg * qg) + _lane_sum(d_kbg * kbg) - tail_term)
    dgc = dgc + jnp.where(_iota2((C, 1), 0) == C - 1, d_g_last, 0.0)
    return dq, dk, dv, dgc, dbeta, ds_in


def _chunk_bwd(qkv, beta, gc, s_all, t_all, do, d_model, name):
    T = qkv.shape[0]
    D = d_model
    H = D // HEAD_DIM
    N = T // CHUNK

    def body(q_ref, k_ref, v_ref, beta_ref, gc_ref, s_ref, t_ref, do_ref, dqkv_ref, dbeta_ref, dg_ref, ds_ref):
        @pl.when(pl.program_id(0) == 0)
        def _():
            ds_ref[...] = jnp.zeros_like(ds_ref)

        lane = _iota2((CHUNK, LANES), 1)
        dgc_all = jnp.zeros((CHUNK, LANES), F32)
        dbeta_all = jnp.zeros((CHUNK, LANES), F32)
        for h in range(H):
            hs = slice(h * HEAD_DIM, (h + 1) * HEAD_DIM)
            dq, dk, dv, dgc, dbeta, ds_in = _chunk_head_bwd(
                q_ref[:, hs], k_ref[:, hs], v_ref[:, hs], gc_ref[:, h:h + 1], beta_ref[:, h:h + 1],
                s_ref[0, h], t_ref[0, h], do_ref[:, hs], ds_ref[h])
            ds_ref[h] = ds_in
            dqkv_ref[:, h * HEAD_DIM:(h + 1) * HEAD_DIM] = dq
            dqkv_ref[:, D + h * HEAD_DIM:D + (h + 1) * HEAD_DIM] = dk
            dqkv_ref[:, 2 * D + h * HEAD_DIM:2 * D + (h + 1) * HEAD_DIM] = dv
            dgc_all = jnp.where(lane == h, dgc, dgc_all)
            dbeta_all = jnp.where(lane == h, dbeta, dbeta_all)
        dbeta_ref[...] = dbeta_all
        dg_ref[...] = _dot_mask(_chunk_tri(CHUNK, upper=True), dgc_all)

    rev = lambda n: N - 1 - n
    return pl.pallas_call(
        body, name=name, grid=(N,),
        in_specs=[pl.BlockSpec((CHUNK, D), lambda n: (rev(n), 0)), pl.BlockSpec((CHUNK, D), lambda n: (rev(n), 1)),
                  pl.BlockSpec((CHUNK, D), lambda n: (rev(n), 2)),
                  pl.BlockSpec((CHUNK, LANES), lambda n: (rev(n), 0)), pl.BlockSpec((CHUNK, LANES), lambda n: (rev(n), 0)),
                  pl.BlockSpec((1, H, HEAD_DIM, HEAD_DIM), lambda n: (rev(n), 0, 0, 0)),
                  pl.BlockSpec((1, H, CHUNK, CHUNK), lambda n: (rev(n), 0, 0, 0)),
                  pl.BlockSpec((CHUNK, D), lambda n: (rev(n), 0))],
        out_specs=[pl.BlockSpec((CHUNK, 3 * D), lambda n: (rev(n), 0)),
                   pl.BlockSpec((CHUNK, LANES), lambda n: (rev(n), 0)), pl.BlockSpec((CHUNK, LANES), lambda n: (rev(n), 0))],
        out_shape=[jax.ShapeDtypeStruct((T, 3 * D), F32), jax.ShapeDtypeStruct((T, LANES), F32),
                   jax.ShapeDtypeStruct((T, LANES), F32)],
        scratch_shapes=[pltpu.VMEM((H, HEAD_DIM, HEAD_DIM), F32)], compiler_params=_params(1),
    )(qkv, qkv, qkv, beta, gc, s_all, t_all, do)


def _gates_bwd(p, gate_params, col0, dbeta, dg, name):
    T = p.shape[0]
    tm = _pick(T, (256, 128, 64))

    def body(pb_ref, pa_ref, gp_ref, dbeta_ref, dg_ref, dp_ref, dgp_ref):
        gp = gp_ref[...]
        _, vjp = jax.vjp(_gates_math, pb_ref[...], pa_ref[...], gp[0:1, :], gp[1:2, :])
        dpb, dpa, d_alog, d_dt = vjp((dbeta_ref[...], dg_ref[...]))
        dp_ref[:, 0:LANES] = dpb.astype(BF16)
        dp_ref[:, LANES:2 * LANES] = dpa.astype(BF16)

        @pl.when(pl.program_id(0) == 0)
        def _():
            dgp_ref[...] = jnp.zeros_like(dgp_ref)

        dgp_ref[0:1, :] += d_alog
        dgp_ref[1:2, :] += d_dt

    return pl.pallas_call(
        body, name=name, grid=(T // tm,),
        in_specs=[pl.BlockSpec((tm, LANES), lambda i: (i, col0)), pl.BlockSpec((tm, LANES), lambda i: (i, col0 + 1)),
                  pl.BlockSpec((8, LANES), lambda i: (0, 0)),
                  pl.BlockSpec((tm, LANES), lambda i: (i, 0)), pl.BlockSpec((tm, LANES), lambda i: (i, 0))],
        out_specs=[pl.BlockSpec((tm, 2 * LANES), lambda i: (i, 0)), pl.BlockSpec((8, LANES), lambda i: (0, 0))],
        out_shape=[jax.ShapeDtypeStruct((T, 2 * LANES), BF16), jax.ShapeDtypeStruct((8, LANES), F32)],
        compiler_params=_params(1),
    )(p, p, gate_params, dbeta, dg)


def _conv_bwd_act(p, conv_w, dqkv, d_model, name):
    T = p.shape[0]
    D = d_model
    H = D // HEAD_DIM
    tm = _pick(T, (256, 128, 64))

    def body(cur_ref, prev_ref, w_ref, dout_ref, dc_ref, dw_ref):
        j, i = pl.program_id(0), pl.program_id(1)
        prev = prev_ref[...] * (i > 0).astype(F32)
        taps = _conv_taps(cur_ref[...], prev)
        w = w_ref[...]
        c = sum(taps[k] * w[k:k + 1, :] for k in range(4))
        _, vjp = jax.vjp(lambda cc: _qkv_post(cc, j, H), c)
        (dc,) = vjp(dout_ref[...])
        dc_ref[...] = dc

        @pl.when(i == 0)
        def _():
            dw_ref[...] = jnp.zeros_like(dw_ref)

        for k in range(4):
            dw_ref[k:k + 1, :] += jnp.sum(dc * taps[k], axis=0, keepdims=True)

    return pl.pallas_call(
        body, name=name, grid=(3, T // tm),
        in_specs=[pl.BlockSpec((tm, D), lambda j, i: (i, j)),
                  pl.BlockSpec((8, D), lambda j, i: (jnp.maximum(i * (tm // 8) - 1, 0), j)),
                  pl.BlockSpec((4, D), lambda j, i: (0, j)),
                  pl.BlockSpec((tm, D), lambda j, i: (i, j))],
        out_specs=[pl.BlockSpec((tm, D), lambda j, i: (i, j)), pl.BlockSpec((4, D), lambda j, i: (0, j))],
        out_shape=[jax.ShapeDtypeStruct((T, 3 * D), F32), jax.ShapeDtypeStruct((4, 3 * D), F32)],
        compiler_params=_params(2),
    )(p, p, conv_w, dqkv)


def _conv_bwd_input(dc, conv_w, name):
    T, D3 = dc.shape
    D = D3 // 3
    tm = _pick(T, (256, 128, 64))
    n_t = T // tm

    def body(cur_ref, next_ref, w_ref, dp_ref):
        i = pl.program_id(0)
        cur = cur_ref[...]
        nxt = next_ref[...] * (i < n_t - 1).astype(F32)
        ext = jnp.concatenate([cur, nxt], axis=0)
        w = w_ref[...]
        acc = cur * w[3:4, :]
        for s in (1, 2, 3):
            acc = acc + pltpu.roll(ext, tm + 8 - s, 0)[0:tm] * w[3 - s:4 - s, :]
        dp_ref[...] = acc.astype(BF16)

    return pl.pallas_call(
        body, name=name, grid=(n_t, 3),
        in_specs=[pl.BlockSpec((tm, D), lambda i, j: (i, j)),
                  pl.BlockSpec((8, D), lambda i, j: (jnp.minimum((i + 1) * (tm // 8), T // 8 - 1), j)),
                  pl.BlockSpec((4, D), lambda i, j: (0, j))],
        out_specs=pl.BlockSpec((tm, D), lambda i, j: (i, j)),
        out_shape=jax.ShapeDtypeStruct((T, D3), BF16), compiler_params=_params(2),
    )(dc, dc, conv_w)


def _exchange(arrays, gather, name):
    n = len(arrays)

    def body(*refs):
        ins, outs = refs[:n], refs[n:2 * n]
        send_sems, recv_sems, local_sems = refs[2 * n:]
        x, y, c = lax.axis_index("x"), lax.axis_index("y"), lax.axis_index("c")
        me = 4 * x + 2 * y + c

        def peer(k):
            px = 1 - x if k & 4 else x
            py = 1 - y if k & 2 else y
            pc = 1 - c if k & 1 else c
            return (px, py, pc), 4 * px + 2 * py + pc

        def remote(a, k):
            dev, idx = peer(k)
            src = ins[a] if gather else ins[a].at[idx]
            return pltpu.make_async_remote_copy(
                src_ref=src, dst_ref=outs[a].at[me], send_sem=send_sems.at[a * 7 + k - 1],
                recv_sem=recv_sems.at[a * 7 + k - 1], device_id=dev, device_id_type=MESH_ID)

        def arrival(a, k):
            dev, idx = peer(k)
            src = ins[a] if gather else ins[a].at[idx]
            return pltpu.make_async_remote_copy(
                src_ref=src, dst_ref=outs[a].at[idx], send_sem=send_sems.at[a * 7 + k - 1],
                recv_sem=recv_sems.at[a * 7 + k - 1], device_id=dev, device_id_type=MESH_ID)

        local = [pltpu.make_async_copy(ins[a] if gather else ins[a].at[me], outs[a].at[me], local_sems.at[a])
                 for a in range(n)]
        sends = [remote(a, k) for k in range(1, 8) for a in range(n)]
        for cp in local + sends:
            cp.start()
        for k in range(1, 8):
            for a in range(n):
                arrival(a, k).wait_recv()
        for cp in sends:
            cp.wait_send()
        for cp in local:
            cp.wait()

    out_shape = [jax.ShapeDtypeStruct((N_DEV,) + a.shape if gather else a.shape, a.dtype) for a in arrays]
    any_spec = pl.BlockSpec(memory_space=pl.ANY)
    return pl.pallas_call(
        body, name=name, in_specs=[any_spec] * n, out_specs=[any_spec] * n, out_shape=out_shape,
        scratch_shapes=[pltpu.SemaphoreType.DMA((7 * n,)), pltpu.SemaphoreType.DMA((7 * n,)), pltpu.SemaphoreType.DMA((n,))],
        compiler_params=pltpu.CompilerParams(has_side_effects=True),
    )(*arrays)


def _reduce_adamw(recv, w, m, v, name):
    _, R, C = recv.shape
    tr = next((t for t in (256, 128, 64, 32, 16, 8) if R % t == 0), R)
    c1 = 1.0 - ADAM_B1 ** ADAM_STEP
    c2 = 1.0 - ADAM_B2 ** ADAM_STEP

    def body(r_ref, w_ref, m_ref, v_ref, g_ref, d_ref, nm_ref, nv_ref):
        g = r_ref[0]
        for s in range(1, N_DEV):
            g = g + r_ref[s]
        nm = ADAM_B1 * m_ref[...] + (1.0 - ADAM_B1) * g
        nv = ADAM_B2 * v_ref[...] + (1.0 - ADAM_B2) * (g * g)
        g_ref[...] = g
        nm_ref[...] = nm
        nv_ref[...] = nv
        d_ref[...] = -ADAM_LR * ((nm / c1) / (jnp.sqrt(nv / c2) + ADAM_EPS) + ADAM_WD * w_ref[...])

    spec = pl.BlockSpec((tr, C), lambda i: (i, 0))
    return pl.pallas_call(
        body, name=name, grid=(R // tr,),
        in_specs=[pl.BlockSpec((N_DEV, tr, C), lambda i: (0, i, 0)), spec, spec, spec], out_specs=[spec] * 4,
        out_shape=[jax.ShapeDtypeStruct((R, C), F32)] * 4, compiler_params=_params(1),
    )(recv, w, m, v)


def _forward_local(x, target, nw0, nw1, fw, wa_in, conv_w, gate_params, o_norm, wa_out, wb_in, wb_out):
    T, D = x.shape
    nD = D // LANES
    sv = {}
    sv["u0"] = _rmsnorm_fwd(x, nw0, "a_norm_fwd")
    sv["pa"] = _mm_nn(sv["u0"], wa_in, "a_in_proj")
    sv["qkv_a"] = _conv_fwd(sv["pa"], conv_w, D, "a_conv_fwd")
    sv["beta"], sv["gc"] = _gates_fwd(sv["pa"], gate_params, 4 * nD, "a_gates_fwd")
    sv["o_a"], sv["s_all"], sv["t_all"] = _chunk_fwd(sv["qkv_a"], sv["beta"], sv["gc"], D, "a_chunk_fwd")
    sv["y_a"] = _onorm_gate_fwd(sv["o_a"], sv["pa"], 3, o_norm, "a_onorm_fwd")
    sv["h1"] = _mm_nn(sv["y_a"], wa_out, "a_out_proj", add=x)
    sv["u1"] = _rmsnorm_fwd(sv["h1"], nw1, "b_norm_fwd")
    sv["qkv_b"] = _mm_nn(sv["u1"], wb_in[:, :3 * D], "b_in_proj_qkv", out_dtype=BF16)
    sv["gate_b"] = _mm_nn(sv["u1"], wb_in[:, 3 * D:], "b_in_proj_gate")
    sv["o_b"], sv["r_b"] = _sb_fwd(sv["qkv_b"], D, "b_attn_fwd")
    sv["y_b"] = _gate_mul_fwd(sv["o_b"], sv["gate_b"], "b_gate_fwd")
    sv["h2"] = _mm_nn(sv["y_b"], wb_out, "b_out_proj", add=sv["h1"])
    sv["dh2"], sv["loss"], sv["dfw"] = _final_loss(sv["h2"], fw, target, "final_loss")
    return sv


def _backward_local(sv, x, nw0, nw1, wa_in, conv_w, gate_params, o_norm, wa_out, wb_in, wb_out):
    T, D = x.shape
    nD = D // LANES
    g = {}
    dh2 = sv["dh2"]
    g["wb_out"] = _mm_tn(sv["y_b"], dh2, "b_out_proj_dw")
    dy_b = _mm_nt(dh2, wb_out, "b_out_proj_dx")
    do_b, dgate_b = _gate_mul_bwd(dy_b, sv["o_b"], sv["gate_b"], "b_gate_bwd")
    dq_b, dk_b, dv_b = _sb_bwd(sv["qkv_b"], do_b, sv["r_b"], D, "b_attn_bwd")
    dp_b = jnp.concatenate([dq_b, dk_b, dv_b, dgate_b], axis=1)
    g["wb_in"] = _mm_tn(sv["u1"], dp_b, "b_in_proj_dw")
    du1 = _mm_nt(dp_b, wb_in, "b_in_proj_dx")
    dh1, g["nw1"] = _rmsnorm_bwd(sv["h1"], nw1, du1, dh2, "b_norm_bwd")
    g["wa_out"] = _mm_tn(sv["y_a"], dh1, "a_out_proj_dw")
    dy_a = _mm_nt(dh1, wa_out, "a_out_proj_dx")
    do_a, dz_a, g["o_norm"] = _onorm_gate_bwd(dy_a, sv["o_a"], sv["pa"], 3, o_norm, "a_onorm_bwd")
    dqkv_a, dbeta, dg = _chunk_bwd(sv["qkv_a"], sv["beta"], sv["gc"], sv["s_all"], sv["t_all"], do_a, D, "a_chunk_bwd")
    dp_gates, g["gate_params"] = _gates_bwd(sv["pa"], gate_params, 4 * nD, dbeta, dg, "a_gates_bwd")
    dc, g["conv_w"] = _conv_bwd_act(sv["pa"], conv_w, dqkv_a, D, "a_conv_bwd_act")
    dp_qkv = _conv_bwd_input(dc, conv_w, "a_conv_bwd_input")
    dp_a = jnp.concatenate([dp_qkv, dz_a, dp_gates], axis=1)
    g["wa_in"] = _mm_tn(sv["u0"], dp_a, "a_in_proj_dw")
    du0 = _mm_nt(dp_a, wa_in, "a_in_proj_dx")
    g["x"], g["nw0"] = _rmsnorm_bwd(x, nw0, du0, dh1, "a_norm_bwd")
    g["fw"] = sv["dfw"]
    return g


def kernel(x, norm_w, a_w_in, a_conv_w, a_a_log, a_dt_bias, a_o_norm, a_w_out, b_w_in, b_w_out, final_norm_w, loss_target, m_norm_w, m_a_w_in, m_a_conv_w, m_a_a_log, m_a_dt_bias, m_a_o_norm, m_a_w_out, m_b_w_in, m_b_w_out, m_final_norm_w, v_norm_w, v_a_w_in, v_a_conv_w, v_a_a_log, v_a_dt_bias, v_a_o_norm, v_a_w_out, v_b_w_in, v_b_w_out, v_final_norm_w):
    D = x.shape[-1]
    H = D // HEAD_DIM
    shards = [a_w_in[0].astype(BF16), a_w_out[0].astype(BF16), b_w_in[0].astype(BF16), b_w_out[0].astype(BF16), a_conv_w[0]]
    ga_in, ga_out, gb_in, gb_out, g_conv = _exchange(shards, gather=True, name="weights_gather")
    wa = ga_in.transpose(1, 0, 2).reshape(D, -1)
    pad = lambda w: jnp.pad(w, ((0, 0), (0, LANES - w.shape[1])))
    wa_in = jnp.concatenate([wa[:, :4 * D], pad(wa[:, 4 * D:4 * D + H]), pad(wa[:, 4 * D + H:])], axis=1)
    wa_out = ga_out.reshape(D, D)
    wb_in = gb_in.transpose(1, 0, 2).reshape(D, 4 * D)
    wb_out = gb_out.reshape(D, D)
    conv_w = g_conv.transpose(1, 0, 2).reshape(4, 3 * D)
    gate_params = jnp.zeros((8, LANES), F32).at[0, :H].set(a_a_log[0]).at[1, :H].set(a_dt_bias[0])
    nw0, nw1, fw = norm_w[0:1], norm_w[1:2], final_norm_w[None]

    sv = _forward_local(x[0], loss_target[0], nw0, nw1, fw, wa_in, conv_w, gate_params, a_o_norm, wa_out, wb_in, wb_out)
    g = _backward_local(sv, x[0], nw0, nw1, wa_in, conv_w, gate_params, a_o_norm, wa_out, wb_in, wb_out)

    gwa = g["wa_in"]
    gwa = jnp.concatenate([gwa[:, :4 * D], gwa[:, 4 * D:4 * D + H], gwa[:, 4 * D + LANES:4 * D + LANES + H]], axis=1)
    row = lambda v: jnp.pad(v.reshape(1, -1), ((0, 0), (0, D - v.size)))
    small = jnp.concatenate([g["nw0"][0:1], g["nw1"][0:1], g["fw"][0:1], row(g["gate_params"][0, :H]),
                             row(g["gate_params"][1, :H]), row(g["o_norm"][0]), row(sv["loss"][0, 0:1]),
                             jnp.zeros((1, D), F32)], axis=0)
    contribs = [gwa.reshape(D, N_DEV, -1).transpose(1, 0, 2), g["wa_out"].reshape(N_DEV, D // N_DEV, D),
                g["wb_in"].reshape(D, N_DEV, -1).transpose(1, 0, 2), g["wb_out"].reshape(N_DEV, D // N_DEV, D),
                g["conv_w"].reshape(4, N_DEV, -1).transpose(1, 0, 2), jnp.broadcast_to(small[None], (N_DEV, 8, D))]
    ra_in, ra_out, rb_in, rb_out, r_conv, r_small = _exchange(contribs, gather=False, name="grads_exchange")

    outs = {}
    for nm, recv, w, m, v in (("a_w_in", ra_in, a_w_in, m_a_w_in, v_a_w_in), ("a_w_out", ra_out, a_w_out, m_a_w_out, v_a_w_out),
                              ("b_w_in", rb_in, b_w_in, m_b_w_in, v_b_w_in), ("b_w_out", rb_out, b_w_out, m_b_w_out, v_b_w_out),
                              ("a_conv_w", r_conv, a_conv_w, m_a_conv_w, v_a_conv_w)):
        outs[nm] = tuple(o[None] for o in _reduce_adamw(recv, w[0], m[0], v[0], "adamw_" + nm))

    def pack(nw, alog, dt, onorm, fnw):
        return jnp.concatenate([nw, fnw.reshape(1, D), row(alog), row(dt), row(onorm), jnp.zeros((2, D), F32)], axis=0)

    s_g, s_d, s_m, s_v = _reduce_adamw(
        r_small, pack(norm_w, a_a_log, a_dt_bias, a_o_norm, final_norm_w),
        pack(m_norm_w, m_a_a_log, m_a_dt_bias, m_a_o_norm, m_final_norm_w),
        pack(v_norm_w, v_a_a_log, v_a_dt_bias, v_a_o_norm, v_final_norm_w), "adamw_small")
    loss = s_g[6, 0]
    for i, s in enumerate((s_g, s_d, s_m, s_v)):
        outs.setdefault("norm_w", [None] * 4)[i] = s[0:2]
        outs.setdefault("final_norm_w", [None] * 4)[i] = s[2]
        outs.setdefault("a_a_log", [None] * 4)[i] = s[3:4, :H]
        outs.setdefault("a_dt_bias", [None] * 4)[i] = s[4:5, :H]
        outs.setdefault("a_o_norm", [None] * 4)[i] = s[5:6, :HEAD_DIM]
    names = ("norm_w", "a_w_in", "a_conv_w", "a_a_log", "a_dt_bias", "a_o_norm", "a_w_out", "b_w_in", "b_w_out", "final_norm_w")
    return (loss, g["x"][None]) + tuple(outs[n][i] for i in range(4) for n in names)
```

```python
import functools

import jax
import jax.numpy as jnp
from jax import lax
from jax.experimental import pallas as pl
from jax.experimental.pallas import tpu as pltpu

F32 = jnp.float32
BF16 = jnp.bfloat16
EPS = 1e-6
LOG2_E = 1.4426950408889634
MASKED_SCORE = -1e30
HEAD_DIM = 128
CHUNK = 64
ATTN_Q_BLOCKS = (512, 256)
ATTN_K_BLOCK = 128
LANES = 128
N_DEV = 8
VMEM_LIMIT_BYTES = 48 * 1024 * 1024
ADAM_LR, ADAM_B1, ADAM_B2, ADAM_EPS, ADAM_WD, ADAM_STEP = 0.001, 0.9, 0.999, 1e-08, 0.01, 10
MESH_ID = pl.DeviceIdType.MESH


def _pick(n, candidates):
    for c in candidates:
        if n % c == 0:
            return c
    raise ValueError(f"no tile for {n} in {candidates}")


def _params(n_grid_axes):
    return pltpu.CompilerParams(dimension_semantics=("arbitrary",) * n_grid_axes, vmem_limit_bytes=VMEM_LIMIT_BYTES)


def _dot(a, b):
    return jnp.dot(a.astype(BF16), b.astype(BF16), preferred_element_type=F32)


def _dot_nt(a, b):
    return lax.dot_general(a.astype(BF16), b.astype(BF16), (((1,), (1,)), ((), ())), preferred_element_type=F32)


def _dot_tn(a, b):
    return lax.dot_general(a.astype(BF16), b.astype(BF16), (((0,), (0,)), ((), ())), preferred_element_type=F32)


def _split2(x):
    hi = x.astype(BF16)
    lo = (x - hi.astype(F32)).astype(BF16)
    return hi, lo


def _split3(x):
    hi = x.astype(BF16)
    r = x - hi.astype(F32)
    mid = r.astype(BF16)
    lo = (r - mid.astype(F32)).astype(BF16)
    return hi, mid, lo


def _dot3(a, b):
    a_hi, a_lo = _split2(a)
    b_hi, b_lo = _split2(b)
    d = functools.partial(jnp.dot, preferred_element_type=F32)
    return d(a_hi, b_hi) + (d(a_hi, b_lo) + d(a_lo, b_hi))


def _silu(x):
    return x * jax.nn.sigmoid(x)


def _softplus(x):
    return jnp.maximum(x, 0.0) + jnp.log1p(jnp.exp(-jnp.abs(x)))


def _iota2(shape, axis):
    return lax.broadcasted_iota(jnp.int32, shape, axis)


def _rms_bwd_math(x, w, dy):
    r = lax.rsqrt(jnp.mean(x * x, axis=-1, keepdims=True) + EPS)
    xhat = x * r
    dxhat = dy * w
    dx = r * (dxhat - xhat * jnp.mean(dxhat * xhat, axis=-1, keepdims=True))
    dw = jnp.sum(dy * xhat, axis=0, keepdims=True)
    return dx, dw


def _rmsnorm_fwd(x, w, name):
    T, D = x.shape
    tm = _pick(T, (512, 256, 128))

    def body(x_ref, w_ref, o_ref):
        xf = x_ref[...]
        r = lax.rsqrt(jnp.mean(xf * xf, axis=-1, keepdims=True) + EPS)
        o_ref[...] = (xf * r * w_ref[...]).astype(BF16)

    return pl.pallas_call(
        body, name=name, grid=(T // tm,),
        in_specs=[pl.BlockSpec((tm, D), lambda i: (i, 0)), pl.BlockSpec((1, D), lambda i: (0, 0))],
        out_specs=pl.BlockSpec((tm, D), lambda i: (i, 0)),
        out_shape=jax.ShapeDtypeStruct((T, D), BF16), compiler_params=_params(1),
    )(x, w)


def _rmsnorm_bwd(x, w, du, dres, name):
    T, D = x.shape
    tm = _pick(T, (512, 256, 128))

    def body(x_ref, w_ref, du_ref, dres_ref, dx_ref, dw_ref):
        dx, dw = _rms_bwd_math(x_ref[...], w_ref[...], du_ref[...].astype(F32))
        dx_ref[...] = dres_ref[...] + dx

        @pl.when(pl.program_id(0) == 0)
        def _():
            dw_ref[...] = jnp.zeros_like(dw_ref)

        dw_ref[...] += jnp.broadcast_to(dw, dw_ref.shape)

    return pl.pallas_call(
        body, name=name, grid=(T // tm,),
        in_specs=[pl.BlockSpec((tm, D), lambda i: (i, 0)), pl.BlockSpec((1, D), lambda i: (0, 0)),
                  pl.BlockSpec((tm, D), lambda i: (i, 0)), pl.BlockSpec((tm, D), lambda i: (i, 0))],
        out_specs=[pl.BlockSpec((tm, D), lambda i: (i, 0)), pl.BlockSpec((8, D), lambda i: (0, 0))],
        out_shape=[jax.ShapeDtypeStruct((T, D), F32), jax.ShapeDtypeStruct((8, D), F32)],
        compiler_params=_params(1),
    )(x, w, du, dres)


def _mm_nn(a, b, name, add=None, out_dtype=F32):
    M, K = a.shape
    _, N = b.shape
    tm = _pick(M, (512, 256, 128))
    tn = _pick(N, (512, 256, 128))

    def body(*refs):
        a_ref, b_ref = refs[0], refs[1]
        o_ref = refs[-1]
        acc = _dot(a_ref[...], b_ref[...])
        if add is not None:
            acc = acc + refs[2][...]
        o_ref[...] = acc.astype(out_dtype)

    in_specs = [pl.BlockSpec((tm, K), lambda i, j: (i, 0)), pl.BlockSpec((K, tn), lambda i, j: (0, j))]
    args = [a, b]
    if add is not None:
        in_specs.append(pl.BlockSpec((tm, tn), lambda i, j: (i, j)))
        args.append(add)
    return pl.pallas_call(
        body, name=name, grid=(M // tm, N // tn), in_specs=in_specs,
        out_specs=pl.BlockSpec((tm, tn), lambda i, j: (i, j)),
        out_shape=jax.ShapeDtypeStruct((M, N), out_dtype), compiler_params=_params(2),
    )(*args)


def _mm_nt(a, b, name, out_dtype=F32):
    M, K = a.shape
    N, _ = b.shape
    tm = _pick(M, (512, 256, 128))
    tn = _pick(N, (512, 256, 128))

    def body(a_ref, b_ref, o_ref):
        o_ref[...] = _dot_nt(a_ref[...], b_ref[...]).astype(out_dtype)

    return pl.pallas_call(
        body, name=name, grid=(M // tm, N // tn),
        in_specs=[pl.BlockSpec((tm, K), lambda i, j: (i, 0)), pl.BlockSpec((tn, K), lambda i, j: (j, 0))],
        out_specs=pl.BlockSpec((tm, tn), lambda i, j: (i, j)),
        out_shape=jax.ShapeDtypeStruct((M, N), out_dtype), compiler_params=_params(2),
    )(a, b)


def _mm_tn(a, b, name):
    R, M = a.shape
    _, N = b.shape
    tm = _pick(M, (1024, 512, 256, 128))
    tn = _pick(N, (512, 256, 128))
    tr = _pick(R, (512, 256, 128))
    n_r = R // tr

    def body(a_ref, b_ref, o_ref, acc_ref):
        r = pl.program_id(2)

        @pl.when(r == 0)
        def _():
            acc_ref[...] = jnp.zeros_like(acc_ref)

        acc_ref[...] += _dot_tn(a_ref[...], b_ref[...])

        @pl.when(r == n_r - 1)
        def _():
            o_ref[...] = acc_ref[...]

    return pl.pallas_call(
        body, name=name, grid=(M // tm, N // tn, n_r),
        in_specs=[pl.BlockSpec((tr, tm), lambda i, j, r: (r, i)), pl.BlockSpec((tr, tn), lambda i, j, r: (r, j))],
        out_specs=pl.BlockSpec((tm, tn), lambda i, j, r: (i, j)),
        out_shape=jax.ShapeDtypeStruct((M, N), F32),
        scratch_shapes=[pltpu.VMEM((tm, tn), F32)], compiler_params=_params(3),
    )(a, b)


def _qkv_post(c, j, n_heads):
    s = _silu(c)
    parts = []
    for h in range(n_heads):
        sh = s[:, h * HEAD_DIM:(h + 1) * HEAD_DIM]
        parts.append(sh * lax.rsqrt(jnp.sum(sh * sh, axis=-1, keepdims=True) + EPS))
    n = jnp.concatenate(parts, axis=-1)
    is_q = (j == 0).astype(F32)
    is_v = (j == 2).astype(F32)
    n = n * (1.0 + is_q * (HEAD_DIM ** -0.5 - 1.0))
    return n * (1.0 - is_v) + s * is_v


def _conv_taps(cur, halo_prev):
    tm = cur.shape[0]
    ext = jnp.concatenate([halo_prev, cur], axis=0)
    taps = [pltpu.roll(ext, s, 0)[8:8 + tm] for s in (3, 2, 1)]
    return taps + [cur]


def _conv_fwd(p, conv_w, d_model, name):
    T = p.shape[0]
    D = d_model
    H = D // HEAD_DIM
    tm = _pick(T, (256, 128, 64))

    def body(cur_ref, prev_ref, w_ref, o_ref):
        i, j = pl.program_id(0), pl.program_id(1)
        prev = prev_ref[...] * (i > 0).astype(F32)
        taps = _conv_taps(cur_ref[...], prev)
        w = w_ref[...]
        c = sum(taps[k] * w[k:k + 1, :] for k in range(4))
        o_ref[...] = _qkv_post(c, j, H)

    return pl.pallas_call(
        body, name=name, grid=(T // tm, 3),
        in_specs=[pl.BlockSpec((tm, D), lambda i, j: (i, j)),
                  pl.BlockSpec((8, D), lambda i, j: (jnp.maximum(i * (tm // 8) - 1, 0), j)),
                  pl.BlockSpec((4, D), lambda i, j: (0, j))],
        out_specs=pl.BlockSpec((tm, D), lambda i, j: (i, j)),
        out_shape=jax.ShapeDtypeStruct((T, 3 * D), F32), compiler_params=_params(2),
    )(p, p, conv_w)


def _chunk_tri(tm, upper):
    r, c = _iota2((tm, tm), 0), _iota2((tm, tm), 1)
    same = (r // CHUNK) == (c // CHUNK)
    tri = (c >= r) if upper else (c <= r)
    return jnp.where(same & tri, 1.0, 0.0).astype(BF16)


def _dot_mask(mask_bf16, x):
    hi, mid, lo = _split3(x)
    d = functools.partial(jnp.dot, preferred_element_type=F32)
    return d(mask_bf16, hi) + (d(mask_bf16, mid) + d(mask_bf16, lo))


def _gates_math(pb, pa, a_log, dt_bias):
    beta = jax.nn.sigmoid(pb)
    g = -jnp.exp(a_log) * _softplus(pa + dt_bias)
    return beta, g


def _gates_fwd(p, gate_params, col0, name):
    T = p.shape[0]
    tm = _pick(T, (256, 128, 64))

    def body(pb_ref, pa_ref, gp_ref, beta_ref, gc_ref):
        gp = gp_ref[...]
        beta, g = _gates_math(pb_ref[...], pa_ref[...], gp[0:1, :], gp[1:2, :])
        beta_ref[...] = beta
        gc_ref[...] = _dot_mask(_chunk_tri(tm, upper=False), g)

    return pl.pallas_call(
        body, name=name, grid=(T // tm,),
        in_specs=[pl.BlockSpec((tm, LANES), lambda i: (i, col0)), pl.BlockSpec((tm, LANES), lambda i: (i, col0 + 1)),
                  pl.BlockSpec((8, LANES), lambda i: (0, 0))],
        out_specs=[pl.BlockSpec((tm, LANES), lambda i: (i, 0))] * 2,
        out_shape=[jax.ShapeDtypeStruct((T, LANES), F32)] * 2, compiler_params=_params(1),
    )(p, p, gate_params)


def _col_to_row(col):
    C = col.shape[0]
    eye = _iota2((C, C), 0) == _iota2((C, C), 1)
    return jnp.sum(jnp.where(eye, col, 0.0), axis=0, keepdims=True)


def _lockstep(generators):
    generators = list(generators)
    results = [None] * len(generators)
    live = list(range(len(generators)))
    while live:
        for idx in list(live):
            try:
                next(generators[idx])
            except StopIteration as done:
                results[idx] = done.value
                live.remove(idx)
    return results


def _unit_lower_inverse(low):
    C = low.shape[0]
    eye = (_iota2((C, C), 0) == _iota2((C, C), 1)).astype(F32)
    t = eye - low
    p = _dot3(low, low)
    yield
    n = 2
    while True:
        tp = _dot3(t, p)
        n *= 2
        if n < C:
            p = _dot3(p, p)
        yield
        t = t + tp
        if n >= C:
            return t


def _chunk_head_fwd(q, k, v, gc, beta, s_in):
    C = q.shape[0]
    r, c = _iota2((C, C), 0), _iota2((C, C), 1)
    causal, strict = r >= c, r > c
    decay = jnp.where(causal, jnp.exp(jnp.where(causal, gc - _col_to_row(gc), 0.0)), 0.0)
    kb, vb = k * beta, v * beta
    eg = jnp.exp(gc)
    kk = _dot_nt(kb, k)
    qk = _dot_nt(q, k)
    o_state = _dot(q * eg, s_in)
    yield
    t_inv = yield from _unit_lower_inverse(jnp.where(strict, kk * decay, 0.0))
    u = _dot(t_inv, vb)
    w = _dot(t_inv, kb * eg)
    yield
    w_state = _dot(w, s_in)
    yield
    v_new = u - w_state
    g_last = gc[C - 1:C, :]
    o_intra = _dot(qk * decay, v_new)
    s_add = _dot_tn(k * jnp.exp(g_last - gc), v_new)
    yield
    return o_state + o_intra, s_in * jnp.exp(g_last) + s_add, t_inv


def _chunk_fwd(qkv, beta, gc, d_model, name):
    T = qkv.shape[0]
    D = d_model
    H = D // HEAD_DIM
    N = T // CHUNK

    def body(q_ref, k_ref, v_ref, beta_ref, gc_ref, o_ref, s_all_ref, t_all_ref, s_ref):
        @pl.when(pl.program_id(0) == 0)
        def _():
            s_ref[...] = jnp.zeros_like(s_ref)

        heads = [slice(h * HEAD_DIM, (h + 1) * HEAD_DIM) for h in range(H)]
        s_all_ref[0] = s_ref[...]
        results = _lockstep(_chunk_head_fwd(q_ref[:, hs], k_ref[:, hs], v_ref[:, hs], gc_ref[:, h:h + 1],
                                            beta_ref[:, h:h + 1], s_ref[h]) for h, hs in enumerate(heads))
        for h, (o, s_out, t_inv) in enumerate(results):
            o_ref[:, heads[h]] = o
            s_ref[h] = s_out
            t_all_ref[0, h] = t_inv

    return pl.pallas_call(
        body, name=name, grid=(N,),
        in_specs=[pl.BlockSpec((CHUNK, D), lambda n: (n, 0)), pl.BlockSpec((CHUNK, D), lambda n: (n, 1)),
                  pl.BlockSpec((CHUNK, D), lambda n: (n, 2)),
                  pl.BlockSpec((CHUNK, LANES), lambda n: (n, 0)), pl.BlockSpec((CHUNK, LANES), lambda n: (n, 0))],
        out_specs=[pl.BlockSpec((CHUNK, D), lambda n: (n, 0)),
                   pl.BlockSpec((1, H, HEAD_DIM, HEAD_DIM), lambda n: (n, 0, 0, 0)),
                   pl.BlockSpec((1, H, CHUNK, CHUNK), lambda n: (n, 0, 0, 0))],
        out_shape=[jax.ShapeDtypeStruct((T, D), F32), jax.ShapeDtypeStruct((N, H, HEAD_DIM, HEAD_DIM), F32),
                   jax.ShapeDtypeStruct((N, H, CHUNK, CHUNK), F32)],
        scratch_shapes=[pltpu.VMEM((H, HEAD_DIM, HEAD_DIM), F32)], compiler_params=_params(1),
    )(qkv, qkv, qkv, beta, gc)


def _onorm_gate_math(o, z, w, n_heads):
    parts = []
    for h in range(n_heads):
        hs = slice(h * HEAD_DIM, (h + 1) * HEAD_DIM)
        oh = o[:, hs]
        y = oh * lax.rsqrt(jnp.mean(oh * oh, axis=-1, keepdims=True) + EPS) * w
        parts.append(y * _silu(z[:, hs]))
    return jnp.concatenate(parts, axis=-1)


def _onorm_gate_fwd(o, p, z_col, o_norm, name):
    T, D = o.shape
    H = D // HEAD_DIM
    tm = _pick(T, (256, 128, 64))

    def body(o_ref, z_ref, w_ref, y_ref):
        y_ref[...] = _onorm_gate_math(o_ref[...], z_ref[...], w_ref[...], H).astype(BF16)

    return pl.pallas_call(
        body, name=name, grid=(T // tm,),
        in_specs=[pl.BlockSpec((tm, D), lambda i: (i, 0)), pl.BlockSpec((tm, D), lambda i: (i, z_col)),
                  pl.BlockSpec((1, HEAD_DIM), lambda i: (0, 0))],
        out_specs=pl.BlockSpec((tm, D), lambda i: (i, 0)),
        out_shape=jax.ShapeDtypeStruct((T, D), BF16), compiler_params=_params(1),
    )(o, p, o_norm)


def _diag_mask(qb, kb, d):
    return _iota2((qb, kb), 0) > _iota2((qb, kb), 1) + d * kb


def _fill_score_masks(mask_buf, qb, kb, ns):
    mask_buf[0] = jnp.zeros(mask_buf.shape[1:], F32)
    for d in range(ns):
        for half in range(2):
            mask_buf[d + 1, :, half * kb:(half + 1) * kb] = jnp.where(_diag_mask(qb, kb, 2 * d + half), 0.0, MASKED_SCORE)


def _softplus_bits(w):
    u = 1.0 + jnp.exp2(jnp.minimum(w, 64.0))
    return jnp.maximum(w, jnp.log2(u)), 1.0 / u


def _incl_lower(n):
    return jnp.where((_iota2((2 * n, n), 0) & (n - 1)) >= _iota2((2 * n, n), 1), 1.0, 0.0).astype(BF16)


def _incl_upper(n):
    return jnp.where((_iota2((2 * n, n), 0) & (n - 1)) <= _iota2((2 * n, n), 1), 1.0, 0.0).astype(BF16)


def _dot_cum(x, tri_bf16):
    hi, lo = _split2(x)
    return jnp.dot(jnp.concatenate([hi, lo], axis=1), tri_bf16, preferred_element_type=F32)


def _sb_fwd(qkv, d_model, name):
    T = qkv.shape[0]
    D = d_model
    H = D // HEAD_DIM
    QB = _pick(T, ATTN_Q_BLOCKS)
    KB = ATTN_K_BLOCK
    KS = 2 * KB
    ns = QB // KS
    nq = T // QB
    scale = HEAD_DIM ** -0.5

    def body(q_ref, k_ref, v_ref, o_ref, r_ref, w_buf, cum_buf, mask_buf):
        i = pl.program_id(1)

        @pl.when(i == 0)
        def _():
            _fill_score_masks(mask_buf, QB, KB, ns)

        q = q_ref[...]
        tri = _incl_lower(KB)
        n_tot = (i + 1) * ns

        def key_step(m):
            return jnp.maximum(n_tot - 1 - m, 0)

        def rows(ref, s):
            return ref[pl.ds(pl.multiple_of(s * KS, KS), KS), :]

        def scores(s):
            return _dot_nt(q, rows(k_ref, s)) * (scale * LOG2_E) + mask_buf[jnp.maximum(s - i * ns + 1, 0)]

        def cums(w):
            sp = _softplus_bits(w)[0]
            return jnp.concatenate([_dot_cum(sp[:, :KB], tri), _dot_cum(sp[:, KB:], tri)], axis=1)

        def weights(w, cum, carry):
            a_r = jnp.exp2(w[:, KB:] - cum[:, KB:] - carry)
            carry = carry + cum[:, KB:KB + 1]
            a_l = jnp.exp2(w[:, :KB] - cum[:, :KB] - carry)
            return jnp.concatenate([a_l, a_r], axis=1).astype(BF16), carry + cum[:, 0:1]

        def trip(m, carry):
            w_new = scores(key_step(m + 2))
            a, carry = weights(w_buf[m % 3], cum_buf[m % 2], carry)
            o_ref[...] += _dot(a, rows(v_ref, key_step(m)))
            cum_buf[(m + 1) % 2] = cums(w_buf[(m + 1) % 3])
            w_buf[(m + 2) % 3] = w_new
            return carry

        o_ref[...] = jnp.zeros_like(o_ref)
        w_buf[0] = scores(key_step(0))
        w_buf[1] = scores(key_step(1))
        cum_buf[0] = cums(w_buf[0])
        carry = lax.fori_loop(0, n_tot, trip, jnp.zeros((QB, 1), F32))
        r_ref[0] = jnp.broadcast_to(carry, (QB, LANES))

    return pl.pallas_call(
        body, name=name, grid=(H, nq),
        in_specs=[pl.BlockSpec((QB, HEAD_DIM), lambda h, i: (i, h)),
                  pl.BlockSpec((T, HEAD_DIM), lambda h, i: (0, H + h)),
                  pl.BlockSpec((T, HEAD_DIM), lambda h, i: (0, 2 * H + h))],
        out_specs=[pl.BlockSpec((QB, HEAD_DIM), lambda h, i: (i, h)),
                   pl.BlockSpec((1, QB, LANES), lambda h, i: (h, i, 0))],
        out_shape=[jax.ShapeDtypeStruct((T, D), F32), jax.ShapeDtypeStruct((H, T, LANES), F32)],
        scratch_shapes=[pltpu.VMEM((3, QB, KS), F32), pltpu.VMEM((2, QB, KS), F32), pltpu.VMEM((ns + 1, QB, KS), F32)],
        compiler_params=_params(2),
    )(qkv, qkv, qkv)


def _gate_mul_fwd(o, gate, name):
    T, D = o.shape
    tm = _pick(T, (512, 256, 128))

    def body(o_ref, g_ref, y_ref):
        y_ref[...] = (o_ref[...] * _silu(g_ref[...])).astype(BF16)

    spec = pl.BlockSpec((tm, D), lambda i: (i, 0))
    return pl.pallas_call(body, name=name, grid=(T // tm,), in_specs=[spec, spec], out_specs=spec,
                          out_shape=jax.ShapeDtypeStruct((T, D), BF16), compiler_params=_params(1))(o, gate)


def _final_loss(h, w, target, name):
    T, D = h.shape
    tm = _pick(T, (512, 256, 128))

    def body(h_ref, w_ref, t_ref, dh_ref, loss_ref, dw_ref):
        x, w = h_ref[...], w_ref[...]
        r = lax.rsqrt(jnp.mean(x * x, axis=-1, keepdims=True) + EPS)
        err = x * r * w - t_ref[...]
        part = 0.5 * jnp.sum(jnp.mean(err * err, axis=-1, keepdims=True), axis=0, keepdims=True)
        dx, dw = _rms_bwd_math(x, w, err * (1.0 / D))
        dh_ref[...] = dx

        @pl.when(pl.program_id(0) == 0)
        def _():
            loss_ref[...] = jnp.zeros_like(loss_ref)
            dw_ref[...] = jnp.zeros_like(dw_ref)

        loss_ref[...] += jnp.broadcast_to(part, loss_ref.shape)
        dw_ref[...] += jnp.broadcast_to(dw, dw_ref.shape)

    return pl.pallas_call(
        body, name=name, grid=(T // tm,),
        in_specs=[pl.BlockSpec((tm, D), lambda i: (i, 0)), pl.BlockSpec((1, D), lambda i: (0, 0)),
                  pl.BlockSpec((tm, D), lambda i: (i, 0))],
        out_specs=[pl.BlockSpec((tm, D), lambda i: (i, 0)), pl.BlockSpec((8, LANES), lambda i: (0, 0)),
                   pl.BlockSpec((8, D), lambda i: (0, 0))],
        out_shape=[jax.ShapeDtypeStruct((T, D), F32), jax.ShapeDtypeStruct((8, LANES), F32),
                   jax.ShapeDtypeStruct((8, D), F32)],
        compiler_params=_params(1),
    )(h, w, target)


def _gate_mul_bwd(dy, o, gate, name):
    T, D = o.shape
    tm = _pick(T, (512, 256, 128))

    def body(dy_ref, o_ref, g_ref, do_ref, dg_ref):
        dy, g = dy_ref[...], g_ref[...]
        s = jax.nn.sigmoid(g)
        do_ref[...] = dy * (g * s)
        dg_ref[...] = (dy * o_ref[...] * (s + g * s * (1.0 - s))).astype(BF16)

    spec = pl.BlockSpec((tm, D), lambda i: (i, 0))
    return pl.pallas_call(body, name=name, grid=(T // tm,), in_specs=[spec] * 3, out_specs=[spec] * 2,
                          out_shape=[jax.ShapeDtypeStruct((T, D), F32), jax.ShapeDtypeStruct((T, D), BF16)],
                          compiler_params=_params(1))(dy, o, gate)


def _sb_bwd(qkv, do, r_tot, d_model, name):
    T = qkv.shape[0]
    D = d_model
    H = D // HEAD_DIM
    QB = _pick(T, ATTN_Q_BLOCKS)
    KB = ATTN_K_BLOCK
    KS = 2 * KB
    ns = QB // KS
    nq = T // QB
    n_key_steps = T // KS
    scale = HEAD_DIM ** -0.5

    def body(q_ref, k_ref, v_ref, do_ref, r_ref, dq_ref, dk_ref, dv_ref,
             dkt_acc, dvt_acc, dq_acc, w_buf, da_buf, cum_buf, sig_buf, mask_buf):
        i = pl.program_id(1)

        @pl.when(i == 0)
        def _():
            dkt_acc[...] = jnp.zeros_like(dkt_acc)
            dvt_acc[...] = jnp.zeros_like(dvt_acc)
            _fill_score_masks(mask_buf, QB, KB, ns)

        q = q_ref[...]
        do_blk = do_ref[...].astype(BF16)
        q_t = q.astype(F32).T.astype(BF16)
        do_t = do_ref[...].T.astype(BF16)
        row_total = r_ref[0][:, 0:1]
        tri_rev = _incl_lower(KB)
        tri_fwd = jnp.where(_iota2((KB, KB), 0) <= _iota2((KB, KB), 1), 1.0, 0.0).astype(BF16)
        n_tot = (i + 1) * ns

        def step_rows(ref, s):
            return ref[pl.ds(pl.multiple_of(s * KS, KS), KS), :]

        def scores(s):
            w = _dot_nt(q, step_rows(k_ref, s)) * (scale * LOG2_E) + mask_buf[jnp.maximum(s - i * ns + 1, 0)]
            return w, _dot_nt(do_blk, step_rows(v_ref, s))

        def softplus_sums(w):
            sp, one_minus_sig = _softplus_bits(w)
            cum = jnp.concatenate([_dot_cum(sp[:, :KB], tri_rev), _dot_cum(sp[:, KB:], tri_rev)], axis=1)
            return cum, 1.0 - one_minus_sig

        def weights(w, cum, da, left_sp):
            right_l = row_total - left_sp - cum[:, 0:1]
            right_r = right_l - cum[:, KB:KB + 1]
            a = jnp.concatenate([jnp.exp2(w[:, :KB] - cum[:, :KB] - right_l),
                                 jnp.exp2(w[:, KB:] - cum[:, KB:] - right_r)], axis=1)
            p = da * a
            cp = jnp.concatenate([_dot(p[:, :KB], tri_fwd), _dot(p[:, KB:], tri_fwd)], axis=1)
            return a.astype(BF16), p, cp, row_total - right_r

        def score_grads(p, cp, sig, left_p):
            cum_l = cp[:, :KB] + left_p
            cum_r = cp[:, KB:] + cum_l[:, KB - 1:KB]
            dz = p - sig * jnp.concatenate([cum_l, cum_r], axis=1)
            return dz.astype(BF16), cum_r[:, KB - 1:KB]

        def trip(m, st):
            left_sp, left_p = st
            s2 = jnp.minimum(m + 2, n_tot - 1)
            s1 = jnp.minimum(m + 1, n_tot - 1)
            w_new, da_new = scores(s2)
            a, p, cp, left_sp = weights(w_buf[m % 3], cum_buf[m % 2], da_buf[m % 3], left_sp)
            cum_new, sig_new = softplus_sums(w_buf[s1 % 3])
            dz, left_p = score_grads(p, cp, sig_buf[m % 2], left_p)
            dq_acc[...] += _dot(dz, step_rows(k_ref, m))
            dkt_acc[m] += jnp.dot(q_t, dz, preferred_element_type=F32) * scale
            dvt_acc[m] += jnp.dot(do_t, a, preferred_element_type=F32)
            cum_buf[(m + 1) % 2] = cum_new
            sig_buf[(m + 1) % 2] = sig_new
            w_buf[(m + 2) % 3] = w_new
            da_buf[(m + 2) % 3] = da_new
            return left_sp, left_p

        dq_acc[...] = jnp.zeros_like(dq_acc)
        w_buf[0], da_buf[0] = scores(0)
        w_buf[1], da_buf[1] = scores(jnp.minimum(1, n_tot - 1))
        cum_buf[0], sig_buf[0] = softplus_sums(w_buf[0])
        zero_col = jnp.zeros((QB, 1), F32)
        lax.fori_loop(0, n_tot, trip, (zero_col, zero_col))
        dq_ref[...] = (dq_acc[...] * scale).astype(BF16)

        @pl.when(i == nq - 1)
        def _():
            for s in range(n_key_steps):
                dk_ref[s * KS:(s + 1) * KS, :] = dkt_acc[s].T.astype(BF16)
                dv_ref[s * KS:(s + 1) * KS, :] = dvt_acc[s].T.astype(BF16)

    return pl.pallas_call(
        body, name=name, grid=(H, nq),
        in_specs=[pl.BlockSpec((QB, HEAD_DIM), lambda h, i: (i, h)),
                  pl.BlockSpec((T, HEAD_DIM), lambda h, i: (0, H + h)),
                  pl.BlockSpec((T, HEAD_DIM), lambda h, i: (0, 2 * H + h)),
                  pl.BlockSpec((QB, HEAD_DIM), lambda h, i: (i, h)),
                  pl.BlockSpec((1, QB, LANES), lambda h, i: (h, i, 0))],
        out_specs=[pl.BlockSpec((QB, HEAD_DIM), lambda h, i: (i, h)),
                   pl.BlockSpec((T, HEAD_DIM), lambda h, i: (0, h)),
                   pl.BlockSpec((T, HEAD_DIM), lambda h, i: (0, h))],
        out_shape=[jax.ShapeDtypeStruct((T, D), BF16)] * 3,
        scratch_shapes=[pltpu.VMEM((n_key_steps, HEAD_DIM, KS), F32), pltpu.VMEM((n_key_steps, HEAD_DIM, KS), F32),
                        pltpu.VMEM((QB, HEAD_DIM), F32), pltpu.VMEM((3, QB, KS), F32), pltpu.VMEM((3, QB, KS), F32),
                        pltpu.VMEM((2, QB, KS), F32), pltpu.VMEM((2, QB, KS), F32), pltpu.VMEM((ns + 1, QB, KS), F32)],
        compiler_params=_params(2),
    )(qkv, qkv, qkv, do, r_tot)


def _onorm_gate_bwd(dy, o, p, z_col, o_norm, name):
    T, D = o.shape
    H = D // HEAD_DIM
    tm = _pick(T, (256, 128, 64))

    def body(dy_ref, o_ref, z_ref, w_ref, do_ref, dz_ref, dw_ref):
        _, vjp = jax.vjp(functools.partial(_onorm_gate_math, n_heads=H), o_ref[...], z_ref[...], w_ref[...])
        do, dz, dw = vjp(dy_ref[...])
        do_ref[...] = do
        dz_ref[...] = dz.astype(BF16)

        @pl.when(pl.program_id(0) == 0)
        def _():
            dw_ref[...] = jnp.zeros_like(dw_ref)

        dw_ref[...] += jnp.broadcast_to(dw, dw_ref.shape)

    return pl.pallas_call(
        body, name=name, grid=(T // tm,),
        in_specs=[pl.BlockSpec((tm, D), lambda i: (i, 0)), pl.BlockSpec((tm, D), lambda i: (i, 0)),
                  pl.BlockSpec((tm, D), lambda i: (i, z_col)), pl.BlockSpec((1, HEAD_DIM), lambda i: (0, 0))],
        out_specs=[pl.BlockSpec((tm, D), lambda i: (i, 0)), pl.BlockSpec((tm, D), lambda i: (i, 0)),
                   pl.BlockSpec((8, HEAD_DIM), lambda i: (0, 0))],
        out_shape=[jax.ShapeDtypeStruct((T, D), F32), jax.ShapeDtypeStruct((T, D), BF16),
                   jax.ShapeDtypeStruct((8, HEAD_DIM), F32)],
        compiler_params=_params(1),
    )(dy, o, p, o_norm)


def _row_to_col(row):
    C = row.shape[1]
    eye = _iota2((C, C), 0) == _iota2((C, C), 1)
    return jnp.sum(jnp.where(eye, row, 0.0), axis=1, keepdims=True)


def _lane_sum(x):
    return jnp.sum(x, axis=-1, keepdims=True)


def _chunk_head_bwd(q, k, v, gc, beta, s_in, t_inv, do, ds_out):
    C = q.shape[0]
    r, c = _iota2((C, C), 0), _iota2((C, C), 1)
    causal, strict = r >= c, r > c
    decay = jnp.where(causal, jnp.exp(jnp.where(causal, gc - _col_to_row(gc), 0.0)), 0.0)
    kb, vb = k * beta, v * beta
    eg = jnp.exp(gc)
    kbg = kb * eg
    g_last = gc[C - 1:C, :]
    e_tail = jnp.exp(g_last - gc)
    k_tail = k * e_tail
    gl = jnp.exp(g_last)
    qg = q * eg
    t_inv_t = t_inv.T
    kk = _dot_nt(kb, k)
    u = _dot(t_inv, vb)
    w = _dot(t_inv, kbg)
    qk = _dot_nt(q, k)
    d_qg = _dot_nt(do, s_in)
    ds_state = _dot_tn(qg, do)
    yield
    low = jnp.where(strict, kk * decay, 0.0)
    attn = qk * decay
    w_state = _dot(w, s_in)
    d_vnew_intra = _dot_tn(attn, do)
    d_vnew_state = _dot(k_tail, ds_out)
    yield
    v_new = u - w_state
    d_vnew = d_vnew_intra + d_vnew_state
    d_ktail = _dot_nt(v_new, ds_out)
    d_attn_raw = _dot_nt(do, v_new)
    d_w = -_dot_nt(d_vnew, s_in)
    ds_w = _dot_tn(w, d_vnew)
    d_vb = _dot(t_inv_t, d_vnew)
    d_tinv_u = _dot_nt(d_vnew, vb)
    yield
    d_gl = jnp.sum(_lane_sum(s_in * ds_out), axis=0, keepdims=True)
    d_attn = jnp.where(causal, d_attn_raw, 0.0)
    ds_in = ds_out * gl + ds_state - ds_w
    d_kbg = _dot(t_inv_t, d_w)
    d_tinv_w = _dot_nt(d_w, kbg)
    d_qk = d_attn * decay
    dq_intra = _dot(d_qk, k)
    dk_intra = _dot_tn(d_qk, q)
    yield
    inner = _dot(t_inv_t, d_tinv_u + d_tinv_w)
    yield
    d_low_raw = _dot_nt(inner, t_inv)
    yield
    d_low = jnp.where(strict, -d_low_raw, 0.0)
    d_kk = d_low * decay
    d_kb_low = _dot(d_kk, k)
    dk_low = _dot_tn(d_kk, kb)
    yield
    d_kb = d_kb_low + d_kbg * eg
    dq = dq_intra + d_qg * eg
    dk = dk_low + dk_intra + d_ktail * e_tail + d_kb * beta
    dv = d_vb * beta
    dbeta = _lane_sum(d_kb * k) + _lane_sum(d_vb * v)
    m = d_low * low + d_attn * attn
    tail_term = _lane_sum(d_ktail * k_tail)
    d_g_last = d_gl * gl + jnp.sum(tail_term, axis=0, keepdims=True)
    dgc = (_lane_sum(m) - _row_to_col(jnp.sum(m, axis=0, keepdims=True))
           + _lane_sum(d_qg * qg) + _lane_sum(d_kbg * kbg) - tail_term)
    dgc = dgc + jnp.where(_iota2((C, 1), 0) == C - 1, d_g_last, 0.0)
    return dq, dk, dv, dgc, dbeta, ds_in


def _chunk_bwd(qkv, beta, gc, s_all, t_all, do, d_model, name):
    T = qkv.shape[0]
    D = d_model
    H = D // HEAD_DIM
    N = T // CHUNK

    def body(q_ref, k_ref, v_ref, beta_ref, gc_ref, s_ref, t_ref, do_ref, dqkv_ref, dbeta_ref, dg_ref, ds_ref):
        @pl.when(pl.program_id(0) == 0)
        def _():
            ds_ref[...] = jnp.zeros_like(ds_ref)

        lane = _iota2((CHUNK, LANES), 1)
        dgc_all = jnp.zeros((CHUNK, LANES), F32)
        dbeta_all = jnp.zeros((CHUNK, LANES), F32)
        results = _lockstep(
            _chunk_head_bwd(q_ref[:, hs], k_ref[:, hs], v_ref[:, hs], gc_ref[:, h:h + 1], beta_ref[:, h:h + 1],
                            s_ref[0, h], t_ref[0, h], do_ref[:, hs], ds_ref[h])
            for h, hs in enumerate(slice(h * HEAD_DIM, (h + 1) * HEAD_DIM) for h in range(H)))
        for h, (dq, dk, dv, dgc, dbeta, ds_in) in enumerate(results):
            ds_ref[h] = ds_in
            dqkv_ref[:, h * HEAD_DIM:(h + 1) * HEAD_DIM] = dq
            dqkv_ref[:, D + h * HEAD_DIM:D + (h + 1) * HEAD_DIM] = dk
            dqkv_ref[:, 2 * D + h * HEAD_DIM:2 * D + (h + 1) * HEAD_DIM] = dv
            dgc_all = jnp.where(lane == h, dgc, dgc_all)
            dbeta_all = jnp.where(lane == h, dbeta, dbeta_all)
        dbeta_ref[...] = dbeta_all
        dg_ref[...] = _dot_mask(_chunk_tri(CHUNK, upper=True), dgc_all)

    rev = lambda n: N - 1 - n
    return pl.pallas_call(
        body, name=name, grid=(N,),
        in_specs=[pl.BlockSpec((CHUNK, D), lambda n: (rev(n), 0)), pl.BlockSpec((CHUNK, D), lambda n: (rev(n), 1)),
                  pl.BlockSpec((CHUNK, D), lambda n: (rev(n), 2)),
                  pl.BlockSpec((CHUNK, LANES), lambda n: (rev(n), 0)), pl.BlockSpec((CHUNK, LANES), lambda n: (rev(n), 0)),
                  pl.BlockSpec((1, H, HEAD_DIM, HEAD_DIM), lambda n: (rev(n), 0, 0, 0)),
                  pl.BlockSpec((1, H, CHUNK, CHUNK), lambda n: (rev(n), 0, 0, 0)),
                  pl.BlockSpec((CHUNK, D), lambda n: (rev(n), 0))],
        out_specs=[pl.BlockSpec((CHUNK, 3 * D), lambda n: (rev(n), 0)),
                   pl.BlockSpec((CHUNK, LANES), lambda n: (rev(n), 0)), pl.BlockSpec((CHUNK, LANES), lambda n: (rev(n), 0))],
        out_shape=[jax.ShapeDtypeStruct((T, 3 * D), F32), jax.ShapeDtypeStruct((T, LANES), F32),
                   jax.ShapeDtypeStruct((T, LANES), F32)],
        scratch_shapes=[pltpu.VMEM((H, HEAD_DIM, HEAD_DIM), F32)], compiler_params=_params(1),
    )(qkv, qkv, qkv, beta, gc, s_all, t_all, do)


def _gates_bwd(p, gate_params, col0, dbeta, dg, name):
    T = p.shape[0]
    tm = _pick(T, (256, 128, 64))

    def body(pb_ref, pa_ref, gp_ref, dbeta_ref, dg_ref, dp_ref, dgp_ref):
        gp = gp_ref[...]
        _, vjp = jax.vjp(_gates_math, pb_ref[...], pa_ref[...], gp[0:1, :], gp[1:2, :])
        dpb, dpa, d_alog, d_dt = vjp((dbeta_ref[...], dg_ref[...]))
        dp_ref[:, 0:LANES] = dpb.astype(BF16)
        dp_ref[:, LANES:2 * LANES] = dpa.astype(BF16)

        @pl.when(pl.program_id(0) == 0)
        def _():
            dgp_ref[...] = jnp.zeros_like(dgp_ref)

        dgp_ref[0:1, :] += d_alog
        dgp_ref[1:2, :] += d_dt

    return pl.pallas_call(
        body, name=name, grid=(T // tm,),
        in_specs=[pl.BlockSpec((tm, LANES), lambda i: (i, col0)), pl.BlockSpec((tm, LANES), lambda i: (i, col0 + 1)),
                  pl.BlockSpec((8, LANES), lambda i: (0, 0)),
                  pl.BlockSpec((tm, LANES), lambda i: (i, 0)), pl.BlockSpec((tm, LANES), lambda i: (i, 0))],
        out_specs=[pl.BlockSpec((tm, 2 * LANES), lambda i: (i, 0)), pl.BlockSpec((8, LANES), lambda i: (0, 0))],
        out_shape=[jax.ShapeDtypeStruct((T, 2 * LANES), BF16), jax.ShapeDtypeStruct((8, LANES), F32)],
        compiler_params=_params(1),
    )(p, p, gate_params, dbeta, dg)


def _conv_bwd_act(p, conv_w, dqkv, d_model, name):
    T = p.shape[0]
    D = d_model
    H = D // HEAD_DIM
    tm = _pick(T, (256, 128, 64))

    def body(cur_ref, prev_ref, w_ref, dout_ref, dc_ref, dw_ref):
        j, i = pl.program_id(0), pl.program_id(1)
        prev = prev_ref[...] * (i > 0).astype(F32)
        taps = _conv_taps(cur_ref[...], prev)
        w = w_ref[...]
        c = sum(taps[k] * w[k:k + 1, :] for k in range(4))
        _, vjp = jax.vjp(lambda cc: _qkv_post(cc, j, H), c)
        (dc,) = vjp(dout_ref[...])
        dc_ref[...] = dc

        @pl.when(i == 0)
        def _():
            dw_ref[...] = jnp.zeros_like(dw_ref)

        for k in range(4):
            dw_ref[k:k + 1, :] += jnp.sum(dc * taps[k], axis=0, keepdims=True)

    return pl.pallas_call(
        body, name=name, grid=(3, T // tm),
        in_specs=[pl.BlockSpec((tm, D), lambda j, i: (i, j)),
                  pl.BlockSpec((8, D), lambda j, i: (jnp.maximum(i * (tm // 8) - 1, 0), j)),
                  pl.BlockSpec((4, D), lambda j, i: (0, j)),
                  pl.BlockSpec((tm, D), lambda j, i: (i, j))],
        out_specs=[pl.BlockSpec((tm, D), lambda j, i: (i, j)), pl.BlockSpec((4, D), lambda j, i: (0, j))],
        out_shape=[jax.ShapeDtypeStruct((T, 3 * D), F32), jax.ShapeDtypeStruct((4, 3 * D), F32)],
        compiler_params=_params(2),
    )(p, p, conv_w, dqkv)


def _conv_bwd_input(dc, conv_w, name):
    T, D3 = dc.shape
    D = D3 // 3
    tm = _pick(T, (256, 128, 64))
    n_t = T // tm

    def body(cur_ref, next_ref, w_ref, dp_ref):
        i = pl.program_id(0)
        cur = cur_ref[...]
        nxt = next_ref[...] * (i < n_t - 1).astype(F32)
        ext = jnp.concatenate([cur, nxt], axis=0)
        w = w_ref[...]
        acc = cur * w[3:4, :]
        for s in (1, 2, 3):
            acc = acc + pltpu.roll(ext, tm + 8 - s, 0)[0:tm] * w[3 - s:4 - s, :]
        dp_ref[...] = acc.astype(BF16)

    return pl.pallas_call(
        body, name=name, grid=(n_t, 3),
        in_specs=[pl.BlockSpec((tm, D), lambda i, j: (i, j)),
                  pl.BlockSpec((8, D), lambda i, j: (jnp.minimum((i + 1) * (tm // 8), T // 8 - 1), j)),
                  pl.BlockSpec((4, D), lambda i, j: (0, j))],
        out_specs=pl.BlockSpec((tm, D), lambda i, j: (i, j)),
        out_shape=jax.ShapeDtypeStruct((T, D3), BF16), compiler_params=_params(2),
    )(dc, dc, conv_w)


def _exchange(arrays, gather, name):
    n = len(arrays)

    def body(*refs):
        ins, outs = refs[:n], refs[n:2 * n]
        send_sems, recv_sems, local_sems = refs[2 * n:]
        x, y, c = lax.axis_index("x"), lax.axis_index("y"), lax.axis_index("c")
        me = 4 * x + 2 * y + c

        def peer(k):
            px = 1 - x if k & 4 else x
            py = 1 - y if k & 2 else y
            pc = 1 - c if k & 1 else c
            return (px, py, pc), 4 * px + 2 * py + pc

        def remote(a, k):
            dev, idx = peer(k)
            src = ins[a] if gather else ins[a].at[idx]
            return pltpu.make_async_remote_copy(
                src_ref=src, dst_ref=outs[a].at[me], send_sem=send_sems.at[a * 7 + k - 1],
                recv_sem=recv_sems.at[a * 7 + k - 1], device_id=dev, device_id_type=MESH_ID)

        def arrival(a, k):
            dev, idx = peer(k)
            src = ins[a] if gather else ins[a].at[idx]
            return pltpu.make_async_remote_copy(
                src_ref=src, dst_ref=outs[a].at[idx], send_sem=send_sems.at[a * 7 + k - 1],
                recv_sem=recv_sems.at[a * 7 + k - 1], device_id=dev, device_id_type=MESH_ID)

        local = [pltpu.make_async_copy(ins[a] if gather else ins[a].at[me], outs[a].at[me], local_sems.at[a])
                 for a in range(n)]
        sends = [remote(a, k) for k in range(1, 8) for a in range(n)]
        for cp in local + sends:
            cp.start()
        for k in range(1, 8):
            for a in range(n):
                arrival(a, k).wait_recv()
        for cp in sends:
            cp.wait_send()
        for cp in local:
            cp.wait()

    out_shape = [jax.ShapeDtypeStruct((N_DEV,) + a.shape if gather else a.shape, a.dtype) for a in arrays]
    any_spec = pl.BlockSpec(memory_space=pl.ANY)
    return pl.pallas_call(
        body, name=name, in_specs=[any_spec] * n, out_specs=[any_spec] * n, out_shape=out_shape,
        scratch_shapes=[pltpu.SemaphoreType.DMA((7 * n,)), pltpu.SemaphoreType.DMA((7 * n,)), pltpu.SemaphoreType.DMA((n,))],
        compiler_params=pltpu.CompilerParams(has_side_effects=True),
    )(*arrays)


def _reduce_adamw(recv, w, m, v, name):
    _, R, C = recv.shape
    tr = next((t for t in (256, 128, 64, 32, 16, 8) if R % t == 0), R)
    c1 = 1.0 - ADAM_B1 ** ADAM_STEP
    c2 = 1.0 - ADAM_B2 ** ADAM_STEP

    def body(r_ref, w_ref, m_ref, v_ref, g_ref, d_ref, nm_ref, nv_ref):
        g = r_ref[0]
        for s in range(1, N_DEV):
            g = g + r_ref[s]
        nm = ADAM_B1 * m_ref[...] + (1.0 - ADAM_B1) * g
        nv = ADAM_B2 * v_ref[...] + (1.0 - ADAM_B2) * (g * g)
        g_ref[...] = g
        nm_ref[...] = nm
        nv_ref[...] = nv
        d_ref[...] = -ADAM_LR * ((nm / c1) / (jnp.sqrt(nv / c2) + ADAM_EPS) + ADAM_WD * w_ref[...])

    spec = pl.BlockSpec((tr, C), lambda i: (i, 0))
    return pl.pallas_call(
        body, name=name, grid=(R // tr,),
        in_specs=[pl.BlockSpec((N_DEV, tr, C), lambda i: (0, i, 0)), spec, spec, spec], out_specs=[spec] * 4,
        out_shape=[jax.ShapeDtypeStruct((R, C), F32)] * 4, compiler_params=_params(1),
    )(recv, w, m, v)


def _forward_local(x, target, nw0, nw1, fw, wa_in, conv_w, gate_params, o_norm, wa_out, wb_in, wb_out):
    T, D = x.shape
    nD = D // LANES
    sv = {}
    sv["u0"] = _rmsnorm_fwd(x, nw0, "a_norm_fwd")
    sv["pa"] = _mm_nn(sv["u0"], wa_in, "a_in_proj")
    sv["qkv_a"] = _conv_fwd(sv["pa"], conv_w, D, "a_conv_fwd")
    sv["beta"], sv["gc"] = _gates_fwd(sv["pa"], gate_params, 4 * nD, "a_gates_fwd")
    sv["o_a"], sv["s_all"], sv["t_all"] = _chunk_fwd(sv["qkv_a"], sv["beta"], sv["gc"], D, "a_chunk_fwd")
    sv["y_a"] = _onorm_gate_fwd(sv["o_a"], sv["pa"], 3, o_norm, "a_onorm_fwd")
    sv["h1"] = _mm_nn(sv["y_a"], wa_out, "a_out_proj", add=x)
    sv["u1"] = _rmsnorm_fwd(sv["h1"], nw1, "b_norm_fwd")
    sv["qkv_b"] = _mm_nn(sv["u1"], wb_in[:, :3 * D], "b_in_proj_qkv", out_dtype=BF16)
    sv["gate_b"] = _mm_nn(sv["u1"], wb_in[:, 3 * D:], "b_in_proj_gate")
    sv["o_b"], sv["r_b"] = _sb_fwd(sv["qkv_b"], D, "b_attn_fwd")
    sv["y_b"] = _gate_mul_fwd(sv["o_b"], sv["gate_b"], "b_gate_fwd")
    sv["h2"] = _mm_nn(sv["y_b"], wb_out, "b_out_proj", add=sv["h1"])
    sv["dh2"], sv["loss"], sv["dfw"] = _final_loss(sv["h2"], fw, target, "final_loss")
    return sv


def _backward_local(sv, x, nw0, nw1, wa_in, conv_w, gate_params, o_norm, wa_out, wb_in, wb_out):
    T, D = x.shape
    nD = D // LANES
    g = {}
    dh2 = sv["dh2"]
    g["wb_out"] = _mm_tn(sv["y_b"], dh2, "b_out_proj_dw")
    dy_b = _mm_nt(dh2, wb_out, "b_out_proj_dx")
    do_b, dgate_b = _gate_mul_bwd(dy_b, sv["o_b"], sv["gate_b"], "b_gate_bwd")
    dq_b, dk_b, dv_b = _sb_bwd(sv["qkv_b"], do_b, sv["r_b"], D, "b_attn_bwd")
    dp_b = jnp.concatenate([dq_b, dk_b, dv_b, dgate_b], axis=1)
    g["wb_in"] = _mm_tn(sv["u1"], dp_b, "b_in_proj_dw")
    du1 = _mm_nt(dp_b, wb_in, "b_in_proj_dx")
    dh1, g["nw1"] = _rmsnorm_bwd(sv["h1"], nw1, du1, dh2, "b_norm_bwd")
    g["wa_out"] = _mm_tn(sv["y_a"], dh1, "a_out_proj_dw")
    dy_a = _mm_nt(dh1, wa_out, "a_out_proj_dx")
    do_a, dz_a, g["o_norm"] = _onorm_gate_bwd(dy_a, sv["o_a"], sv["pa"], 3, o_norm, "a_onorm_bwd")
    dqkv_a, dbeta, dg = _chunk_bwd(sv["qkv_a"], sv["beta"], sv["gc"], sv["s_all"], sv["t_all"], do_a, D, "a_chunk_bwd")
    dp_gates, g["gate_params"] = _gates_bwd(sv["pa"], gate_params, 4 * nD, dbeta, dg, "a_gates_bwd")
    dc, g["conv_w"] = _conv_bwd_act(sv["pa"], conv_w, dqkv_a, D, "a_conv_bwd_act")
    dp_qkv = _conv_bwd_input(dc, conv_w, "a_conv_bwd_input")
    dp_a = jnp.concatenate([dp_qkv, dz_a, dp_gates], axis=1)
    g["wa_in"] = _mm_tn(sv["u0"], dp_a, "a_in_proj_dw")
    du0 = _mm_nt(dp_a, wa_in, "a_in_proj_dx")
    g["x"], g["nw0"] = _rmsnorm_bwd(x, nw0, du0, dh1, "a_norm_bwd")
    g["fw"] = sv["dfw"]
    return g


def kernel(x, norm_w, a_w_in, a_conv_w, a_a_log, a_dt_bias, a_o_norm, a_w_out, b_w_in, b_w_out, final_norm_w, loss_target, m_norm_w, m_a_w_in, m_a_conv_w, m_a_a_log, m_a_dt_bias, m_a_o_norm, m_a_w_out, m_b_w_in, m_b_w_out, m_final_norm_w, v_norm_w, v_a_w_in, v_a_conv_w, v_a_a_log, v_a_dt_bias, v_a_o_norm, v_a_w_out, v_b_w_in, v_b_w_out, v_final_norm_w):
    D = x.shape[-1]
    H = D // HEAD_DIM
    shards = [a_w_in[0].astype(BF16), a_w_out[0].astype(BF16), b_w_in[0].astype(BF16), b_w_out[0].astype(BF16), a_conv_w[0]]
    ga_in, ga_out, gb_in, gb_out, g_conv = _exchange(shards, gather=True, name="weights_gather")
    wa = ga_in.transpose(1, 0, 2).reshape(D, -1)
    pad = lambda w: jnp.pad(w, ((0, 0), (0, LANES - w.shape[1])))
    wa_in = jnp.concatenate([wa[:, :4 * D], pad(wa[:, 4 * D:4 * D + H]), pad(wa[:, 4 * D + H:])], axis=1)
    wa_out = ga_out.reshape(D, D)
    wb_in = gb_in.transpose(1, 0, 2).reshape(D, 4 * D)
    wb_out = gb_out.reshape(D, D)
    conv_w = g_conv.transpose(1, 0, 2).reshape(4, 3 * D)
    gate_params = jnp.zeros((8, LANES), F32).at[0, :H].set(a_a_log[0]).at[1, :H].set(a_dt_bias[0])
    nw0, nw1, fw = norm_w[0:1], norm_w[1:2], final_norm_w[None]

    sv = _forward_local(x[0], loss_target[0], nw0, nw1, fw, wa_in, conv_w, gate_params, a_o_norm, wa_out, wb_in, wb_out)
    g = _backward_local(sv, x[0], nw0, nw1, wa_in, conv_w, gate_params, a_o_norm, wa_out, wb_in, wb_out)

    gwa = g["wa_in"]
    gwa = jnp.concatenate([gwa[:, :4 * D], gwa[:, 4 * D:4 * D + H], gwa[:, 4 * D + LANES:4 * D + LANES + H]], axis=1)
    row = lambda v: jnp.pad(v.reshape(1, -1), ((0, 0), (0, D - v.size)))
    small = jnp.concatenate([g["nw0"][0:1], g["nw1"][0:1], g["fw"][0:1], row(g["gate_params"][0, :H]),
                             row(g["gate_params"][1, :H]), row(g["o_norm"][0]), row(sv["loss"][0, 0:1]),
                             jnp.zeros((1, D), F32)], axis=0)
    contribs = [gwa.reshape(D, N_DEV, -1).transpose(1, 0, 2), g["wa_out"].reshape(N_DEV, D // N_DEV, D),
                g["wb_in"].reshape(D, N_DEV, -1).transpose(1, 0, 2), g["wb_out"].reshape(N_DEV, D // N_DEV, D),
                g["conv_w"].reshape(4, N_DEV, -1).transpose(1, 0, 2), jnp.broadcast_to(small[None], (N_DEV, 8, D))]
    ra_in, ra_out, rb_in, rb_out, r_conv, r_small = _exchange(contribs, gather=False, name="grads_exchange")

    outs = {}
    for nm, recv, w, m, v in (("a_w_in", ra_in, a_w_in, m_a_w_in, v_a_w_in), ("a_w_out", ra_out, a_w_out, m_a_w_out, v_a_w_out),
                              ("b_w_in", rb_in, b_w_in, m_b_w_in, v_b_w_in), ("b_w_out", rb_out, b_w_out, m_b_w_out, v_b_w_out),
                              ("a_conv_w", r_conv, a_conv_w, m_a_conv_w, v_a_conv_w)):
        outs[nm] = tuple(o[None] for o in _reduce_adamw(recv, w[0], m[0], v[0], "adamw_" + nm))

    def pack(nw, alog, dt, onorm, fnw):
        return jnp.concatenate([nw, fnw.reshape(1, D), row(alog), row(dt), row(onorm), jnp.zeros((2, D), F32)], axis=0)

    s_g, s_d, s_m, s_v = _reduce_adamw(
        r_small, pack(norm_w, a_a_log, a_dt_bias, a_o_norm, final_norm_w),
        pack(m_norm_w, m_a_a_log, m_a_dt_bias, m_a_o_norm, m_final_norm_w),
        pack(v_norm_w, v_a_a_log, v_a_dt_bias, v_a_o_norm, v_final_norm_w), "adamw_small")
    loss = s_g[6, 0]
    for i, s in enumerate((s_g, s_d, s_m, s_v)):
        outs.setdefault("norm_w", [None] * 4)[i] = s[0:2]
        outs.setdefault("final_norm_w", [None] * 4)[i] = s[2]
        outs.setdefault("a_a_log", [None] * 4)[i] = s[3:4, :H]
        outs.setdefault("a_dt_bias", [None] * 4)[i] = s[4:5, :H]
        outs.setdefault("a_o_norm", [None] * 4)[i] = s[5:6, :HEAD_DIM]
    names = ("norm_w", "a_w_in", "a_conv_w", "a_a_log", "a_dt_bias", "a_o_norm", "a_w_out", "b_w_in", "b_w_out", "final_norm_w")
    return (loss, g["x"][None]) + tuple(outs[n][i] for i in range(4) for n in names)
```

```python
import functools

import jax
import jax.numpy as jnp
from jax import lax
from jax.experimental import pallas as pl
from jax.experimental.pallas import tpu as pltpu

F32 = jnp.float32
BF16 = jnp.bfloat16
EPS = 1e-6
LOG2_E = 1.4426950408889634
MASKED_SCORE = -1e30
HEAD_DIM = 128
CHUNK = 64
ATTN_Q_BLOCKS = (512, 256)
ATTN_K_BLOCK = 128
LANES = 128
N_DEV = 8
VMEM_LIMIT_BYTES = 48 * 1024 * 1024
ADAM_LR, ADAM_B1, ADAM_B2, ADAM_EPS, ADAM_WD, ADAM_STEP = 0.001, 0.9, 0.999, 1e-08, 0.01, 10
MESH_ID = pl.DeviceIdType.MESH


def _pick(n, candidates):
    for c in candidates:
        if n % c == 0:
            return c
    raise ValueError(f"no tile for {n} in {candidates}")


def _params(n_grid_axes):
    return pltpu.CompilerParams(dimension_semantics=("arbitrary",) * n_grid_axes, vmem_limit_bytes=VMEM_LIMIT_BYTES)


def _dot(a, b):
    return jnp.dot(a.astype(BF16), b.astype(BF16), preferred_element_type=F32)


def _dot_nt(a, b):
    return lax.dot_general(a.astype(BF16), b.astype(BF16), (((1,), (1,)), ((), ())), preferred_element_type=F32)


def _dot_tn(a, b):
    return lax.dot_general(a.astype(BF16), b.astype(BF16), (((0,), (0,)), ((), ())), preferred_element_type=F32)


def _split2(x):
    hi = x.astype(BF16)
    lo = (x - hi.astype(F32)).astype(BF16)
    return hi, lo


def _split3(x):
    hi = x.astype(BF16)
    r = x - hi.astype(F32)
    mid = r.astype(BF16)
    lo = (r - mid.astype(F32)).astype(BF16)
    return hi, mid, lo


def _dot3(a, b):
    a_hi, a_lo = _split2(a)
    b_hi, b_lo = _split2(b)
    d = functools.partial(jnp.dot, preferred_element_type=F32)
    return d(a_hi, b_hi) + (d(a_hi, b_lo) + d(a_lo, b_hi))


def _silu(x):
    return x * jax.nn.sigmoid(x)


def _softplus(x):
    return jnp.maximum(x, 0.0) + jnp.log1p(jnp.exp(-jnp.abs(x)))


def _iota2(shape, axis):
    return lax.broadcasted_iota(jnp.int32, shape, axis)


def _rms_bwd_math(x, w, dy):
    r = lax.rsqrt(jnp.mean(x * x, axis=-1, keepdims=True) + EPS)
    xhat = x * r
    dxhat = dy * w
    dx = r * (dxhat - xhat * jnp.mean(dxhat * xhat, axis=-1, keepdims=True))
    dw = jnp.sum(dy * xhat, axis=0, keepdims=True)
    return dx, dw


def _rmsnorm_fwd(x, w, name):
    T, D = x.shape
    tm = _pick(T, (512, 256, 128))

    def body(x_ref, w_ref, o_ref):
        xf = x_ref[...]
        r = lax.rsqrt(jnp.mean(xf * xf, axis=-1, keepdims=True) + EPS)
        o_ref[...] = (xf * r * w_ref[...]).astype(BF16)

    return pl.pallas_call(
        body, name=name, grid=(T // tm,),
        in_specs=[pl.BlockSpec((tm, D), lambda i: (i, 0)), pl.BlockSpec((1, D), lambda i: (0, 0))],
        out_specs=pl.BlockSpec((tm, D), lambda i: (i, 0)),
        out_shape=jax.ShapeDtypeStruct((T, D), BF16), compiler_params=_params(1),
    )(x, w)


def _rmsnorm_bwd(x, w, du, dres, name):
    T, D = x.shape
    tm = _pick(T, (512, 256, 128))

    def body(x_ref, w_ref, du_ref, dres_ref, dx_ref, dw_ref):
        dx, dw = _rms_bwd_math(x_ref[...], w_ref[...], du_ref[...].astype(F32))
        dx_ref[...] = dres_ref[...] + dx

        @pl.when(pl.program_id(0) == 0)
        def _():
            dw_ref[...] = jnp.zeros_like(dw_ref)

        dw_ref[...] += jnp.broadcast_to(dw, dw_ref.shape)

    return pl.pallas_call(
        body, name=name, grid=(T // tm,),
        in_specs=[pl.BlockSpec((tm, D), lambda i: (i, 0)), pl.BlockSpec((1, D), lambda i: (0, 0)),
                  pl.BlockSpec((tm, D), lambda i: (i, 0)), pl.BlockSpec((tm, D), lambda i: (i, 0))],
        out_specs=[pl.BlockSpec((tm, D), lambda i: (i, 0)), pl.BlockSpec((8, D), lambda i: (0, 0))],
        out_shape=[jax.ShapeDtypeStruct((T, D), F32), jax.ShapeDtypeStruct((8, D), F32)],
        compiler_params=_params(1),
    )(x, w, du, dres)


def _mm_nn(a, b, name, add=None, out_dtype=F32):
    M, K = a.shape
    _, N = b.shape
    tm = _pick(M, (256, 128)) if N > 2048 else _pick(M, (512, 256, 128))

    def body(*refs):
        a_ref, b_ref = refs[0], refs[1]
        o_ref = refs[-1]
        acc = _dot(a_ref[...], b_ref[...])
        if add is not None:
            acc = acc + refs[2][...]
        o_ref[...] = acc.astype(out_dtype)

    in_specs = [pl.BlockSpec((tm, K), lambda i: (i, 0)), pl.BlockSpec((K, N), lambda i: (0, 0))]
    args = [a, b]
    if add is not None:
        in_specs.append(pl.BlockSpec((tm, N), lambda i: (i, 0)))
        args.append(add)
    return pl.pallas_call(
        body, name=name, grid=(M // tm,), in_specs=in_specs,
        out_specs=pl.BlockSpec((tm, N), lambda i: (i, 0)),
        out_shape=jax.ShapeDtypeStruct((M, N), out_dtype), compiler_params=_params(1),
    )(*args)


def _mm_nt(pairs, name):
    M = pairs[0][0].shape[0]
    N = pairs[0][1].shape[0]
    n = len(pairs)
    tm = _pick(M, (512, 256, 128))

    def body(*refs):
        acc = _dot_nt(refs[0][...], refs[n][...])
        for p in range(1, n):
            acc = acc + _dot_nt(refs[p][...], refs[n + p][...])
        refs[-1][...] = acc

    in_specs = ([pl.BlockSpec((tm, a.shape[1]), lambda i: (i, 0)) for a, _ in pairs]
                + [pl.BlockSpec(b.shape, lambda i: (0, 0)) for _, b in pairs])
    return pl.pallas_call(
        body, name=name, grid=(M // tm,), in_specs=in_specs,
        out_specs=pl.BlockSpec((tm, N), lambda i: (i, 0)),
        out_shape=jax.ShapeDtypeStruct((M, N), F32), compiler_params=_params(1),
    )(*[a for a, _ in pairs], *[b for _, b in pairs])


def _mm_tn(a, b, name):
    R, M = a.shape
    _, N = b.shape
    tn = _pick(N, (1536, 1024, 512, 256, 128))
    tr = _pick(R, (512, 256, 128))

    def body(a_ref, b_ref, o_ref):
        @pl.when(pl.program_id(1) == 0)
        def _():
            o_ref[...] = jnp.zeros_like(o_ref)

        o_ref[...] += _dot_tn(a_ref[...], b_ref[...])

    return pl.pallas_call(
        body, name=name, grid=(N // tn, R // tr),
        in_specs=[pl.BlockSpec((tr, M), lambda j, r: (r, 0)), pl.BlockSpec((tr, tn), lambda j, r: (r, j))],
        out_specs=pl.BlockSpec((M, tn), lambda j, r: (0, j)),
        out_shape=jax.ShapeDtypeStruct((M, N), F32), compiler_params=_params(2),
    )(a, b)


def _qkv_post(c, j, n_heads):
    s = _silu(c)
    parts = []
    for h in range(n_heads):
        sh = s[:, h * HEAD_DIM:(h + 1) * HEAD_DIM]
        parts.append(sh * lax.rsqrt(jnp.sum(sh * sh, axis=-1, keepdims=True) + EPS))
    n = jnp.concatenate(parts, axis=-1)
    is_q = (j == 0).astype(F32)
    is_v = (j == 2).astype(F32)
    n = n * (1.0 + is_q * (HEAD_DIM ** -0.5 - 1.0))
    return n * (1.0 - is_v) + s * is_v


def _conv_taps(cur, halo_prev):
    tm = cur.shape[0]
    ext = jnp.concatenate([halo_prev, cur], axis=0)
    taps = [pltpu.roll(ext, s, 0)[8:8 + tm] for s in (3, 2, 1)]
    return taps + [cur]


def _conv_fwd(p, conv_w, d_model, name):
    T = p.shape[0]
    D = d_model
    H = D // HEAD_DIM
    tm = _pick(T, (256, 128, 64))

    def body(cur_ref, prev_ref, w_ref, o_ref):
        i, j = pl.program_id(0), pl.program_id(1)
        prev = prev_ref[...] * (i > 0).astype(F32)
        taps = _conv_taps(cur_ref[...], prev)
        w = w_ref[...]
        c = sum(taps[k] * w[k:k + 1, :] for k in range(4))
        o_ref[...] = _qkv_post(c, j, H)

    return pl.pallas_call(
        body, name=name, grid=(T // tm, 3),
        in_specs=[pl.BlockSpec((tm, D), lambda i, j: (i, j)),
                  pl.BlockSpec((8, D), lambda i, j: (jnp.maximum(i * (tm // 8) - 1, 0), j)),
                  pl.BlockSpec((4, D), lambda i, j: (0, j))],
        out_specs=pl.BlockSpec((tm, D), lambda i, j: (i, j)),
        out_shape=jax.ShapeDtypeStruct((T, 3 * D), F32), compiler_params=_params(2),
    )(p, p, conv_w)


def _chunk_tri(tm, upper):
    r, c = _iota2((tm, tm), 0), _iota2((tm, tm), 1)
    same = (r // CHUNK) == (c // CHUNK)
    tri = (c >= r) if upper else (c <= r)
    return jnp.where(same & tri, 1.0, 0.0).astype(BF16)


def _dot_mask(mask_bf16, x):
    hi, mid, lo = _split3(x)
    d = functools.partial(jnp.dot, preferred_element_type=F32)
    return d(mask_bf16, hi) + (d(mask_bf16, mid) + d(mask_bf16, lo))


def _gates_math(pb, pa, a_log, dt_bias):
    beta = jax.nn.sigmoid(pb)
    g = -jnp.exp(a_log) * _softplus(pa + dt_bias)
    return beta, g


def _gates_fwd(p, gate_params, col0, name):
    T = p.shape[0]
    tm = _pick(T, (256, 128, 64))

    def body(pb_ref, pa_ref, gp_ref, beta_ref, gc_ref):
        gp = gp_ref[...]
        beta, g = _gates_math(pb_ref[...], pa_ref[...], gp[0:1, :], gp[1:2, :])
        beta_ref[...] = beta
        gc_ref[...] = _dot_mask(_chunk_tri(tm, upper=False), g)

    return pl.pallas_call(
        body, name=name, grid=(T // tm,),
        in_specs=[pl.BlockSpec((tm, LANES), lambda i: (i, col0)), pl.BlockSpec((tm, LANES), lambda i: (i, col0 + 1)),
                  pl.BlockSpec((8, LANES), lambda i: (0, 0))],
        out_specs=[pl.BlockSpec((tm, LANES), lambda i: (i, 0))] * 2,
        out_shape=[jax.ShapeDtypeStruct((T, LANES), F32)] * 2, compiler_params=_params(1),
    )(p, p, gate_params)


def _col_to_row(col):
    C = col.shape[0]
    eye = _iota2((C, C), 0) == _iota2((C, C), 1)
    return jnp.sum(jnp.where(eye, col, 0.0), axis=0, keepdims=True)


def _lockstep(generators):
    generators = list(generators)
    results = [None] * len(generators)
    live = list(range(len(generators)))
    while live:
        for idx in list(live):
            try:
                next(generators[idx])
            except StopIteration as done:
                results[idx] = done.value
                live.remove(idx)
    return results


def _unit_lower_inverse(low):
    C = low.shape[0]
    eye = (_iota2((C, C), 0) == _iota2((C, C), 1)).astype(F32)
    t = eye - low
    p = _dot3(low, low)
    yield
    n = 2
    while True:
        tp = _dot3(t, p)
        n *= 2
        if n < C:
            p = _dot3(p, p)
        yield
        t = t + tp
        if n >= C:
            return t


def _chunk_head_fwd(q, k, v, gc, beta, s_in):
    C = q.shape[0]
    r, c = _iota2((C, C), 0), _iota2((C, C), 1)
    causal, strict = r >= c, r > c
    decay = jnp.where(causal, jnp.exp(jnp.where(causal, gc - _col_to_row(gc), 0.0)), 0.0)
    kb, vb = k * beta, v * beta
    eg = jnp.exp(gc)
    kk = _dot_nt(kb, k)
    qk = _dot_nt(q, k)
    o_state = _dot(q * eg, s_in)
    yield
    t_inv = yield from _unit_lower_inverse(jnp.where(strict, kk * decay, 0.0))
    u = _dot(t_inv, vb)
    w = _dot(t_inv, kb * eg)
    yield
    w_state = _dot(w, s_in)
    yield
    v_new = u - w_state
    g_last = gc[C - 1:C, :]
    o_intra = _dot(qk * decay, v_new)
    s_add = _dot_tn(k * jnp.exp(g_last - gc), v_new)
    yield
    return o_state + o_intra, s_in * jnp.exp(g_last) + s_add, t_inv


def _chunk_fwd(qkv, beta, gc, d_model, name):
    T = qkv.shape[0]
    D = d_model
    H = D // HEAD_DIM
    N = T // CHUNK

    def body(q_ref, k_ref, v_ref, beta_ref, gc_ref, o_ref, s_all_ref, t_all_ref, s_ref):
        @pl.when(pl.program_id(0) == 0)
        def _():
            s_ref[...] = jnp.zeros_like(s_ref)

        heads = [slice(h * HEAD_DIM, (h + 1) * HEAD_DIM) for h in range(H)]
        s_all_ref[0] = s_ref[...]
        results = _lockstep(_chunk_head_fwd(q_ref[:, hs], k_ref[:, hs], v_ref[:, hs], gc_ref[:, h:h + 1],
                                            beta_ref[:, h:h + 1], s_ref[h]) for h, hs in enumerate(heads))
        for h, (o, s_out, t_inv) in enumerate(results):
            o_ref[:, heads[h]] = o
            s_ref[h] = s_out
            t_all_ref[0, h] = t_inv

    return pl.pallas_call(
        body, name=name, grid=(N,),
        in_specs=[pl.BlockSpec((CHUNK, D), lambda n: (n, 0)), pl.BlockSpec((CHUNK, D), lambda n: (n, 1)),
                  pl.BlockSpec((CHUNK, D), lambda n: (n, 2)),
                  pl.BlockSpec((CHUNK, LANES), lambda n: (n, 0)), pl.BlockSpec((CHUNK, LANES), lambda n: (n, 0))],
        out_specs=[pl.BlockSpec((CHUNK, D), lambda n: (n, 0)),
                   pl.BlockSpec((1, H, HEAD_DIM, HEAD_DIM), lambda n: (n, 0, 0, 0)),
                   pl.BlockSpec((1, H, CHUNK, CHUNK), lambda n: (n, 0, 0, 0))],
        out_shape=[jax.ShapeDtypeStruct((T, D), F32), jax.ShapeDtypeStruct((N, H, HEAD_DIM, HEAD_DIM), F32),
                   jax.ShapeDtypeStruct((N, H, CHUNK, CHUNK), F32)],
        scratch_shapes=[pltpu.VMEM((H, HEAD_DIM, HEAD_DIM), F32)], compiler_params=_params(1),
    )(qkv, qkv, qkv, beta, gc)


def _onorm_gate_math(o, z, w, n_heads):
    parts = []
    for h in range(n_heads):
        hs = slice(h * HEAD_DIM, (h + 1) * HEAD_DIM)
        oh = o[:, hs]
        y = oh * lax.rsqrt(jnp.mean(oh * oh, axis=-1, keepdims=True) + EPS) * w
        parts.append(y * _silu(z[:, hs]))
    return jnp.concatenate(parts, axis=-1)


def _onorm_gate_fwd(o, p, z_col, o_norm, name):
    T, D = o.shape
    H = D // HEAD_DIM
    tm = _pick(T, (256, 128, 64))

    def body(o_ref, z_ref, w_ref, y_ref):
        y_ref[...] = _onorm_gate_math(o_ref[...], z_ref[...], w_ref[...], H).astype(BF16)

    return pl.pallas_call(
        body, name=name, grid=(T // tm,),
        in_specs=[pl.BlockSpec((tm, D), lambda i: (i, 0)), pl.BlockSpec((tm, D), lambda i: (i, z_col)),
                  pl.BlockSpec((1, HEAD_DIM), lambda i: (0, 0))],
        out_specs=pl.BlockSpec((tm, D), lambda i: (i, 0)),
        out_shape=jax.ShapeDtypeStruct((T, D), BF16), compiler_params=_params(1),
    )(o, p, o_norm)


def _diag_mask(qb, kb, d):
    return _iota2((qb, kb), 0) > _iota2((qb, kb), 1) + d * kb


def _fill_score_masks(mask_buf, qb, kb, ns):
    mask_buf[0] = jnp.zeros(mask_buf.shape[1:], F32)
    for d in range(ns):
        for half in range(2):
            mask_buf[d + 1, :, half * kb:(half + 1) * kb] = jnp.where(_diag_mask(qb, kb, 2 * d + half), 0.0, MASKED_SCORE)


def _softplus_bits(w):
    u = 1.0 + jnp.exp2(jnp.minimum(w, 64.0))
    return jnp.maximum(w, jnp.log2(u)), 1.0 / u


def _incl_lower(n):
    return jnp.where((_iota2((2 * n, n), 0) & (n - 1)) >= _iota2((2 * n, n), 1), 1.0, 0.0).astype(BF16)


def _incl_upper(n):
    return jnp.where((_iota2((2 * n, n), 0) & (n - 1)) <= _iota2((2 * n, n), 1), 1.0, 0.0).astype(BF16)


def _dot_cum(x, tri_bf16):
    hi, lo = _split2(x)
    return jnp.dot(jnp.concatenate([hi, lo], axis=1), tri_bf16, preferred_element_type=F32)


def _sb_fwd(qkv, d_model, name):
    T = qkv.shape[0]
    D = d_model
    H = D // HEAD_DIM
    QB = _pick(T, ATTN_Q_BLOCKS)
    KB = ATTN_K_BLOCK
    KS = 2 * KB
    ns = QB // KS
    nq = T // QB
    scale = HEAD_DIM ** -0.5

    def body(q_ref, k_ref, v_ref, o_ref, r_ref, w_buf, cum_buf, mask_buf):
        i = pl.program_id(1)

        @pl.when(i == 0)
        def _():
            _fill_score_masks(mask_buf, QB, KB, ns)

        q = q_ref[...]
        tri = _incl_lower(KB)
        n_tot = (i + 1) * ns

        def key_step(m):
            return jnp.maximum(n_tot - 1 - m, 0)

        def rows(ref, s):
            return ref[pl.ds(pl.multiple_of(s * KS, KS), KS), :]

        def scores(s):
            return _dot_nt(q, rows(k_ref, s)) * (scale * LOG2_E) + mask_buf[jnp.maximum(s - i * ns + 1, 0)]

        def cums(w):
            sp = _softplus_bits(w)[0]
            return jnp.concatenate([_dot_cum(sp[:, :KB], tri), _dot_cum(sp[:, KB:], tri)], axis=1)

        def weights(w, cum, carry):
            a_r = jnp.exp2(w[:, KB:] - cum[:, KB:] - carry)
            carry = carry + cum[:, KB:KB + 1]
            a_l = jnp.exp2(w[:, :KB] - cum[:, :KB] - carry)
            return jnp.concatenate([a_l, a_r], axis=1).astype(BF16), carry + cum[:, 0:1]

        def trip(m, carry):
            w_new = scores(key_step(m + 2))
            a, carry = weights(w_buf[m % 3], cum_buf[m % 2], carry)
            o_ref[...] += _dot(a, rows(v_ref, key_step(m)))
            cum_buf[(m + 1) % 2] = cums(w_buf[(m + 1) % 3])
            w_buf[(m + 2) % 3] = w_new
            return carry

        o_ref[...] = jnp.zeros_like(o_ref)
        w_buf[0] = scores(key_step(0))
        w_buf[1] = scores(key_step(1))
        cum_buf[0] = cums(w_buf[0])
        carry = lax.fori_loop(0, n_tot, trip, jnp.zeros((QB, 1), F32))
        r_ref[0] = jnp.broadcast_to(carry, (QB, LANES))

    return pl.pallas_call(
        body, name=name, grid=(H, nq),
        in_specs=[pl.BlockSpec((QB, HEAD_DIM), lambda h, i: (i, h)),
                  pl.BlockSpec((T, HEAD_DIM), lambda h, i: (0, H + h)),
                  pl.BlockSpec((T, HEAD_DIM), lambda h, i: (0, 2 * H + h))],
        out_specs=[pl.BlockSpec((QB, HEAD_DIM), lambda h, i: (i, h)),
                   pl.BlockSpec((1, QB, LANES), lambda h, i: (h, i, 0))],
        out_shape=[jax.ShapeDtypeStruct((T, D), F32), jax.ShapeDtypeStruct((H, T, LANES), F32)],
        scratch_shapes=[pltpu.VMEM((3, QB, KS), F32), pltpu.VMEM((2, QB, KS), F32), pltpu.VMEM((ns + 1, QB, KS), F32)],
        compiler_params=_params(2),
    )(qkv, qkv, qkv)


def _gate_mul_fwd(o, gate, name):
    T, D = o.shape
    tm = _pick(T, (512, 256, 128))

    def body(o_ref, g_ref, y_ref):
        y_ref[...] = (o_ref[...] * _silu(g_ref[...])).astype(BF16)

    spec = pl.BlockSpec((tm, D), lambda i: (i, 0))
    return pl.pallas_call(body, name=name, grid=(T // tm,), in_specs=[spec, spec], out_specs=spec,
                          out_shape=jax.ShapeDtypeStruct((T, D), BF16), compiler_params=_params(1))(o, gate)


def _final_loss(h, w, target, name):
    T, D = h.shape
    tm = _pick(T, (512, 256, 128))

    def body(h_ref, w_ref, t_ref, dh_ref, loss_ref, dw_ref):
        x, w = h_ref[...], w_ref[...]
        r = lax.rsqrt(jnp.mean(x * x, axis=-1, keepdims=True) + EPS)
        err = x * r * w - t_ref[...]
        part = 0.5 * jnp.sum(jnp.mean(err * err, axis=-1, keepdims=True), axis=0, keepdims=True)
        dx, dw = _rms_bwd_math(x, w, err * (1.0 / D))
        dh_ref[...] = dx

        @pl.when(pl.program_id(0) == 0)
        def _():
            loss_ref[...] = jnp.zeros_like(loss_ref)
            dw_ref[...] = jnp.zeros_like(dw_ref)

        loss_ref[...] += jnp.broadcast_to(part, loss_ref.shape)
        dw_ref[...] += jnp.broadcast_to(dw, dw_ref.shape)

    return pl.pallas_call(
        body, name=name, grid=(T // tm,),
        in_specs=[pl.BlockSpec((tm, D), lambda i: (i, 0)), pl.BlockSpec((1, D), lambda i: (0, 0)),
                  pl.BlockSpec((tm, D), lambda i: (i, 0))],
        out_specs=[pl.BlockSpec((tm, D), lambda i: (i, 0)), pl.BlockSpec((8, LANES), lambda i: (0, 0)),
                   pl.BlockSpec((8, D), lambda i: (0, 0))],
        out_shape=[jax.ShapeDtypeStruct((T, D), F32), jax.ShapeDtypeStruct((8, LANES), F32),
                   jax.ShapeDtypeStruct((8, D), F32)],
        compiler_params=_params(1),
    )(h, w, target)


def _gate_mul_bwd(dy, o, gate, name):
    T, D = o.shape
    tm = _pick(T, (512, 256, 128))

    def body(dy_ref, o_ref, g_ref, do_ref, dg_ref):
        dy, g = dy_ref[...], g_ref[...]
        s = jax.nn.sigmoid(g)
        do_ref[...] = dy * (g * s)
        dg_ref[...] = (dy * o_ref[...] * (s + g * s * (1.0 - s))).astype(BF16)

    spec = pl.BlockSpec((tm, D), lambda i: (i, 0))
    return pl.pallas_call(body, name=name, grid=(T // tm,), in_specs=[spec] * 3, out_specs=[spec] * 2,
                          out_shape=[jax.ShapeDtypeStruct((T, D), F32), jax.ShapeDtypeStruct((T, D), BF16)],
                          compiler_params=_params(1))(dy, o, gate)


def _sb_bwd(qkv, do, r_tot, d_model, name):
    T = qkv.shape[0]
    D = d_model
    H = D // HEAD_DIM
    QB = _pick(T, ATTN_Q_BLOCKS)
    KB = ATTN_K_BLOCK
    KS = 2 * KB
    ns = QB // KS
    nq = T // QB
    n_key_steps = T // KS
    scale = HEAD_DIM ** -0.5

    def body(q_ref, k_ref, v_ref, do_ref, r_ref, dq_ref, dk_ref, dv_ref,
             dkt_acc, dvt_acc, dq_acc, w_buf, da_buf, cum_buf, sig_buf, mask_buf):
        i = pl.program_id(1)

        @pl.when(i == 0)
        def _():
            dkt_acc[...] = jnp.zeros_like(dkt_acc)
            dvt_acc[...] = jnp.zeros_like(dvt_acc)
            _fill_score_masks(mask_buf, QB, KB, ns)

        q = q_ref[...]
        do_blk = do_ref[...].astype(BF16)
        q_t = q.astype(F32).T.astype(BF16)
        do_t = do_ref[...].T.astype(BF16)
        row_total = r_ref[0][:, 0:1]
        tri_rev = _incl_lower(KB)
        tri_fwd = jnp.where(_iota2((KB, KB), 0) <= _iota2((KB, KB), 1), 1.0, 0.0).astype(BF16)
        n_tot = (i + 1) * ns

        def step_rows(ref, s):
            return ref[pl.ds(pl.multiple_of(s * KS, KS), KS), :]

        def scores(s):
            w = _dot_nt(q, step_rows(k_ref, s)) * (scale * LOG2_E) + mask_buf[jnp.maximum(s - i * ns + 1, 0)]
            return w, _dot_nt(do_blk, step_rows(v_ref, s))

        def softplus_sums(w):
            sp, one_minus_sig = _softplus_bits(w)
            cum = jnp.concatenate([_dot_cum(sp[:, :KB], tri_rev), _dot_cum(sp[:, KB:], tri_rev)], axis=1)
            return cum, 1.0 - one_minus_sig

        def weights(w, cum, da, left_sp):
            right_l = row_total - left_sp - cum[:, 0:1]
            right_r = right_l - cum[:, KB:KB + 1]
            a = jnp.concatenate([jnp.exp2(w[:, :KB] - cum[:, :KB] - right_l),
                                 jnp.exp2(w[:, KB:] - cum[:, KB:] - right_r)], axis=1)
            p = da * a
            cp = jnp.concatenate([_dot(p[:, :KB], tri_fwd), _dot(p[:, KB:], tri_fwd)], axis=1)
            return a.astype(BF16), p, cp, row_total - right_r

        def score_grads(p, cp, sig, left_p):
            cum_l = cp[:, :KB] + left_p
            cum_r = cp[:, KB:] + cum_l[:, KB - 1:KB]
            dz = p - sig * jnp.concatenate([cum_l, cum_r], axis=1)
            return dz.astype(BF16), cum_r[:, KB - 1:KB]

        def trip(m, st):
            left_sp, left_p = st
            s2 = jnp.minimum(m + 2, n_tot - 1)
            s1 = jnp.minimum(m + 1, n_tot - 1)
            w_new, da_new = scores(s2)
            a, p, cp, left_sp = weights(w_buf[m % 3], cum_buf[m % 2], da_buf[m % 3], left_sp)
            cum_new, sig_new = softplus_sums(w_buf[s1 % 3])
            dz, left_p = score_grads(p, cp, sig_buf[m % 2], left_p)
            dq_acc[...] += _dot(dz, step_rows(k_ref, m))
            dkt_acc[m] += jnp.dot(q_t, dz, preferred_element_type=F32) * scale
            dvt_acc[m] += jnp.dot(do_t, a, preferred_element_type=F32)
            cum_buf[(m + 1) % 2] = cum_new
            sig_buf[(m + 1) % 2] = sig_new
            w_buf[(m + 2) % 3] = w_new
            da_buf[(m + 2) % 3] = da_new
            return left_sp, left_p

        dq_acc[...] = jnp.zeros_like(dq_acc)
        w_buf[0], da_buf[0] = scores(0)
        w_buf[1], da_buf[1] = scores(jnp.minimum(1, n_tot - 1))
        cum_buf[0], sig_buf[0] = softplus_sums(w_buf[0])
        zero_col = jnp.zeros((QB, 1), F32)
        lax.fori_loop(0, n_tot, trip, (zero_col, zero_col))
        dq_ref[...] = (dq_acc[...] * scale).astype(BF16)

        @pl.when(i == nq - 1)
        def _():
            for s in range(n_key_steps):
                dk_ref[s * KS:(s + 1) * KS, :] = dkt_acc[s].T.astype(BF16)
                dv_ref[s * KS:(s + 1) * KS, :] = dvt_acc[s].T.astype(BF16)

    return pl.pallas_call(
        body, name=name, grid=(H, nq),
        in_specs=[pl.BlockSpec((QB, HEAD_DIM), lambda h, i: (i, h)),
                  pl.BlockSpec((T, HEAD_DIM), lambda h, i: (0, H + h)),
                  pl.BlockSpec((T, HEAD_DIM), lambda h, i: (0, 2 * H + h)),
                  pl.BlockSpec((QB, HEAD_DIM), lambda h, i: (i, h)),
                  pl.BlockSpec((1, QB, LANES), lambda h, i: (h, i, 0))],
        out_specs=[pl.BlockSpec((QB, HEAD_DIM), lambda h, i: (i, h)),
                   pl.BlockSpec((T, HEAD_DIM), lambda h, i: (0, h)),
                   pl.BlockSpec((T, HEAD_DIM), lambda h, i: (0, h))],
        out_shape=[jax.ShapeDtypeStruct((T, D), BF16)] * 3,
        scratch_shapes=[pltpu.VMEM((n_key_steps, HEAD_DIM, KS), F32), pltpu.VMEM((n_key_steps, HEAD_DIM, KS), F32),
                        pltpu.VMEM((QB, HEAD_DIM), F32), pltpu.VMEM((3, QB, KS), F32), pltpu.VMEM((3, QB, KS), F32),
                        pltpu.VMEM((2, QB, KS), F32), pltpu.VMEM((2, QB, KS), F32), pltpu.VMEM((ns + 1, QB, KS), F32)],
        compiler_params=_params(2),
    )(qkv, qkv, qkv, do, r_tot)


def _onorm_gate_bwd(dy, o, p, z_col, o_norm, name):
    T, D = o.shape
    H = D // HEAD_DIM
    tm = _pick(T, (256, 128, 64))

    def body(dy_ref, o_ref, z_ref, w_ref, do_ref, dz_ref, dw_ref):
        _, vjp = jax.vjp(functools.partial(_onorm_gate_math, n_heads=H), o_ref[...], z_ref[...], w_ref[...])
        do, dz, dw = vjp(dy_ref[...])
        do_ref[...] = do
        dz_ref[...] = dz.astype(BF16)

        @pl.when(pl.program_id(0) == 0)
        def _():
            dw_ref[...] = jnp.zeros_like(dw_ref)

        dw_ref[...] += jnp.broadcast_to(dw, dw_ref.shape)

    return pl.pallas_call(
        body, name=name, grid=(T // tm,),
        in_specs=[pl.BlockSpec((tm, D), lambda i: (i, 0)), pl.BlockSpec((tm, D), lambda i: (i, 0)),
                  pl.BlockSpec((tm, D), lambda i: (i, z_col)), pl.BlockSpec((1, HEAD_DIM), lambda i: (0, 0))],
        out_specs=[pl.BlockSpec((tm, D), lambda i: (i, 0)), pl.BlockSpec((tm, D), lambda i: (i, 0)),
                   pl.BlockSpec((8, HEAD_DIM), lambda i: (0, 0))],
        out_shape=[jax.ShapeDtypeStruct((T, D), F32), jax.ShapeDtypeStruct((T, D), BF16),
                   jax.ShapeDtypeStruct((8, HEAD_DIM), F32)],
        compiler_params=_params(1),
    )(dy, o, p, o_norm)


def _row_to_col(row):
    C = row.shape[1]
    eye = _iota2((C, C), 0) == _iota2((C, C), 1)
    return jnp.sum(jnp.where(eye, row, 0.0), axis=1, keepdims=True)


def _lane_sum(x):
    return jnp.sum(x, axis=-1, keepdims=True)


def _chunk_head_bwd(q, k, v, gc, beta, s_in, t_inv, do, ds_out):
    C = q.shape[0]
    r, c = _iota2((C, C), 0), _iota2((C, C), 1)
    causal, strict = r >= c, r > c
    decay = jnp.where(causal, jnp.exp(jnp.where(causal, gc - _col_to_row(gc), 0.0)), 0.0)
    kb, vb = k * beta, v * beta
    eg = jnp.exp(gc)
    kbg = kb * eg
    g_last = gc[C - 1:C, :]
    e_tail = jnp.exp(g_last - gc)
    k_tail = k * e_tail
    gl = jnp.exp(g_last)
    qg = q * eg
    t_inv_t = t_inv.T
    kk = _dot_nt(kb, k)
    u = _dot(t_inv, vb)
    w = _dot(t_inv, kbg)
    qk = _dot_nt(q, k)
    d_qg = _dot_nt(do, s_in)
    ds_state = _dot_tn(qg, do)
    yield
    low = jnp.where(strict, kk * decay, 0.0)
    attn = qk * decay
    w_state = _dot(w, s_in)
    d_vnew_intra = _dot_tn(attn, do)
    d_vnew_state = _dot(k_tail, ds_out)
    yield
    v_new = u - w_state
    d_vnew = d_vnew_intra + d_vnew_state
    d_ktail = _dot_nt(v_new, ds_out)
    d_attn_raw = _dot_nt(do, v_new)
    d_w = -_dot_nt(d_vnew, s_in)
    ds_w = _dot_tn(w, d_vnew)
    d_vb = _dot(t_inv_t, d_vnew)
    d_tinv_u = _dot_nt(d_vnew, vb)
    yield
    d_gl = jnp.sum(_lane_sum(s_in * ds_out), axis=0, keepdims=True)
    d_attn = jnp.where(causal, d_attn_raw, 0.0)
    ds_in = ds_out * gl + ds_state - ds_w
    d_kbg = _dot(t_inv_t, d_w)
    d_tinv_w = _dot_nt(d_w, kbg)
    d_qk = d_attn * decay
    dq_intra = _dot(d_qk, k)
    dk_intra = _dot_tn(d_qk, q)
    yield
    inner = _dot(t_inv_t, d_tinv_u + d_tinv_w)
    yield
    d_low_raw = _dot_nt(inner, t_inv)
    yield
    d_low = jnp.where(strict, -d_low_raw, 0.0)
    d_kk = d_low * decay
    d_kb_low = _dot(d_kk, k)
    dk_low = _dot_tn(d_kk, kb)
    yield
    d_kb = d_kb_low + d_kbg * eg
    dq = dq_intra + d_qg * eg
    dk = dk_low + dk_intra + d_ktail * e_tail + d_kb * beta
    dv = d_vb * beta
    dbeta = _lane_sum(d_kb * k) + _lane_sum(d_vb * v)
    m = d_low * low + d_attn * attn
    tail_term = _lane_sum(d_ktail * k_tail)
    d_g_last = d_gl * gl + jnp.sum(tail_term, axis=0, keepdims=True)
    dgc = (_lane_sum(m) - _row_to_col(jnp.sum(m, axis=0, keepdims=True))
           + _lane_sum(d_qg * qg) + _lane_sum(d_kbg * kbg) - tail_term)
    dgc = dgc + jnp.where(_iota2((C, 1), 0) == C - 1, d_g_last, 0.0)
    return dq, dk, dv, dgc, dbeta, ds_in


def _chunk_bwd(qkv, beta, gc, s_all, t_all, do, d_model, name):
    T = qkv.shape[0]
    D = d_model
    H = D // HEAD_DIM
    N = T // CHUNK

    def body(q_ref, k_ref, v_ref, beta_ref, gc_ref, s_ref, t_ref, do_ref, dqkv_ref, dbeta_ref, dg_ref, ds_ref):
        @pl.when(pl.program_id(0) == 0)
        def _():
            ds_ref[...] = jnp.zeros_like(ds_ref)

        lane = _iota2((CHUNK, LANES), 1)
        dgc_all = jnp.zeros((CHUNK, LANES), F32)
        dbeta_all = jnp.zeros((CHUNK, LANES), F32)
        results = _lockstep(
            _chunk_head_bwd(q_ref[:, hs], k_ref[:, hs], v_ref[:, hs], gc_ref[:, h:h + 1], beta_ref[:, h:h + 1],
                            s_ref[0, h], t_ref[0, h], do_ref[:, hs], ds_ref[h])
            for h, hs in enumerate(slice(h * HEAD_DIM, (h + 1) * HEAD_DIM) for h in range(H)))
        for h, (dq, dk, dv, dgc, dbeta, ds_in) in enumerate(results):
            ds_ref[h] = ds_in
            dqkv_ref[:, h * HEAD_DIM:(h + 1) * HEAD_DIM] = dq
            dqkv_ref[:, D + h * HEAD_DIM:D + (h + 1) * HEAD_DIM] = dk
            dqkv_ref[:, 2 * D + h * HEAD_DIM:2 * D + (h + 1) * HEAD_DIM] = dv
            dgc_all = jnp.where(lane == h, dgc, dgc_all)
            dbeta_all = jnp.where(lane == h, dbeta, dbeta_all)
        dbeta_ref[...] = dbeta_all
        dg_ref[...] = _dot_mask(_chunk_tri(CHUNK, upper=True), dgc_all)

    rev = lambda n: N - 1 - n
    return pl.pallas_call(
        body, name=name, grid=(N,),
        in_specs=[pl.BlockSpec((CHUNK, D), lambda n: (rev(n), 0)), pl.BlockSpec((CHUNK, D), lambda n: (rev(n), 1)),
                  pl.BlockSpec((CHUNK, D), lambda n: (rev(n), 2)),
                  pl.BlockSpec((CHUNK, LANES), lambda n: (rev(n), 0)), pl.BlockSpec((CHUNK, LANES), lambda n: (rev(n), 0)),
                  pl.BlockSpec((1, H, HEAD_DIM, HEAD_DIM), lambda n: (rev(n), 0, 0, 0)),
                  pl.BlockSpec((1, H, CHUNK, CHUNK), lambda n: (rev(n), 0, 0, 0)),
                  pl.BlockSpec((CHUNK, D), lambda n: (rev(n), 0))],
        out_specs=[pl.BlockSpec((CHUNK, 3 * D), lambda n: (rev(n), 0)),
                   pl.BlockSpec((CHUNK, LANES), lambda n: (rev(n), 0)), pl.BlockSpec((CHUNK, LANES), lambda n: (rev(n), 0))],
        out_shape=[jax.ShapeDtypeStruct((T, 3 * D), F32), jax.ShapeDtypeStruct((T, LANES), F32),
                   jax.ShapeDtypeStruct((T, LANES), F32)],
        scratch_shapes=[pltpu.VMEM((H, HEAD_DIM, HEAD_DIM), F32)], compiler_params=_params(1),
    )(qkv, qkv, qkv, beta, gc, s_all, t_all, do)


def _gates_bwd(p, gate_params, col0, dbeta, dg, name):
    T = p.shape[0]
    tm = _pick(T, (256, 128, 64))

    def body(pb_ref, pa_ref, gp_ref, dbeta_ref, dg_ref, dp_ref, dgp_ref):
        gp = gp_ref[...]
        _, vjp = jax.vjp(_gates_math, pb_ref[...], pa_ref[...], gp[0:1, :], gp[1:2, :])
        dpb, dpa, d_alog, d_dt = vjp((dbeta_ref[...], dg_ref[...]))
        dp_ref[:, 0:LANES] = dpb.astype(BF16)
        dp_ref[:, LANES:2 * LANES] = dpa.astype(BF16)

        @pl.when(pl.program_id(0) == 0)
        def _():
            dgp_ref[...] = jnp.zeros_like(dgp_ref)

        dgp_ref[0:1, :] += d_alog
        dgp_ref[1:2, :] += d_dt

    return pl.pallas_call(
        body, name=name, grid=(T // tm,),
        in_specs=[pl.BlockSpec((tm, LANES), lambda i: (i, col0)), pl.BlockSpec((tm, LANES), lambda i: (i, col0 + 1)),
                  pl.BlockSpec((8, LANES), lambda i: (0, 0)),
                  pl.BlockSpec((tm, LANES), lambda i: (i, 0)), pl.BlockSpec((tm, LANES), lambda i: (i, 0))],
        out_specs=[pl.BlockSpec((tm, 2 * LANES), lambda i: (i, 0)), pl.BlockSpec((8, LANES), lambda i: (0, 0))],
        out_shape=[jax.ShapeDtypeStruct((T, 2 * LANES), BF16), jax.ShapeDtypeStruct((8, LANES), F32)],
        compiler_params=_params(1),
    )(p, p, gate_params, dbeta, dg)


def _conv_bwd_act(p, conv_w, dqkv, d_model, name):
    T = p.shape[0]
    D = d_model
    H = D // HEAD_DIM
    tm = _pick(T, (256, 128, 64))

    def body(cur_ref, prev_ref, w_ref, dout_ref, dc_ref, dw_ref):
        j, i = pl.program_id(0), pl.program_id(1)
        prev = prev_ref[...] * (i > 0).astype(F32)
        taps = _conv_taps(cur_ref[...], prev)
        w = w_ref[...]
        c = sum(taps[k] * w[k:k + 1, :] for k in range(4))
        _, vjp = jax.vjp(lambda cc: _qkv_post(cc, j, H), c)
        (dc,) = vjp(dout_ref[...])
        dc_ref[...] = dc

        @pl.when(i == 0)
        def _():
            dw_ref[...] = jnp.zeros_like(dw_ref)

        for k in range(4):
            dw_ref[k:k + 1, :] += jnp.sum(dc * taps[k], axis=0, keepdims=True)

    return pl.pallas_call(
        body, name=name, grid=(3, T // tm),
        in_specs=[pl.BlockSpec((tm, D), lambda j, i: (i, j)),
                  pl.BlockSpec((8, D), lambda j, i: (jnp.maximum(i * (tm // 8) - 1, 0), j)),
                  pl.BlockSpec((4, D), lambda j, i: (0, j)),
                  pl.BlockSpec((tm, D), lambda j, i: (i, j))],
        out_specs=[pl.BlockSpec((tm, D), lambda j, i: (i, j)), pl.BlockSpec((4, D), lambda j, i: (0, j))],
        out_shape=[jax.ShapeDtypeStruct((T, 3 * D), F32), jax.ShapeDtypeStruct((4, 3 * D), F32)],
        compiler_params=_params(2),
    )(p, p, conv_w, dqkv)


def _conv_bwd_input(dc, conv_w, name):
    T, D3 = dc.shape
    D = D3 // 3
    tm = _pick(T, (256, 128, 64))
    n_t = T // tm

    def body(cur_ref, next_ref, w_ref, dp_ref):
        i = pl.program_id(0)
        cur = cur_ref[...]
        nxt = next_ref[...] * (i < n_t - 1).astype(F32)
        ext = jnp.concatenate([cur, nxt], axis=0)
        w = w_ref[...]
        acc = cur * w[3:4, :]
        for s in (1, 2, 3):
            acc = acc + pltpu.roll(ext, tm + 8 - s, 0)[0:tm] * w[3 - s:4 - s, :]
        dp_ref[...] = acc.astype(BF16)

    return pl.pallas_call(
        body, name=name, grid=(n_t, 3),
        in_specs=[pl.BlockSpec((tm, D), lambda i, j: (i, j)),
                  pl.BlockSpec((8, D), lambda i, j: (jnp.minimum((i + 1) * (tm // 8), T // 8 - 1), j)),
                  pl.BlockSpec((4, D), lambda i, j: (0, j))],
        out_specs=pl.BlockSpec((tm, D), lambda i, j: (i, j)),
        out_shape=jax.ShapeDtypeStruct((T, D3), BF16), compiler_params=_params(2),
    )(dc, dc, conv_w)


def _exchange(arrays, gather, name):
    n = len(arrays)

    def body(*refs):
        ins, outs = refs[:n], refs[n:2 * n]
        send_sems, recv_sems, local_sems = refs[2 * n:]
        x, y, c = lax.axis_index("x"), lax.axis_index("y"), lax.axis_index("c")
        me = 4 * x + 2 * y + c

        def peer(k):
            px = 1 - x if k & 4 else x
            py = 1 - y if k & 2 else y
            pc = 1 - c if k & 1 else c
            return (px, py, pc), 4 * px + 2 * py + pc

        def remote(a, k):
            dev, idx = peer(k)
            src = ins[a] if gather else ins[a].at[idx]
            return pltpu.make_async_remote_copy(
                src_ref=src, dst_ref=outs[a].at[me], send_sem=send_sems.at[a * 7 + k - 1],
                recv_sem=recv_sems.at[a * 7 + k - 1], device_id=dev, device_id_type=MESH_ID)

        def arrival(a, k):
            dev, idx = peer(k)
            src = ins[a] if gather else ins[a].at[idx]
            return pltpu.make_async_remote_copy(
                src_ref=src, dst_ref=outs[a].at[idx], send_sem=send_sems.at[a * 7 + k - 1],
                recv_sem=recv_sems.at[a * 7 + k - 1], device_id=dev, device_id_type=MESH_ID)

        local = [pltpu.make_async_copy(ins[a] if gather else ins[a].at[me], outs[a].at[me], local_sems.at[a])
                 for a in range(n)]
        sends = [remote(a, k) for k in range(1, 8) for a in range(n)]
        for cp in local + sends:
            cp.start()
        for k in range(1, 8):
            for a in range(n):
                arrival(a, k).wait_recv()
        for cp in sends:
            cp.wait_send()
        for cp in local:
            cp.wait()

    out_shape = [jax.ShapeDtypeStruct((N_DEV,) + a.shape if gather else a.shape, a.dtype) for a in arrays]
    any_spec = pl.BlockSpec(memory_space=pl.ANY)
    return pl.pallas_call(
        body, name=name, in_specs=[any_spec] * n, out_specs=[any_spec] * n, out_shape=out_shape,
        scratch_shapes=[pltpu.SemaphoreType.DMA((7 * n,)), pltpu.SemaphoreType.DMA((7 * n,)), pltpu.SemaphoreType.DMA((n,))],
        compiler_params=pltpu.CompilerParams(has_side_effects=True),
    )(*arrays)


def _reduce_adamw(recv, w, m, v, name):
    _, R, C = recv.shape
    tr = next((t for t in (256, 128, 64, 32, 16, 8) if R % t == 0), R)
    c1 = 1.0 - ADAM_B1 ** ADAM_STEP
    c2 = 1.0 - ADAM_B2 ** ADAM_STEP

    def body(r_ref, w_ref, m_ref, v_ref, g_ref, d_ref, nm_ref, nv_ref):
        g = r_ref[0]
        for s in range(1, N_DEV):
            g = g + r_ref[s]
        nm = ADAM_B1 * m_ref[...] + (1.0 - ADAM_B1) * g
        nv = ADAM_B2 * v_ref[...] + (1.0 - ADAM_B2) * (g * g)
        g_ref[...] = g
        nm_ref[...] = nm
        nv_ref[...] = nv
        d_ref[...] = -ADAM_LR * ((nm / c1) / (jnp.sqrt(nv / c2) + ADAM_EPS) + ADAM_WD * w_ref[...])

    spec = pl.BlockSpec((tr, C), lambda i: (i, 0))
    return pl.pallas_call(
        body, name=name, grid=(R // tr,),
        in_specs=[pl.BlockSpec((N_DEV, tr, C), lambda i: (0, i, 0)), spec, spec, spec], out_specs=[spec] * 4,
        out_shape=[jax.ShapeDtypeStruct((R, C), F32)] * 4, compiler_params=_params(1),
    )(recv, w, m, v)


def _forward_local(x, target, nw0, nw1, fw, wa_in, conv_w, gate_params, o_norm, wa_out, wb_in, wb_out):
    T, D = x.shape
    nD = D // LANES
    sv = {}
    sv["u0"] = _rmsnorm_fwd(x, nw0, "a_norm_fwd")
    sv["pa"] = _mm_nn(sv["u0"], wa_in, "a_in_proj")
    sv["qkv_a"] = _conv_fwd(sv["pa"], conv_w, D, "a_conv_fwd")
    sv["beta"], sv["gc"] = _gates_fwd(sv["pa"], gate_params, 4 * nD, "a_gates_fwd")
    sv["o_a"], sv["s_all"], sv["t_all"] = _chunk_fwd(sv["qkv_a"], sv["beta"], sv["gc"], D, "a_chunk_fwd")
    sv["y_a"] = _onorm_gate_fwd(sv["o_a"], sv["pa"], 3, o_norm, "a_onorm_fwd")
    sv["h1"] = _mm_nn(sv["y_a"], wa_out, "a_out_proj", add=x)
    sv["u1"] = _rmsnorm_fwd(sv["h1"], nw1, "b_norm_fwd")
    sv["qkv_b"] = _mm_nn(sv["u1"], wb_in[:, :3 * D], "b_in_proj_qkv", out_dtype=BF16)
    sv["gate_b"] = _mm_nn(sv["u1"], wb_in[:, 3 * D:], "b_in_proj_gate")
    sv["o_b"], sv["r_b"] = _sb_fwd(sv["qkv_b"], D, "b_attn_fwd")
    sv["y_b"] = _gate_mul_fwd(sv["o_b"], sv["gate_b"], "b_gate_fwd")
    sv["h2"] = _mm_nn(sv["y_b"], wb_out, "b_out_proj", add=sv["h1"])
    sv["dh2"], sv["loss"], sv["dfw"] = _final_loss(sv["h2"], fw, target, "final_loss")
    return sv


def _backward_local(sv, x, nw0, nw1, wa_in, conv_w, gate_params, o_norm, wa_out, wb_in, wb_out):
    T, D = x.shape
    nD = D // LANES
    g = {}
    dh2 = sv["dh2"]
    g["wb_out"] = _mm_tn(sv["y_b"], dh2, "b_out_proj_dw")
    dy_b = _mm_nt([(dh2, wb_out)], "b_out_proj_dx")
    do_b, dgate_b = _gate_mul_bwd(dy_b, sv["o_b"], sv["gate_b"], "b_gate_bwd")
    dq_b, dk_b, dv_b = _sb_bwd(sv["qkv_b"], do_b, sv["r_b"], D, "b_attn_bwd")
    dp_b = [dq_b, dk_b, dv_b, dgate_b]
    g["wb_in"] = jnp.concatenate([_mm_tn(sv["u1"], dp, "b_in_proj_dw%d" % c) for c, dp in enumerate(dp_b)], axis=1)
    du1 = _mm_nt([(dp, wb_in[:, c * D:(c + 1) * D]) for c, dp in enumerate(dp_b)], "b_in_proj_dx")
    dh1, g["nw1"] = _rmsnorm_bwd(sv["h1"], nw1, du1, dh2, "b_norm_bwd")
    g["wa_out"] = _mm_tn(sv["y_a"], dh1, "a_out_proj_dw")
    dy_a = _mm_nt([(dh1, wa_out)], "a_out_proj_dx")
    do_a, dz_a, g["o_norm"] = _onorm_gate_bwd(dy_a, sv["o_a"], sv["pa"], 3, o_norm, "a_onorm_bwd")
    dqkv_a, dbeta, dg = _chunk_bwd(sv["qkv_a"], sv["beta"], sv["gc"], sv["s_all"], sv["t_all"], do_a, D, "a_chunk_bwd")
    dp_gates, g["gate_params"] = _gates_bwd(sv["pa"], gate_params, 4 * nD, dbeta, dg, "a_gates_bwd")
    dc, g["conv_w"] = _conv_bwd_act(sv["pa"], conv_w, dqkv_a, D, "a_conv_bwd_act")
    dp_qkv = _conv_bwd_input(dc, conv_w, "a_conv_bwd_input")
    dp_a = [(dp_qkv, 0, 3 * D), (dz_a, 3 * D, 4 * D), (dp_gates, 4 * D, 4 * D + 2 * LANES)]
    g["wa_in"] = jnp.concatenate([_mm_tn(sv["u0"], dp, "a_in_proj_dw%d" % c) for c, (dp, _, _) in enumerate(dp_a)], axis=1)
    du0 = _mm_nt([(dp, wa_in[:, lo:hi]) for dp, lo, hi in dp_a], "a_in_proj_dx")
    g["x"], g["nw0"] = _rmsnorm_bwd(x, nw0, du0, dh1, "a_norm_bwd")
    g["fw"] = sv["dfw"]
    return g


def kernel(x, norm_w, a_w_in, a_conv_w, a_a_log, a_dt_bias, a_o_norm, a_w_out, b_w_in, b_w_out, final_norm_w, loss_target, m_norm_w, m_a_w_in, m_a_conv_w, m_a_a_log, m_a_dt_bias, m_a_o_norm, m_a_w_out, m_b_w_in, m_b_w_out, m_final_norm_w, v_norm_w, v_a_w_in, v_a_conv_w, v_a_a_log, v_a_dt_bias, v_a_o_norm, v_a_w_out, v_b_w_in, v_b_w_out, v_final_norm_w):
    D = x.shape[-1]
    H = D // HEAD_DIM
    shards = [a_w_in[0].astype(BF16), a_w_out[0].astype(BF16), b_w_in[0].astype(BF16), b_w_out[0].astype(BF16), a_conv_w[0]]
    ga_in, ga_out, gb_in, gb_out, g_conv = _exchange(shards, gather=True, name="weights_gather")
    wa = ga_in.transpose(1, 0, 2).reshape(D, -1)
    pad = lambda w: jnp.pad(w, ((0, 0), (0, LANES - w.shape[1])))
    wa_in = jnp.concatenate([wa[:, :4 * D], pad(wa[:, 4 * D:4 * D + H]), pad(wa[:, 4 * D + H:])], axis=1)
    wa_out = ga_out.reshape(D, D)
    wb_in = gb_in.transpose(1, 0, 2).reshape(D, 4 * D)
    wb_out = gb_out.reshape(D, D)
    conv_w = g_conv.transpose(1, 0, 2).reshape(4, 3 * D)
    gate_params = jnp.zeros((8, LANES), F32).at[0, :H].set(a_a_log[0]).at[1, :H].set(a_dt_bias[0])
    nw0, nw1, fw = norm_w[0:1], norm_w[1:2], final_norm_w[None]

    sv = _forward_local(x[0], loss_target[0], nw0, nw1, fw, wa_in, conv_w, gate_params, a_o_norm, wa_out, wb_in, wb_out)
    g = _backward_local(sv, x[0], nw0, nw1, wa_in, conv_w, gate_params, a_o_norm, wa_out, wb_in, wb_out)

    gwa = g["wa_in"]
    gwa = jnp.concatenate([gwa[:, :4 * D], gwa[:, 4 * D:4 * D + H], gwa[:, 4 * D + LANES:4 * D + LANES + H]], axis=1)
    row = lambda v: jnp.pad(v.reshape(1, -1), ((0, 0), (0, D - v.size)))
    small = jnp.concatenate([g["nw0"][0:1], g["nw1"][0:1], g["fw"][0:1], row(g["gate_params"][0, :H]),
                             row(g["gate_params"][1, :H]), row(g["o_norm"][0]), row(sv["loss"][0, 0:1]),
                             jnp.zeros((1, D), F32)], axis=0)
    contribs = [gwa.reshape(D, N_DEV, -1).transpose(1, 0, 2), g["wa_out"].reshape(N_DEV, D // N_DEV, D),
                g["wb_in"].reshape(D, N_DEV, -1).transpose(1, 0, 2), g["wb_out"].reshape(N_DEV, D // N_DEV, D),
                g["conv_w"].reshape(4, N_DEV, -1).transpose(1, 0, 2), jnp.broadcast_to(small[None], (N_DEV, 8, D))]
    ra_in, ra_out, rb_in, rb_out, r_conv, r_small = _exchange(contribs, gather=False, name="grads_exchange")

    outs = {}
    for nm, recv, w, m, v in (("a_w_in", ra_in, a_w_in, m_a_w_in, v_a_w_in), ("a_w_out", ra_out, a_w_out, m_a_w_out, v_a_w_out),
                              ("b_w_in", rb_in, b_w_in, m_b_w_in, v_b_w_in), ("b_w_out", rb_out, b_w_out, m_b_w_out, v_b_w_out),
                              ("a_conv_w", r_conv, a_conv_w, m_a_conv_w, v_a_conv_w)):
        outs[nm] = tuple(o[None] for o in _reduce_adamw(recv, w[0], m[0], v[0], "adamw_" + nm))

    def pack(nw, alog, dt, onorm, fnw):
        return jnp.concatenate([nw, fnw.reshape(1, D), row(alog), row(dt), row(onorm), jnp.zeros((2, D), F32)], axis=0)

    s_g, s_d, s_m, s_v = _reduce_adamw(
        r_small, pack(norm_w, a_a_log, a_dt_bias, a_o_norm, final_norm_w),
        pack(m_norm_w, m_a_a_log, m_a_dt_bias, m_a_o_norm, m_final_norm_w),
        pack(v_norm_w, v_a_a_log, v_a_dt_bias, v_a_o_norm, v_final_norm_w), "adamw_small")
    loss = s_g[6, 0]
    for i, s in enumerate((s_g, s_d, s_m, s_v)):
        outs.setdefault("norm_w", [None] * 4)[i] = s[0:2]
        outs.setdefault("final_norm_w", [None] * 4)[i] = s[2]
        outs.setdefault("a_a_log", [None] * 4)[i] = s[3:4, :H]
        outs.setdefault("a_dt_bias", [None] * 4)[i] = s[4:5, :H]
        outs.setdefault("a_o_norm", [None] * 4)[i] = s[5:6, :HEAD_DIM]
    names = ("norm_w", "a_w_in", "a_conv_w", "a_a_log", "a_dt_bias", "a_o_norm", "a_w_out", "b_w_in", "b_w_out", "final_norm_w")
    return (loss, g["x"][None]) + tuple(outs[n][i] for i in range(4) for n in names)
```

```python
import functools

import jax
import jax.numpy as jnp
from jax import lax
from jax.experimental import pallas as pl
from jax.experimental.pallas import tpu as pltpu

F32 = jnp.float32
BF16 = jnp.bfloat16
EPS = 1e-6
LOG2_E = 1.4426950408889634
MASKED_SCORE = -1e30
HEAD_DIM = 128
CHUNK = 64
ATTN_Q_BLOCKS_FWD = (512, 256)
ATTN_Q_BLOCKS_BWD = (512, 256)
ATTN_K_BLOCK = 128
LANES = 128
N_DEV = 8
VMEM_LIMIT_BYTES = 48 * 1024 * 1024
ADAM_LR, ADAM_B1, ADAM_B2, ADAM_EPS, ADAM_WD, ADAM_STEP = 0.001, 0.9, 0.999, 1e-08, 0.01, 10
MESH_ID = pl.DeviceIdType.MESH


def _pick(n, candidates):
    for c in candidates:
        if n % c == 0:
            return c
    raise ValueError(f"no tile for {n} in {candidates}")


def _params(n_grid_axes):
    return pltpu.CompilerParams(dimension_semantics=("arbitrary",) * n_grid_axes, vmem_limit_bytes=VMEM_LIMIT_BYTES)


def _dot(a, b):
    return jnp.dot(a.astype(BF16), b.astype(BF16), preferred_element_type=F32)


def _dot_nt(a, b):
    return lax.dot_general(a.astype(BF16), b.astype(BF16), (((1,), (1,)), ((), ())), preferred_element_type=F32)


def _dot_tn(a, b):
    return lax.dot_general(a.astype(BF16), b.astype(BF16), (((0,), (0,)), ((), ())), preferred_element_type=F32)


def _split2(x):
    hi = x.astype(BF16)
    lo = (x - hi.astype(F32)).astype(BF16)
    return hi, lo


def _split3(x):
    hi = x.astype(BF16)
    r = x - hi.astype(F32)
    mid = r.astype(BF16)
    lo = (r - mid.astype(F32)).astype(BF16)
    return hi, mid, lo


def _dot3(a, b):
    a_hi, a_lo = _split2(a)
    b_hi, b_lo = _split2(b)
    d = functools.partial(jnp.dot, preferred_element_type=F32)
    return d(a_hi, b_hi) + (d(a_hi, b_lo) + d(a_lo, b_hi))


def _silu(x):
    return x * jax.nn.sigmoid(x)


def _softplus(x):
    return jnp.maximum(x, 0.0) + jnp.log1p(jnp.exp(-jnp.abs(x)))


def _iota2(shape, axis):
    return lax.broadcasted_iota(jnp.int32, shape, axis)


def _rms_bwd_math(x, w, dy):
    r = lax.rsqrt(jnp.mean(x * x, axis=-1, keepdims=True) + EPS)
    xhat = x * r
    dxhat = dy * w
    dx = r * (dxhat - xhat * jnp.mean(dxhat * xhat, axis=-1, keepdims=True))
    dw = jnp.sum(dy * xhat, axis=0, keepdims=True)
    return dx, dw


def _rmsnorm_fwd(x, w, name):
    T, D = x.shape
    tm = _pick(T, (512, 256, 128))

    def body(x_ref, w_ref, o_ref):
        xf = x_ref[...]
        r = lax.rsqrt(jnp.mean(xf * xf, axis=-1, keepdims=True) + EPS)
        o_ref[...] = (xf * r * w_ref[...]).astype(BF16)

    return pl.pallas_call(
        body, name=name, grid=(T // tm,),
        in_specs=[pl.BlockSpec((tm, D), lambda i: (i, 0)), pl.BlockSpec((1, D), lambda i: (0, 0))],
        out_specs=pl.BlockSpec((tm, D), lambda i: (i, 0)),
        out_shape=jax.ShapeDtypeStruct((T, D), BF16), compiler_params=_params(1),
    )(x, w)


def _rmsnorm_bwd(x, w, du, dres, name):
    T, D = x.shape
    tm = _pick(T, (512, 256, 128))

    def body(x_ref, w_ref, du_ref, dres_ref, dx_ref, dw_ref):
        dx, dw = _rms_bwd_math(x_ref[...], w_ref[...], du_ref[...].astype(F32))
        dx_ref[...] = dres_ref[...] + dx

        @pl.when(pl.program_id(0) == 0)
        def _():
            dw_ref[...] = jnp.zeros_like(dw_ref)

        dw_ref[...] += jnp.broadcast_to(dw, dw_ref.shape)

    return pl.pallas_call(
        body, name=name, grid=(T // tm,),
        in_specs=[pl.BlockSpec((tm, D), lambda i: (i, 0)), pl.BlockSpec((1, D), lambda i: (0, 0)),
                  pl.BlockSpec((tm, D), lambda i: (i, 0)), pl.BlockSpec((tm, D), lambda i: (i, 0))],
        out_specs=[pl.BlockSpec((tm, D), lambda i: (i, 0)), pl.BlockSpec((8, D), lambda i: (0, 0))],
        out_shape=[jax.ShapeDtypeStruct((T, D), F32), jax.ShapeDtypeStruct((8, D), F32)],
        compiler_params=_params(1),
    )(x, w, du, dres)


def _mm_nn(a, b, name, add=None, out_dtype=F32):
    M, K = a.shape
    _, N = b.shape
    tm = _pick(M, (256, 128)) if N > 2048 else _pick(M, (512, 256, 128))

    def body(*refs):
        a_ref, b_ref = refs[0], refs[1]
        o_ref = refs[-1]
        acc = _dot(a_ref[...], b_ref[...])
        if add is not None:
            acc = acc + refs[2][...]
        o_ref[...] = acc.astype(out_dtype)

    in_specs = [pl.BlockSpec((tm, K), lambda i: (i, 0)), pl.BlockSpec((K, N), lambda i: (0, 0))]
    args = [a, b]
    if add is not None:
        in_specs.append(pl.BlockSpec((tm, N), lambda i: (i, 0)))
        args.append(add)
    return pl.pallas_call(
        body, name=name, grid=(M // tm,), in_specs=in_specs,
        out_specs=pl.BlockSpec((tm, N), lambda i: (i, 0)),
        out_shape=jax.ShapeDtypeStruct((M, N), out_dtype), compiler_params=_params(1),
    )(*args)


def _mm_nt(pairs, name):
    M = pairs[0][0].shape[0]
    N = pairs[0][1].shape[0]
    n = len(pairs)
    tm = _pick(M, (512, 256, 128))

    def body(*refs):
        acc = _dot_nt(refs[0][...], refs[n][...])
        for p in range(1, n):
            acc = acc + _dot_nt(refs[p][...], refs[n + p][...])
        refs[-1][...] = acc

    in_specs = ([pl.BlockSpec((tm, a.shape[1]), lambda i: (i, 0)) for a, _ in pairs]
                + [pl.BlockSpec(b.shape, lambda i: (0, 0)) for _, b in pairs])
    return pl.pallas_call(
        body, name=name, grid=(M // tm,), in_specs=in_specs,
        out_specs=pl.BlockSpec((tm, N), lambda i: (i, 0)),
        out_shape=jax.ShapeDtypeStruct((M, N), F32), compiler_params=_params(1),
    )(*[a for a, _ in pairs], *[b for _, b in pairs])


def _mm_tn(a, b, name):
    R, M = a.shape
    _, N = b.shape
    tn = _pick(N, (1536, 1024, 512, 256, 128))
    tr = _pick(R, (512, 256, 128))

    def body(a_ref, b_ref, o_ref):
        @pl.when(pl.program_id(1) == 0)
        def _():
            o_ref[...] = jnp.zeros_like(o_ref)

        o_ref[...] += _dot_tn(a_ref[...], b_ref[...])

    return pl.pallas_call(
        body, name=name, grid=(N // tn, R // tr),
        in_specs=[pl.BlockSpec((tr, M), lambda j, r: (r, 0)), pl.BlockSpec((tr, tn), lambda j, r: (r, j))],
        out_specs=pl.BlockSpec((M, tn), lambda j, r: (0, j)),
        out_shape=jax.ShapeDtypeStruct((M, N), F32), compiler_params=_params(2),
    )(a, b)


def _qkv_post(c, j, n_heads):
    s = _silu(c)
    parts = []
    for h in range(n_heads):
        sh = s[:, h * HEAD_DIM:(h + 1) * HEAD_DIM]
        parts.append(sh * lax.rsqrt(jnp.sum(sh * sh, axis=-1, keepdims=True) + EPS))
    n = jnp.concatenate(parts, axis=-1)
    is_q = (j == 0).astype(F32)
    is_v = (j == 2).astype(F32)
    n = n * (1.0 + is_q * (HEAD_DIM ** -0.5 - 1.0))
    return n * (1.0 - is_v) + s * is_v


def _conv_taps(cur, halo_prev):
    tm = cur.shape[0]
    ext = jnp.concatenate([halo_prev, cur], axis=0)
    taps = [pltpu.roll(ext, s, 0)[8:8 + tm] for s in (3, 2, 1)]
    return taps + [cur]


def _conv_fwd(p, conv_w, d_model, name):
    T = p.shape[0]
    D = d_model
    H = D // HEAD_DIM
    tm = _pick(T, (256, 128, 64))

    def body(cur_ref, prev_ref, w_ref, o_ref):
        i, j = pl.program_id(0), pl.program_id(1)
        prev = prev_ref[...] * (i > 0).astype(F32)
        taps = _conv_taps(cur_ref[...], prev)
        w = w_ref[...]
        c = sum(taps[k] * w[k:k + 1, :] for k in range(4))
        o_ref[...] = _qkv_post(c, j, H)

    return pl.pallas_call(
        body, name=name, grid=(T // tm, 3),
        in_specs=[pl.BlockSpec((tm, D), lambda i, j: (i, j)),
                  pl.BlockSpec((8, D), lambda i, j: (jnp.maximum(i * (tm // 8) - 1, 0), j)),
                  pl.BlockSpec((4, D), lambda i, j: (0, j))],
        out_specs=pl.BlockSpec((tm, D), lambda i, j: (i, j)),
        out_shape=jax.ShapeDtypeStruct((T, 3 * D), F32), compiler_params=_params(2),
    )(p, p, conv_w)


def _chunk_tri(tm, upper):
    r, c = _iota2((tm, tm), 0), _iota2((tm, tm), 1)
    same = (r // CHUNK) == (c // CHUNK)
    tri = (c >= r) if upper else (c <= r)
    return jnp.where(same & tri, 1.0, 0.0).astype(BF16)


def _dot_mask(mask_bf16, x):
    hi, mid, lo = _split3(x)
    d = functools.partial(jnp.dot, preferred_element_type=F32)
    return d(mask_bf16, hi) + (d(mask_bf16, mid) + d(mask_bf16, lo))


def _gates_math(pb, pa, a_log, dt_bias):
    beta = jax.nn.sigmoid(pb)
    g = -jnp.exp(a_log) * _softplus(pa + dt_bias)
    return beta, g


def _gates_fwd(p, gate_params, col0, name):
    T = p.shape[0]
    tm = _pick(T, (256, 128, 64))

    def body(pb_ref, pa_ref, gp_ref, beta_ref, gc_ref):
        gp = gp_ref[...]
        beta, g = _gates_math(pb_ref[...], pa_ref[...], gp[0:1, :], gp[1:2, :])
        beta_ref[...] = beta
        gc_ref[...] = _dot_mask(_chunk_tri(tm, upper=False), g)

    return pl.pallas_call(
        body, name=name, grid=(T // tm,),
        in_specs=[pl.BlockSpec((tm, LANES), lambda i: (i, col0)), pl.BlockSpec((tm, LANES), lambda i: (i, col0 + 1)),
                  pl.BlockSpec((8, LANES), lambda i: (0, 0))],
        out_specs=[pl.BlockSpec((tm, LANES), lambda i: (i, 0))] * 2,
        out_shape=[jax.ShapeDtypeStruct((T, LANES), F32)] * 2, compiler_params=_params(1),
    )(p, p, gate_params)


def _col_to_row(col):
    C = col.shape[0]
    eye = _iota2((C, C), 0) == _iota2((C, C), 1)
    return jnp.sum(jnp.where(eye, col, 0.0), axis=0, keepdims=True)


def _lockstep(generators):
    generators = list(generators)
    results = [None] * len(generators)
    live = list(range(len(generators)))
    while live:
        for idx in list(live):
            try:
                next(generators[idx])
            except StopIteration as done:
                results[idx] = done.value
                live.remove(idx)
    return results


def _unit_lower_inverse(low):
    C = low.shape[0]
    eye = (_iota2((C, C), 0) == _iota2((C, C), 1)).astype(F32)
    t = eye - low
    p = _dot3(low, low)
    yield
    n = 2
    while True:
        tp = _dot3(t, p)
        n *= 2
        if n < C:
            p = _dot3(p, p)
        yield
        t = t + tp
        if n >= C:
            return t


def _chunk_head_fwd(q, k, v, gc, beta, s_in):
    C = q.shape[0]
    r, c = _iota2((C, C), 0), _iota2((C, C), 1)
    causal, strict = r >= c, r > c
    decay = jnp.where(causal, jnp.exp(jnp.where(causal, gc - _col_to_row(gc), 0.0)), 0.0)
    kb, vb = k * beta, v * beta
    eg = jnp.exp(gc)
    kk = _dot_nt(kb, k)
    qk = _dot_nt(q, k)
    o_state = _dot(q * eg, s_in)
    yield
    t_inv = yield from _unit_lower_inverse(jnp.where(strict, kk * decay, 0.0))
    u = _dot(t_inv, vb)
    w = _dot(t_inv, kb * eg)
    yield
    w_state = _dot(w, s_in)
    yield
    v_new = u - w_state
    g_last = gc[C - 1:C, :]
    o_intra = _dot(qk * decay, v_new)
    s_add = _dot_tn(k * jnp.exp(g_last - gc), v_new)
    yield
    return o_state + o_intra, s_in * jnp.exp(g_last) + s_add, t_inv


def _chunk_fwd(qkv, beta, gc, d_model, name):
    T = qkv.shape[0]
    D = d_model
    H = D // HEAD_DIM
    N = T // CHUNK

    def body(q_ref, k_ref, v_ref, beta_ref, gc_ref, o_ref, s_all_ref, t_all_ref, s_ref):
        @pl.when(pl.program_id(0) == 0)
        def _():
            s_ref[...] = jnp.zeros_like(s_ref)

        heads = [slice(h * HEAD_DIM, (h + 1) * HEAD_DIM) for h in range(H)]
        s_all_ref[0] = s_ref[...]
        results = _lockstep(_chunk_head_fwd(q_ref[:, hs], k_ref[:, hs], v_ref[:, hs], gc_ref[:, h:h + 1],
                                            beta_ref[:, h:h + 1], s_ref[h]) for h, hs in enumerate(heads))
        for h, (o, s_out, t_inv) in enumerate(results):
            o_ref[:, heads[h]] = o
            s_ref[h] = s_out
            t_all_ref[0, h] = t_inv

    return pl.pallas_call(
        body, name=name, grid=(N,),
        in_specs=[pl.BlockSpec((CHUNK, D), lambda n: (n, 0)), pl.BlockSpec((CHUNK, D), lambda n: (n, 1)),
                  pl.BlockSpec((CHUNK, D), lambda n: (n, 2)),
                  pl.BlockSpec((CHUNK, LANES), lambda n: (n, 0)), pl.BlockSpec((CHUNK, LANES), lambda n: (n, 0))],
        out_specs=[pl.BlockSpec((CHUNK, D), lambda n: (n, 0)),
                   pl.BlockSpec((1, H, HEAD_DIM, HEAD_DIM), lambda n: (n, 0, 0, 0)),
                   pl.BlockSpec((1, H, CHUNK, CHUNK), lambda n: (n, 0, 0, 0))],
        out_shape=[jax.ShapeDtypeStruct((T, D), F32), jax.ShapeDtypeStruct((N, H, HEAD_DIM, HEAD_DIM), F32),
                   jax.ShapeDtypeStruct((N, H, CHUNK, CHUNK), F32)],
        scratch_shapes=[pltpu.VMEM((H, HEAD_DIM, HEAD_DIM), F32)], compiler_params=_params(1),
    )(qkv, qkv, qkv, beta, gc)


def _onorm_gate_math(o, z, w, n_heads):
    parts = []
    for h in range(n_heads):
        hs = slice(h * HEAD_DIM, (h + 1) * HEAD_DIM)
        oh = o[:, hs]
        y = oh * lax.rsqrt(jnp.mean(oh * oh, axis=-1, keepdims=True) + EPS) * w
        parts.append(y * _silu(z[:, hs]))
    return jnp.concatenate(parts, axis=-1)


def _onorm_gate_fwd(o, p, z_col, o_norm, name):
    T, D = o.shape
    H = D // HEAD_DIM
    tm = _pick(T, (256, 128, 64))

    def body(o_ref, z_ref, w_ref, y_ref):
        y_ref[...] = _onorm_gate_math(o_ref[...], z_ref[...], w_ref[...], H).astype(BF16)

    return pl.pallas_call(
        body, name=name, grid=(T // tm,),
        in_specs=[pl.BlockSpec((tm, D), lambda i: (i, 0)), pl.BlockSpec((tm, D), lambda i: (i, z_col)),
                  pl.BlockSpec((1, HEAD_DIM), lambda i: (0, 0))],
        out_specs=pl.BlockSpec((tm, D), lambda i: (i, 0)),
        out_shape=jax.ShapeDtypeStruct((T, D), BF16), compiler_params=_params(1),
    )(o, p, o_norm)


def _diag_mask(qb, kb, d):
    return _iota2((qb, kb), 0) > _iota2((qb, kb), 1) + d * kb


def _fill_score_masks(mask_buf, qb, kb, ns):
    mask_buf[0] = jnp.zeros(mask_buf.shape[1:], F32)
    for d in range(ns):
        for half in range(2):
            mask_buf[d + 1, :, half * kb:(half + 1) * kb] = jnp.where(_diag_mask(qb, kb, 2 * d + half), 0.0, MASKED_SCORE)


def _softplus_bits(w):
    u = 1.0 + jnp.exp2(jnp.minimum(w, 64.0))
    return jnp.maximum(w, jnp.log2(u)), 1.0 / u


def _incl_lower(n):
    return jnp.where((_iota2((2 * n, n), 0) & (n - 1)) >= _iota2((2 * n, n), 1), 1.0, 0.0).astype(BF16)


def _incl_upper(n):
    return jnp.where((_iota2((2 * n, n), 0) & (n - 1)) <= _iota2((2 * n, n), 1), 1.0, 0.0).astype(BF16)


def _dot_cum(x, tri_bf16):
    hi, lo = _split2(x)
    return jnp.dot(jnp.concatenate([hi, lo], axis=1), tri_bf16, preferred_element_type=F32)


def _sb_fwd(qkv, d_model, name):
    T = qkv.shape[0]
    D = d_model
    H = D // HEAD_DIM
    QB = _pick(T, ATTN_Q_BLOCKS_FWD)
    KB = ATTN_K_BLOCK
    KS = 2 * KB
    ns = QB // KS
    nq = T // QB
    scale = HEAD_DIM ** -0.5

    def body(q_ref, k_ref, v_ref, o_ref, r_ref, w_buf, cum_buf, mask_buf):
        i = pl.program_id(1)

        @pl.when(i == 0)
        def _():
            _fill_score_masks(mask_buf, QB, KB, ns)

        q = q_ref[...]
        tri = _incl_lower(KB)
        n_tot = (i + 1) * ns

        def key_step(m):
            return jnp.maximum(n_tot - 1 - m, 0)

        def rows(ref, s):
            return ref[pl.ds(pl.multiple_of(s * KS, KS), KS), :]

        def scores(s):
            return _dot_nt(q, rows(k_ref, s)) * (scale * LOG2_E) + mask_buf[jnp.maximum(s - i * ns + 1, 0)]

        def cums(w):
            sp = _softplus_bits(w)[0]
            return jnp.concatenate([_dot_cum(sp[:, :KB], tri), _dot_cum(sp[:, KB:], tri)], axis=1)

        def weights(w, cum, carry):
            a_r = jnp.exp2(w[:, KB:] - cum[:, KB:] - carry)
            carry = carry + cum[:, KB:KB + 1]
            a_l = jnp.exp2(w[:, :KB] - cum[:, :KB] - carry)
            return jnp.concatenate([a_l, a_r], axis=1).astype(BF16), carry + cum[:, 0:1]

        def trip(m, carry):
            w_new = scores(key_step(m + 2))
            a, carry = weights(w_buf[m % 3], cum_buf[m % 2], carry)
            o_ref[...] += _dot(a, rows(v_ref, key_step(m)))
            cum_buf[(m + 1) % 2] = cums(w_buf[(m + 1) % 3])
            w_buf[(m + 2) % 3] = w_new
            return carry

        o_ref[...] = jnp.zeros_like(o_ref)
        w_buf[0] = scores(key_step(0))
        w_buf[1] = scores(key_step(1))
        cum_buf[0] = cums(w_buf[0])
        carry = lax.fori_loop(0, n_tot // 2, lambda j, c: trip(2 * j + 1, trip(2 * j, c)), jnp.zeros((QB, 1), F32))
        r_ref[0] = jnp.broadcast_to(carry, (QB, LANES))

    return pl.pallas_call(
        body, name=name, grid=(H, nq),
        in_specs=[pl.BlockSpec((QB, HEAD_DIM), lambda h, i: (i, h)),
                  pl.BlockSpec((T, HEAD_DIM), lambda h, i: (0, H + h)),
                  pl.BlockSpec((T, HEAD_DIM), lambda h, i: (0, 2 * H + h))],
        out_specs=[pl.BlockSpec((QB, HEAD_DIM), lambda h, i: (i, h)),
                   pl.BlockSpec((1, QB, LANES), lambda h, i: (h, i, 0))],
        out_shape=[jax.ShapeDtypeStruct((T, D), F32), jax.ShapeDtypeStruct((H, T, LANES), F32)],
        scratch_shapes=[pltpu.VMEM((3, QB, KS), F32), pltpu.VMEM((2, QB, KS), F32), pltpu.VMEM((ns + 1, QB, KS), F32)],
        compiler_params=_params(2),
    )(qkv, qkv, qkv)


def _gate_mul_fwd(o, gate, name):
    T, D = o.shape
    tm = _pick(T, (512, 256, 128))

    def body(o_ref, g_ref, y_ref):
        y_ref[...] = (o_ref[...] * _silu(g_ref[...])).astype(BF16)

    spec = pl.BlockSpec((tm, D), lambda i: (i, 0))
    return pl.pallas_call(body, name=name, grid=(T // tm,), in_specs=[spec, spec], out_specs=spec,
                          out_shape=jax.ShapeDtypeStruct((T, D), BF16), compiler_params=_params(1))(o, gate)


def _final_loss(h, w, target, name):
    T, D = h.shape
    tm = _pick(T, (512, 256, 128))

    def body(h_ref, w_ref, t_ref, dh_ref, loss_ref, dw_ref):
        x, w = h_ref[...], w_ref[...]
        r = lax.rsqrt(jnp.mean(x * x, axis=-1, keepdims=True) + EPS)
        err = x * r * w - t_ref[...]
        part = 0.5 * jnp.sum(jnp.mean(err * err, axis=-1, keepdims=True), axis=0, keepdims=True)
        dx, dw = _rms_bwd_math(x, w, err * (1.0 / D))
        dh_ref[...] = dx

        @pl.when(pl.program_id(0) == 0)
        def _():
            loss_ref[...] = jnp.zeros_like(loss_ref)
            dw_ref[...] = jnp.zeros_like(dw_ref)

        loss_ref[...] += jnp.broadcast_to(part, loss_ref.shape)
        dw_ref[...] += jnp.broadcast_to(dw, dw_ref.shape)

    return pl.pallas_call(
        body, name=name, grid=(T // tm,),
        in_specs=[pl.BlockSpec((tm, D), lambda i: (i, 0)), pl.BlockSpec((1, D), lambda i: (0, 0)),
                  pl.BlockSpec((tm, D), lambda i: (i, 0))],
        out_specs=[pl.BlockSpec((tm, D), lambda i: (i, 0)), pl.BlockSpec((8, LANES), lambda i: (0, 0)),
                   pl.BlockSpec((8, D), lambda i: (0, 0))],
        out_shape=[jax.ShapeDtypeStruct((T, D), F32), jax.ShapeDtypeStruct((8, LANES), F32),
                   jax.ShapeDtypeStruct((8, D), F32)],
        compiler_params=_params(1),
    )(h, w, target)


def _gate_mul_bwd(dy, o, gate, name):
    T, D = o.shape
    tm = _pick(T, (512, 256, 128))

    def body(dy_ref, o_ref, g_ref, do_ref, dg_ref):
        dy, g = dy_ref[...], g_ref[...]
        s = jax.nn.sigmoid(g)
        do_ref[...] = dy * (g * s)
        dg_ref[...] = (dy * o_ref[...] * (s + g * s * (1.0 - s))).astype(BF16)

    spec = pl.BlockSpec((tm, D), lambda i: (i, 0))
    return pl.pallas_call(body, name=name, grid=(T // tm,), in_specs=[spec] * 3, out_specs=[spec] * 2,
                          out_shape=[jax.ShapeDtypeStruct((T, D), F32), jax.ShapeDtypeStruct((T, D), BF16)],
                          compiler_params=_params(1))(dy, o, gate)


def _sb_bwd(qkv, do, r_tot, d_model, name):
    T = qkv.shape[0]
    D = d_model
    H = D // HEAD_DIM
    QB = _pick(T, ATTN_Q_BLOCKS_BWD)
    KB = ATTN_K_BLOCK
    KS = 2 * KB
    ns = QB // KS
    nq = T // QB
    n_key_steps = T // KS
    scale = HEAD_DIM ** -0.5

    def body(q_ref, k_ref, v_ref, do_ref, r_ref, dq_ref, dk_ref, dv_ref,
             dkt_acc, dvt_acc, dq_acc, w_buf, da_buf, cum_buf, sig_buf, mask_buf):
        i = pl.program_id(1)

        @pl.when(i == 0)
        def _():
            dkt_acc[...] = jnp.zeros_like(dkt_acc)
            dvt_acc[...] = jnp.zeros_like(dvt_acc)
            _fill_score_masks(mask_buf, QB, KB, ns)

        q = q_ref[...]
        do_blk = do_ref[...].astype(BF16)
        q_t = q.astype(F32).T.astype(BF16)
        do_t = do_ref[...].T.astype(BF16)
        row_total = r_ref[0][:, 0:1]
        tri_rev = _incl_lower(KB)
        tri_fwd = jnp.where(_iota2((KB, KB), 0) <= _iota2((KB, KB), 1), 1.0, 0.0).astype(BF16)
        n_tot = (i + 1) * ns

        def step_rows(ref, s):
            return ref[pl.ds(pl.multiple_of(s * KS, KS), KS), :]

        def scores(s):
            w = _dot_nt(q, step_rows(k_ref, s)) * (scale * LOG2_E) + mask_buf[jnp.maximum(s - i * ns + 1, 0)]
            return w, _dot_nt(do_blk, step_rows(v_ref, s))

        def softplus_sums(w):
            sp, one_minus_sig = _softplus_bits(w)
            cum = jnp.concatenate([_dot_cum(sp[:, :KB], tri_rev), _dot_cum(sp[:, KB:], tri_rev)], axis=1)
            return cum, 1.0 - one_minus_sig

        def weights(w, cum, da, left_sp):
            right_l = row_total - left_sp - cum[:, 0:1]
            right_r = right_l - cum[:, KB:KB + 1]
            a = jnp.concatenate([jnp.exp2(w[:, :KB] - cum[:, :KB] - right_l),
                                 jnp.exp2(w[:, KB:] - cum[:, KB:] - right_r)], axis=1)
            p = da * a
            cp = jnp.concatenate([_dot(p[:, :KB], tri_fwd), _dot(p[:, KB:], tri_fwd)], axis=1)
            return a.astype(BF16), p, cp, row_total - right_r

        def score_grads(p, cp, sig, left_p):
            cum_l = cp[:, :KB] + left_p
            cum_r = cp[:, KB:] + cum_l[:, KB - 1:KB]
            dz = p - sig * jnp.concatenate([cum_l, cum_r], axis=1)
            return dz.astype(BF16), cum_r[:, KB - 1:KB]

        def trip(m, st):
            left_sp, left_p = st
            s2 = jnp.minimum(m + 2, n_tot - 1)
            s1 = jnp.minimum(m + 1, n_tot - 1)
            w_new, da_new = scores(s2)
            a, p, cp, left_sp = weights(w_buf[m % 3], cum_buf[m % 2], da_buf[m % 3], left_sp)
            cum_new, sig_new = softplus_sums(w_buf[s1 % 3])
            dz, left_p = score_grads(p, cp, sig_buf[m % 2], left_p)
            dq_acc[...] += _dot(dz, step_rows(k_ref, m))
            dkt_acc[m] += jnp.dot(q_t, dz, preferred_element_type=F32) * scale
            dvt_acc[m] += jnp.dot(do_t, a, preferred_element_type=F32)
            cum_buf[(m + 1) % 2] = cum_new
            sig_buf[(m + 1) % 2] = sig_new
            w_buf[(m + 2) % 3] = w_new
            da_buf[(m + 2) % 3] = da_new
            return left_sp, left_p

        dq_acc[...] = jnp.zeros_like(dq_acc)
        w_buf[0], da_buf[0] = scores(0)
        w_buf[1], da_buf[1] = scores(jnp.minimum(1, n_tot - 1))
        cum_buf[0], sig_buf[0] = softplus_sums(w_buf[0])
        zero_col = jnp.zeros((QB, 1), F32)
        lax.fori_loop(0, n_tot // 2, lambda j, c: trip(2 * j + 1, trip(2 * j, c)), (zero_col, zero_col))
        dq_ref[...] = (dq_acc[...] * scale).astype(BF16)

        @pl.when(i == nq - 1)
        def _():
            for s in range(n_key_steps):
                dk_ref[s * KS:(s + 1) * KS, :] = dkt_acc[s].T.astype(BF16)
                dv_ref[s * KS:(s + 1) * KS, :] = dvt_acc[s].T.astype(BF16)

    return pl.pallas_call(
        body, name=name, grid=(H, nq),
        in_specs=[pl.BlockSpec((QB, HEAD_DIM), lambda h, i: (i, h)),
                  pl.BlockSpec((T, HEAD_DIM), lambda h, i: (0, H + h)),
                  pl.BlockSpec((T, HEAD_DIM), lambda h, i: (0, 2 * H + h)),
                  pl.BlockSpec((QB, HEAD_DIM), lambda h, i: (i, h)),
                  pl.BlockSpec((1, QB, LANES), lambda h, i: (h, i, 0))],
        out_specs=[pl.BlockSpec((QB, HEAD_DIM), lambda h, i: (i, h)),
                   pl.BlockSpec((T, HEAD_DIM), lambda h, i: (0, h)),
                   pl.BlockSpec((T, HEAD_DIM), lambda h, i: (0, h))],
        out_shape=[jax.ShapeDtypeStruct((T, D), BF16)] * 3,
        scratch_shapes=[pltpu.VMEM((n_key_steps, HEAD_DIM, KS), F32), pltpu.VMEM((n_key_steps, HEAD_DIM, KS), F32),
                        pltpu.VMEM((QB, HEAD_DIM), F32), pltpu.VMEM((3, QB, KS), F32), pltpu.VMEM((3, QB, KS), F32),
                        pltpu.VMEM((2, QB, KS), F32), pltpu.VMEM((2, QB, KS), F32), pltpu.VMEM((ns + 1, QB, KS), F32)],
        compiler_params=_params(2),
    )(qkv, qkv, qkv, do, r_tot)


def _onorm_gate_bwd(dy, o, p, z_col, o_norm, name):
    T, D = o.shape
    H = D // HEAD_DIM
    tm = _pick(T, (256, 128, 64))

    def body(dy_ref, o_ref, z_ref, w_ref, do_ref, dz_ref, dw_ref):
        _, vjp = jax.vjp(functools.partial(_onorm_gate_math, n_heads=H), o_ref[...], z_ref[...], w_ref[...])
        do, dz, dw = vjp(dy_ref[...])
        do_ref[...] = do
        dz_ref[...] = dz.astype(BF16)

        @pl.when(pl.program_id(0) == 0)
        def _():
            dw_ref[...] = jnp.zeros_like(dw_ref)

        dw_ref[...] += jnp.broadcast_to(dw, dw_ref.shape)

    return pl.pallas_call(
        body, name=name, grid=(T // tm,),
        in_specs=[pl.BlockSpec((tm, D), lambda i: (i, 0)), pl.BlockSpec((tm, D), lambda i: (i, 0)),
                  pl.BlockSpec((tm, D), lambda i: (i, z_col)), pl.BlockSpec((1, HEAD_DIM), lambda i: (0, 0))],
        out_specs=[pl.BlockSpec((tm, D), lambda i: (i, 0)), pl.BlockSpec((tm, D), lambda i: (i, 0)),
                   pl.BlockSpec((8, HEAD_DIM), lambda i: (0, 0))],
        out_shape=[jax.ShapeDtypeStruct((T, D), F32), jax.ShapeDtypeStruct((T, D), BF16),
                   jax.ShapeDtypeStruct((8, HEAD_DIM), F32)],
        compiler_params=_params(1),
    )(dy, o, p, o_norm)


def _row_to_col(row):
    C = row.shape[1]
    eye = _iota2((C, C), 0) == _iota2((C, C), 1)
    return jnp.sum(jnp.where(eye, row, 0.0), axis=1, keepdims=True)


def _lane_sum(x):
    return jnp.sum(x, axis=-1, keepdims=True)


def _chunk_head_bwd(q, k, v, gc, beta, s_in, t_inv, do, ds_out):
    C = q.shape[0]
    r, c = _iota2((C, C), 0), _iota2((C, C), 1)
    causal, strict = r >= c, r > c
    decay = jnp.where(causal, jnp.exp(jnp.where(causal, gc - _col_to_row(gc), 0.0)), 0.0)
    kb, vb = k * beta, v * beta
    eg = jnp.exp(gc)
    kbg = kb * eg
    g_last = gc[C - 1:C, :]
    e_tail = jnp.exp(g_last - gc)
    k_tail = k * e_tail
    gl = jnp.exp(g_last)
    qg = q * eg
    t_inv_t = t_inv.T
    kk = _dot_nt(kb, k)
    u = _dot(t_inv, vb)
    w = _dot(t_inv, kbg)
    qk = _dot_nt(q, k)
    d_qg = _dot_nt(do, s_in)
    ds_state = _dot_tn(qg, do)
    yield
    low = jnp.where(strict, kk * decay, 0.0)
    attn = qk * decay
    w_state = _dot(w, s_in)
    d_vnew_intra = _dot_tn(attn, do)
    d_vnew_state = _dot(k_tail, ds_out)
    yield
    v_new = u - w_state
    d_vnew = d_vnew_intra + d_vnew_state
    d_ktail = _dot_nt(v_new, ds_out)
    d_attn_raw = _dot_nt(do, v_new)
    d_w = -_dot_nt(d_vnew, s_in)
    ds_w = _dot_tn(w, d_vnew)
    d_vb = _dot(t_inv_t, d_vnew)
    d_tinv_u = _dot_nt(d_vnew, vb)
    yield
    d_gl = jnp.sum(_lane_sum(s_in * ds_out), axis=0, keepdims=True)
    d_attn = jnp.where(causal, d_attn_raw, 0.0)
    ds_in = ds_out * gl + ds_state - ds_w
    d_kbg = _dot(t_inv_t, d_w)
    d_tinv_w = _dot_nt(d_w, kbg)
    d_qk = d_attn * decay
    dq_intra = _dot(d_qk, k)
    dk_intra = _dot_tn(d_qk, q)
    yield
    inner = _dot(t_inv_t, d_tinv_u + d_tinv_w)
    yield
    d_low_raw = _dot_nt(inner, t_inv)
    yield
    d_low = jnp.where(strict, -d_low_raw, 0.0)
    d_kk = d_low * decay
    d_kb_low = _dot(d_kk, k)
    dk_low = _dot_tn(d_kk, kb)
    yield
    d_kb = d_kb_low + d_kbg * eg
    dq = dq_intra + d_qg * eg
    dk = dk_low + dk_intra + d_ktail * e_tail + d_kb * beta
    dv = d_vb * beta
    dbeta = _lane_sum(d_kb * k) + _lane_sum(d_vb * v)
    m = d_low * low + d_attn * attn
    tail_term = _lane_sum(d_ktail * k_tail)
    d_g_last = d_gl * gl + jnp.sum(tail_term, axis=0, keepdims=True)
    dgc = (_lane_sum(m) - _row_to_col(jnp.sum(m, axis=0, keepdims=True))
           + _lane_sum(d_qg * qg) + _lane_sum(d_kbg * kbg) - tail_term)
    dgc = dgc + jnp.where(_iota2((C, 1), 0) == C - 1, d_g_last, 0.0)
    return dq, dk, dv, dgc, dbeta, ds_in


def _chunk_bwd(qkv, beta, gc, s_all, t_all, do, d_model, name):
    T = qkv.shape[0]
    D = d_model
    H = D // HEAD_DIM
    N = T // CHUNK

    def body(q_ref, k_ref, v_ref, beta_ref, gc_ref, s_ref, t_ref, do_ref, dqkv_ref, dbeta_ref, dg_ref, ds_ref):
        @pl.when(pl.program_id(0) == 0)
        def _():
            ds_ref[...] = jnp.zeros_like(ds_ref)

        lane = _iota2((CHUNK, LANES), 1)
        dgc_all = jnp.zeros((CHUNK, LANES), F32)
        dbeta_all = jnp.zeros((CHUNK, LANES), F32)
        results = _lockstep(
            _chunk_head_bwd(q_ref[:, hs], k_ref[:, hs], v_ref[:, hs], gc_ref[:, h:h + 1], beta_ref[:, h:h + 1],
                            s_ref[0, h], t_ref[0, h], do_ref[:, hs], ds_ref[h])
            for h, hs in enumerate(slice(h * HEAD_DIM, (h + 1) * HEAD_DIM) for h in range(H)))
        for h, (dq, dk, dv, dgc, dbeta, ds_in) in enumerate(results):
            ds_ref[h] = ds_in
            dqkv_ref[:, h * HEAD_DIM:(h + 1) * HEAD_DIM] = dq
            dqkv_ref[:, D + h * HEAD_DIM:D + (h + 1) * HEAD_DIM] = dk
            dqkv_ref[:, 2 * D + h * HEAD_DIM:2 * D + (h + 1) * HEAD_DIM] = dv
            dgc_all = jnp.where(lane == h, dgc, dgc_all)
            dbeta_all = jnp.where(lane == h, dbeta, dbeta_all)
        dbeta_ref[...] = dbeta_all
        dg_ref[...] = _dot_mask(_chunk_tri(CHUNK, upper=True), dgc_all)

    rev = lambda n: N - 1 - n
    return pl.pallas_call(
        body, name=name, grid=(N,),
        in_specs=[pl.BlockSpec((CHUNK, D), lambda n: (rev(n), 0)), pl.BlockSpec((CHUNK, D), lambda n: (rev(n), 1)),
                  pl.BlockSpec((CHUNK, D), lambda n: (rev(n), 2)),
                  pl.BlockSpec((CHUNK, LANES), lambda n: (rev(n), 0)), pl.BlockSpec((CHUNK, LANES), lambda n: (rev(n), 0)),
                  pl.BlockSpec((1, H, HEAD_DIM, HEAD_DIM), lambda n: (rev(n), 0, 0, 0)),
                  pl.BlockSpec((1, H, CHUNK, CHUNK), lambda n: (rev(n), 0, 0, 0)),
                  pl.BlockSpec((CHUNK, D), lambda n: (rev(n), 0))],
        out_specs=[pl.BlockSpec((CHUNK, 3 * D), lambda n: (rev(n), 0)),
                   pl.BlockSpec((CHUNK, LANES), lambda n: (rev(n), 0)), pl.BlockSpec((CHUNK, LANES), lambda n: (rev(n), 0))],
        out_shape=[jax.ShapeDtypeStruct((T, 3 * D), F32), jax.ShapeDtypeStruct((T, LANES), F32),
                   jax.ShapeDtypeStruct((T, LANES), F32)],
        scratch_shapes=[pltpu.VMEM((H, HEAD_DIM, HEAD_DIM), F32)], compiler_params=_params(1),
    )(qkv, qkv, qkv, beta, gc, s_all, t_all, do)


def _gates_bwd(p, gate_params, col0, dbeta, dg, name):
    T = p.shape[0]
    tm = _pick(T, (256, 128, 64))

    def body(pb_ref, pa_ref, gp_ref, dbeta_ref, dg_ref, dp_ref, dgp_ref):
        gp = gp_ref[...]
        _, vjp = jax.vjp(_gates_math, pb_ref[...], pa_ref[...], gp[0:1, :], gp[1:2, :])
        dpb, dpa, d_alog, d_dt = vjp((dbeta_ref[...], dg_ref[...]))
        dp_ref[:, 0:LANES] = dpb.astype(BF16)
        dp_ref[:, LANES:2 * LANES] = dpa.astype(BF16)

        @pl.when(pl.program_id(0) == 0)
        def _():
            dgp_ref[...] = jnp.zeros_like(dgp_ref)

        dgp_ref[0:1, :] += d_alog
        dgp_ref[1:2, :] += d_dt

    return pl.pallas_call(
        body, name=name, grid=(T // tm,),
        in_specs=[pl.BlockSpec((tm, LANES), lambda i: (i, col0)), pl.BlockSpec((tm, LANES), lambda i: (i, col0 + 1)),
                  pl.BlockSpec((8, LANES), lambda i: (0, 0)),
                  pl.BlockSpec((tm, LANES), lambda i: (i, 0)), pl.BlockSpec((tm, LANES), lambda i: (i, 0))],
        out_specs=[pl.BlockSpec((tm, 2 * LANES), lambda i: (i, 0)), pl.BlockSpec((8, LANES), lambda i: (0, 0))],
        out_shape=[jax.ShapeDtypeStruct((T, 2 * LANES), BF16), jax.ShapeDtypeStruct((8, LANES), F32)],
        compiler_params=_params(1),
    )(p, p, gate_params, dbeta, dg)


def _conv_bwd_act(p, conv_w, dqkv, d_model, name):
    T = p.shape[0]
    D = d_model
    H = D // HEAD_DIM
    tm = _pick(T, (256, 128, 64))

    def body(cur_ref, prev_ref, w_ref, dout_ref, dc_ref, dw_ref):
        j, i = pl.program_id(0), pl.program_id(1)
        prev = prev_ref[...] * (i > 0).astype(F32)
        taps = _conv_taps(cur_ref[...], prev)
        w = w_ref[...]
        c = sum(taps[k] * w[k:k + 1, :] for k in range(4))
        _, vjp = jax.vjp(lambda cc: _qkv_post(cc, j, H), c)
        (dc,) = vjp(dout_ref[...])
        dc_ref[...] = dc

        @pl.when(i == 0)
        def _():
            dw_ref[...] = jnp.zeros_like(dw_ref)

        for k in range(4):
            dw_ref[k:k + 1, :] += jnp.sum(dc * taps[k], axis=0, keepdims=True)

    return pl.pallas_call(
        body, name=name, grid=(3, T // tm),
        in_specs=[pl.BlockSpec((tm, D), lambda j, i: (i, j)),
                  pl.BlockSpec((8, D), lambda j, i: (jnp.maximum(i * (tm // 8) - 1, 0), j)),
                  pl.BlockSpec((4, D), lambda j, i: (0, j)),
                  pl.BlockSpec((tm, D), lambda j, i: (i, j))],
        out_specs=[pl.BlockSpec((tm, D), lambda j, i: (i, j)), pl.BlockSpec((4, D), lambda j, i: (0, j))],
        out_shape=[jax.ShapeDtypeStruct((T, 3 * D), F32), jax.ShapeDtypeStruct((4, 3 * D), F32)],
        compiler_params=_params(2),
    )(p, p, conv_w, dqkv)


def _conv_bwd_input(dc, conv_w, name):
    T, D3 = dc.shape
    D = D3 // 3
    tm = _pick(T, (256, 128, 64))
    n_t = T // tm

    def body(cur_ref, next_ref, w_ref, dp_ref):
        i = pl.program_id(0)
        cur = cur_ref[...]
        nxt = next_ref[...] * (i < n_t - 1).astype(F32)
        ext = jnp.concatenate([cur, nxt], axis=0)
        w = w_ref[...]
        acc = cur * w[3:4, :]
        for s in (1, 2, 3):
            acc = acc + pltpu.roll(ext, tm + 8 - s, 0)[0:tm] * w[3 - s:4 - s, :]
        dp_ref[...] = acc.astype(BF16)

    return pl.pallas_call(
        body, name=name, grid=(n_t, 3),
        in_specs=[pl.BlockSpec((tm, D), lambda i, j: (i, j)),
                  pl.BlockSpec((8, D), lambda i, j: (jnp.minimum((i + 1) * (tm // 8), T // 8 - 1), j)),
                  pl.BlockSpec((4, D), lambda i, j: (0, j))],
        out_specs=pl.BlockSpec((tm, D), lambda i, j: (i, j)),
        out_shape=jax.ShapeDtypeStruct((T, D3), BF16), compiler_params=_params(2),
    )(dc, dc, conv_w)


def _comm_call(body, arrays, out_shape, n_remote, n_local, name):
    any_spec = pl.BlockSpec(memory_space=pl.ANY)
    return pl.pallas_call(
        body, name=name, in_specs=[any_spec] * len(arrays), out_specs=[any_spec] * len(out_shape), out_shape=out_shape,
        scratch_shapes=[pltpu.SemaphoreType.DMA((n_remote,)), pltpu.SemaphoreType.DMA((n_remote,)),
                        pltpu.SemaphoreType.DMA((n_local,))],
        compiler_params=pltpu.CompilerParams(has_side_effects=True),
    )(*arrays)


def _gather_shards(arrays, name):
    n = len(arrays)

    def body(*refs):
        ins, outs = refs[:n], refs[n:2 * n]
        send_sems, recv_sems, local_sems = refs[2 * n:]
        x, y, c = lax.axis_index("x"), lax.axis_index("y"), lax.axis_index("c")
        me, sibling = (x, y, c), (x, y, 1 - c)
        chips = [(1 - x, y), (x, 1 - y), (1 - x, 1 - y)]

        def copy(a, k, block, to, src=None):
            dst = outs[a].at[4 * block[0] + 2 * block[1] + block[2]]
            return pltpu.make_async_remote_copy(
                src_ref=dst if src is None else src, dst_ref=dst, send_sem=send_sems.at[a * 7 + k],
                recv_sem=recv_sems.at[a * 7 + k], device_id=to, device_id_type=MESH_ID)

        local = [pltpu.make_async_copy(ins[a], outs[a].at[4 * x + 2 * y + c], local_sems.at[a]) for a in range(n)]
        first = [copy(a, 0, me, sibling, src=ins[a]) for a in range(n)]
        first += [copy(a, 1 + j, me, (*chip, c), src=ins[a]) for j, chip in enumerate(chips) for a in range(n)]
        for cp in local + first:
            cp.start()
        passed = []
        for j, chip in enumerate(chips):
            for a in range(n):
                copy(a, 1 + j, (*chip, c), me).wait_recv()
                passed.append(copy(a, 4 + j, (*chip, c), sibling))
                passed[-1].start()
        for a in range(n):
            copy(a, 0, sibling, me).wait_recv()
            for j, chip in enumerate(chips):
                copy(a, 4 + j, (*chip, 1 - c), me).wait_recv()
        for cp in first + passed:
            cp.wait_send()
        for cp in local:
            cp.wait()

    out_shape = [jax.ShapeDtypeStruct((N_DEV,) + a.shape, a.dtype) for a in arrays]
    return _comm_call(body, arrays, out_shape, 7 * n, n, name)


def _pair_exchange(arrays, name):
    n = len(arrays)

    def body(*refs):
        ins, own, pair = refs[:n], refs[n:2 * n], refs[2 * n:3 * n]
        send_sems, recv_sems, local_sems = refs[3 * n:]
        x, y, c = lax.axis_index("x"), lax.axis_index("y"), lax.axis_index("c")
        local = [pltpu.make_async_copy(ins[a].at[q, c], own[a].at[q], local_sems.at[a * 4 + q])
                 for a in range(n) for q in range(4)]
        sends = [pltpu.make_async_remote_copy(
            src_ref=ins[a].at[q, 1 - c], dst_ref=pair[a].at[q], send_sem=send_sems.at[a * 4 + q],
            recv_sem=recv_sems.at[a * 4 + q], device_id=(x, y, 1 - c), device_id_type=MESH_ID)
            for a in range(n) for q in range(4)]
        for cp in local + sends:
            cp.start()
        for cp in sends:
            cp.wait_recv()
        for cp in sends:
            cp.wait_send()
        for cp in local:
            cp.wait()

    out_shape = [jax.ShapeDtypeStruct((4,) + a.shape[2:], a.dtype) for a in arrays] * 2
    outs = _comm_call(body, arrays, out_shape, 4 * n, 4 * n, name)
    return outs[:n], outs[n:]


def _pair_add(own, pair, name):
    _, R, C = own.shape
    tr = next((t for t in (256, 128, 64, 32, 16, 8) if R % t == 0), R)

    def body(a_ref, b_ref, o_ref):
        o_ref[...] = a_ref[...] + b_ref[...]

    spec = pl.BlockSpec((1, tr, C), lambda q, i: (q, i, 0))
    return pl.pallas_call(body, name=name, grid=(4, R // tr), in_specs=[spec, spec], out_specs=spec,
                          out_shape=jax.ShapeDtypeStruct(own.shape, own.dtype), compiler_params=_params(2))(own, pair)


def _chip_exchange(arrays, name):
    n = len(arrays)

    def body(*refs):
        ins, outs = refs[:n], refs[n:2 * n]
        send_sems, recv_sems, local_sems = refs[2 * n:]
        x, y, c = lax.axis_index("x"), lax.axis_index("y"), lax.axis_index("c")
        my_chip = 2 * x + y
        chips = [(1 - x, y), (x, 1 - y), (1 - x, 1 - y)]
        local = [pltpu.make_async_copy(ins[a].at[my_chip], outs[a].at[my_chip], local_sems.at[a]) for a in range(n)]
        sends = [pltpu.make_async_remote_copy(
            src_ref=ins[a].at[2 * px + py], dst_ref=outs[a].at[my_chip], send_sem=send_sems.at[a * 3 + j],
            recv_sem=recv_sems.at[a * 3 + j], device_id=(px, py, c), device_id_type=MESH_ID)
            for j, (px, py) in enumerate(chips) for a in range(n)]
        arrivals = [pltpu.make_async_remote_copy(
            src_ref=ins[a].at[my_chip], dst_ref=outs[a].at[2 * px + py], send_sem=send_sems.at[a * 3 + j],
            recv_sem=recv_sems.at[a * 3 + j], device_id=(px, py, c), device_id_type=MESH_ID)
            for j, (px, py) in enumerate(chips) for a in range(n)]
        for cp in local + sends:
            cp.start()
        for cp in arrivals:
            cp.wait_recv()
        for cp in sends:
            cp.wait_send()
        for cp in local:
            cp.wait()

    out_shape = [jax.ShapeDtypeStruct(a.shape, a.dtype) for a in arrays]
    return _comm_call(body, arrays, out_shape, 3 * n, n, name)


def _reduce_adamw(recv, w, m, v, name):
    S, R, C = recv.shape
    tr = next((t for t in (256, 128, 64, 32, 16, 8) if R % t == 0), R)
    c1 = 1.0 - ADAM_B1 ** ADAM_STEP
    c2 = 1.0 - ADAM_B2 ** ADAM_STEP

    def body(r_ref, w_ref, m_ref, v_ref, g_ref, d_ref, nm_ref, nv_ref):
        g = r_ref[0]
        for s in range(1, S):
            g = g + r_ref[s]
        nm = ADAM_B1 * m_ref[...] + (1.0 - ADAM_B1) * g
        nv = ADAM_B2 * v_ref[...] + (1.0 - ADAM_B2) * (g * g)
        g_ref[...] = g
        nm_ref[...] = nm
        nv_ref[...] = nv
        d_ref[...] = -ADAM_LR * ((nm / c1) / (jnp.sqrt(nv / c2) + ADAM_EPS) + ADAM_WD * w_ref[...])

    spec = pl.BlockSpec((tr, C), lambda i: (i, 0))
    return pl.pallas_call(
        body, name=name, grid=(R // tr,),
        in_specs=[pl.BlockSpec((S, tr, C), lambda i: (0, i, 0)), spec, spec, spec], out_specs=[spec] * 4,
        out_shape=[jax.ShapeDtypeStruct((R, C), F32)] * 4, compiler_params=_params(1),
    )(recv, w, m, v)


def _forward_local(x, target, nw0, nw1, fw, wa_in, conv_w, gate_params, o_norm, wa_out, wb_in, wb_out):
    T, D = x.shape
    nD = D // LANES
    sv = {}
    sv["u0"] = _rmsnorm_fwd(x, nw0, "a_norm_fwd")
    sv["pa"] = _mm_nn(sv["u0"], wa_in, "a_in_proj")
    sv["qkv_a"] = _conv_fwd(sv["pa"], conv_w, D, "a_conv_fwd")
    sv["beta"], sv["gc"] = _gates_fwd(sv["pa"], gate_params, 4 * nD, "a_gates_fwd")
    sv["o_a"], sv["s_all"], sv["t_all"] = _chunk_fwd(sv["qkv_a"], sv["beta"], sv["gc"], D, "a_chunk_fwd")
    sv["y_a"] = _onorm_gate_fwd(sv["o_a"], sv["pa"], 3, o_norm, "a_onorm_fwd")
    sv["h1"] = _mm_nn(sv["y_a"], wa_out, "a_out_proj", add=x)
    sv["u1"] = _rmsnorm_fwd(sv["h1"], nw1, "b_norm_fwd")
    sv["qkv_b"] = _mm_nn(sv["u1"], wb_in[:, :3 * D], "b_in_proj_qkv", out_dtype=BF16)
    sv["gate_b"] = _mm_nn(sv["u1"], wb_in[:, 3 * D:], "b_in_proj_gate")
    sv["o_b"], sv["r_b"] = _sb_fwd(sv["qkv_b"], D, "b_attn_fwd")
    sv["y_b"] = _gate_mul_fwd(sv["o_b"], sv["gate_b"], "b_gate_fwd")
    sv["h2"] = _mm_nn(sv["y_b"], wb_out, "b_out_proj", add=sv["h1"])
    sv["dh2"], sv["loss"], sv["dfw"] = _final_loss(sv["h2"], fw, target, "final_loss")
    return sv


def _backward_local(sv, x, nw0, nw1, wa_in, conv_w, gate_params, o_norm, wa_out, wb_in, wb_out):
    T, D = x.shape
    nD = D // LANES
    g = {}
    dh2 = sv["dh2"]
    g["wb_out"] = _mm_tn(sv["y_b"], dh2, "b_out_proj_dw")
    dy_b = _mm_nt([(dh2, wb_out)], "b_out_proj_dx")
    do_b, dgate_b = _gate_mul_bwd(dy_b, sv["o_b"], sv["gate_b"], "b_gate_bwd")
    dq_b, dk_b, dv_b = _sb_bwd(sv["qkv_b"], do_b, sv["r_b"], D, "b_attn_bwd")
    dp_b = [dq_b, dk_b, dv_b, dgate_b]
    g["wb_in"] = jnp.concatenate([_mm_tn(sv["u1"], dp, "b_in_proj_dw%d" % c) for c, dp in enumerate(dp_b)], axis=1)
    du1 = _mm_nt([(dp, wb_in[:, c * D:(c + 1) * D]) for c, dp in enumerate(dp_b)], "b_in_proj_dx")
    dh1, g["nw1"] = _rmsnorm_bwd(sv["h1"], nw1, du1, dh2, "b_norm_bwd")
    g["wa_out"] = _mm_tn(sv["y_a"], dh1, "a_out_proj_dw")
    dy_a = _mm_nt([(dh1, wa_out)], "a_out_proj_dx")
    do_a, dz_a, g["o_norm"] = _onorm_gate_bwd(dy_a, sv["o_a"], sv["pa"], 3, o_norm, "a_onorm_bwd")
    dqkv_a, dbeta, dg = _chunk_bwd(sv["qkv_a"], sv["beta"], sv["gc"], sv["s_all"], sv["t_all"], do_a, D, "a_chunk_bwd")
    dp_gates, g["gate_params"] = _gates_bwd(sv["pa"], gate_params, 4 * nD, dbeta, dg, "a_gates_bwd")
    dc, g["conv_w"] = _conv_bwd_act(sv["pa"], conv_w, dqkv_a, D, "a_conv_bwd_act")
    dp_qkv = _conv_bwd_input(dc, conv_w, "a_conv_bwd_input")
    dp_a = [(dp_qkv, 0, 3 * D), (dz_a, 3 * D, 4 * D), (dp_gates, 4 * D, 4 * D + 2 * LANES)]
    g["wa_in"] = jnp.concatenate([_mm_tn(sv["u0"], dp, "a_in_proj_dw%d" % c) for c, (dp, _, _) in enumerate(dp_a)], axis=1)
    du0 = _mm_nt([(dp, wa_in[:, lo:hi]) for dp, lo, hi in dp_a], "a_in_proj_dx")
    g["x"], g["nw0"] = _rmsnorm_bwd(x, nw0, du0, dh1, "a_norm_bwd")
    g["fw"] = sv["dfw"]
    return g


def kernel(x, norm_w, a_w_in, a_conv_w, a_a_log, a_dt_bias, a_o_norm, a_w_out, b_w_in, b_w_out, final_norm_w, loss_target, m_norm_w, m_a_w_in, m_a_conv_w, m_a_a_log, m_a_dt_bias, m_a_o_norm, m_a_w_out, m_b_w_in, m_b_w_out, m_final_norm_w, v_norm_w, v_a_w_in, v_a_conv_w, v_a_a_log, v_a_dt_bias, v_a_o_norm, v_a_w_out, v_b_w_in, v_b_w_out, v_final_norm_w):
    D = x.shape[-1]
    H = D // HEAD_DIM
    shards = [a_w_in[0].astype(BF16), a_w_out[0].astype(BF16), b_w_in[0].astype(BF16), b_w_out[0].astype(BF16), a_conv_w[0]]
    ga_in, ga_out, gb_in, gb_out, g_conv = _gather_shards(shards, "weights_gather")
    wa = ga_in.transpose(1, 0, 2).reshape(D, -1)
    pad = lambda w: jnp.pad(w, ((0, 0), (0, LANES - w.shape[1])))
    wa_in = jnp.concatenate([wa[:, :4 * D], pad(wa[:, 4 * D:4 * D + H]), pad(wa[:, 4 * D + H:])], axis=1)
    wa_out = ga_out.reshape(D, D)
    wb_in = gb_in.transpose(1, 0, 2).reshape(D, 4 * D)
    wb_out = gb_out.reshape(D, D)
    conv_w = g_conv.transpose(1, 0, 2).reshape(4, 3 * D)
    gate_params = jnp.zeros((8, LANES), F32).at[0, :H].set(a_a_log[0]).at[1, :H].set(a_dt_bias[0])
    nw0, nw1, fw = norm_w[0:1], norm_w[1:2], final_norm_w[None]

    sv = _forward_local(x[0], loss_target[0], nw0, nw1, fw, wa_in, conv_w, gate_params, a_o_norm, wa_out, wb_in, wb_out)
    g = _backward_local(sv, x[0], nw0, nw1, wa_in, conv_w, gate_params, a_o_norm, wa_out, wb_in, wb_out)

    gwa = g["wa_in"]
    gwa = jnp.concatenate([gwa[:, :4 * D], gwa[:, 4 * D:4 * D + H], gwa[:, 4 * D + LANES:4 * D + LANES + H]], axis=1)
    row = lambda v: jnp.pad(v.reshape(1, -1), ((0, 0), (0, D - v.size)))
    small = jnp.concatenate([g["nw0"][0:1], g["nw1"][0:1], g["fw"][0:1], row(g["gate_params"][0, :H]),
                             row(g["gate_params"][1, :H]), row(g["o_norm"][0]), row(sv["loss"][0, 0:1]),
                             jnp.zeros((1, D), F32)], axis=0)
    contribs = [gwa.reshape(D, N_DEV, -1).transpose(1, 0, 2), g["wa_out"].reshape(N_DEV, D // N_DEV, D),
                g["wb_in"].reshape(D, N_DEV, -1).transpose(1, 0, 2), g["wb_out"].reshape(N_DEV, D // N_DEV, D),
                g["conv_w"].reshape(4, N_DEV, -1).transpose(1, 0, 2), jnp.broadcast_to(small[None], (N_DEV, 8, D))]
    own, pair = _pair_exchange([a.reshape((4, 2) + a.shape[1:]) for a in contribs], "grads_pair_exchange")
    partial = [_pair_add(o, p, "grads_pair_add%d" % k) for k, (o, p) in enumerate(zip(own, pair))]
    ra_in, ra_out, rb_in, rb_out, r_conv, r_small = _chip_exchange(partial, "grads_chip_exchange")

    outs = {}
    for nm, recv, w, m, v in (("a_w_in", ra_in, a_w_in, m_a_w_in, v_a_w_in), ("a_w_out", ra_out, a_w_out, m_a_w_out, v_a_w_out),
                              ("b_w_in", rb_in, b_w_in, m_b_w_in, v_b_w_in), ("b_w_out", rb_out, b_w_out, m_b_w_out, v_b_w_out),
                              ("a_conv_w", r_conv, a_conv_w, m_a_conv_w, v_a_conv_w)):
        outs[nm] = tuple(o[None] for o in _reduce_adamw(recv, w[0], m[0], v[0], "adamw_" + nm))

    def pack(nw, alog, dt, onorm, fnw):
        return jnp.concatenate([nw, fnw.reshape(1, D), row(alog), row(dt), row(onorm), jnp.zeros((2, D), F32)], axis=0)

    s_g, s_d, s_m, s_v = _reduce_adamw(
        r_small, pack(norm_w, a_a_log, a_dt_bias, a_o_norm, final_norm_w),
        pack(m_norm_w, m_a_a_log, m_a_dt_bias, m_a_o_norm, m_final_norm_w),
        pack(v_norm_w, v_a_a_log, v_a_dt_bias, v_a_o_norm, v_final_norm_w), "adamw_small")
    loss = s_g[6, 0]
    for i, s in enumerate((s_g, s_d, s_m, s_v)):
        outs.setdefault("norm_w", [None] * 4)[i] = s[0:2]
        outs.setdefault("final_norm_w", [None] * 4)[i] = s[2]
        outs.setdefault("a_a_log", [None] * 4)[i] = s[3:4, :H]
        outs.setdefault("a_dt_bias", [None] * 4)[i] = s[4:5, :H]
        outs.setdefault("a_o_norm", [None] * 4)[i] = s[5:6, :HEAD_DIM]
    names = ("norm_w", "a_w_in", "a_conv_w", "a_a_log", "a_dt_bias", "a_o_norm", "a_w_out", "b_w_in", "b_w_out", "final_norm_w")
    return (loss, g["x"][None]) + tuple(outs[n][i] for i in range(4) for n in names)
```

```python
import functools

import jax
import jax.numpy as jnp
from jax import lax
from jax.experimental import pallas as pl
from jax.experimental.pallas import tpu as pltpu

F32 = jnp.float32
BF16 = jnp.bfloat16
EPS = 1e-6
LOG2_E = 1.4426950408889634
MASKED_SCORE = -1e30
HEAD_DIM = 128
CHUNK = 64
ATTN_Q_BLOCKS_FWD = (512, 256)
ATTN_Q_BLOCKS_BWD = (512, 256)
ATTN_K_BLOCK = 128
LANES = 128
N_DEV = 8
VMEM_LIMIT_BYTES = 48 * 1024 * 1024
ADAM_LR, ADAM_B1, ADAM_B2, ADAM_EPS, ADAM_WD, ADAM_STEP = 0.001, 0.9, 0.999, 1e-08, 0.01, 10
MESH_ID = pl.DeviceIdType.MESH


def _pick(n, candidates):
    for c in candidates:
        if n % c == 0:
            return c
    raise ValueError(f"no tile for {n} in {candidates}")


def _params(n_grid_axes):
    return pltpu.CompilerParams(dimension_semantics=("arbitrary",) * n_grid_axes, vmem_limit_bytes=VMEM_LIMIT_BYTES)


def _dot(a, b):
    return jnp.dot(a.astype(BF16), b.astype(BF16), preferred_element_type=F32)


def _dot_nt(a, b):
    return lax.dot_general(a.astype(BF16), b.astype(BF16), (((1,), (1,)), ((), ())), preferred_element_type=F32)


def _dot_tn(a, b):
    return lax.dot_general(a.astype(BF16), b.astype(BF16), (((0,), (0,)), ((), ())), preferred_element_type=F32)


def _split2(x):
    hi = x.astype(BF16)
    lo = (x - hi.astype(F32)).astype(BF16)
    return hi, lo


def _split3(x):
    hi = x.astype(BF16)
    r = x - hi.astype(F32)
    mid = r.astype(BF16)
    lo = (r - mid.astype(F32)).astype(BF16)
    return hi, mid, lo


def _dot3(a, b):
    a_hi, a_lo = _split2(a)
    b_hi, b_lo = _split2(b)
    d = functools.partial(jnp.dot, preferred_element_type=F32)
    return d(a_hi, b_hi) + (d(a_hi, b_lo) + d(a_lo, b_hi))


def _silu(x):
    return x * jax.nn.sigmoid(x)


def _softplus(x):
    return jnp.maximum(x, 0.0) + jnp.log1p(jnp.exp(-jnp.abs(x)))


def _iota2(shape, axis):
    return lax.broadcasted_iota(jnp.int32, shape, axis)


def _rms_bwd_math(x, w, dy):
    r = lax.rsqrt(jnp.mean(x * x, axis=-1, keepdims=True) + EPS)
    xhat = x * r
    dxhat = dy * w
    dx = r * (dxhat - xhat * jnp.mean(dxhat * xhat, axis=-1, keepdims=True))
    dw = jnp.sum(dy * xhat, axis=0, keepdims=True)
    return dx, dw


def _rmsnorm_fwd(x, w, name):
    T, D = x.shape
    tm = _pick(T, (512, 256, 128))

    def body(x_ref, w_ref, o_ref):
        xf = x_ref[...]
        r = lax.rsqrt(jnp.mean(xf * xf, axis=-1, keepdims=True) + EPS)
        o_ref[...] = (xf * r * w_ref[...]).astype(BF16)

    return pl.pallas_call(
        body, name=name, grid=(T // tm,),
        in_specs=[pl.BlockSpec((tm, D), lambda i: (i, 0)), pl.BlockSpec((1, D), lambda i: (0, 0))],
        out_specs=pl.BlockSpec((tm, D), lambda i: (i, 0)),
        out_shape=jax.ShapeDtypeStruct((T, D), BF16), compiler_params=_params(1),
    )(x, w)


def _rmsnorm_bwd(x, w, du, dres, name):
    T, D = x.shape
    tm = _pick(T, (512, 256, 128))

    def body(x_ref, w_ref, du_ref, dres_ref, dx_ref, dw_ref):
        dx, dw = _rms_bwd_math(x_ref[...], w_ref[...], du_ref[...].astype(F32))
        dx_ref[...] = dres_ref[...] + dx

        @pl.when(pl.program_id(0) == 0)
        def _():
            dw_ref[...] = jnp.zeros_like(dw_ref)

        dw_ref[...] += jnp.broadcast_to(dw, dw_ref.shape)

    return pl.pallas_call(
        body, name=name, grid=(T // tm,),
        in_specs=[pl.BlockSpec((tm, D), lambda i: (i, 0)), pl.BlockSpec((1, D), lambda i: (0, 0)),
                  pl.BlockSpec((tm, D), lambda i: (i, 0)), pl.BlockSpec((tm, D), lambda i: (i, 0))],
        out_specs=[pl.BlockSpec((tm, D), lambda i: (i, 0)), pl.BlockSpec((8, D), lambda i: (0, 0))],
        out_shape=[jax.ShapeDtypeStruct((T, D), F32), jax.ShapeDtypeStruct((8, D), F32)],
        compiler_params=_params(1),
    )(x, w, du, dres)


def _mm_nn(a, b, name, add=None, out_dtype=F32):
    M, K = a.shape
    _, N = b.shape
    tm = _pick(M, (256, 128)) if N > 2048 else _pick(M, (512, 256, 128))

    def body(*refs):
        a_ref, b_ref = refs[0], refs[1]
        o_ref = refs[-1]
        acc = _dot(a_ref[...], b_ref[...])
        if add is not None:
            acc = acc + refs[2][...]
        o_ref[...] = acc.astype(out_dtype)

    in_specs = [pl.BlockSpec((tm, K), lambda i: (i, 0)), pl.BlockSpec((K, N), lambda i: (0, 0))]
    args = [a, b]
    if add is not None:
        in_specs.append(pl.BlockSpec((tm, N), lambda i: (i, 0)))
        args.append(add)
    return pl.pallas_call(
        body, name=name, grid=(M // tm,), in_specs=in_specs,
        out_specs=pl.BlockSpec((tm, N), lambda i: (i, 0)),
        out_shape=jax.ShapeDtypeStruct((M, N), out_dtype), compiler_params=_params(1),
    )(*args)


def _mm_nt(pairs, name):
    M = pairs[0][0].shape[0]
    N = pairs[0][1].shape[0]
    n = len(pairs)
    tm = _pick(M, (512, 256, 128))

    def body(*refs):
        acc = _dot_nt(refs[0][...], refs[n][...])
        for p in range(1, n):
            acc = acc + _dot_nt(refs[p][...], refs[n + p][...])
        refs[-1][...] = acc

    in_specs = ([pl.BlockSpec((tm, a.shape[1]), lambda i: (i, 0)) for a, _ in pairs]
                + [pl.BlockSpec(b.shape, lambda i: (0, 0)) for _, b in pairs])
    return pl.pallas_call(
        body, name=name, grid=(M // tm,), in_specs=in_specs,
        out_specs=pl.BlockSpec((tm, N), lambda i: (i, 0)),
        out_shape=jax.ShapeDtypeStruct((M, N), F32), compiler_params=_params(1),
    )(*[a for a, _ in pairs], *[b for _, b in pairs])


def _mm_tn(a, b, name):
    R, M = a.shape
    _, N = b.shape
    tn = _pick(N, (1536, 1024, 512, 256, 128))
    tr = _pick(R, (512, 256, 128))

    def body(a_ref, b_ref, o_ref):
        @pl.when(pl.program_id(1) == 0)
        def _():
            o_ref[...] = jnp.zeros_like(o_ref)

        o_ref[...] += _dot_tn(a_ref[...], b_ref[...])

    return pl.pallas_call(
        body, name=name, grid=(N // tn, R // tr),
        in_specs=[pl.BlockSpec((tr, M), lambda j, r: (r, 0)), pl.BlockSpec((tr, tn), lambda j, r: (r, j))],
        out_specs=pl.BlockSpec((M, tn), lambda j, r: (0, j)),
        out_shape=jax.ShapeDtypeStruct((M, N), F32), compiler_params=_params(2),
    )(a, b)


def _qkv_post(c, j, n_heads):
    s = _silu(c)
    parts = []
    for h in range(n_heads):
        sh = s[:, h * HEAD_DIM:(h + 1) * HEAD_DIM]
        parts.append(sh * lax.rsqrt(jnp.sum(sh * sh, axis=-1, keepdims=True) + EPS))
    n = jnp.concatenate(parts, axis=-1)
    is_q = (j == 0).astype(F32)
    is_v = (j == 2).astype(F32)
    n = n * (1.0 + is_q * (HEAD_DIM ** -0.5 - 1.0))
    return n * (1.0 - is_v) + s * is_v


def _conv_taps(cur, halo_prev):
    tm = cur.shape[0]
    ext = jnp.concatenate([halo_prev, cur], axis=0)
    taps = [pltpu.roll(ext, s, 0)[8:8 + tm] for s in (3, 2, 1)]
    return taps + [cur]


def _conv_fwd(p, conv_w, d_model, name):
    T = p.shape[0]
    D = d_model
    H = D // HEAD_DIM
    tm = _pick(T, (256, 128, 64))

    def body(cur_ref, prev_ref, w_ref, o_ref):
        i, j = pl.program_id(0), pl.program_id(1)
        prev = prev_ref[...] * (i > 0).astype(F32)
        taps = _conv_taps(cur_ref[...], prev)
        w = w_ref[...]
        c = sum(taps[k] * w[k:k + 1, :] for k in range(4))
        o_ref[...] = _qkv_post(c, j, H)

    return pl.pallas_call(
        body, name=name, grid=(T // tm, 3),
        in_specs=[pl.BlockSpec((tm, D), lambda i, j: (i, j)),
                  pl.BlockSpec((8, D), lambda i, j: (jnp.maximum(i * (tm // 8) - 1, 0), j)),
                  pl.BlockSpec((4, D), lambda i, j: (0, j))],
        out_specs=pl.BlockSpec((tm, D), lambda i, j: (i, j)),
        out_shape=jax.ShapeDtypeStruct((T, 3 * D), F32), compiler_params=_params(2),
    )(p, p, conv_w)


def _chunk_tri(tm, upper):
    r, c = _iota2((tm, tm), 0), _iota2((tm, tm), 1)
    same = (r // CHUNK) == (c // CHUNK)
    tri = (c >= r) if upper else (c <= r)
    return jnp.where(same & tri, 1.0, 0.0).astype(BF16)


def _dot_mask(mask_bf16, x):
    hi, mid, lo = _split3(x)
    d = functools.partial(jnp.dot, preferred_element_type=F32)
    return d(mask_bf16, hi) + (d(mask_bf16, mid) + d(mask_bf16, lo))


def _gates_math(pb, pa, a_log, dt_bias):
    beta = jax.nn.sigmoid(pb)
    g = -jnp.exp(a_log) * _softplus(pa + dt_bias)
    return beta, g


def _gates_fwd(p, gate_params, col0, name):
    T = p.shape[0]
    tm = _pick(T, (256, 128, 64))

    def body(pb_ref, pa_ref, gp_ref, beta_ref, gc_ref):
        gp = gp_ref[...]
        beta, g = _gates_math(pb_ref[...], pa_ref[...], gp[0:1, :], gp[1:2, :])
        beta_ref[...] = beta
        gc_ref[...] = _dot_mask(_chunk_tri(tm, upper=False), g)

    return pl.pallas_call(
        body, name=name, grid=(T // tm,),
        in_specs=[pl.BlockSpec((tm, LANES), lambda i: (i, col0)), pl.BlockSpec((tm, LANES), lambda i: (i, col0 + 1)),
                  pl.BlockSpec((8, LANES), lambda i: (0, 0))],
        out_specs=[pl.BlockSpec((tm, LANES), lambda i: (i, 0))] * 2,
        out_shape=[jax.ShapeDtypeStruct((T, LANES), F32)] * 2, compiler_params=_params(1),
    )(p, p, gate_params)


def _col_to_row(col):
    C = col.shape[0]
    eye = _iota2((C, C), 0) == _iota2((C, C), 1)
    return jnp.sum(jnp.where(eye, col, 0.0), axis=0, keepdims=True)


def _lockstep(generators):
    generators = list(generators)
    results = [None] * len(generators)
    live = list(range(len(generators)))
    while live:
        for idx in list(live):
            try:
                next(generators[idx])
            except StopIteration as done:
                results[idx] = done.value
                live.remove(idx)
    return results


def _unit_lower_inverse(low):
    C = low.shape[0]
    eye = (_iota2((C, C), 0) == _iota2((C, C), 1)).astype(F32)
    t = eye - low
    p = _dot3(low, low)
    yield
    n = 2
    while True:
        tp = _dot3(t, p)
        n *= 2
        if n < C:
            p = _dot3(p, p)
        yield
        t = t + tp
        if n >= C:
            return t


def _chunk_head_fwd(q, k, v, gc, beta, s_in):
    C = q.shape[0]
    r, c = _iota2((C, C), 0), _iota2((C, C), 1)
    causal, strict = r >= c, r > c
    decay = jnp.where(causal, jnp.exp(jnp.where(causal, gc - _col_to_row(gc), 0.0)), 0.0)
    kb, vb = k * beta, v * beta
    eg = jnp.exp(gc)
    kk = _dot_nt(kb, k)
    qk = _dot_nt(q, k)
    o_state = _dot(q * eg, s_in)
    yield
    t_inv = yield from _unit_lower_inverse(jnp.where(strict, kk * decay, 0.0))
    u = _dot(t_inv, vb)
    w = _dot(t_inv, kb * eg)
    yield
    w_state = _dot(w, s_in)
    yield
    v_new = u - w_state
    g_last = gc[C - 1:C, :]
    o_intra = _dot(qk * decay, v_new)
    s_add = _dot_tn(k * jnp.exp(g_last - gc), v_new)
    yield
    return o_state + o_intra, s_in * jnp.exp(g_last) + s_add, t_inv


def _chunk_fwd(qkv, beta, gc, d_model, name):
    T = qkv.shape[0]
    D = d_model
    H = D // HEAD_DIM
    N = T // CHUNK

    def body(q_ref, k_ref, v_ref, beta_ref, gc_ref, o_ref, s_all_ref, t_all_ref, s_ref):
        @pl.when(pl.program_id(0) == 0)
        def _():
            s_ref[...] = jnp.zeros_like(s_ref)

        heads = [slice(h * HEAD_DIM, (h + 1) * HEAD_DIM) for h in range(H)]
        s_all_ref[0] = s_ref[...]
        results = _lockstep(_chunk_head_fwd(q_ref[:, hs], k_ref[:, hs], v_ref[:, hs], gc_ref[:, h:h + 1],
                                            beta_ref[:, h:h + 1], s_ref[h]) for h, hs in enumerate(heads))
        for h, (o, s_out, t_inv) in enumerate(results):
            o_ref[:, heads[h]] = o
            s_ref[h] = s_out
            t_all_ref[0, h] = t_inv

    return pl.pallas_call(
        body, name=name, grid=(N,),
        in_specs=[pl.BlockSpec((CHUNK, D), lambda n: (n, 0)), pl.BlockSpec((CHUNK, D), lambda n: (n, 1)),
                  pl.BlockSpec((CHUNK, D), lambda n: (n, 2)),
                  pl.BlockSpec((CHUNK, LANES), lambda n: (n, 0)), pl.BlockSpec((CHUNK, LANES), lambda n: (n, 0))],
        out_specs=[pl.BlockSpec((CHUNK, D), lambda n: (n, 0)),
                   pl.BlockSpec((1, H, HEAD_DIM, HEAD_DIM), lambda n: (n, 0, 0, 0)),
                   pl.BlockSpec((1, H, CHUNK, CHUNK), lambda n: (n, 0, 0, 0))],
        out_shape=[jax.ShapeDtypeStruct((T, D), F32), jax.ShapeDtypeStruct((N, H, HEAD_DIM, HEAD_DIM), F32),
                   jax.ShapeDtypeStruct((N, H, CHUNK, CHUNK), F32)],
        scratch_shapes=[pltpu.VMEM((H, HEAD_DIM, HEAD_DIM), F32)], compiler_params=_params(1),
    )(qkv, qkv, qkv, beta, gc)


def _onorm_gate_math(o, z, w, n_heads):
    parts = []
    for h in range(n_heads):
        hs = slice(h * HEAD_DIM, (h + 1) * HEAD_DIM)
        oh = o[:, hs]
        y = oh * lax.rsqrt(jnp.mean(oh * oh, axis=-1, keepdims=True) + EPS) * w
        parts.append(y * _silu(z[:, hs]))
    return jnp.concatenate(parts, axis=-1)


def _onorm_gate_fwd(o, p, z_col, o_norm, name):
    T, D = o.shape
    H = D // HEAD_DIM
    tm = _pick(T, (256, 128, 64))

    def body(o_ref, z_ref, w_ref, y_ref):
        y_ref[...] = _onorm_gate_math(o_ref[...], z_ref[...], w_ref[...], H).astype(BF16)

    return pl.pallas_call(
        body, name=name, grid=(T // tm,),
        in_specs=[pl.BlockSpec((tm, D), lambda i: (i, 0)), pl.BlockSpec((tm, D), lambda i: (i, z_col)),
                  pl.BlockSpec((1, HEAD_DIM), lambda i: (0, 0))],
        out_specs=pl.BlockSpec((tm, D), lambda i: (i, 0)),
        out_shape=jax.ShapeDtypeStruct((T, D), BF16), compiler_params=_params(1),
    )(o, p, o_norm)


def _diag_mask(qb, kb, d):
    return _iota2((qb, kb), 0) > _iota2((qb, kb), 1) + d * kb


def _fill_score_masks(mask_buf, qb, kb, ns):
    mask_buf[0] = jnp.zeros(mask_buf.shape[1:], F32)
    for d in range(ns):
        for half in range(2):
            mask_buf[d + 1, :, half * kb:(half + 1) * kb] = jnp.where(_diag_mask(qb, kb, 2 * d + half), 0.0, MASKED_SCORE)


def _softplus_bits(w):
    u = 1.0 + jnp.exp2(jnp.minimum(w, 64.0))
    return jnp.maximum(w, jnp.log2(u)), 1.0 / u


def _incl_lower(n):
    return jnp.where((_iota2((2 * n, n), 0) & (n - 1)) >= _iota2((2 * n, n), 1), 1.0, 0.0).astype(BF16)


def _incl_upper(n):
    return jnp.where((_iota2((2 * n, n), 0) & (n - 1)) <= _iota2((2 * n, n), 1), 1.0, 0.0).astype(BF16)


def _dot_cum(x, tri_bf16):
    hi, lo = _split2(x)
    return jnp.dot(jnp.concatenate([hi, lo], axis=1), tri_bf16, preferred_element_type=F32)


def _sb_fwd(qkv, d_model, name):
    T = qkv.shape[0]
    D = d_model
    H = D // HEAD_DIM
    QB = _pick(T, ATTN_Q_BLOCKS_FWD)
    KB = ATTN_K_BLOCK
    KS = 2 * KB
    ns = QB // KS
    nq = T // QB
    scale = HEAD_DIM ** -0.5

    def body(q_ref, k_ref, v_ref, o_ref, r_ref, w_buf, cum_buf, mask_buf):
        i = pl.program_id(1)

        @pl.when(i == 0)
        def _():
            _fill_score_masks(mask_buf, QB, KB, ns)

        q = q_ref[...]
        tri = _incl_lower(KB)
        n_tot = (i + 1) * ns

        def key_step(m):
            return jnp.maximum(n_tot - 1 - m, 0)

        def rows(ref, s):
            return ref[pl.ds(pl.multiple_of(s * KS, KS), KS), :]

        def scores(s):
            return _dot_nt(q, rows(k_ref, s)) * (scale * LOG2_E) + mask_buf[jnp.maximum(s - i * ns + 1, 0)]

        def cums(w):
            sp = _softplus_bits(w)[0]
            return jnp.concatenate([_dot_cum(sp[:, :KB], tri), _dot_cum(sp[:, KB:], tri)], axis=1)

        def weights(w, cum, carry):
            a_r = jnp.exp2(w[:, KB:] - cum[:, KB:] - carry)
            carry = carry + cum[:, KB:KB + 1]
            a_l = jnp.exp2(w[:, :KB] - cum[:, :KB] - carry)
            return jnp.concatenate([a_l, a_r], axis=1).astype(BF16), carry + cum[:, 0:1]

        def trip(m, carry):
            w_new = scores(key_step(m + 2))
            a, carry = weights(w_buf[m % 3], cum_buf[m % 2], carry)
            o_ref[...] += _dot(a, rows(v_ref, key_step(m)))
            cum_buf[(m + 1) % 2] = cums(w_buf[(m + 1) % 3])
            w_buf[(m + 2) % 3] = w_new
            return carry

        o_ref[...] = jnp.zeros_like(o_ref)
        w_buf[0] = scores(key_step(0))
        w_buf[1] = scores(key_step(1))
        cum_buf[0] = cums(w_buf[0])
        carry = lax.fori_loop(0, n_tot // 2, lambda j, c: trip(2 * j + 1, trip(2 * j, c)), jnp.zeros((QB, 1), F32))
        r_ref[0] = jnp.broadcast_to(carry, (QB, LANES))

    return pl.pallas_call(
        body, name=name, grid=(H, nq),
        in_specs=[pl.BlockSpec((QB, HEAD_DIM), lambda h, i: (i, h)),
                  pl.BlockSpec((T, HEAD_DIM), lambda h, i: (0, H + h)),
                  pl.BlockSpec((T, HEAD_DIM), lambda h, i: (0, 2 * H + h))],
        out_specs=[pl.BlockSpec((QB, HEAD_DIM), lambda h, i: (i, h)),
                   pl.BlockSpec((1, QB, LANES), lambda h, i: (h, i, 0))],
        out_shape=[jax.ShapeDtypeStruct((T, D), F32), jax.ShapeDtypeStruct((H, T, LANES), F32)],
        scratch_shapes=[pltpu.VMEM((3, QB, KS), F32), pltpu.VMEM((2, QB, KS), F32), pltpu.VMEM((ns + 1, QB, KS), F32)],
        compiler_params=_params(2),
    )(qkv, qkv, qkv)


def _gate_mul_fwd(o, gate, name):
    T, D = o.shape
    tm = _pick(T, (512, 256, 128))

    def body(o_ref, g_ref, y_ref):
        y_ref[...] = (o_ref[...] * _silu(g_ref[...])).astype(BF16)

    spec = pl.BlockSpec((tm, D), lambda i: (i, 0))
    return pl.pallas_call(body, name=name, grid=(T // tm,), in_specs=[spec, spec], out_specs=spec,
                          out_shape=jax.ShapeDtypeStruct((T, D), BF16), compiler_params=_params(1))(o, gate)


def _final_loss(h, w, target, name):
    T, D = h.shape
    tm = _pick(T, (512, 256, 128))

    def body(h_ref, w_ref, t_ref, dh_ref, loss_ref, dw_ref):
        x, w = h_ref[...], w_ref[...]
        r = lax.rsqrt(jnp.mean(x * x, axis=-1, keepdims=True) + EPS)
        err = x * r * w - t_ref[...]
        part = 0.5 * jnp.sum(jnp.mean(err * err, axis=-1, keepdims=True), axis=0, keepdims=True)
        dx, dw = _rms_bwd_math(x, w, err * (1.0 / D))
        dh_ref[...] = dx

        @pl.when(pl.program_id(0) == 0)
        def _():
            loss_ref[...] = jnp.zeros_like(loss_ref)
            dw_ref[...] = jnp.zeros_like(dw_ref)

        loss_ref[...] += jnp.broadcast_to(part, loss_ref.shape)
        dw_ref[...] += jnp.broadcast_to(dw, dw_ref.shape)

    return pl.pallas_call(
        body, name=name, grid=(T // tm,),
        in_specs=[pl.BlockSpec((tm, D), lambda i: (i, 0)), pl.BlockSpec((1, D), lambda i: (0, 0)),
                  pl.BlockSpec((tm, D), lambda i: (i, 0))],
        out_specs=[pl.BlockSpec((tm, D), lambda i: (i, 0)), pl.BlockSpec((8, LANES), lambda i: (0, 0)),
                   pl.BlockSpec((8, D), lambda i: (0, 0))],
        out_shape=[jax.ShapeDtypeStruct((T, D), F32), jax.ShapeDtypeStruct((8, LANES), F32),
                   jax.ShapeDtypeStruct((8, D), F32)],
        compiler_params=_params(1),
    )(h, w, target)


def _gate_mul_bwd(dy, o, gate, name):
    T, D = o.shape
    tm = _pick(T, (512, 256, 128))

    def body(dy_ref, o_ref, g_ref, do_ref, dg_ref):
        dy, g = dy_ref[...], g_ref[...]
        s = jax.nn.sigmoid(g)
        do_ref[...] = dy * (g * s)
        dg_ref[...] = (dy * o_ref[...] * (s + g * s * (1.0 - s))).astype(BF16)

    spec = pl.BlockSpec((tm, D), lambda i: (i, 0))
    return pl.pallas_call(body, name=name, grid=(T // tm,), in_specs=[spec] * 3, out_specs=[spec] * 2,
                          out_shape=[jax.ShapeDtypeStruct((T, D), F32), jax.ShapeDtypeStruct((T, D), BF16)],
                          compiler_params=_params(1))(dy, o, gate)


def _sb_bwd(qkv, do, r_tot, d_model, name):
    T = qkv.shape[0]
    D = d_model
    H = D // HEAD_DIM
    QB = _pick(T, ATTN_Q_BLOCKS_BWD)
    KB = ATTN_K_BLOCK
    KS = 2 * KB
    ns = QB // KS
    nq = T // QB
    n_key_steps = T // KS
    scale = HEAD_DIM ** -0.5

    def body(q_ref, k_ref, v_ref, do_ref, r_ref, dq_ref, dk_ref, dv_ref,
             dkt_acc, dvt_acc, dq_acc, w_buf, da_buf, cum_buf, sig_buf, mask_buf):
        i = pl.program_id(1)

        @pl.when(i == 0)
        def _():
            dkt_acc[...] = jnp.zeros_like(dkt_acc)
            dvt_acc[...] = jnp.zeros_like(dvt_acc)
            _fill_score_masks(mask_buf, QB, KB, ns)

        q = q_ref[...]
        do_blk = do_ref[...].astype(BF16)
        q_t = q.astype(F32).T.astype(BF16)
        do_t = do_ref[...].T.astype(BF16)
        row_total = r_ref[0][:, 0:1]
        tri_rev = _incl_lower(KB)
        tri_fwd = jnp.where(_iota2((KB, KB), 0) <= _iota2((KB, KB), 1), 1.0, 0.0).astype(BF16)
        n_tot = (i + 1) * ns

        def step_rows(ref, s):
            return ref[pl.ds(pl.multiple_of(s * KS, KS), KS), :]

        def scores(s):
            w = _dot_nt(q, step_rows(k_ref, s)) * (scale * LOG2_E) + mask_buf[jnp.maximum(s - i * ns + 1, 0)]
            return w, _dot_nt(do_blk, step_rows(v_ref, s))

        def softplus_sums(w):
            sp, one_minus_sig = _softplus_bits(w)
            cum = jnp.concatenate([_dot_cum(sp[:, :KB], tri_rev), _dot_cum(sp[:, KB:], tri_rev)], axis=1)
            return cum, 1.0 - one_minus_sig

        def weights(w, cum, da, left_sp):
            right_l = row_total - left_sp - cum[:, 0:1]
            right_r = right_l - cum[:, KB:KB + 1]
            a = jnp.concatenate([jnp.exp2(w[:, :KB] - cum[:, :KB] - right_l),
                                 jnp.exp2(w[:, KB:] - cum[:, KB:] - right_r)], axis=1)
            p = da * a
            cp = jnp.concatenate([_dot(p[:, :KB], tri_fwd), _dot(p[:, KB:], tri_fwd)], axis=1)
            return a.astype(BF16), p, cp, row_total - right_r

        def score_grads(p, cp, sig, left_p):
            cum_l = cp[:, :KB] + left_p
            cum_r = cp[:, KB:] + cum_l[:, KB - 1:KB]
            dz = p - sig * jnp.concatenate([cum_l, cum_r], axis=1)
            return dz.astype(BF16), cum_r[:, KB - 1:KB]

        def trip(m, st):
            left_sp, left_p = st
            s2 = jnp.minimum(m + 2, n_tot - 1)
            s1 = jnp.minimum(m + 1, n_tot - 1)
            w_new, da_new = scores(s2)
            a, p, cp, left_sp = weights(w_buf[m % 3], cum_buf[m % 2], da_buf[m % 3], left_sp)
            cum_new, sig_new = softplus_sums(w_buf[s1 % 3])
            dz, left_p = score_grads(p, cp, sig_buf[m % 2], left_p)
            dq_acc[...] += _dot(dz, step_rows(k_ref, m))
            dkt_acc[m] += jnp.dot(q_t, dz, preferred_element_type=F32) * scale
            dvt_acc[m] += jnp.dot(do_t, a, preferred_element_type=F32)
            cum_buf[(m + 1) % 2] = cum_new
            sig_buf[(m + 1) % 2] = sig_new
            w_buf[(m + 2) % 3] = w_new
            da_buf[(m + 2) % 3] = da_new
            return left_sp, left_p

        dq_acc[...] = jnp.zeros_like(dq_acc)
        w_buf[0], da_buf[0] = scores(0)
        w_buf[1], da_buf[1] = scores(jnp.minimum(1, n_tot - 1))
        cum_buf[0], sig_buf[0] = softplus_sums(w_buf[0])
        zero_col = jnp.zeros((QB, 1), F32)
        lax.fori_loop(0, n_tot // 2, lambda j, c: trip(2 * j + 1, trip(2 * j, c)), (zero_col, zero_col))
        dq_ref[...] = (dq_acc[...] * scale).astype(BF16)

        @pl.when(i == nq - 1)
        def _():
            for s in range(n_key_steps):
                dk_ref[s * KS:(s + 1) * KS, :] = dkt_acc[s].T.astype(BF16)
                dv_ref[s * KS:(s + 1) * KS, :] = dvt_acc[s].T.astype(BF16)

    return pl.pallas_call(
        body, name=name, grid=(H, nq),
        in_specs=[pl.BlockSpec((QB, HEAD_DIM), lambda h, i: (i, h)),
                  pl.BlockSpec((T, HEAD_DIM), lambda h, i: (0, H + h)),
                  pl.BlockSpec((T, HEAD_DIM), lambda h, i: (0, 2 * H + h)),
                  pl.BlockSpec((QB, HEAD_DIM), lambda h, i: (i, h)),
                  pl.BlockSpec((1, QB, LANES), lambda h, i: (h, i, 0))],
        out_specs=[pl.BlockSpec((QB, HEAD_DIM), lambda h, i: (i, h)),
                   pl.BlockSpec((T, HEAD_DIM), lambda h, i: (0, h)),
                   pl.BlockSpec((T, HEAD_DIM), lambda h, i: (0, h))],
        out_shape=[jax.ShapeDtypeStruct((T, D), BF16)] * 3,
        scratch_shapes=[pltpu.VMEM((n_key_steps, HEAD_DIM, KS), F32), pltpu.VMEM((n_key_steps, HEAD_DIM, KS), F32),
                        pltpu.VMEM((QB, HEAD_DIM), F32), pltpu.VMEM((3, QB, KS), F32), pltpu.VMEM((3, QB, KS), F32),
                        pltpu.VMEM((2, QB, KS), F32), pltpu.VMEM((2, QB, KS), F32), pltpu.VMEM((ns + 1, QB, KS), F32)],
        compiler_params=_params(2),
    )(qkv, qkv, qkv, do, r_tot)


def _onorm_gate_bwd(dy, o, p, z_col, o_norm, name):
    T, D = o.shape
    H = D // HEAD_DIM
    tm = _pick(T, (256, 128, 64))

    def body(dy_ref, o_ref, z_ref, w_ref, do_ref, dz_ref, dw_ref):
        _, vjp = jax.vjp(functools.partial(_onorm_gate_math, n_heads=H), o_ref[...], z_ref[...], w_ref[...])
        do, dz, dw = vjp(dy_ref[...])
        do_ref[...] = do
        dz_ref[...] = dz.astype(BF16)

        @pl.when(pl.program_id(0) == 0)
        def _():
            dw_ref[...] = jnp.zeros_like(dw_ref)

        dw_ref[...] += jnp.broadcast_to(dw, dw_ref.shape)

    return pl.pallas_call(
        body, name=name, grid=(T // tm,),
        in_specs=[pl.BlockSpec((tm, D), lambda i: (i, 0)), pl.BlockSpec((tm, D), lambda i: (i, 0)),
                  pl.BlockSpec((tm, D), lambda i: (i, z_col)), pl.BlockSpec((1, HEAD_DIM), lambda i: (0, 0))],
        out_specs=[pl.BlockSpec((tm, D), lambda i: (i, 0)), pl.BlockSpec((tm, D), lambda i: (i, 0)),
                   pl.BlockSpec((8, HEAD_DIM), lambda i: (0, 0))],
        out_shape=[jax.ShapeDtypeStruct((T, D), F32), jax.ShapeDtypeStruct((T, D), BF16),
                   jax.ShapeDtypeStruct((8, HEAD_DIM), F32)],
        compiler_params=_params(1),
    )(dy, o, p, o_norm)


def _row_to_col(row):
    C = row.shape[1]
    eye = _iota2((C, C), 0) == _iota2((C, C), 1)
    return jnp.sum(jnp.where(eye, row, 0.0), axis=1, keepdims=True)


def _lane_sum(x):
    return jnp.sum(x, axis=-1, keepdims=True)


def _chunk_head_bwd(q, k, v, gc, beta, s_in, t_inv, do, ds_out):
    C = q.shape[0]
    r, c = _iota2((C, C), 0), _iota2((C, C), 1)
    causal, strict = r >= c, r > c
    decay = jnp.where(causal, jnp.exp(jnp.where(causal, gc - _col_to_row(gc), 0.0)), 0.0)
    kb, vb = k * beta, v * beta
    eg = jnp.exp(gc)
    kbg = kb * eg
    g_last = gc[C - 1:C, :]
    e_tail = jnp.exp(g_last - gc)
    k_tail = k * e_tail
    gl = jnp.exp(g_last)
    qg = q * eg
    t_inv_t = t_inv.T
    kk = _dot_nt(kb, k)
    u = _dot(t_inv, vb)
    w = _dot(t_inv, kbg)
    qk = _dot_nt(q, k)
    d_qg = _dot_nt(do, s_in)
    ds_state = _dot_tn(qg, do)
    yield
    low = jnp.where(strict, kk * decay, 0.0)
    attn = qk * decay
    w_state = _dot(w, s_in)
    d_vnew_intra = _dot_tn(attn, do)
    d_vnew_state = _dot(k_tail, ds_out)
    yield
    v_new = u - w_state
    d_vnew = d_vnew_intra + d_vnew_state
    d_ktail = _dot_nt(v_new, ds_out)
    d_attn_raw = _dot_nt(do, v_new)
    d_w = -_dot_nt(d_vnew, s_in)
    ds_w = _dot_tn(w, d_vnew)
    d_vb = _dot(t_inv_t, d_vnew)
    d_tinv_u = _dot_nt(d_vnew, vb)
    yield
    d_gl = jnp.sum(_lane_sum(s_in * ds_out), axis=0, keepdims=True)
    d_attn = jnp.where(causal, d_attn_raw, 0.0)
    ds_in = ds_out * gl + ds_state - ds_w
    d_kbg = _dot(t_inv_t, d_w)
    d_tinv_w = _dot_nt(d_w, kbg)
    d_qk = d_attn * decay
    dq_intra = _dot(d_qk, k)
    dk_intra = _dot_tn(d_qk, q)
    yield
    inner = _dot(t_inv_t, d_tinv_u + d_tinv_w)
    yield
    d_low_raw = _dot_nt(inner, t_inv)
    yield
    d_low = jnp.where(strict, -d_low_raw, 0.0)
    d_kk = d_low * decay
    d_kb_low = _dot(d_kk, k)
    dk_low = _dot_tn(d_kk, kb)
    yield
    d_kb = d_kb_low + d_kbg * eg
    dq = dq_intra + d_qg * eg
    dk = dk_low + dk_intra + d_ktail * e_tail + d_kb * beta
    dv = d_vb * beta
    dbeta = _lane_sum(d_kb * k) + _lane_sum(d_vb * v)
    m = d_low * low + d_attn * attn
    tail_term = _lane_sum(d_ktail * k_tail)
    d_g_last = d_gl * gl + jnp.sum(tail_term, axis=0, keepdims=True)
    dgc = (_lane_sum(m) - _row_to_col(jnp.sum(m, axis=0, keepdims=True))
           + _lane_sum(d_qg * qg) + _lane_sum(d_kbg * kbg) - tail_term)
    dgc = dgc + jnp.where(_iota2((C, 1), 0) == C - 1, d_g_last, 0.0)
    return dq, dk, dv, dgc, dbeta, ds_in


def _chunk_bwd(qkv, beta, gc, s_all, t_all, do, d_model, name):
    T = qkv.shape[0]
    D = d_model
    H = D // HEAD_DIM
    N = T // CHUNK

    def body(q_ref, k_ref, v_ref, beta_ref, gc_ref, s_ref, t_ref, do_ref, dqkv_ref, dbeta_ref, dg_ref, ds_ref):
        @pl.when(pl.program_id(0) == 0)
        def _():
            ds_ref[...] = jnp.zeros_like(ds_ref)

        lane = _iota2((CHUNK, LANES), 1)
        dgc_all = jnp.zeros((CHUNK, LANES), F32)
        dbeta_all = jnp.zeros((CHUNK, LANES), F32)
        results = _lockstep(
            _chunk_head_bwd(q_ref[:, hs], k_ref[:, hs], v_ref[:, hs], gc_ref[:, h:h + 1], beta_ref[:, h:h + 1],
                            s_ref[0, h], t_ref[0, h], do_ref[:, hs], ds_ref[h])
            for h, hs in enumerate(slice(h * HEAD_DIM, (h + 1) * HEAD_DIM) for h in range(H)))
        for h, (dq, dk, dv, dgc, dbeta, ds_in) in enumerate(results):
            ds_ref[h] = ds_in
            dqkv_ref[:, h * HEAD_DIM:(h + 1) * HEAD_DIM] = dq
            dqkv_ref[:, D + h * HEAD_DIM:D + (h + 1) * HEAD_DIM] = dk
            dqkv_ref[:, 2 * D + h * HEAD_DIM:2 * D + (h + 1) * HEAD_DIM] = dv
            dgc_all = jnp.where(lane == h, dgc, dgc_all)
            dbeta_all = jnp.where(lane == h, dbeta, dbeta_all)
        dbeta_ref[...] = dbeta_all
        dg_ref[...] = _dot_mask(_chunk_tri(CHUNK, upper=True), dgc_all)

    rev = lambda n: N - 1 - n
    return pl.pallas_call(
        body, name=name, grid=(N,),
        in_specs=[pl.BlockSpec((CHUNK, D), lambda n: (rev(n), 0)), pl.BlockSpec((CHUNK, D), lambda n: (rev(n), 1)),
                  pl.BlockSpec((CHUNK, D), lambda n: (rev(n), 2)),
                  pl.BlockSpec((CHUNK, LANES), lambda n: (rev(n), 0)), pl.BlockSpec((CHUNK, LANES), lambda n: (rev(n), 0)),
                  pl.BlockSpec((1, H, HEAD_DIM, HEAD_DIM), lambda n: (rev(n), 0, 0, 0)),
                  pl.BlockSpec((1, H, CHUNK, CHUNK), lambda n: (rev(n), 0, 0, 0)),
                  pl.BlockSpec((CHUNK, D), lambda n: (rev(n), 0))],
        out_specs=[pl.BlockSpec((CHUNK, 3 * D), lambda n: (rev(n), 0)),
                   pl.BlockSpec((CHUNK, LANES), lambda n: (rev(n), 0)), pl.BlockSpec((CHUNK, LANES), lambda n: (rev(n), 0))],
        out_shape=[jax.ShapeDtypeStruct((T, 3 * D), F32), jax.ShapeDtypeStruct((T, LANES), F32),
                   jax.ShapeDtypeStruct((T, LANES), F32)],
        scratch_shapes=[pltpu.VMEM((H, HEAD_DIM, HEAD_DIM), F32)], compiler_params=_params(1),
    )(qkv, qkv, qkv, beta, gc, s_all, t_all, do)


def _gates_bwd(p, gate_params, col0, dbeta, dg, name):
    T = p.shape[0]
    tm = _pick(T, (256, 128, 64))

    def body(pb_ref, pa_ref, gp_ref, dbeta_ref, dg_ref, dp_ref, dgp_ref):
        gp = gp_ref[...]
        _, vjp = jax.vjp(_gates_math, pb_ref[...], pa_ref[...], gp[0:1, :], gp[1:2, :])
        dpb, dpa, d_alog, d_dt = vjp((dbeta_ref[...], dg_ref[...]))
        dp_ref[:, 0:LANES] = dpb.astype(BF16)
        dp_ref[:, LANES:2 * LANES] = dpa.astype(BF16)

        @pl.when(pl.program_id(0) == 0)
        def _():
            dgp_ref[...] = jnp.zeros_like(dgp_ref)

        dgp_ref[0:1, :] += d_alog
        dgp_ref[1:2, :] += d_dt

    return pl.pallas_call(
        body, name=name, grid=(T // tm,),
        in_specs=[pl.BlockSpec((tm, LANES), lambda i: (i, col0)), pl.BlockSpec((tm, LANES), lambda i: (i, col0 + 1)),
                  pl.BlockSpec((8, LANES), lambda i: (0, 0)),
                  pl.BlockSpec((tm, LANES), lambda i: (i, 0)), pl.BlockSpec((tm, LANES), lambda i: (i, 0))],
        out_specs=[pl.BlockSpec((tm, 2 * LANES), lambda i: (i, 0)), pl.BlockSpec((8, LANES), lambda i: (0, 0))],
        out_shape=[jax.ShapeDtypeStruct((T, 2 * LANES), BF16), jax.ShapeDtypeStruct((8, LANES), F32)],
        compiler_params=_params(1),
    )(p, p, gate_params, dbeta, dg)


def _conv_bwd_act(p, conv_w, dqkv, d_model, name):
    T = p.shape[0]
    D = d_model
    H = D // HEAD_DIM
    tm = _pick(T, (256, 128, 64))

    def body(cur_ref, prev_ref, w_ref, dout_ref, dc_ref, dw_ref):
        j, i = pl.program_id(0), pl.program_id(1)
        prev = prev_ref[...] * (i > 0).astype(F32)
        taps = _conv_taps(cur_ref[...], prev)
        w = w_ref[...]
        c = sum(taps[k] * w[k:k + 1, :] for k in range(4))
        _, vjp = jax.vjp(lambda cc: _qkv_post(cc, j, H), c)
        (dc,) = vjp(dout_ref[...])
        dc_ref[...] = dc

        @pl.when(i == 0)
        def _():
            dw_ref[...] = jnp.zeros_like(dw_ref)

        for k in range(4):
            dw_ref[k:k + 1, :] += jnp.sum(dc * taps[k], axis=0, keepdims=True)

    return pl.pallas_call(
        body, name=name, grid=(3, T // tm),
        in_specs=[pl.BlockSpec((tm, D), lambda j, i: (i, j)),
                  pl.BlockSpec((8, D), lambda j, i: (jnp.maximum(i * (tm // 8) - 1, 0), j)),
                  pl.BlockSpec((4, D), lambda j, i: (0, j)),
                  pl.BlockSpec((tm, D), lambda j, i: (i, j))],
        out_specs=[pl.BlockSpec((tm, D), lambda j, i: (i, j)), pl.BlockSpec((4, D), lambda j, i: (0, j))],
        out_shape=[jax.ShapeDtypeStruct((T, 3 * D), F32), jax.ShapeDtypeStruct((4, 3 * D), F32)],
        compiler_params=_params(2),
    )(p, p, conv_w, dqkv)


def _conv_bwd_input(dc, conv_w, name):
    T, D3 = dc.shape
    D = D3 // 3
    tm = _pick(T, (256, 128, 64))
    n_t = T // tm

    def body(cur_ref, next_ref, w_ref, dp_ref):
        i = pl.program_id(0)
        cur = cur_ref[...]
        nxt = next_ref[...] * (i < n_t - 1).astype(F32)
        ext = jnp.concatenate([cur, nxt], axis=0)
        w = w_ref[...]
        acc = cur * w[3:4, :]
        for s in (1, 2, 3):
            acc = acc + pltpu.roll(ext, tm + 8 - s, 0)[0:tm] * w[3 - s:4 - s, :]
        dp_ref[...] = acc.astype(BF16)

    return pl.pallas_call(
        body, name=name, grid=(n_t, 3),
        in_specs=[pl.BlockSpec((tm, D), lambda i, j: (i, j)),
                  pl.BlockSpec((8, D), lambda i, j: (jnp.minimum((i + 1) * (tm // 8), T // 8 - 1), j)),
                  pl.BlockSpec((4, D), lambda i, j: (0, j))],
        out_specs=pl.BlockSpec((tm, D), lambda i, j: (i, j)),
        out_shape=jax.ShapeDtypeStruct((T, D3), BF16), compiler_params=_params(2),
    )(dc, dc, conv_w)


def _comm_call(body, arrays, out_shape, n_remote, n_local, name):
    any_spec = pl.BlockSpec(memory_space=pl.ANY)
    return pl.pallas_call(
        body, name=name, in_specs=[any_spec] * len(arrays), out_specs=[any_spec] * len(out_shape), out_shape=out_shape,
        scratch_shapes=[pltpu.SemaphoreType.DMA((n_remote,)), pltpu.SemaphoreType.DMA((n_remote,)),
                        pltpu.SemaphoreType.DMA((n_local,))],
        compiler_params=pltpu.CompilerParams(has_side_effects=True),
    )(*arrays)


def _gather_shards(arrays, name):
    n = len(arrays)

    def body(*refs):
        ins, outs = refs[:n], refs[n:2 * n]
        send_sems, recv_sems, local_sems = refs[2 * n:]
        x, y, c = lax.axis_index("x"), lax.axis_index("y"), lax.axis_index("c")
        me, sibling = (x, y, c), (x, y, 1 - c)
        chips = [(1 - x, y), (x, 1 - y), (1 - x, 1 - y)]

        def copy(a, k, block, to, src=None):
            dst = outs[a].at[4 * block[0] + 2 * block[1] + block[2]]
            return pltpu.make_async_remote_copy(
                src_ref=dst if src is None else src, dst_ref=dst, send_sem=send_sems.at[a * 7 + k],
                recv_sem=recv_sems.at[a * 7 + k], device_id=to, device_id_type=MESH_ID)

        local = [pltpu.make_async_copy(ins[a], outs[a].at[4 * x + 2 * y + c], local_sems.at[a]) for a in range(n)]
        first = [copy(a, 0, me, sibling, src=ins[a]) for a in range(n)]
        first += [copy(a, 1 + j, me, (*chip, c), src=ins[a]) for j, chip in enumerate(chips) for a in range(n)]
        for cp in local + first:
            cp.start()
        passed = []
        for j, chip in enumerate(chips):
            for a in range(n):
                copy(a, 1 + j, (*chip, c), me).wait_recv()
                passed.append(copy(a, 4 + j, (*chip, c), sibling))
                passed[-1].start()
        for a in range(n):
            copy(a, 0, sibling, me).wait_recv()
            for j, chip in enumerate(chips):
                copy(a, 4 + j, (*chip, 1 - c), me).wait_recv()
        for cp in first + passed:
            cp.wait_send()
        for cp in local:
            cp.wait()

    out_shape = [jax.ShapeDtypeStruct((N_DEV,) + a.shape, a.dtype) for a in arrays]
    return _comm_call(body, arrays, out_shape, 7 * n, n, name)


def _pair_exchange(arrays, name):
    n = len(arrays)

    def body(*refs):
        ins, pair = refs[:n], refs[n:2 * n]
        send_sems, recv_sems, _ = refs[2 * n:]
        x, y, c = lax.axis_index("x"), lax.axis_index("y"), lax.axis_index("c")
        sends = [pltpu.make_async_remote_copy(
            src_ref=ins[a].at[1 - c], dst_ref=pair[a], send_sem=send_sems.at[a], recv_sem=recv_sems.at[a],
            device_id=(x, y, 1 - c), device_id_type=MESH_ID) for a in range(n)]
        for cp in sends:
            cp.start()
        for cp in sends:
            cp.wait_recv()
        for cp in sends:
            cp.wait_send()

    out_shape = [jax.ShapeDtypeStruct(a.shape[1:], a.dtype) for a in arrays]
    return _comm_call(body, arrays, out_shape, n, 1, name)


def _pair_add(own, pair, name, out_dtype):
    _, R, C = own.shape
    tr = next((t for t in (256, 128, 64, 32, 16) if R % t == 0), R)

    def body(a_ref, b_ref, o_ref):
        o_ref[...] = (a_ref[...] + b_ref[...]).astype(out_dtype)

    spec = pl.BlockSpec((1, tr, C), lambda q, i: (q, i, 0))
    return pl.pallas_call(body, name=name, grid=(4, R // tr), in_specs=[spec, spec], out_specs=spec,
                          out_shape=jax.ShapeDtypeStruct(own.shape, out_dtype), compiler_params=_params(2))(own, pair)


def _chip_exchange(arrays, name):
    n = len(arrays)

    def body(*refs):
        ins, outs = refs[:n], refs[n:2 * n]
        send_sems, recv_sems, local_sems = refs[2 * n:]
        x, y, c = lax.axis_index("x"), lax.axis_index("y"), lax.axis_index("c")
        my_chip = 2 * x + y
        chips = [(1 - x, y), (x, 1 - y), (1 - x, 1 - y)]
        local = [pltpu.make_async_copy(ins[a].at[my_chip], outs[a].at[my_chip], local_sems.at[a]) for a in range(n)]
        sends = [pltpu.make_async_remote_copy(
            src_ref=ins[a].at[2 * px + py], dst_ref=outs[a].at[my_chip], send_sem=send_sems.at[a * 3 + j],
            recv_sem=recv_sems.at[a * 3 + j], device_id=(px, py, c), device_id_type=MESH_ID)
            for j, (px, py) in enumerate(chips) for a in range(n)]
        arrivals = [pltpu.make_async_remote_copy(
            src_ref=ins[a].at[my_chip], dst_ref=outs[a].at[2 * px + py], send_sem=send_sems.at[a * 3 + j],
            recv_sem=recv_sems.at[a * 3 + j], device_id=(px, py, c), device_id_type=MESH_ID)
            for j, (px, py) in enumerate(chips) for a in range(n)]
        for cp in local + sends:
            cp.start()
        for cp in arrivals:
            cp.wait_recv()
        for cp in sends:
            cp.wait_send()
        for cp in local:
            cp.wait()

    out_shape = [jax.ShapeDtypeStruct(a.shape, a.dtype) for a in arrays]
    return _comm_call(body, arrays, out_shape, 3 * n, n, name)


def _reduce_adamw(recv, w, m, v, name):
    S, R, C = recv.shape
    tr = next((t for t in (256, 128, 64, 32, 16, 8) if R % t == 0), R)
    c1 = 1.0 - ADAM_B1 ** ADAM_STEP
    c2 = 1.0 - ADAM_B2 ** ADAM_STEP

    def body(r_ref, w_ref, m_ref, v_ref, g_ref, d_ref, nm_ref, nv_ref):
        g = r_ref[0].astype(F32)
        for s in range(1, S):
            g = g + r_ref[s].astype(F32)
        nm = ADAM_B1 * m_ref[...] + (1.0 - ADAM_B1) * g
        nv = ADAM_B2 * v_ref[...] + (1.0 - ADAM_B2) * (g * g)
        g_ref[...] = g
        nm_ref[...] = nm
        nv_ref[...] = nv
        d_ref[...] = -ADAM_LR * ((nm / c1) / (jnp.sqrt(nv / c2) + ADAM_EPS) + ADAM_WD * w_ref[...])

    spec = pl.BlockSpec((tr, C), lambda i: (i, 0))
    return pl.pallas_call(
        body, name=name, grid=(R // tr,),
        in_specs=[pl.BlockSpec((S, tr, C), lambda i: (0, i, 0)), spec, spec, spec], out_specs=[spec] * 4,
        out_shape=[jax.ShapeDtypeStruct((R, C), F32)] * 4, compiler_params=_params(1),
    )(recv, w, m, v)


def _forward_local(x, target, nw0, nw1, fw, wa_in, conv_w, gate_params, o_norm, wa_out, wb_in, wb_out):
    T, D = x.shape
    nD = D // LANES
    sv = {}
    sv["u0"] = _rmsnorm_fwd(x, nw0, "a_norm_fwd")
    sv["pa"] = _mm_nn(sv["u0"], wa_in, "a_in_proj")
    sv["qkv_a"] = _conv_fwd(sv["pa"], conv_w, D, "a_conv_fwd")
    sv["beta"], sv["gc"] = _gates_fwd(sv["pa"], gate_params, 4 * nD, "a_gates_fwd")
    sv["o_a"], sv["s_all"], sv["t_all"] = _chunk_fwd(sv["qkv_a"], sv["beta"], sv["gc"], D, "a_chunk_fwd")
    sv["y_a"] = _onorm_gate_fwd(sv["o_a"], sv["pa"], 3, o_norm, "a_onorm_fwd")
    sv["h1"] = _mm_nn(sv["y_a"], wa_out, "a_out_proj", add=x)
    sv["u1"] = _rmsnorm_fwd(sv["h1"], nw1, "b_norm_fwd")
    sv["qkv_b"] = _mm_nn(sv["u1"], wb_in[:, :3 * D], "b_in_proj_qkv", out_dtype=BF16)
    sv["gate_b"] = _mm_nn(sv["u1"], wb_in[:, 3 * D:], "b_in_proj_gate")
    sv["o_b"], sv["r_b"] = _sb_fwd(sv["qkv_b"], D, "b_attn_fwd")
    sv["y_b"] = _gate_mul_fwd(sv["o_b"], sv["gate_b"], "b_gate_fwd")
    sv["h2"] = _mm_nn(sv["y_b"], wb_out, "b_out_proj", add=sv["h1"])
    sv["dh2"], sv["loss"], sv["dfw"] = _final_loss(sv["h2"], fw, target, "final_loss")
    return sv


def _backward_local(sv, x, nw0, nw1, wa_in, conv_w, gate_params, o_norm, wa_out, wb_in, wb_out):
    T, D = x.shape
    nD = D // LANES
    g = {}
    dh2 = sv["dh2"]
    g["wb_out"] = _mm_tn(sv["y_b"], dh2, "b_out_proj_dw")
    dy_b = _mm_nt([(dh2, wb_out)], "b_out_proj_dx")
    do_b, dgate_b = _gate_mul_bwd(dy_b, sv["o_b"], sv["gate_b"], "b_gate_bwd")
    dq_b, dk_b, dv_b = _sb_bwd(sv["qkv_b"], do_b, sv["r_b"], D, "b_attn_bwd")
    dp_b = [dq_b, dk_b, dv_b, dgate_b]
    g["wb_in"] = jnp.concatenate([_mm_tn(sv["u1"], dp, "b_in_proj_dw%d" % c) for c, dp in enumerate(dp_b)], axis=1)
    du1 = _mm_nt([(dp, wb_in[:, c * D:(c + 1) * D]) for c, dp in enumerate(dp_b)], "b_in_proj_dx")
    dh1, g["nw1"] = _rmsnorm_bwd(sv["h1"], nw1, du1, dh2, "b_norm_bwd")
    g["wa_out"] = _mm_tn(sv["y_a"], dh1, "a_out_proj_dw")
    dy_a = _mm_nt([(dh1, wa_out)], "a_out_proj_dx")
    do_a, dz_a, g["o_norm"] = _onorm_gate_bwd(dy_a, sv["o_a"], sv["pa"], 3, o_norm, "a_onorm_bwd")
    dqkv_a, dbeta, dg = _chunk_bwd(sv["qkv_a"], sv["beta"], sv["gc"], sv["s_all"], sv["t_all"], do_a, D, "a_chunk_bwd")
    dp_gates, g["gate_params"] = _gates_bwd(sv["pa"], gate_params, 4 * nD, dbeta, dg, "a_gates_bwd")
    dc, g["conv_w"] = _conv_bwd_act(sv["pa"], conv_w, dqkv_a, D, "a_conv_bwd_act")
    dp_qkv = _conv_bwd_input(dc, conv_w, "a_conv_bwd_input")
    dp_a = [(dp_qkv, 0, 3 * D), (dz_a, 3 * D, 4 * D), (dp_gates, 4 * D, 4 * D + 2 * LANES)]
    g["wa_in"] = jnp.concatenate([_mm_tn(sv["u0"], dp, "a_in_proj_dw%d" % c) for c, (dp, _, _) in enumerate(dp_a)], axis=1)
    du0 = _mm_nt([(dp, wa_in[:, lo:hi]) for dp, lo, hi in dp_a], "a_in_proj_dx")
    g["x"], g["nw0"] = _rmsnorm_bwd(x, nw0, du0, dh1, "a_norm_bwd")
    g["fw"] = sv["dfw"]
    return g


def kernel(x, norm_w, a_w_in, a_conv_w, a_a_log, a_dt_bias, a_o_norm, a_w_out, b_w_in, b_w_out, final_norm_w, loss_target, m_norm_w, m_a_w_in, m_a_conv_w, m_a_a_log, m_a_dt_bias, m_a_o_norm, m_a_w_out, m_b_w_in, m_b_w_out, m_final_norm_w, v_norm_w, v_a_w_in, v_a_conv_w, v_a_a_log, v_a_dt_bias, v_a_o_norm, v_a_w_out, v_b_w_in, v_b_w_out, v_final_norm_w):
    D = x.shape[-1]
    H = D // HEAD_DIM
    shards = [a_w_in[0].astype(BF16), a_w_out[0].astype(BF16), b_w_in[0].astype(BF16), b_w_out[0].astype(BF16), a_conv_w[0]]
    ga_in, ga_out, gb_in, gb_out, g_conv = _gather_shards(shards, "weights_gather")
    wa = ga_in.transpose(1, 0, 2).reshape(D, -1)
    pad = lambda w: jnp.pad(w, ((0, 0), (0, LANES - w.shape[1])))
    wa_in = jnp.concatenate([wa[:, :4 * D], pad(wa[:, 4 * D:4 * D + H]), pad(wa[:, 4 * D + H:])], axis=1)
    wa_out = ga_out.reshape(D, D)
    wb_in = gb_in.transpose(1, 0, 2).reshape(D, 4 * D)
    wb_out = gb_out.reshape(D, D)
    conv_w = g_conv.transpose(1, 0, 2).reshape(4, 3 * D)
    gate_params = jnp.zeros((8, LANES), F32).at[0, :H].set(a_a_log[0]).at[1, :H].set(a_dt_bias[0])
    nw0, nw1, fw = norm_w[0:1], norm_w[1:2], final_norm_w[None]

    sv = _forward_local(x[0], loss_target[0], nw0, nw1, fw, wa_in, conv_w, gate_params, a_o_norm, wa_out, wb_in, wb_out)
    g = _backward_local(sv, x[0], nw0, nw1, wa_in, conv_w, gate_params, a_o_norm, wa_out, wb_in, wb_out)

    gwa = g["wa_in"]
    gwa = jnp.concatenate([gwa[:, :4 * D], gwa[:, 4 * D:4 * D + H], gwa[:, 4 * D + LANES:4 * D + LANES + H]], axis=1)
    row = lambda v: jnp.pad(v.reshape(1, -1), ((0, 0), (0, D - v.size)))
    small = jnp.concatenate([g["nw0"][0:1], g["nw1"][0:1], g["fw"][0:1], row(g["gate_params"][0, :H]),
                             row(g["gate_params"][1, :H]), row(g["o_norm"][0]), row(sv["loss"][0, 0:1]),
                             jnp.zeros((1, D), F32)], axis=0)
    cols = lambda a: a.reshape(a.shape[0], 4, 2, -1).transpose(2, 1, 0, 3)
    rows = lambda a: a.reshape(4, 2, -1, a.shape[1]).transpose(1, 0, 2, 3)
    contribs = [cols(gwa), rows(g["wa_out"]), cols(g["wb_in"]), rows(g["wb_out"]), cols(g["conv_w"]),
                jnp.broadcast_to(small[None, None], (2, 4, 8, D))]
    pair = _pair_exchange(contribs, "grads_pair_exchange")
    my_core = lax.axis_index("c")
    own = [lax.dynamic_index_in_dim(a, my_core, axis=0, keepdims=False) for a in contribs]
    partial = [_pair_add(o, p, "grads_pair_add%d" % k, BF16 if k < len(own) - 1 else F32)
               for k, (o, p) in enumerate(zip(own, pair))]
    ra_in, ra_out, rb_in, rb_out, r_conv, r_small = _chip_exchange(partial, "grads_chip_exchange")

    outs = {}
    for nm, recv, w, m, v in (("a_w_in", ra_in, a_w_in, m_a_w_in, v_a_w_in), ("a_w_out", ra_out, a_w_out, m_a_w_out, v_a_w_out),
                              ("b_w_in", rb_in, b_w_in, m_b_w_in, v_b_w_in), ("b_w_out", rb_out, b_w_out, m_b_w_out, v_b_w_out),
                              ("a_conv_w", r_conv, a_conv_w, m_a_conv_w, v_a_conv_w)):
        outs[nm] = tuple(o[None] for o in _reduce_adamw(recv, w[0], m[0], v[0], "adamw_" + nm))

    def pack(nw, alog, dt, onorm, fnw):
        return jnp.concatenate([nw, fnw.reshape(1, D), row(alog), row(dt), row(onorm), jnp.zeros((2, D), F32)], axis=0)

    s_g, s_d, s_m, s_v = _reduce_adamw(
        r_small, pack(norm_w, a_a_log, a_dt_bias, a_o_norm, final_norm_w),
        pack(m_norm_w, m_a_a_log, m_a_dt_bias, m_a_o_norm, m_final_norm_w),
        pack(v_norm_w, v_a_a_log, v_a_dt_bias, v_a_o_norm, v_final_norm_w), "adamw_small")
    loss = s_g[6, 0]
    for i, s in enumerate((s_g, s_d, s_m, s_v)):
        outs.setdefault("norm_w", [None] * 4)[i] = s[0:2]
        outs.setdefault("final_norm_w", [None] * 4)[i] = s[2]
        outs.setdefault("a_a_log", [None] * 4)[i] = s[3:4, :H]
        outs.setdefault("a_dt_bias", [None] * 4)[i] = s[4:5, :H]
        outs.setdefault("a_o_norm", [None] * 4)[i] = s[5:6, :HEAD_DIM]
    names = ("norm_w", "a_w_in", "a_conv_w", "a_a_log", "a_dt_bias", "a_o_norm", "a_w_out", "b_w_in", "b_w_out", "final_norm_w")
    return (loss, g["x"][None]) + tuple(outs[n][i] for i in range(4) for n in names)
```

```python
import functools

import jax
import jax.numpy as jnp
from jax import lax
from jax.experimental import pallas as pl
from jax.experimental.pallas import tpu as pltpu

F32 = jnp.float32
BF16 = jnp.bfloat16
EPS = 1e-6
LOG2_E = 1.4426950408889634
MASKED_SCORE = -1e30
HEAD_DIM = 128
CHUNK = 64
ATTN_Q_BLOCKS_FWD = (512, 256)
ATTN_Q_BLOCKS_BWD = (512, 256)
ATTN_K_BLOCK = 128
LANES = 128
N_DEV = 8
VMEM_LIMIT_BYTES = 48 * 1024 * 1024
ADAM_LR, ADAM_B1, ADAM_B2, ADAM_EPS, ADAM_WD, ADAM_STEP = 0.001, 0.9, 0.999, 1e-08, 0.01, 10
MESH_ID = pl.DeviceIdType.MESH


def _pick(n, candidates):
    for c in candidates:
        if n % c == 0:
            return c
    raise ValueError(f"no tile for {n} in {candidates}")


def _params(n_grid_axes):
    return pltpu.CompilerParams(dimension_semantics=("arbitrary",) * n_grid_axes, vmem_limit_bytes=VMEM_LIMIT_BYTES)


def _dot(a, b):
    return jnp.dot(a.astype(BF16), b.astype(BF16), preferred_element_type=F32)


def _dot_nt(a, b):
    return lax.dot_general(a.astype(BF16), b.astype(BF16), (((1,), (1,)), ((), ())), preferred_element_type=F32)


def _dot_tn(a, b):
    return lax.dot_general(a.astype(BF16), b.astype(BF16), (((0,), (0,)), ((), ())), preferred_element_type=F32)


def _split2(x):
    hi = x.astype(BF16)
    lo = (x - hi.astype(F32)).astype(BF16)
    return hi, lo


def _split3(x):
    hi = x.astype(BF16)
    r = x - hi.astype(F32)
    mid = r.astype(BF16)
    lo = (r - mid.astype(F32)).astype(BF16)
    return hi, mid, lo


def _dot3(a, b):
    a_hi, a_lo = _split2(a)
    b_hi, b_lo = _split2(b)
    d = functools.partial(jnp.dot, preferred_element_type=F32)
    return d(a_hi, b_hi) + (d(a_hi, b_lo) + d(a_lo, b_hi))


def _silu(x):
    return x * jax.nn.sigmoid(x)


def _softplus(x):
    return jnp.maximum(x, 0.0) + jnp.log1p(jnp.exp(-jnp.abs(x)))


def _iota2(shape, axis):
    return lax.broadcasted_iota(jnp.int32, shape, axis)


def _rms_bwd_math(x, w, dy):
    r = lax.rsqrt(jnp.mean(x * x, axis=-1, keepdims=True) + EPS)
    xhat = x * r
    dxhat = dy * w
    dx = r * (dxhat - xhat * jnp.mean(dxhat * xhat, axis=-1, keepdims=True))
    dw = jnp.sum(dy * xhat, axis=0, keepdims=True)
    return dx, dw


def _rmsnorm_fwd(x, w, name):
    T, D = x.shape
    tm = _pick(T, (512, 256, 128))

    def body(x_ref, w_ref, o_ref):
        xf = x_ref[...]
        r = lax.rsqrt(jnp.mean(xf * xf, axis=-1, keepdims=True) + EPS)
        o_ref[...] = (xf * r * w_ref[...]).astype(BF16)

    return pl.pallas_call(
        body, name=name, grid=(T // tm,),
        in_specs=[pl.BlockSpec((tm, D), lambda i: (i, 0)), pl.BlockSpec((1, D), lambda i: (0, 0))],
        out_specs=pl.BlockSpec((tm, D), lambda i: (i, 0)),
        out_shape=jax.ShapeDtypeStruct((T, D), BF16), compiler_params=_params(1),
    )(x, w)


def _rmsnorm_bwd(x, w, du, dres, name):
    T, D = x.shape
    tm = _pick(T, (512, 256, 128))

    def body(x_ref, w_ref, du_ref, dres_ref, dx_ref, dw_ref):
        dx, dw = _rms_bwd_math(x_ref[...], w_ref[...], du_ref[...].astype(F32))
        dx_ref[...] = dres_ref[...] + dx

        @pl.when(pl.program_id(0) == 0)
        def _():
            dw_ref[...] = jnp.zeros_like(dw_ref)

        dw_ref[...] += jnp.broadcast_to(dw, dw_ref.shape)

    return pl.pallas_call(
        body, name=name, grid=(T // tm,),
        in_specs=[pl.BlockSpec((tm, D), lambda i: (i, 0)), pl.BlockSpec((1, D), lambda i: (0, 0)),
                  pl.BlockSpec((tm, D), lambda i: (i, 0)), pl.BlockSpec((tm, D), lambda i: (i, 0))],
        out_specs=[pl.BlockSpec((tm, D), lambda i: (i, 0)), pl.BlockSpec((8, D), lambda i: (0, 0))],
        out_shape=[jax.ShapeDtypeStruct((T, D), F32), jax.ShapeDtypeStruct((8, D), F32)],
        compiler_params=_params(1),
    )(x, w, du, dres)


def _mm_nn(a, b, name, add=None, out_dtype=F32):
    M, K = a.shape
    _, N = b.shape
    tm = _pick(M, (256, 128)) if N > 2048 else _pick(M, (512, 256, 128))

    def body(*refs):
        a_ref, b_ref = refs[0], refs[1]
        o_ref = refs[-1]
        acc = _dot(a_ref[...], b_ref[...])
        if add is not None:
            acc = acc + refs[2][...]
        o_ref[...] = acc.astype(out_dtype)

    in_specs = [pl.BlockSpec((tm, K), lambda i: (i, 0)), pl.BlockSpec((K, N), lambda i: (0, 0))]
    args = [a, b]
    if add is not None:
        in_specs.append(pl.BlockSpec((tm, N), lambda i: (i, 0)))
        args.append(add)
    return pl.pallas_call(
        body, name=name, grid=(M // tm,), in_specs=in_specs,
        out_specs=pl.BlockSpec((tm, N), lambda i: (i, 0)),
        out_shape=jax.ShapeDtypeStruct((M, N), out_dtype), compiler_params=_params(1),
    )(*args)


def _mm_nt(pairs, name):
    M = pairs[0][0].shape[0]
    N = pairs[0][1].shape[0]
    n = len(pairs)
    tm = _pick(M, (512, 256, 128))

    def body(*refs):
        acc = _dot_nt(refs[0][...], refs[n][...])
        for p in range(1, n):
            acc = acc + _dot_nt(refs[p][...], refs[n + p][...])
        refs[-1][...] = acc

    in_specs = ([pl.BlockSpec((tm, a.shape[1]), lambda i: (i, 0)) for a, _ in pairs]
                + [pl.BlockSpec(b.shape, lambda i: (0, 0)) for _, b in pairs])
    return pl.pallas_call(
        body, name=name, grid=(M // tm,), in_specs=in_specs,
        out_specs=pl.BlockSpec((tm, N), lambda i: (i, 0)),
        out_shape=jax.ShapeDtypeStruct((M, N), F32), compiler_params=_params(1),
    )(*[a for a, _ in pairs], *[b for _, b in pairs])


def _mm_tn(a, b, name):
    R, M = a.shape
    _, N = b.shape
    tn = _pick(N, (1536, 1024, 512, 256, 128))
    tr = _pick(R, (512, 256, 128))

    def body(a_ref, b_ref, o_ref):
        @pl.when(pl.program_id(1) == 0)
        def _():
            o_ref[...] = jnp.zeros_like(o_ref)

        o_ref[...] += _dot_tn(a_ref[...], b_ref[...])

    return pl.pallas_call(
        body, name=name, grid=(N // tn, R // tr),
        in_specs=[pl.BlockSpec((tr, M), lambda j, r: (r, 0)), pl.BlockSpec((tr, tn), lambda j, r: (r, j))],
        out_specs=pl.BlockSpec((M, tn), lambda j, r: (0, j)),
        out_shape=jax.ShapeDtypeStruct((M, N), F32), compiler_params=_params(2),
    )(a, b)


def _qkv_post(c, j, n_heads):
    s = _silu(c)
    if j == 2:
        return s
    parts = []
    for h in range(n_heads):
        sh = s[:, h * HEAD_DIM:(h + 1) * HEAD_DIM]
        parts.append(sh * lax.rsqrt(jnp.sum(sh * sh, axis=-1, keepdims=True) + EPS))
    n = jnp.concatenate(parts, axis=-1)
    return n * (HEAD_DIM ** -0.5) if j == 0 else n


def _column_calls(make_call, n_cols=3):
    outs = None
    for j in range(n_cols):
        outs = make_call(j, outs)
    return outs


def _alias_previous(prev, n_inputs):
    if prev is None:
        return [], [], {}
    prev = list(prev) if isinstance(prev, (list, tuple)) else [prev]
    return ([pl.BlockSpec(memory_space=pl.ANY)] * len(prev), prev, {n_inputs + k: k for k in range(len(prev))})


def _conv_taps(cur, halo_prev):
    tm = cur.shape[0]
    ext = jnp.concatenate([halo_prev, cur], axis=0)
    taps = [pltpu.roll(ext, s, 0)[8:8 + tm] for s in (3, 2, 1)]
    return taps + [cur]


def _conv_fwd(p, conv_w, d_model, name):
    T = p.shape[0]
    D = d_model
    H = D // HEAD_DIM
    tm = _pick(T, (256, 128, 64))

    def columns(j, prev_out):
        def body(cur_ref, prev_ref, w_ref, *rest):
            o_ref = rest[-1]
            prev = prev_ref[...] * (pl.program_id(0) > 0).astype(F32)
            taps = _conv_taps(cur_ref[...], prev)
            w = w_ref[...]
            c = sum(taps[k] * w[k:k + 1, :] for k in range(4))
            o_ref[...] = _qkv_post(c, j, H)

        specs, operands, aliases = _alias_previous(prev_out, 3)
        return pl.pallas_call(
            body, name="%s%d" % (name, j), grid=(T // tm,),
            in_specs=[pl.BlockSpec((tm, D), lambda i: (i, j)),
                      pl.BlockSpec((8, D), lambda i: (jnp.maximum(i * (tm // 8) - 1, 0), j)),
                      pl.BlockSpec((4, D), lambda i: (0, j))] + specs,
            out_specs=pl.BlockSpec((tm, D), lambda i: (i, j)),
            out_shape=jax.ShapeDtypeStruct((T, 3 * D), F32), input_output_aliases=aliases, compiler_params=_params(1),
        )(p, p, conv_w, *operands)

    return _column_calls(columns)


def _chunk_tri(tm, upper):
    r, c = _iota2((tm, tm), 0), _iota2((tm, tm), 1)
    same = (r // CHUNK) == (c // CHUNK)
    tri = (c >= r) if upper else (c <= r)
    return jnp.where(same & tri, 1.0, 0.0).astype(BF16)


def _dot_mask(mask_bf16, x):
    hi, mid, lo = _split3(x)
    d = functools.partial(jnp.dot, preferred_element_type=F32)
    return d(mask_bf16, hi) + (d(mask_bf16, mid) + d(mask_bf16, lo))


def _gates_math(pb, pa, a_log, dt_bias):
    beta = jax.nn.sigmoid(pb)
    g = -jnp.exp(a_log) * _softplus(pa + dt_bias)
    return beta, g


def _gates_fwd(p, gate_params, col0, name):
    T = p.shape[0]
    tm = _pick(T, (256, 128, 64))

    def body(pb_ref, pa_ref, gp_ref, beta_ref, gc_ref):
        gp = gp_ref[...]
        beta, g = _gates_math(pb_ref[...], pa_ref[...], gp[0:1, :], gp[1:2, :])
        beta_ref[...] = beta
        gc_ref[...] = _dot_mask(_chunk_tri(tm, upper=False), g)

    return pl.pallas_call(
        body, name=name, grid=(T // tm,),
        in_specs=[pl.BlockSpec((tm, LANES), lambda i: (i, col0)), pl.BlockSpec((tm, LANES), lambda i: (i, col0 + 1)),
                  pl.BlockSpec((8, LANES), lambda i: (0, 0))],
        out_specs=[pl.BlockSpec((tm, LANES), lambda i: (i, 0))] * 2,
        out_shape=[jax.ShapeDtypeStruct((T, LANES), F32)] * 2, compiler_params=_params(1),
    )(p, p, gate_params)


def _col_to_row(col):
    C = col.shape[0]
    eye = _iota2((C, C), 0) == _iota2((C, C), 1)
    return jnp.sum(jnp.where(eye, col, 0.0), axis=0, keepdims=True)


def _lockstep(generators):
    generators = list(generators)
    results = [None] * len(generators)
    live = list(range(len(generators)))
    while live:
        for idx in list(live):
            try:
                next(generators[idx])
            except StopIteration as done:
                results[idx] = done.value
                live.remove(idx)
    return results


def _unit_lower_inverse(low):
    C = low.shape[0]
    eye = (_iota2((C, C), 0) == _iota2((C, C), 1)).astype(F32)
    t = eye - low
    p = _dot3(low, low)
    yield
    n = 2
    while True:
        tp = _dot3(t, p)
        n *= 2
        if n < C:
            p = _dot3(p, p)
        yield
        t = t + tp
        if n >= C:
            return t


def _chunk_head_fwd(q, k, v, gc, beta, s_in):
    C = q.shape[0]
    r, c = _iota2((C, C), 0), _iota2((C, C), 1)
    causal, strict = r >= c, r > c
    decay = jnp.where(causal, jnp.exp(jnp.where(causal, gc - _col_to_row(gc), 0.0)), 0.0)
    kb, vb = k * beta, v * beta
    eg = jnp.exp(gc)
    kk = _dot_nt(kb, k)
    qk = _dot_nt(q, k)
    o_state = _dot(q * eg, s_in)
    yield
    t_inv = yield from _unit_lower_inverse(jnp.where(strict, kk * decay, 0.0))
    u = _dot(t_inv, vb)
    w = _dot(t_inv, kb * eg)
    yield
    w_state = _dot(w, s_in)
    yield
    v_new = u - w_state
    g_last = gc[C - 1:C, :]
    o_intra = _dot(qk * decay, v_new)
    s_add = _dot_tn(k * jnp.exp(g_last - gc), v_new)
    yield
    return o_state + o_intra, s_in * jnp.exp(g_last) + s_add, t_inv


def _chunk_fwd(qkv, beta, gc, d_model, name):
    T = qkv.shape[0]
    D = d_model
    H = D // HEAD_DIM
    N = T // CHUNK

    def body(q_ref, k_ref, v_ref, beta_ref, gc_ref, o_ref, s_all_ref, t_all_ref, s_ref):
        @pl.when(pl.program_id(0) == 0)
        def _():
            s_ref[...] = jnp.zeros_like(s_ref)

        heads = [slice(h * HEAD_DIM, (h + 1) * HEAD_DIM) for h in range(H)]
        s_all_ref[0] = s_ref[...]
        results = _lockstep(_chunk_head_fwd(q_ref[:, hs], k_ref[:, hs], v_ref[:, hs], gc_ref[:, h:h + 1],
                                            beta_ref[:, h:h + 1], s_ref[h]) for h, hs in enumerate(heads))
        for h, (o, s_out, t_inv) in enumerate(results):
            o_ref[:, heads[h]] = o
            s_ref[h] = s_out
            t_all_ref[0, h] = t_inv

    return pl.pallas_call(
        body, name=name, grid=(N,),
        in_specs=[pl.BlockSpec((CHUNK, D), lambda n: (n, 0)), pl.BlockSpec((CHUNK, D), lambda n: (n, 1)),
                  pl.BlockSpec((CHUNK, D), lambda n: (n, 2)),
                  pl.BlockSpec((CHUNK, LANES), lambda n: (n, 0)), pl.BlockSpec((CHUNK, LANES), lambda n: (n, 0))],
        out_specs=[pl.BlockSpec((CHUNK, D), lambda n: (n, 0)),
                   pl.BlockSpec((1, H, HEAD_DIM, HEAD_DIM), lambda n: (n, 0, 0, 0)),
                   pl.BlockSpec((1, H, CHUNK, CHUNK), lambda n: (n, 0, 0, 0))],
        out_shape=[jax.ShapeDtypeStruct((T, D), F32), jax.ShapeDtypeStruct((N, H, HEAD_DIM, HEAD_DIM), F32),
                   jax.ShapeDtypeStruct((N, H, CHUNK, CHUNK), F32)],
        scratch_shapes=[pltpu.VMEM((H, HEAD_DIM, HEAD_DIM), F32)], compiler_params=_params(1),
    )(qkv, qkv, qkv, beta, gc)


def _onorm_gate_math(o, z, w, n_heads):
    parts = []
    for h in range(n_heads):
        hs = slice(h * HEAD_DIM, (h + 1) * HEAD_DIM)
        oh = o[:, hs]
        y = oh * lax.rsqrt(jnp.mean(oh * oh, axis=-1, keepdims=True) + EPS) * w
        parts.append(y * _silu(z[:, hs]))
    return jnp.concatenate(parts, axis=-1)


def _onorm_gate_fwd(o, p, z_col, o_norm, name):
    T, D = o.shape
    H = D // HEAD_DIM
    tm = _pick(T, (256, 128, 64))

    def body(o_ref, z_ref, w_ref, y_ref):
        y_ref[...] = _onorm_gate_math(o_ref[...], z_ref[...], w_ref[...], H).astype(BF16)

    return pl.pallas_call(
        body, name=name, grid=(T // tm,),
        in_specs=[pl.BlockSpec((tm, D), lambda i: (i, 0)), pl.BlockSpec((tm, D), lambda i: (i, z_col)),
                  pl.BlockSpec((1, HEAD_DIM), lambda i: (0, 0))],
        out_specs=pl.BlockSpec((tm, D), lambda i: (i, 0)),
        out_shape=jax.ShapeDtypeStruct((T, D), BF16), compiler_params=_params(1),
    )(o, p, o_norm)


def _diag_mask(qb, kb, d):
    return _iota2((qb, kb), 0) > _iota2((qb, kb), 1) + d * kb


def _run_trips(trip, n, state):
    def four(j, st):
        for u in range(4):
            st = trip(4 * j + u, st)
        return st

    state = lax.fori_loop(0, n // 4, four, state)
    return lax.fori_loop(0, (n % 4) // 2, lambda j, st: trip(n - 1, trip(n - 2, st)), state)


def _fill_score_masks(mask_buf, qb, kb, ns):
    mask_buf[0] = jnp.zeros(mask_buf.shape[1:], F32)
    for d in range(ns):
        for half in range(2):
            mask_buf[d + 1, :, half * kb:(half + 1) * kb] = jnp.where(_diag_mask(qb, kb, 2 * d + half), 0.0, MASKED_SCORE)


def _softplus_bits(w):
    u = 1.0 + jnp.exp2(jnp.minimum(w, 64.0))
    return jnp.maximum(w, jnp.log2(u)), 1.0 / u


def _incl_lower(n):
    return jnp.where((_iota2((2 * n, n), 0) & (n - 1)) >= _iota2((2 * n, n), 1), 1.0, 0.0).astype(BF16)


def _incl_upper(n):
    return jnp.where((_iota2((2 * n, n), 0) & (n - 1)) <= _iota2((2 * n, n), 1), 1.0, 0.0).astype(BF16)


def _dot_cum(x, tri_bf16):
    hi, lo = _split2(x)
    return jnp.dot(jnp.concatenate([hi, lo], axis=1), tri_bf16, preferred_element_type=F32)


def _sb_fwd(qkv, d_model, name):
    T = qkv.shape[0]
    D = d_model
    H = D // HEAD_DIM
    QB = _pick(T, ATTN_Q_BLOCKS_FWD)
    KB = ATTN_K_BLOCK
    KS = 2 * KB
    ns = QB // KS
    nq = T // QB
    scale = HEAD_DIM ** -0.5

    def body(q_ref, k_ref, v_ref, o_ref, r_ref, w_buf, cum_buf, mask_buf):
        i = pl.program_id(1)

        @pl.when(i == 0)
        def _():
            _fill_score_masks(mask_buf, QB, KB, ns)

        q = q_ref[...]
        tri = _incl_lower(KB)
        n_tot = (i + 1) * ns

        def key_step(m):
            return jnp.maximum(n_tot - 1 - m, 0)

        def rows(ref, s):
            return ref[pl.ds(pl.multiple_of(s * KS, KS), KS), :]

        def scores(s):
            return _dot_nt(q, rows(k_ref, s)) * (scale * LOG2_E) + mask_buf[jnp.maximum(s - i * ns + 1, 0)]

        def cums(w):
            sp = _softplus_bits(w)[0]
            return jnp.concatenate([_dot_cum(sp[:, :KB], tri), _dot_cum(sp[:, KB:], tri)], axis=1)

        def weights(w, cum, carry):
            a_r = jnp.exp2(w[:, KB:] - cum[:, KB:] - carry)
            carry = carry + cum[:, KB:KB + 1]
            a_l = jnp.exp2(w[:, :KB] - cum[:, :KB] - carry)
            return jnp.concatenate([a_l, a_r], axis=1).astype(BF16), carry + cum[:, 0:1]

        def trip(m, carry):
            w_new = scores(key_step(m + 2))
            a, carry = weights(w_buf[m % 3], cum_buf[m % 2], carry)
            o_ref[...] += _dot(a, rows(v_ref, key_step(m)))
            cum_buf[(m + 1) % 2] = cums(w_buf[(m + 1) % 3])
            w_buf[(m + 2) % 3] = w_new
            return carry

        o_ref[...] = jnp.zeros_like(o_ref)
        w_buf[0] = scores(key_step(0))
        w_buf[1] = scores(key_step(1))
        cum_buf[0] = cums(w_buf[0])
        carry = _run_trips(trip, n_tot, jnp.zeros((QB, 1), F32))
        r_ref[0] = jnp.broadcast_to(carry, (QB, LANES))

    return pl.pallas_call(
        body, name=name, grid=(H, nq),
        in_specs=[pl.BlockSpec((QB, HEAD_DIM), lambda h, i: (i, h)),
                  pl.BlockSpec((T, HEAD_DIM), lambda h, i: (0, H + h)),
                  pl.BlockSpec((T, HEAD_DIM), lambda h, i: (0, 2 * H + h))],
        out_specs=[pl.BlockSpec((QB, HEAD_DIM), lambda h, i: (i, h)),
                   pl.BlockSpec((1, QB, LANES), lambda h, i: (h, i, 0))],
        out_shape=[jax.ShapeDtypeStruct((T, D), F32), jax.ShapeDtypeStruct((H, T, LANES), F32)],
        scratch_shapes=[pltpu.VMEM((3, QB, KS), F32), pltpu.VMEM((2, QB, KS), F32), pltpu.VMEM((ns + 1, QB, KS), F32)],
        compiler_params=_params(2),
    )(qkv, qkv, qkv)


def _gate_mul_fwd(o, gate, name):
    T, D = o.shape
    tm = _pick(T, (512, 256, 128))

    def body(o_ref, g_ref, y_ref):
        y_ref[...] = (o_ref[...] * _silu(g_ref[...])).astype(BF16)

    spec = pl.BlockSpec((tm, D), lambda i: (i, 0))
    return pl.pallas_call(body, name=name, grid=(T // tm,), in_specs=[spec, spec], out_specs=spec,
                          out_shape=jax.ShapeDtypeStruct((T, D), BF16), compiler_params=_params(1))(o, gate)


def _final_loss(h, w, target, name):
    T, D = h.shape
    tm = _pick(T, (512, 256, 128))

    def body(h_ref, w_ref, t_ref, dh_ref, loss_ref, dw_ref):
        x, w = h_ref[...], w_ref[...]
        r = lax.rsqrt(jnp.mean(x * x, axis=-1, keepdims=True) + EPS)
        err = x * r * w - t_ref[...]
        part = 0.5 * jnp.sum(jnp.mean(err * err, axis=-1, keepdims=True), axis=0, keepdims=True)
        dx, dw = _rms_bwd_math(x, w, err * (1.0 / D))
        dh_ref[...] = dx

        @pl.when(pl.program_id(0) == 0)
        def _():
            loss_ref[...] = jnp.zeros_like(loss_ref)
            dw_ref[...] = jnp.zeros_like(dw_ref)

        loss_ref[...] += jnp.broadcast_to(part, loss_ref.shape)
        dw_ref[...] += jnp.broadcast_to(dw, dw_ref.shape)

    return pl.pallas_call(
        body, name=name, grid=(T // tm,),
        in_specs=[pl.BlockSpec((tm, D), lambda i: (i, 0)), pl.BlockSpec((1, D), lambda i: (0, 0)),
                  pl.BlockSpec((tm, D), lambda i: (i, 0))],
        out_specs=[pl.BlockSpec((tm, D), lambda i: (i, 0)), pl.BlockSpec((8, LANES), lambda i: (0, 0)),
                   pl.BlockSpec((8, D), lambda i: (0, 0))],
        out_shape=[jax.ShapeDtypeStruct((T, D), F32), jax.ShapeDtypeStruct((8, LANES), F32),
                   jax.ShapeDtypeStruct((8, D), F32)],
        compiler_params=_params(1),
    )(h, w, target)


def _gate_mul_bwd(dy, o, gate, name):
    T, D = o.shape
    tm = _pick(T, (512, 256, 128))

    def body(dy_ref, o_ref, g_ref, do_ref, dg_ref):
        dy, g = dy_ref[...], g_ref[...]
        s = jax.nn.sigmoid(g)
        do_ref[...] = dy * (g * s)
        dg_ref[...] = (dy * o_ref[...] * (s + g * s * (1.0 - s))).astype(BF16)

    spec = pl.BlockSpec((tm, D), lambda i: (i, 0))
    return pl.pallas_call(body, name=name, grid=(T // tm,), in_specs=[spec] * 3, out_specs=[spec] * 2,
                          out_shape=[jax.ShapeDtypeStruct((T, D), F32), jax.ShapeDtypeStruct((T, D), BF16)],
                          compiler_params=_params(1))(dy, o, gate)


def _sb_bwd(qkv, do, r_tot, d_model, name):
    T = qkv.shape[0]
    D = d_model
    H = D // HEAD_DIM
    QB = _pick(T, ATTN_Q_BLOCKS_BWD)
    KB = ATTN_K_BLOCK
    KS = 2 * KB
    ns = QB // KS
    nq = T // QB
    n_key_steps = T // KS
    scale = HEAD_DIM ** -0.5

    def body(q_ref, k_ref, v_ref, do_ref, r_ref, dq_ref, dk_ref, dv_ref,
             dkt_acc, dvt_acc, dq_acc, w_buf, da_buf, cum_buf, sig_buf, mask_buf):
        i = pl.program_id(1)

        @pl.when(i == 0)
        def _():
            dkt_acc[...] = jnp.zeros_like(dkt_acc)
            dvt_acc[...] = jnp.zeros_like(dvt_acc)
            _fill_score_masks(mask_buf, QB, KB, ns)

        q = q_ref[...]
        do_blk = do_ref[...].astype(BF16)
        q_t = q.astype(F32).T.astype(BF16)
        do_t = do_ref[...].T.astype(BF16)
        row_total = r_ref[0][:, 0:1]
        tri_rev = _incl_lower(KB)
        tri_fwd = jnp.where(_iota2((KB, KB), 0) <= _iota2((KB, KB), 1), 1.0, 0.0).astype(BF16)
        n_tot = (i + 1) * ns

        def step_rows(ref, s):
            return ref[pl.ds(pl.multiple_of(s * KS, KS), KS), :]

        def scores(s):
            w = _dot_nt(q, step_rows(k_ref, s)) * (scale * LOG2_E) + mask_buf[jnp.maximum(s - i * ns + 1, 0)]
            return w, _dot_nt(do_blk, step_rows(v_ref, s))

        def softplus_sums(w):
            sp, one_minus_sig = _softplus_bits(w)
            cum = jnp.concatenate([_dot_cum(sp[:, :KB], tri_rev), _dot_cum(sp[:, KB:], tri_rev)], axis=1)
            return cum, 1.0 - one_minus_sig

        def weights(w, cum, da, left_sp):
            right_l = row_total - left_sp - cum[:, 0:1]
            right_r = right_l - cum[:, KB:KB + 1]
            a = jnp.concatenate([jnp.exp2(w[:, :KB] - cum[:, :KB] - right_l),
                                 jnp.exp2(w[:, KB:] - cum[:, KB:] - right_r)], axis=1)
            p = da * a
            cp = jnp.concatenate([_dot(p[:, :KB], tri_fwd), _dot(p[:, KB:], tri_fwd)], axis=1)
            return a.astype(BF16), p, cp, row_total - right_r

        def score_grads(p, cp, sig, left_p):
            cum_l = cp[:, :KB] + left_p
            cum_r = cp[:, KB:] + cum_l[:, KB - 1:KB]
            dz = p - sig * jnp.concatenate([cum_l, cum_r], axis=1)
            return dz.astype(BF16), cum_r[:, KB - 1:KB]

        def trip(m, st):
            left_sp, left_p = st
            s2 = jnp.minimum(m + 2, n_tot - 1)
            s1 = jnp.minimum(m + 1, n_tot - 1)
            w_new, da_new = scores(s2)
            a, p, cp, left_sp = weights(w_buf[m % 3], cum_buf[m % 2], da_buf[m % 3], left_sp)
            cum_new, sig_new = softplus_sums(w_buf[s1 % 3])
            dz, left_p = score_grads(p, cp, sig_buf[m % 2], left_p)
            dq_acc[...] += _dot(dz, step_rows(k_ref, m))
            dkt_acc[m] += jnp.dot(q_t, dz, preferred_element_type=F32) * scale
            dvt_acc[m] += jnp.dot(do_t, a, preferred_element_type=F32)
            cum_buf[(m + 1) % 2] = cum_new
            sig_buf[(m + 1) % 2] = sig_new
            w_buf[(m + 2) % 3] = w_new
            da_buf[(m + 2) % 3] = da_new
            return left_sp, left_p

        dq_acc[...] = jnp.zeros_like(dq_acc)
        w_buf[0], da_buf[0] = scores(0)
        w_buf[1], da_buf[1] = scores(jnp.minimum(1, n_tot - 1))
        cum_buf[0], sig_buf[0] = softplus_sums(w_buf[0])
        zero_col = jnp.zeros((QB, 1), F32)
        _run_trips(trip, n_tot, (zero_col, zero_col))
        dq_ref[...] = (dq_acc[...] * scale).astype(BF16)

        @pl.when(i == nq - 1)
        def _():
            for s in range(n_key_steps):
                dk_ref[s * KS:(s + 1) * KS, :] = dkt_acc[s].T.astype(BF16)
                dv_ref[s * KS:(s + 1) * KS, :] = dvt_acc[s].T.astype(BF16)

    return pl.pallas_call(
        body, name=name, grid=(H, nq),
        in_specs=[pl.BlockSpec((QB, HEAD_DIM), lambda h, i: (i, h)),
                  pl.BlockSpec((T, HEAD_DIM), lambda h, i: (0, H + h)),
                  pl.BlockSpec((T, HEAD_DIM), lambda h, i: (0, 2 * H + h)),
                  pl.BlockSpec((QB, HEAD_DIM), lambda h, i: (i, h)),
                  pl.BlockSpec((1, QB, LANES), lambda h, i: (h, i, 0))],
        out_specs=[pl.BlockSpec((QB, HEAD_DIM), lambda h, i: (i, h)),
                   pl.BlockSpec((T, HEAD_DIM), lambda h, i: (0, h)),
                   pl.BlockSpec((T, HEAD_DIM), lambda h, i: (0, h))],
        out_shape=[jax.ShapeDtypeStruct((T, D), BF16)] * 3,
        scratch_shapes=[pltpu.VMEM((n_key_steps, HEAD_DIM, KS), F32), pltpu.VMEM((n_key_steps, HEAD_DIM, KS), F32),
                        pltpu.VMEM((QB, HEAD_DIM), F32), pltpu.VMEM((3, QB, KS), F32), pltpu.VMEM((3, QB, KS), F32),
                        pltpu.VMEM((2, QB, KS), F32), pltpu.VMEM((2, QB, KS), F32), pltpu.VMEM((ns + 1, QB, KS), F32)],
        compiler_params=_params(2),
    )(qkv, qkv, qkv, do, r_tot)


def _onorm_gate_bwd(dy, o, p, z_col, o_norm, name):
    T, D = o.shape
    H = D // HEAD_DIM
    tm = _pick(T, (256, 128, 64))

    def body(dy_ref, o_ref, z_ref, w_ref, do_ref, dz_ref, dw_ref):
        _, vjp = jax.vjp(functools.partial(_onorm_gate_math, n_heads=H), o_ref[...], z_ref[...], w_ref[...])
        do, dz, dw = vjp(dy_ref[...])
        do_ref[...] = do
        dz_ref[...] = dz.astype(BF16)

        @pl.when(pl.program_id(0) == 0)
        def _():
            dw_ref[...] = jnp.zeros_like(dw_ref)

        dw_ref[...] += jnp.broadcast_to(dw, dw_ref.shape)

    return pl.pallas_call(
        body, name=name, grid=(T // tm,),
        in_specs=[pl.BlockSpec((tm, D), lambda i: (i, 0)), pl.BlockSpec((tm, D), lambda i: (i, 0)),
                  pl.BlockSpec((tm, D), lambda i: (i, z_col)), pl.BlockSpec((1, HEAD_DIM), lambda i: (0, 0))],
        out_specs=[pl.BlockSpec((tm, D), lambda i: (i, 0)), pl.BlockSpec((tm, D), lambda i: (i, 0)),
                   pl.BlockSpec((8, HEAD_DIM), lambda i: (0, 0))],
        out_shape=[jax.ShapeDtypeStruct((T, D), F32), jax.ShapeDtypeStruct((T, D), BF16),
                   jax.ShapeDtypeStruct((8, HEAD_DIM), F32)],
        compiler_params=_params(1),
    )(dy, o, p, o_norm)


def _row_to_col(row):
    C = row.shape[1]
    eye = _iota2((C, C), 0) == _iota2((C, C), 1)
    return jnp.sum(jnp.where(eye, row, 0.0), axis=1, keepdims=True)


def _lane_sum(x):
    return jnp.sum(x, axis=-1, keepdims=True)


def _chunk_head_bwd(q, k, v, gc, beta, s_in, t_inv, do, ds_out):
    C = q.shape[0]
    r, c = _iota2((C, C), 0), _iota2((C, C), 1)
    causal, strict = r >= c, r > c
    decay = jnp.where(causal, jnp.exp(jnp.where(causal, gc - _col_to_row(gc), 0.0)), 0.0)
    kb, vb = k * beta, v * beta
    eg = jnp.exp(gc)
    kbg = kb * eg
    g_last = gc[C - 1:C, :]
    e_tail = jnp.exp(g_last - gc)
    k_tail = k * e_tail
    gl = jnp.exp(g_last)
    qg = q * eg
    t_inv_t = t_inv.T
    kk = _dot_nt(kb, k)
    u = _dot(t_inv, vb)
    w = _dot(t_inv, kbg)
    qk = _dot_nt(q, k)
    d_qg = _dot_nt(do, s_in)
    ds_state = _dot_tn(qg, do)
    yield
    low = jnp.where(strict, kk * decay, 0.0)
    attn = qk * decay
    w_state = _dot(w, s_in)
    d_vnew_intra = _dot_tn(attn, do)
    d_vnew_state = _dot(k_tail, ds_out)
    yield
    v_new = u - w_state
    d_vnew = d_vnew_intra + d_vnew_state
    d_ktail = _dot_nt(v_new, ds_out)
    d_attn_raw = _dot_nt(do, v_new)
    d_w = -_dot_nt(d_vnew, s_in)
    ds_w = _dot_tn(w, d_vnew)
    d_vb = _dot(t_inv_t, d_vnew)
    d_tinv_u = _dot_nt(d_vnew, vb)
    yield
    d_gl = jnp.sum(_lane_sum(s_in * ds_out), axis=0, keepdims=True)
    d_attn = jnp.where(causal, d_attn_raw, 0.0)
    ds_in = ds_out * gl + ds_state - ds_w
    d_kbg = _dot(t_inv_t, d_w)
    d_tinv_w = _dot_nt(d_w, kbg)
    d_qk = d_attn * decay
    dq_intra = _dot(d_qk, k)
    dk_intra = _dot_tn(d_qk, q)
    yield
    inner = _dot(t_inv_t, d_tinv_u + d_tinv_w)
    yield
    d_low_raw = _dot_nt(inner, t_inv)
    yield
    d_low = jnp.where(strict, -d_low_raw, 0.0)
    d_kk = d_low * decay
    d_kb_low = _dot(d_kk, k)
    dk_low = _dot_tn(d_kk, kb)
    yield
    d_kb = d_kb_low + d_kbg * eg
    dq = dq_intra + d_qg * eg
    dk = dk_low + dk_intra + d_ktail * e_tail + d_kb * beta
    dv = d_vb * beta
    dbeta = _lane_sum(d_kb * k) + _lane_sum(d_vb * v)
    m = d_low * low + d_attn * attn
    tail_term = _lane_sum(d_ktail * k_tail)
    d_g_last = d_gl * gl + jnp.sum(tail_term, axis=0, keepdims=True)
    dgc = (_lane_sum(m) - _row_to_col(jnp.sum(m, axis=0, keepdims=True))
           + _lane_sum(d_qg * qg) + _lane_sum(d_kbg * kbg) - tail_term)
    dgc = dgc + jnp.where(_iota2((C, 1), 0) == C - 1, d_g_last, 0.0)
    return dq, dk, dv, dgc, dbeta, ds_in


def _chunk_bwd(qkv, beta, gc, s_all, t_all, do, d_model, name):
    T = qkv.shape[0]
    D = d_model
    H = D // HEAD_DIM
    N = T // CHUNK

    def body(q_ref, k_ref, v_ref, beta_ref, gc_ref, s_ref, t_ref, do_ref, dqkv_ref, dbeta_ref, dg_ref, ds_ref):
        @pl.when(pl.program_id(0) == 0)
        def _():
            ds_ref[...] = jnp.zeros_like(ds_ref)

        lane = _iota2((CHUNK, LANES), 1)
        dgc_all = jnp.zeros((CHUNK, LANES), F32)
        dbeta_all = jnp.zeros((CHUNK, LANES), F32)
        results = _lockstep(
            _chunk_head_bwd(q_ref[:, hs], k_ref[:, hs], v_ref[:, hs], gc_ref[:, h:h + 1], beta_ref[:, h:h + 1],
                            s_ref[0, h], t_ref[0, h], do_ref[:, hs], ds_ref[h])
            for h, hs in enumerate(slice(h * HEAD_DIM, (h + 1) * HEAD_DIM) for h in range(H)))
        for h, (dq, dk, dv, dgc, dbeta, ds_in) in enumerate(results):
            ds_ref[h] = ds_in
            dqkv_ref[:, h * HEAD_DIM:(h + 1) * HEAD_DIM] = dq
            dqkv_ref[:, D + h * HEAD_DIM:D + (h + 1) * HEAD_DIM] = dk
            dqkv_ref[:, 2 * D + h * HEAD_DIM:2 * D + (h + 1) * HEAD_DIM] = dv
            dgc_all = jnp.where(lane == h, dgc, dgc_all)
            dbeta_all = jnp.where(lane == h, dbeta, dbeta_all)
        dbeta_ref[...] = dbeta_all
        dg_ref[...] = _dot_mask(_chunk_tri(CHUNK, upper=True), dgc_all)

    rev = lambda n: N - 1 - n
    return pl.pallas_call(
        body, name=name, grid=(N,),
        in_specs=[pl.BlockSpec((CHUNK, D), lambda n: (rev(n), 0)), pl.BlockSpec((CHUNK, D), lambda n: (rev(n), 1)),
                  pl.BlockSpec((CHUNK, D), lambda n: (rev(n), 2)),
                  pl.BlockSpec((CHUNK, LANES), lambda n: (rev(n), 0)), pl.BlockSpec((CHUNK, LANES), lambda n: (rev(n), 0)),
                  pl.BlockSpec((1, H, HEAD_DIM, HEAD_DIM), lambda n: (rev(n), 0, 0, 0)),
                  pl.BlockSpec((1, H, CHUNK, CHUNK), lambda n: (rev(n), 0, 0, 0)),
                  pl.BlockSpec((CHUNK, D), lambda n: (rev(n), 0))],
        out_specs=[pl.BlockSpec((CHUNK, 3 * D), lambda n: (rev(n), 0)),
                   pl.BlockSpec((CHUNK, LANES), lambda n: (rev(n), 0)), pl.BlockSpec((CHUNK, LANES), lambda n: (rev(n), 0))],
        out_shape=[jax.ShapeDtypeStruct((T, 3 * D), F32), jax.ShapeDtypeStruct((T, LANES), F32),
                   jax.ShapeDtypeStruct((T, LANES), F32)],
        scratch_shapes=[pltpu.VMEM((H, HEAD_DIM, HEAD_DIM), F32)], compiler_params=_params(1),
    )(qkv, qkv, qkv, beta, gc, s_all, t_all, do)


def _gates_bwd(p, gate_params, col0, dbeta, dg, name):
    T = p.shape[0]
    tm = _pick(T, (256, 128, 64))

    def body(pb_ref, pa_ref, gp_ref, dbeta_ref, dg_ref, dp_ref, dgp_ref):
        gp = gp_ref[...]
        _, vjp = jax.vjp(_gates_math, pb_ref[...], pa_ref[...], gp[0:1, :], gp[1:2, :])
        dpb, dpa, d_alog, d_dt = vjp((dbeta_ref[...], dg_ref[...]))
        dp_ref[:, 0:LANES] = dpb.astype(BF16)
        dp_ref[:, LANES:2 * LANES] = dpa.astype(BF16)

        @pl.when(pl.program_id(0) == 0)
        def _():
            dgp_ref[...] = jnp.zeros_like(dgp_ref)

        dgp_ref[0:1, :] += d_alog
        dgp_ref[1:2, :] += d_dt

    return pl.pallas_call(
        body, name=name, grid=(T // tm,),
        in_specs=[pl.BlockSpec((tm, LANES), lambda i: (i, col0)), pl.BlockSpec((tm, LANES), lambda i: (i, col0 + 1)),
                  pl.BlockSpec((8, LANES), lambda i: (0, 0)),
                  pl.BlockSpec((tm, LANES), lambda i: (i, 0)), pl.BlockSpec((tm, LANES), lambda i: (i, 0))],
        out_specs=[pl.BlockSpec((tm, 2 * LANES), lambda i: (i, 0)), pl.BlockSpec((8, LANES), lambda i: (0, 0))],
        out_shape=[jax.ShapeDtypeStruct((T, 2 * LANES), BF16), jax.ShapeDtypeStruct((8, LANES), F32)],
        compiler_params=_params(1),
    )(p, p, gate_params, dbeta, dg)


def _conv_bwd_act(p, conv_w, dqkv, d_model, name):
    T = p.shape[0]
    D = d_model
    H = D // HEAD_DIM
    tm = _pick(T, (256, 128, 64))

    def columns(j, prev_outs):
        def body(cur_ref, prev_ref, w_ref, dout_ref, *rest):
            dc_ref, dw_ref = rest[-2:]
            i = pl.program_id(0)
            prev = prev_ref[...] * (i > 0).astype(F32)
            taps = _conv_taps(cur_ref[...], prev)
            w = w_ref[...]
            c = sum(taps[k] * w[k:k + 1, :] for k in range(4))
            _, vjp = jax.vjp(lambda cc: _qkv_post(cc, j, H), c)
            (dc,) = vjp(dout_ref[...])
            dc_ref[...] = dc

            @pl.when(i == 0)
            def _():
                dw_ref[...] = jnp.zeros_like(dw_ref)

            for k in range(4):
                dw_ref[k:k + 1, :] += jnp.sum(dc * taps[k], axis=0, keepdims=True)

        specs, operands, aliases = _alias_previous(prev_outs, 4)
        return pl.pallas_call(
            body, name="%s%d" % (name, j), grid=(T // tm,),
            in_specs=[pl.BlockSpec((tm, D), lambda i: (i, j)),
                      pl.BlockSpec((8, D), lambda i: (jnp.maximum(i * (tm // 8) - 1, 0), j)),
                      pl.BlockSpec((4, D), lambda i: (0, j)),
                      pl.BlockSpec((tm, D), lambda i: (i, j))] + specs,
            out_specs=[pl.BlockSpec((tm, D), lambda i: (i, j)), pl.BlockSpec((4, D), lambda i: (0, j))],
            out_shape=[jax.ShapeDtypeStruct((T, 3 * D), F32), jax.ShapeDtypeStruct((4, 3 * D), F32)],
            input_output_aliases=aliases, compiler_params=_params(1),
        )(p, p, conv_w, dqkv, *operands)

    return _column_calls(columns)


def _conv_bwd_input(dc, conv_w, name):
    T, D3 = dc.shape
    D = D3 // 3
    tm = _pick(T, (256, 128, 64))
    n_t = T // tm

    def body(cur_ref, next_ref, w_ref, dp_ref):
        i = pl.program_id(0)
        cur = cur_ref[...]
        nxt = next_ref[...] * (i < n_t - 1).astype(F32)
        ext = jnp.concatenate([cur, nxt], axis=0)
        w = w_ref[...]
        acc = cur * w[3:4, :]
        for s in (1, 2, 3):
            acc = acc + pltpu.roll(ext, tm + 8 - s, 0)[0:tm] * w[3 - s:4 - s, :]
        dp_ref[...] = acc.astype(BF16)

    return pl.pallas_call(
        body, name=name, grid=(n_t, 3),
        in_specs=[pl.BlockSpec((tm, D), lambda i, j: (i, j)),
                  pl.BlockSpec((8, D), lambda i, j: (jnp.minimum((i + 1) * (tm // 8), T // 8 - 1), j)),
                  pl.BlockSpec((4, D), lambda i, j: (0, j))],
        out_specs=pl.BlockSpec((tm, D), lambda i, j: (i, j)),
        out_shape=jax.ShapeDtypeStruct((T, D3), BF16), compiler_params=_params(2),
    )(dc, dc, conv_w)


def _comm_call(body, arrays, out_shape, n_remote, n_local, name):
    any_spec = pl.BlockSpec(memory_space=pl.ANY)
    return pl.pallas_call(
        body, name=name, in_specs=[any_spec] * len(arrays), out_specs=[any_spec] * len(out_shape), out_shape=out_shape,
        scratch_shapes=[pltpu.SemaphoreType.DMA((n_remote,)), pltpu.SemaphoreType.DMA((n_remote,)),
                        pltpu.SemaphoreType.DMA((n_local,))],
        compiler_params=pltpu.CompilerParams(has_side_effects=True),
    )(*arrays)


def _gather_shards(arrays, name):
    n = len(arrays)

    def body(*refs):
        ins, outs = refs[:n], refs[n:2 * n]
        send_sems, recv_sems, local_sems = refs[2 * n:]
        x, y, c = lax.axis_index("x"), lax.axis_index("y"), lax.axis_index("c")
        me, sibling = (x, y, c), (x, y, 1 - c)
        chips = [(1 - x, y), (x, 1 - y), (1 - x, 1 - y)]

        def copy(a, k, block, to, src=None):
            dst = outs[a].at[4 * block[0] + 2 * block[1] + block[2]]
            return pltpu.make_async_remote_copy(
                src_ref=dst if src is None else src, dst_ref=dst, send_sem=send_sems.at[a * 7 + k],
                recv_sem=recv_sems.at[a * 7 + k], device_id=to, device_id_type=MESH_ID)

        local = [pltpu.make_async_copy(ins[a], outs[a].at[4 * x + 2 * y + c], local_sems.at[a]) for a in range(n)]
        first = [copy(a, 0, me, sibling, src=ins[a]) for a in range(n)]
        first += [copy(a, 1 + j, me, (*chip, c), src=ins[a]) for j, chip in enumerate(chips) for a in range(n)]
        for cp in local + first:
            cp.start()
        passed = []
        for j, chip in enumerate(chips):
            for a in range(n):
                copy(a, 1 + j, (*chip, c), me).wait_recv()
                passed.append(copy(a, 4 + j, (*chip, c), sibling))
                passed[-1].start()
        for a in range(n):
            copy(a, 0, sibling, me).wait_recv()
            for j, chip in enumerate(chips):
                copy(a, 4 + j, (*chip, 1 - c), me).wait_recv()
        for cp in first + passed:
            cp.wait_send()
        for cp in local:
            cp.wait()

    out_shape = [jax.ShapeDtypeStruct((N_DEV,) + a.shape, a.dtype) for a in arrays]
    return _comm_call(body, arrays, out_shape, 7 * n, n, name)


def _pair_exchange(arrays, name):
    n = len(arrays)

    def body(*refs):
        ins, pair = refs[:n], refs[n:2 * n]
        send_sems, recv_sems, _ = refs[2 * n:]
        x, y, c = lax.axis_index("x"), lax.axis_index("y"), lax.axis_index("c")
        sends = [pltpu.make_async_remote_copy(
            src_ref=ins[a].at[1 - c], dst_ref=pair[a], send_sem=send_sems.at[a], recv_sem=recv_sems.at[a],
            device_id=(x, y, 1 - c), device_id_type=MESH_ID) for a in range(n)]
        for cp in sends:
            cp.start()
        for cp in sends:
            cp.wait_recv()
        for cp in sends:
            cp.wait_send()

    out_shape = [jax.ShapeDtypeStruct(a.shape[1:], a.dtype) for a in arrays]
    return _comm_call(body, arrays, out_shape, n, 1, name)


def _pair_add(own, pair, name, out_dtype):
    _, R, C = own.shape
    tr = next((t for t in (256, 128, 64, 32, 16) if R % t == 0), R)

    def body(a_ref, b_ref, o_ref):
        o_ref[...] = (a_ref[...] + b_ref[...]).astype(out_dtype)

    spec = pl.BlockSpec((1, tr, C), lambda q, i: (q, i, 0))
    return pl.pallas_call(body, name=name, grid=(4, R // tr), in_specs=[spec, spec], out_specs=spec,
                          out_shape=jax.ShapeDtypeStruct(own.shape, out_dtype), compiler_params=_params(2))(own, pair)


def _chip_exchange(arrays, name):
    n = len(arrays)

    def body(*refs):
        ins, outs = refs[:n], refs[n:2 * n]
        send_sems, recv_sems, local_sems = refs[2 * n:]
        x, y, c = lax.axis_index("x"), lax.axis_index("y"), lax.axis_index("c")
        my_chip = 2 * x + y
        chips = [(1 - x, y), (x, 1 - y), (1 - x, 1 - y)]
        local = [pltpu.make_async_copy(ins[a].at[my_chip], outs[a].at[my_chip], local_sems.at[a]) for a in range(n)]
        sends = [pltpu.make_async_remote_copy(
            src_ref=ins[a].at[2 * px + py], dst_ref=outs[a].at[my_chip], send_sem=send_sems.at[a * 3 + j],
            recv_sem=recv_sems.at[a * 3 + j], device_id=(px, py, c), device_id_type=MESH_ID)
            for j, (px, py) in enumerate(chips) for a in range(n)]
        arrivals = [pltpu.make_async_remote_copy(
            src_ref=ins[a].at[my_chip], dst_ref=outs[a].at[2 * px + py], send_sem=send_sems.at[a * 3 + j],
            recv_sem=recv_sems.at[a * 3 + j], device_id=(px, py, c), device_id_type=MESH_ID)
            for j, (px, py) in enumerate(chips) for a in range(n)]
        for cp in local + sends:
            cp.start()
        for cp in arrivals:
            cp.wait_recv()
        for cp in sends:
            cp.wait_send()
        for cp in local:
            cp.wait()

    out_shape = [jax.ShapeDtypeStruct(a.shape, a.dtype) for a in arrays]
    return _comm_call(body, arrays, out_shape, 3 * n, n, name)


def _reduce_adamw(recv, w, m, v, name):
    S, R, C = recv.shape
    tr = next((t for t in (256, 128, 64, 32, 16, 8) if R % t == 0), R)
    c1 = 1.0 - ADAM_B1 ** ADAM_STEP
    c2 = 1.0 - ADAM_B2 ** ADAM_STEP

    def body(r_ref, w_ref, m_ref, v_ref, g_ref, d_ref, nm_ref, nv_ref):
        g = r_ref[0].astype(F32)
        for s in range(1, S):
            g = g + r_ref[s].astype(F32)
        nm = ADAM_B1 * m_ref[...] + (1.0 - ADAM_B1) * g
        nv = ADAM_B2 * v_ref[...] + (1.0 - ADAM_B2) * (g * g)
        g_ref[...] = g
        nm_ref[...] = nm
        nv_ref[...] = nv
        d_ref[...] = -ADAM_LR * ((nm / c1) / (jnp.sqrt(nv / c2) + ADAM_EPS) + ADAM_WD * w_ref[...])

    spec = pl.BlockSpec((tr, C), lambda i: (i, 0))
    return pl.pallas_call(
        body, name=name, grid=(R // tr,),
        in_specs=[pl.BlockSpec((S, tr, C), lambda i: (0, i, 0)), spec, spec, spec], out_specs=[spec] * 4,
        out_shape=[jax.ShapeDtypeStruct((R, C), F32)] * 4, compiler_params=_params(1),
    )(recv, w, m, v)


def _forward_local(x, target, nw0, nw1, fw, wa_in, conv_w, gate_params, o_norm, wa_out, wb_in, wb_out):
    T, D = x.shape
    nD = D // LANES
    sv = {}
    sv["u0"] = _rmsnorm_fwd(x, nw0, "a_norm_fwd")
    sv["pa"] = _mm_nn(sv["u0"], wa_in, "a_in_proj")
    sv["qkv_a"] = _conv_fwd(sv["pa"], conv_w, D, "a_conv_fwd")
    sv["beta"], sv["gc"] = _gates_fwd(sv["pa"], gate_params, 4 * nD, "a_gates_fwd")
    sv["o_a"], sv["s_all"], sv["t_all"] = _chunk_fwd(sv["qkv_a"], sv["beta"], sv["gc"], D, "a_chunk_fwd")
    sv["y_a"] = _onorm_gate_fwd(sv["o_a"], sv["pa"], 3, o_norm, "a_onorm_fwd")
    sv["h1"] = _mm_nn(sv["y_a"], wa_out, "a_out_proj", add=x)
    sv["u1"] = _rmsnorm_fwd(sv["h1"], nw1, "b_norm_fwd")
    sv["qkv_b"] = _mm_nn(sv["u1"], wb_in[:, :3 * D], "b_in_proj_qkv", out_dtype=BF16)
    sv["gate_b"] = _mm_nn(sv["u1"], wb_in[:, 3 * D:], "b_in_proj_gate")
    sv["o_b"], sv["r_b"] = _sb_fwd(sv["qkv_b"], D, "b_attn_fwd")
    sv["y_b"] = _gate_mul_fwd(sv["o_b"], sv["gate_b"], "b_gate_fwd")
    sv["h2"] = _mm_nn(sv["y_b"], wb_out, "b_out_proj", add=sv["h1"])
    sv["dh2"], sv["loss"], sv["dfw"] = _final_loss(sv["h2"], fw, target, "final_loss")
    return sv


def _backward_local(sv, x, nw0, nw1, wa_in, conv_w, gate_params, o_norm, wa_out, wb_in, wb_out):
    T, D = x.shape
    nD = D // LANES
    g = {}
    dh2 = sv["dh2"]
    g["wb_out"] = _mm_tn(sv["y_b"], dh2, "b_out_proj_dw")
    dy_b = _mm_nt([(dh2, wb_out)], "b_out_proj_dx")
    do_b, dgate_b = _gate_mul_bwd(dy_b, sv["o_b"], sv["gate_b"], "b_gate_bwd")
    dq_b, dk_b, dv_b = _sb_bwd(sv["qkv_b"], do_b, sv["r_b"], D, "b_attn_bwd")
    dp_b = [dq_b, dk_b, dv_b, dgate_b]
    g["wb_in"] = jnp.concatenate([_mm_tn(sv["u1"], dp, "b_in_proj_dw%d" % c) for c, dp in enumerate(dp_b)], axis=1)
    du1 = _mm_nt([(dp, wb_in[:, c * D:(c + 1) * D]) for c, dp in enumerate(dp_b)], "b_in_proj_dx")
    dh1, g["nw1"] = _rmsnorm_bwd(sv["h1"], nw1, du1, dh2, "b_norm_bwd")
    g["wa_out"] = _mm_tn(sv["y_a"], dh1, "a_out_proj_dw")
    dy_a = _mm_nt([(dh1, wa_out)], "a_out_proj_dx")
    do_a, dz_a, g["o_norm"] = _onorm_gate_bwd(dy_a, sv["o_a"], sv["pa"], 3, o_norm, "a_onorm_bwd")
    dqkv_a, dbeta, dg = _chunk_bwd(sv["qkv_a"], sv["beta"], sv["gc"], sv["s_all"], sv["t_all"], do_a, D, "a_chunk_bwd")
    dp_gates, g["gate_params"] = _gates_bwd(sv["pa"], gate_params, 4 * nD, dbeta, dg, "a_gates_bwd")
    dc, g["conv_w"] = _conv_bwd_act(sv["pa"], conv_w, dqkv_a, D, "a_conv_bwd_act")
    dp_qkv = _conv_bwd_input(dc, conv_w, "a_conv_bwd_input")
    dp_a = [(dp_qkv, 0, 3 * D), (dz_a, 3 * D, 4 * D), (dp_gates, 4 * D, 4 * D + 2 * LANES)]
    g["wa_in"] = jnp.concatenate([_mm_tn(sv["u0"], dp, "a_in_proj_dw%d" % c) for c, (dp, _, _) in enumerate(dp_a)], axis=1)
    du0 = _mm_nt([(dp, wa_in[:, lo:hi]) for dp, lo, hi in dp_a], "a_in_proj_dx")
    g["x"], g["nw0"] = _rmsnorm_bwd(x, nw0, du0, dh1, "a_norm_bwd")
    g["fw"] = sv["dfw"]
    return g


def kernel(x, norm_w, a_w_in, a_conv_w, a_a_log, a_dt_bias, a_o_norm, a_w_out, b_w_in, b_w_out, final_norm_w, loss_target, m_norm_w, m_a_w_in, m_a_conv_w, m_a_a_log, m_a_dt_bias, m_a_o_norm, m_a_w_out, m_b_w_in, m_b_w_out, m_final_norm_w, v_norm_w, v_a_w_in, v_a_conv_w, v_a_a_log, v_a_dt_bias, v_a_o_norm, v_a_w_out, v_b_w_in, v_b_w_out, v_final_norm_w):
    D = x.shape[-1]
    H = D // HEAD_DIM
    shards = [a_w_in[0].astype(BF16), a_w_out[0].astype(BF16), b_w_in[0].astype(BF16), b_w_out[0].astype(BF16), a_conv_w[0]]
    ga_in, ga_out, gb_in, gb_out, g_conv = _gather_shards(shards, "weights_gather")
    wa = ga_in.transpose(1, 0, 2).reshape(D, -1)
    pad = lambda w: jnp.pad(w, ((0, 0), (0, LANES - w.shape[1])))
    wa_in = jnp.concatenate([wa[:, :4 * D], pad(wa[:, 4 * D:4 * D + H]), pad(wa[:, 4 * D + H:])], axis=1)
    wa_out = ga_out.reshape(D, D)
    wb_in = gb_in.transpose(1, 0, 2).reshape(D, 4 * D)
    wb_out = gb_out.reshape(D, D)
    conv_w = g_conv.transpose(1, 0, 2).reshape(4, 3 * D)
    gate_params = jnp.zeros((8, LANES), F32).at[0, :H].set(a_a_log[0]).at[1, :H].set(a_dt_bias[0])
    nw0, nw1, fw = norm_w[0:1], norm_w[1:2], final_norm_w[None]

    sv = _forward_local(x[0], loss_target[0], nw0, nw1, fw, wa_in, conv_w, gate_params, a_o_norm, wa_out, wb_in, wb_out)
    g = _backward_local(sv, x[0], nw0, nw1, wa_in, conv_w, gate_params, a_o_norm, wa_out, wb_in, wb_out)

    gwa = g["wa_in"]
    gwa = jnp.concatenate([gwa[:, :4 * D], gwa[:, 4 * D:4 * D + H], gwa[:, 4 * D + LANES:4 * D + LANES + H]], axis=1)
    row = lambda v: jnp.pad(v.reshape(1, -1), ((0, 0), (0, D - v.size)))
    small = jnp.concatenate([g["nw0"][0:1], g["nw1"][0:1], g["fw"][0:1], row(g["gate_params"][0, :H]),
                             row(g["gate_params"][1, :H]), row(g["o_norm"][0]), row(sv["loss"][0, 0:1]),
                             jnp.zeros((1, D), F32)], axis=0)
    cols = lambda a: a.reshape(a.shape[0], 4, 2, -1).transpose(2, 1, 0, 3)
    rows = lambda a: a.reshape(4, 2, -1, a.shape[1]).transpose(1, 0, 2, 3)
    contribs = [cols(gwa), rows(g["wa_out"]), cols(g["wb_in"]), rows(g["wb_out"]), cols(g["conv_w"]),
                jnp.broadcast_to(small[None, None], (2, 4, 8, D))]
    pair = _pair_exchange(contribs, "grads_pair_exchange")
    my_core = lax.axis_index("c")
    own = [lax.dynamic_index_in_dim(a, my_core, axis=0, keepdims=False) for a in contribs]
    partial = [_pair_add(o, p, "grads_pair_add%d" % k, BF16 if k < len(own) - 1 else F32)
               for k, (o, p) in enumerate(zip(own, pair))]
    ra_in, ra_out, rb_in, rb_out, r_conv, r_small = _chip_exchange(partial, "grads_chip_exchange")

    outs = {}
    for nm, recv, w, m, v in (("a_w_in", ra_in, a_w_in, m_a_w_in, v_a_w_in), ("a_w_out", ra_out, a_w_out, m_a_w_out, v_a_w_out),
                              ("b_w_in", rb_in, b_w_in, m_b_w_in, v_b_w_in), ("b_w_out", rb_out, b_w_out, m_b_w_out, v_b_w_out),
                              ("a_conv_w", r_conv, a_conv_w, m_a_conv_w, v_a_conv_w)):
        outs[nm] = tuple(o[None] for o in _reduce_adamw(recv, w[0], m[0], v[0], "adamw_" + nm))

    def pack(nw, alog, dt, onorm, fnw):
        return jnp.concatenate([nw, fnw.reshape(1, D), row(alog), row(dt), row(onorm), jnp.zeros((2, D), F32)], axis=0)

    s_g, s_d, s_m, s_v = _reduce_adamw(
        r_small, pack(norm_w, a_a_log, a_dt_bias, a_o_norm, final_norm_w),
        pack(m_norm_w, m_a_a_log, m_a_dt_bias, m_a_o_norm, m_final_norm_w),
        pack(v_norm_w, v_a_a_log, v_a_dt_bias, v_a_o_norm, v_final_norm_w), "adamw_small")
    loss = s_g[6, 0]
    for i, s in enumerate((s_g, s_d, s_m, s_v)):
        outs.setdefault("norm_w", [None] * 4)[i] = s[0:2]
        outs.setdefault("final_norm_w", [None] * 4)[i] = s[2]
        outs.setdefault("a_a_log", [None] * 4)[i] = s[3:4, :H]
        outs.setdefault("a_dt_bias", [None] * 4)[i] = s[4:5, :H]
        outs.setdefault("a_o_norm", [None] * 4)[i] = s[5:6, :HEAD_DIM]
    names = ("norm_w", "a_w_in", "a_conv_w", "a_a_log", "a_dt_bias", "a_o_norm", "a_w_out", "b_w_in", "b_w_out", "final_norm_w")
    return (loss, g["x"][None]) + tuple(outs[n][i] for i in range(4) for n in names)
```

```python
import functools

import jax
import jax.numpy as jnp
from jax import lax
from jax.experimental import pallas as pl
from jax.experimental.pallas import tpu as pltpu

F32 = jnp.float32
BF16 = jnp.bfloat16
EPS = 1e-6
LOG2_E = 1.4426950408889634
MASKED_SCORE = -1e30
HEAD_DIM = 128
CHUNK = 64
CHUNKS_PER_STEP = 2
ATTN_Q_BLOCKS_FWD = (512, 256)
ATTN_Q_BLOCKS_BWD = (512, 256)
ATTN_K_BLOCK = 128
LANES = 128
N_DEV = 8
VMEM_LIMIT_BYTES = 48 * 1024 * 1024
ADAM_LR, ADAM_B1, ADAM_B2, ADAM_EPS, ADAM_WD, ADAM_STEP = 0.001, 0.9, 0.999, 1e-08, 0.01, 10
MESH_ID = pl.DeviceIdType.MESH


def _pick(n, candidates):
    for c in candidates:
        if n % c == 0:
            return c
    raise ValueError(f"no tile for {n} in {candidates}")


def _params(n_grid_axes):
    return pltpu.CompilerParams(dimension_semantics=("arbitrary",) * n_grid_axes, vmem_limit_bytes=VMEM_LIMIT_BYTES)


def _dot(a, b):
    return jnp.dot(a.astype(BF16), b.astype(BF16), preferred_element_type=F32)


def _dot_nt(a, b):
    return lax.dot_general(a.astype(BF16), b.astype(BF16), (((1,), (1,)), ((), ())), preferred_element_type=F32)


def _dot_tn(a, b):
    return lax.dot_general(a.astype(BF16), b.astype(BF16), (((0,), (0,)), ((), ())), preferred_element_type=F32)


def _split2(x):
    hi = x.astype(BF16)
    lo = (x - hi.astype(F32)).astype(BF16)
    return hi, lo


def _split3(x):
    hi = x.astype(BF16)
    r = x - hi.astype(F32)
    mid = r.astype(BF16)
    lo = (r - mid.astype(F32)).astype(BF16)
    return hi, mid, lo


def _dot3(a, b):
    a_hi, a_lo = _split2(a)
    b_hi, b_lo = _split2(b)
    d = functools.partial(jnp.dot, preferred_element_type=F32)
    return d(a_hi, b_hi) + (d(a_hi, b_lo) + d(a_lo, b_hi))


def _silu(x):
    return x * jax.nn.sigmoid(x)


def _softplus(x):
    return jnp.maximum(x, 0.0) + jnp.log1p(jnp.exp(-jnp.abs(x)))


def _iota2(shape, axis):
    return lax.broadcasted_iota(jnp.int32, shape, axis)


def _rms_bwd_math(x, w, dy):
    r = lax.rsqrt(jnp.mean(x * x, axis=-1, keepdims=True) + EPS)
    xhat = x * r
    dxhat = dy * w
    dx = r * (dxhat - xhat * jnp.mean(dxhat * xhat, axis=-1, keepdims=True))
    dw = jnp.sum(dy * xhat, axis=0, keepdims=True)
    return dx, dw


def _rmsnorm_fwd(x, w, name):
    T, D = x.shape
    tm = _pick(T, (512, 256, 128))

    def body(x_ref, w_ref, o_ref):
        xf = x_ref[...]
        r = lax.rsqrt(jnp.mean(xf * xf, axis=-1, keepdims=True) + EPS)
        o_ref[...] = (xf * r * w_ref[...]).astype(BF16)

    return pl.pallas_call(
        body, name=name, grid=(T // tm,),
        in_specs=[pl.BlockSpec((tm, D), lambda i: (i, 0)), pl.BlockSpec((1, D), lambda i: (0, 0))],
        out_specs=pl.BlockSpec((tm, D), lambda i: (i, 0)),
        out_shape=jax.ShapeDtypeStruct((T, D), BF16), compiler_params=_params(1),
    )(x, w)


def _rmsnorm_bwd(x, w, du, dres, name):
    T, D = x.shape
    tm = _pick(T, (512, 256, 128))

    def body(x_ref, w_ref, du_ref, dres_ref, dx_ref, dw_ref):
        dx, dw = _rms_bwd_math(x_ref[...], w_ref[...], du_ref[...].astype(F32))
        dx_ref[...] = dres_ref[...] + dx

        @pl.when(pl.program_id(0) == 0)
        def _():
            dw_ref[...] = jnp.zeros_like(dw_ref)

        dw_ref[...] += jnp.broadcast_to(dw, dw_ref.shape)

    return pl.pallas_call(
        body, name=name, grid=(T // tm,),
        in_specs=[pl.BlockSpec((tm, D), lambda i: (i, 0)), pl.BlockSpec((1, D), lambda i: (0, 0)),
                  pl.BlockSpec((tm, D), lambda i: (i, 0)), pl.BlockSpec((tm, D), lambda i: (i, 0))],
        out_specs=[pl.BlockSpec((tm, D), lambda i: (i, 0)), pl.BlockSpec((8, D), lambda i: (0, 0))],
        out_shape=[jax.ShapeDtypeStruct((T, D), F32), jax.ShapeDtypeStruct((8, D), F32)],
        compiler_params=_params(1),
    )(x, w, du, dres)


def _mm_nn(a, b, name, add=None, out_dtype=F32):
    M, K = a.shape
    _, N = b.shape
    tm = _pick(M, (256, 128)) if N > 2048 else _pick(M, (512, 256, 128))

    def body(*refs):
        a_ref, b_ref = refs[0], refs[1]
        o_ref = refs[-1]
        acc = _dot(a_ref[...], b_ref[...])
        if add is not None:
            acc = acc + refs[2][...]
        o_ref[...] = acc.astype(out_dtype)

    in_specs = [pl.BlockSpec((tm, K), lambda i: (i, 0)), pl.BlockSpec((K, N), lambda i: (0, 0))]
    args = [a, b]
    if add is not None:
        in_specs.append(pl.BlockSpec((tm, N), lambda i: (i, 0)))
        args.append(add)
    return pl.pallas_call(
        body, name=name, grid=(M // tm,), in_specs=in_specs,
        out_specs=pl.BlockSpec((tm, N), lambda i: (i, 0)),
        out_shape=jax.ShapeDtypeStruct((M, N), out_dtype), compiler_params=_params(1),
    )(*args)


def _mm_nt(pairs, name):
    M = pairs[0][0].shape[0]
    N = pairs[0][1].shape[0]
    n = len(pairs)
    tm = _pick(M, (512, 256, 128))

    def body(*refs):
        acc = _dot_nt(refs[0][...], refs[n][...])
        for p in range(1, n):
            acc = acc + _dot_nt(refs[p][...], refs[n + p][...])
        refs[-1][...] = acc

    in_specs = ([pl.BlockSpec((tm, a.shape[1]), lambda i: (i, 0)) for a, _ in pairs]
                + [pl.BlockSpec(b.shape, lambda i: (0, 0)) for _, b in pairs])
    return pl.pallas_call(
        body, name=name, grid=(M // tm,), in_specs=in_specs,
        out_specs=pl.BlockSpec((tm, N), lambda i: (i, 0)),
        out_shape=jax.ShapeDtypeStruct((M, N), F32), compiler_params=_params(1),
    )(*[a for a, _ in pairs], *[b for _, b in pairs])


def _mm_tn(a, b, name):
    R, M = a.shape
    _, N = b.shape
    tn = _pick(N, (1536, 1024, 512, 256, 128))
    tr = _pick(R, (512, 256, 128))

    def body(a_ref, b_ref, o_ref):
        @pl.when(pl.program_id(1) == 0)
        def _():
            o_ref[...] = jnp.zeros_like(o_ref)

        o_ref[...] += _dot_tn(a_ref[...], b_ref[...])

    return pl.pallas_call(
        body, name=name, grid=(N // tn, R // tr),
        in_specs=[pl.BlockSpec((tr, M), lambda j, r: (r, 0)), pl.BlockSpec((tr, tn), lambda j, r: (r, j))],
        out_specs=pl.BlockSpec((M, tn), lambda j, r: (0, j)),
        out_shape=jax.ShapeDtypeStruct((M, N), F32), compiler_params=_params(2),
    )(a, b)


def _qkv_post(c, j, n_heads):
    s = _silu(c)
    if j == 2:
        return s
    parts = []
    for h in range(n_heads):
        sh = s[:, h * HEAD_DIM:(h + 1) * HEAD_DIM]
        parts.append(sh * lax.rsqrt(jnp.sum(sh * sh, axis=-1, keepdims=True) + EPS))
    n = jnp.concatenate(parts, axis=-1)
    return n * (HEAD_DIM ** -0.5) if j == 0 else n


def _column_calls(make_call, n_cols=3):
    outs = None
    for j in range(n_cols):
        outs = make_call(j, outs)
    return outs


def _alias_previous(prev, n_inputs):
    if prev is None:
        return [], [], {}
    prev = list(prev) if isinstance(prev, (list, tuple)) else [prev]
    return ([pl.BlockSpec(memory_space=pl.ANY)] * len(prev), prev, {n_inputs + k: k for k in range(len(prev))})


def _conv_taps(cur, halo_prev):
    tm = cur.shape[0]
    ext = jnp.concatenate([halo_prev, cur], axis=0)
    taps = [pltpu.roll(ext, s, 0)[8:8 + tm] for s in (3, 2, 1)]
    return taps + [cur]


def _conv_fwd(p, conv_w, d_model, name):
    T = p.shape[0]
    D = d_model
    H = D // HEAD_DIM
    tm = _pick(T, (256, 128, 64))

    def columns(j, prev_out):
        def body(cur_ref, prev_ref, w_ref, *rest):
            o_ref = rest[-1]
            prev = prev_ref[...] * (pl.program_id(0) > 0).astype(F32)
            taps = _conv_taps(cur_ref[...], prev)
            w = w_ref[...]
            c = sum(taps[k] * w[k:k + 1, :] for k in range(4))
            o_ref[...] = _qkv_post(c, j, H)

        specs, operands, aliases = _alias_previous(prev_out, 3)
        return pl.pallas_call(
            body, name="%s%d" % (name, j), grid=(T // tm,),
            in_specs=[pl.BlockSpec((tm, D), lambda i: (i, j)),
                      pl.BlockSpec((8, D), lambda i: (jnp.maximum(i * (tm // 8) - 1, 0), j)),
                      pl.BlockSpec((4, D), lambda i: (0, j))] + specs,
            out_specs=pl.BlockSpec((tm, D), lambda i: (i, j)),
            out_shape=jax.ShapeDtypeStruct((T, 3 * D), F32), input_output_aliases=aliases, compiler_params=_params(1),
        )(p, p, conv_w, *operands)

    return _column_calls(columns)


def _chunk_tri(tm, upper):
    r, c = _iota2((tm, tm), 0), _iota2((tm, tm), 1)
    same = (r // CHUNK) == (c // CHUNK)
    tri = (c >= r) if upper else (c <= r)
    return jnp.where(same & tri, 1.0, 0.0).astype(BF16)


def _dot_mask(mask_bf16, x):
    hi, mid, lo = _split3(x)
    d = functools.partial(jnp.dot, preferred_element_type=F32)
    return d(mask_bf16, hi) + (d(mask_bf16, mid) + d(mask_bf16, lo))


def _gates_math(pb, pa, a_log, dt_bias):
    beta = jax.nn.sigmoid(pb)
    g = -jnp.exp(a_log) * _softplus(pa + dt_bias)
    return beta, g


def _gates_fwd(p, gate_params, col0, name):
    T = p.shape[0]
    tm = _pick(T, (256, 128, 64))

    def body(pb_ref, pa_ref, gp_ref, beta_ref, gc_ref):
        gp = gp_ref[...]
        beta, g = _gates_math(pb_ref[...], pa_ref[...], gp[0:1, :], gp[1:2, :])
        beta_ref[...] = beta
        gc_ref[...] = _dot_mask(_chunk_tri(tm, upper=False), g)

    return pl.pallas_call(
        body, name=name, grid=(T // tm,),
        in_specs=[pl.BlockSpec((tm, LANES), lambda i: (i, col0)), pl.BlockSpec((tm, LANES), lambda i: (i, col0 + 1)),
                  pl.BlockSpec((8, LANES), lambda i: (0, 0))],
        out_specs=[pl.BlockSpec((tm, LANES), lambda i: (i, 0))] * 2,
        out_shape=[jax.ShapeDtypeStruct((T, LANES), F32)] * 2, compiler_params=_params(1),
    )(p, p, gate_params)


def _col_to_row(col):
    C = col.shape[0]
    eye = _iota2((C, C), 0) == _iota2((C, C), 1)
    return jnp.sum(jnp.where(eye, col, 0.0), axis=0, keepdims=True)


def _lockstep(generators):
    generators = list(generators)
    results = [None] * len(generators)
    live = list(range(len(generators)))
    while live:
        for idx in list(live):
            try:
                next(generators[idx])
            except StopIteration as done:
                results[idx] = done.value
                live.remove(idx)
    return results


def _unit_lower_inverse(low):
    C = low.shape[0]
    eye = (_iota2((C, C), 0) == _iota2((C, C), 1)).astype(F32)
    t = eye - low
    p = _dot3(low, low)
    yield
    n = 2
    while True:
        tp = _dot3(t, p)
        n *= 2
        if n < C:
            p = _dot3(p, p)
        yield
        t = t + tp
        if n >= C:
            return t


def _chunk_head_fwd(q, k, v, gc, beta, s_in):
    C = q.shape[0]
    r, c = _iota2((C, C), 0), _iota2((C, C), 1)
    causal, strict = r >= c, r > c
    decay = jnp.where(causal, jnp.exp(jnp.where(causal, gc - _col_to_row(gc), 0.0)), 0.0)
    kb, vb = k * beta, v * beta
    eg = jnp.exp(gc)
    kk = _dot_nt(kb, k)
    qk = _dot_nt(q, k)
    o_state = _dot(q * eg, s_in)
    yield
    t_inv = yield from _unit_lower_inverse(jnp.where(strict, kk * decay, 0.0))
    u = _dot(t_inv, vb)
    w = _dot(t_inv, kb * eg)
    yield
    w_state = _dot(w, s_in)
    yield
    v_new = u - w_state
    g_last = gc[C - 1:C, :]
    o_intra = _dot(qk * decay, v_new)
    s_add = _dot_tn(k * jnp.exp(g_last - gc), v_new)
    yield
    return o_state + o_intra, s_in * jnp.exp(g_last) + s_add, t_inv


def _chunk_fwd(qkv, beta, gc, d_model, name):
    T = qkv.shape[0]
    D = d_model
    H = D // HEAD_DIM
    N = T // CHUNK
    G = CHUNKS_PER_STEP
    R = G * CHUNK

    def body(q_ref, k_ref, v_ref, beta_ref, gc_ref, o_ref, s_all_ref, t_all_ref, s_ref):
        @pl.when(pl.program_id(0) == 0)
        def _():
            s_ref[...] = jnp.zeros_like(s_ref)

        heads = [slice(h * HEAD_DIM, (h + 1) * HEAD_DIM) for h in range(H)]
        states = [s_ref[h] for h in range(H)]
        for sub in range(G):
            rows = slice(sub * CHUNK, (sub + 1) * CHUNK)
            for h in range(H):
                s_all_ref[sub, h] = states[h]
            results = _lockstep(_chunk_head_fwd(q_ref[rows, hs], k_ref[rows, hs], v_ref[rows, hs], gc_ref[rows, h:h + 1],
                                                beta_ref[rows, h:h + 1], states[h]) for h, hs in enumerate(heads))
            for h, (o, s_out, t_inv) in enumerate(results):
                o_ref[rows, heads[h]] = o
                states[h] = s_out
                t_all_ref[sub, h] = t_inv
        for h in range(H):
            s_ref[h] = states[h]

    return pl.pallas_call(
        body, name=name, grid=(N // G,),
        in_specs=[pl.BlockSpec((R, D), lambda n: (n, 0)), pl.BlockSpec((R, D), lambda n: (n, 1)),
                  pl.BlockSpec((R, D), lambda n: (n, 2)),
                  pl.BlockSpec((R, LANES), lambda n: (n, 0)), pl.BlockSpec((R, LANES), lambda n: (n, 0))],
        out_specs=[pl.BlockSpec((R, D), lambda n: (n, 0)),
                   pl.BlockSpec((G, H, HEAD_DIM, HEAD_DIM), lambda n: (n, 0, 0, 0)),
                   pl.BlockSpec((G, H, CHUNK, CHUNK), lambda n: (n, 0, 0, 0))],
        out_shape=[jax.ShapeDtypeStruct((T, D), F32), jax.ShapeDtypeStruct((N, H, HEAD_DIM, HEAD_DIM), F32),
                   jax.ShapeDtypeStruct((N, H, CHUNK, CHUNK), F32)],
        scratch_shapes=[pltpu.VMEM((H, HEAD_DIM, HEAD_DIM), F32)], compiler_params=_params(1),
    )(qkv, qkv, qkv, beta, gc)


def _onorm_gate_math(o, z, w, n_heads):
    parts = []
    for h in range(n_heads):
        hs = slice(h * HEAD_DIM, (h + 1) * HEAD_DIM)
        oh = o[:, hs]
        y = oh * lax.rsqrt(jnp.mean(oh * oh, axis=-1, keepdims=True) + EPS) * w
        parts.append(y * _silu(z[:, hs]))
    return jnp.concatenate(parts, axis=-1)


def _onorm_gate_fwd(o, p, z_col, o_norm, name):
    T, D = o.shape
    H = D // HEAD_DIM
    tm = _pick(T, (256, 128, 64))

    def body(o_ref, z_ref, w_ref, y_ref):
        y_ref[...] = _onorm_gate_math(o_ref[...], z_ref[...], w_ref[...], H).astype(BF16)

    return pl.pallas_call(
        body, name=name, grid=(T // tm,),
        in_specs=[pl.BlockSpec((tm, D), lambda i: (i, 0)), pl.BlockSpec((tm, D), lambda i: (i, z_col)),
                  pl.BlockSpec((1, HEAD_DIM), lambda i: (0, 0))],
        out_specs=pl.BlockSpec((tm, D), lambda i: (i, 0)),
        out_shape=jax.ShapeDtypeStruct((T, D), BF16), compiler_params=_params(1),
    )(o, p, o_norm)


def _diag_mask(qb, kb, d):
    return _iota2((qb, kb), 0) > _iota2((qb, kb), 1) + d * kb


def _run_trips(trip, n, state):
    def four(j, st):
        for u in range(4):
            st = trip(4 * j + u, st)
        return st

    state = lax.fori_loop(0, n // 4, four, state)
    return lax.fori_loop(0, (n % 4) // 2, lambda j, st: trip(n - 1, trip(n - 2, st)), state)


def _fill_score_masks(mask_buf, qb, kb, ns):
    mask_buf[0] = jnp.zeros(mask_buf.shape[1:], F32)
    for d in range(ns):
        for half in range(2):
            mask_buf[d + 1, :, half * kb:(half + 1) * kb] = jnp.where(_diag_mask(qb, kb, 2 * d + half), 0.0, MASKED_SCORE)


def _softplus_bits(w):
    u = 1.0 + jnp.exp2(jnp.minimum(w, 64.0))
    return jnp.maximum(w, jnp.log2(u)), 1.0 / u


def _incl_lower(n):
    return jnp.where((_iota2((2 * n, n), 0) & (n - 1)) >= _iota2((2 * n, n), 1), 1.0, 0.0).astype(BF16)


def _incl_upper(n):
    return jnp.where((_iota2((2 * n, n), 0) & (n - 1)) <= _iota2((2 * n, n), 1), 1.0, 0.0).astype(BF16)


def _dot_cum(x, tri_bf16):
    hi, lo = _split2(x)
    return jnp.dot(jnp.concatenate([hi, lo], axis=1), tri_bf16, preferred_element_type=F32)


def _sb_fwd(qkv, d_model, name):
    T = qkv.shape[0]
    D = d_model
    H = D // HEAD_DIM
    QB = _pick(T, ATTN_Q_BLOCKS_FWD)
    KB = ATTN_K_BLOCK
    KS = 2 * KB
    ns = QB // KS
    nq = T // QB
    scale = HEAD_DIM ** -0.5

    def body(q_ref, k_ref, v_ref, o_ref, r_ref, w_buf, cum_buf, mask_buf):
        i = pl.program_id(1)

        @pl.when(i == 0)
        def _():
            _fill_score_masks(mask_buf, QB, KB, ns)

        q = q_ref[...]
        tri = _incl_lower(KB)
        n_tot = (i + 1) * ns

        def key_step(m):
            return jnp.maximum(n_tot - 1 - m, 0)

        def rows(ref, s):
            return ref[pl.ds(pl.multiple_of(s * KS, KS), KS), :]

        def scores(s, may_be_diagonal):
            w = _dot_nt(q, rows(k_ref, s)) * (scale * LOG2_E)
            return w + mask_buf[jnp.maximum(s - i * ns + 1, 0)] if may_be_diagonal else w

        def cums(w):
            sp = _softplus_bits(w)[0]
            return jnp.concatenate([_dot_cum(sp[:, :KB], tri), _dot_cum(sp[:, KB:], tri)], axis=1)

        def weights(w, cum, carry):
            a_r = jnp.exp2(w[:, KB:] - cum[:, KB:] - carry)
            carry = carry + cum[:, KB:KB + 1]
            a_l = jnp.exp2(w[:, :KB] - cum[:, :KB] - carry)
            return jnp.concatenate([a_l, a_r], axis=1).astype(BF16), carry + cum[:, 0:1]

        def trip(m, carry):
            w_new = scores(key_step(m + 2), False)
            a, carry = weights(w_buf[m % 3], cum_buf[m % 2], carry)
            o_ref[...] += _dot(a, rows(v_ref, key_step(m)))
            cum_buf[(m + 1) % 2] = cums(w_buf[(m + 1) % 3])
            w_buf[(m + 2) % 3] = w_new
            return carry

        o_ref[...] = jnp.zeros_like(o_ref)
        assert ns == 2
        w_buf[0] = scores(key_step(0), True)
        w_buf[1] = scores(key_step(1), True)
        cum_buf[0] = cums(w_buf[0])
        carry = _run_trips(trip, n_tot, jnp.zeros((QB, 1), F32))
        r_ref[0] = jnp.broadcast_to(carry, (QB, LANES))

    return pl.pallas_call(
        body, name=name, grid=(H, nq),
        in_specs=[pl.BlockSpec((QB, HEAD_DIM), lambda h, i: (i, h)),
                  pl.BlockSpec((T, HEAD_DIM), lambda h, i: (0, H + h)),
                  pl.BlockSpec((T, HEAD_DIM), lambda h, i: (0, 2 * H + h))],
        out_specs=[pl.BlockSpec((QB, HEAD_DIM), lambda h, i: (i, h)),
                   pl.BlockSpec((1, QB, LANES), lambda h, i: (h, i, 0))],
        out_shape=[jax.ShapeDtypeStruct((T, D), F32), jax.ShapeDtypeStruct((H, T, LANES), F32)],
        scratch_shapes=[pltpu.VMEM((3, QB, KS), F32), pltpu.VMEM((2, QB, KS), F32), pltpu.VMEM((ns + 1, QB, KS), F32)],
        compiler_params=_params(2),
    )(qkv, qkv, qkv)


def _gate_mul_fwd(o, gate, name):
    T, D = o.shape
    tm = _pick(T, (512, 256, 128))

    def body(o_ref, g_ref, y_ref):
        y_ref[...] = (o_ref[...] * _silu(g_ref[...])).astype(BF16)

    spec = pl.BlockSpec((tm, D), lambda i: (i, 0))
    return pl.pallas_call(body, name=name, grid=(T // tm,), in_specs=[spec, spec], out_specs=spec,
                          out_shape=jax.ShapeDtypeStruct((T, D), BF16), compiler_params=_params(1))(o, gate)


def _final_loss(h, w, target, name):
    T, D = h.shape
    tm = _pick(T, (512, 256, 128))

    def body(h_ref, w_ref, t_ref, dh_ref, loss_ref, dw_ref):
        x, w = h_ref[...], w_ref[...]
        r = lax.rsqrt(jnp.mean(x * x, axis=-1, keepdims=True) + EPS)
        err = x * r * w - t_ref[...]
        part = 0.5 * jnp.sum(jnp.mean(err * err, axis=-1, keepdims=True), axis=0, keepdims=True)
        dx, dw = _rms_bwd_math(x, w, err * (1.0 / D))
        dh_ref[...] = dx

        @pl.when(pl.program_id(0) == 0)
        def _():
            loss_ref[...] = jnp.zeros_like(loss_ref)
            dw_ref[...] = jnp.zeros_like(dw_ref)

        loss_ref[...] += jnp.broadcast_to(part, loss_ref.shape)
        dw_ref[...] += jnp.broadcast_to(dw, dw_ref.shape)

    return pl.pallas_call(
        body, name=name, grid=(T // tm,),
        in_specs=[pl.BlockSpec((tm, D), lambda i: (i, 0)), pl.BlockSpec((1, D), lambda i: (0, 0)),
                  pl.BlockSpec((tm, D), lambda i: (i, 0))],
        out_specs=[pl.BlockSpec((tm, D), lambda i: (i, 0)), pl.BlockSpec((8, LANES), lambda i: (0, 0)),
                   pl.BlockSpec((8, D), lambda i: (0, 0))],
        out_shape=[jax.ShapeDtypeStruct((T, D), F32), jax.ShapeDtypeStruct((8, LANES), F32),
                   jax.ShapeDtypeStruct((8, D), F32)],
        compiler_params=_params(1),
    )(h, w, target)


def _gate_mul_bwd(dy, o, gate, name):
    T, D = o.shape
    tm = _pick(T, (512, 256, 128))

    def body(dy_ref, o_ref, g_ref, do_ref, dg_ref):
        dy, g = dy_ref[...], g_ref[...]
        s = jax.nn.sigmoid(g)
        do_ref[...] = dy * (g * s)
        dg_ref[...] = (dy * o_ref[...] * (s + g * s * (1.0 - s))).astype(BF16)

    spec = pl.BlockSpec((tm, D), lambda i: (i, 0))
    return pl.pallas_call(body, name=name, grid=(T // tm,), in_specs=[spec] * 3, out_specs=[spec] * 2,
                          out_shape=[jax.ShapeDtypeStruct((T, D), F32), jax.ShapeDtypeStruct((T, D), BF16)],
                          compiler_params=_params(1))(dy, o, gate)


def _sb_bwd(qkv, do, r_tot, d_model, name):
    T = qkv.shape[0]
    D = d_model
    H = D // HEAD_DIM
    QB = _pick(T, ATTN_Q_BLOCKS_BWD)
    KB = ATTN_K_BLOCK
    KS = 2 * KB
    ns = QB // KS
    nq = T // QB
    n_key_steps = T // KS
    scale = HEAD_DIM ** -0.5

    def body(q_ref, k_ref, v_ref, do_ref, r_ref, dq_ref, dk_ref, dv_ref,
             dkt_acc, dvt_acc, dq_acc, w_buf, da_buf, cum_buf, sig_buf, mask_buf):
        i = pl.program_id(1)

        @pl.when(i == 0)
        def _():
            dkt_acc[...] = jnp.zeros_like(dkt_acc)
            dvt_acc[...] = jnp.zeros_like(dvt_acc)
            _fill_score_masks(mask_buf, QB, KB, ns)

        q = q_ref[...]
        do_blk = do_ref[...].astype(BF16)
        q_t = q.astype(F32).T.astype(BF16)
        do_t = do_ref[...].T.astype(BF16)
        row_total = r_ref[0][:, 0:1]
        tri_rev = _incl_lower(KB)
        tri_fwd = jnp.where(_iota2((KB, KB), 0) <= _iota2((KB, KB), 1), 1.0, 0.0).astype(BF16)
        n_tot = (i + 1) * ns

        def step_rows(ref, s):
            return ref[pl.ds(pl.multiple_of(s * KS, KS), KS), :]

        def scores(s):
            w = _dot_nt(q, step_rows(k_ref, s)) * (scale * LOG2_E) + mask_buf[jnp.maximum(s - i * ns + 1, 0)]
            return w, _dot_nt(do_blk, step_rows(v_ref, s))

        def softplus_sums(w):
            sp, one_minus_sig = _softplus_bits(w)
            cum = jnp.concatenate([_dot_cum(sp[:, :KB], tri_rev), _dot_cum(sp[:, KB:], tri_rev)], axis=1)
            return cum, 1.0 - one_minus_sig

        def weights(w, cum, da, left_sp):
            right_l = row_total - left_sp - cum[:, 0:1]
            right_r = right_l - cum[:, KB:KB + 1]
            a = jnp.concatenate([jnp.exp2(w[:, :KB] - cum[:, :KB] - right_l),
                                 jnp.exp2(w[:, KB:] - cum[:, KB:] - right_r)], axis=1)
            p = da * a
            cp = jnp.concatenate([_dot(p[:, :KB], tri_fwd), _dot(p[:, KB:], tri_fwd)], axis=1)
            return a.astype(BF16), p, cp, row_total - right_r

        def score_grads(p, cp, sig, left_p):
            cum_l = cp[:, :KB] + left_p
            cum_r = cp[:, KB:] + cum_l[:, KB - 1:KB]
            dz = p - sig * jnp.concatenate([cum_l, cum_r], axis=1)
            return dz.astype(BF16), cum_r[:, KB - 1:KB]

        def trip(m, st):
            left_sp, left_p = st
            s2 = jnp.minimum(m + 2, n_tot - 1)
            s1 = jnp.minimum(m + 1, n_tot - 1)
            w_new, da_new = scores(s2)
            a, p, cp, left_sp = weights(w_buf[m % 3], cum_buf[m % 2], da_buf[m % 3], left_sp)
            cum_new, sig_new = softplus_sums(w_buf[s1 % 3])
            dz, left_p = score_grads(p, cp, sig_buf[m % 2], left_p)
            dq_acc[...] += _dot(dz, step_rows(k_ref, m))
            dkt_acc[m] += jnp.dot(q_t, dz, preferred_element_type=F32) * scale
            dvt_acc[m] += jnp.dot(do_t, a, preferred_element_type=F32)
            cum_buf[(m + 1) % 2] = cum_new
            sig_buf[(m + 1) % 2] = sig_new
            w_buf[(m + 2) % 3] = w_new
            da_buf[(m + 2) % 3] = da_new
            return left_sp, left_p

        dq_acc[...] = jnp.zeros_like(dq_acc)
        w_buf[0], da_buf[0] = scores(0)
        w_buf[1], da_buf[1] = scores(jnp.minimum(1, n_tot - 1))
        cum_buf[0], sig_buf[0] = softplus_sums(w_buf[0])
        zero_col = jnp.zeros((QB, 1), F32)
        _run_trips(trip, n_tot, (zero_col, zero_col))
        dq_ref[...] = (dq_acc[...] * scale).astype(BF16)

        @pl.when(i == nq - 1)
        def _():
            for s in range(n_key_steps):
                dk_ref[s * KS:(s + 1) * KS, :] = dkt_acc[s].T.astype(BF16)
                dv_ref[s * KS:(s + 1) * KS, :] = dvt_acc[s].T.astype(BF16)

    return pl.pallas_call(
        body, name=name, grid=(H, nq),
        in_specs=[pl.BlockSpec((QB, HEAD_DIM), lambda h, i: (i, h)),
                  pl.BlockSpec((T, HEAD_DIM), lambda h, i: (0, H + h)),
                  pl.BlockSpec((T, HEAD_DIM), lambda h, i: (0, 2 * H + h)),
                  pl.BlockSpec((QB, HEAD_DIM), lambda h, i: (i, h)),
                  pl.BlockSpec((1, QB, LANES), lambda h, i: (h, i, 0))],
        out_specs=[pl.BlockSpec((QB, HEAD_DIM), lambda h, i: (i, h)),
                   pl.BlockSpec((T, HEAD_DIM), lambda h, i: (0, h)),
                   pl.BlockSpec((T, HEAD_DIM), lambda h, i: (0, h))],
        out_shape=[jax.ShapeDtypeStruct((T, D), BF16)] * 3,
        scratch_shapes=[pltpu.VMEM((n_key_steps, HEAD_DIM, KS), F32), pltpu.VMEM((n_key_steps, HEAD_DIM, KS), F32),
                        pltpu.VMEM((QB, HEAD_DIM), F32), pltpu.VMEM((3, QB, KS), F32), pltpu.VMEM((3, QB, KS), F32),
                        pltpu.VMEM((2, QB, KS), F32), pltpu.VMEM((2, QB, KS), F32), pltpu.VMEM((ns + 1, QB, KS), F32)],
        compiler_params=_params(2),
    )(qkv, qkv, qkv, do, r_tot)


def _onorm_gate_bwd(dy, o, p, z_col, o_norm, name):
    T, D = o.shape
    H = D // HEAD_DIM
    tm = _pick(T, (256, 128, 64))

    def body(dy_ref, o_ref, z_ref, w_ref, do_ref, dz_ref, dw_ref):
        _, vjp = jax.vjp(functools.partial(_onorm_gate_math, n_heads=H), o_ref[...], z_ref[...], w_ref[...])
        do, dz, dw = vjp(dy_ref[...])
        do_ref[...] = do
        dz_ref[...] = dz.astype(BF16)

        @pl.when(pl.program_id(0) == 0)
        def _():
            dw_ref[...] = jnp.zeros_like(dw_ref)

        dw_ref[...] += jnp.broadcast_to(dw, dw_ref.shape)

    return pl.pallas_call(
        body, name=name, grid=(T // tm,),
        in_specs=[pl.BlockSpec((tm, D), lambda i: (i, 0)), pl.BlockSpec((tm, D), lambda i: (i, 0)),
                  pl.BlockSpec((tm, D), lambda i: (i, z_col)), pl.BlockSpec((1, HEAD_DIM), lambda i: (0, 0))],
        out_specs=[pl.BlockSpec((tm, D), lambda i: (i, 0)), pl.BlockSpec((tm, D), lambda i: (i, 0)),
                   pl.BlockSpec((8, HEAD_DIM), lambda i: (0, 0))],
        out_shape=[jax.ShapeDtypeStruct((T, D), F32), jax.ShapeDtypeStruct((T, D), BF16),
                   jax.ShapeDtypeStruct((8, HEAD_DIM), F32)],
        compiler_params=_params(1),
    )(dy, o, p, o_norm)


def _row_to_col(row):
    C = row.shape[1]
    eye = _iota2((C, C), 0) == _iota2((C, C), 1)
    return jnp.sum(jnp.where(eye, row, 0.0), axis=1, keepdims=True)


def _lane_sum(x):
    return jnp.sum(x, axis=-1, keepdims=True)


def _chunk_head_bwd(q, k, v, gc, beta, s_in, t_inv, do, ds_out):
    C = q.shape[0]
    r, c = _iota2((C, C), 0), _iota2((C, C), 1)
    causal, strict = r >= c, r > c
    decay = jnp.where(causal, jnp.exp(jnp.where(causal, gc - _col_to_row(gc), 0.0)), 0.0)
    kb, vb = k * beta, v * beta
    eg = jnp.exp(gc)
    kbg = kb * eg
    g_last = gc[C - 1:C, :]
    e_tail = jnp.exp(g_last - gc)
    k_tail = k * e_tail
    gl = jnp.exp(g_last)
    qg = q * eg
    t_inv_t = t_inv.T
    kk = _dot_nt(kb, k)
    u = _dot(t_inv, vb)
    w = _dot(t_inv, kbg)
    qk = _dot_nt(q, k)
    d_qg = _dot_nt(do, s_in)
    ds_state = _dot_tn(qg, do)
    yield
    low = jnp.where(strict, kk * decay, 0.0)
    attn = qk * decay
    w_state = _dot(w, s_in)
    d_vnew_intra = _dot_tn(attn, do)
    d_vnew_state = _dot(k_tail, ds_out)
    yield
    v_new = u - w_state
    d_vnew = d_vnew_intra + d_vnew_state
    d_ktail = _dot_nt(v_new, ds_out)
    d_attn_raw = _dot_nt(do, v_new)
    d_w = -_dot_nt(d_vnew, s_in)
    ds_w = _dot_tn(w, d_vnew)
    d_vb = _dot(t_inv_t, d_vnew)
    d_tinv_u = _dot_nt(d_vnew, vb)
    yield
    d_gl = _lane_sum(jnp.sum(s_in * ds_out, axis=0, keepdims=True))
    d_attn = jnp.where(causal, d_attn_raw, 0.0)
    ds_in = ds_out * gl + ds_state - ds_w
    d_kbg = _dot(t_inv_t, d_w)
    d_tinv_w = _dot_nt(d_w, kbg)
    d_qk = d_attn * decay
    dq_intra = _dot(d_qk, k)
    dk_intra = _dot_tn(d_qk, q)
    yield
    inner = _dot(t_inv_t, d_tinv_u + d_tinv_w)
    yield
    d_low_raw = _dot_nt(inner, t_inv)
    yield
    d_low = jnp.where(strict, -d_low_raw, 0.0)
    d_kk = d_low * decay
    d_kb_low = _dot(d_kk, k)
    dk_low = _dot_tn(d_kk, kb)
    yield
    d_kb = d_kb_low + d_kbg * eg
    dq = dq_intra + d_qg * eg
    dk = dk_low + dk_intra + d_ktail * e_tail + d_kb * beta
    dv = d_vb * beta
    dbeta = _lane_sum(d_kb * k + d_vb * v)
    m = d_low * low + d_attn * attn
    tail = d_ktail * k_tail
    d_g_last = d_gl * gl + _lane_sum(jnp.sum(tail, axis=0, keepdims=True))
    dgc = (_lane_sum(m) - _row_to_col(jnp.sum(m, axis=0, keepdims=True))
           + _lane_sum(d_qg * qg + d_kbg * kbg - tail))
    dgc = dgc + jnp.where(_iota2((C, 1), 0) == C - 1, d_g_last, 0.0)
    return dq, dk, dv, dgc, dbeta, ds_in


def _chunk_bwd(qkv, beta, gc, s_all, t_all, do, d_model, name):
    T = qkv.shape[0]
    D = d_model
    H = D // HEAD_DIM
    N = T // CHUNK
    G = CHUNKS_PER_STEP
    R = G * CHUNK
    n_steps = N // G

    def body(q_ref, k_ref, v_ref, beta_ref, gc_ref, s_ref, t_ref, do_ref, dqkv_ref, dbeta_ref, dg_ref, ds_ref):
        @pl.when(pl.program_id(0) == 0)
        def _():
            ds_ref[...] = jnp.zeros_like(ds_ref)

        lane = _iota2((CHUNK, LANES), 1)
        heads = [slice(h * HEAD_DIM, (h + 1) * HEAD_DIM) for h in range(H)]
        d_states = [ds_ref[h] for h in range(H)]
        for sub in reversed(range(G)):
            rows = slice(sub * CHUNK, (sub + 1) * CHUNK)
            dgc_all = jnp.zeros((CHUNK, LANES), F32)
            dbeta_all = jnp.zeros((CHUNK, LANES), F32)
            results = _lockstep(
                _chunk_head_bwd(q_ref[rows, hs], k_ref[rows, hs], v_ref[rows, hs], gc_ref[rows, h:h + 1],
                                beta_ref[rows, h:h + 1], s_ref[sub, h], t_ref[sub, h], do_ref[rows, hs], d_states[h])
                for h, hs in enumerate(heads))
            for h, (dq, dk, dv, dgc, dbeta, ds_in) in enumerate(results):
                d_states[h] = ds_in
                dqkv_ref[rows, h * HEAD_DIM:(h + 1) * HEAD_DIM] = dq
                dqkv_ref[rows, D + h * HEAD_DIM:D + (h + 1) * HEAD_DIM] = dk
                dqkv_ref[rows, 2 * D + h * HEAD_DIM:2 * D + (h + 1) * HEAD_DIM] = dv
                dgc_all = jnp.where(lane == h, dgc, dgc_all)
                dbeta_all = jnp.where(lane == h, dbeta, dbeta_all)
            dbeta_ref[rows, :] = dbeta_all
            dg_ref[rows, :] = _dot_mask(_chunk_tri(CHUNK, upper=True), dgc_all)
        for h in range(H):
            ds_ref[h] = d_states[h]

    rev = lambda n: n_steps - 1 - n
    return pl.pallas_call(
        body, name=name, grid=(n_steps,),
        in_specs=[pl.BlockSpec((R, D), lambda n: (rev(n), 0)), pl.BlockSpec((R, D), lambda n: (rev(n), 1)),
                  pl.BlockSpec((R, D), lambda n: (rev(n), 2)),
                  pl.BlockSpec((R, LANES), lambda n: (rev(n), 0)), pl.BlockSpec((R, LANES), lambda n: (rev(n), 0)),
                  pl.BlockSpec((G, H, HEAD_DIM, HEAD_DIM), lambda n: (rev(n), 0, 0, 0)),
                  pl.BlockSpec((G, H, CHUNK, CHUNK), lambda n: (rev(n), 0, 0, 0)),
                  pl.BlockSpec((R, D), lambda n: (rev(n), 0))],
        out_specs=[pl.BlockSpec((R, 3 * D), lambda n: (rev(n), 0)),
                   pl.BlockSpec((R, LANES), lambda n: (rev(n), 0)), pl.BlockSpec((R, LANES), lambda n: (rev(n), 0))],
        out_shape=[jax.ShapeDtypeStruct((T, 3 * D), F32), jax.ShapeDtypeStruct((T, LANES), F32),
                   jax.ShapeDtypeStruct((T, LANES), F32)],
        scratch_shapes=[pltpu.VMEM((H, HEAD_DIM, HEAD_DIM), F32)], compiler_params=_params(1),
    )(qkv, qkv, qkv, beta, gc, s_all, t_all, do)


def _gates_bwd(p, gate_params, col0, dbeta, dg, name):
    T = p.shape[0]
    tm = _pick(T, (256, 128, 64))

    def body(pb_ref, pa_ref, gp_ref, dbeta_ref, dg_ref, dp_ref, dgp_ref):
        gp = gp_ref[...]
        _, vjp = jax.vjp(_gates_math, pb_ref[...], pa_ref[...], gp[0:1, :], gp[1:2, :])
        dpb, dpa, d_alog, d_dt = vjp((dbeta_ref[...], dg_ref[...]))
        dp_ref[:, 0:LANES] = dpb.astype(BF16)
        dp_ref[:, LANES:2 * LANES] = dpa.astype(BF16)

        @pl.when(pl.program_id(0) == 0)
        def _():
            dgp_ref[...] = jnp.zeros_like(dgp_ref)

        dgp_ref[0:1, :] += d_alog
        dgp_ref[1:2, :] += d_dt

    return pl.pallas_call(
        body, name=name, grid=(T // tm,),
        in_specs=[pl.BlockSpec((tm, LANES), lambda i: (i, col0)), pl.BlockSpec((tm, LANES), lambda i: (i, col0 + 1)),
                  pl.BlockSpec((8, LANES), lambda i: (0, 0)),
                  pl.BlockSpec((tm, LANES), lambda i: (i, 0)), pl.BlockSpec((tm, LANES), lambda i: (i, 0))],
        out_specs=[pl.BlockSpec((tm, 2 * LANES), lambda i: (i, 0)), pl.BlockSpec((8, LANES), lambda i: (0, 0))],
        out_shape=[jax.ShapeDtypeStruct((T, 2 * LANES), BF16), jax.ShapeDtypeStruct((8, LANES), F32)],
        compiler_params=_params(1),
    )(p, p, gate_params, dbeta, dg)


def _conv_bwd_act(p, conv_w, dqkv, d_model, name):
    T = p.shape[0]
    D = d_model
    H = D // HEAD_DIM
    tm = _pick(T, (256, 128, 64))

    def columns(j, prev_outs):
        def body(cur_ref, prev_ref, w_ref, dout_ref, *rest):
            dc_ref, dw_ref = rest[-2:]
            i = pl.program_id(0)
            prev = prev_ref[...] * (i > 0).astype(F32)
            taps = _conv_taps(cur_ref[...], prev)
            w = w_ref[...]
            c = sum(taps[k] * w[k:k + 1, :] for k in range(4))
            _, vjp = jax.vjp(lambda cc: _qkv_post(cc, j, H), c)
            (dc,) = vjp(dout_ref[...])
            dc_ref[...] = dc

            @pl.when(i == 0)
            def _():
                dw_ref[...] = jnp.zeros_like(dw_ref)

            for k in range(4):
                dw_ref[k:k + 1, :] += jnp.sum(dc * taps[k], axis=0, keepdims=True)

        specs, operands, aliases = _alias_previous(prev_outs, 4)
        return pl.pallas_call(
            body, name="%s%d" % (name, j), grid=(T // tm,),
            in_specs=[pl.BlockSpec((tm, D), lambda i: (i, j)),
                      pl.BlockSpec((8, D), lambda i: (jnp.maximum(i * (tm // 8) - 1, 0), j)),
                      pl.BlockSpec((4, D), lambda i: (0, j)),
                      pl.BlockSpec((tm, D), lambda i: (i, j))] + specs,
            out_specs=[pl.BlockSpec((tm, D), lambda i: (i, j)), pl.BlockSpec((4, D), lambda i: (0, j))],
            out_shape=[jax.ShapeDtypeStruct((T, 3 * D), F32), jax.ShapeDtypeStruct((4, 3 * D), F32)],
            input_output_aliases=aliases, compiler_params=_params(1),
        )(p, p, conv_w, dqkv, *operands)

    return _column_calls(columns)


def _conv_bwd_input(dc, conv_w, name):
    T, D3 = dc.shape
    D = D3 // 3
    tm = _pick(T, (256, 128, 64))
    n_t = T // tm

    def body(cur_ref, next_ref, w_ref, dp_ref):
        i = pl.program_id(0)
        cur = cur_ref[...]
        nxt = next_ref[...] * (i < n_t - 1).astype(F32)
        ext = jnp.concatenate([cur, nxt], axis=0)
        w = w_ref[...]
        acc = cur * w[3:4, :]
        for s in (1, 2, 3):
            acc = acc + pltpu.roll(ext, tm + 8 - s, 0)[0:tm] * w[3 - s:4 - s, :]
        dp_ref[...] = acc.astype(BF16)

    return pl.pallas_call(
        body, name=name, grid=(n_t, 3),
        in_specs=[pl.BlockSpec((tm, D), lambda i, j: (i, j)),
                  pl.BlockSpec((8, D), lambda i, j: (jnp.minimum((i + 1) * (tm // 8), T // 8 - 1), j)),
                  pl.BlockSpec((4, D), lambda i, j: (0, j))],
        out_specs=pl.BlockSpec((tm, D), lambda i, j: (i, j)),
        out_shape=jax.ShapeDtypeStruct((T, D3), BF16), compiler_params=_params(2),
    )(dc, dc, conv_w)


def _comm_call(body, arrays, out_shape, n_remote, n_local, name):
    any_spec = pl.BlockSpec(memory_space=pl.ANY)
    return pl.pallas_call(
        body, name=name, in_specs=[any_spec] * len(arrays), out_specs=[any_spec] * len(out_shape), out_shape=out_shape,
        scratch_shapes=[pltpu.SemaphoreType.DMA((n_remote,)), pltpu.SemaphoreType.DMA((n_remote,)),
                        pltpu.SemaphoreType.DMA((n_local,))],
        compiler_params=pltpu.CompilerParams(has_side_effects=True),
    )(*arrays)


def _gather_shards(arrays, name):
    n = len(arrays)

    def body(*refs):
        ins, outs = refs[:n], refs[n:2 * n]
        send_sems, recv_sems, local_sems = refs[2 * n:]
        x, y, c = lax.axis_index("x"), lax.axis_index("y"), lax.axis_index("c")
        me, sibling = (x, y, c), (x, y, 1 - c)
        chips = [(1 - x, y), (x, 1 - y), (1 - x, 1 - y)]

        def copy(a, k, block, to, src=None):
            dst = outs[a].at[4 * block[0] + 2 * block[1] + block[2]]
            return pltpu.make_async_remote_copy(
                src_ref=dst if src is None else src, dst_ref=dst, send_sem=send_sems.at[a * 7 + k],
                recv_sem=recv_sems.at[a * 7 + k], device_id=to, device_id_type=MESH_ID)

        local = [pltpu.make_async_copy(ins[a], outs[a].at[4 * x + 2 * y + c], local_sems.at[a]) for a in range(n)]
        first = [copy(a, 0, me, sibling, src=ins[a]) for a in range(n)]
        first += [copy(a, 1 + j, me, (*chip, c), src=ins[a]) for j, chip in enumerate(chips) for a in range(n)]
        for cp in local + first:
            cp.start()
        passed = []
        for j, chip in enumerate(chips):
            for a in range(n):
                copy(a, 1 + j, (*chip, c), me).wait_recv()
                passed.append(copy(a, 4 + j, (*chip, c), sibling))
                passed[-1].start()
        for a in range(n):
            copy(a, 0, sibling, me).wait_recv()
            for j, chip in enumerate(chips):
                copy(a, 4 + j, (*chip, 1 - c), me).wait_recv()
        for cp in first + passed:
            cp.wait_send()
        for cp in local:
            cp.wait()

    out_shape = [jax.ShapeDtypeStruct((N_DEV,) + a.shape, a.dtype) for a in arrays]
    return _comm_call(body, arrays, out_shape, 7 * n, n, name)


def _pair_exchange(arrays, name):
    n = len(arrays)

    def body(*refs):
        ins, pair = refs[:n], refs[n:2 * n]
        send_sems, recv_sems, _ = refs[2 * n:]
        x, y, c = lax.axis_index("x"), lax.axis_index("y"), lax.axis_index("c")
        sends = [pltpu.make_async_remote_copy(
            src_ref=ins[a].at[1 - c], dst_ref=pair[a], send_sem=send_sems.at[a], recv_sem=recv_sems.at[a],
            device_id=(x, y, 1 - c), device_id_type=MESH_ID) for a in range(n)]
        for cp in sends:
            cp.start()
        for cp in sends:
            cp.wait_recv()
        for cp in sends:
            cp.wait_send()

    out_shape = [jax.ShapeDtypeStruct(a.shape[1:], a.dtype) for a in arrays]
    return _comm_call(body, arrays, out_shape, n, 1, name)


def _pair_add(own, pair, name, out_dtype):
    _, R, C = own.shape
    tr = next((t for t in (256, 128, 64, 32, 16) if R % t == 0), R)

    def body(a_ref, b_ref, o_ref):
        o_ref[...] = (a_ref[...] + b_ref[...]).astype(out_dtype)

    spec = pl.BlockSpec((1, tr, C), lambda q, i: (q, i, 0))
    return pl.pallas_call(body, name=name, grid=(4, R // tr), in_specs=[spec, spec], out_specs=spec,
                          out_shape=jax.ShapeDtypeStruct(own.shape, out_dtype), compiler_params=_params(2))(own, pair)


def _chip_exchange(arrays, name):
    n = len(arrays)

    def body(*refs):
        ins, outs = refs[:n], refs[n:2 * n]
        send_sems, recv_sems, local_sems = refs[2 * n:]
        x, y, c = lax.axis_index("x"), lax.axis_index("y"), lax.axis_index("c")
        my_chip = 2 * x + y
        chips = [(1 - x, y), (x, 1 - y), (1 - x, 1 - y)]
        local = [pltpu.make_async_copy(ins[a].at[my_chip], outs[a].at[my_chip], local_sems.at[a]) for a in range(n)]
        sends = [pltpu.make_async_remote_copy(
            src_ref=ins[a].at[2 * px + py], dst_ref=outs[a].at[my_chip], send_sem=send_sems.at[a * 3 + j],
            recv_sem=recv_sems.at[a * 3 + j], device_id=(px, py, c), device_id_type=MESH_ID)
            for j, (px, py) in enumerate(chips) for a in range(n)]
        arrivals = [pltpu.make_async_remote_copy(
            src_ref=ins[a].at[my_chip], dst_ref=outs[a].at[2 * px + py], send_sem=send_sems.at[a * 3 + j],
            recv_sem=recv_sems.at[a * 3 + j], device_id=(px, py, c), device_id_type=MESH_ID)
            for j, (px, py) in enumerate(chips) for a in range(n)]
        for cp in local + sends:
            cp.start()
        for cp in arrivals:
            cp.wait_recv()
        for cp in sends:
            cp.wait_send()
        for cp in local:
            cp.wait()

    out_shape = [jax.ShapeDtypeStruct(a.shape, a.dtype) for a in arrays]
    return _comm_call(body, arrays, out_shape, 3 * n, n, name)


def _reduce_adamw(recv, w, m, v, name):
    S, R, C = recv.shape
    tr = next((t for t in (256, 128, 64, 32, 16, 8) if R % t == 0), R)
    c1 = 1.0 - ADAM_B1 ** ADAM_STEP
    c2 = 1.0 - ADAM_B2 ** ADAM_STEP

    def body(r_ref, w_ref, m_ref, v_ref, g_ref, d_ref, nm_ref, nv_ref):
        g = r_ref[0].astype(F32)
        for s in range(1, S):
            g = g + r_ref[s].astype(F32)
        nm = ADAM_B1 * m_ref[...] + (1.0 - ADAM_B1) * g
        nv = ADAM_B2 * v_ref[...] + (1.0 - ADAM_B2) * (g * g)
        g_ref[...] = g
        nm_ref[...] = nm
        nv_ref[...] = nv
        d_ref[...] = -ADAM_LR * ((nm / c1) / (jnp.sqrt(nv / c2) + ADAM_EPS) + ADAM_WD * w_ref[...])

    spec = pl.BlockSpec((tr, C), lambda i: (i, 0))
    return pl.pallas_call(
        body, name=name, grid=(R // tr,),
        in_specs=[pl.BlockSpec((S, tr, C), lambda i: (0, i, 0)), spec, spec, spec], out_specs=[spec] * 4,
        out_shape=[jax.ShapeDtypeStruct((R, C), F32)] * 4, compiler_params=_params(1),
    )(recv, w, m, v)


def _forward_local(x, target, nw0, nw1, fw, wa_in, conv_w, gate_params, o_norm, wa_out, wb_in, wb_out):
    T, D = x.shape
    nD = D // LANES
    sv = {}
    sv["u0"] = _rmsnorm_fwd(x, nw0, "a_norm_fwd")
    sv["pa"] = _mm_nn(sv["u0"], wa_in, "a_in_proj")
    sv["qkv_a"] = _conv_fwd(sv["pa"], conv_w, D, "a_conv_fwd")
    sv["beta"], sv["gc"] = _gates_fwd(sv["pa"], gate_params, 4 * nD, "a_gates_fwd")
    sv["o_a"], sv["s_all"], sv["t_all"] = _chunk_fwd(sv["qkv_a"], sv["beta"], sv["gc"], D, "a_chunk_fwd")
    sv["y_a"] = _onorm_gate_fwd(sv["o_a"], sv["pa"], 3, o_norm, "a_onorm_fwd")
    sv["h1"] = _mm_nn(sv["y_a"], wa_out, "a_out_proj", add=x)
    sv["u1"] = _rmsnorm_fwd(sv["h1"], nw1, "b_norm_fwd")
    sv["qkv_b"] = _mm_nn(sv["u1"], wb_in[:, :3 * D], "b_in_proj_qkv", out_dtype=BF16)
    sv["gate_b"] = _mm_nn(sv["u1"], wb_in[:, 3 * D:], "b_in_proj_gate")
    sv["o_b"], sv["r_b"] = _sb_fwd(sv["qkv_b"], D, "b_attn_fwd")
    sv["y_b"] = _gate_mul_fwd(sv["o_b"], sv["gate_b"], "b_gate_fwd")
    sv["h2"] = _mm_nn(sv["y_b"], wb_out, "b_out_proj", add=sv["h1"])
    sv["dh2"], sv["loss"], sv["dfw"] = _final_loss(sv["h2"], fw, target, "final_loss")
    return sv


def _backward_local(sv, x, nw0, nw1, wa_in, conv_w, gate_params, o_norm, wa_out, wb_in, wb_out):
    T, D = x.shape
    nD = D // LANES
    g = {}
    dh2 = sv["dh2"]
    g["wb_out"] = _mm_tn(sv["y_b"], dh2, "b_out_proj_dw")
    dy_b = _mm_nt([(dh2, wb_out)], "b_out_proj_dx")
    do_b, dgate_b = _gate_mul_bwd(dy_b, sv["o_b"], sv["gate_b"], "b_gate_bwd")
    dq_b, dk_b, dv_b = _sb_bwd(sv["qkv_b"], do_b, sv["r_b"], D, "b_attn_bwd")
    dp_b = [dq_b, dk_b, dv_b, dgate_b]
    g["wb_in"] = jnp.concatenate([_mm_tn(sv["u1"], dp, "b_in_proj_dw%d" % c) for c, dp in enumerate(dp_b)], axis=1)
    du1 = _mm_nt([(dp, wb_in[:, c * D:(c + 1) * D]) for c, dp in enumerate(dp_b)], "b_in_proj_dx")
    dh1, g["nw1"] = _rmsnorm_bwd(sv["h1"], nw1, du1, dh2, "b_norm_bwd")
    g["wa_out"] = _mm_tn(sv["y_a"], dh1, "a_out_proj_dw")
    dy_a = _mm_nt([(dh1, wa_out)], "a_out_proj_dx")
    do_a, dz_a, g["o_norm"] = _onorm_gate_bwd(dy_a, sv["o_a"], sv["pa"], 3, o_norm, "a_onorm_bwd")
    dqkv_a, dbeta, dg = _chunk_bwd(sv["qkv_a"], sv["beta"], sv["gc"], sv["s_all"], sv["t_all"], do_a, D, "a_chunk_bwd")
    dp_gates, g["gate_params"] = _gates_bwd(sv["pa"], gate_params, 4 * nD, dbeta, dg, "a_gates_bwd")
    dc, g["conv_w"] = _conv_bwd_act(sv["pa"], conv_w, dqkv_a, D, "a_conv_bwd_act")
    dp_qkv = _conv_bwd_input(dc, conv_w, "a_conv_bwd_input")
    dp_a = [(dp_qkv, 0, 3 * D), (dz_a, 3 * D, 4 * D), (dp_gates, 4 * D, 4 * D + 2 * LANES)]
    g["wa_in"] = jnp.concatenate([_mm_tn(sv["u0"], dp, "a_in_proj_dw%d" % c) for c, (dp, _, _) in enumerate(dp_a)], axis=1)
    du0 = _mm_nt([(dp, wa_in[:, lo:hi]) for dp, lo, hi in dp_a], "a_in_proj_dx")
    g["x"], g["nw0"] = _rmsnorm_bwd(x, nw0, du0, dh1, "a_norm_bwd")
    g["fw"] = sv["dfw"]
    return g


def kernel(x, norm_w, a_w_in, a_conv_w, a_a_log, a_dt_bias, a_o_norm, a_w_out, b_w_in, b_w_out, final_norm_w, loss_target, m_norm_w, m_a_w_in, m_a_conv_w, m_a_a_log, m_a_dt_bias, m_a_o_norm, m_a_w_out, m_b_w_in, m_b_w_out, m_final_norm_w, v_norm_w, v_a_w_in, v_a_conv_w, v_a_a_log, v_a_dt_bias, v_a_o_norm, v_a_w_out, v_b_w_in, v_b_w_out, v_final_norm_w):
    D = x.shape[-1]
    H = D // HEAD_DIM
    shards = [a_w_in[0].astype(BF16), a_w_out[0].astype(BF16), b_w_in[0].astype(BF16), b_w_out[0].astype(BF16), a_conv_w[0]]
    ga_in, ga_out, gb_in, gb_out, g_conv = _gather_shards(shards, "weights_gather")
    wa = ga_in.transpose(1, 0, 2).reshape(D, -1)
    pad = lambda w: jnp.pad(w, ((0, 0), (0, LANES - w.shape[1])))
    wa_in = jnp.concatenate([wa[:, :4 * D], pad(wa[:, 4 * D:4 * D + H]), pad(wa[:, 4 * D + H:])], axis=1)
    wa_out = ga_out.reshape(D, D)
    wb_in = gb_in.transpose(1, 0, 2).reshape(D, 4 * D)
    wb_out = gb_out.reshape(D, D)
    conv_w = g_conv.transpose(1, 0, 2).reshape(4, 3 * D)
    gate_params = jnp.zeros((8, LANES), F32).at[0, :H].set(a_a_log[0]).at[1, :H].set(a_dt_bias[0])
    nw0, nw1, fw = norm_w[0:1], norm_w[1:2], final_norm_w[None]

    sv = _forward_local(x[0], loss_target[0], nw0, nw1, fw, wa_in, conv_w, gate_params, a_o_norm, wa_out, wb_in, wb_out)
    g = _backward_local(sv, x[0], nw0, nw1, wa_in, conv_w, gate_params, a_o_norm, wa_out, wb_in, wb_out)

    gwa = g["wa_in"]
    gwa = jnp.concatenate([gwa[:, :4 * D], gwa[:, 4 * D:4 * D + H], gwa[:, 4 * D + LANES:4 * D + LANES + H]], axis=1)
    row = lambda v: jnp.pad(v.reshape(1, -1), ((0, 0), (0, D - v.size)))
    small = jnp.concatenate([g["nw0"][0:1], g["nw1"][0:1], g["fw"][0:1], row(g["gate_params"][0, :H]),
                             row(g["gate_params"][1, :H]), row(g["o_norm"][0]), row(sv["loss"][0, 0:1]),
                             jnp.zeros((1, D), F32)], axis=0)
    cols = lambda a: a.reshape(a.shape[0], 4, 2, -1).transpose(2, 1, 0, 3)
    rows = lambda a: a.reshape(4, 2, -1, a.shape[1]).transpose(1, 0, 2, 3)
    contribs = [cols(gwa), rows(g["wa_out"]), cols(g["wb_in"]), rows(g["wb_out"]), cols(g["conv_w"]),
                jnp.broadcast_to(small[None, None], (2, 4, 8, D))]
    pair = _pair_exchange(contribs, "grads_pair_exchange")
    my_core = lax.axis_index("c")
    own = [lax.dynamic_index_in_dim(a, my_core, axis=0, keepdims=False) for a in contribs]
    partial = [_pair_add(o, p, "grads_pair_add%d" % k, BF16 if k < len(own) - 1 else F32)
               for k, (o, p) in enumerate(zip(own, pair))]
    ra_in, ra_out, rb_in, rb_out, r_conv, r_small = _chip_exchange(partial, "grads_chip_exchange")

    outs = {}
    for nm, recv, w, m, v in (("a_w_in", ra_in, a_w_in, m_a_w_in, v_a_w_in), ("a_w_out", ra_out, a_w_out, m_a_w_out, v_a_w_out),
                              ("b_w_in", rb_in, b_w_in, m_b_w_in, v_b_w_in), ("b_w_out", rb_out, b_w_out, m_b_w_out, v_b_w_out),
                              ("a_conv_w", r_conv, a_conv_w, m_a_conv_w, v_a_conv_w)):
        outs[nm] = tuple(o[None] for o in _reduce_adamw(recv, w[0], m[0], v[0], "adamw_" + nm))

    def pack(nw, alog, dt, onorm, fnw):
        return jnp.concatenate([nw, fnw.reshape(1, D), row(alog), row(dt), row(onorm), jnp.zeros((2, D), F32)], axis=0)

    s_g, s_d, s_m, s_v = _reduce_adamw(
        r_small, pack(norm_w, a_a_log, a_dt_bias, a_o_norm, final_norm_w),
        pack(m_norm_w, m_a_a_log, m_a_dt_bias, m_a_o_norm, m_final_norm_w),
        pack(v_norm_w, v_a_a_log, v_a_dt_bias, v_a_o_norm, v_final_norm_w), "adamw_small")
    loss = s_g[6, 0]
    for i, s in enumerate((s_g, s_d, s_m, s_v)):
        outs.setdefault("norm_w", [None] * 4)[i] = s[0:2]
        outs.setdefault("final_norm_w", [None] * 4)[i] = s[2]
        outs.setdefault("a_a_log", [None] * 4)[i] = s[3:4, :H]
        outs.setdefault("a_dt_bias", [None] * 4)[i] = s[4:5, :H]
        outs.setdefault("a_o_norm", [None] * 4)[i] = s[5:6, :HEAD_DIM]
    names = ("norm_w", "a_w_in", "a_conv_w", "a_a_log", "a_dt_bias", "a_o_norm", "a_w_out", "b_w_in", "b_w_out", "final_norm_w")
    return (loss, g["x"][None]) + tuple(outs[n][i] for i in range(4) for n in names)
```

```python
import functools

import jax
import jax.numpy as jnp
from jax import lax
from jax.experimental import pallas as pl
from jax.experimental.pallas import tpu as pltpu

F32 = jnp.float32
BF16 = jnp.bfloat16
EPS = 1e-6
LOG2_E = 1.4426950408889634
MASKED_SCORE = -1e30
HEAD_DIM = 128
CHUNK = 64
CHUNKS_PER_STEP = 2
ATTN_Q_BLOCKS_FWD = (512, 256)
ATTN_Q_BLOCKS_BWD = (512, 256)
ATTN_K_BLOCK = 128
LANES = 128
N_DEV = 8
VMEM_LIMIT_BYTES = 48 * 1024 * 1024
ADAM_LR, ADAM_B1, ADAM_B2, ADAM_EPS, ADAM_WD, ADAM_STEP = 0.001, 0.9, 0.999, 1e-08, 0.01, 10
MESH_ID = pl.DeviceIdType.MESH


def _pick(n, candidates):
    for c in candidates:
        if n % c == 0:
            return c
    raise ValueError(f"no tile for {n} in {candidates}")


def _params(n_grid_axes):
    return pltpu.CompilerParams(dimension_semantics=("arbitrary",) * n_grid_axes, vmem_limit_bytes=VMEM_LIMIT_BYTES)


def _dot(a, b):
    return jnp.dot(a.astype(BF16), b.astype(BF16), preferred_element_type=F32)


def _dot_nt(a, b):
    return lax.dot_general(a.astype(BF16), b.astype(BF16), (((1,), (1,)), ((), ())), preferred_element_type=F32)


def _dot_tn(a, b):
    return lax.dot_general(a.astype(BF16), b.astype(BF16), (((0,), (0,)), ((), ())), preferred_element_type=F32)


def _split2(x):
    hi = x.astype(BF16)
    lo = (x - hi.astype(F32)).astype(BF16)
    return hi, lo


def _split3(x):
    hi = x.astype(BF16)
    r = x - hi.astype(F32)
    mid = r.astype(BF16)
    lo = (r - mid.astype(F32)).astype(BF16)
    return hi, mid, lo


def _dot3(a, b):
    a_hi, a_lo = _split2(a)
    b_hi, b_lo = _split2(b)
    d = functools.partial(jnp.dot, preferred_element_type=F32)
    return d(a_hi, b_hi) + (d(a_hi, b_lo) + d(a_lo, b_hi))


def _silu(x):
    return x * jax.nn.sigmoid(x)


def _softplus(x):
    return jnp.maximum(x, 0.0) + jnp.log1p(jnp.exp(-jnp.abs(x)))


def _iota2(shape, axis):
    return lax.broadcasted_iota(jnp.int32, shape, axis)


def _rms_bwd_math(x, w, dy):
    r = lax.rsqrt(jnp.mean(x * x, axis=-1, keepdims=True) + EPS)
    xhat = x * r
    dxhat = dy * w
    dx = r * (dxhat - xhat * jnp.mean(dxhat * xhat, axis=-1, keepdims=True))
    dw = jnp.sum(dy * xhat, axis=0, keepdims=True)
    return dx, dw


def _rmsnorm_fwd(x, w, name):
    T, D = x.shape
    tm = _pick(T, (512, 256, 128))

    def body(x_ref, w_ref, o_ref):
        xf = x_ref[...]
        r = lax.rsqrt(jnp.mean(xf * xf, axis=-1, keepdims=True) + EPS)
        o_ref[...] = (xf * r * w_ref[...]).astype(BF16)

    return pl.pallas_call(
        body, name=name, grid=(T // tm,),
        in_specs=[pl.BlockSpec((tm, D), lambda i: (i, 0)), pl.BlockSpec((1, D), lambda i: (0, 0))],
        out_specs=pl.BlockSpec((tm, D), lambda i: (i, 0)),
        out_shape=jax.ShapeDtypeStruct((T, D), BF16), compiler_params=_params(1),
    )(x, w)


def _rmsnorm_bwd(x, w, du, dres, name):
    T, D = x.shape
    tm = _pick(T, (512, 256, 128))

    def body(x_ref, w_ref, du_ref, dres_ref, dx_ref, dw_ref):
        dx, dw = _rms_bwd_math(x_ref[...], w_ref[...], du_ref[...].astype(F32))
        dx_ref[...] = dres_ref[...] + dx

        @pl.when(pl.program_id(0) == 0)
        def _():
            dw_ref[...] = jnp.zeros_like(dw_ref)

        dw_ref[...] += jnp.broadcast_to(dw, dw_ref.shape)

    return pl.pallas_call(
        body, name=name, grid=(T // tm,),
        in_specs=[pl.BlockSpec((tm, D), lambda i: (i, 0)), pl.BlockSpec((1, D), lambda i: (0, 0)),
                  pl.BlockSpec((tm, D), lambda i: (i, 0)), pl.BlockSpec((tm, D), lambda i: (i, 0))],
        out_specs=[pl.BlockSpec((tm, D), lambda i: (i, 0)), pl.BlockSpec((8, D), lambda i: (0, 0))],
        out_shape=[jax.ShapeDtypeStruct((T, D), F32), jax.ShapeDtypeStruct((8, D), F32)],
        compiler_params=_params(1),
    )(x, w, du, dres)


def _mm_nn(a, b, name, add=None, out_dtype=F32):
    M, K = a.shape
    _, N = b.shape
    tm = _pick(M, (256, 128)) if N > 2048 else _pick(M, (512, 256, 128))

    def body(*refs):
        a_ref, b_ref = refs[0], refs[1]
        o_ref = refs[-1]
        acc = _dot(a_ref[...], b_ref[...])
        if add is not None:
            acc = acc + refs[2][...]
        o_ref[...] = acc.astype(out_dtype)

    in_specs = [pl.BlockSpec((tm, K), lambda i: (i, 0)), pl.BlockSpec((K, N), lambda i: (0, 0))]
    args = [a, b]
    if add is not None:
        in_specs.append(pl.BlockSpec((tm, N), lambda i: (i, 0)))
        args.append(add)
    return pl.pallas_call(
        body, name=name, grid=(M // tm,), in_specs=in_specs,
        out_specs=pl.BlockSpec((tm, N), lambda i: (i, 0)),
        out_shape=jax.ShapeDtypeStruct((M, N), out_dtype), compiler_params=_params(1),
    )(*args)


def _mm_nt(pairs, name):
    M = pairs[0][0].shape[0]
    N = pairs[0][1].shape[0]
    n = len(pairs)
    tm = _pick(M, (512, 256, 128))

    def body(*refs):
        acc = _dot_nt(refs[0][...], refs[n][...])
        for p in range(1, n):
            acc = acc + _dot_nt(refs[p][...], refs[n + p][...])
        refs[-1][...] = acc

    in_specs = ([pl.BlockSpec((tm, a.shape[1]), lambda i: (i, 0)) for a, _ in pairs]
                + [pl.BlockSpec(b.shape, lambda i: (0, 0)) for _, b in pairs])
    return pl.pallas_call(
        body, name=name, grid=(M // tm,), in_specs=in_specs,
        out_specs=pl.BlockSpec((tm, N), lambda i: (i, 0)),
        out_shape=jax.ShapeDtypeStruct((M, N), F32), compiler_params=_params(1),
    )(*[a for a, _ in pairs], *[b for _, b in pairs])


def _mm_tn(a, b, name):
    R, M = a.shape
    _, N = b.shape
    tn = _pick(N, (1536, 1024, 512, 256, 128))
    tr = _pick(R, (512, 256, 128))

    def body(a_ref, b_ref, o_ref):
        @pl.when(pl.program_id(1) == 0)
        def _():
            o_ref[...] = jnp.zeros_like(o_ref)

        o_ref[...] += _dot_tn(a_ref[...], b_ref[...])

    return pl.pallas_call(
        body, name=name, grid=(N // tn, R // tr),
        in_specs=[pl.BlockSpec((tr, M), lambda j, r: (r, 0)), pl.BlockSpec((tr, tn), lambda j, r: (r, j))],
        out_specs=pl.BlockSpec((M, tn), lambda j, r: (0, j)),
        out_shape=jax.ShapeDtypeStruct((M, N), F32), compiler_params=_params(2),
    )(a, b)


def _qkv_post(c, j, n_heads):
    s = _silu(c)
    if j == 2:
        return s
    parts = []
    for h in range(n_heads):
        sh = s[:, h * HEAD_DIM:(h + 1) * HEAD_DIM]
        parts.append(sh * lax.rsqrt(jnp.sum(sh * sh, axis=-1, keepdims=True) + EPS))
    n = jnp.concatenate(parts, axis=-1)
    return n * (HEAD_DIM ** -0.5) if j == 0 else n


def _column_calls(make_call, n_cols=3):
    outs = None
    for j in range(n_cols):
        outs = make_call(j, outs)
    return outs


def _alias_previous(prev, n_inputs):
    if prev is None:
        return [], [], {}
    prev = list(prev) if isinstance(prev, (list, tuple)) else [prev]
    return ([pl.BlockSpec(memory_space=pl.ANY)] * len(prev), prev, {n_inputs + k: k for k in range(len(prev))})


def _conv_taps(cur, halo_prev):
    tm = cur.shape[0]
    ext = jnp.concatenate([halo_prev, cur], axis=0)
    taps = [pltpu.roll(ext, s, 0)[8:8 + tm] for s in (3, 2, 1)]
    return taps + [cur]


def _conv_fwd(p, conv_w, d_model, name):
    T = p.shape[0]
    D = d_model
    H = D // HEAD_DIM
    tm = _pick(T, (256, 128, 64))

    def columns(j, prev_out):
        def body(cur_ref, prev_ref, w_ref, *rest):
            o_ref = rest[-1]
            prev = prev_ref[...] * (pl.program_id(0) > 0).astype(F32)
            taps = _conv_taps(cur_ref[...], prev)
            w = w_ref[...]
            c = sum(taps[k] * w[k:k + 1, :] for k in range(4))
            o_ref[...] = _qkv_post(c, j, H)

        specs, operands, aliases = _alias_previous(prev_out, 3)
        return pl.pallas_call(
            body, name="%s%d" % (name, j), grid=(T // tm,),
            in_specs=[pl.BlockSpec((tm, D), lambda i: (i, j)),
                      pl.BlockSpec((8, D), lambda i: (jnp.maximum(i * (tm // 8) - 1, 0), j)),
                      pl.BlockSpec((4, D), lambda i: (0, j))] + specs,
            out_specs=pl.BlockSpec((tm, D), lambda i: (i, j)),
            out_shape=jax.ShapeDtypeStruct((T, 3 * D), F32), input_output_aliases=aliases, compiler_params=_params(1),
        )(p, p, conv_w, *operands)

    return _column_calls(columns)


def _chunk_tri(tm, upper):
    r, c = _iota2((tm, tm), 0), _iota2((tm, tm), 1)
    same = (r // CHUNK) == (c // CHUNK)
    tri = (c >= r) if upper else (c <= r)
    return jnp.where(same & tri, 1.0, 0.0).astype(BF16)


def _dot_mask(mask_bf16, x):
    hi, mid, lo = _split3(x)
    d = functools.partial(jnp.dot, preferred_element_type=F32)
    return d(mask_bf16, hi) + (d(mask_bf16, mid) + d(mask_bf16, lo))


def _gates_math(pb, pa, a_log, dt_bias):
    beta = jax.nn.sigmoid(pb)
    g = -jnp.exp(a_log) * _softplus(pa + dt_bias)
    return beta, g


def _gates_fwd(p, gate_params, col0, name):
    T = p.shape[0]
    tm = _pick(T, (256, 128, 64))

    def body(pb_ref, pa_ref, gp_ref, beta_ref, gc_ref):
        gp = gp_ref[...]
        beta, g = _gates_math(pb_ref[...], pa_ref[...], gp[0:1, :], gp[1:2, :])
        beta_ref[...] = beta
        gc_ref[...] = _dot_mask(_chunk_tri(tm, upper=False), g)

    return pl.pallas_call(
        body, name=name, grid=(T // tm,),
        in_specs=[pl.BlockSpec((tm, LANES), lambda i: (i, col0)), pl.BlockSpec((tm, LANES), lambda i: (i, col0 + 1)),
                  pl.BlockSpec((8, LANES), lambda i: (0, 0))],
        out_specs=[pl.BlockSpec((tm, LANES), lambda i: (i, 0))] * 2,
        out_shape=[jax.ShapeDtypeStruct((T, LANES), F32)] * 2, compiler_params=_params(1),
    )(p, p, gate_params)


def _col_to_row(col):
    C = col.shape[0]
    eye = _iota2((C, C), 0) == _iota2((C, C), 1)
    return jnp.sum(jnp.where(eye, col, 0.0), axis=0, keepdims=True)


def _lockstep(generators):
    generators = list(generators)
    results = [None] * len(generators)
    live = list(range(len(generators)))
    while live:
        for idx in list(live):
            try:
                next(generators[idx])
            except StopIteration as done:
                results[idx] = done.value
                live.remove(idx)
    return results


def _unit_lower_inverse(low):
    C = low.shape[0]
    eye = (_iota2((C, C), 0) == _iota2((C, C), 1)).astype(F32)
    t = eye - low
    p = _dot3(low, low)
    yield
    n = 2
    while True:
        tp = _dot3(t, p)
        n *= 2
        if n < C:
            p = _dot3(p, p)
        yield
        t = t + tp
        if n >= C:
            return t


def _chunk_head_fwd(q, k, v, gc, beta, s_in):
    C = q.shape[0]
    r, c = _iota2((C, C), 0), _iota2((C, C), 1)
    causal, strict = r >= c, r > c
    decay = jnp.where(causal, jnp.exp(jnp.where(causal, gc - _col_to_row(gc), 0.0)), 0.0)
    kb, vb = k * beta, v * beta
    eg = jnp.exp(gc)
    kk = _dot_nt(kb, k)
    qk = _dot_nt(q, k)
    o_state = _dot(q * eg, s_in)
    yield
    t_inv = yield from _unit_lower_inverse(jnp.where(strict, kk * decay, 0.0))
    u = _dot(t_inv, vb)
    w = _dot(t_inv, kb * eg)
    yield
    w_state = _dot(w, s_in)
    yield
    v_new = u - w_state
    g_last = gc[C - 1:C, :]
    o_intra = _dot(qk * decay, v_new)
    s_add = _dot_tn(k * jnp.exp(g_last - gc), v_new)
    yield
    return o_state + o_intra, s_in * jnp.exp(g_last) + s_add, t_inv


def _chunk_fwd(qkv, beta, gc, d_model, name):
    T = qkv.shape[0]
    D = d_model
    H = D // HEAD_DIM
    N = T // CHUNK
    G = CHUNKS_PER_STEP
    R = G * CHUNK

    def body(q_ref, k_ref, v_ref, beta_ref, gc_ref, o_ref, s_all_ref, t_all_ref, s_ref):
        @pl.when(pl.program_id(0) == 0)
        def _():
            s_ref[...] = jnp.zeros_like(s_ref)

        heads = [slice(h * HEAD_DIM, (h + 1) * HEAD_DIM) for h in range(H)]
        states = [s_ref[h] for h in range(H)]
        for sub in range(G):
            rows = slice(sub * CHUNK, (sub + 1) * CHUNK)
            for h in range(H):
                s_all_ref[sub, h] = states[h]
            results = _lockstep(_chunk_head_fwd(q_ref[rows, hs], k_ref[rows, hs], v_ref[rows, hs], gc_ref[rows, h:h + 1],
                                                beta_ref[rows, h:h + 1], states[h]) for h, hs in enumerate(heads))
            for h, (o, s_out, t_inv) in enumerate(results):
                o_ref[rows, heads[h]] = o
                states[h] = s_out
                t_all_ref[sub, h] = t_inv
        for h in range(H):
            s_ref[h] = states[h]

    return pl.pallas_call(
        body, name=name, grid=(N // G,),
        in_specs=[pl.BlockSpec((R, D), lambda n: (n, 0)), pl.BlockSpec((R, D), lambda n: (n, 1)),
                  pl.BlockSpec((R, D), lambda n: (n, 2)),
                  pl.BlockSpec((R, LANES), lambda n: (n, 0)), pl.BlockSpec((R, LANES), lambda n: (n, 0))],
        out_specs=[pl.BlockSpec((R, D), lambda n: (n, 0)),
                   pl.BlockSpec((G, H, HEAD_DIM, HEAD_DIM), lambda n: (n, 0, 0, 0)),
                   pl.BlockSpec((G, H, CHUNK, CHUNK), lambda n: (n, 0, 0, 0))],
        out_shape=[jax.ShapeDtypeStruct((T, D), F32), jax.ShapeDtypeStruct((N, H, HEAD_DIM, HEAD_DIM), F32),
                   jax.ShapeDtypeStruct((N, H, CHUNK, CHUNK), F32)],
        scratch_shapes=[pltpu.VMEM((H, HEAD_DIM, HEAD_DIM), F32)], compiler_params=_params(1),
    )(qkv, qkv, qkv, beta, gc)


def _onorm_gate_math(o, z, w, n_heads):
    parts = []
    for h in range(n_heads):
        hs = slice(h * HEAD_DIM, (h + 1) * HEAD_DIM)
        oh = o[:, hs]
        y = oh * lax.rsqrt(jnp.mean(oh * oh, axis=-1, keepdims=True) + EPS) * w
        parts.append(y * _silu(z[:, hs]))
    return jnp.concatenate(parts, axis=-1)


def _onorm_gate_fwd(o, p, z_col, o_norm, name):
    T, D = o.shape
    H = D // HEAD_DIM
    tm = _pick(T, (256, 128, 64))

    def body(o_ref, z_ref, w_ref, y_ref):
        y_ref[...] = _onorm_gate_math(o_ref[...], z_ref[...], w_ref[...], H).astype(BF16)

    return pl.pallas_call(
        body, name=name, grid=(T // tm,),
        in_specs=[pl.BlockSpec((tm, D), lambda i: (i, 0)), pl.BlockSpec((tm, D), lambda i: (i, z_col)),
                  pl.BlockSpec((1, HEAD_DIM), lambda i: (0, 0))],
        out_specs=pl.BlockSpec((tm, D), lambda i: (i, 0)),
        out_shape=jax.ShapeDtypeStruct((T, D), BF16), compiler_params=_params(1),
    )(o, p, o_norm)


def _diag_mask(qb, kb, d):
    return _iota2((qb, kb), 0) > _iota2((qb, kb), 1) + d * kb


def _run_trips(trip, n, state):
    def six(j, st):
        for u in range(6):
            st = trip(6 * j + u, st, u)
        return st

    state = lax.fori_loop(0, n // 6, six, state)
    base = (n // 6) * 6
    for u in (0, 2):
        pair = lambda st, u=u: trip(base + u + 1, trip(base + u, st, u), u + 1)
        state = lax.cond(n - base > u, pair, lambda st: st, state)
    return state


def _fill_score_masks(mask_buf, qb, kb, ns):
    mask_buf[0] = jnp.zeros(mask_buf.shape[1:], F32)
    for d in range(ns):
        for half in range(2):
            mask_buf[d + 1, :, half * kb:(half + 1) * kb] = jnp.where(_diag_mask(qb, kb, 2 * d + half), 0.0, MASKED_SCORE)


def _softplus_bits(w):
    u = 1.0 + jnp.exp2(jnp.minimum(w, 64.0))
    return jnp.maximum(w, jnp.log2(u)), 1.0 / u


def _incl_lower(n):
    return jnp.where((_iota2((2 * n, n), 0) & (n - 1)) >= _iota2((2 * n, n), 1), 1.0, 0.0).astype(BF16)


def _incl_upper(n):
    return jnp.where((_iota2((2 * n, n), 0) & (n - 1)) <= _iota2((2 * n, n), 1), 1.0, 0.0).astype(BF16)


def _dot_cum(x, tri_bf16):
    hi, lo = _split2(x)
    return jnp.dot(jnp.concatenate([hi, lo], axis=1), tri_bf16, preferred_element_type=F32)


def _sb_fwd(qkv, d_model, name):
    T = qkv.shape[0]
    D = d_model
    H = D // HEAD_DIM
    QB = _pick(T, ATTN_Q_BLOCKS_FWD)
    KB = ATTN_K_BLOCK
    KS = 2 * KB
    ns = QB // KS
    nq = T // QB
    scale = HEAD_DIM ** -0.5

    def body(q_ref, k_ref, v_ref, o_ref, r_ref, w0, w1, w2, cum0, cum1, mask_buf):
        w_bufs, cum_bufs = (w0, w1, w2), (cum0, cum1)
        i = pl.program_id(1)

        @pl.when(i == 0)
        def _():
            _fill_score_masks(mask_buf, QB, KB, ns)

        q = q_ref[...]
        tri = _incl_lower(KB)
        n_tot = (i + 1) * ns

        def key_step(m):
            return jnp.maximum(n_tot - 1 - m, 0)

        def rows(ref, s):
            return ref[pl.ds(pl.multiple_of(s * KS, KS), KS), :]

        def scores(s, may_be_diagonal):
            w = _dot_nt(q, rows(k_ref, s)) * (scale * LOG2_E)
            return w + mask_buf[jnp.maximum(s - i * ns + 1, 0)] if may_be_diagonal else w

        def cums(w):
            sp = _softplus_bits(w)[0]
            return jnp.concatenate([_dot_cum(sp[:, :KB], tri), _dot_cum(sp[:, KB:], tri)], axis=1)

        def weights(w, cum, carry):
            a_r = jnp.exp2(w[:, KB:] - cum[:, KB:] - carry)
            carry = carry + cum[:, KB:KB + 1]
            a_l = jnp.exp2(w[:, :KB] - cum[:, :KB] - carry)
            return jnp.concatenate([a_l, a_r], axis=1).astype(BF16), carry + cum[:, 0:1]

        def trip(m, carry, ph):
            w_bufs[(ph + 2) % 3][...] = scores(key_step(m + 2), False)
            a, carry = weights(w_bufs[ph % 3][...], cum_bufs[ph % 2][...], carry)
            o_ref[...] += _dot(a, rows(v_ref, key_step(m)))
            cum_bufs[(ph + 1) % 2][...] = cums(w_bufs[(ph + 1) % 3][...])
            return carry

        o_ref[...] = jnp.zeros_like(o_ref)
        assert ns == 2
        w_bufs[0][...] = scores(key_step(0), True)
        w_bufs[1][...] = scores(key_step(1), True)
        cum_bufs[0][...] = cums(w_bufs[0][...])
        carry = _run_trips(trip, n_tot, jnp.zeros((QB, 1), F32))
        r_ref[0] = jnp.broadcast_to(carry, (QB, LANES))

    return pl.pallas_call(
        body, name=name, grid=(H, nq),
        in_specs=[pl.BlockSpec((QB, HEAD_DIM), lambda h, i: (i, h)),
                  pl.BlockSpec((T, HEAD_DIM), lambda h, i: (0, H + h)),
                  pl.BlockSpec((T, HEAD_DIM), lambda h, i: (0, 2 * H + h))],
        out_specs=[pl.BlockSpec((QB, HEAD_DIM), lambda h, i: (i, h)),
                   pl.BlockSpec((1, QB, LANES), lambda h, i: (h, i, 0))],
        out_shape=[jax.ShapeDtypeStruct((T, D), F32), jax.ShapeDtypeStruct((H, T, LANES), F32)],
        scratch_shapes=[pltpu.VMEM((QB, KS), F32)] * 5 + [pltpu.VMEM((ns + 1, QB, KS), F32)],
        compiler_params=_params(2),
    )(qkv, qkv, qkv)


def _gate_mul_fwd(o, gate, name):
    T, D = o.shape
    tm = _pick(T, (512, 256, 128))

    def body(o_ref, g_ref, y_ref):
        y_ref[...] = (o_ref[...] * _silu(g_ref[...])).astype(BF16)

    spec = pl.BlockSpec((tm, D), lambda i: (i, 0))
    return pl.pallas_call(body, name=name, grid=(T // tm,), in_specs=[spec, spec], out_specs=spec,
                          out_shape=jax.ShapeDtypeStruct((T, D), BF16), compiler_params=_params(1))(o, gate)


def _final_loss(h, w, target, name):
    T, D = h.shape
    tm = _pick(T, (512, 256, 128))

    def body(h_ref, w_ref, t_ref, dh_ref, loss_ref, dw_ref):
        x, w = h_ref[...], w_ref[...]
        r = lax.rsqrt(jnp.mean(x * x, axis=-1, keepdims=True) + EPS)
        err = x * r * w - t_ref[...]
        part = 0.5 * jnp.sum(jnp.mean(err * err, axis=-1, keepdims=True), axis=0, keepdims=True)
        dx, dw = _rms_bwd_math(x, w, err * (1.0 / D))
        dh_ref[...] = dx

        @pl.when(pl.program_id(0) == 0)
        def _():
            loss_ref[...] = jnp.zeros_like(loss_ref)
            dw_ref[...] = jnp.zeros_like(dw_ref)

        loss_ref[...] += jnp.broadcast_to(part, loss_ref.shape)
        dw_ref[...] += jnp.broadcast_to(dw, dw_ref.shape)

    return pl.pallas_call(
        body, name=name, grid=(T // tm,),
        in_specs=[pl.BlockSpec((tm, D), lambda i: (i, 0)), pl.BlockSpec((1, D), lambda i: (0, 0)),
                  pl.BlockSpec((tm, D), lambda i: (i, 0))],
        out_specs=[pl.BlockSpec((tm, D), lambda i: (i, 0)), pl.BlockSpec((8, LANES), lambda i: (0, 0)),
                   pl.BlockSpec((8, D), lambda i: (0, 0))],
        out_shape=[jax.ShapeDtypeStruct((T, D), F32), jax.ShapeDtypeStruct((8, LANES), F32),
                   jax.ShapeDtypeStruct((8, D), F32)],
        compiler_params=_params(1),
    )(h, w, target)


def _gate_mul_bwd(dy, o, gate, name):
    T, D = o.shape
    tm = _pick(T, (512, 256, 128))

    def body(dy_ref, o_ref, g_ref, do_ref, dg_ref):
        dy, g = dy_ref[...], g_ref[...]
        s = jax.nn.sigmoid(g)
        do_ref[...] = dy * (g * s)
        dg_ref[...] = (dy * o_ref[...] * (s + g * s * (1.0 - s))).astype(BF16)

    spec = pl.BlockSpec((tm, D), lambda i: (i, 0))
    return pl.pallas_call(body, name=name, grid=(T // tm,), in_specs=[spec] * 3, out_specs=[spec] * 2,
                          out_shape=[jax.ShapeDtypeStruct((T, D), F32), jax.ShapeDtypeStruct((T, D), BF16)],
                          compiler_params=_params(1))(dy, o, gate)


def _sb_bwd(qkv, do, r_tot, d_model, name):
    T = qkv.shape[0]
    D = d_model
    H = D // HEAD_DIM
    QB = _pick(T, ATTN_Q_BLOCKS_BWD)
    KB = ATTN_K_BLOCK
    KS = 2 * KB
    ns = QB // KS
    nq = T // QB
    n_key_steps = T // KS
    scale = HEAD_DIM ** -0.5

    def body(q_ref, k_ref, v_ref, do_ref, r_ref, dq_ref, dk_ref, dv_ref,
             dkt_acc, dvt_acc, dq_acc, w0, w1, w2, da0, da1, da2, cum0, cum1, sig0, sig1, mask_buf):
        w_bufs, da_bufs, cum_bufs, sig_bufs = (w0, w1, w2), (da0, da1, da2), (cum0, cum1), (sig0, sig1)
        i = pl.program_id(1)

        @pl.when(i == 0)
        def _():
            dkt_acc[...] = jnp.zeros_like(dkt_acc)
            dvt_acc[...] = jnp.zeros_like(dvt_acc)
            _fill_score_masks(mask_buf, QB, KB, ns)

        q = q_ref[...]
        do_blk = do_ref[...].astype(BF16)
        q_t = q.astype(F32).T.astype(BF16)
        do_t = do_ref[...].T.astype(BF16)
        row_total = r_ref[0][:, 0:1]
        tri_rev = _incl_lower(KB)
        tri_fwd = jnp.where(_iota2((KB, KB), 0) <= _iota2((KB, KB), 1), 1.0, 0.0).astype(BF16)
        n_tot = (i + 1) * ns

        def step_rows(ref, s):
            return ref[pl.ds(pl.multiple_of(s * KS, KS), KS), :]

        def scores(s):
            w = _dot_nt(q, step_rows(k_ref, s)) * (scale * LOG2_E) + mask_buf[jnp.maximum(s - i * ns + 1, 0)]
            return w, _dot_nt(do_blk, step_rows(v_ref, s))

        def softplus_sums(w):
            sp, one_minus_sig = _softplus_bits(w)
            cum = jnp.concatenate([_dot_cum(sp[:, :KB], tri_rev), _dot_cum(sp[:, KB:], tri_rev)], axis=1)
            return cum, 1.0 - one_minus_sig

        def weights(w, cum, da, left_sp):
            right_l = row_total - left_sp - cum[:, 0:1]
            right_r = right_l - cum[:, KB:KB + 1]
            a = jnp.concatenate([jnp.exp2(w[:, :KB] - cum[:, :KB] - right_l),
                                 jnp.exp2(w[:, KB:] - cum[:, KB:] - right_r)], axis=1)
            p = da * a
            cp = jnp.concatenate([_dot(p[:, :KB], tri_fwd), _dot(p[:, KB:], tri_fwd)], axis=1)
            return a.astype(BF16), p, cp, row_total - right_r

        def score_grads(p, cp, sig, left_p):
            cum_l = cp[:, :KB] + left_p
            cum_r = cp[:, KB:] + cum_l[:, KB - 1:KB]
            dz = p - sig * jnp.concatenate([cum_l, cum_r], axis=1)
            return dz.astype(BF16), cum_r[:, KB - 1:KB]

        def trip(m, st, ph):
            left_sp, left_p = st
            w_bufs[(ph + 2) % 3][...], da_bufs[(ph + 2) % 3][...] = scores(jnp.minimum(m + 2, n_tot - 1))
            a, p, cp, left_sp = weights(w_bufs[ph % 3][...], cum_bufs[ph % 2][...], da_bufs[ph % 3][...], left_sp)
            cum_bufs[(ph + 1) % 2][...], sig_bufs[(ph + 1) % 2][...] = softplus_sums(w_bufs[(ph + 1) % 3][...])
            dz, left_p = score_grads(p, cp, sig_bufs[ph % 2][...], left_p)
            dq_acc[...] += _dot(dz, step_rows(k_ref, m))
            dkt_acc[m] += jnp.dot(q_t, dz, preferred_element_type=F32) * scale
            dvt_acc[m] += jnp.dot(do_t, a, preferred_element_type=F32)
            return left_sp, left_p

        dq_acc[...] = jnp.zeros_like(dq_acc)
        w_bufs[0][...], da_bufs[0][...] = scores(0)
        w_bufs[1][...], da_bufs[1][...] = scores(jnp.minimum(1, n_tot - 1))
        cum_bufs[0][...], sig_bufs[0][...] = softplus_sums(w_bufs[0][...])
        zero_col = jnp.zeros((QB, 1), F32)
        _run_trips(trip, n_tot, (zero_col, zero_col))
        dq_ref[...] = (dq_acc[...] * scale).astype(BF16)

        @pl.when(i == nq - 1)
        def _():
            for s in range(n_key_steps):
                dk_ref[s * KS:(s + 1) * KS, :] = dkt_acc[s].T.astype(BF16)
                dv_ref[s * KS:(s + 1) * KS, :] = dvt_acc[s].T.astype(BF16)

    return pl.pallas_call(
        body, name=name, grid=(H, nq),
        in_specs=[pl.BlockSpec((QB, HEAD_DIM), lambda h, i: (i, h)),
                  pl.BlockSpec((T, HEAD_DIM), lambda h, i: (0, H + h)),
                  pl.BlockSpec((T, HEAD_DIM), lambda h, i: (0, 2 * H + h)),
                  pl.BlockSpec((QB, HEAD_DIM), lambda h, i: (i, h)),
                  pl.BlockSpec((1, QB, LANES), lambda h, i: (h, i, 0))],
        out_specs=[pl.BlockSpec((QB, HEAD_DIM), lambda h, i: (i, h)),
                   pl.BlockSpec((T, HEAD_DIM), lambda h, i: (0, h)),
                   pl.BlockSpec((T, HEAD_DIM), lambda h, i: (0, h))],
        out_shape=[jax.ShapeDtypeStruct((T, D), BF16)] * 3,
        scratch_shapes=[pltpu.VMEM((n_key_steps, HEAD_DIM, KS), F32), pltpu.VMEM((n_key_steps, HEAD_DIM, KS), F32),
                        pltpu.VMEM((QB, HEAD_DIM), F32)] + [pltpu.VMEM((QB, KS), F32)] * 10
                       + [pltpu.VMEM((ns + 1, QB, KS), F32)],
        compiler_params=_params(2),
    )(qkv, qkv, qkv, do, r_tot)


def _onorm_gate_bwd(dy, o, p, z_col, o_norm, name):
    T, D = o.shape
    H = D // HEAD_DIM
    tm = _pick(T, (256, 128, 64))

    def body(dy_ref, o_ref, z_ref, w_ref, do_ref, dz_ref, dw_ref):
        _, vjp = jax.vjp(functools.partial(_onorm_gate_math, n_heads=H), o_ref[...], z_ref[...], w_ref[...])
        do, dz, dw = vjp(dy_ref[...])
        do_ref[...] = do
        dz_ref[...] = dz.astype(BF16)

        @pl.when(pl.program_id(0) == 0)
        def _():
            dw_ref[...] = jnp.zeros_like(dw_ref)

        dw_ref[...] += jnp.broadcast_to(dw, dw_ref.shape)

    return pl.pallas_call(
        body, name=name, grid=(T // tm,),
        in_specs=[pl.BlockSpec((tm, D), lambda i: (i, 0)), pl.BlockSpec((tm, D), lambda i: (i, 0)),
                  pl.BlockSpec((tm, D), lambda i: (i, z_col)), pl.BlockSpec((1, HEAD_DIM), lambda i: (0, 0))],
        out_specs=[pl.BlockSpec((tm, D), lambda i: (i, 0)), pl.BlockSpec((tm, D), lambda i: (i, 0)),
                   pl.BlockSpec((8, HEAD_DIM), lambda i: (0, 0))],
        out_shape=[jax.ShapeDtypeStruct((T, D), F32), jax.ShapeDtypeStruct((T, D), BF16),
                   jax.ShapeDtypeStruct((8, HEAD_DIM), F32)],
        compiler_params=_params(1),
    )(dy, o, p, o_norm)


def _row_to_col(row):
    C = row.shape[1]
    eye = _iota2((C, C), 0) == _iota2((C, C), 1)
    return jnp.sum(jnp.where(eye, row, 0.0), axis=1, keepdims=True)


def _lane_sum(x):
    return jnp.sum(x, axis=-1, keepdims=True)


def _chunk_head_bwd(q, k, v, gc, beta, s_in, t_inv, do, ds_out):
    C = q.shape[0]
    r, c = _iota2((C, C), 0), _iota2((C, C), 1)
    causal, strict = r >= c, r > c
    decay = jnp.where(causal, jnp.exp(jnp.where(causal, gc - _col_to_row(gc), 0.0)), 0.0)
    kb, vb = k * beta, v * beta
    eg = jnp.exp(gc)
    kbg = kb * eg
    g_last = gc[C - 1:C, :]
    e_tail = jnp.exp(g_last - gc)
    k_tail = k * e_tail
    gl = jnp.exp(g_last)
    qg = q * eg
    t_inv_t = t_inv.T
    kk = _dot_nt(kb, k)
    u = _dot(t_inv, vb)
    w = _dot(t_inv, kbg)
    qk = _dot_nt(q, k)
    d_qg = _dot_nt(do, s_in)
    ds_state = _dot_tn(qg, do)
    yield
    low = jnp.where(strict, kk * decay, 0.0)
    attn = qk * decay
    w_state = _dot(w, s_in)
    d_vnew_intra = _dot_tn(attn, do)
    d_vnew_state = _dot(k_tail, ds_out)
    yield
    v_new = u - w_state
    d_vnew = d_vnew_intra + d_vnew_state
    d_ktail = _dot_nt(v_new, ds_out)
    d_attn_raw = _dot_nt(do, v_new)
    d_w = -_dot_nt(d_vnew, s_in)
    ds_w = _dot_tn(w, d_vnew)
    d_vb = _dot(t_inv_t, d_vnew)
    d_tinv_u = _dot_nt(d_vnew, vb)
    yield
    d_gl = _lane_sum(jnp.sum(s_in * ds_out, axis=0, keepdims=True))
    d_attn = jnp.where(causal, d_attn_raw, 0.0)
    ds_in = ds_out * gl + ds_state - ds_w
    d_kbg = _dot(t_inv_t, d_w)
    d_tinv_w = _dot_nt(d_w, kbg)
    d_qk = d_attn * decay
    dq_intra = _dot(d_qk, k)
    dk_intra = _dot_tn(d_qk, q)
    yield
    inner = _dot(t_inv_t, d_tinv_u + d_tinv_w)
    yield
    d_low_raw = _dot_nt(inner, t_inv)
    yield
    d_low = jnp.where(strict, -d_low_raw, 0.0)
    d_kk = d_low * decay
    d_kb_low = _dot(d_kk, k)
    dk_low = _dot_tn(d_kk, kb)
    yield
    d_kb = d_kb_low + d_kbg * eg
    dq = dq_intra + d_qg * eg
    dk = dk_low + dk_intra + d_ktail * e_tail + d_kb * beta
    dv = d_vb * beta
    dbeta = _lane_sum(d_kb * k + d_vb * v)
    m = d_low * low + d_attn * attn
    tail = d_ktail * k_tail
    d_g_last = d_gl * gl + _lane_sum(jnp.sum(tail, axis=0, keepdims=True))
    dgc = (_lane_sum(m) - _row_to_col(jnp.sum(m, axis=0, keepdims=True))
           + _lane_sum(d_qg * qg + d_kbg * kbg - tail))
    dgc = dgc + jnp.where(_iota2((C, 1), 0) == C - 1, d_g_last, 0.0)
    return dq, dk, dv, dgc, dbeta, ds_in


def _chunk_bwd(qkv, beta, gc, s_all, t_all, do, d_model, name):
    T = qkv.shape[0]
    D = d_model
    H = D // HEAD_DIM
    N = T // CHUNK
    G = CHUNKS_PER_STEP
    R = G * CHUNK
    n_steps = N // G

    def body(q_ref, k_ref, v_ref, beta_ref, gc_ref, s_ref, t_ref, do_ref, dqkv_ref, dbeta_ref, dg_ref, ds_ref):
        @pl.when(pl.program_id(0) == 0)
        def _():
            ds_ref[...] = jnp.zeros_like(ds_ref)

        lane = _iota2((CHUNK, LANES), 1)
        heads = [slice(h * HEAD_DIM, (h + 1) * HEAD_DIM) for h in range(H)]
        d_states = [ds_ref[h] for h in range(H)]
        for sub in reversed(range(G)):
            rows = slice(sub * CHUNK, (sub + 1) * CHUNK)
            dgc_all = jnp.zeros((CHUNK, LANES), F32)
            dbeta_all = jnp.zeros((CHUNK, LANES), F32)
            results = _lockstep(
                _chunk_head_bwd(q_ref[rows, hs], k_ref[rows, hs], v_ref[rows, hs], gc_ref[rows, h:h + 1],
                                beta_ref[rows, h:h + 1], s_ref[sub, h], t_ref[sub, h], do_ref[rows, hs], d_states[h])
                for h, hs in enumerate(heads))
            for h, (dq, dk, dv, dgc, dbeta, ds_in) in enumerate(results):
                d_states[h] = ds_in
                dqkv_ref[rows, h * HEAD_DIM:(h + 1) * HEAD_DIM] = dq
                dqkv_ref[rows, D + h * HEAD_DIM:D + (h + 1) * HEAD_DIM] = dk
                dqkv_ref[rows, 2 * D + h * HEAD_DIM:2 * D + (h + 1) * HEAD_DIM] = dv
                dgc_all = jnp.where(lane == h, dgc, dgc_all)
                dbeta_all = jnp.where(lane == h, dbeta, dbeta_all)
            dbeta_ref[rows, :] = dbeta_all
            dg_ref[rows, :] = _dot_mask(_chunk_tri(CHUNK, upper=True), dgc_all)
        for h in range(H):
            ds_ref[h] = d_states[h]

    rev = lambda n: n_steps - 1 - n
    return pl.pallas_call(
        body, name=name, grid=(n_steps,),
        in_specs=[pl.BlockSpec((R, D), lambda n: (rev(n), 0)), pl.BlockSpec((R, D), lambda n: (rev(n), 1)),
                  pl.BlockSpec((R, D), lambda n: (rev(n), 2)),
                  pl.BlockSpec((R, LANES), lambda n: (rev(n), 0)), pl.BlockSpec((R, LANES), lambda n: (rev(n), 0)),
                  pl.BlockSpec((G, H, HEAD_DIM, HEAD_DIM), lambda n: (rev(n), 0, 0, 0)),
                  pl.BlockSpec((G, H, CHUNK, CHUNK), lambda n: (rev(n), 0, 0, 0)),
                  pl.BlockSpec((R, D), lambda n: (rev(n), 0))],
        out_specs=[pl.BlockSpec((R, 3 * D), lambda n: (rev(n), 0)),
                   pl.BlockSpec((R, LANES), lambda n: (rev(n), 0)), pl.BlockSpec((R, LANES), lambda n: (rev(n), 0))],
        out_shape=[jax.ShapeDtypeStruct((T, 3 * D), F32), jax.ShapeDtypeStruct((T, LANES), F32),
                   jax.ShapeDtypeStruct((T, LANES), F32)],
        scratch_shapes=[pltpu.VMEM((H, HEAD_DIM, HEAD_DIM), F32)], compiler_params=_params(1),
    )(qkv, qkv, qkv, beta, gc, s_all, t_all, do)


def _gates_bwd(p, gate_params, col0, dbeta, dg, name):
    T = p.shape[0]
    tm = _pick(T, (256, 128, 64))

    def body(pb_ref, pa_ref, gp_ref, dbeta_ref, dg_ref, dp_ref, dgp_ref):
        gp = gp_ref[...]
        _, vjp = jax.vjp(_gates_math, pb_ref[...], pa_ref[...], gp[0:1, :], gp[1:2, :])
        dpb, dpa, d_alog, d_dt = vjp((dbeta_ref[...], dg_ref[...]))
        dp_ref[:, 0:LANES] = dpb.astype(BF16)
        dp_ref[:, LANES:2 * LANES] = dpa.astype(BF16)

        @pl.when(pl.program_id(0) == 0)
        def _():
            dgp_ref[...] = jnp.zeros_like(dgp_ref)

        dgp_ref[0:1, :] += d_alog
        dgp_ref[1:2, :] += d_dt

    return pl.pallas_call(
        body, name=name, grid=(T // tm,),
        in_specs=[pl.BlockSpec((tm, LANES), lambda i: (i, col0)), pl.BlockSpec((tm, LANES), lambda i: (i, col0 + 1)),
                  pl.BlockSpec((8, LANES), lambda i: (0, 0)),
                  pl.BlockSpec((tm, LANES), lambda i: (i, 0)), pl.BlockSpec((tm, LANES), lambda i: (i, 0))],
        out_specs=[pl.BlockSpec((tm, 2 * LANES), lambda i: (i, 0)), pl.BlockSpec((8, LANES), lambda i: (0, 0))],
        out_shape=[jax.ShapeDtypeStruct((T, 2 * LANES), BF16), jax.ShapeDtypeStruct((8, LANES), F32)],
        compiler_params=_params(1),
    )(p, p, gate_params, dbeta, dg)


def _conv_bwd_act(p, conv_w, dqkv, d_model, name):
    T = p.shape[0]
    D = d_model
    H = D // HEAD_DIM
    tm = _pick(T, (256, 128, 64))

    def columns(j, prev_outs):
        def body(cur_ref, prev_ref, w_ref, dout_ref, *rest):
            dc_ref, dw_ref = rest[-2:]
            i = pl.program_id(0)
            prev = prev_ref[...] * (i > 0).astype(F32)
            taps = _conv_taps(cur_ref[...], prev)
            w = w_ref[...]
            c = sum(taps[k] * w[k:k + 1, :] for k in range(4))
            _, vjp = jax.vjp(lambda cc: _qkv_post(cc, j, H), c)
            (dc,) = vjp(dout_ref[...])
            dc_ref[...] = dc

            @pl.when(i == 0)
            def _():
                dw_ref[...] = jnp.zeros_like(dw_ref)

            for k in range(4):
                dw_ref[k:k + 1, :] += jnp.sum(dc * taps[k], axis=0, keepdims=True)

        specs, operands, aliases = _alias_previous(prev_outs, 4)
        return pl.pallas_call(
            body, name="%s%d" % (name, j), grid=(T // tm,),
            in_specs=[pl.BlockSpec((tm, D), lambda i: (i, j)),
                      pl.BlockSpec((8, D), lambda i: (jnp.maximum(i * (tm // 8) - 1, 0), j)),
                      pl.BlockSpec((4, D), lambda i: (0, j)),
                      pl.BlockSpec((tm, D), lambda i: (i, j))] + specs,
            out_specs=[pl.BlockSpec((tm, D), lambda i: (i, j)), pl.BlockSpec((4, D), lambda i: (0, j))],
            out_shape=[jax.ShapeDtypeStruct((T, 3 * D), F32), jax.ShapeDtypeStruct((4, 3 * D), F32)],
            input_output_aliases=aliases, compiler_params=_params(1),
        )(p, p, conv_w, dqkv, *operands)

    return _column_calls(columns)


def _conv_bwd_input(dc, conv_w, name):
    T, D3 = dc.shape
    D = D3 // 3
    tm = _pick(T, (256, 128, 64))
    n_t = T // tm

    def body(cur_ref, next_ref, w_ref, dp_ref):
        i = pl.program_id(0)
        cur = cur_ref[...]
        nxt = next_ref[...] * (i < n_t - 1).astype(F32)
        ext = jnp.concatenate([cur, nxt], axis=0)
        w = w_ref[...]
        acc = cur * w[3:4, :]
        for s in (1, 2, 3):
            acc = acc + pltpu.roll(ext, tm + 8 - s, 0)[0:tm] * w[3 - s:4 - s, :]
        dp_ref[...] = acc.astype(BF16)

    return pl.pallas_call(
        body, name=name, grid=(n_t, 3),
        in_specs=[pl.BlockSpec((tm, D), lambda i, j: (i, j)),
                  pl.BlockSpec((8, D), lambda i, j: (jnp.minimum((i + 1) * (tm // 8), T // 8 - 1), j)),
                  pl.BlockSpec((4, D), lambda i, j: (0, j))],
        out_specs=pl.BlockSpec((tm, D), lambda i, j: (i, j)),
        out_shape=jax.ShapeDtypeStruct((T, D3), BF16), compiler_params=_params(2),
    )(dc, dc, conv_w)


def _comm_call(body, arrays, out_shape, n_remote, n_local, name):
    any_spec = pl.BlockSpec(memory_space=pl.ANY)
    return pl.pallas_call(
        body, name=name, in_specs=[any_spec] * len(arrays), out_specs=[any_spec] * len(out_shape), out_shape=out_shape,
        scratch_shapes=[pltpu.SemaphoreType.DMA((n_remote,)), pltpu.SemaphoreType.DMA((n_remote,)),
                        pltpu.SemaphoreType.DMA((n_local,))],
        compiler_params=pltpu.CompilerParams(has_side_effects=True),
    )(*arrays)


def _gather_shards(arrays, name):
    n = len(arrays)

    def body(*refs):
        ins, outs = refs[:n], refs[n:2 * n]
        send_sems, recv_sems, local_sems = refs[2 * n:]
        x, y, c = lax.axis_index("x"), lax.axis_index("y"), lax.axis_index("c")
        me, sibling = (x, y, c), (x, y, 1 - c)
        chips = [(1 - x, y), (x, 1 - y), (1 - x, 1 - y)]

        def copy(a, k, block, to, src=None):
            dst = outs[a].at[4 * block[0] + 2 * block[1] + block[2]]
            return pltpu.make_async_remote_copy(
                src_ref=dst if src is None else src, dst_ref=dst, send_sem=send_sems.at[a * 7 + k],
                recv_sem=recv_sems.at[a * 7 + k], device_id=to, device_id_type=MESH_ID)

        local = [pltpu.make_async_copy(ins[a], outs[a].at[4 * x + 2 * y + c], local_sems.at[a]) for a in range(n)]
        first = [copy(a, 0, me, sibling, src=ins[a]) for a in range(n)]
        first += [copy(a, 1 + j, me, (*chip, c), src=ins[a]) for j, chip in enumerate(chips) for a in range(n)]
        for cp in local + first:
            cp.start()
        passed = []
        for j, chip in enumerate(chips):
            for a in range(n):
                copy(a, 1 + j, (*chip, c), me).wait_recv()
                passed.append(copy(a, 4 + j, (*chip, c), sibling))
                passed[-1].start()
        for a in range(n):
            copy(a, 0, sibling, me).wait_recv()
            for j, chip in enumerate(chips):
                copy(a, 4 + j, (*chip, 1 - c), me).wait_recv()
        for cp in first + passed:
            cp.wait_send()
        for cp in local:
            cp.wait()

    out_shape = [jax.ShapeDtypeStruct((N_DEV,) + a.shape, a.dtype) for a in arrays]
    return _comm_call(body, arrays, out_shape, 7 * n, n, name)


def _pair_exchange(arrays, name):
    n = len(arrays)

    def body(*refs):
        ins, pair = refs[:n], refs[n:2 * n]
        send_sems, recv_sems, _ = refs[2 * n:]
        x, y, c = lax.axis_index("x"), lax.axis_index("y"), lax.axis_index("c")
        sends = [pltpu.make_async_remote_copy(
            src_ref=ins[a].at[1 - c], dst_ref=pair[a], send_sem=send_sems.at[a], recv_sem=recv_sems.at[a],
            device_id=(x, y, 1 - c), device_id_type=MESH_ID) for a in range(n)]
        for cp in sends:
            cp.start()
        for cp in sends:
            cp.wait_recv()
        for cp in sends:
            cp.wait_send()

    out_shape = [jax.ShapeDtypeStruct(a.shape[1:], a.dtype) for a in arrays]
    return _comm_call(body, arrays, out_shape, n, 1, name)


def _pair_add(own, pair, name, out_dtype):
    _, R, C = own.shape
    tr = next((t for t in (256, 128, 64, 32, 16) if R % t == 0), R)

    def body(a_ref, b_ref, o_ref):
        o_ref[...] = (a_ref[...] + b_ref[...]).astype(out_dtype)

    spec = pl.BlockSpec((1, tr, C), lambda q, i: (q, i, 0))
    return pl.pallas_call(body, name=name, grid=(4, R // tr), in_specs=[spec, spec], out_specs=spec,
                          out_shape=jax.ShapeDtypeStruct(own.shape, out_dtype), compiler_params=_params(2))(own, pair)


def _chip_exchange(arrays, name):
    n = len(arrays)

    def body(*refs):
        ins, outs = refs[:n], refs[n:2 * n]
        send_sems, recv_sems, local_sems = refs[2 * n:]
        x, y, c = lax.axis_index("x"), lax.axis_index("y"), lax.axis_index("c")
        my_chip = 2 * x + y
        chips = [(1 - x, y), (x, 1 - y), (1 - x, 1 - y)]
        local = [pltpu.make_async_copy(ins[a].at[my_chip], outs[a].at[my_chip], local_sems.at[a]) for a in range(n)]
        sends = [pltpu.make_async_remote_copy(
            src_ref=ins[a].at[2 * px + py], dst_ref=outs[a].at[my_chip], send_sem=send_sems.at[a * 3 + j],
            recv_sem=recv_sems.at[a * 3 + j], device_id=(px, py, c), device_id_type=MESH_ID)
            for j, (px, py) in enumerate(chips) for a in range(n)]
        arrivals = [pltpu.make_async_remote_copy(
            src_ref=ins[a].at[my_chip], dst_ref=outs[a].at[2 * px + py], send_sem=send_sems.at[a * 3 + j],
            recv_sem=recv_sems.at[a * 3 + j], device_id=(px, py, c), device_id_type=MESH_ID)
            for j, (px, py) in enumerate(chips) for a in range(n)]
        for cp in local + sends:
            cp.start()
        for cp in arrivals:
            cp.wait_recv()
        for cp in sends:
            cp.wait_send()
        for cp in local:
            cp.wait()

    out_shape = [jax.ShapeDtypeStruct(a.shape, a.dtype) for a in arrays]
    return _comm_call(body, arrays, out_shape, 3 * n, n, name)


def _reduce_adamw(recv, w, m, v, name):
    S, R, C = recv.shape
    tr = next((t for t in (256, 128, 64, 32, 16, 8) if R % t == 0), R)
    c1 = 1.0 - ADAM_B1 ** ADAM_STEP
    c2 = 1.0 - ADAM_B2 ** ADAM_STEP

    def body(r_ref, w_ref, m_ref, v_ref, g_ref, d_ref, nm_ref, nv_ref):
        g = r_ref[0].astype(F32)
        for s in range(1, S):
            g = g + r_ref[s].astype(F32)
        nm = ADAM_B1 * m_ref[...] + (1.0 - ADAM_B1) * g
        nv = ADAM_B2 * v_ref[...] + (1.0 - ADAM_B2) * (g * g)
        g_ref[...] = g
        nm_ref[...] = nm
        nv_ref[...] = nv
        d_ref[...] = -ADAM_LR * ((nm / c1) / (jnp.sqrt(nv / c2) + ADAM_EPS) + ADAM_WD * w_ref[...])

    spec = pl.BlockSpec((tr, C), lambda i: (i, 0))
    return pl.pallas_call(
        body, name=name, grid=(R // tr,),
        in_specs=[pl.BlockSpec((S, tr, C), lambda i: (0, i, 0)), spec, spec, spec], out_specs=[spec] * 4,
        out_shape=[jax.ShapeDtypeStruct((R, C), F32)] * 4, compiler_params=_params(1),
    )(recv, w, m, v)


def _forward_local(x, target, nw0, nw1, fw, wa_in, conv_w, gate_params, o_norm, wa_out, wb_in, wb_out):
    T, D = x.shape
    nD = D // LANES
    sv = {}
    sv["u0"] = _rmsnorm_fwd(x, nw0, "a_norm_fwd")
    sv["pa"] = _mm_nn(sv["u0"], wa_in, "a_in_proj")
    sv["qkv_a"] = _conv_fwd(sv["pa"], conv_w, D, "a_conv_fwd")
    sv["beta"], sv["gc"] = _gates_fwd(sv["pa"], gate_params, 4 * nD, "a_gates_fwd")
    sv["o_a"], sv["s_all"], sv["t_all"] = _chunk_fwd(sv["qkv_a"], sv["beta"], sv["gc"], D, "a_chunk_fwd")
    sv["y_a"] = _onorm_gate_fwd(sv["o_a"], sv["pa"], 3, o_norm, "a_onorm_fwd")
    sv["h1"] = _mm_nn(sv["y_a"], wa_out, "a_out_proj", add=x)
    sv["u1"] = _rmsnorm_fwd(sv["h1"], nw1, "b_norm_fwd")
    sv["qkv_b"] = _mm_nn(sv["u1"], wb_in[:, :3 * D], "b_in_proj_qkv", out_dtype=BF16)
    sv["gate_b"] = _mm_nn(sv["u1"], wb_in[:, 3 * D:], "b_in_proj_gate")
    sv["o_b"], sv["r_b"] = _sb_fwd(sv["qkv_b"], D, "b_attn_fwd")
    sv["y_b"] = _gate_mul_fwd(sv["o_b"], sv["gate_b"], "b_gate_fwd")
    sv["h2"] = _mm_nn(sv["y_b"], wb_out, "b_out_proj", add=sv["h1"])
    sv["dh2"], sv["loss"], sv["dfw"] = _final_loss(sv["h2"], fw, target, "final_loss")
    return sv


def _backward_local(sv, x, nw0, nw1, wa_in, conv_w, gate_params, o_norm, wa_out, wb_in, wb_out):
    T, D = x.shape
    nD = D // LANES
    g = {}
    dh2 = sv["dh2"]
    g["wb_out"] = _mm_tn(sv["y_b"], dh2, "b_out_proj_dw")
    dy_b = _mm_nt([(dh2, wb_out)], "b_out_proj_dx")
    do_b, dgate_b = _gate_mul_bwd(dy_b, sv["o_b"], sv["gate_b"], "b_gate_bwd")
    dq_b, dk_b, dv_b = _sb_bwd(sv["qkv_b"], do_b, sv["r_b"], D, "b_attn_bwd")
    dp_b = [dq_b, dk_b, dv_b, dgate_b]
    g["wb_in"] = jnp.concatenate([_mm_tn(sv["u1"], dp, "b_in_proj_dw%d" % c) for c, dp in enumerate(dp_b)], axis=1)
    du1 = _mm_nt([(dp, wb_in[:, c * D:(c + 1) * D]) for c, dp in enumerate(dp_b)], "b_in_proj_dx")
    dh1, g["nw1"] = _rmsnorm_bwd(sv["h1"], nw1, du1, dh2, "b_norm_bwd")
    g["wa_out"] = _mm_tn(sv["y_a"], dh1, "a_out_proj_dw")
    dy_a = _mm_nt([(dh1, wa_out)], "a_out_proj_dx")
    do_a, dz_a, g["o_norm"] = _onorm_gate_bwd(dy_a, sv["o_a"], sv["pa"], 3, o_norm, "a_onorm_bwd")
    dqkv_a, dbeta, dg = _chunk_bwd(sv["qkv_a"], sv["beta"], sv["gc"], sv["s_all"], sv["t_all"], do_a, D, "a_chunk_bwd")
    dp_gates, g["gate_params"] = _gates_bwd(sv["pa"], gate_params, 4 * nD, dbeta, dg, "a_gates_bwd")
    dc, g["conv_w"] = _conv_bwd_act(sv["pa"], conv_w, dqkv_a, D, "a_conv_bwd_act")
    dp_qkv = _conv_bwd_input(dc, conv_w, "a_conv_bwd_input")
    dp_a = [(dp_qkv, 0, 3 * D), (dz_a, 3 * D, 4 * D), (dp_gates, 4 * D, 4 * D + 2 * LANES)]
    g["wa_in"] = jnp.concatenate([_mm_tn(sv["u0"], dp, "a_in_proj_dw%d" % c) for c, (dp, _, _) in enumerate(dp_a)], axis=1)
    du0 = _mm_nt([(dp, wa_in[:, lo:hi]) for dp, lo, hi in dp_a], "a_in_proj_dx")
    g["x"], g["nw0"] = _rmsnorm_bwd(x, nw0, du0, dh1, "a_norm_bwd")
    g["fw"] = sv["dfw"]
    return g


def kernel(x, norm_w, a_w_in, a_conv_w, a_a_log, a_dt_bias, a_o_norm, a_w_out, b_w_in, b_w_out, final_norm_w, loss_target, m_norm_w, m_a_w_in, m_a_conv_w, m_a_a_log, m_a_dt_bias, m_a_o_norm, m_a_w_out, m_b_w_in, m_b_w_out, m_final_norm_w, v_norm_w, v_a_w_in, v_a_conv_w, v_a_a_log, v_a_dt_bias, v_a_o_norm, v_a_w_out, v_b_w_in, v_b_w_out, v_final_norm_w):
    D = x.shape[-1]
    H = D // HEAD_DIM
    shards = [a_w_in[0].astype(BF16), a_w_out[0].astype(BF16), b_w_in[0].astype(BF16), b_w_out[0].astype(BF16), a_conv_w[0]]
    ga_in, ga_out, gb_in, gb_out, g_conv = _gather_shards(shards, "weights_gather")
    wa = ga_in.transpose(1, 0, 2).reshape(D, -1)
    pad = lambda w: jnp.pad(w, ((0, 0), (0, LANES - w.shape[1])))
    wa_in = jnp.concatenate([wa[:, :4 * D], pad(wa[:, 4 * D:4 * D + H]), pad(wa[:, 4 * D + H:])], axis=1)
    wa_out = ga_out.reshape(D, D)
    wb_in = gb_in.transpose(1, 0, 2).reshape(D, 4 * D)
    wb_out = gb_out.reshape(D, D)
    conv_w = g_conv.transpose(1, 0, 2).reshape(4, 3 * D)
    gate_params = jnp.zeros((8, LANES), F32).at[0, :H].set(a_a_log[0]).at[1, :H].set(a_dt_bias[0])
    nw0, nw1, fw = norm_w[0:1], norm_w[1:2], final_norm_w[None]

    sv = _forward_local(x[0], loss_target[0], nw0, nw1, fw, wa_in, conv_w, gate_params, a_o_norm, wa_out, wb_in, wb_out)
    g = _backward_local(sv, x[0], nw0, nw1, wa_in, conv_w, gate_params, a_o_norm, wa_out, wb_in, wb_out)

    gwa = g["wa_in"]
    gwa = jnp.concatenate([gwa[:, :4 * D], gwa[:, 4 * D:4 * D + H], gwa[:, 4 * D + LANES:4 * D + LANES + H]], axis=1)
    row = lambda v: jnp.pad(v.reshape(1, -1), ((0, 0), (0, D - v.size)))
    small = jnp.concatenate([g["nw0"][0:1], g["nw1"][0:1], g["fw"][0:1], row(g["gate_params"][0, :H]),
                             row(g["gate_params"][1, :H]), row(g["o_norm"][0]), row(sv["loss"][0, 0:1]),
                             jnp.zeros((1, D), F32)], axis=0)
    cols = lambda a: a.reshape(a.shape[0], 4, 2, -1).transpose(2, 1, 0, 3)
    rows = lambda a: a.reshape(4, 2, -1, a.shape[1]).transpose(1, 0, 2, 3)
    contribs = [cols(gwa), rows(g["wa_out"]), cols(g["wb_in"]), rows(g["wb_out"]), cols(g["conv_w"]),
                jnp.broadcast_to(small[None, None], (2, 4, 8, D))]
    pair = _pair_exchange(contribs, "grads_pair_exchange")
    my_core = lax.axis_index("c")
    own = [lax.dynamic_index_in_dim(a, my_core, axis=0, keepdims=False) for a in contribs]
    partial = [_pair_add(o, p, "grads_pair_add%d" % k, BF16 if k < len(own) - 1 else F32)
               for k, (o, p) in enumerate(zip(own, pair))]
    ra_in, ra_out, rb_in, rb_out, r_conv, r_small = _chip_exchange(partial, "grads_chip_exchange")

    outs = {}
    for nm, recv, w, m, v in (("a_w_in", ra_in, a_w_in, m_a_w_in, v_a_w_in), ("a_w_out", ra_out, a_w_out, m_a_w_out, v_a_w_out),
                              ("b_w_in", rb_in, b_w_in, m_b_w_in, v_b_w_in), ("b_w_out", rb_out, b_w_out, m_b_w_out, v_b_w_out),
                              ("a_conv_w", r_conv, a_conv_w, m_a_conv_w, v_a_conv_w)):
        outs[nm] = tuple(o[None] for o in _reduce_adamw(recv, w[0], m[0], v[0], "adamw_" + nm))

    def pack(nw, alog, dt, onorm, fnw):
        return jnp.concatenate([nw, fnw.reshape(1, D), row(alog), row(dt), row(onorm), jnp.zeros((2, D), F32)], axis=0)

    s_g, s_d, s_m, s_v = _reduce_adamw(
        r_small, pack(norm_w, a_a_log, a_dt_bias, a_o_norm, final_norm_w),
        pack(m_norm_w, m_a_a_log, m_a_dt_bias, m_a_o_norm, m_final_norm_w),
        pack(v_norm_w, v_a_a_log, v_a_dt_bias, v_a_o_norm, v_final_norm_w), "adamw_small")
    loss = s_g[6, 0]
    for i, s in enumerate((s_g, s_d, s_m, s_v)):
        outs.setdefault("norm_w", [None] * 4)[i] = s[0:2]
        outs.setdefault("final_norm_w", [None] * 4)[i] = s[2]
        outs.setdefault("a_a_log", [None] * 4)[i] = s[3:4, :H]
        outs.setdefault("a_dt_bias", [None] * 4)[i] = s[4:5, :H]
        outs.setdefault("a_o_norm", [None] * 4)[i] = s[5:6, :HEAD_DIM]
    names = ("norm_w", "a_w_in", "a_conv_w", "a_a_log", "a_dt_bias", "a_o_norm", "a_w_out", "b_w_in", "b_w_out", "final_norm_w")
    return (loss, g["x"][None]) + tuple(outs[n][i] for i in range(4) for n in names)
```

```python
import functools

import jax
import jax.numpy as jnp
from jax import lax
from jax.experimental import pallas as pl
from jax.experimental.pallas import tpu as pltpu

F32 = jnp.float32
BF16 = jnp.bfloat16
EPS = 1e-6
LOG2_E = 1.4426950408889634
MASKED_SCORE = -1e30
HEAD_DIM = 128
CHUNK = 64
CHUNKS_PER_STEP = 4
ATTN_Q_BLOCKS_FWD = (512, 256)
ATTN_Q_BLOCKS_BWD = (512, 256)
ATTN_K_BLOCK = 128
LANES = 128
N_DEV = 8
VMEM_LIMIT_BYTES = 48 * 1024 * 1024
ADAM_LR, ADAM_B1, ADAM_B2, ADAM_EPS, ADAM_WD, ADAM_STEP = 0.001, 0.9, 0.999, 1e-08, 0.01, 10
MESH_ID = pl.DeviceIdType.MESH


def _pick(n, candidates):
    for c in candidates:
        if n % c == 0:
            return c
    raise ValueError(f"no tile for {n} in {candidates}")


def _params(n_grid_axes):
    return pltpu.CompilerParams(dimension_semantics=("arbitrary",) * n_grid_axes, vmem_limit_bytes=VMEM_LIMIT_BYTES)


def _dot(a, b):
    return jnp.dot(a.astype(BF16), b.astype(BF16), preferred_element_type=F32)


def _dot_nt(a, b):
    return lax.dot_general(a.astype(BF16), b.astype(BF16), (((1,), (1,)), ((), ())), preferred_element_type=F32)


def _dot_tn(a, b):
    return lax.dot_general(a.astype(BF16), b.astype(BF16), (((0,), (0,)), ((), ())), preferred_element_type=F32)


def _split2(x):
    hi = x.astype(BF16)
    lo = (x - hi.astype(F32)).astype(BF16)
    return hi, lo


def _split3(x):
    hi = x.astype(BF16)
    r = x - hi.astype(F32)
    mid = r.astype(BF16)
    lo = (r - mid.astype(F32)).astype(BF16)
    return hi, mid, lo


def _dot3(a, b):
    a_hi, a_lo = _split2(a)
    b_hi, b_lo = _split2(b)
    d = functools.partial(jnp.dot, preferred_element_type=F32)
    return d(a_hi, b_hi) + (d(a_hi, b_lo) + d(a_lo, b_hi))


def _silu(x):
    return x * jax.nn.sigmoid(x)


def _softplus(x):
    return jnp.maximum(x, 0.0) + jnp.log1p(jnp.exp(-jnp.abs(x)))


def _iota2(shape, axis):
    return lax.broadcasted_iota(jnp.int32, shape, axis)


def _rms_bwd_math(x, w, dy):
    r = lax.rsqrt(jnp.mean(x * x, axis=-1, keepdims=True) + EPS)
    xhat = x * r
    dxhat = dy * w
    dx = r * (dxhat - xhat * jnp.mean(dxhat * xhat, axis=-1, keepdims=True))
    dw = jnp.sum(dy * xhat, axis=0, keepdims=True)
    return dx, dw


def _rmsnorm_fwd(x, w, name):
    T, D = x.shape
    tm = _pick(T, (512, 256, 128))

    def body(x_ref, w_ref, o_ref):
        xf = x_ref[...]
        r = lax.rsqrt(jnp.mean(xf * xf, axis=-1, keepdims=True) + EPS)
        o_ref[...] = (xf * r * w_ref[...]).astype(BF16)

    return pl.pallas_call(
        body, name=name, grid=(T // tm,),
        in_specs=[pl.BlockSpec((tm, D), lambda i: (i, 0)), pl.BlockSpec((1, D), lambda i: (0, 0))],
        out_specs=pl.BlockSpec((tm, D), lambda i: (i, 0)),
        out_shape=jax.ShapeDtypeStruct((T, D), BF16), compiler_params=_params(1),
    )(x, w)


def _rmsnorm_bwd(x, w, du, dres, name):
    T, D = x.shape
    tm = _pick(T, (512, 256, 128))

    def body(x_ref, w_ref, du_ref, dres_ref, dx_ref, dw_ref):
        dx, dw = _rms_bwd_math(x_ref[...], w_ref[...], du_ref[...].astype(F32))
        dx_ref[...] = dres_ref[...] + dx

        @pl.when(pl.program_id(0) == 0)
        def _():
            dw_ref[...] = jnp.zeros_like(dw_ref)

        dw_ref[...] += jnp.broadcast_to(dw, dw_ref.shape)

    return pl.pallas_call(
        body, name=name, grid=(T // tm,),
        in_specs=[pl.BlockSpec((tm, D), lambda i: (i, 0)), pl.BlockSpec((1, D), lambda i: (0, 0)),
                  pl.BlockSpec((tm, D), lambda i: (i, 0)), pl.BlockSpec((tm, D), lambda i: (i, 0))],
        out_specs=[pl.BlockSpec((tm, D), lambda i: (i, 0)), pl.BlockSpec((8, D), lambda i: (0, 0))],
        out_shape=[jax.ShapeDtypeStruct((T, D), F32), jax.ShapeDtypeStruct((8, D), F32)],
        compiler_params=_params(1),
    )(x, w, du, dres)


def _mm_nn(a, b, name, add=None, out_dtype=F32):
    M, K = a.shape
    _, N = b.shape
    tm = _pick(M, (256, 128)) if N > 2048 else _pick(M, (512, 256, 128))

    def body(*refs):
        a_ref, b_ref = refs[0], refs[1]
        o_ref = refs[-1]
        acc = _dot(a_ref[...], b_ref[...])
        if add is not None:
            acc = acc + refs[2][...]
        o_ref[...] = acc.astype(out_dtype)

    in_specs = [pl.BlockSpec((tm, K), lambda i: (i, 0)), pl.BlockSpec((K, N), lambda i: (0, 0))]
    args = [a, b]
    if add is not None:
        in_specs.append(pl.BlockSpec((tm, N), lambda i: (i, 0)))
        args.append(add)
    return pl.pallas_call(
        body, name=name, grid=(M // tm,), in_specs=in_specs,
        out_specs=pl.BlockSpec((tm, N), lambda i: (i, 0)),
        out_shape=jax.ShapeDtypeStruct((M, N), out_dtype), compiler_params=_params(1),
    )(*args)


def _mm_nt(pairs, name):
    M = pairs[0][0].shape[0]
    N = pairs[0][1].shape[0]
    n = len(pairs)
    tm = _pick(M, (512, 256, 128))

    def body(*refs):
        acc = _dot_nt(refs[0][...], refs[n][...])
        for p in range(1, n):
            acc = acc + _dot_nt(refs[p][...], refs[n + p][...])
        refs[-1][...] = acc

    in_specs = ([pl.BlockSpec((tm, a.shape[1]), lambda i: (i, 0)) for a, _ in pairs]
                + [pl.BlockSpec(b.shape, lambda i: (0, 0)) for _, b in pairs])
    return pl.pallas_call(
        body, name=name, grid=(M // tm,), in_specs=in_specs,
        out_specs=pl.BlockSpec((tm, N), lambda i: (i, 0)),
        out_shape=jax.ShapeDtypeStruct((M, N), F32), compiler_params=_params(1),
    )(*[a for a, _ in pairs], *[b for _, b in pairs])


def _mm_tn(a, b, name):
    R, M = a.shape
    _, N = b.shape
    tn = _pick(N, (1536, 1024, 512, 256, 128))
    tr = _pick(R, (512, 256, 128))

    def body(a_ref, b_ref, o_ref):
        @pl.when(pl.program_id(1) == 0)
        def _():
            o_ref[...] = jnp.zeros_like(o_ref)

        o_ref[...] += _dot_tn(a_ref[...], b_ref[...])

    return pl.pallas_call(
        body, name=name, grid=(N // tn, R // tr),
        in_specs=[pl.BlockSpec((tr, M), lambda j, r: (r, 0)), pl.BlockSpec((tr, tn), lambda j, r: (r, j))],
        out_specs=pl.BlockSpec((M, tn), lambda j, r: (0, j)),
        out_shape=jax.ShapeDtypeStruct((M, N), F32), compiler_params=_params(2),
    )(a, b)


def _qkv_post(c, j, n_heads):
    s = _silu(c)
    if j == 2:
        return s
    parts = []
    for h in range(n_heads):
        sh = s[:, h * HEAD_DIM:(h + 1) * HEAD_DIM]
        parts.append(sh * lax.rsqrt(jnp.sum(sh * sh, axis=-1, keepdims=True) + EPS))
    n = jnp.concatenate(parts, axis=-1)
    return n * (HEAD_DIM ** -0.5) if j == 0 else n


def _column_calls(make_call, n_cols=3):
    outs = None
    for j in range(n_cols):
        outs = make_call(j, outs)
    return outs


def _alias_previous(prev, n_inputs):
    if prev is None:
        return [], [], {}
    prev = list(prev) if isinstance(prev, (list, tuple)) else [prev]
    return ([pl.BlockSpec(memory_space=pl.ANY)] * len(prev), prev, {n_inputs + k: k for k in range(len(prev))})


def _conv_taps(cur, halo_prev):
    tm = cur.shape[0]
    ext = jnp.concatenate([halo_prev, cur], axis=0)
    taps = [pltpu.roll(ext, s, 0)[8:8 + tm] for s in (3, 2, 1)]
    return taps + [cur]


def _conv_fwd(p, conv_w, d_model, name):
    T = p.shape[0]
    D = d_model
    H = D // HEAD_DIM
    tm = _pick(T, (256, 128, 64))

    def columns(j, prev_out):
        def body(cur_ref, prev_ref, w_ref, *rest):
            o_ref = rest[-1]
            prev = prev_ref[...] * (pl.program_id(0) > 0).astype(F32)
            taps = _conv_taps(cur_ref[...], prev)
            w = w_ref[...]
            c = sum(taps[k] * w[k:k + 1, :] for k in range(4))
            o_ref[...] = _qkv_post(c, j, H)

        specs, operands, aliases = _alias_previous(prev_out, 3)
        return pl.pallas_call(
            body, name="%s%d" % (name, j), grid=(T // tm,),
            in_specs=[pl.BlockSpec((tm, D), lambda i: (i, j)),
                      pl.BlockSpec((8, D), lambda i: (jnp.maximum(i * (tm // 8) - 1, 0), j)),
                      pl.BlockSpec((4, D), lambda i: (0, j))] + specs,
            out_specs=pl.BlockSpec((tm, D), lambda i: (i, j)),
            out_shape=jax.ShapeDtypeStruct((T, 3 * D), F32), input_output_aliases=aliases, compiler_params=_params(1),
        )(p, p, conv_w, *operands)

    return _column_calls(columns)


def _chunk_tri(tm, upper):
    r, c = _iota2((tm, tm), 0), _iota2((tm, tm), 1)
    same = (r // CHUNK) == (c // CHUNK)
    tri = (c >= r) if upper else (c <= r)
    return jnp.where(same & tri, 1.0, 0.0).astype(BF16)


def _dot_mask(mask_bf16, x):
    hi, mid, lo = _split3(x)
    d = functools.partial(jnp.dot, preferred_element_type=F32)
    return d(mask_bf16, hi) + (d(mask_bf16, mid) + d(mask_bf16, lo))


def _gates_math(pb, pa, a_log, dt_bias):
    beta = jax.nn.sigmoid(pb)
    g = -jnp.exp(a_log) * _softplus(pa + dt_bias)
    return beta, g


def _gates_fwd(p, gate_params, col0, name):
    T = p.shape[0]
    tm = _pick(T, (256, 128, 64))

    def body(pb_ref, pa_ref, gp_ref, beta_ref, gc_ref):
        gp = gp_ref[...]
        beta, g = _gates_math(pb_ref[...], pa_ref[...], gp[0:1, :], gp[1:2, :])
        beta_ref[...] = beta
        gc_ref[...] = _dot_mask(_chunk_tri(tm, upper=False), g)

    return pl.pallas_call(
        body, name=name, grid=(T // tm,),
        in_specs=[pl.BlockSpec((tm, LANES), lambda i: (i, col0)), pl.BlockSpec((tm, LANES), lambda i: (i, col0 + 1)),
                  pl.BlockSpec((8, LANES), lambda i: (0, 0))],
        out_specs=[pl.BlockSpec((tm, LANES), lambda i: (i, 0))] * 2,
        out_shape=[jax.ShapeDtypeStruct((T, LANES), F32)] * 2, compiler_params=_params(1),
    )(p, p, gate_params)


def _col_to_row(col):
    C = col.shape[0]
    eye = _iota2((C, C), 0) == _iota2((C, C), 1)
    return jnp.sum(jnp.where(eye, col, 0.0), axis=0, keepdims=True)


def _lockstep(generators):
    generators = list(generators)
    results = [None] * len(generators)
    live = list(range(len(generators)))
    while live:
        for idx in list(live):
            try:
                next(generators[idx])
            except StopIteration as done:
                results[idx] = done.value
                live.remove(idx)
    return results


def _unit_lower_inverse(low):
    C = low.shape[0]
    eye = (_iota2((C, C), 0) == _iota2((C, C), 1)).astype(F32)
    t = eye - low
    p = _dot3(low, low)
    yield
    n = 2
    while True:
        tp = _dot3(t, p)
        n *= 2
        if n < C:
            p = _dot3(p, p)
        yield
        t = t + tp
        if n >= C:
            return t


def _chunk_head_fwd(q, k, v, gc, beta, s_in):
    C = q.shape[0]
    r, c = _iota2((C, C), 0), _iota2((C, C), 1)
    causal, strict = r >= c, r > c
    decay = jnp.where(causal, jnp.exp(jnp.where(causal, gc - _col_to_row(gc), 0.0)), 0.0)
    kb, vb = k * beta, v * beta
    eg = jnp.exp(gc)
    kk = _dot_nt(kb, k)
    qk = _dot_nt(q, k)
    o_state = _dot(q * eg, s_in)
    yield
    t_inv = yield from _unit_lower_inverse(jnp.where(strict, kk * decay, 0.0))
    u = _dot(t_inv, vb)
    w = _dot(t_inv, kb * eg)
    yield
    w_state = _dot(w, s_in)
    yield
    v_new = u - w_state
    g_last = gc[C - 1:C, :]
    o_intra = _dot(qk * decay, v_new)
    s_add = _dot_tn(k * jnp.exp(g_last - gc), v_new)
    yield
    return o_state + o_intra, s_in * jnp.exp(g_last) + s_add, t_inv


def _chunk_fwd(qkv, beta, gc, d_model, name):
    T = qkv.shape[0]
    D = d_model
    H = D // HEAD_DIM
    N = T // CHUNK
    G = CHUNKS_PER_STEP
    R = G * CHUNK

    def body(q_ref, k_ref, v_ref, beta_ref, gc_ref, o_ref, s_all_ref, t_all_ref, s_ref):
        @pl.when(pl.program_id(0) == 0)
        def _():
            s_ref[...] = jnp.zeros_like(s_ref)

        heads = [slice(h * HEAD_DIM, (h + 1) * HEAD_DIM) for h in range(H)]
        states = [s_ref[h] for h in range(H)]
        for sub in range(G):
            rows = slice(sub * CHUNK, (sub + 1) * CHUNK)
            for h in range(H):
                s_all_ref[sub, h] = states[h]
            results = _lockstep(_chunk_head_fwd(q_ref[rows, hs], k_ref[rows, hs], v_ref[rows, hs], gc_ref[rows, h:h + 1],
                                                beta_ref[rows, h:h + 1], states[h]) for h, hs in enumerate(heads))
            for h, (o, s_out, t_inv) in enumerate(results):
                o_ref[rows, heads[h]] = o
                states[h] = s_out
                t_all_ref[sub, h] = t_inv
        for h in range(H):
            s_ref[h] = states[h]

    return pl.pallas_call(
        body, name=name, grid=(N // G,),
        in_specs=[pl.BlockSpec((R, D), lambda n: (n, 0)), pl.BlockSpec((R, D), lambda n: (n, 1)),
                  pl.BlockSpec((R, D), lambda n: (n, 2)),
                  pl.BlockSpec((R, LANES), lambda n: (n, 0)), pl.BlockSpec((R, LANES), lambda n: (n, 0))],
        out_specs=[pl.BlockSpec((R, D), lambda n: (n, 0)),
                   pl.BlockSpec((G, H, HEAD_DIM, HEAD_DIM), lambda n: (n, 0, 0, 0)),
                   pl.BlockSpec((G, H, CHUNK, CHUNK), lambda n: (n, 0, 0, 0))],
        out_shape=[jax.ShapeDtypeStruct((T, D), F32), jax.ShapeDtypeStruct((N, H, HEAD_DIM, HEAD_DIM), F32),
                   jax.ShapeDtypeStruct((N, H, CHUNK, CHUNK), F32)],
        scratch_shapes=[pltpu.VMEM((H, HEAD_DIM, HEAD_DIM), F32)], compiler_params=_params(1),
    )(qkv, qkv, qkv, beta, gc)


def _onorm_gate_math(o, z, w, n_heads):
    parts = []
    for h in range(n_heads):
        hs = slice(h * HEAD_DIM, (h + 1) * HEAD_DIM)
        oh = o[:, hs]
        y = oh * lax.rsqrt(jnp.mean(oh * oh, axis=-1, keepdims=True) + EPS) * w
        parts.append(y * _silu(z[:, hs]))
    return jnp.concatenate(parts, axis=-1)


def _onorm_gate_fwd(o, p, z_col, o_norm, name):
    T, D = o.shape
    H = D // HEAD_DIM
    tm = _pick(T, (256, 128, 64))

    def body(o_ref, z_ref, w_ref, y_ref):
        y_ref[...] = _onorm_gate_math(o_ref[...], z_ref[...], w_ref[...], H).astype(BF16)

    return pl.pallas_call(
        body, name=name, grid=(T // tm,),
        in_specs=[pl.BlockSpec((tm, D), lambda i: (i, 0)), pl.BlockSpec((tm, D), lambda i: (i, z_col)),
                  pl.BlockSpec((1, HEAD_DIM), lambda i: (0, 0))],
        out_specs=pl.BlockSpec((tm, D), lambda i: (i, 0)),
        out_shape=jax.ShapeDtypeStruct((T, D), BF16), compiler_params=_params(1),
    )(o, p, o_norm)


def _diag_mask(qb, kb, d):
    return _iota2((qb, kb), 0) > _iota2((qb, kb), 1) + d * kb


def _run_trips(trip, n, state):
    def six(j, st):
        for u in range(6):
            st = trip(6 * j + u, st, u)
        return st

    state = lax.fori_loop(0, n // 6, six, state)
    base = (n // 6) * 6
    for u in (0, 2):
        pair = lambda st, u=u: trip(base + u + 1, trip(base + u, st, u), u + 1)
        state = lax.cond(n - base > u, pair, lambda st: st, state)
    return state


def _fill_score_masks(mask_buf, qb, kb, ns):
    mask_buf[0] = jnp.zeros(mask_buf.shape[1:], F32)
    for d in range(ns):
        for half in range(2):
            mask_buf[d + 1, :, half * kb:(half + 1) * kb] = jnp.where(_diag_mask(qb, kb, 2 * d + half), 0.0, MASKED_SCORE)


def _softplus_bits(w):
    u = 1.0 + jnp.exp2(jnp.minimum(w, 64.0))
    return jnp.maximum(w, jnp.log2(u)), 1.0 / u


def _incl_lower(n):
    return jnp.where((_iota2((2 * n, n), 0) & (n - 1)) >= _iota2((2 * n, n), 1), 1.0, 0.0).astype(BF16)


def _incl_upper(n):
    return jnp.where((_iota2((2 * n, n), 0) & (n - 1)) <= _iota2((2 * n, n), 1), 1.0, 0.0).astype(BF16)


def _dot_cum(x, tri_bf16):
    hi, lo = _split2(x)
    return jnp.dot(jnp.concatenate([hi, lo], axis=1), tri_bf16, preferred_element_type=F32)


def _sb_fwd(qkv, d_model, name):
    T = qkv.shape[0]
    D = d_model
    H = D // HEAD_DIM
    QB = _pick(T, ATTN_Q_BLOCKS_FWD)
    KB = ATTN_K_BLOCK
    KS = 2 * KB
    ns = QB // KS
    nq = T // QB
    scale = HEAD_DIM ** -0.5

    def body(q_ref, k_ref, v_ref, o_ref, r_ref, w0, w1, w2, cum0, cum1, mask_buf):
        w_bufs, cum_bufs = (w0, w1, w2), (cum0, cum1)
        i = pl.program_id(1)

        @pl.when(i == 0)
        def _():
            _fill_score_masks(mask_buf, QB, KB, ns)

        q = q_ref[...]
        tri = _incl_lower(KB)
        n_tot = (i + 1) * ns

        def key_step(m):
            return jnp.maximum(n_tot - 1 - m, 0)

        def rows(ref, s):
            return ref[pl.ds(pl.multiple_of(s * KS, KS), KS), :]

        def scores(s, may_be_diagonal):
            w = _dot_nt(q, rows(k_ref, s)) * (scale * LOG2_E)
            return w + mask_buf[jnp.maximum(s - i * ns + 1, 0)] if may_be_diagonal else w

        def cums(w):
            sp = _softplus_bits(w)[0]
            return jnp.concatenate([_dot_cum(sp[:, :KB], tri), _dot_cum(sp[:, KB:], tri)], axis=1)

        def weights(w, cum, carry):
            a_r = jnp.exp2(w[:, KB:] - cum[:, KB:] - carry)
            carry = carry + cum[:, KB:KB + 1]
            a_l = jnp.exp2(w[:, :KB] - cum[:, :KB] - carry)
            return jnp.concatenate([a_l, a_r], axis=1).astype(BF16), carry + cum[:, 0:1]

        def trip(m, carry, ph):
            w_bufs[(ph + 2) % 3][...] = scores(key_step(m + 2), False)
            a, carry = weights(w_bufs[ph % 3][...], cum_bufs[ph % 2][...], carry)
            o_ref[...] += _dot(a, rows(v_ref, key_step(m)))
            cum_bufs[(ph + 1) % 2][...] = cums(w_bufs[(ph + 1) % 3][...])
            return carry

        o_ref[...] = jnp.zeros_like(o_ref)
        assert ns == 2
        w_bufs[0][...] = scores(key_step(0), True)
        w_bufs[1][...] = scores(key_step(1), True)
        cum_bufs[0][...] = cums(w_bufs[0][...])
        carry = _run_trips(trip, n_tot, jnp.zeros((QB, 1), F32))
        r_ref[0] = jnp.broadcast_to(carry, (QB, LANES))

    return pl.pallas_call(
        body, name=name, grid=(H, nq),
        in_specs=[pl.BlockSpec((QB, HEAD_DIM), lambda h, i: (i, h)),
                  pl.BlockSpec((T, HEAD_DIM), lambda h, i: (0, H + h)),
                  pl.BlockSpec((T, HEAD_DIM), lambda h, i: (0, 2 * H + h))],
        out_specs=[pl.BlockSpec((QB, HEAD_DIM), lambda h, i: (i, h)),
                   pl.BlockSpec((1, QB, LANES), lambda h, i: (h, i, 0))],
        out_shape=[jax.ShapeDtypeStruct((T, D), F32), jax.ShapeDtypeStruct((H, T, LANES), F32)],
        scratch_shapes=[pltpu.VMEM((QB, KS), F32)] * 5 + [pltpu.VMEM((ns + 1, QB, KS), F32)],
        compiler_params=_params(2),
    )(qkv, qkv, qkv)


def _gate_mul_fwd(o, gate, name):
    T, D = o.shape
    tm = _pick(T, (512, 256, 128))

    def body(o_ref, g_ref, y_ref):
        y_ref[...] = (o_ref[...] * _silu(g_ref[...])).astype(BF16)

    spec = pl.BlockSpec((tm, D), lambda i: (i, 0))
    return pl.pallas_call(body, name=name, grid=(T // tm,), in_specs=[spec, spec], out_specs=spec,
                          out_shape=jax.ShapeDtypeStruct((T, D), BF16), compiler_params=_params(1))(o, gate)


def _final_loss(h, w, target, name):
    T, D = h.shape
    tm = _pick(T, (512, 256, 128))

    def body(h_ref, w_ref, t_ref, dh_ref, loss_ref, dw_ref):
        x, w = h_ref[...], w_ref[...]
        r = lax.rsqrt(jnp.mean(x * x, axis=-1, keepdims=True) + EPS)
        err = x * r * w - t_ref[...]
        part = 0.5 * jnp.sum(jnp.mean(err * err, axis=-1, keepdims=True), axis=0, keepdims=True)
        dx, dw = _rms_bwd_math(x, w, err * (1.0 / D))
        dh_ref[...] = dx

        @pl.when(pl.program_id(0) == 0)
        def _():
            loss_ref[...] = jnp.zeros_like(loss_ref)
            dw_ref[...] = jnp.zeros_like(dw_ref)

        loss_ref[...] += jnp.broadcast_to(part, loss_ref.shape)
        dw_ref[...] += jnp.broadcast_to(dw, dw_ref.shape)

    return pl.pallas_call(
        body, name=name, grid=(T // tm,),
        in_specs=[pl.BlockSpec((tm, D), lambda i: (i, 0)), pl.BlockSpec((1, D), lambda i: (0, 0)),
                  pl.BlockSpec((tm, D), lambda i: (i, 0))],
        out_specs=[pl.BlockSpec((tm, D), lambda i: (i, 0)), pl.BlockSpec((8, LANES), lambda i: (0, 0)),
                   pl.BlockSpec((8, D), lambda i: (0, 0))],
        out_shape=[jax.ShapeDtypeStruct((T, D), F32), jax.ShapeDtypeStruct((8, LANES), F32),
                   jax.ShapeDtypeStruct((8, D), F32)],
        compiler_params=_params(1),
    )(h, w, target)


def _gate_mul_bwd(dy, o, gate, name):
    T, D = o.shape
    tm = _pick(T, (512, 256, 128))

    def body(dy_ref, o_ref, g_ref, do_ref, dg_ref):
        dy, g = dy_ref[...], g_ref[...]
        s = jax.nn.sigmoid(g)
        do_ref[...] = dy * (g * s)
        dg_ref[...] = (dy * o_ref[...] * (s + g * s * (1.0 - s))).astype(BF16)

    spec = pl.BlockSpec((tm, D), lambda i: (i, 0))
    return pl.pallas_call(body, name=name, grid=(T // tm,), in_specs=[spec] * 3, out_specs=[spec] * 2,
                          out_shape=[jax.ShapeDtypeStruct((T, D), F32), jax.ShapeDtypeStruct((T, D), BF16)],
                          compiler_params=_params(1))(dy, o, gate)


def _sb_bwd(qkv, do, r_tot, d_model, name):
    T = qkv.shape[0]
    D = d_model
    H = D // HEAD_DIM
    QB = _pick(T, ATTN_Q_BLOCKS_BWD)
    KB = ATTN_K_BLOCK
    KS = 2 * KB
    ns = QB // KS
    nq = T // QB
    n_key_steps = T // KS
    scale = HEAD_DIM ** -0.5

    def body(q_ref, k_ref, v_ref, do_ref, r_ref, dq_ref, dk_ref, dv_ref,
             dkt_acc, dvt_acc, dq_acc, w0, w1, w2, da0, da1, da2, cum0, cum1, sig0, sig1, mask_buf):
        w_bufs, da_bufs, cum_bufs, sig_bufs = (w0, w1, w2), (da0, da1, da2), (cum0, cum1), (sig0, sig1)
        i = pl.program_id(1)

        @pl.when(i == 0)
        def _():
            dkt_acc[...] = jnp.zeros_like(dkt_acc)
            dvt_acc[...] = jnp.zeros_like(dvt_acc)
            _fill_score_masks(mask_buf, QB, KB, ns)

        q = q_ref[...]
        do_blk = do_ref[...].astype(BF16)
        q_t = q.astype(F32).T.astype(BF16)
        do_t = do_ref[...].T.astype(BF16)
        row_total = r_ref[0][:, 0:1]
        tri_rev = _incl_lower(KB)
        tri_fwd = jnp.where(_iota2((KB, KB), 0) <= _iota2((KB, KB), 1), 1.0, 0.0).astype(BF16)
        n_tot = (i + 1) * ns

        def step_rows(ref, s):
            return ref[pl.ds(pl.multiple_of(s * KS, KS), KS), :]

        def scores(s):
            w = _dot_nt(q, step_rows(k_ref, s)) * (scale * LOG2_E) + mask_buf[jnp.maximum(s - i * ns + 1, 0)]
            return w, _dot_nt(do_blk, step_rows(v_ref, s))

        def softplus_sums(w):
            sp, one_minus_sig = _softplus_bits(w)
            cum = jnp.concatenate([_dot_cum(sp[:, :KB], tri_rev), _dot_cum(sp[:, KB:], tri_rev)], axis=1)
            return cum, 1.0 - one_minus_sig

        def weights(w, cum, da, left_sp):
            right_l = row_total - left_sp - cum[:, 0:1]
            right_r = right_l - cum[:, KB:KB + 1]
            a = jnp.concatenate([jnp.exp2(w[:, :KB] - cum[:, :KB] - right_l),
                                 jnp.exp2(w[:, KB:] - cum[:, KB:] - right_r)], axis=1)
            p = da * a
            cp = jnp.concatenate([_dot(p[:, :KB], tri_fwd), _dot(p[:, KB:], tri_fwd)], axis=1)
            return a.astype(BF16), p, cp, row_total - right_r

        def score_grads(p, cp, sig, left_p):
            cum_l = cp[:, :KB] + left_p
            cum_r = cp[:, KB:] + cum_l[:, KB - 1:KB]
            dz = p - sig * jnp.concatenate([cum_l, cum_r], axis=1)
            return dz.astype(BF16), cum_r[:, KB - 1:KB]

        def trip(m, st, ph):
            left_sp, left_p = st
            w_bufs[(ph + 2) % 3][...], da_bufs[(ph + 2) % 3][...] = scores(jnp.minimum(m + 2, n_tot - 1))
            a, p, cp, left_sp = weights(w_bufs[ph % 3][...], cum_bufs[ph % 2][...], da_bufs[ph % 3][...], left_sp)
            cum_bufs[(ph + 1) % 2][...], sig_bufs[(ph + 1) % 2][...] = softplus_sums(w_bufs[(ph + 1) % 3][...])
            dz, left_p = score_grads(p, cp, sig_bufs[ph % 2][...], left_p)
            dq_acc[...] += _dot(dz, step_rows(k_ref, m))
            dkt_acc[m] += jnp.dot(q_t, dz, preferred_element_type=F32) * scale
            dvt_acc[m] += jnp.dot(do_t, a, preferred_element_type=F32)
            return left_sp, left_p

        dq_acc[...] = jnp.zeros_like(dq_acc)
        w_bufs[0][...], da_bufs[0][...] = scores(0)
        w_bufs[1][...], da_bufs[1][...] = scores(jnp.minimum(1, n_tot - 1))
        cum_bufs[0][...], sig_bufs[0][...] = softplus_sums(w_bufs[0][...])
        zero_col = jnp.zeros((QB, 1), F32)
        _run_trips(trip, n_tot, (zero_col, zero_col))
        dq_ref[...] = (dq_acc[...] * scale).astype(BF16)

        @pl.when(i == nq - 1)
        def _():
            for s in range(n_key_steps):
                dk_ref[s * KS:(s + 1) * KS, :] = dkt_acc[s].T.astype(BF16)
                dv_ref[s * KS:(s + 1) * KS, :] = dvt_acc[s].T.astype(BF16)

    return pl.pallas_call(
        body, name=name, grid=(H, nq),
        in_specs=[pl.BlockSpec((QB, HEAD_DIM), lambda h, i: (i, h)),
                  pl.BlockSpec((T, HEAD_DIM), lambda h, i: (0, H + h)),
                  pl.BlockSpec((T, HEAD_DIM), lambda h, i: (0, 2 * H + h)),
                  pl.BlockSpec((QB, HEAD_DIM), lambda h, i: (i, h)),
                  pl.BlockSpec((1, QB, LANES), lambda h, i: (h, i, 0))],
        out_specs=[pl.BlockSpec((QB, HEAD_DIM), lambda h, i: (i, h)),
                   pl.BlockSpec((T, HEAD_DIM), lambda h, i: (0, h)),
                   pl.BlockSpec((T, HEAD_DIM), lambda h, i: (0, h))],
        out_shape=[jax.ShapeDtypeStruct((T, D), BF16)] * 3,
        scratch_shapes=[pltpu.VMEM((n_key_steps, HEAD_DIM, KS), F32), pltpu.VMEM((n_key_steps, HEAD_DIM, KS), F32),
                        pltpu.VMEM((QB, HEAD_DIM), F32)] + [pltpu.VMEM((QB, KS), F32)] * 10
                       + [pltpu.VMEM((ns + 1, QB, KS), F32)],
        compiler_params=_params(2),
    )(qkv, qkv, qkv, do, r_tot)


def _onorm_gate_bwd(dy, o, p, z_col, o_norm, name):
    T, D = o.shape
    H = D // HEAD_DIM
    tm = _pick(T, (256, 128, 64))

    def body(dy_ref, o_ref, z_ref, w_ref, do_ref, dz_ref, dw_ref):
        _, vjp = jax.vjp(functools.partial(_onorm_gate_math, n_heads=H), o_ref[...], z_ref[...], w_ref[...])
        do, dz, dw = vjp(dy_ref[...])
        do_ref[...] = do
        dz_ref[...] = dz.astype(BF16)

        @pl.when(pl.program_id(0) == 0)
        def _():
            dw_ref[...] = jnp.zeros_like(dw_ref)

        dw_ref[...] += jnp.broadcast_to(dw, dw_ref.shape)

    return pl.pallas_call(
        body, name=name, grid=(T // tm,),
        in_specs=[pl.BlockSpec((tm, D), lambda i: (i, 0)), pl.BlockSpec((tm, D), lambda i: (i, 0)),
                  pl.BlockSpec((tm, D), lambda i: (i, z_col)), pl.BlockSpec((1, HEAD_DIM), lambda i: (0, 0))],
        out_specs=[pl.BlockSpec((tm, D), lambda i: (i, 0)), pl.BlockSpec((tm, D), lambda i: (i, 0)),
                   pl.BlockSpec((8, HEAD_DIM), lambda i: (0, 0))],
        out_shape=[jax.ShapeDtypeStruct((T, D), F32), jax.ShapeDtypeStruct((T, D), BF16),
                   jax.ShapeDtypeStruct((8, HEAD_DIM), F32)],
        compiler_params=_params(1),
    )(dy, o, p, o_norm)


def _row_to_col(row):
    C = row.shape[1]
    eye = _iota2((C, C), 0) == _iota2((C, C), 1)
    return jnp.sum(jnp.where(eye, row, 0.0), axis=1, keepdims=True)


def _lane_sum(x):
    return jnp.sum(x, axis=-1, keepdims=True)


def _chunk_head_bwd(q, k, v, gc, beta, s_in, t_inv, do, ds_out):
    C = q.shape[0]
    r, c = _iota2((C, C), 0), _iota2((C, C), 1)
    causal, strict = r >= c, r > c
    decay = jnp.where(causal, jnp.exp(jnp.where(causal, gc - _col_to_row(gc), 0.0)), 0.0)
    kb, vb = k * beta, v * beta
    eg = jnp.exp(gc)
    kbg = kb * eg
    g_last = gc[C - 1:C, :]
    e_tail = jnp.exp(g_last - gc)
    k_tail = k * e_tail
    gl = jnp.exp(g_last)
    qg = q * eg
    t_inv_t = t_inv.T
    kk = _dot_nt(kb, k)
    u = _dot(t_inv, vb)
    w = _dot(t_inv, kbg)
    qk = _dot_nt(q, k)
    d_qg = _dot_nt(do, s_in)
    ds_state = _dot_tn(qg, do)
    yield
    low = jnp.where(strict, kk * decay, 0.0)
    attn = qk * decay
    w_state = _dot(w, s_in)
    d_vnew_intra = _dot_tn(attn, do)
    d_vnew_state = _dot(k_tail, ds_out)
    yield
    v_new = u - w_state
    d_vnew = d_vnew_intra + d_vnew_state
    d_ktail = _dot_nt(v_new, ds_out)
    d_attn_raw = _dot_nt(do, v_new)
    d_w = -_dot_nt(d_vnew, s_in)
    ds_w = _dot_tn(w, d_vnew)
    d_vb = _dot(t_inv_t, d_vnew)
    d_tinv_u = _dot_nt(d_vnew, vb)
    yield
    d_gl = _lane_sum(jnp.sum(s_in * ds_out, axis=0, keepdims=True))
    d_attn = jnp.where(causal, d_attn_raw, 0.0)
    ds_in = ds_out * gl + ds_state - ds_w
    d_kbg = _dot(t_inv_t, d_w)
    d_tinv_w = _dot_nt(d_w, kbg)
    d_qk = d_attn * decay
    dq_intra = _dot(d_qk, k)
    dk_intra = _dot_tn(d_qk, q)
    yield
    inner = _dot(t_inv_t, d_tinv_u + d_tinv_w)
    yield
    d_low_raw = _dot_nt(inner, t_inv)
    yield
    d_low = jnp.where(strict, -d_low_raw, 0.0)
    d_kk = d_low * decay
    d_kb_low = _dot(d_kk, k)
    dk_low = _dot_tn(d_kk, kb)
    yield
    d_kb = d_kb_low + d_kbg * eg
    dq = dq_intra + d_qg * eg
    dk = dk_low + dk_intra + d_ktail * e_tail + d_kb * beta
    dv = d_vb * beta
    dbeta = _lane_sum(d_kb * k + d_vb * v)
    m = d_low * low + d_attn * attn
    tail = d_ktail * k_tail
    d_g_last = d_gl * gl + _lane_sum(jnp.sum(tail, axis=0, keepdims=True))
    dgc = (_lane_sum(m) - _row_to_col(jnp.sum(m, axis=0, keepdims=True))
           + _lane_sum(d_qg * qg + d_kbg * kbg - tail))
    dgc = dgc + jnp.where(_iota2((C, 1), 0) == C - 1, d_g_last, 0.0)
    return dq, dk, dv, dgc, dbeta, ds_in


def _chunk_bwd(qkv, beta, gc, s_all, t_all, do, d_model, name):
    T = qkv.shape[0]
    D = d_model
    H = D // HEAD_DIM
    N = T // CHUNK
    G = CHUNKS_PER_STEP
    R = G * CHUNK
    n_steps = N // G

    def body(q_ref, k_ref, v_ref, beta_ref, gc_ref, s_ref, t_ref, do_ref, dqkv_ref, dbeta_ref, dg_ref, ds_ref):
        @pl.when(pl.program_id(0) == 0)
        def _():
            ds_ref[...] = jnp.zeros_like(ds_ref)

        lane = _iota2((CHUNK, LANES), 1)
        heads = [slice(h * HEAD_DIM, (h + 1) * HEAD_DIM) for h in range(H)]
        d_states = [ds_ref[h] for h in range(H)]
        for sub in reversed(range(G)):
            rows = slice(sub * CHUNK, (sub + 1) * CHUNK)
            dgc_all = jnp.zeros((CHUNK, LANES), F32)
            dbeta_all = jnp.zeros((CHUNK, LANES), F32)
            results = _lockstep(
                _chunk_head_bwd(q_ref[rows, hs], k_ref[rows, hs], v_ref[rows, hs], gc_ref[rows, h:h + 1],
                                beta_ref[rows, h:h + 1], s_ref[sub, h], t_ref[sub, h], do_ref[rows, hs], d_states[h])
                for h, hs in enumerate(heads))
            for h, (dq, dk, dv, dgc, dbeta, ds_in) in enumerate(results):
                d_states[h] = ds_in
                dqkv_ref[rows, h * HEAD_DIM:(h + 1) * HEAD_DIM] = dq
                dqkv_ref[rows, D + h * HEAD_DIM:D + (h + 1) * HEAD_DIM] = dk
                dqkv_ref[rows, 2 * D + h * HEAD_DIM:2 * D + (h + 1) * HEAD_DIM] = dv
                dgc_all = jnp.where(lane == h, dgc, dgc_all)
                dbeta_all = jnp.where(lane == h, dbeta, dbeta_all)
            dbeta_ref[rows, :] = dbeta_all
            dg_ref[rows, :] = _dot_mask(_chunk_tri(CHUNK, upper=True), dgc_all)
        for h in range(H):
            ds_ref[h] = d_states[h]

    rev = lambda n: n_steps - 1 - n
    return pl.pallas_call(
        body, name=name, grid=(n_steps,),
        in_specs=[pl.BlockSpec((R, D), lambda n: (rev(n), 0)), pl.BlockSpec((R, D), lambda n: (rev(n), 1)),
                  pl.BlockSpec((R, D), lambda n: (rev(n), 2)),
                  pl.BlockSpec((R, LANES), lambda n: (rev(n), 0)), pl.BlockSpec((R, LANES), lambda n: (rev(n), 0)),
                  pl.BlockSpec((G, H, HEAD_DIM, HEAD_DIM), lambda n: (rev(n), 0, 0, 0)),
                  pl.BlockSpec((G, H, CHUNK, CHUNK), lambda n: (rev(n), 0, 0, 0)),
                  pl.BlockSpec((R, D), lambda n: (rev(n), 0))],
        out_specs=[pl.BlockSpec((R, 3 * D), lambda n: (rev(n), 0)),
                   pl.BlockSpec((R, LANES), lambda n: (rev(n), 0)), pl.BlockSpec((R, LANES), lambda n: (rev(n), 0))],
        out_shape=[jax.ShapeDtypeStruct((T, 3 * D), F32), jax.ShapeDtypeStruct((T, LANES), F32),
                   jax.ShapeDtypeStruct((T, LANES), F32)],
        scratch_shapes=[pltpu.VMEM((H, HEAD_DIM, HEAD_DIM), F32)], compiler_params=_params(1),
    )(qkv, qkv, qkv, beta, gc, s_all, t_all, do)


def _gates_bwd(p, gate_params, col0, dbeta, dg, name):
    T = p.shape[0]
    tm = _pick(T, (256, 128, 64))

    def body(pb_ref, pa_ref, gp_ref, dbeta_ref, dg_ref, dp_ref, dgp_ref):
        gp = gp_ref[...]
        _, vjp = jax.vjp(_gates_math, pb_ref[...], pa_ref[...], gp[0:1, :], gp[1:2, :])
        dpb, dpa, d_alog, d_dt = vjp((dbeta_ref[...], dg_ref[...]))
        dp_ref[:, 0:LANES] = dpb.astype(BF16)
        dp_ref[:, LANES:2 * LANES] = dpa.astype(BF16)

        @pl.when(pl.program_id(0) == 0)
        def _():
            dgp_ref[...] = jnp.zeros_like(dgp_ref)

        dgp_ref[0:1, :] += d_alog
        dgp_ref[1:2, :] += d_dt

    return pl.pallas_call(
        body, name=name, grid=(T // tm,),
        in_specs=[pl.BlockSpec((tm, LANES), lambda i: (i, col0)), pl.BlockSpec((tm, LANES), lambda i: (i, col0 + 1)),
                  pl.BlockSpec((8, LANES), lambda i: (0, 0)),
                  pl.BlockSpec((tm, LANES), lambda i: (i, 0)), pl.BlockSpec((tm, LANES), lambda i: (i, 0))],
        out_specs=[pl.BlockSpec((tm, 2 * LANES), lambda i: (i, 0)), pl.BlockSpec((8, LANES), lambda i: (0, 0))],
        out_shape=[jax.ShapeDtypeStruct((T, 2 * LANES), BF16), jax.ShapeDtypeStruct((8, LANES), F32)],
        compiler_params=_params(1),
    )(p, p, gate_params, dbeta, dg)


def _conv_bwd_act(p, conv_w, dqkv, d_model, name):
    T = p.shape[0]
    D = d_model
    H = D // HEAD_DIM
    tm = _pick(T, (256, 128, 64))

    def columns(j, prev_outs):
        def body(cur_ref, prev_ref, w_ref, dout_ref, *rest):
            dc_ref, dw_ref = rest[-2:]
            i = pl.program_id(0)
            prev = prev_ref[...] * (i > 0).astype(F32)
            taps = _conv_taps(cur_ref[...], prev)
            w = w_ref[...]
            c = sum(taps[k] * w[k:k + 1, :] for k in range(4))
            _, vjp = jax.vjp(lambda cc: _qkv_post(cc, j, H), c)
            (dc,) = vjp(dout_ref[...])
            dc_ref[...] = dc

            @pl.when(i == 0)
            def _():
                dw_ref[...] = jnp.zeros_like(dw_ref)

            for k in range(4):
                dw_ref[k:k + 1, :] += jnp.sum(dc * taps[k], axis=0, keepdims=True)

        specs, operands, aliases = _alias_previous(prev_outs, 4)
        return pl.pallas_call(
            body, name="%s%d" % (name, j), grid=(T // tm,),
            in_specs=[pl.BlockSpec((tm, D), lambda i: (i, j)),
                      pl.BlockSpec((8, D), lambda i: (jnp.maximum(i * (tm // 8) - 1, 0), j)),
                      pl.BlockSpec((4, D), lambda i: (0, j)),
                      pl.BlockSpec((tm, D), lambda i: (i, j))] + specs,
            out_specs=[pl.BlockSpec((tm, D), lambda i: (i, j)), pl.BlockSpec((4, D), lambda i: (0, j))],
            out_shape=[jax.ShapeDtypeStruct((T, 3 * D), F32), jax.ShapeDtypeStruct((4, 3 * D), F32)],
            input_output_aliases=aliases, compiler_params=_params(1),
        )(p, p, conv_w, dqkv, *operands)

    return _column_calls(columns)


def _conv_bwd_input(dc, conv_w, name):
    T, D3 = dc.shape
    D = D3 // 3
    tm = _pick(T, (256, 128, 64))
    n_t = T // tm

    def body(cur_ref, next_ref, w_ref, dp_ref):
        i = pl.program_id(0)
        cur = cur_ref[...]
        nxt = next_ref[...] * (i < n_t - 1).astype(F32)
        ext = jnp.concatenate([cur, nxt], axis=0)
        w = w_ref[...]
        acc = cur * w[3:4, :]
        for s in (1, 2, 3):
            acc = acc + pltpu.roll(ext, tm + 8 - s, 0)[0:tm] * w[3 - s:4 - s, :]
        dp_ref[...] = acc.astype(BF16)

    return pl.pallas_call(
        body, name=name, grid=(n_t, 3),
        in_specs=[pl.BlockSpec((tm, D), lambda i, j: (i, j)),
                  pl.BlockSpec((8, D), lambda i, j: (jnp.minimum((i + 1) * (tm // 8), T // 8 - 1), j)),
                  pl.BlockSpec((4, D), lambda i, j: (0, j))],
        out_specs=pl.BlockSpec((tm, D), lambda i, j: (i, j)),
        out_shape=jax.ShapeDtypeStruct((T, D3), BF16), compiler_params=_params(2),
    )(dc, dc, conv_w)


def _comm_call(body, arrays, out_shape, n_remote, n_local, name):
    any_spec = pl.BlockSpec(memory_space=pl.ANY)
    return pl.pallas_call(
        body, name=name, in_specs=[any_spec] * len(arrays), out_specs=[any_spec] * len(out_shape), out_shape=out_shape,
        scratch_shapes=[pltpu.SemaphoreType.DMA((n_remote,)), pltpu.SemaphoreType.DMA((n_remote,)),
                        pltpu.SemaphoreType.DMA((n_local,))],
        compiler_params=pltpu.CompilerParams(has_side_effects=True),
    )(*arrays)


def _gather_shards(arrays, name):
    n = len(arrays)

    def body(*refs):
        ins, outs = refs[:n], refs[n:2 * n]
        send_sems, recv_sems, local_sems = refs[2 * n:]
        x, y, c = lax.axis_index("x"), lax.axis_index("y"), lax.axis_index("c")
        me, sibling = (x, y, c), (x, y, 1 - c)
        chips = [(1 - x, y), (x, 1 - y), (1 - x, 1 - y)]

        def copy(a, k, block, to, src=None):
            dst = outs[a].at[4 * block[0] + 2 * block[1] + block[2]]
            return pltpu.make_async_remote_copy(
                src_ref=dst if src is None else src, dst_ref=dst, send_sem=send_sems.at[a * 7 + k],
                recv_sem=recv_sems.at[a * 7 + k], device_id=to, device_id_type=MESH_ID)

        local = [pltpu.make_async_copy(ins[a], outs[a].at[4 * x + 2 * y + c], local_sems.at[a]) for a in range(n)]
        first = [copy(a, 0, me, sibling, src=ins[a]) for a in range(n)]
        first += [copy(a, 1 + j, me, (*chip, c), src=ins[a]) for j, chip in enumerate(chips) for a in range(n)]
        for cp in local + first:
            cp.start()
        passed = []
        for j, chip in enumerate(chips):
            for a in range(n):
                copy(a, 1 + j, (*chip, c), me).wait_recv()
                passed.append(copy(a, 4 + j, (*chip, c), sibling))
                passed[-1].start()
        for a in range(n):
            copy(a, 0, sibling, me).wait_recv()
            for j, chip in enumerate(chips):
                copy(a, 4 + j, (*chip, 1 - c), me).wait_recv()
        for cp in first + passed:
            cp.wait_send()
        for cp in local:
            cp.wait()

    out_shape = [jax.ShapeDtypeStruct((N_DEV,) + a.shape, a.dtype) for a in arrays]
    return _comm_call(body, arrays, out_shape, 7 * n, n, name)


def _pair_exchange(arrays, name):
    n = len(arrays)

    def body(*refs):
        ins, pair = refs[:n], refs[n:2 * n]
        send_sems, recv_sems, _ = refs[2 * n:]
        x, y, c = lax.axis_index("x"), lax.axis_index("y"), lax.axis_index("c")
        sends = [pltpu.make_async_remote_copy(
            src_ref=ins[a].at[1 - c], dst_ref=pair[a], send_sem=send_sems.at[a], recv_sem=recv_sems.at[a],
            device_id=(x, y, 1 - c), device_id_type=MESH_ID) for a in range(n)]
        for cp in sends:
            cp.start()
        for cp in sends:
            cp.wait_recv()
        for cp in sends:
            cp.wait_send()

    out_shape = [jax.ShapeDtypeStruct(a.shape[1:], a.dtype) for a in arrays]
    return _comm_call(body, arrays, out_shape, n, 1, name)


def _pair_add(own, pair, name, out_dtype):
    _, R, C = own.shape
    tr = next((t for t in (256, 128, 64, 32, 16) if R % t == 0), R)

    def body(a_ref, b_ref, o_ref):
        o_ref[...] = (a_ref[...] + b_ref[...]).astype(out_dtype)

    spec = pl.BlockSpec((1, tr, C), lambda q, i: (q, i, 0))
    return pl.pallas_call(body, name=name, grid=(4, R // tr), in_specs=[spec, spec], out_specs=spec,
                          out_shape=jax.ShapeDtypeStruct(own.shape, out_dtype), compiler_params=_params(2))(own, pair)


def _chip_exchange(arrays, name):
    n = len(arrays)

    def body(*refs):
        ins, outs = refs[:n], refs[n:2 * n]
        send_sems, recv_sems, local_sems = refs[2 * n:]
        x, y, c = lax.axis_index("x"), lax.axis_index("y"), lax.axis_index("c")
        my_chip = 2 * x + y
        chips = [(1 - x, y), (x, 1 - y), (1 - x, 1 - y)]
        local = [pltpu.make_async_copy(ins[a].at[my_chip], outs[a].at[my_chip], local_sems.at[a]) for a in range(n)]
        sends = [pltpu.make_async_remote_copy(
            src_ref=ins[a].at[2 * px + py], dst_ref=outs[a].at[my_chip], send_sem=send_sems.at[a * 3 + j],
            recv_sem=recv_sems.at[a * 3 + j], device_id=(px, py, c), device_id_type=MESH_ID)
            for j, (px, py) in enumerate(chips) for a in range(n)]
        arrivals = [pltpu.make_async_remote_copy(
            src_ref=ins[a].at[my_chip], dst_ref=outs[a].at[2 * px + py], send_sem=send_sems.at[a * 3 + j],
            recv_sem=recv_sems.at[a * 3 + j], device_id=(px, py, c), device_id_type=MESH_ID)
            for j, (px, py) in enumerate(chips) for a in range(n)]
        for cp in local + sends:
            cp.start()
        for cp in arrivals:
            cp.wait_recv()
        for cp in sends:
            cp.wait_send()
        for cp in local:
            cp.wait()

    out_shape = [jax.ShapeDtypeStruct(a.shape, a.dtype) for a in arrays]
    return _comm_call(body, arrays, out_shape, 3 * n, n, name)


def _reduce_adamw(recv, w, m, v, name):
    S, R, C = recv.shape
    tr = next((t for t in (256, 128, 64, 32, 16, 8) if R % t == 0), R)
    c1 = 1.0 - ADAM_B1 ** ADAM_STEP
    c2 = 1.0 - ADAM_B2 ** ADAM_STEP

    def body(r_ref, w_ref, m_ref, v_ref, g_ref, d_ref, nm_ref, nv_ref):
        g = r_ref[0].astype(F32)
        for s in range(1, S):
            g = g + r_ref[s].astype(F32)
        nm = ADAM_B1 * m_ref[...] + (1.0 - ADAM_B1) * g
        nv = ADAM_B2 * v_ref[...] + (1.0 - ADAM_B2) * (g * g)
        g_ref[...] = g
        nm_ref[...] = nm
        nv_ref[...] = nv
        d_ref[...] = -ADAM_LR * ((nm / c1) / (jnp.sqrt(nv / c2) + ADAM_EPS) + ADAM_WD * w_ref[...])

    spec = pl.BlockSpec((tr, C), lambda i: (i, 0))
    return pl.pallas_call(
        body, name=name, grid=(R // tr,),
        in_specs=[pl.BlockSpec((S, tr, C), lambda i: (0, i, 0)), spec, spec, spec], out_specs=[spec] * 4,
        out_shape=[jax.ShapeDtypeStruct((R, C), F32)] * 4, compiler_params=_params(1),
    )(recv, w, m, v)


def _forward_local(x, target, nw0, nw1, fw, wa_in, conv_w, gate_params, o_norm, wa_out, wb_in, wb_out):
    T, D = x.shape
    nD = D // LANES
    sv = {}
    sv["u0"] = _rmsnorm_fwd(x, nw0, "a_norm_fwd")
    sv["pa"] = _mm_nn(sv["u0"], wa_in, "a_in_proj")
    sv["qkv_a"] = _conv_fwd(sv["pa"], conv_w, D, "a_conv_fwd")
    sv["beta"], sv["gc"] = _gates_fwd(sv["pa"], gate_params, 4 * nD, "a_gates_fwd")
    sv["o_a"], sv["s_all"], sv["t_all"] = _chunk_fwd(sv["qkv_a"], sv["beta"], sv["gc"], D, "a_chunk_fwd")
    sv["y_a"] = _onorm_gate_fwd(sv["o_a"], sv["pa"], 3, o_norm, "a_onorm_fwd")
    sv["h1"] = _mm_nn(sv["y_a"], wa_out, "a_out_proj", add=x)
    sv["u1"] = _rmsnorm_fwd(sv["h1"], nw1, "b_norm_fwd")
    sv["qkv_b"] = _mm_nn(sv["u1"], wb_in[:, :3 * D], "b_in_proj_qkv", out_dtype=BF16)
    sv["gate_b"] = _mm_nn(sv["u1"], wb_in[:, 3 * D:], "b_in_proj_gate")
    sv["o_b"], sv["r_b"] = _sb_fwd(sv["qkv_b"], D, "b_attn_fwd")
    sv["y_b"] = _gate_mul_fwd(sv["o_b"], sv["gate_b"], "b_gate_fwd")
    sv["h2"] = _mm_nn(sv["y_b"], wb_out, "b_out_proj", add=sv["h1"])
    sv["dh2"], sv["loss"], sv["dfw"] = _final_loss(sv["h2"], fw, target, "final_loss")
    return sv


def _backward_local(sv, x, nw0, nw1, wa_in, conv_w, gate_params, o_norm, wa_out, wb_in, wb_out):
    T, D = x.shape
    nD = D // LANES
    g = {}
    dh2 = sv["dh2"]
    g["wb_out"] = _mm_tn(sv["y_b"], dh2, "b_out_proj_dw")
    dy_b = _mm_nt([(dh2, wb_out)], "b_out_proj_dx")
    do_b, dgate_b = _gate_mul_bwd(dy_b, sv["o_b"], sv["gate_b"], "b_gate_bwd")
    dq_b, dk_b, dv_b = _sb_bwd(sv["qkv_b"], do_b, sv["r_b"], D, "b_attn_bwd")
    dp_b = [dq_b, dk_b, dv_b, dgate_b]
    g["wb_in"] = jnp.concatenate([_mm_tn(sv["u1"], dp, "b_in_proj_dw%d" % c) for c, dp in enumerate(dp_b)], axis=1)
    du1 = _mm_nt([(dp, wb_in[:, c * D:(c + 1) * D]) for c, dp in enumerate(dp_b)], "b_in_proj_dx")
    dh1, g["nw1"] = _rmsnorm_bwd(sv["h1"], nw1, du1, dh2, "b_norm_bwd")
    g["wa_out"] = _mm_tn(sv["y_a"], dh1, "a_out_proj_dw")
    dy_a = _mm_nt([(dh1, wa_out)], "a_out_proj_dx")
    do_a, dz_a, g["o_norm"] = _onorm_gate_bwd(dy_a, sv["o_a"], sv["pa"], 3, o_norm, "a_onorm_bwd")
    dqkv_a, dbeta, dg = _chunk_bwd(sv["qkv_a"], sv["beta"], sv["gc"], sv["s_all"], sv["t_all"], do_a, D, "a_chunk_bwd")
    dp_gates, g["gate_params"] = _gates_bwd(sv["pa"], gate_params, 4 * nD, dbeta, dg, "a_gates_bwd")
    dc, g["conv_w"] = _conv_bwd_act(sv["pa"], conv_w, dqkv_a, D, "a_conv_bwd_act")
    dp_qkv = _conv_bwd_input(dc, conv_w, "a_conv_bwd_input")
    dp_a = [(dp_qkv, 0, 3 * D), (dz_a, 3 * D, 4 * D), (dp_gates, 4 * D, 4 * D + 2 * LANES)]
    g["wa_in"] = jnp.concatenate([_mm_tn(sv["u0"], dp, "a_in_proj_dw%d" % c) for c, (dp, _, _) in enumerate(dp_a)], axis=1)
    du0 = _mm_nt([(dp, wa_in[:, lo:hi]) for dp, lo, hi in dp_a], "a_in_proj_dx")
    g["x"], g["nw0"] = _rmsnorm_bwd(x, nw0, du0, dh1, "a_norm_bwd")
    g["fw"] = sv["dfw"]
    return g


def kernel(x, norm_w, a_w_in, a_conv_w, a_a_log, a_dt_bias, a_o_norm, a_w_out, b_w_in, b_w_out, final_norm_w, loss_target, m_norm_w, m_a_w_in, m_a_conv_w, m_a_a_log, m_a_dt_bias, m_a_o_norm, m_a_w_out, m_b_w_in, m_b_w_out, m_final_norm_w, v_norm_w, v_a_w_in, v_a_conv_w, v_a_a_log, v_a_dt_bias, v_a_o_norm, v_a_w_out, v_b_w_in, v_b_w_out, v_final_norm_w):
    D = x.shape[-1]
    H = D // HEAD_DIM
    shards = [a_w_in[0].astype(BF16), a_w_out[0].astype(BF16), b_w_in[0].astype(BF16), b_w_out[0].astype(BF16), a_conv_w[0]]
    ga_in, ga_out, gb_in, gb_out, g_conv = _gather_shards(shards, "weights_gather")
    wa = ga_in.transpose(1, 0, 2).reshape(D, -1)
    pad = lambda w: jnp.pad(w, ((0, 0), (0, LANES - w.shape[1])))
    wa_in = jnp.concatenate([wa[:, :4 * D], pad(wa[:, 4 * D:4 * D + H]), pad(wa[:, 4 * D + H:])], axis=1)
    wa_out = ga_out.reshape(D, D)
    wb_in = gb_in.transpose(1, 0, 2).reshape(D, 4 * D)
    wb_out = gb_out.reshape(D, D)
    conv_w = g_conv.transpose(1, 0, 2).reshape(4, 3 * D)
    gate_params = jnp.zeros((8, LANES), F32).at[0, :H].set(a_a_log[0]).at[1, :H].set(a_dt_bias[0])
    nw0, nw1, fw = norm_w[0:1], norm_w[1:2], final_norm_w[None]

    sv = _forward_local(x[0], loss_target[0], nw0, nw1, fw, wa_in, conv_w, gate_params, a_o_norm, wa_out, wb_in, wb_out)
    g = _backward_local(sv, x[0], nw0, nw1, wa_in, conv_w, gate_params, a_o_norm, wa_out, wb_in, wb_out)

    gwa = g["wa_in"]
    gwa = jnp.concatenate([gwa[:, :4 * D], gwa[:, 4 * D:4 * D + H], gwa[:, 4 * D + LANES:4 * D + LANES + H]], axis=1)
    row = lambda v: jnp.pad(v.reshape(1, -1), ((0, 0), (0, D - v.size)))
    small = jnp.concatenate([g["nw0"][0:1], g["nw1"][0:1], g["fw"][0:1], row(g["gate_params"][0, :H]),
                             row(g["gate_params"][1, :H]), row(g["o_norm"][0]), row(sv["loss"][0, 0:1]),
                             jnp.zeros((1, D), F32)], axis=0)
    cols = lambda a: a.reshape(a.shape[0], 4, 2, -1).transpose(2, 1, 0, 3)
    rows = lambda a: a.reshape(4, 2, -1, a.shape[1]).transpose(1, 0, 2, 3)
    contribs = [cols(gwa), rows(g["wa_out"]), cols(g["wb_in"]), rows(g["wb_out"]), cols(g["conv_w"]),
                jnp.broadcast_to(small[None, None], (2, 4, 8, D))]
    pair = _pair_exchange(contribs, "grads_pair_exchange")
    my_core = lax.axis_index("c")
    own = [lax.dynamic_index_in_dim(a, my_core, axis=0, keepdims=False) for a in contribs]
    partial = [_pair_add(o, p, "grads_pair_add%d" % k, BF16 if k < len(own) - 1 else F32)
               for k, (o, p) in enumerate(zip(own, pair))]
    ra_in, ra_out, rb_in, rb_out, r_conv, r_small = _chip_exchange(partial, "grads_chip_exchange")

    outs = {}
    for nm, recv, w, m, v in (("a_w_in", ra_in, a_w_in, m_a_w_in, v_a_w_in), ("a_w_out", ra_out, a_w_out, m_a_w_out, v_a_w_out),
                              ("b_w_in", rb_in, b_w_in, m_b_w_in, v_b_w_in), ("b_w_out", rb_out, b_w_out, m_b_w_out, v_b_w_out),
                              ("a_conv_w", r_conv, a_conv_w, m_a_conv_w, v_a_conv_w)):
        outs[nm] = tuple(o[None] for o in _reduce_adamw(recv, w[0], m[0], v[0], "adamw_" + nm))

    def pack(nw, alog, dt, onorm, fnw):
        return jnp.concatenate([nw, fnw.reshape(1, D), row(alog), row(dt), row(onorm), jnp.zeros((2, D), F32)], axis=0)

    s_g, s_d, s_m, s_v = _reduce_adamw(
        r_small, pack(norm_w, a_a_log, a_dt_bias, a_o_norm, final_norm_w),
        pack(m_norm_w, m_a_a_log, m_a_dt_bias, m_a_o_norm, m_final_norm_w),
        pack(v_norm_w, v_a_a_log, v_a_dt_bias, v_a_o_norm, v_final_norm_w), "adamw_small")
    loss = s_g[6, 0]
    for i, s in enumerate((s_g, s_d, s_m, s_v)):
        outs.setdefault("norm_w", [None] * 4)[i] = s[0:2]
        outs.setdefault("final_norm_w", [None] * 4)[i] = s[2]
        outs.setdefault("a_a_log", [None] * 4)[i] = s[3:4, :H]
        outs.setdefault("a_dt_bias", [None] * 4)[i] = s[4:5, :H]
        outs.setdefault("a_o_norm", [None] * 4)[i] = s[5:6, :HEAD_DIM]
    names = ("norm_w", "a_w_in", "a_conv_w", "a_a_log", "a_dt_bias", "a_o_norm", "a_w_out", "b_w_in", "b_w_out", "final_norm_w")
    return (loss, g["x"][None]) + tuple(outs[n][i] for i in range(4) for n in names)
```

```python
import functools

import jax
import jax.numpy as jnp
from jax import lax
from jax.experimental import pallas as pl
from jax.experimental.pallas import tpu as pltpu

F32 = jnp.float32
BF16 = jnp.bfloat16
EPS = 1e-6
LOG2_E = 1.4426950408889634
MASKED_SCORE = -1e30
HEAD_DIM = 128
CHUNK = 64
CHUNKS_PER_STEP = 4
ATTN_Q_BLOCKS_FWD = (512, 256)
ATTN_Q_BLOCKS_BWD = (512, 256)
ATTN_K_BLOCK = 128
LANES = 128
N_DEV = 8
VMEM_LIMIT_BYTES = 48 * 1024 * 1024
ADAM_LR, ADAM_B1, ADAM_B2, ADAM_EPS, ADAM_WD, ADAM_STEP = 0.001, 0.9, 0.999, 1e-08, 0.01, 10
MESH_ID = pl.DeviceIdType.MESH


def _pick(n, candidates):
    for c in candidates:
        if n % c == 0:
            return c
    raise ValueError(f"no tile for {n} in {candidates}")


def _params(n_grid_axes):
    return pltpu.CompilerParams(dimension_semantics=("arbitrary",) * n_grid_axes, vmem_limit_bytes=VMEM_LIMIT_BYTES)


def _dot(a, b):
    return jnp.dot(a.astype(BF16), b.astype(BF16), preferred_element_type=F32)


def _dot_nt(a, b):
    return lax.dot_general(a.astype(BF16), b.astype(BF16), (((1,), (1,)), ((), ())), preferred_element_type=F32)


def _dot_tn(a, b):
    return lax.dot_general(a.astype(BF16), b.astype(BF16), (((0,), (0,)), ((), ())), preferred_element_type=F32)


def _split2(x):
    hi = x.astype(BF16)
    lo = (x - hi.astype(F32)).astype(BF16)
    return hi, lo


def _split3(x):
    hi = x.astype(BF16)
    r = x - hi.astype(F32)
    mid = r.astype(BF16)
    lo = (r - mid.astype(F32)).astype(BF16)
    return hi, mid, lo


def _dot3(a, b):
    a_hi, a_lo = _split2(a)
    b_hi, b_lo = _split2(b)
    d = functools.partial(jnp.dot, preferred_element_type=F32)
    return d(a_hi, b_hi) + (d(a_hi, b_lo) + d(a_lo, b_hi))


def _silu(x):
    return x * jax.nn.sigmoid(x)


def _softplus(x):
    return jnp.maximum(x, 0.0) + jnp.log1p(jnp.exp(-jnp.abs(x)))


def _iota2(shape, axis):
    return lax.broadcasted_iota(jnp.int32, shape, axis)


def _rms_bwd_math(x, w, dy):
    r = lax.rsqrt(jnp.mean(x * x, axis=-1, keepdims=True) + EPS)
    xhat = x * r
    dxhat = dy * w
    dx = r * (dxhat - xhat * jnp.mean(dxhat * xhat, axis=-1, keepdims=True))
    dw = jnp.sum(dy * xhat, axis=0, keepdims=True)
    return dx, dw


def _rmsnorm_fwd(x, w, name):
    T, D = x.shape
    tm = _pick(T, (512, 256, 128))

    def body(x_ref, w_ref, o_ref):
        xf = x_ref[...]
        r = lax.rsqrt(jnp.mean(xf * xf, axis=-1, keepdims=True) + EPS)
        o_ref[...] = (xf * r * w_ref[...]).astype(BF16)

    return pl.pallas_call(
        body, name=name, grid=(T // tm,),
        in_specs=[pl.BlockSpec((tm, D), lambda i: (i, 0)), pl.BlockSpec((1, D), lambda i: (0, 0))],
        out_specs=pl.BlockSpec((tm, D), lambda i: (i, 0)),
        out_shape=jax.ShapeDtypeStruct((T, D), BF16), compiler_params=_params(1),
    )(x, w)


def _rmsnorm_bwd(x, w, du, dres, name):
    T, D = x.shape
    tm = _pick(T, (512, 256, 128))

    def body(x_ref, w_ref, du_ref, dres_ref, dx_ref, dw_ref):
        dx, dw = _rms_bwd_math(x_ref[...], w_ref[...], du_ref[...].astype(F32))
        dx_ref[...] = dres_ref[...] + dx

        @pl.when(pl.program_id(0) == 0)
        def _():
            dw_ref[...] = jnp.zeros_like(dw_ref)

        dw_ref[...] += jnp.broadcast_to(dw, dw_ref.shape)

    return pl.pallas_call(
        body, name=name, grid=(T // tm,),
        in_specs=[pl.BlockSpec((tm, D), lambda i: (i, 0)), pl.BlockSpec((1, D), lambda i: (0, 0)),
                  pl.BlockSpec((tm, D), lambda i: (i, 0)), pl.BlockSpec((tm, D), lambda i: (i, 0))],
        out_specs=[pl.BlockSpec((tm, D), lambda i: (i, 0)), pl.BlockSpec((8, D), lambda i: (0, 0))],
        out_shape=[jax.ShapeDtypeStruct((T, D), F32), jax.ShapeDtypeStruct((8, D), F32)],
        compiler_params=_params(1),
    )(x, w, du, dres)


def _mm_nn(a, b, name, add=None, out_dtype=F32):
    M, K = a.shape
    _, N = b.shape
    tm = _pick(M, (256, 128)) if N > 2048 else _pick(M, (512, 256, 128))

    def body(*refs):
        a_ref, b_ref = refs[0], refs[1]
        o_ref = refs[-1]
        acc = _dot(a_ref[...], b_ref[...])
        if add is not None:
            acc = acc + refs[2][...]
        o_ref[...] = acc.astype(out_dtype)

    in_specs = [pl.BlockSpec((tm, K), lambda i: (i, 0)), pl.BlockSpec((K, N), lambda i: (0, 0))]
    args = [a, b]
    if add is not None:
        in_specs.append(pl.BlockSpec((tm, N), lambda i: (i, 0)))
        args.append(add)
    return pl.pallas_call(
        body, name=name, grid=(M // tm,), in_specs=in_specs,
        out_specs=pl.BlockSpec((tm, N), lambda i: (i, 0)),
        out_shape=jax.ShapeDtypeStruct((M, N), out_dtype), compiler_params=_params(1),
    )(*args)


def _mm_nt(pairs, name):
    M = pairs[0][0].shape[0]
    N = pairs[0][1].shape[0]
    n = len(pairs)
    tm = _pick(M, (512, 256, 128))

    def body(*refs):
        acc = _dot_nt(refs[0][...], refs[n][...])
        for p in range(1, n):
            acc = acc + _dot_nt(refs[p][...], refs[n + p][...])
        refs[-1][...] = acc

    in_specs = ([pl.BlockSpec((tm, a.shape[1]), lambda i: (i, 0)) for a, _ in pairs]
                + [pl.BlockSpec(b.shape, lambda i: (0, 0)) for _, b in pairs])
    return pl.pallas_call(
        body, name=name, grid=(M // tm,), in_specs=in_specs,
        out_specs=pl.BlockSpec((tm, N), lambda i: (i, 0)),
        out_shape=jax.ShapeDtypeStruct((M, N), F32), compiler_params=_params(1),
    )(*[a for a, _ in pairs], *[b for _, b in pairs])


def _mm_tn(a, b, name):
    R, M = a.shape
    _, N = b.shape
    tn = _pick(N, (1536, 1024, 512, 256, 128))
    tr = _pick(R, (512, 256, 128))

    def body(a_ref, b_ref, o_ref):
        @pl.when(pl.program_id(1) == 0)
        def _():
            o_ref[...] = jnp.zeros_like(o_ref)

        o_ref[...] += _dot_tn(a_ref[...], b_ref[...])

    return pl.pallas_call(
        body, name=name, grid=(N // tn, R // tr),
        in_specs=[pl.BlockSpec((tr, M), lambda j, r: (r, 0)), pl.BlockSpec((tr, tn), lambda j, r: (r, j))],
        out_specs=pl.BlockSpec((M, tn), lambda j, r: (0, j)),
        out_shape=jax.ShapeDtypeStruct((M, N), F32), compiler_params=_params(2),
    )(a, b)


def _qkv_post(c, j, n_heads):
    s = _silu(c)
    if j == 2:
        return s
    parts = []
    for h in range(n_heads):
        sh = s[:, h * HEAD_DIM:(h + 1) * HEAD_DIM]
        parts.append(sh * lax.rsqrt(jnp.sum(sh * sh, axis=-1, keepdims=True) + EPS))
    n = jnp.concatenate(parts, axis=-1)
    return n * (HEAD_DIM ** -0.5) if j == 0 else n


def _column_calls(make_call, n_cols=3):
    outs = None
    for j in range(n_cols):
        outs = make_call(j, outs)
    return outs


def _alias_previous(prev, n_inputs):
    if prev is None:
        return [], [], {}
    prev = list(prev) if isinstance(prev, (list, tuple)) else [prev]
    return ([pl.BlockSpec(memory_space=pl.ANY)] * len(prev), prev, {n_inputs + k: k for k in range(len(prev))})


def _conv_taps(cur, halo_prev):
    tm = cur.shape[0]
    ext = jnp.concatenate([halo_prev, cur], axis=0)
    taps = [pltpu.roll(ext, s, 0)[8:8 + tm] for s in (3, 2, 1)]
    return taps + [cur]


def _conv_fwd(p, conv_w, d_model, name):
    T = p.shape[0]
    D = d_model
    H = D // HEAD_DIM
    tm = _pick(T, (256, 128, 64))

    def columns(j, prev_out):
        def body(cur_ref, prev_ref, w_ref, *rest):
            o_ref = rest[-1]
            prev = prev_ref[...] * (pl.program_id(0) > 0).astype(F32)
            taps = _conv_taps(cur_ref[...], prev)
            w = w_ref[...]
            c = sum(taps[k] * w[k:k + 1, :] for k in range(4))
            o_ref[...] = _qkv_post(c, j, H)

        specs, operands, aliases = _alias_previous(prev_out, 3)
        return pl.pallas_call(
            body, name="%s%d" % (name, j), grid=(T // tm,),
            in_specs=[pl.BlockSpec((tm, D), lambda i: (i, j)),
                      pl.BlockSpec((8, D), lambda i: (jnp.maximum(i * (tm // 8) - 1, 0), j)),
                      pl.BlockSpec((4, D), lambda i: (0, j))] + specs,
            out_specs=pl.BlockSpec((tm, D), lambda i: (i, j)),
            out_shape=jax.ShapeDtypeStruct((T, 3 * D), F32), input_output_aliases=aliases, compiler_params=_params(1),
        )(p, p, conv_w, *operands)

    return _column_calls(columns)


def _chunk_tri(tm, upper):
    r, c = _iota2((tm, tm), 0), _iota2((tm, tm), 1)
    same = (r // CHUNK) == (c // CHUNK)
    tri = (c >= r) if upper else (c <= r)
    return jnp.where(same & tri, 1.0, 0.0).astype(BF16)


def _dot_mask(mask_bf16, x):
    hi, mid, lo = _split3(x)
    d = functools.partial(jnp.dot, preferred_element_type=F32)
    return d(mask_bf16, hi) + (d(mask_bf16, mid) + d(mask_bf16, lo))


def _gates_math(pb, pa, a_log, dt_bias):
    beta = jax.nn.sigmoid(pb)
    g = -jnp.exp(a_log) * _softplus(pa + dt_bias)
    return beta, g


def _gates_fwd(p, gate_params, col0, name):
    T = p.shape[0]
    tm = _pick(T, (256, 128, 64))

    def body(pb_ref, pa_ref, gp_ref, beta_ref, gc_ref):
        gp = gp_ref[...]
        beta, g = _gates_math(pb_ref[...], pa_ref[...], gp[0:1, :], gp[1:2, :])
        beta_ref[...] = beta
        gc_ref[...] = _dot_mask(_chunk_tri(tm, upper=False), g)

    return pl.pallas_call(
        body, name=name, grid=(T // tm,),
        in_specs=[pl.BlockSpec((tm, LANES), lambda i: (i, col0)), pl.BlockSpec((tm, LANES), lambda i: (i, col0 + 1)),
                  pl.BlockSpec((8, LANES), lambda i: (0, 0))],
        out_specs=[pl.BlockSpec((tm, LANES), lambda i: (i, 0))] * 2,
        out_shape=[jax.ShapeDtypeStruct((T, LANES), F32)] * 2, compiler_params=_params(1),
    )(p, p, gate_params)


def _col_to_row(col):
    C = col.shape[0]
    eye = _iota2((C, C), 0) == _iota2((C, C), 1)
    return jnp.sum(jnp.where(eye, col, 0.0), axis=0, keepdims=True)


def _lockstep(groups, skew, finish):
    groups = [list(g) for g in groups]
    results = [[None] * len(g) for g in groups]
    left = [len(g) for g in groups]
    rnd = 0
    while any(left):
        for gi, gens in enumerate(groups):
            if rnd < gi * skew or not left[gi]:
                continue
            for idx, gen in enumerate(gens):
                if gen is None:
                    continue
                try:
                    next(gen)
                except StopIteration as done:
                    results[gi][idx] = done.value
                    gens[idx] = None
                    left[gi] -= 1
            if not left[gi]:
                finish(gi, results[gi])
        rnd += 1


def _unit_lower_inverse(low):
    C = low.shape[0]
    eye = (_iota2((C, C), 0) == _iota2((C, C), 1)).astype(F32)
    t = eye - low
    p = _dot3(low, low)
    yield
    n = 2
    while True:
        tp = _dot3(t, p)
        n *= 2
        if n < C:
            p = _dot3(p, p)
        yield
        t = t + tp
        if n >= C:
            return t


FWD_STATE_SKEW = 3


def _chunk_head_fwd(load, take_state, give):
    q, k, v, gc, beta = load()
    C = q.shape[0]
    r, c = _iota2((C, C), 0), _iota2((C, C), 1)
    causal, strict = r >= c, r > c
    decay = jnp.where(causal, jnp.exp(jnp.where(causal, gc - _col_to_row(gc), 0.0)), 0.0)
    kb, vb = k * beta, v * beta
    eg = jnp.exp(gc)
    kk = _dot_nt(kb, k)
    qk = _dot_nt(q, k)
    yield
    t_inv = yield from _unit_lower_inverse(jnp.where(strict, kk * decay, 0.0))
    give(t_inv=t_inv)
    u = _dot(t_inv, vb)
    w = _dot(t_inv, kb * eg)
    yield
    s_in = take_state()
    give(s_in=s_in)
    o_state = _dot(q * eg, s_in)
    w_state = _dot(w, s_in)
    yield
    v_new = u - w_state
    g_last = gc[C - 1:C, :]
    o_intra = _dot(qk * decay, v_new)
    s_add = _dot_tn(k * jnp.exp(g_last - gc), v_new)
    yield
    give(o=o_state + o_intra, s_out=s_in * jnp.exp(g_last) + s_add)


def _chunk_fwd(qkv, beta, gc, d_model, name):
    T = qkv.shape[0]
    D = d_model
    H = D // HEAD_DIM
    N = T // CHUNK
    G = CHUNKS_PER_STEP
    R = G * CHUNK

    def body(q_ref, k_ref, v_ref, beta_ref, gc_ref, o_ref, s_all_ref, t_all_ref, s_ref):
        @pl.when(pl.program_id(0) == 0)
        def _():
            s_ref[...] = jnp.zeros_like(s_ref)

        heads = [slice(h * HEAD_DIM, (h + 1) * HEAD_DIM) for h in range(H)]
        states = [[s_ref[h] for h in range(H)]] + [[None] * H for _ in range(G)]

        def head(sub, h):
            rows = slice(sub * CHUNK, (sub + 1) * CHUNK)

            def give(t_inv=None, s_in=None, o=None, s_out=None):
                if t_inv is not None:
                    t_all_ref[sub, h] = t_inv
                if s_in is not None:
                    s_all_ref[sub, h] = s_in
                if o is not None:
                    o_ref[rows, heads[h]] = o
                    states[sub + 1][h] = s_out

            load = lambda: (q_ref[rows, heads[h]], k_ref[rows, heads[h]], v_ref[rows, heads[h]],
                            gc_ref[rows, h:h + 1], beta_ref[rows, h:h + 1])
            return _chunk_head_fwd(load, lambda: states[sub][h], give)

        _lockstep([[head(sub, h) for h in range(H)] for sub in range(G)], FWD_STATE_SKEW, lambda sub, results: None)
        for h in range(H):
            s_ref[h] = states[G][h]

    return pl.pallas_call(
        body, name=name, grid=(N // G,),
        in_specs=[pl.BlockSpec((R, D), lambda n: (n, 0)), pl.BlockSpec((R, D), lambda n: (n, 1)),
                  pl.BlockSpec((R, D), lambda n: (n, 2)),
                  pl.BlockSpec((R, LANES), lambda n: (n, 0)), pl.BlockSpec((R, LANES), lambda n: (n, 0))],
        out_specs=[pl.BlockSpec((R, D), lambda n: (n, 0)),
                   pl.BlockSpec((G, H, HEAD_DIM, HEAD_DIM), lambda n: (n, 0, 0, 0)),
                   pl.BlockSpec((G, H, CHUNK, CHUNK), lambda n: (n, 0, 0, 0))],
        out_shape=[jax.ShapeDtypeStruct((T, D), F32), jax.ShapeDtypeStruct((N, H, HEAD_DIM, HEAD_DIM), F32),
                   jax.ShapeDtypeStruct((N, H, CHUNK, CHUNK), F32)],
        scratch_shapes=[pltpu.VMEM((H, HEAD_DIM, HEAD_DIM), F32)], compiler_params=_params(1),
    )(qkv, qkv, qkv, beta, gc)


def _onorm_gate_math(o, z, w, n_heads):
    parts = []
    for h in range(n_heads):
        hs = slice(h * HEAD_DIM, (h + 1) * HEAD_DIM)
        oh = o[:, hs]
        y = oh * lax.rsqrt(jnp.mean(oh * oh, axis=-1, keepdims=True) + EPS) * w
        parts.append(y * _silu(z[:, hs]))
    return jnp.concatenate(parts, axis=-1)


def _onorm_gate_fwd(o, p, z_col, o_norm, name):
    T, D = o.shape
    H = D // HEAD_DIM
    tm = _pick(T, (256, 128, 64))

    def body(o_ref, z_ref, w_ref, y_ref):
        y_ref[...] = _onorm_gate_math(o_ref[...], z_ref[...], w_ref[...], H).astype(BF16)

    return pl.pallas_call(
        body, name=name, grid=(T // tm,),
        in_specs=[pl.BlockSpec((tm, D), lambda i: (i, 0)), pl.BlockSpec((tm, D), lambda i: (i, z_col)),
                  pl.BlockSpec((1, HEAD_DIM), lambda i: (0, 0))],
        out_specs=pl.BlockSpec((tm, D), lambda i: (i, 0)),
        out_shape=jax.ShapeDtypeStruct((T, D), BF16), compiler_params=_params(1),
    )(o, p, o_norm)


def _diag_mask(qb, kb, d):
    return _iota2((qb, kb), 0) > _iota2((qb, kb), 1) + d * kb


def _run_trips(trip, n, state):
    def six(j, st):
        for u in range(6):
            st = trip(6 * j + u, st, u)
        return st

    state = lax.fori_loop(0, n // 6, six, state)
    base = (n // 6) * 6
    for u in (0, 2):
        pair = lambda st, u=u: trip(base + u + 1, trip(base + u, st, u), u + 1)
        state = lax.cond(n - base > u, pair, lambda st: st, state)
    return state


def _fill_score_masks(mask_buf, qb, kb, ns):
    mask_buf[0] = jnp.zeros(mask_buf.shape[1:], F32)
    for d in range(ns):
        for half in range(2):
            mask_buf[d + 1, :, half * kb:(half + 1) * kb] = jnp.where(_diag_mask(qb, kb, 2 * d + half), 0.0, MASKED_SCORE)


def _softplus_bits(w):
    u = 1.0 + jnp.exp2(jnp.minimum(w, 64.0))
    return jnp.maximum(w, jnp.log2(u)), 1.0 / u


def _incl_lower(n):
    return jnp.where((_iota2((2 * n, n), 0) & (n - 1)) >= _iota2((2 * n, n), 1), 1.0, 0.0).astype(BF16)


def _incl_upper(n):
    return jnp.where((_iota2((2 * n, n), 0) & (n - 1)) <= _iota2((2 * n, n), 1), 1.0, 0.0).astype(BF16)


def _dot_cum(x, tri_bf16):
    hi, lo = _split2(x)
    return jnp.dot(jnp.concatenate([hi, lo], axis=1), tri_bf16, preferred_element_type=F32)


def _sb_fwd(qkv, d_model, name):
    T = qkv.shape[0]
    D = d_model
    H = D // HEAD_DIM
    QB = _pick(T, ATTN_Q_BLOCKS_FWD)
    KB = ATTN_K_BLOCK
    KS = 2 * KB
    ns = QB // KS
    nq = T // QB
    scale = HEAD_DIM ** -0.5

    def body(q_ref, k_ref, v_ref, o_ref, r_ref, w0, w1, w2, cum0, cum1, mask_buf):
        w_bufs, cum_bufs = (w0, w1, w2), (cum0, cum1)
        i = pl.program_id(1)

        @pl.when(i == 0)
        def _():
            _fill_score_masks(mask_buf, QB, KB, ns)

        q = q_ref[...]
        tri = _incl_lower(KB)
        n_tot = (i + 1) * ns

        def key_step(m):
            return jnp.maximum(n_tot - 1 - m, 0)

        def rows(ref, s):
            return ref[pl.ds(pl.multiple_of(s * KS, KS), KS), :]

        def scores(s, may_be_diagonal):
            w = _dot_nt(q, rows(k_ref, s)) * (scale * LOG2_E)
            return w + mask_buf[jnp.maximum(s - i * ns + 1, 0)] if may_be_diagonal else w

        def cums(w):
            sp = _softplus_bits(w)[0]
            return jnp.concatenate([_dot_cum(sp[:, :KB], tri), _dot_cum(sp[:, KB:], tri)], axis=1)

        def weights(w, cum, carry):
            a_r = jnp.exp2(w[:, KB:] - cum[:, KB:] - carry)
            carry = carry + cum[:, KB:KB + 1]
            a_l = jnp.exp2(w[:, :KB] - cum[:, :KB] - carry)
            return jnp.concatenate([a_l, a_r], axis=1).astype(BF16), carry + cum[:, 0:1]

        def trip(m, carry, ph):
            w_bufs[(ph + 2) % 3][...] = scores(key_step(m + 2), False)
            a, carry = weights(w_bufs[ph % 3][...], cum_bufs[ph % 2][...], carry)
            o_ref[...] += _dot(a, rows(v_ref, key_step(m)))
            cum_bufs[(ph + 1) % 2][...] = cums(w_bufs[(ph + 1) % 3][...])
            return carry

        o_ref[...] = jnp.zeros_like(o_ref)
        assert ns == 2
        w_bufs[0][...] = scores(key_step(0), True)
        w_bufs[1][...] = scores(key_step(1), True)
        cum_bufs[0][...] = cums(w_bufs[0][...])
        carry = _run_trips(trip, n_tot, jnp.zeros((QB, 1), F32))
        r_ref[0] = jnp.broadcast_to(carry, (QB, LANES))

    return pl.pallas_call(
        body, name=name, grid=(H, nq),
        in_specs=[pl.BlockSpec((QB, HEAD_DIM), lambda h, i: (i, h)),
                  pl.BlockSpec((T, HEAD_DIM), lambda h, i: (0, H + h)),
                  pl.BlockSpec((T, HEAD_DIM), lambda h, i: (0, 2 * H + h))],
        out_specs=[pl.BlockSpec((QB, HEAD_DIM), lambda h, i: (i, h)),
                   pl.BlockSpec((1, QB, LANES), lambda h, i: (h, i, 0))],
        out_shape=[jax.ShapeDtypeStruct((T, D), F32), jax.ShapeDtypeStruct((H, T, LANES), F32)],
        scratch_shapes=[pltpu.VMEM((QB, KS), F32)] * 5 + [pltpu.VMEM((ns + 1, QB, KS), F32)],
        compiler_params=_params(2),
    )(qkv, qkv, qkv)


def _gate_mul_fwd(o, gate, name):
    T, D = o.shape
    tm = _pick(T, (512, 256, 128))

    def body(o_ref, g_ref, y_ref):
        y_ref[...] = (o_ref[...] * _silu(g_ref[...])).astype(BF16)

    spec = pl.BlockSpec((tm, D), lambda i: (i, 0))
    return pl.pallas_call(body, name=name, grid=(T // tm,), in_specs=[spec, spec], out_specs=spec,
                          out_shape=jax.ShapeDtypeStruct((T, D), BF16), compiler_params=_params(1))(o, gate)


def _final_loss(h, w, target, name):
    T, D = h.shape
    tm = _pick(T, (512, 256, 128))

    def body(h_ref, w_ref, t_ref, dh_ref, loss_ref, dw_ref):
        x, w = h_ref[...], w_ref[...]
        r = lax.rsqrt(jnp.mean(x * x, axis=-1, keepdims=True) + EPS)
        err = x * r * w - t_ref[...]
        part = 0.5 * jnp.sum(jnp.mean(err * err, axis=-1, keepdims=True), axis=0, keepdims=True)
        dx, dw = _rms_bwd_math(x, w, err * (1.0 / D))
        dh_ref[...] = dx

        @pl.when(pl.program_id(0) == 0)
        def _():
            loss_ref[...] = jnp.zeros_like(loss_ref)
            dw_ref[...] = jnp.zeros_like(dw_ref)

        loss_ref[...] += jnp.broadcast_to(part, loss_ref.shape)
        dw_ref[...] += jnp.broadcast_to(dw, dw_ref.shape)

    return pl.pallas_call(
        body, name=name, grid=(T // tm,),
        in_specs=[pl.BlockSpec((tm, D), lambda i: (i, 0)), pl.BlockSpec((1, D), lambda i: (0, 0)),
                  pl.BlockSpec((tm, D), lambda i: (i, 0))],
        out_specs=[pl.BlockSpec((tm, D), lambda i: (i, 0)), pl.BlockSpec((8, LANES), lambda i: (0, 0)),
                   pl.BlockSpec((8, D), lambda i: (0, 0))],
        out_shape=[jax.ShapeDtypeStruct((T, D), F32), jax.ShapeDtypeStruct((8, LANES), F32),
                   jax.ShapeDtypeStruct((8, D), F32)],
        compiler_params=_params(1),
    )(h, w, target)


def _gate_mul_bwd(dy, o, gate, name):
    T, D = o.shape
    tm = _pick(T, (512, 256, 128))

    def body(dy_ref, o_ref, g_ref, do_ref, dg_ref):
        dy, g = dy_ref[...], g_ref[...]
        s = jax.nn.sigmoid(g)
        do_ref[...] = dy * (g * s)
        dg_ref[...] = (dy * o_ref[...] * (s + g * s * (1.0 - s))).astype(BF16)

    spec = pl.BlockSpec((tm, D), lambda i: (i, 0))
    return pl.pallas_call(body, name=name, grid=(T // tm,), in_specs=[spec] * 3, out_specs=[spec] * 2,
                          out_shape=[jax.ShapeDtypeStruct((T, D), F32), jax.ShapeDtypeStruct((T, D), BF16)],
                          compiler_params=_params(1))(dy, o, gate)


def _sb_bwd(qkv, do, r_tot, d_model, name):
    T = qkv.shape[0]
    D = d_model
    H = D // HEAD_DIM
    QB = _pick(T, ATTN_Q_BLOCKS_BWD)
    KB = ATTN_K_BLOCK
    KS = 2 * KB
    ns = QB // KS
    nq = T // QB
    n_key_steps = T // KS
    scale = HEAD_DIM ** -0.5

    def body(q_ref, k_ref, v_ref, do_ref, r_ref, dq_ref, dk_ref, dv_ref,
             dkt_acc, dvt_acc, dq_acc, w0, w1, w2, da0, da1, da2, cum0, cum1, sig0, sig1, mask_buf):
        w_bufs, da_bufs, cum_bufs, sig_bufs = (w0, w1, w2), (da0, da1, da2), (cum0, cum1), (sig0, sig1)
        i = pl.program_id(1)

        @pl.when(i == 0)
        def _():
            dkt_acc[...] = jnp.zeros_like(dkt_acc)
            dvt_acc[...] = jnp.zeros_like(dvt_acc)
            _fill_score_masks(mask_buf, QB, KB, ns)

        q = q_ref[...]
        do_blk = do_ref[...].astype(BF16)
        q_t = q.astype(F32).T.astype(BF16)
        do_t = do_ref[...].T.astype(BF16)
        row_total = r_ref[0][:, 0:1]
        tri_rev = _incl_lower(KB)
        tri_fwd = jnp.where(_iota2((KB, KB), 0) <= _iota2((KB, KB), 1), 1.0, 0.0).astype(BF16)
        n_tot = (i + 1) * ns

        def step_rows(ref, s):
            return ref[pl.ds(pl.multiple_of(s * KS, KS), KS), :]

        def scores(s):
            w = _dot_nt(q, step_rows(k_ref, s)) * (scale * LOG2_E) + mask_buf[jnp.maximum(s - i * ns + 1, 0)]
            return w, _dot_nt(do_blk, step_rows(v_ref, s))

        def softplus_sums(w):
            sp, one_minus_sig = _softplus_bits(w)
            cum = jnp.concatenate([_dot_cum(sp[:, :KB], tri_rev), _dot_cum(sp[:, KB:], tri_rev)], axis=1)
            return cum, 1.0 - one_minus_sig

        def weights(w, cum, da, left_sp):
            right_l = row_total - left_sp - cum[:, 0:1]
            right_r = right_l - cum[:, KB:KB + 1]
            a = jnp.concatenate([jnp.exp2(w[:, :KB] - cum[:, :KB] - right_l),
                                 jnp.exp2(w[:, KB:] - cum[:, KB:] - right_r)], axis=1)
            p = da * a
            cp = jnp.concatenate([_dot(p[:, :KB], tri_fwd), _dot(p[:, KB:], tri_fwd)], axis=1)
            return a.astype(BF16), p, cp, row_total - right_r

        def score_grads(p, cp, sig, left_p):
            cum_l = cp[:, :KB] + left_p
            cum_r = cp[:, KB:] + cum_l[:, KB - 1:KB]
            dz = p - sig * jnp.concatenate([cum_l, cum_r], axis=1)
            return dz.astype(BF16), cum_r[:, KB - 1:KB]

        def trip(m, st, ph):
            left_sp, left_p = st
            w_bufs[(ph + 2) % 3][...], da_bufs[(ph + 2) % 3][...] = scores(jnp.minimum(m + 2, n_tot - 1))
            a, p, cp, left_sp = weights(w_bufs[ph % 3][...], cum_bufs[ph % 2][...], da_bufs[ph % 3][...], left_sp)
            cum_bufs[(ph + 1) % 2][...], sig_bufs[(ph + 1) % 2][...] = softplus_sums(w_bufs[(ph + 1) % 3][...])
            dz, left_p = score_grads(p, cp, sig_bufs[ph % 2][...], left_p)
            dq_acc[...] += _dot(dz, step_rows(k_ref, m))
            dkt_acc[m] += jnp.dot(q_t, dz, preferred_element_type=F32) * scale
            dvt_acc[m] += jnp.dot(do_t, a, preferred_element_type=F32)
            return left_sp, left_p

        dq_acc[...] = jnp.zeros_like(dq_acc)
        w_bufs[0][...], da_bufs[0][...] = scores(0)
        w_bufs[1][...], da_bufs[1][...] = scores(jnp.minimum(1, n_tot - 1))
        cum_bufs[0][...], sig_bufs[0][...] = softplus_sums(w_bufs[0][...])
        zero_col = jnp.zeros((QB, 1), F32)
        _run_trips(trip, n_tot, (zero_col, zero_col))
        dq_ref[...] = (dq_acc[...] * scale).astype(BF16)

        @pl.when(i == nq - 1)
        def _():
            for s in range(n_key_steps):
                dk_ref[s * KS:(s + 1) * KS, :] = dkt_acc[s].T.astype(BF16)
                dv_ref[s * KS:(s + 1) * KS, :] = dvt_acc[s].T.astype(BF16)

    return pl.pallas_call(
        body, name=name, grid=(H, nq),
        in_specs=[pl.BlockSpec((QB, HEAD_DIM), lambda h, i: (i, h)),
                  pl.BlockSpec((T, HEAD_DIM), lambda h, i: (0, H + h)),
                  pl.BlockSpec((T, HEAD_DIM), lambda h, i: (0, 2 * H + h)),
                  pl.BlockSpec((QB, HEAD_DIM), lambda h, i: (i, h)),
                  pl.BlockSpec((1, QB, LANES), lambda h, i: (h, i, 0))],
        out_specs=[pl.BlockSpec((QB, HEAD_DIM), lambda h, i: (i, h)),
                   pl.BlockSpec((T, HEAD_DIM), lambda h, i: (0, h)),
                   pl.BlockSpec((T, HEAD_DIM), lambda h, i: (0, h))],
        out_shape=[jax.ShapeDtypeStruct((T, D), BF16)] * 3,
        scratch_shapes=[pltpu.VMEM((n_key_steps, HEAD_DIM, KS), F32), pltpu.VMEM((n_key_steps, HEAD_DIM, KS), F32),
                        pltpu.VMEM((QB, HEAD_DIM), F32)] + [pltpu.VMEM((QB, KS), F32)] * 10
                       + [pltpu.VMEM((ns + 1, QB, KS), F32)],
        compiler_params=_params(2),
    )(qkv, qkv, qkv, do, r_tot)


def _onorm_gate_bwd(dy, o, p, z_col, o_norm, name):
    T, D = o.shape
    H = D // HEAD_DIM
    tm = _pick(T, (256, 128, 64))

    def body(dy_ref, o_ref, z_ref, w_ref, do_ref, dz_ref, dw_ref):
        _, vjp = jax.vjp(functools.partial(_onorm_gate_math, n_heads=H), o_ref[...], z_ref[...], w_ref[...])
        do, dz, dw = vjp(dy_ref[...])
        do_ref[...] = do
        dz_ref[...] = dz.astype(BF16)

        @pl.when(pl.program_id(0) == 0)
        def _():
            dw_ref[...] = jnp.zeros_like(dw_ref)

        dw_ref[...] += jnp.broadcast_to(dw, dw_ref.shape)

    return pl.pallas_call(
        body, name=name, grid=(T // tm,),
        in_specs=[pl.BlockSpec((tm, D), lambda i: (i, 0)), pl.BlockSpec((tm, D), lambda i: (i, 0)),
                  pl.BlockSpec((tm, D), lambda i: (i, z_col)), pl.BlockSpec((1, HEAD_DIM), lambda i: (0, 0))],
        out_specs=[pl.BlockSpec((tm, D), lambda i: (i, 0)), pl.BlockSpec((tm, D), lambda i: (i, 0)),
                   pl.BlockSpec((8, HEAD_DIM), lambda i: (0, 0))],
        out_shape=[jax.ShapeDtypeStruct((T, D), F32), jax.ShapeDtypeStruct((T, D), BF16),
                   jax.ShapeDtypeStruct((8, HEAD_DIM), F32)],
        compiler_params=_params(1),
    )(dy, o, p, o_norm)


def _row_to_col(row):
    C = row.shape[1]
    eye = _iota2((C, C), 0) == _iota2((C, C), 1)
    return jnp.sum(jnp.where(eye, row, 0.0), axis=1, keepdims=True)


def _lane_sum(x):
    return jnp.sum(x, axis=-1, keepdims=True)


BWD_STATE_SKEW = 3


def _chunk_head_bwd(load, take_dstate, give):
    q, k, v, gc, beta, s_in, t_inv, do = load()
    C = q.shape[0]
    r, c = _iota2((C, C), 0), _iota2((C, C), 1)
    causal, strict = r >= c, r > c
    decay = jnp.where(causal, jnp.exp(jnp.where(causal, gc - _col_to_row(gc), 0.0)), 0.0)
    kb, vb = k * beta, v * beta
    eg = jnp.exp(gc)
    kbg = kb * eg
    g_last = gc[C - 1:C, :]
    e_tail = jnp.exp(g_last - gc)
    k_tail = k * e_tail
    gl = jnp.exp(g_last)
    qg = q * eg
    t_inv_t = t_inv.T
    kk = _dot_nt(kb, k)
    u = _dot(t_inv, vb)
    w = _dot(t_inv, kbg)
    qk = _dot_nt(q, k)
    d_qg = _dot_nt(do, s_in)
    ds_state = _dot_tn(qg, do)
    yield
    ds_out = take_dstate()
    low = jnp.where(strict, kk * decay, 0.0)
    attn = qk * decay
    w_state = _dot(w, s_in)
    d_vnew_intra = _dot_tn(attn, do)
    d_vnew_state = _dot(k_tail, ds_out)
    yield
    v_new = u - w_state
    d_vnew = d_vnew_intra + d_vnew_state
    d_ktail = _dot_nt(v_new, ds_out)
    d_attn_raw = _dot_nt(do, v_new)
    d_w = -_dot_nt(d_vnew, s_in)
    ds_w = _dot_tn(w, d_vnew)
    d_vb = _dot(t_inv_t, d_vnew)
    d_tinv_u = _dot_nt(d_vnew, vb)
    yield
    d_gl = _lane_sum(jnp.sum(s_in * ds_out, axis=0, keepdims=True))
    d_attn = jnp.where(causal, d_attn_raw, 0.0)
    give(ds_in=ds_out * gl + ds_state - ds_w)
    d_kbg = _dot(t_inv_t, d_w)
    d_tinv_w = _dot_nt(d_w, kbg)
    d_qk = d_attn * decay
    dq_intra = _dot(d_qk, k)
    dk_intra = _dot_tn(d_qk, q)
    yield
    inner = _dot(t_inv_t, d_tinv_u + d_tinv_w)
    yield
    d_low_raw = _dot_nt(inner, t_inv)
    yield
    d_low = jnp.where(strict, -d_low_raw, 0.0)
    d_kk = d_low * decay
    d_kb_low = _dot(d_kk, k)
    dk_low = _dot_tn(d_kk, kb)
    yield
    d_kb = d_kb_low + d_kbg * eg
    give(dq=dq_intra + d_qg * eg, dk=dk_low + dk_intra + d_ktail * e_tail + d_kb * beta, dv=d_vb * beta)
    dbeta = _lane_sum(d_kb * k + d_vb * v)
    m = d_low * low + d_attn * attn
    tail = d_ktail * k_tail
    d_g_last = d_gl * gl + _lane_sum(jnp.sum(tail, axis=0, keepdims=True))
    dgc = (_lane_sum(m) - _row_to_col(jnp.sum(m, axis=0, keepdims=True))
           + _lane_sum(d_qg * qg + d_kbg * kbg - tail))
    return dgc + jnp.where(_iota2((C, 1), 0) == C - 1, d_g_last, 0.0), dbeta


def _chunk_bwd(qkv, beta, gc, s_all, t_all, do, d_model, name):
    T = qkv.shape[0]
    D = d_model
    H = D // HEAD_DIM
    N = T // CHUNK
    G = CHUNKS_PER_STEP
    R = G * CHUNK
    n_steps = N // G

    def body(q_ref, k_ref, v_ref, beta_ref, gc_ref, s_ref, t_ref, do_ref, dqkv_ref, dbeta_ref, dg_ref, ds_ref):
        @pl.when(pl.program_id(0) == 0)
        def _():
            ds_ref[...] = jnp.zeros_like(ds_ref)

        lane = _iota2((CHUNK, LANES), 1)
        heads = [slice(h * HEAD_DIM, (h + 1) * HEAD_DIM) for h in range(H)]
        d_states = [[ds_ref[h] for h in range(H)]] + [[None] * H for _ in range(G)]

        def head(g, h):
            sub = G - 1 - g
            rows = slice(sub * CHUNK, (sub + 1) * CHUNK)

            def give(ds_in=None, dq=None, dk=None, dv=None):
                if ds_in is not None:
                    d_states[g + 1][h] = ds_in
                if dq is not None:
                    dqkv_ref[rows, h * HEAD_DIM:(h + 1) * HEAD_DIM] = dq
                    dqkv_ref[rows, D + h * HEAD_DIM:D + (h + 1) * HEAD_DIM] = dk
                    dqkv_ref[rows, 2 * D + h * HEAD_DIM:2 * D + (h + 1) * HEAD_DIM] = dv

            load = lambda: (q_ref[rows, heads[h]], k_ref[rows, heads[h]], v_ref[rows, heads[h]], gc_ref[rows, h:h + 1],
                            beta_ref[rows, h:h + 1], s_ref[sub, h], t_ref[sub, h], do_ref[rows, heads[h]])
            return _chunk_head_bwd(load, lambda: d_states[g][h], give)

        def finish(g, results):
            rows = slice((G - 1 - g) * CHUNK, (G - g) * CHUNK)
            dgc_all = jnp.zeros((CHUNK, LANES), F32)
            dbeta_all = jnp.zeros((CHUNK, LANES), F32)
            for h, (dgc, dbeta) in enumerate(results):
                dgc_all = jnp.where(lane == h, dgc, dgc_all)
                dbeta_all = jnp.where(lane == h, dbeta, dbeta_all)
            dbeta_ref[rows, :] = dbeta_all
            dg_ref[rows, :] = _dot_mask(_chunk_tri(CHUNK, upper=True), dgc_all)

        _lockstep([[head(g, h) for h in range(H)] for g in range(G)], BWD_STATE_SKEW, finish)
        for h in range(H):
            ds_ref[h] = d_states[G][h]

    rev = lambda n: n_steps - 1 - n
    return pl.pallas_call(
        body, name=name, grid=(n_steps,),
        in_specs=[pl.BlockSpec((R, D), lambda n: (rev(n), 0)), pl.BlockSpec((R, D), lambda n: (rev(n), 1)),
                  pl.BlockSpec((R, D), lambda n: (rev(n), 2)),
                  pl.BlockSpec((R, LANES), lambda n: (rev(n), 0)), pl.BlockSpec((R, LANES), lambda n: (rev(n), 0)),
                  pl.BlockSpec((G, H, HEAD_DIM, HEAD_DIM), lambda n: (rev(n), 0, 0, 0)),
                  pl.BlockSpec((G, H, CHUNK, CHUNK), lambda n: (rev(n), 0, 0, 0)),
                  pl.BlockSpec((R, D), lambda n: (rev(n), 0))],
        out_specs=[pl.BlockSpec((R, 3 * D), lambda n: (rev(n), 0)),
                   pl.BlockSpec((R, LANES), lambda n: (rev(n), 0)), pl.BlockSpec((R, LANES), lambda n: (rev(n), 0))],
        out_shape=[jax.ShapeDtypeStruct((T, 3 * D), F32), jax.ShapeDtypeStruct((T, LANES), F32),
                   jax.ShapeDtypeStruct((T, LANES), F32)],
        scratch_shapes=[pltpu.VMEM((H, HEAD_DIM, HEAD_DIM), F32)], compiler_params=_params(1),
    )(qkv, qkv, qkv, beta, gc, s_all, t_all, do)


def _gates_bwd(p, gate_params, col0, dbeta, dg, name):
    T = p.shape[0]
    tm = _pick(T, (256, 128, 64))

    def body(pb_ref, pa_ref, gp_ref, dbeta_ref, dg_ref, dp_ref, dgp_ref):
        gp = gp_ref[...]
        _, vjp = jax.vjp(_gates_math, pb_ref[...], pa_ref[...], gp[0:1, :], gp[1:2, :])
        dpb, dpa, d_alog, d_dt = vjp((dbeta_ref[...], dg_ref[...]))
        dp_ref[:, 0:LANES] = dpb.astype(BF16)
        dp_ref[:, LANES:2 * LANES] = dpa.astype(BF16)

        @pl.when(pl.program_id(0) == 0)
        def _():
            dgp_ref[...] = jnp.zeros_like(dgp_ref)

        dgp_ref[0:1, :] += d_alog
        dgp_ref[1:2, :] += d_dt

    return pl.pallas_call(
        body, name=name, grid=(T // tm,),
        in_specs=[pl.BlockSpec((tm, LANES), lambda i: (i, col0)), pl.BlockSpec((tm, LANES), lambda i: (i, col0 + 1)),
                  pl.BlockSpec((8, LANES), lambda i: (0, 0)),
                  pl.BlockSpec((tm, LANES), lambda i: (i, 0)), pl.BlockSpec((tm, LANES), lambda i: (i, 0))],
        out_specs=[pl.BlockSpec((tm, 2 * LANES), lambda i: (i, 0)), pl.BlockSpec((8, LANES), lambda i: (0, 0))],
        out_shape=[jax.ShapeDtypeStruct((T, 2 * LANES), BF16), jax.ShapeDtypeStruct((8, LANES), F32)],
        compiler_params=_params(1),
    )(p, p, gate_params, dbeta, dg)


def _conv_bwd_act(p, conv_w, dqkv, d_model, name):
    T = p.shape[0]
    D = d_model
    H = D // HEAD_DIM
    tm = _pick(T, (256, 128, 64))

    def columns(j, prev_outs):
        def body(cur_ref, prev_ref, w_ref, dout_ref, *rest):
            dc_ref, dw_ref = rest[-2:]
            i = pl.program_id(0)
            prev = prev_ref[...] * (i > 0).astype(F32)
            taps = _conv_taps(cur_ref[...], prev)
            w = w_ref[...]
            c = sum(taps[k] * w[k:k + 1, :] for k in range(4))
            _, vjp = jax.vjp(lambda cc: _qkv_post(cc, j, H), c)
            (dc,) = vjp(dout_ref[...])
            dc_ref[...] = dc

            @pl.when(i == 0)
            def _():
                dw_ref[...] = jnp.zeros_like(dw_ref)

            for k in range(4):
                dw_ref[k:k + 1, :] += jnp.sum(dc * taps[k], axis=0, keepdims=True)

        specs, operands, aliases = _alias_previous(prev_outs, 4)
        return pl.pallas_call(
            body, name="%s%d" % (name, j), grid=(T // tm,),
            in_specs=[pl.BlockSpec((tm, D), lambda i: (i, j)),
                      pl.BlockSpec((8, D), lambda i: (jnp.maximum(i * (tm // 8) - 1, 0), j)),
                      pl.BlockSpec((4, D), lambda i: (0, j)),
                      pl.BlockSpec((tm, D), lambda i: (i, j))] + specs,
            out_specs=[pl.BlockSpec((tm, D), lambda i: (i, j)), pl.BlockSpec((4, D), lambda i: (0, j))],
            out_shape=[jax.ShapeDtypeStruct((T, 3 * D), F32), jax.ShapeDtypeStruct((4, 3 * D), F32)],
            input_output_aliases=aliases, compiler_params=_params(1),
        )(p, p, conv_w, dqkv, *operands)

    return _column_calls(columns)


def _conv_bwd_input(dc, conv_w, name):
    T, D3 = dc.shape
    D = D3 // 3
    tm = _pick(T, (256, 128, 64))
    n_t = T // tm

    def body(cur_ref, next_ref, w_ref, dp_ref):
        i = pl.program_id(0)
        cur = cur_ref[...]
        nxt = next_ref[...] * (i < n_t - 1).astype(F32)
        ext = jnp.concatenate([cur, nxt], axis=0)
        w = w_ref[...]
        acc = cur * w[3:4, :]
        for s in (1, 2, 3):
            acc = acc + pltpu.roll(ext, tm + 8 - s, 0)[0:tm] * w[3 - s:4 - s, :]
        dp_ref[...] = acc.astype(BF16)

    return pl.pallas_call(
        body, name=name, grid=(n_t, 3),
        in_specs=[pl.BlockSpec((tm, D), lambda i, j: (i, j)),
                  pl.BlockSpec((8, D), lambda i, j: (jnp.minimum((i + 1) * (tm // 8), T // 8 - 1), j)),
                  pl.BlockSpec((4, D), lambda i, j: (0, j))],
        out_specs=pl.BlockSpec((tm, D), lambda i, j: (i, j)),
        out_shape=jax.ShapeDtypeStruct((T, D3), BF16), compiler_params=_params(2),
    )(dc, dc, conv_w)


def _comm_call(body, arrays, out_shape, n_remote, n_local, name):
    any_spec = pl.BlockSpec(memory_space=pl.ANY)
    return pl.pallas_call(
        body, name=name, in_specs=[any_spec] * len(arrays), out_specs=[any_spec] * len(out_shape), out_shape=out_shape,
        scratch_shapes=[pltpu.SemaphoreType.DMA((n_remote,)), pltpu.SemaphoreType.DMA((n_remote,)),
                        pltpu.SemaphoreType.DMA((n_local,))],
        compiler_params=pltpu.CompilerParams(has_side_effects=True),
    )(*arrays)


def _gather_shards(arrays, name):
    n = len(arrays)

    def body(*refs):
        ins, outs = refs[:n], refs[n:2 * n]
        send_sems, recv_sems, local_sems = refs[2 * n:]
        x, y, c = lax.axis_index("x"), lax.axis_index("y"), lax.axis_index("c")
        me, sibling = (x, y, c), (x, y, 1 - c)
        chips = [(1 - x, y), (x, 1 - y), (1 - x, 1 - y)]

        def copy(a, k, block, to, src=None):
            dst = outs[a].at[4 * block[0] + 2 * block[1] + block[2]]
            return pltpu.make_async_remote_copy(
                src_ref=dst if src is None else src, dst_ref=dst, send_sem=send_sems.at[a * 7 + k],
                recv_sem=recv_sems.at[a * 7 + k], device_id=to, device_id_type=MESH_ID)

        local = [pltpu.make_async_copy(ins[a], outs[a].at[4 * x + 2 * y + c], local_sems.at[a]) for a in range(n)]
        first = [copy(a, 0, me, sibling, src=ins[a]) for a in range(n)]
        first += [copy(a, 1 + j, me, (*chip, c), src=ins[a]) for j, chip in enumerate(chips) for a in range(n)]
        for cp in local + first:
            cp.start()
        passed = []
        for j, chip in enumerate(chips):
            for a in range(n):
                copy(a, 1 + j, (*chip, c), me).wait_recv()
                passed.append(copy(a, 4 + j, (*chip, c), sibling))
                passed[-1].start()
        for a in range(n):
            copy(a, 0, sibling, me).wait_recv()
            for j, chip in enumerate(chips):
                copy(a, 4 + j, (*chip, 1 - c), me).wait_recv()
        for cp in first + passed:
            cp.wait_send()
        for cp in local:
            cp.wait()

    out_shape = [jax.ShapeDtypeStruct((N_DEV,) + a.shape, a.dtype) for a in arrays]
    return _comm_call(body, arrays, out_shape, 7 * n, n, name)


def _pair_exchange(arrays, name):
    n = len(arrays)

    def body(*refs):
        ins, pair = refs[:n], refs[n:2 * n]
        send_sems, recv_sems, _ = refs[2 * n:]
        x, y, c = lax.axis_index("x"), lax.axis_index("y"), lax.axis_index("c")
        sends = [pltpu.make_async_remote_copy(
            src_ref=ins[a].at[1 - c], dst_ref=pair[a], send_sem=send_sems.at[a], recv_sem=recv_sems.at[a],
            device_id=(x, y, 1 - c), device_id_type=MESH_ID) for a in range(n)]
        for cp in sends:
            cp.start()
        for cp in sends:
            cp.wait_recv()
        for cp in sends:
            cp.wait_send()

    out_shape = [jax.ShapeDtypeStruct(a.shape[1:], a.dtype) for a in arrays]
    return _comm_call(body, arrays, out_shape, n, 1, name)


def _pair_add(own, pair, name, out_dtype):
    _, R, C = own.shape
    tr = next((t for t in (256, 128, 64, 32, 16) if R % t == 0), R)

    def body(a_ref, b_ref, o_ref):
        o_ref[...] = (a_ref[...].astype(F32) + b_ref[...].astype(F32)).astype(out_dtype)

    spec = pl.BlockSpec((1, tr, C), lambda q, i: (q, i, 0))
    return pl.pallas_call(body, name=name, grid=(4, R // tr), in_specs=[spec, spec], out_specs=spec,
                          out_shape=jax.ShapeDtypeStruct(own.shape, out_dtype), compiler_params=_params(2))(own, pair)


def _chip_exchange(arrays, name):
    n = len(arrays)

    def body(*refs):
        ins, outs = refs[:n], refs[n:2 * n]
        send_sems, recv_sems, local_sems = refs[2 * n:]
        x, y, c = lax.axis_index("x"), lax.axis_index("y"), lax.axis_index("c")
        my_chip = 2 * x + y
        chips = [(1 - x, y), (x, 1 - y), (1 - x, 1 - y)]
        local = [pltpu.make_async_copy(ins[a].at[my_chip], outs[a].at[my_chip], local_sems.at[a]) for a in range(n)]
        sends = [pltpu.make_async_remote_copy(
            src_ref=ins[a].at[2 * px + py], dst_ref=outs[a].at[my_chip], send_sem=send_sems.at[a * 3 + j],
            recv_sem=recv_sems.at[a * 3 + j], device_id=(px, py, c), device_id_type=MESH_ID)
            for j, (px, py) in enumerate(chips) for a in range(n)]
        arrivals = [pltpu.make_async_remote_copy(
            src_ref=ins[a].at[my_chip], dst_ref=outs[a].at[2 * px + py], send_sem=send_sems.at[a * 3 + j],
            recv_sem=recv_sems.at[a * 3 + j], device_id=(px, py, c), device_id_type=MESH_ID)
            for j, (px, py) in enumerate(chips) for a in range(n)]
        for cp in local + sends:
            cp.start()
        for cp in arrivals:
            cp.wait_recv()
        for cp in sends:
            cp.wait_send()
        for cp in local:
            cp.wait()

    out_shape = [jax.ShapeDtypeStruct(a.shape, a.dtype) for a in arrays]
    return _comm_call(body, arrays, out_shape, 3 * n, n, name)


def _reduce_adamw(recv, w, m, v, name):
    S, R, C = recv.shape
    tr = next((t for t in (256, 128, 64, 32, 16, 8) if R % t == 0), R)
    c1 = 1.0 - ADAM_B1 ** ADAM_STEP
    c2 = 1.0 - ADAM_B2 ** ADAM_STEP

    def body(r_ref, w_ref, m_ref, v_ref, g_ref, d_ref, nm_ref, nv_ref):
        g = r_ref[0].astype(F32)
        for s in range(1, S):
            g = g + r_ref[s].astype(F32)
        nm = ADAM_B1 * m_ref[...] + (1.0 - ADAM_B1) * g
        nv = ADAM_B2 * v_ref[...] + (1.0 - ADAM_B2) * (g * g)
        g_ref[...] = g
        nm_ref[...] = nm
        nv_ref[...] = nv
        d_ref[...] = -ADAM_LR * ((nm / c1) / (jnp.sqrt(nv / c2) + ADAM_EPS) + ADAM_WD * w_ref[...])

    spec = pl.BlockSpec((tr, C), lambda i: (i, 0))
    return pl.pallas_call(
        body, name=name, grid=(R // tr,),
        in_specs=[pl.BlockSpec((S, tr, C), lambda i: (0, i, 0)), spec, spec, spec], out_specs=[spec] * 4,
        out_shape=[jax.ShapeDtypeStruct((R, C), F32)] * 4, compiler_params=_params(1),
    )(recv, w, m, v)


def _forward_local(x, target, nw0, nw1, fw, wa_in, conv_w, gate_params, o_norm, wa_out, wb_in, wb_out):
    T, D = x.shape
    nD = D // LANES
    sv = {}
    sv["u0"] = _rmsnorm_fwd(x, nw0, "a_norm_fwd")
    sv["pa"] = _mm_nn(sv["u0"], wa_in, "a_in_proj")
    sv["qkv_a"] = _conv_fwd(sv["pa"], conv_w, D, "a_conv_fwd")
    sv["beta"], sv["gc"] = _gates_fwd(sv["pa"], gate_params, 4 * nD, "a_gates_fwd")
    sv["o_a"], sv["s_all"], sv["t_all"] = _chunk_fwd(sv["qkv_a"], sv["beta"], sv["gc"], D, "a_chunk_fwd")
    sv["y_a"] = _onorm_gate_fwd(sv["o_a"], sv["pa"], 3, o_norm, "a_onorm_fwd")
    sv["h1"] = _mm_nn(sv["y_a"], wa_out, "a_out_proj", add=x)
    sv["u1"] = _rmsnorm_fwd(sv["h1"], nw1, "b_norm_fwd")
    sv["qkv_b"] = _mm_nn(sv["u1"], wb_in[:, :3 * D], "b_in_proj_qkv", out_dtype=BF16)
    sv["gate_b"] = _mm_nn(sv["u1"], wb_in[:, 3 * D:], "b_in_proj_gate")
    sv["o_b"], sv["r_b"] = _sb_fwd(sv["qkv_b"], D, "b_attn_fwd")
    sv["y_b"] = _gate_mul_fwd(sv["o_b"], sv["gate_b"], "b_gate_fwd")
    sv["h2"] = _mm_nn(sv["y_b"], wb_out, "b_out_proj", add=sv["h1"])
    sv["dh2"], sv["loss"], sv["dfw"] = _final_loss(sv["h2"], fw, target, "final_loss")
    return sv


def _backward_local(sv, x, nw0, nw1, wa_in, conv_w, gate_params, o_norm, wa_out, wb_in, wb_out):
    T, D = x.shape
    nD = D // LANES
    g = {}
    dh2 = sv["dh2"]
    g["wb_out"] = _mm_tn(sv["y_b"], dh2, "b_out_proj_dw")
    dy_b = _mm_nt([(dh2, wb_out)], "b_out_proj_dx")
    do_b, dgate_b = _gate_mul_bwd(dy_b, sv["o_b"], sv["gate_b"], "b_gate_bwd")
    dq_b, dk_b, dv_b = _sb_bwd(sv["qkv_b"], do_b, sv["r_b"], D, "b_attn_bwd")
    dp_b = [dq_b, dk_b, dv_b, dgate_b]
    g["wb_in"] = jnp.concatenate([_mm_tn(sv["u1"], dp, "b_in_proj_dw%d" % c) for c, dp in enumerate(dp_b)], axis=1)
    du1 = _mm_nt([(dp, wb_in[:, c * D:(c + 1) * D]) for c, dp in enumerate(dp_b)], "b_in_proj_dx")
    dh1, g["nw1"] = _rmsnorm_bwd(sv["h1"], nw1, du1, dh2, "b_norm_bwd")
    g["wa_out"] = _mm_tn(sv["y_a"], dh1, "a_out_proj_dw")
    dy_a = _mm_nt([(dh1, wa_out)], "a_out_proj_dx")
    do_a, dz_a, g["o_norm"] = _onorm_gate_bwd(dy_a, sv["o_a"], sv["pa"], 3, o_norm, "a_onorm_bwd")
    dqkv_a, dbeta, dg = _chunk_bwd(sv["qkv_a"], sv["beta"], sv["gc"], sv["s_all"], sv["t_all"], do_a, D, "a_chunk_bwd")
    dp_gates, g["gate_params"] = _gates_bwd(sv["pa"], gate_params, 4 * nD, dbeta, dg, "a_gates_bwd")
    dc, g["conv_w"] = _conv_bwd_act(sv["pa"], conv_w, dqkv_a, D, "a_conv_bwd_act")
    dp_qkv = _conv_bwd_input(dc, conv_w, "a_conv_bwd_input")
    dp_a = [(dp_qkv, 0, 3 * D), (dz_a, 3 * D, 4 * D), (dp_gates, 4 * D, 4 * D + 2 * LANES)]
    g["wa_in"] = jnp.concatenate([_mm_tn(sv["u0"], dp, "a_in_proj_dw%d" % c) for c, (dp, _, _) in enumerate(dp_a)], axis=1)
    du0 = _mm_nt([(dp, wa_in[:, lo:hi]) for dp, lo, hi in dp_a], "a_in_proj_dx")
    g["x"], g["nw0"] = _rmsnorm_bwd(x, nw0, du0, dh1, "a_norm_bwd")
    g["fw"] = sv["dfw"]
    return g


def kernel(x, norm_w, a_w_in, a_conv_w, a_a_log, a_dt_bias, a_o_norm, a_w_out, b_w_in, b_w_out, final_norm_w, loss_target, m_norm_w, m_a_w_in, m_a_conv_w, m_a_a_log, m_a_dt_bias, m_a_o_norm, m_a_w_out, m_b_w_in, m_b_w_out, m_final_norm_w, v_norm_w, v_a_w_in, v_a_conv_w, v_a_a_log, v_a_dt_bias, v_a_o_norm, v_a_w_out, v_b_w_in, v_b_w_out, v_final_norm_w):
    D = x.shape[-1]
    H = D // HEAD_DIM
    shards = [a_w_in[0].astype(BF16), a_w_out[0].astype(BF16), b_w_in[0].astype(BF16), b_w_out[0].astype(BF16), a_conv_w[0]]
    ga_in, ga_out, gb_in, gb_out, g_conv = _gather_shards(shards, "weights_gather")
    wa = ga_in.transpose(1, 0, 2).reshape(D, -1)
    pad = lambda w: jnp.pad(w, ((0, 0), (0, LANES - w.shape[1])))
    wa_in = jnp.concatenate([wa[:, :4 * D], pad(wa[:, 4 * D:4 * D + H]), pad(wa[:, 4 * D + H:])], axis=1)
    wa_out = ga_out.reshape(D, D)
    wb_in = gb_in.transpose(1, 0, 2).reshape(D, 4 * D)
    wb_out = gb_out.reshape(D, D)
    conv_w = g_conv.transpose(1, 0, 2).reshape(4, 3 * D)
    gate_params = jnp.zeros((8, LANES), F32).at[0, :H].set(a_a_log[0]).at[1, :H].set(a_dt_bias[0])
    nw0, nw1, fw = norm_w[0:1], norm_w[1:2], final_norm_w[None]

    sv = _forward_local(x[0], loss_target[0], nw0, nw1, fw, wa_in, conv_w, gate_params, a_o_norm, wa_out, wb_in, wb_out)
    g = _backward_local(sv, x[0], nw0, nw1, wa_in, conv_w, gate_params, a_o_norm, wa_out, wb_in, wb_out)

    gwa = g["wa_in"]
    gwa = jnp.concatenate([gwa[:, :4 * D], gwa[:, 4 * D:4 * D + H], gwa[:, 4 * D + LANES:4 * D + LANES + H]], axis=1)
    row = lambda v: jnp.pad(v.reshape(1, -1), ((0, 0), (0, D - v.size)))
    small = jnp.concatenate([g["nw0"][0:1], g["nw1"][0:1], g["fw"][0:1], row(g["gate_params"][0, :H]),
                             row(g["gate_params"][1, :H]), row(g["o_norm"][0]), row(sv["loss"][0, 0:1]),
                             jnp.zeros((1, D), F32)], axis=0)
    cols = lambda a: a.astype(BF16).reshape(a.shape[0], 4, 2, -1).transpose(2, 1, 0, 3)
    rows = lambda a: a.astype(BF16).reshape(4, 2, -1, a.shape[1]).transpose(1, 0, 2, 3)
    contribs = [cols(gwa), rows(g["wa_out"]), cols(g["wb_in"]), rows(g["wb_out"]), cols(g["conv_w"]),
                jnp.broadcast_to(small[None, None], (2, 4, 8, D))]
    pair = _pair_exchange(contribs, "grads_pair_exchange")
    my_core = lax.axis_index("c")
    own = [lax.dynamic_index_in_dim(a, my_core, axis=0, keepdims=False) for a in contribs]
    partial = [_pair_add(o, p, "grads_pair_add%d" % k, o.dtype) for k, (o, p) in enumerate(zip(own, pair))]
    ra_in, ra_out, rb_in, rb_out, r_conv, r_small = _chip_exchange(partial, "grads_chip_exchange")

    outs = {}
    for nm, recv, w, m, v in (("a_w_in", ra_in, a_w_in, m_a_w_in, v_a_w_in), ("a_w_out", ra_out, a_w_out, m_a_w_out, v_a_w_out),
                              ("b_w_in", rb_in, b_w_in, m_b_w_in, v_b_w_in), ("b_w_out", rb_out, b_w_out, m_b_w_out, v_b_w_out),
                              ("a_conv_w", r_conv, a_conv_w, m_a_conv_w, v_a_conv_w)):
        outs[nm] = tuple(o[None] for o in _reduce_adamw(recv, w[0], m[0], v[0], "adamw_" + nm))

    def pack(nw, alog, dt, onorm, fnw):
        return jnp.concatenate([nw, fnw.reshape(1, D), row(alog), row(dt), row(onorm), jnp.zeros((2, D), F32)], axis=0)

    s_g, s_d, s_m, s_v = _reduce_adamw(
        r_small, pack(norm_w, a_a_log, a_dt_bias, a_o_norm, final_norm_w),
        pack(m_norm_w, m_a_a_log, m_a_dt_bias, m_a_o_norm, m_final_norm_w),
        pack(v_norm_w, v_a_a_log, v_a_dt_bias, v_a_o_norm, v_final_norm_w), "adamw_small")
    loss = s_g[6, 0]
    for i, s in enumerate((s_g, s_d, s_m, s_v)):
        outs.setdefault("norm_w", [None] * 4)[i] = s[0:2]
        outs.setdefault("final_norm_w", [None] * 4)[i] = s[2]
        outs.setdefault("a_a_log", [None] * 4)[i] = s[3:4, :H]
        outs.setdefault("a_dt_bias", [None] * 4)[i] = s[4:5, :H]
        outs.setdefault("a_o_norm", [None] * 4)[i] = s[5:6, :HEAD_DIM]
    names = ("norm_w", "a_w_in", "a_conv_w", "a_a_log", "a_dt_bias", "a_o_norm", "a_w_out", "b_w_in", "b_w_out", "final_norm_w")
    return (loss, g["x"][None]) + tuple(outs[n][i] for i in range(4) for n in names)
```

```python
import functools

import jax
import jax.numpy as jnp
from jax import lax
from jax.experimental import pallas as pl
from jax.experimental.pallas import tpu as pltpu

F32 = jnp.float32
BF16 = jnp.bfloat16
EPS = 1e-6
LOG2_E = 1.4426950408889634
MASKED_SCORE = -1e30
HEAD_DIM = 128
CHUNK = 64
CHUNKS_PER_STEP = 4
ATTN_Q_BLOCKS_FWD = (512, 256)
ATTN_Q_BLOCKS_BWD = (512, 256)
ATTN_K_BLOCK = 128
LANES = 128
N_DEV = 8
VMEM_LIMIT_BYTES = 48 * 1024 * 1024
ADAM_LR, ADAM_B1, ADAM_B2, ADAM_EPS, ADAM_WD, ADAM_STEP = 0.001, 0.9, 0.999, 1e-08, 0.01, 10
MESH_ID = pl.DeviceIdType.MESH


def _pick(n, candidates):
    for c in candidates:
        if n % c == 0:
            return c
    raise ValueError(f"no tile for {n} in {candidates}")


def _params(n_grid_axes):
    return pltpu.CompilerParams(dimension_semantics=("arbitrary",) * n_grid_axes, vmem_limit_bytes=VMEM_LIMIT_BYTES)


def _dot(a, b):
    return jnp.dot(a.astype(BF16), b.astype(BF16), preferred_element_type=F32)


def _dot_nt(a, b):
    return lax.dot_general(a.astype(BF16), b.astype(BF16), (((1,), (1,)), ((), ())), preferred_element_type=F32)


def _dot_tn(a, b):
    return lax.dot_general(a.astype(BF16), b.astype(BF16), (((0,), (0,)), ((), ())), preferred_element_type=F32)


def _split2(x):
    hi = x.astype(BF16)
    lo = (x - hi.astype(F32)).astype(BF16)
    return hi, lo


def _split3(x):
    hi = x.astype(BF16)
    r = x - hi.astype(F32)
    mid = r.astype(BF16)
    lo = (r - mid.astype(F32)).astype(BF16)
    return hi, mid, lo


def _dot3(a, b):
    a_hi, a_lo = _split2(a)
    b_hi, b_lo = _split2(b)
    d = functools.partial(jnp.dot, preferred_element_type=F32)
    return d(a_hi, b_hi) + (d(a_hi, b_lo) + d(a_lo, b_hi))


def _silu(x):
    return x * jax.nn.sigmoid(x)


def _softplus(x):
    return jnp.maximum(x, 0.0) + jnp.log1p(jnp.exp(-jnp.abs(x)))


def _iota2(shape, axis):
    return lax.broadcasted_iota(jnp.int32, shape, axis)


def _rms_bwd_math(x, w, dy):
    r = lax.rsqrt(jnp.mean(x * x, axis=-1, keepdims=True) + EPS)
    xhat = x * r
    dxhat = dy * w
    dx = r * (dxhat - xhat * jnp.mean(dxhat * xhat, axis=-1, keepdims=True))
    dw = jnp.sum(dy * xhat, axis=0, keepdims=True)
    return dx, dw


def _rmsnorm_fwd(x, w, name):
    T, D = x.shape
    tm = _pick(T, (512, 256, 128))

    def body(x_ref, w_ref, o_ref):
        xf = x_ref[...]
        r = lax.rsqrt(jnp.mean(xf * xf, axis=-1, keepdims=True) + EPS)
        o_ref[...] = (xf * r * w_ref[...]).astype(BF16)

    return pl.pallas_call(
        body, name=name, grid=(T // tm,),
        in_specs=[pl.BlockSpec((tm, D), lambda i: (i, 0)), pl.BlockSpec((1, D), lambda i: (0, 0))],
        out_specs=pl.BlockSpec((tm, D), lambda i: (i, 0)),
        out_shape=jax.ShapeDtypeStruct((T, D), BF16), compiler_params=_params(1),
    )(x, w)


def _mm_nn(a, b, name, add=None, out_dtype=F32):
    M, K = a.shape
    _, N = b.shape
    tm = _pick(M, (256, 128)) if N > 2048 else _pick(M, (512, 256, 128))

    def body(*refs):
        a_ref, b_ref = refs[0], refs[1]
        o_ref = refs[-1]
        acc = _dot(a_ref[...], b_ref[...])
        if add is not None:
            acc = acc + refs[2][...]
        o_ref[...] = acc.astype(out_dtype)

    in_specs = [pl.BlockSpec((tm, K), lambda i: (i, 0)), pl.BlockSpec((K, N), lambda i: (0, 0))]
    args = [a, b]
    if add is not None:
        in_specs.append(pl.BlockSpec((tm, N), lambda i: (i, 0)))
        args.append(add)
    return pl.pallas_call(
        body, name=name, grid=(M // tm,), in_specs=in_specs,
        out_specs=pl.BlockSpec((tm, N), lambda i: (i, 0)),
        out_shape=jax.ShapeDtypeStruct((M, N), out_dtype), compiler_params=_params(1),
    )(*args)


def _mm_nt(pairs, name):
    def write(acc, ins, outs):
        outs[0][...] = acc

    return _mm_nt_then(pairs, name, [], [], write,
                       [(jax.ShapeDtypeStruct((pairs[0][0].shape[0], pairs[0][1].shape[0]), F32), True)])


def _mm_nt_then(pairs, name, extras, extra_cols, tail, outs):
    M = pairs[0][0].shape[0]
    n, e = len(pairs), len(extras)
    tm = _pick(M, (256, 128)) if e else _pick(M, (512, 256, 128))

    def body(*refs):
        acc = _dot_nt(refs[0][...], refs[n][...])
        for p in range(1, n):
            acc = acc + _dot_nt(refs[p][...], refs[n + p][...])
        tail(acc, refs[2 * n:2 * n + e], refs[2 * n + e:])

    def extra_spec(x, col):
        if col is None:
            return pl.BlockSpec(x.shape, lambda i: (0,) * x.ndim)
        return pl.BlockSpec((tm, outs[0][0].shape[1]), lambda i: (i, col))

    in_specs = ([pl.BlockSpec((tm, a.shape[1]), lambda i: (i, 0)) for a, _ in pairs]
                + [pl.BlockSpec(b.shape, lambda i: (0, 0)) for _, b in pairs]
                + [extra_spec(x, col) for x, col in zip(extras, extra_cols)])
    out_specs = [pl.BlockSpec((tm, s.shape[1]), lambda i: (i, 0)) if tiled else pl.BlockSpec(s.shape, lambda i: (0, 0))
                 for s, tiled in outs]
    res = pl.pallas_call(
        body, name=name, grid=(M // tm,), in_specs=in_specs, out_specs=out_specs,
        out_shape=[s for s, _ in outs], compiler_params=_params(1),
    )(*[a for a, _ in pairs], *[b for _, b in pairs], *extras)
    return res[0] if len(outs) == 1 else res


def _mm_tn(a, b, name):
    R, M = a.shape
    _, N = b.shape
    tn = _pick(N, (1536, 1024, 512, 256, 128))
    tr = _pick(R, (512, 256, 128))

    def body(a_ref, b_ref, o_ref):
        @pl.when(pl.program_id(1) == 0)
        def _():
            o_ref[...] = jnp.zeros_like(o_ref)

        o_ref[...] += _dot_tn(a_ref[...], b_ref[...])

    return pl.pallas_call(
        body, name=name, grid=(N // tn, R // tr),
        in_specs=[pl.BlockSpec((tr, M), lambda j, r: (r, 0)), pl.BlockSpec((tr, tn), lambda j, r: (r, j))],
        out_specs=pl.BlockSpec((M, tn), lambda j, r: (0, j)),
        out_shape=jax.ShapeDtypeStruct((M, N), F32), compiler_params=_params(2),
    )(a, b)


def _qkv_post(c, j, n_heads):
    s = _silu(c)
    if j == 2:
        return s
    parts = []
    for h in range(n_heads):
        sh = s[:, h * HEAD_DIM:(h + 1) * HEAD_DIM]
        parts.append(sh * lax.rsqrt(jnp.sum(sh * sh, axis=-1, keepdims=True) + EPS))
    n = jnp.concatenate(parts, axis=-1)
    return n * (HEAD_DIM ** -0.5) if j == 0 else n


def _column_calls(make_call, n_cols=3):
    outs = None
    for j in range(n_cols):
        outs = make_call(j, outs)
    return outs


def _alias_previous(prev, n_inputs):
    if prev is None:
        return [], [], {}
    prev = list(prev) if isinstance(prev, (list, tuple)) else [prev]
    return ([pl.BlockSpec(memory_space=pl.ANY)] * len(prev), prev, {n_inputs + k: k for k in range(len(prev))})


def _conv_taps(cur, halo_prev):
    tm = cur.shape[0]
    ext = jnp.concatenate([halo_prev, cur], axis=0)
    taps = [pltpu.roll(ext, s, 0)[8:8 + tm] for s in (3, 2, 1)]
    return taps + [cur]


def _conv_fwd(p, conv_w, d_model, name):
    T = p.shape[0]
    D = d_model
    H = D // HEAD_DIM
    tm = _pick(T, (256, 128, 64))

    def columns(j, prev_out):
        def body(cur_ref, prev_ref, w_ref, *rest):
            o_ref = rest[-1]
            prev = prev_ref[...] * (pl.program_id(0) > 0).astype(F32)
            taps = _conv_taps(cur_ref[...], prev)
            w = w_ref[...]
            c = sum(taps[k] * w[k:k + 1, :] for k in range(4))
            o_ref[...] = _qkv_post(c, j, H)

        specs, operands, aliases = _alias_previous(prev_out, 3)
        return pl.pallas_call(
            body, name="%s%d" % (name, j), grid=(T // tm,),
            in_specs=[pl.BlockSpec((tm, D), lambda i: (i, j)),
                      pl.BlockSpec((8, D), lambda i: (jnp.maximum(i * (tm // 8) - 1, 0), j)),
                      pl.BlockSpec((4, D), lambda i: (0, j))] + specs,
            out_specs=pl.BlockSpec((tm, D), lambda i: (i, j)),
            out_shape=jax.ShapeDtypeStruct((T, 3 * D), F32), input_output_aliases=aliases, compiler_params=_params(1),
        )(p, p, conv_w, *operands)

    return _column_calls(columns)


def _chunk_tri(tm, upper):
    r, c = _iota2((tm, tm), 0), _iota2((tm, tm), 1)
    same = (r // CHUNK) == (c // CHUNK)
    tri = (c >= r) if upper else (c <= r)
    return jnp.where(same & tri, 1.0, 0.0).astype(BF16)


def _dot_mask(mask_bf16, x):
    hi, mid, lo = _split3(x)
    d = functools.partial(jnp.dot, preferred_element_type=F32)
    return d(mask_bf16, hi) + (d(mask_bf16, mid) + d(mask_bf16, lo))


def _gates_math(pb, pa, a_log, dt_bias):
    beta = jax.nn.sigmoid(pb)
    g = -jnp.exp(a_log) * _softplus(pa + dt_bias)
    return beta, g


def _gates_fwd(p, gate_params, col0, name):
    T = p.shape[0]
    tm = _pick(T, (256, 128, 64))

    def body(pb_ref, pa_ref, gp_ref, beta_ref, gc_ref):
        gp = gp_ref[...]
        beta, g = _gates_math(pb_ref[...], pa_ref[...], gp[0:1, :], gp[1:2, :])
        beta_ref[...] = beta
        gc_ref[...] = _dot_mask(_chunk_tri(tm, upper=False), g)

    return pl.pallas_call(
        body, name=name, grid=(T // tm,),
        in_specs=[pl.BlockSpec((tm, LANES), lambda i: (i, col0)), pl.BlockSpec((tm, LANES), lambda i: (i, col0 + 1)),
                  pl.BlockSpec((8, LANES), lambda i: (0, 0))],
        out_specs=[pl.BlockSpec((tm, LANES), lambda i: (i, 0))] * 2,
        out_shape=[jax.ShapeDtypeStruct((T, LANES), F32)] * 2, compiler_params=_params(1),
    )(p, p, gate_params)


def _col_to_row(col):
    C = col.shape[0]
    eye = _iota2((C, C), 0) == _iota2((C, C), 1)
    return jnp.sum(jnp.where(eye, col, 0.0), axis=0, keepdims=True)


def _lockstep(groups, skew, finish):
    groups = [list(g) for g in groups]
    results = [[None] * len(g) for g in groups]
    left = [len(g) for g in groups]
    rnd = 0
    while any(left):
        for gi, gens in enumerate(groups):
            if rnd < gi * skew or not left[gi]:
                continue
            for idx, gen in enumerate(gens):
                if gen is None:
                    continue
                try:
                    next(gen)
                except StopIteration as done:
                    results[gi][idx] = done.value
                    gens[idx] = None
                    left[gi] -= 1
            if not left[gi]:
                finish(gi, results[gi])
        rnd += 1


def _unit_lower_inverse(low):
    C = low.shape[0]
    eye = (_iota2((C, C), 0) == _iota2((C, C), 1)).astype(F32)
    t = eye - low
    p = _dot3(low, low)
    yield
    n = 2
    while True:
        tp = _dot3(t, p)
        n *= 2
        if n < C:
            p = _dot3(p, p)
        yield
        t = t + tp
        if n >= C:
            return t


FWD_STATE_SKEW = 3


def _chunk_head_fwd(load, take_state, give):
    q, k, v, gc, beta = load()
    C = q.shape[0]
    r, c = _iota2((C, C), 0), _iota2((C, C), 1)
    causal, strict = r >= c, r > c
    decay = jnp.where(causal, jnp.exp(jnp.where(causal, gc - _col_to_row(gc), 0.0)), 0.0)
    kb, vb = k * beta, v * beta
    eg = jnp.exp(gc)
    kk = _dot_nt(kb, k)
    qk = _dot_nt(q, k)
    yield
    t_inv = yield from _unit_lower_inverse(jnp.where(strict, kk * decay, 0.0))
    give(t_inv=t_inv)
    u = _dot(t_inv, vb)
    w = _dot(t_inv, kb * eg)
    yield
    s_in = take_state()
    give(s_in=s_in)
    o_state = _dot(q * eg, s_in)
    w_state = _dot(w, s_in)
    yield
    v_new = u - w_state
    g_last = gc[C - 1:C, :]
    o_intra = _dot(qk * decay, v_new)
    s_add = _dot_tn(k * jnp.exp(g_last - gc), v_new)
    yield
    give(o=o_state + o_intra, s_out=s_in * jnp.exp(g_last) + s_add)


def _chunk_fwd(qkv, beta, gc, d_model, name):
    T = qkv.shape[0]
    D = d_model
    H = D // HEAD_DIM
    N = T // CHUNK
    G = CHUNKS_PER_STEP
    R = G * CHUNK

    def body(q_ref, k_ref, v_ref, beta_ref, gc_ref, o_ref, s_all_ref, t_all_ref, s_ref):
        @pl.when(pl.program_id(0) == 0)
        def _():
            s_ref[...] = jnp.zeros_like(s_ref)

        heads = [slice(h * HEAD_DIM, (h + 1) * HEAD_DIM) for h in range(H)]
        states = [[s_ref[h] for h in range(H)]] + [[None] * H for _ in range(G)]

        def head(sub, h):
            rows = slice(sub * CHUNK, (sub + 1) * CHUNK)

            def give(t_inv=None, s_in=None, o=None, s_out=None):
                if t_inv is not None:
                    t_all_ref[sub, h] = t_inv
                if s_in is not None:
                    s_all_ref[sub, h] = s_in
                if o is not None:
                    o_ref[rows, heads[h]] = o
                    states[sub + 1][h] = s_out

            load = lambda: (q_ref[rows, heads[h]], k_ref[rows, heads[h]], v_ref[rows, heads[h]],
                            gc_ref[rows, h:h + 1], beta_ref[rows, h:h + 1])
            return _chunk_head_fwd(load, lambda: states[sub][h], give)

        _lockstep([[head(sub, h) for h in range(H)] for sub in range(G)], FWD_STATE_SKEW, lambda sub, results: None)
        for h in range(H):
            s_ref[h] = states[G][h]

    return pl.pallas_call(
        body, name=name, grid=(N // G,),
        in_specs=[pl.BlockSpec((R, D), lambda n: (n, 0)), pl.BlockSpec((R, D), lambda n: (n, 1)),
                  pl.BlockSpec((R, D), lambda n: (n, 2)),
                  pl.BlockSpec((R, LANES), lambda n: (n, 0)), pl.BlockSpec((R, LANES), lambda n: (n, 0))],
        out_specs=[pl.BlockSpec((R, D), lambda n: (n, 0)),
                   pl.BlockSpec((G, H, HEAD_DIM, HEAD_DIM), lambda n: (n, 0, 0, 0)),
                   pl.BlockSpec((G, H, CHUNK, CHUNK), lambda n: (n, 0, 0, 0))],
        out_shape=[jax.ShapeDtypeStruct((T, D), F32), jax.ShapeDtypeStruct((N, H, HEAD_DIM, HEAD_DIM), F32),
                   jax.ShapeDtypeStruct((N, H, CHUNK, CHUNK), F32)],
        scratch_shapes=[pltpu.VMEM((H, HEAD_DIM, HEAD_DIM), F32)], compiler_params=_params(1),
    )(qkv, qkv, qkv, beta, gc)


def _onorm_gate_math(o, z, w, n_heads):
    parts = []
    for h in range(n_heads):
        hs = slice(h * HEAD_DIM, (h + 1) * HEAD_DIM)
        oh = o[:, hs]
        y = oh * lax.rsqrt(jnp.mean(oh * oh, axis=-1, keepdims=True) + EPS) * w
        parts.append(y * _silu(z[:, hs]))
    return jnp.concatenate(parts, axis=-1)


def _onorm_gate_fwd(o, p, z_col, o_norm, name):
    T, D = o.shape
    H = D // HEAD_DIM
    tm = _pick(T, (256, 128, 64))

    def body(o_ref, z_ref, w_ref, y_ref):
        y_ref[...] = _onorm_gate_math(o_ref[...], z_ref[...], w_ref[...], H).astype(BF16)

    return pl.pallas_call(
        body, name=name, grid=(T // tm,),
        in_specs=[pl.BlockSpec((tm, D), lambda i: (i, 0)), pl.BlockSpec((tm, D), lambda i: (i, z_col)),
                  pl.BlockSpec((1, HEAD_DIM), lambda i: (0, 0))],
        out_specs=pl.BlockSpec((tm, D), lambda i: (i, 0)),
        out_shape=jax.ShapeDtypeStruct((T, D), BF16), compiler_params=_params(1),
    )(o, p, o_norm)


def _diag_mask(qb, kb, d):
    return _iota2((qb, kb), 0) > _iota2((qb, kb), 1) + d * kb


def _run_trips(trip, n, state):
    def six(j, st):
        for u in range(6):
            st = trip(6 * j + u, st, u)
        return st

    state = lax.fori_loop(0, n // 6, six, state)
    base = (n // 6) * 6
    for u in (0, 2):
        pair = lambda st, u=u: trip(base + u + 1, trip(base + u, st, u), u + 1)
        state = lax.cond(n - base > u, pair, lambda st: st, state)
    return state


def _fill_score_masks(mask_buf, qb, kb, ns):
    mask_buf[0] = jnp.zeros(mask_buf.shape[1:], F32)
    for d in range(ns):
        for half in range(2):
            mask_buf[d + 1, :, half * kb:(half + 1) * kb] = jnp.where(_diag_mask(qb, kb, 2 * d + half), 0.0, MASKED_SCORE)


def _softplus_bits(w):
    u = 1.0 + jnp.exp2(jnp.minimum(w, 64.0))
    return jnp.maximum(w, jnp.log2(u)), 1.0 / u


def _incl_lower(n):
    return jnp.where((_iota2((2 * n, n), 0) & (n - 1)) >= _iota2((2 * n, n), 1), 1.0, 0.0).astype(BF16)


def _incl_upper(n):
    return jnp.where((_iota2((2 * n, n), 0) & (n - 1)) <= _iota2((2 * n, n), 1), 1.0, 0.0).astype(BF16)


def _dot_cum(x, tri_bf16):
    hi, lo = _split2(x)
    return jnp.dot(jnp.concatenate([hi, lo], axis=1), tri_bf16, preferred_element_type=F32)


def _sb_fwd(qkv, d_model, name):
    T = qkv.shape[0]
    D = d_model
    H = D // HEAD_DIM
    QB = _pick(T, ATTN_Q_BLOCKS_FWD)
    KB = ATTN_K_BLOCK
    KS = 2 * KB
    ns = QB // KS
    nq = T // QB
    scale = HEAD_DIM ** -0.5

    def body(q_ref, k_ref, v_ref, o_ref, r_ref, w0, w1, w2, cum0, cum1, mask_buf):
        w_bufs, cum_bufs = (w0, w1, w2), (cum0, cum1)
        i = pl.program_id(1)

        @pl.when(i == 0)
        def _():
            _fill_score_masks(mask_buf, QB, KB, ns)

        q = q_ref[...]
        tri = _incl_lower(KB)
        n_tot = (i + 1) * ns

        def key_step(m):
            return jnp.maximum(n_tot - 1 - m, 0)

        def rows(ref, s):
            return ref[pl.ds(pl.multiple_of(s * KS, KS), KS), :]

        def scores(s, may_be_diagonal):
            w = _dot_nt(q, rows(k_ref, s)) * (scale * LOG2_E)
            return w + mask_buf[jnp.maximum(s - i * ns + 1, 0)] if may_be_diagonal else w

        def cums(w):
            sp = _softplus_bits(w)[0]
            return jnp.concatenate([_dot_cum(sp[:, :KB], tri), _dot_cum(sp[:, KB:], tri)], axis=1)

        def weights(w, cum, carry):
            a_r = jnp.exp2(w[:, KB:] - cum[:, KB:] - carry)
            carry = carry + cum[:, KB:KB + 1]
            a_l = jnp.exp2(w[:, :KB] - cum[:, :KB] - carry)
            return jnp.concatenate([a_l, a_r], axis=1).astype(BF16), carry + cum[:, 0:1]

        def trip(m, carry, ph):
            w_bufs[(ph + 2) % 3][...] = scores(key_step(m + 2), False)
            a, carry = weights(w_bufs[ph % 3][...], cum_bufs[ph % 2][...], carry)
            o_ref[...] += _dot(a, rows(v_ref, key_step(m)))
            cum_bufs[(ph + 1) % 2][...] = cums(w_bufs[(ph + 1) % 3][...])
            return carry

        o_ref[...] = jnp.zeros_like(o_ref)
        assert ns == 2
        w_bufs[0][...] = scores(key_step(0), True)
        w_bufs[1][...] = scores(key_step(1), True)
        cum_bufs[0][...] = cums(w_bufs[0][...])
        carry = _run_trips(trip, n_tot, jnp.zeros((QB, 1), F32))
        r_ref[0] = jnp.broadcast_to(carry, (QB, LANES))

    return pl.pallas_call(
        body, name=name, grid=(H, nq),
        in_specs=[pl.BlockSpec((QB, HEAD_DIM), lambda h, i: (i, h)),
                  pl.BlockSpec((T, HEAD_DIM), lambda h, i: (0, H + h)),
                  pl.BlockSpec((T, HEAD_DIM), lambda h, i: (0, 2 * H + h))],
        out_specs=[pl.BlockSpec((QB, HEAD_DIM), lambda h, i: (i, h)),
                   pl.BlockSpec((1, QB, LANES), lambda h, i: (h, i, 0))],
        out_shape=[jax.ShapeDtypeStruct((T, D), F32), jax.ShapeDtypeStruct((H, T, LANES), F32)],
        scratch_shapes=[pltpu.VMEM((QB, KS), F32)] * 5 + [pltpu.VMEM((ns + 1, QB, KS), F32)],
        compiler_params=_params(2),
    )(qkv, qkv, qkv)


def _gate_mul_fwd(o, gate, name):
    T, D = o.shape
    tm = _pick(T, (512, 256, 128))

    def body(o_ref, g_ref, y_ref):
        y_ref[...] = (o_ref[...] * _silu(g_ref[...])).astype(BF16)

    spec = pl.BlockSpec((tm, D), lambda i: (i, 0))
    return pl.pallas_call(body, name=name, grid=(T // tm,), in_specs=[spec, spec], out_specs=spec,
                          out_shape=jax.ShapeDtypeStruct((T, D), BF16), compiler_params=_params(1))(o, gate)


def _final_loss(h, w, target, name):
    T, D = h.shape
    tm = _pick(T, (512, 256, 128))

    def body(h_ref, w_ref, t_ref, dh_ref, loss_ref, dw_ref):
        x, w = h_ref[...], w_ref[...]
        r = lax.rsqrt(jnp.mean(x * x, axis=-1, keepdims=True) + EPS)
        err = x * r * w - t_ref[...]
        part = 0.5 * jnp.sum(jnp.mean(err * err, axis=-1, keepdims=True), axis=0, keepdims=True)
        dx, dw = _rms_bwd_math(x, w, err * (1.0 / D))
        dh_ref[...] = dx

        @pl.when(pl.program_id(0) == 0)
        def _():
            loss_ref[...] = jnp.zeros_like(loss_ref)
            dw_ref[...] = jnp.zeros_like(dw_ref)

        loss_ref[...] += jnp.broadcast_to(part, loss_ref.shape)
        dw_ref[...] += jnp.broadcast_to(dw, dw_ref.shape)

    return pl.pallas_call(
        body, name=name, grid=(T // tm,),
        in_specs=[pl.BlockSpec((tm, D), lambda i: (i, 0)), pl.BlockSpec((1, D), lambda i: (0, 0)),
                  pl.BlockSpec((tm, D), lambda i: (i, 0))],
        out_specs=[pl.BlockSpec((tm, D), lambda i: (i, 0)), pl.BlockSpec((8, LANES), lambda i: (0, 0)),
                   pl.BlockSpec((8, D), lambda i: (0, 0))],
        out_shape=[jax.ShapeDtypeStruct((T, D), F32), jax.ShapeDtypeStruct((8, LANES), F32),
                   jax.ShapeDtypeStruct((8, D), F32)],
        compiler_params=_params(1),
    )(h, w, target)


def _sb_bwd(qkv, do, r_tot, d_model, name):
    T = qkv.shape[0]
    D = d_model
    H = D // HEAD_DIM
    QB = _pick(T, ATTN_Q_BLOCKS_BWD)
    KB = ATTN_K_BLOCK
    KS = 2 * KB
    ns = QB // KS
    nq = T // QB
    n_key_steps = T // KS
    scale = HEAD_DIM ** -0.5

    def body(q_ref, k_ref, v_ref, do_ref, r_ref, dq_ref, dk_ref, dv_ref,
             dkt_acc, dvt_acc, dq_acc, w0, w1, w2, da0, da1, da2, cum0, cum1, sig0, sig1, mask_buf):
        w_bufs, da_bufs, cum_bufs, sig_bufs = (w0, w1, w2), (da0, da1, da2), (cum0, cum1), (sig0, sig1)
        i = pl.program_id(1)

        @pl.when(i == 0)
        def _():
            dkt_acc[...] = jnp.zeros_like(dkt_acc)
            dvt_acc[...] = jnp.zeros_like(dvt_acc)
            _fill_score_masks(mask_buf, QB, KB, ns)

        q = q_ref[...]
        do_blk = do_ref[...].astype(BF16)
        q_t = q.astype(F32).T.astype(BF16)
        do_t = do_ref[...].T.astype(BF16)
        row_total = r_ref[0][:, 0:1]
        tri_rev = _incl_lower(KB)
        tri_fwd = jnp.where(_iota2((KB, KB), 0) <= _iota2((KB, KB), 1), 1.0, 0.0).astype(BF16)
        n_tot = (i + 1) * ns

        def step_rows(ref, s):
            return ref[pl.ds(pl.multiple_of(s * KS, KS), KS), :]

        def scores(s):
            w = _dot_nt(q, step_rows(k_ref, s)) * (scale * LOG2_E) + mask_buf[jnp.maximum(s - i * ns + 1, 0)]
            return w, _dot_nt(do_blk, step_rows(v_ref, s))

        def softplus_sums(w):
            sp, one_minus_sig = _softplus_bits(w)
            cum = jnp.concatenate([_dot_cum(sp[:, :KB], tri_rev), _dot_cum(sp[:, KB:], tri_rev)], axis=1)
            return cum, 1.0 - one_minus_sig

        def weights(w, cum, da, left_sp):
            right_l = row_total - left_sp - cum[:, 0:1]
            right_r = right_l - cum[:, KB:KB + 1]
            a = jnp.concatenate([jnp.exp2(w[:, :KB] - cum[:, :KB] - right_l),
                                 jnp.exp2(w[:, KB:] - cum[:, KB:] - right_r)], axis=1)
            p = da * a
            cp = jnp.concatenate([_dot(p[:, :KB], tri_fwd), _dot(p[:, KB:], tri_fwd)], axis=1)
            return a.astype(BF16), p, cp, row_total - right_r

        def score_grads(p, cp, sig, left_p):
            cum_l = cp[:, :KB] + left_p
            cum_r = cp[:, KB:] + cum_l[:, KB - 1:KB]
            dz = p - sig * jnp.concatenate([cum_l, cum_r], axis=1)
            return dz.astype(BF16), cum_r[:, KB - 1:KB]

        def trip(m, st, ph):
            left_sp, left_p = st
            w_bufs[(ph + 2) % 3][...], da_bufs[(ph + 2) % 3][...] = scores(jnp.minimum(m + 2, n_tot - 1))
            a, p, cp, left_sp = weights(w_bufs[ph % 3][...], cum_bufs[ph % 2][...], da_bufs[ph % 3][...], left_sp)
            cum_bufs[(ph + 1) % 2][...], sig_bufs[(ph + 1) % 2][...] = softplus_sums(w_bufs[(ph + 1) % 3][...])
            dz, left_p = score_grads(p, cp, sig_bufs[ph % 2][...], left_p)
            dq_acc[...] += _dot(dz, step_rows(k_ref, m))
            dkt_acc[m] += jnp.dot(q_t, dz, preferred_element_type=F32) * scale
            dvt_acc[m] += jnp.dot(do_t, a, preferred_element_type=F32)
            return left_sp, left_p

        dq_acc[...] = jnp.zeros_like(dq_acc)
        w_bufs[0][...], da_bufs[0][...] = scores(0)
        w_bufs[1][...], da_bufs[1][...] = scores(jnp.minimum(1, n_tot - 1))
        cum_bufs[0][...], sig_bufs[0][...] = softplus_sums(w_bufs[0][...])
        zero_col = jnp.zeros((QB, 1), F32)
        _run_trips(trip, n_tot, (zero_col, zero_col))
        dq_ref[...] = (dq_acc[...] * scale).astype(BF16)

        @pl.when(i == nq - 1)
        def _():
            for s in range(n_key_steps):
                dk_ref[s * KS:(s + 1) * KS, :] = dkt_acc[s].T.astype(BF16)
                dv_ref[s * KS:(s + 1) * KS, :] = dvt_acc[s].T.astype(BF16)

    return pl.pallas_call(
        body, name=name, grid=(H, nq),
        in_specs=[pl.BlockSpec((QB, HEAD_DIM), lambda h, i: (i, h)),
                  pl.BlockSpec((T, HEAD_DIM), lambda h, i: (0, H + h)),
                  pl.BlockSpec((T, HEAD_DIM), lambda h, i: (0, 2 * H + h)),
                  pl.BlockSpec((QB, HEAD_DIM), lambda h, i: (i, h)),
                  pl.BlockSpec((1, QB, LANES), lambda h, i: (h, i, 0))],
        out_specs=[pl.BlockSpec((QB, HEAD_DIM), lambda h, i: (i, h)),
                   pl.BlockSpec((T, HEAD_DIM), lambda h, i: (0, h)),
                   pl.BlockSpec((T, HEAD_DIM), lambda h, i: (0, h))],
        out_shape=[jax.ShapeDtypeStruct((T, D), BF16)] * 3,
        scratch_shapes=[pltpu.VMEM((n_key_steps, HEAD_DIM, KS), F32), pltpu.VMEM((n_key_steps, HEAD_DIM, KS), F32),
                        pltpu.VMEM((QB, HEAD_DIM), F32)] + [pltpu.VMEM((QB, KS), F32)] * 10
                       + [pltpu.VMEM((ns + 1, QB, KS), F32)],
        compiler_params=_params(2),
    )(qkv, qkv, qkv, do, r_tot)


def _row_to_col(row):
    C = row.shape[1]
    eye = _iota2((C, C), 0) == _iota2((C, C), 1)
    return jnp.sum(jnp.where(eye, row, 0.0), axis=1, keepdims=True)


def _lane_sum(x):
    return jnp.sum(x, axis=-1, keepdims=True)


BWD_STATE_SKEW = 3


def _chunk_head_bwd(load, take_dstate, give):
    q, k, v, gc, beta, s_in, t_inv, do = load()
    C = q.shape[0]
    r, c = _iota2((C, C), 0), _iota2((C, C), 1)
    causal, strict = r >= c, r > c
    decay = jnp.where(causal, jnp.exp(jnp.where(causal, gc - _col_to_row(gc), 0.0)), 0.0)
    kb, vb = k * beta, v * beta
    eg = jnp.exp(gc)
    kbg = kb * eg
    g_last = gc[C - 1:C, :]
    e_tail = jnp.exp(g_last - gc)
    k_tail = k * e_tail
    gl = jnp.exp(g_last)
    qg = q * eg
    t_inv_t = t_inv.T
    kk = _dot_nt(kb, k)
    u = _dot(t_inv, vb)
    w = _dot(t_inv, kbg)
    qk = _dot_nt(q, k)
    d_qg = _dot_nt(do, s_in)
    ds_state = _dot_tn(qg, do)
    yield
    ds_out = take_dstate()
    low = jnp.where(strict, kk * decay, 0.0)
    attn = qk * decay
    w_state = _dot(w, s_in)
    d_vnew_intra = _dot_tn(attn, do)
    d_vnew_state = _dot(k_tail, ds_out)
    yield
    v_new = u - w_state
    d_vnew = d_vnew_intra + d_vnew_state
    d_ktail = _dot_nt(v_new, ds_out)
    d_attn_raw = _dot_nt(do, v_new)
    d_w = -_dot_nt(d_vnew, s_in)
    ds_w = _dot_tn(w, d_vnew)
    d_vb = _dot(t_inv_t, d_vnew)
    d_tinv_u = _dot_nt(d_vnew, vb)
    yield
    d_gl = _lane_sum(jnp.sum(s_in * ds_out, axis=0, keepdims=True))
    d_attn = jnp.where(causal, d_attn_raw, 0.0)
    give(ds_in=ds_out * gl + ds_state - ds_w)
    d_kbg = _dot(t_inv_t, d_w)
    d_tinv_w = _dot_nt(d_w, kbg)
    d_qk = d_attn * decay
    dq_intra = _dot(d_qk, k)
    dk_intra = _dot_tn(d_qk, q)
    yield
    inner = _dot(t_inv_t, d_tinv_u + d_tinv_w)
    yield
    d_low_raw = _dot_nt(inner, t_inv)
    yield
    d_low = jnp.where(strict, -d_low_raw, 0.0)
    d_kk = d_low * decay
    d_kb_low = _dot(d_kk, k)
    dk_low = _dot_tn(d_kk, kb)
    yield
    d_kb = d_kb_low + d_kbg * eg
    give(dq=dq_intra + d_qg * eg, dk=dk_low + dk_intra + d_ktail * e_tail + d_kb * beta, dv=d_vb * beta)
    dbeta = _lane_sum(d_kb * k + d_vb * v)
    m = d_low * low + d_attn * attn
    tail = d_ktail * k_tail
    d_g_last = d_gl * gl + _lane_sum(jnp.sum(tail, axis=0, keepdims=True))
    dgc = (_lane_sum(m) - _row_to_col(jnp.sum(m, axis=0, keepdims=True))
           + _lane_sum(d_qg * qg + d_kbg * kbg - tail))
    return dgc + jnp.where(_iota2((C, 1), 0) == C - 1, d_g_last, 0.0), dbeta


def _chunk_bwd(qkv, beta, gc, s_all, t_all, do, d_model, name):
    T = qkv.shape[0]
    D = d_model
    H = D // HEAD_DIM
    N = T // CHUNK
    G = CHUNKS_PER_STEP
    R = G * CHUNK
    n_steps = N // G

    def body(q_ref, k_ref, v_ref, beta_ref, gc_ref, s_ref, t_ref, do_ref, dqkv_ref, dbeta_ref, dg_ref, ds_ref):
        @pl.when(pl.program_id(0) == 0)
        def _():
            ds_ref[...] = jnp.zeros_like(ds_ref)

        lane = _iota2((CHUNK, LANES), 1)
        heads = [slice(h * HEAD_DIM, (h + 1) * HEAD_DIM) for h in range(H)]
        d_states = [[ds_ref[h] for h in range(H)]] + [[None] * H for _ in range(G)]

        def head(g, h):
            sub = G - 1 - g
            rows = slice(sub * CHUNK, (sub + 1) * CHUNK)

            def give(ds_in=None, dq=None, dk=None, dv=None):
                if ds_in is not None:
                    d_states[g + 1][h] = ds_in
                if dq is not None:
                    dqkv_ref[rows, h * HEAD_DIM:(h + 1) * HEAD_DIM] = dq
                    dqkv_ref[rows, D + h * HEAD_DIM:D + (h + 1) * HEAD_DIM] = dk
                    dqkv_ref[rows, 2 * D + h * HEAD_DIM:2 * D + (h + 1) * HEAD_DIM] = dv

            load = lambda: (q_ref[rows, heads[h]], k_ref[rows, heads[h]], v_ref[rows, heads[h]], gc_ref[rows, h:h + 1],
                            beta_ref[rows, h:h + 1], s_ref[sub, h], t_ref[sub, h], do_ref[rows, heads[h]])
            return _chunk_head_bwd(load, lambda: d_states[g][h], give)

        def finish(g, results):
            rows = slice((G - 1 - g) * CHUNK, (G - g) * CHUNK)
            dgc_all = jnp.zeros((CHUNK, LANES), F32)
            dbeta_all = jnp.zeros((CHUNK, LANES), F32)
            for h, (dgc, dbeta) in enumerate(results):
                dgc_all = jnp.where(lane == h, dgc, dgc_all)
                dbeta_all = jnp.where(lane == h, dbeta, dbeta_all)
            dbeta_ref[rows, :] = dbeta_all
            dg_ref[rows, :] = _dot_mask(_chunk_tri(CHUNK, upper=True), dgc_all)

        _lockstep([[head(g, h) for h in range(H)] for g in range(G)], BWD_STATE_SKEW, finish)
        for h in range(H):
            ds_ref[h] = d_states[G][h]

    rev = lambda n: n_steps - 1 - n
    return pl.pallas_call(
        body, name=name, grid=(n_steps,),
        in_specs=[pl.BlockSpec((R, D), lambda n: (rev(n), 0)), pl.BlockSpec((R, D), lambda n: (rev(n), 1)),
                  pl.BlockSpec((R, D), lambda n: (rev(n), 2)),
                  pl.BlockSpec((R, LANES), lambda n: (rev(n), 0)), pl.BlockSpec((R, LANES), lambda n: (rev(n), 0)),
                  pl.BlockSpec((G, H, HEAD_DIM, HEAD_DIM), lambda n: (rev(n), 0, 0, 0)),
                  pl.BlockSpec((G, H, CHUNK, CHUNK), lambda n: (rev(n), 0, 0, 0)),
                  pl.BlockSpec((R, D), lambda n: (rev(n), 0))],
        out_specs=[pl.BlockSpec((R, 3 * D), lambda n: (rev(n), 0)),
                   pl.BlockSpec((R, LANES), lambda n: (rev(n), 0)), pl.BlockSpec((R, LANES), lambda n: (rev(n), 0))],
        out_shape=[jax.ShapeDtypeStruct((T, 3 * D), F32), jax.ShapeDtypeStruct((T, LANES), F32),
                   jax.ShapeDtypeStruct((T, LANES), F32)],
        scratch_shapes=[pltpu.VMEM((H, HEAD_DIM, HEAD_DIM), F32)], compiler_params=_params(1),
    )(qkv, qkv, qkv, beta, gc, s_all, t_all, do)


def _gates_bwd(p, gate_params, col0, dbeta, dg, name):
    T = p.shape[0]
    tm = _pick(T, (256, 128, 64))

    def body(pb_ref, pa_ref, gp_ref, dbeta_ref, dg_ref, dp_ref, dgp_ref):
        gp = gp_ref[...]
        _, vjp = jax.vjp(_gates_math, pb_ref[...], pa_ref[...], gp[0:1, :], gp[1:2, :])
        dpb, dpa, d_alog, d_dt = vjp((dbeta_ref[...], dg_ref[...]))
        dp_ref[:, 0:LANES] = dpb.astype(BF16)
        dp_ref[:, LANES:2 * LANES] = dpa.astype(BF16)

        @pl.when(pl.program_id(0) == 0)
        def _():
            dgp_ref[...] = jnp.zeros_like(dgp_ref)

        dgp_ref[0:1, :] += d_alog
        dgp_ref[1:2, :] += d_dt

    return pl.pallas_call(
        body, name=name, grid=(T // tm,),
        in_specs=[pl.BlockSpec((tm, LANES), lambda i: (i, col0)), pl.BlockSpec((tm, LANES), lambda i: (i, col0 + 1)),
                  pl.BlockSpec((8, LANES), lambda i: (0, 0)),
                  pl.BlockSpec((tm, LANES), lambda i: (i, 0)), pl.BlockSpec((tm, LANES), lambda i: (i, 0))],
        out_specs=[pl.BlockSpec((tm, 2 * LANES), lambda i: (i, 0)), pl.BlockSpec((8, LANES), lambda i: (0, 0))],
        out_shape=[jax.ShapeDtypeStruct((T, 2 * LANES), BF16), jax.ShapeDtypeStruct((8, LANES), F32)],
        compiler_params=_params(1),
    )(p, p, gate_params, dbeta, dg)


def _conv_bwd_act(p, conv_w, dqkv, d_model, name):
    T = p.shape[0]
    D = d_model
    H = D // HEAD_DIM
    tm = _pick(T, (256, 128, 64))

    def columns(j, prev_outs):
        def body(cur_ref, prev_ref, w_ref, dout_ref, *rest):
            dc_ref, dw_ref = rest[-2:]
            i = pl.program_id(0)
            prev = prev_ref[...] * (i > 0).astype(F32)
            taps = _conv_taps(cur_ref[...], prev)
            w = w_ref[...]
            c = sum(taps[k] * w[k:k + 1, :] for k in range(4))
            _, vjp = jax.vjp(lambda cc: _qkv_post(cc, j, H), c)
            (dc,) = vjp(dout_ref[...])
            dc_ref[...] = dc

            @pl.when(i == 0)
            def _():
                dw_ref[...] = jnp.zeros_like(dw_ref)

            for k in range(4):
                dw_ref[k:k + 1, :] += jnp.sum(dc * taps[k], axis=0, keepdims=True)

        specs, operands, aliases = _alias_previous(prev_outs, 4)
        return pl.pallas_call(
            body, name="%s%d" % (name, j), grid=(T // tm,),
            in_specs=[pl.BlockSpec((tm, D), lambda i: (i, j)),
                      pl.BlockSpec((8, D), lambda i: (jnp.maximum(i * (tm // 8) - 1, 0), j)),
                      pl.BlockSpec((4, D), lambda i: (0, j)),
                      pl.BlockSpec((tm, D), lambda i: (i, j))] + specs,
            out_specs=[pl.BlockSpec((tm, D), lambda i: (i, j)), pl.BlockSpec((4, D), lambda i: (0, j))],
            out_shape=[jax.ShapeDtypeStruct((T, 3 * D), F32), jax.ShapeDtypeStruct((4, 3 * D), F32)],
            input_output_aliases=aliases, compiler_params=_params(1),
        )(p, p, conv_w, dqkv, *operands)

    return _column_calls(columns)


def _conv_bwd_input(dc, conv_w, name):
    T, D3 = dc.shape
    D = D3 // 3
    tm = _pick(T, (256, 128, 64))
    n_t = T // tm

    def body(cur_ref, next_ref, w_ref, dp_ref):
        i = pl.program_id(0)
        cur = cur_ref[...]
        nxt = next_ref[...] * (i < n_t - 1).astype(F32)
        ext = jnp.concatenate([cur, nxt], axis=0)
        w = w_ref[...]
        acc = cur * w[3:4, :]
        for s in (1, 2, 3):
            acc = acc + pltpu.roll(ext, tm + 8 - s, 0)[0:tm] * w[3 - s:4 - s, :]
        dp_ref[...] = acc.astype(BF16)

    return pl.pallas_call(
        body, name=name, grid=(n_t, 3),
        in_specs=[pl.BlockSpec((tm, D), lambda i, j: (i, j)),
                  pl.BlockSpec((8, D), lambda i, j: (jnp.minimum((i + 1) * (tm // 8), T // 8 - 1), j)),
                  pl.BlockSpec((4, D), lambda i, j: (0, j))],
        out_specs=pl.BlockSpec((tm, D), lambda i, j: (i, j)),
        out_shape=jax.ShapeDtypeStruct((T, D3), BF16), compiler_params=_params(2),
    )(dc, dc, conv_w)


def _comm_call(body, arrays, out_shape, n_remote, n_local, name):
    any_spec = pl.BlockSpec(memory_space=pl.ANY)
    return pl.pallas_call(
        body, name=name, in_specs=[any_spec] * len(arrays), out_specs=[any_spec] * len(out_shape), out_shape=out_shape,
        scratch_shapes=[pltpu.SemaphoreType.DMA((n_remote,)), pltpu.SemaphoreType.DMA((n_remote,)),
                        pltpu.SemaphoreType.DMA((n_local,))],
        compiler_params=pltpu.CompilerParams(has_side_effects=True),
    )(*arrays)


def _gather_shards(arrays, name):
    n = len(arrays)

    def body(*refs):
        ins, outs = refs[:n], refs[n:2 * n]
        send_sems, recv_sems, local_sems = refs[2 * n:]
        x, y, c = lax.axis_index("x"), lax.axis_index("y"), lax.axis_index("c")
        me, sibling = (x, y, c), (x, y, 1 - c)
        chips = [(1 - x, y), (x, 1 - y), (1 - x, 1 - y)]

        def copy(a, k, block, to, src=None):
            dst = outs[a].at[4 * block[0] + 2 * block[1] + block[2]]
            return pltpu.make_async_remote_copy(
                src_ref=dst if src is None else src, dst_ref=dst, send_sem=send_sems.at[a * 7 + k],
                recv_sem=recv_sems.at[a * 7 + k], device_id=to, device_id_type=MESH_ID)

        local = [pltpu.make_async_copy(ins[a], outs[a].at[4 * x + 2 * y + c], local_sems.at[a]) for a in range(n)]
        first = [copy(a, 0, me, sibling, src=ins[a]) for a in range(n)]
        first += [copy(a, 1 + j, me, (*chip, c), src=ins[a]) for j, chip in enumerate(chips) for a in range(n)]
        for cp in local + first:
            cp.start()
        passed = []
        for j, chip in enumerate(chips):
            for a in range(n):
                copy(a, 1 + j, (*chip, c), me).wait_recv()
                passed.append(copy(a, 4 + j, (*chip, c), sibling))
                passed[-1].start()
        for a in range(n):
            copy(a, 0, sibling, me).wait_recv()
            for j, chip in enumerate(chips):
                copy(a, 4 + j, (*chip, 1 - c), me).wait_recv()
        for cp in first + passed:
            cp.wait_send()
        for cp in local:
            cp.wait()

    out_shape = [jax.ShapeDtypeStruct((N_DEV,) + a.shape, a.dtype) for a in arrays]
    return _comm_call(body, arrays, out_shape, 7 * n, n, name)


def _pair_exchange(arrays, name):
    n = len(arrays)

    def body(*refs):
        ins, pair = refs[:n], refs[n:2 * n]
        send_sems, recv_sems, _ = refs[2 * n:]
        x, y, c = lax.axis_index("x"), lax.axis_index("y"), lax.axis_index("c")
        sends = [pltpu.make_async_remote_copy(
            src_ref=ins[a].at[1 - c], dst_ref=pair[a], send_sem=send_sems.at[a], recv_sem=recv_sems.at[a],
            device_id=(x, y, 1 - c), device_id_type=MESH_ID) for a in range(n)]
        for cp in sends:
            cp.start()
        for cp in sends:
            cp.wait_recv()
        for cp in sends:
            cp.wait_send()

    out_shape = [jax.ShapeDtypeStruct(a.shape[1:], a.dtype) for a in arrays]
    return _comm_call(body, arrays, out_shape, n, 1, name)


def _pair_add(own, pair, name, out_dtype):
    _, R, C = own.shape
    tr = next((t for t in (256, 128, 64, 32, 16) if R % t == 0), R)

    def body(a_ref, b_ref, o_ref):
        o_ref[...] = (a_ref[...].astype(F32) + b_ref[...].astype(F32)).astype(out_dtype)

    spec = pl.BlockSpec((1, tr, C), lambda q, i: (q, i, 0))
    return pl.pallas_call(body, name=name, grid=(4, R // tr), in_specs=[spec, spec], out_specs=spec,
                          out_shape=jax.ShapeDtypeStruct(own.shape, out_dtype), compiler_params=_params(2))(own, pair)


def _chip_exchange(arrays, name):
    n = len(arrays)

    def body(*refs):
        ins, outs = refs[:n], refs[n:2 * n]
        send_sems, recv_sems, local_sems = refs[2 * n:]
        x, y, c = lax.axis_index("x"), lax.axis_index("y"), lax.axis_index("c")
        my_chip = 2 * x + y
        chips = [(1 - x, y), (x, 1 - y), (1 - x, 1 - y)]
        local = [pltpu.make_async_copy(ins[a].at[my_chip], outs[a].at[my_chip], local_sems.at[a]) for a in range(n)]
        sends = [pltpu.make_async_remote_copy(
            src_ref=ins[a].at[2 * px + py], dst_ref=outs[a].at[my_chip], send_sem=send_sems.at[a * 3 + j],
            recv_sem=recv_sems.at[a * 3 + j], device_id=(px, py, c), device_id_type=MESH_ID)
            for j, (px, py) in enumerate(chips) for a in range(n)]
        arrivals = [pltpu.make_async_remote_copy(
            src_ref=ins[a].at[my_chip], dst_ref=outs[a].at[2 * px + py], send_sem=send_sems.at[a * 3 + j],
            recv_sem=recv_sems.at[a * 3 + j], device_id=(px, py, c), device_id_type=MESH_ID)
            for j, (px, py) in enumerate(chips) for a in range(n)]
        for cp in local + sends:
            cp.start()
        for cp in arrivals:
            cp.wait_recv()
        for cp in sends:
            cp.wait_send()
        for cp in local:
            cp.wait()

    out_shape = [jax.ShapeDtypeStruct(a.shape, a.dtype) for a in arrays]
    return _comm_call(body, arrays, out_shape, 3 * n, n, name)


def _reduce_adamw(recv, w, m, v, name):
    S, R, C = recv.shape
    tr = next((t for t in (256, 128, 64, 32, 16, 8) if R % t == 0), R)
    c1 = 1.0 - ADAM_B1 ** ADAM_STEP
    c2 = 1.0 - ADAM_B2 ** ADAM_STEP

    def body(r_ref, w_ref, m_ref, v_ref, g_ref, d_ref, nm_ref, nv_ref):
        g = r_ref[0].astype(F32)
        for s in range(1, S):
            g = g + r_ref[s].astype(F32)
        nm = ADAM_B1 * m_ref[...] + (1.0 - ADAM_B1) * g
        nv = ADAM_B2 * v_ref[...] + (1.0 - ADAM_B2) * (g * g)
        g_ref[...] = g
        nm_ref[...] = nm
        nv_ref[...] = nv
        d_ref[...] = -ADAM_LR * ((nm / c1) / (jnp.sqrt(nv / c2) + ADAM_EPS) + ADAM_WD * w_ref[...])

    spec = pl.BlockSpec((tr, C), lambda i: (i, 0))
    return pl.pallas_call(
        body, name=name, grid=(R // tr,),
        in_specs=[pl.BlockSpec((S, tr, C), lambda i: (0, i, 0)), spec, spec, spec], out_specs=[spec] * 4,
        out_shape=[jax.ShapeDtypeStruct((R, C), F32)] * 4, compiler_params=_params(1),
    )(recv, w, m, v)


def _forward_local(x, target, nw0, nw1, fw, wa_in, conv_w, gate_params, o_norm, wa_out, wb_in, wb_out):
    T, D = x.shape
    nD = D // LANES
    sv = {}
    sv["u0"] = _rmsnorm_fwd(x, nw0, "a_norm_fwd")
    sv["pa"] = _mm_nn(sv["u0"], wa_in, "a_in_proj")
    sv["qkv_a"] = _conv_fwd(sv["pa"], conv_w, D, "a_conv_fwd")
    sv["beta"], sv["gc"] = _gates_fwd(sv["pa"], gate_params, 4 * nD, "a_gates_fwd")
    sv["o_a"], sv["s_all"], sv["t_all"] = _chunk_fwd(sv["qkv_a"], sv["beta"], sv["gc"], D, "a_chunk_fwd")
    sv["y_a"] = _onorm_gate_fwd(sv["o_a"], sv["pa"], 3, o_norm, "a_onorm_fwd")
    sv["h1"] = _mm_nn(sv["y_a"], wa_out, "a_out_proj", add=x)
    sv["u1"] = _rmsnorm_fwd(sv["h1"], nw1, "b_norm_fwd")
    sv["qkv_b"] = _mm_nn(sv["u1"], wb_in[:, :3 * D], "b_in_proj_qkv", out_dtype=BF16)
    sv["gate_b"] = _mm_nn(sv["u1"], wb_in[:, 3 * D:], "b_in_proj_gate")
    sv["o_b"], sv["r_b"] = _sb_fwd(sv["qkv_b"], D, "b_attn_fwd")
    sv["y_b"] = _gate_mul_fwd(sv["o_b"], sv["gate_b"], "b_gate_fwd")
    sv["h2"] = _mm_nn(sv["y_b"], wb_out, "b_out_proj", add=sv["h1"])
    sv["dh2"], sv["loss"], sv["dfw"] = _final_loss(sv["h2"], fw, target, "final_loss")
    return sv


def _accumulate(acc_ref, value):
    @pl.when(pl.program_id(0) == 0)
    def _():
        acc_ref[...] = jnp.zeros_like(acc_ref)

    acc_ref[...] += jnp.broadcast_to(value, acc_ref.shape)


def _gate_mul_bwd_tail(dy, ins, outs):
    o_ref, g_ref = ins
    do_ref, dg_ref = outs
    g = g_ref[...]
    s = jax.nn.sigmoid(g)
    do_ref[...] = dy * (g * s)
    dg_ref[...] = (dy * o_ref[...] * (s + g * s * (1.0 - s))).astype(BF16)


def _rmsnorm_bwd_tail(du, ins, outs):
    x_ref, w_ref, dres_ref = ins
    dx_ref, dw_ref = outs
    dx, dw = _rms_bwd_math(x_ref[...], w_ref[...], du)
    dx_ref[...] = dres_ref[...] + dx
    _accumulate(dw_ref, dw)


def _onorm_gate_bwd_tail(dy, ins, outs):
    o_ref, z_ref, w_ref = ins
    do_ref, dz_ref, dw_ref = outs
    n_heads = o_ref.shape[1] // HEAD_DIM
    _, vjp = jax.vjp(functools.partial(_onorm_gate_math, n_heads=n_heads), o_ref[...], z_ref[...], w_ref[...])
    do, dz, dw = vjp(dy)
    do_ref[...] = do
    dz_ref[...] = dz.astype(BF16)
    _accumulate(dw_ref, dw)


def _backward_local(sv, x, nw0, nw1, wa_in, conv_w, gate_params, o_norm, wa_out, wb_in, wb_out):
    T, D = x.shape
    nD = D // LANES
    g = {}
    dh2 = sv["dh2"]
    act = lambda dtype: (jax.ShapeDtypeStruct((T, D), dtype), True)
    acc = lambda width: (jax.ShapeDtypeStruct((8, width), F32), False)
    g["wb_out"] = _mm_tn(sv["y_b"], dh2, "b_out_proj_dw")
    do_b, dgate_b = _mm_nt_then([(dh2, wb_out)], "b_out_proj_dx", [sv["o_b"], sv["gate_b"]], [0, 0],
                                _gate_mul_bwd_tail, [act(F32), act(BF16)])
    dq_b, dk_b, dv_b = _sb_bwd(sv["qkv_b"], do_b, sv["r_b"], D, "b_attn_bwd")
    dp_b = [dq_b, dk_b, dv_b, dgate_b]
    g["wb_in"] = jnp.concatenate([_mm_tn(sv["u1"], dp, "b_in_proj_dw%d" % c) for c, dp in enumerate(dp_b)], axis=1)
    dh1, g["nw1"] = _mm_nt_then([(dp, wb_in[:, c * D:(c + 1) * D]) for c, dp in enumerate(dp_b)], "b_in_proj_dx",
                                [sv["h1"], nw1, dh2], [0, None, 0], _rmsnorm_bwd_tail, [act(F32), acc(D)])
    g["wa_out"] = _mm_tn(sv["y_a"], dh1, "a_out_proj_dw")
    do_a, dz_a, g["o_norm"] = _mm_nt_then([(dh1, wa_out)], "a_out_proj_dx", [sv["o_a"], sv["pa"], o_norm], [0, 3, None],
                                          _onorm_gate_bwd_tail, [act(F32), act(BF16), acc(HEAD_DIM)])
    dqkv_a, dbeta, dg = _chunk_bwd(sv["qkv_a"], sv["beta"], sv["gc"], sv["s_all"], sv["t_all"], do_a, D, "a_chunk_bwd")
    dp_gates, g["gate_params"] = _gates_bwd(sv["pa"], gate_params, 4 * nD, dbeta, dg, "a_gates_bwd")
    dc, g["conv_w"] = _conv_bwd_act(sv["pa"], conv_w, dqkv_a, D, "a_conv_bwd_act")
    dp_qkv = _conv_bwd_input(dc, conv_w, "a_conv_bwd_input")
    dp_a = [(dp_qkv, 0, 3 * D), (dz_a, 3 * D, 4 * D), (dp_gates, 4 * D, 4 * D + 2 * LANES)]
    g["wa_in"] = jnp.concatenate([_mm_tn(sv["u0"], dp, "a_in_proj_dw%d" % c) for c, (dp, _, _) in enumerate(dp_a)], axis=1)
    g["x"], g["nw0"] = _mm_nt_then([(dp, wa_in[:, lo:hi]) for dp, lo, hi in dp_a], "a_in_proj_dx",
                                   [x, nw0, dh1], [0, None, 0], _rmsnorm_bwd_tail, [act(F32), acc(D)])
    g["fw"] = sv["dfw"]
    return g


def kernel(x, norm_w, a_w_in, a_conv_w, a_a_log, a_dt_bias, a_o_norm, a_w_out, b_w_in, b_w_out, final_norm_w, loss_target, m_norm_w, m_a_w_in, m_a_conv_w, m_a_a_log, m_a_dt_bias, m_a_o_norm, m_a_w_out, m_b_w_in, m_b_w_out, m_final_norm_w, v_norm_w, v_a_w_in, v_a_conv_w, v_a_a_log, v_a_dt_bias, v_a_o_norm, v_a_w_out, v_b_w_in, v_b_w_out, v_final_norm_w):
    D = x.shape[-1]
    H = D // HEAD_DIM
    shards = [a_w_in[0].astype(BF16), a_w_out[0].astype(BF16), b_w_in[0].astype(BF16), b_w_out[0].astype(BF16), a_conv_w[0]]
    ga_in, ga_out, gb_in, gb_out, g_conv = _gather_shards(shards, "weights_gather")
    wa = ga_in.transpose(1, 0, 2).reshape(D, -1)
    pad = lambda w: jnp.pad(w, ((0, 0), (0, LANES - w.shape[1])))
    wa_in = jnp.concatenate([wa[:, :4 * D], pad(wa[:, 4 * D:4 * D + H]), pad(wa[:, 4 * D + H:])], axis=1)
    wa_out = ga_out.reshape(D, D)
    wb_in = gb_in.transpose(1, 0, 2).reshape(D, 4 * D)
    wb_out = gb_out.reshape(D, D)
    conv_w = g_conv.transpose(1, 0, 2).reshape(4, 3 * D)
    gate_params = jnp.zeros((8, LANES), F32).at[0, :H].set(a_a_log[0]).at[1, :H].set(a_dt_bias[0])
    nw0, nw1, fw = norm_w[0:1], norm_w[1:2], final_norm_w[None]

    sv = _forward_local(x[0], loss_target[0], nw0, nw1, fw, wa_in, conv_w, gate_params, a_o_norm, wa_out, wb_in, wb_out)
    g = _backward_local(sv, x[0], nw0, nw1, wa_in, conv_w, gate_params, a_o_norm, wa_out, wb_in, wb_out)

    gwa = g["wa_in"]
    gwa = jnp.concatenate([gwa[:, :4 * D], gwa[:, 4 * D:4 * D + H], gwa[:, 4 * D + LANES:4 * D + LANES + H]], axis=1)
    row = lambda v: jnp.pad(v.reshape(1, -1), ((0, 0), (0, D - v.size)))
    small = jnp.concatenate([g["nw0"][0:1], g["nw1"][0:1], g["fw"][0:1], row(g["gate_params"][0, :H]),
                             row(g["gate_params"][1, :H]), row(g["o_norm"][0]), row(sv["loss"][0, 0:1]),
                             jnp.zeros((1, D), F32)], axis=0)
    cols = lambda a: a.astype(BF16).reshape(a.shape[0], 4, 2, -1).transpose(2, 1, 0, 3)
    rows = lambda a: a.astype(BF16).reshape(4, 2, -1, a.shape[1]).transpose(1, 0, 2, 3)
    contribs = [cols(gwa), rows(g["wa_out"]), cols(g["wb_in"]), rows(g["wb_out"]), cols(g["conv_w"]),
                jnp.broadcast_to(small[None, None], (2, 4, 8, D))]
    pair = _pair_exchange(contribs, "grads_pair_exchange")
    my_core = lax.axis_index("c")
    own = [lax.dynamic_index_in_dim(a, my_core, axis=0, keepdims=False) for a in contribs]
    partial = [_pair_add(o, p, "grads_pair_add%d" % k, o.dtype) for k, (o, p) in enumerate(zip(own, pair))]
    ra_in, ra_out, rb_in, rb_out, r_conv, r_small = _chip_exchange(partial, "grads_chip_exchange")

    outs = {}
    for nm, recv, w, m, v in (("a_w_in", ra_in, a_w_in, m_a_w_in, v_a_w_in), ("a_w_out", ra_out, a_w_out, m_a_w_out, v_a_w_out),
                              ("b_w_in", rb_in, b_w_in, m_b_w_in, v_b_w_in), ("b_w_out", rb_out, b_w_out, m_b_w_out, v_b_w_out),
                              ("a_conv_w", r_conv, a_conv_w, m_a_conv_w, v_a_conv_w)):
        outs[nm] = tuple(o[None] for o in _reduce_adamw(recv, w[0], m[0], v[0], "adamw_" + nm))

    def pack(nw, alog, dt, onorm, fnw):
        return jnp.concatenate([nw, fnw.reshape(1, D), row(alog), row(dt), row(onorm), jnp.zeros((2, D), F32)], axis=0)

    s_g, s_d, s_m, s_v = _reduce_adamw(
        r_small, pack(norm_w, a_a_log, a_dt_bias, a_o_norm, final_norm_w),
        pack(m_norm_w, m_a_a_log, m_a_dt_bias, m_a_o_norm, m_final_norm_w),
        pack(v_norm_w, v_a_a_log, v_a_dt_bias, v_a_o_norm, v_final_norm_w), "adamw_small")
    loss = s_g[6, 0]
    for i, s in enumerate((s_g, s_d, s_m, s_v)):
        outs.setdefault("norm_w", [None] * 4)[i] = s[0:2]
        outs.setdefault("final_norm_w", [None] * 4)[i] = s[2]
        outs.setdefault("a_a_log", [None] * 4)[i] = s[3:4, :H]
        outs.setdefault("a_dt_bias", [None] * 4)[i] = s[4:5, :H]
        outs.setdefault("a_o_norm", [None] * 4)[i] = s[5:6, :HEAD_DIM]
    names = ("norm_w", "a_w_in", "a_conv_w", "a_a_log", "a_dt_bias", "a_o_norm", "a_w_out", "b_w_in", "b_w_out", "final_norm_w")
    return (loss, g["x"][None]) + tuple(outs[n][i] for i in range(4) for n in names)
```

```python
import functools

import jax
import jax.numpy as jnp
from jax import lax
from jax.experimental import pallas as pl
from jax.experimental.pallas import tpu as pltpu

F32 = jnp.float32
BF16 = jnp.bfloat16
EPS = 1e-6
LOG2_E = 1.4426950408889634
MASKED_SCORE = -1e30
HEAD_DIM = 128
CHUNK = 64
CHUNKS_PER_STEP = 4
ATTN_Q_BLOCKS_FWD = (512, 256)
ATTN_Q_BLOCKS_BWD = (512, 256)
ATTN_K_BLOCK = 128
LANES = 128
N_DEV = 8
VMEM_LIMIT_BYTES = 48 * 1024 * 1024
ADAM_LR, ADAM_B1, ADAM_B2, ADAM_EPS, ADAM_WD, ADAM_STEP = 0.001, 0.9, 0.999, 1e-08, 0.01, 10
MESH_ID = pl.DeviceIdType.MESH


def _pick(n, candidates):
    for c in candidates:
        if n % c == 0:
            return c
    raise ValueError(f"no tile for {n} in {candidates}")


def _params(n_grid_axes):
    return pltpu.CompilerParams(dimension_semantics=("arbitrary",) * n_grid_axes, vmem_limit_bytes=VMEM_LIMIT_BYTES)


def _dot(a, b):
    return jnp.dot(a.astype(BF16), b.astype(BF16), preferred_element_type=F32)


def _dot_nt(a, b):
    return lax.dot_general(a.astype(BF16), b.astype(BF16), (((1,), (1,)), ((), ())), preferred_element_type=F32)


def _dot_tn(a, b):
    return lax.dot_general(a.astype(BF16), b.astype(BF16), (((0,), (0,)), ((), ())), preferred_element_type=F32)


def _split2(x):
    hi = x.astype(BF16)
    lo = (x - hi.astype(F32)).astype(BF16)
    return hi, lo


def _split3(x):
    hi = x.astype(BF16)
    r = x - hi.astype(F32)
    mid = r.astype(BF16)
    lo = (r - mid.astype(F32)).astype(BF16)
    return hi, mid, lo


def _dot3(a, b):
    a_hi, a_lo = _split2(a)
    b_hi, b_lo = _split2(b)
    d = functools.partial(jnp.dot, preferred_element_type=F32)
    return d(a_hi, b_hi) + (d(a_hi, b_lo) + d(a_lo, b_hi))


def _silu(x):
    return x * jax.nn.sigmoid(x)


def _softplus(x):
    return jnp.maximum(x, 0.0) + jnp.log1p(jnp.exp(-jnp.abs(x)))


def _iota2(shape, axis):
    return lax.broadcasted_iota(jnp.int32, shape, axis)


def _rms_bwd_math(x, w, dy):
    r = lax.rsqrt(jnp.mean(x * x, axis=-1, keepdims=True) + EPS)
    xhat = x * r
    dxhat = dy * w
    dx = r * (dxhat - xhat * jnp.mean(dxhat * xhat, axis=-1, keepdims=True))
    dw = jnp.sum(dy * xhat, axis=0, keepdims=True)
    return dx, dw


def _rmsnorm_head(ins):
    x_ref, w_ref = ins
    xf = x_ref[...]
    r = lax.rsqrt(jnp.mean(xf * xf, axis=-1, keepdims=True) + EPS)
    return (xf * r * w_ref[...]).astype(BF16)


def _mm_nn_from(head, extras, extra_cols, b, name, outs, add=None):
    M = extras[0].shape[0]
    K, N = b.shape
    e = len(extras)
    tm = _pick(M, (256, 128))

    def body(*refs):
        ins, b_ref = refs[:e], refs[e]
        add_ref = refs[e + 1] if add is not None else None
        a_ref, out_refs = refs[e + 1 + (add is not None)], refs[e + 2 + (add is not None):]
        a = head(ins)
        a_ref[...] = a
        acc = jnp.dot(a, b_ref[...], preferred_element_type=F32)
        if add is not None:
            acc = acc + add_ref[...]
        col = 0
        for o_ref, (width, dtype) in zip(out_refs, outs):
            o_ref[...] = acc[:, col:col + width].astype(dtype)
            col += width

    def extra_spec(x, col):
        if col is None:
            return pl.BlockSpec(x.shape, lambda i: (0,) * x.ndim)
        return pl.BlockSpec((tm, K), lambda i: (i, col))

    in_specs = [extra_spec(x, col) for x, col in zip(extras, extra_cols)] + [pl.BlockSpec((K, N), lambda i: (0, 0))]
    args = list(extras) + [b]
    if add is not None:
        in_specs.append(pl.BlockSpec((tm, N), lambda i: (i, 0)))
        args.append(add)
    return pl.pallas_call(
        body, name=name, grid=(M // tm,), in_specs=in_specs,
        out_specs=[pl.BlockSpec((tm, K), lambda i: (i, 0))] + [pl.BlockSpec((tm, w), lambda i: (i, 0)) for w, _ in outs],
        out_shape=[jax.ShapeDtypeStruct((M, K), BF16)] + [jax.ShapeDtypeStruct((M, w), dt) for w, dt in outs],
        compiler_params=_params(1),
    )(*args)


def _mm_nt(pairs, name):
    def write(acc, ins, outs):
        outs[0][...] = acc

    return _mm_nt_then(pairs, name, [], [], write,
                       [(jax.ShapeDtypeStruct((pairs[0][0].shape[0], pairs[0][1].shape[0]), F32), True)])


def _mm_nt_then(pairs, name, extras, extra_cols, tail, outs):
    M = pairs[0][0].shape[0]
    n, e = len(pairs), len(extras)
    tm = _pick(M, (256, 128)) if e else _pick(M, (512, 256, 128))

    def body(*refs):
        acc = _dot_nt(refs[0][...], refs[n][...])
        for p in range(1, n):
            acc = acc + _dot_nt(refs[p][...], refs[n + p][...])
        tail(acc, refs[2 * n:2 * n + e], refs[2 * n + e:])

    def extra_spec(x, col):
        if col is None:
            return pl.BlockSpec(x.shape, lambda i: (0,) * x.ndim)
        return pl.BlockSpec((tm, outs[0][0].shape[1]), lambda i: (i, col))

    in_specs = ([pl.BlockSpec((tm, a.shape[1]), lambda i: (i, 0)) for a, _ in pairs]
                + [pl.BlockSpec(b.shape, lambda i: (0, 0)) for _, b in pairs]
                + [extra_spec(x, col) for x, col in zip(extras, extra_cols)])
    out_specs = [pl.BlockSpec((tm, s.shape[1]), lambda i: (i, 0)) if tiled else pl.BlockSpec(s.shape, lambda i: (0, 0))
                 for s, tiled in outs]
    res = pl.pallas_call(
        body, name=name, grid=(M // tm,), in_specs=in_specs, out_specs=out_specs,
        out_shape=[s for s, _ in outs], compiler_params=_params(1),
    )(*[a for a, _ in pairs], *[b for _, b in pairs], *extras)
    return res[0] if len(outs) == 1 else res


def _mm_tn(a, b, name):
    R, M = a.shape
    _, N = b.shape
    tn = _pick(N, (1536, 1024, 512, 256, 128))
    tr = _pick(R, (512, 256, 128))

    def body(a_ref, b_ref, o_ref):
        @pl.when(pl.program_id(1) == 0)
        def _():
            o_ref[...] = jnp.zeros_like(o_ref)

        o_ref[...] += _dot_tn(a_ref[...], b_ref[...])

    return pl.pallas_call(
        body, name=name, grid=(N // tn, R // tr),
        in_specs=[pl.BlockSpec((tr, M), lambda j, r: (r, 0)), pl.BlockSpec((tr, tn), lambda j, r: (r, j))],
        out_specs=pl.BlockSpec((M, tn), lambda j, r: (0, j)),
        out_shape=jax.ShapeDtypeStruct((M, N), F32), compiler_params=_params(2),
    )(a, b)


def _qkv_post(c, j, n_heads):
    s = _silu(c)
    if j == 2:
        return s
    parts = []
    for h in range(n_heads):
        sh = s[:, h * HEAD_DIM:(h + 1) * HEAD_DIM]
        parts.append(sh * lax.rsqrt(jnp.sum(sh * sh, axis=-1, keepdims=True) + EPS))
    n = jnp.concatenate(parts, axis=-1)
    return n * (HEAD_DIM ** -0.5) if j == 0 else n


def _column_calls(make_call, n_cols=3):
    outs = None
    for j in range(n_cols):
        outs = make_call(j, outs)
    return outs


def _alias_previous(prev, n_inputs):
    if prev is None:
        return [], [], {}
    prev = list(prev) if isinstance(prev, (list, tuple)) else [prev]
    return ([pl.BlockSpec(memory_space=pl.ANY)] * len(prev), prev, {n_inputs + k: k for k in range(len(prev))})


def _conv_taps(cur, halo_prev):
    tm = cur.shape[0]
    ext = jnp.concatenate([halo_prev, cur], axis=0)
    taps = [pltpu.roll(ext, s, 0)[8:8 + tm] for s in (3, 2, 1)]
    return taps + [cur]


def _conv_fwd(p, conv_w, d_model, name):
    T = p.shape[0]
    D = d_model
    H = D // HEAD_DIM
    tm = _pick(T, (256, 128, 64))

    def columns(j, prev_out):
        def body(cur_ref, prev_ref, w_ref, *rest):
            o_ref = rest[-1]
            prev = prev_ref[...] * (pl.program_id(0) > 0).astype(F32)
            taps = _conv_taps(cur_ref[...], prev)
            w = w_ref[...]
            c = sum(taps[k] * w[k:k + 1, :] for k in range(4))
            o_ref[...] = _qkv_post(c, j, H)

        specs, operands, aliases = _alias_previous(prev_out, 3)
        return pl.pallas_call(
            body, name="%s%d" % (name, j), grid=(T // tm,),
            in_specs=[pl.BlockSpec((tm, D), lambda i: (i, j)),
                      pl.BlockSpec((8, D), lambda i: (jnp.maximum(i * (tm // 8) - 1, 0), j)),
                      pl.BlockSpec((4, D), lambda i: (0, j))] + specs,
            out_specs=pl.BlockSpec((tm, D), lambda i: (i, j)),
            out_shape=jax.ShapeDtypeStruct((T, 3 * D), F32), input_output_aliases=aliases, compiler_params=_params(1),
        )(p, p, conv_w, *operands)

    return _column_calls(columns)


def _chunk_tri(tm, upper):
    r, c = _iota2((tm, tm), 0), _iota2((tm, tm), 1)
    same = (r // CHUNK) == (c // CHUNK)
    tri = (c >= r) if upper else (c <= r)
    return jnp.where(same & tri, 1.0, 0.0).astype(BF16)


def _dot_mask(mask_bf16, x):
    hi, mid, lo = _split3(x)
    d = functools.partial(jnp.dot, preferred_element_type=F32)
    return d(mask_bf16, hi) + (d(mask_bf16, mid) + d(mask_bf16, lo))


def _gates_math(pb, pa, a_log, dt_bias):
    beta = jax.nn.sigmoid(pb)
    g = -jnp.exp(a_log) * _softplus(pa + dt_bias)
    return beta, g


def _gates_fwd(p, gate_params, col0, name):
    T = p.shape[0]
    tm = _pick(T, (256, 128, 64))

    def body(pb_ref, pa_ref, gp_ref, beta_ref, gc_ref):
        gp = gp_ref[...]
        beta, g = _gates_math(pb_ref[...], pa_ref[...], gp[0:1, :], gp[1:2, :])
        beta_ref[...] = beta
        gc_ref[...] = _dot_mask(_chunk_tri(tm, upper=False), g)

    return pl.pallas_call(
        body, name=name, grid=(T // tm,),
        in_specs=[pl.BlockSpec((tm, LANES), lambda i: (i, col0)), pl.BlockSpec((tm, LANES), lambda i: (i, col0 + 1)),
                  pl.BlockSpec((8, LANES), lambda i: (0, 0))],
        out_specs=[pl.BlockSpec((tm, LANES), lambda i: (i, 0))] * 2,
        out_shape=[jax.ShapeDtypeStruct((T, LANES), F32)] * 2, compiler_params=_params(1),
    )(p, p, gate_params)


def _col_to_row(col):
    C = col.shape[0]
    eye = _iota2((C, C), 0) == _iota2((C, C), 1)
    return jnp.sum(jnp.where(eye, col, 0.0), axis=0, keepdims=True)


def _lockstep(groups, skew, finish):
    groups = [list(g) for g in groups]
    results = [[None] * len(g) for g in groups]
    left = [len(g) for g in groups]
    rnd = 0
    while any(left):
        for gi, gens in enumerate(groups):
            if rnd < gi * skew or not left[gi]:
                continue
            for idx, gen in enumerate(gens):
                if gen is None:
                    continue
                try:
                    next(gen)
                except StopIteration as done:
                    results[gi][idx] = done.value
                    gens[idx] = None
                    left[gi] -= 1
            if not left[gi]:
                finish(gi, results[gi])
        rnd += 1


def _unit_lower_inverse(low):
    C = low.shape[0]
    eye = (_iota2((C, C), 0) == _iota2((C, C), 1)).astype(F32)
    t = eye - low
    p = _dot3(low, low)
    yield
    n = 2
    while True:
        tp = _dot3(t, p)
        n *= 2
        if n < C:
            p = _dot3(p, p)
        yield
        t = t + tp
        if n >= C:
            return t


FWD_STATE_SKEW = 3


def _chunk_head_fwd(load, take_state, give):
    q, k, v, gc, beta = load()
    C = q.shape[0]
    r, c = _iota2((C, C), 0), _iota2((C, C), 1)
    causal, strict = r >= c, r > c
    decay = jnp.where(causal, jnp.exp(jnp.where(causal, gc - _col_to_row(gc), 0.0)), 0.0)
    kb, vb = k * beta, v * beta
    eg = jnp.exp(gc)
    kk = _dot_nt(kb, k)
    qk = _dot_nt(q, k)
    yield
    t_inv = yield from _unit_lower_inverse(jnp.where(strict, kk * decay, 0.0))
    give(t_inv=t_inv)
    u = _dot(t_inv, vb)
    w = _dot(t_inv, kb * eg)
    yield
    s_in = take_state()
    give(s_in=s_in)
    o_state = _dot(q * eg, s_in)
    w_state = _dot(w, s_in)
    yield
    v_new = u - w_state
    g_last = gc[C - 1:C, :]
    o_intra = _dot(qk * decay, v_new)
    s_add = _dot_tn(k * jnp.exp(g_last - gc), v_new)
    yield
    give(o=o_state + o_intra, s_out=s_in * jnp.exp(g_last) + s_add)


def _chunk_fwd(qkv, beta, gc, d_model, name):
    T = qkv.shape[0]
    D = d_model
    H = D // HEAD_DIM
    N = T // CHUNK
    G = CHUNKS_PER_STEP
    R = G * CHUNK

    def body(q_ref, k_ref, v_ref, beta_ref, gc_ref, o_ref, s_all_ref, t_all_ref, s_ref):
        @pl.when(pl.program_id(0) == 0)
        def _():
            s_ref[...] = jnp.zeros_like(s_ref)

        heads = [slice(h * HEAD_DIM, (h + 1) * HEAD_DIM) for h in range(H)]
        states = [[s_ref[h] for h in range(H)]] + [[None] * H for _ in range(G)]

        def head(sub, h):
            rows = slice(sub * CHUNK, (sub + 1) * CHUNK)

            def give(t_inv=None, s_in=None, o=None, s_out=None):
                if t_inv is not None:
                    t_all_ref[sub, h] = t_inv
                if s_in is not None:
                    s_all_ref[sub, h] = s_in
                if o is not None:
                    o_ref[rows, heads[h]] = o
                    states[sub + 1][h] = s_out

            load = lambda: (q_ref[rows, heads[h]], k_ref[rows, heads[h]], v_ref[rows, heads[h]],
                            gc_ref[rows, h:h + 1], beta_ref[rows, h:h + 1])
            return _chunk_head_fwd(load, lambda: states[sub][h], give)

        _lockstep([[head(sub, h) for h in range(H)] for sub in range(G)], FWD_STATE_SKEW, lambda sub, results: None)
        for h in range(H):
            s_ref[h] = states[G][h]

    return pl.pallas_call(
        body, name=name, grid=(N // G,),
        in_specs=[pl.BlockSpec((R, D), lambda n: (n, 0)), pl.BlockSpec((R, D), lambda n: (n, 1)),
                  pl.BlockSpec((R, D), lambda n: (n, 2)),
                  pl.BlockSpec((R, LANES), lambda n: (n, 0)), pl.BlockSpec((R, LANES), lambda n: (n, 0))],
        out_specs=[pl.BlockSpec((R, D), lambda n: (n, 0)),
                   pl.BlockSpec((G, H, HEAD_DIM, HEAD_DIM), lambda n: (n, 0, 0, 0)),
                   pl.BlockSpec((G, H, CHUNK, CHUNK), lambda n: (n, 0, 0, 0))],
        out_shape=[jax.ShapeDtypeStruct((T, D), F32), jax.ShapeDtypeStruct((N, H, HEAD_DIM, HEAD_DIM), F32),
                   jax.ShapeDtypeStruct((N, H, CHUNK, CHUNK), F32)],
        scratch_shapes=[pltpu.VMEM((H, HEAD_DIM, HEAD_DIM), F32)], compiler_params=_params(1),
    )(qkv, qkv, qkv, beta, gc)


def _onorm_gate_math(o, z, w, n_heads):
    parts = []
    for h in range(n_heads):
        hs = slice(h * HEAD_DIM, (h + 1) * HEAD_DIM)
        oh = o[:, hs]
        y = oh * lax.rsqrt(jnp.mean(oh * oh, axis=-1, keepdims=True) + EPS) * w
        parts.append(y * _silu(z[:, hs]))
    return jnp.concatenate(parts, axis=-1)


def _onorm_gate_head(ins):
    o_ref, z_ref, w_ref = ins
    return _onorm_gate_math(o_ref[...], z_ref[...], w_ref[...], o_ref.shape[1] // HEAD_DIM).astype(BF16)


def _diag_mask(qb, kb, d):
    return _iota2((qb, kb), 0) > _iota2((qb, kb), 1) + d * kb


def _run_trips(trip, n, state):
    def six(j, st):
        for u in range(6):
            st = trip(6 * j + u, st, u)
        return st

    state = lax.fori_loop(0, n // 6, six, state)
    base = (n // 6) * 6
    for u in (0, 2):
        pair = lambda st, u=u: trip(base + u + 1, trip(base + u, st, u), u + 1)
        state = lax.cond(n - base > u, pair, lambda st: st, state)
    return state


def _fill_score_masks(mask_buf, qb, kb, ns):
    mask_buf[0] = jnp.zeros(mask_buf.shape[1:], F32)
    for d in range(ns):
        for half in range(2):
            mask_buf[d + 1, :, half * kb:(half + 1) * kb] = jnp.where(_diag_mask(qb, kb, 2 * d + half), 0.0, MASKED_SCORE)


def _softplus_bits(w):
    u = 1.0 + jnp.exp2(jnp.minimum(w, 64.0))
    return jnp.maximum(w, jnp.log2(u)), 1.0 / u


def _incl_lower(n):
    return jnp.where((_iota2((2 * n, n), 0) & (n - 1)) >= _iota2((2 * n, n), 1), 1.0, 0.0).astype(BF16)


def _incl_upper(n):
    return jnp.where((_iota2((2 * n, n), 0) & (n - 1)) <= _iota2((2 * n, n), 1), 1.0, 0.0).astype(BF16)


def _dot_cum(x, tri_bf16):
    hi, lo = _split2(x)
    return jnp.dot(jnp.concatenate([hi, lo], axis=1), tri_bf16, preferred_element_type=F32)


def _sb_fwd(qkv, d_model, name):
    T = qkv.shape[0]
    D = d_model
    H = D // HEAD_DIM
    QB = _pick(T, ATTN_Q_BLOCKS_FWD)
    KB = ATTN_K_BLOCK
    KS = 2 * KB
    ns = QB // KS
    nq = T // QB
    scale = HEAD_DIM ** -0.5

    def body(q_ref, k_ref, v_ref, o_ref, r_ref, w0, w1, w2, cum0, cum1, mask_buf):
        w_bufs, cum_bufs = (w0, w1, w2), (cum0, cum1)
        i = pl.program_id(1)

        @pl.when(i == 0)
        def _():
            _fill_score_masks(mask_buf, QB, KB, ns)

        q = q_ref[...]
        tri = _incl_lower(KB)
        n_tot = (i + 1) * ns

        def key_step(m):
            return jnp.maximum(n_tot - 1 - m, 0)

        def rows(ref, s):
            return ref[pl.ds(pl.multiple_of(s * KS, KS), KS), :]

        def scores(s, may_be_diagonal):
            w = _dot_nt(q, rows(k_ref, s)) * (scale * LOG2_E)
            return w + mask_buf[jnp.maximum(s - i * ns + 1, 0)] if may_be_diagonal else w

        def cums(w):
            sp = _softplus_bits(w)[0]
            return jnp.concatenate([_dot_cum(sp[:, :KB], tri), _dot_cum(sp[:, KB:], tri)], axis=1)

        def weights(w, cum, carry):
            a_r = jnp.exp2(w[:, KB:] - cum[:, KB:] - carry)
            carry = carry + cum[:, KB:KB + 1]
            a_l = jnp.exp2(w[:, :KB] - cum[:, :KB] - carry)
            return jnp.concatenate([a_l, a_r], axis=1).astype(BF16), carry + cum[:, 0:1]

        def trip(m, carry, ph):
            w_bufs[(ph + 2) % 3][...] = scores(key_step(m + 2), False)
            a, carry = weights(w_bufs[ph % 3][...], cum_bufs[ph % 2][...], carry)
            o_ref[...] += _dot(a, rows(v_ref, key_step(m)))
            cum_bufs[(ph + 1) % 2][...] = cums(w_bufs[(ph + 1) % 3][...])
            return carry

        o_ref[...] = jnp.zeros_like(o_ref)
        assert ns == 2
        w_bufs[0][...] = scores(key_step(0), True)
        w_bufs[1][...] = scores(key_step(1), True)
        cum_bufs[0][...] = cums(w_bufs[0][...])
        carry = _run_trips(trip, n_tot, jnp.zeros((QB, 1), F32))
        r_ref[0] = jnp.broadcast_to(carry, (QB, LANES))

    return pl.pallas_call(
        body, name=name, grid=(H, nq),
        in_specs=[pl.BlockSpec((QB, HEAD_DIM), lambda h, i: (i, h)),
                  pl.BlockSpec((T, HEAD_DIM), lambda h, i: (0, H + h)),
                  pl.BlockSpec((T, HEAD_DIM), lambda h, i: (0, 2 * H + h))],
        out_specs=[pl.BlockSpec((QB, HEAD_DIM), lambda h, i: (i, h)),
                   pl.BlockSpec((1, QB, LANES), lambda h, i: (h, i, 0))],
        out_shape=[jax.ShapeDtypeStruct((T, D), F32), jax.ShapeDtypeStruct((H, T, LANES), F32)],
        scratch_shapes=[pltpu.VMEM((QB, KS), F32)] * 5 + [pltpu.VMEM((ns + 1, QB, KS), F32)],
        compiler_params=_params(2),
    )(qkv, qkv, qkv)


def _gate_mul_head(ins):
    o_ref, g_ref = ins
    return (o_ref[...] * _silu(g_ref[...])).astype(BF16)


def _final_loss(h, w, target, name):
    T, D = h.shape
    tm = _pick(T, (512, 256, 128))

    def body(h_ref, w_ref, t_ref, dh_ref, loss_ref, dw_ref):
        x, w = h_ref[...], w_ref[...]
        r = lax.rsqrt(jnp.mean(x * x, axis=-1, keepdims=True) + EPS)
        err = x * r * w - t_ref[...]
        part = 0.5 * jnp.sum(jnp.mean(err * err, axis=-1, keepdims=True), axis=0, keepdims=True)
        dx, dw = _rms_bwd_math(x, w, err * (1.0 / D))
        dh_ref[...] = dx

        @pl.when(pl.program_id(0) == 0)
        def _():
            loss_ref[...] = jnp.zeros_like(loss_ref)
            dw_ref[...] = jnp.zeros_like(dw_ref)

        loss_ref[...] += jnp.broadcast_to(part, loss_ref.shape)
        dw_ref[...] += jnp.broadcast_to(dw, dw_ref.shape)

    return pl.pallas_call(
        body, name=name, grid=(T // tm,),
        in_specs=[pl.BlockSpec((tm, D), lambda i: (i, 0)), pl.BlockSpec((1, D), lambda i: (0, 0)),
                  pl.BlockSpec((tm, D), lambda i: (i, 0))],
        out_specs=[pl.BlockSpec((tm, D), lambda i: (i, 0)), pl.BlockSpec((8, LANES), lambda i: (0, 0)),
                   pl.BlockSpec((8, D), lambda i: (0, 0))],
        out_shape=[jax.ShapeDtypeStruct((T, D), F32), jax.ShapeDtypeStruct((8, LANES), F32),
                   jax.ShapeDtypeStruct((8, D), F32)],
        compiler_params=_params(1),
    )(h, w, target)


def _sb_bwd(qkv, do, r_tot, d_model, name):
    T = qkv.shape[0]
    D = d_model
    H = D // HEAD_DIM
    QB = _pick(T, ATTN_Q_BLOCKS_BWD)
    KB = ATTN_K_BLOCK
    KS = 2 * KB
    ns = QB // KS
    nq = T // QB
    n_key_steps = T // KS
    scale = HEAD_DIM ** -0.5

    def body(q_ref, k_ref, v_ref, do_ref, r_ref, dq_ref, dk_ref, dv_ref,
             dkt_acc, dvt_acc, dq_acc, w0, w1, w2, da0, da1, da2, cum0, cum1, sig0, sig1, mask_buf):
        w_bufs, da_bufs, cum_bufs, sig_bufs = (w0, w1, w2), (da0, da1, da2), (cum0, cum1), (sig0, sig1)
        i = pl.program_id(1)

        @pl.when(i == 0)
        def _():
            dkt_acc[...] = jnp.zeros_like(dkt_acc)
            dvt_acc[...] = jnp.zeros_like(dvt_acc)
            _fill_score_masks(mask_buf, QB, KB, ns)

        q = q_ref[...]
        do_blk = do_ref[...].astype(BF16)
        q_t = q.astype(F32).T.astype(BF16)
        do_t = do_ref[...].T.astype(BF16)
        row_total = r_ref[0][:, 0:1]
        tri_rev = _incl_lower(KB)
        tri_fwd = jnp.where(_iota2((KB, KB), 0) <= _iota2((KB, KB), 1), 1.0, 0.0).astype(BF16)
        n_tot = (i + 1) * ns

        def step_rows(ref, s):
            return ref[pl.ds(pl.multiple_of(s * KS, KS), KS), :]

        def scores(s):
            w = _dot_nt(q, step_rows(k_ref, s)) * (scale * LOG2_E) + mask_buf[jnp.maximum(s - i * ns + 1, 0)]
            return w, _dot_nt(do_blk, step_rows(v_ref, s))

        def softplus_sums(w):
            sp, one_minus_sig = _softplus_bits(w)
            cum = jnp.concatenate([_dot_cum(sp[:, :KB], tri_rev), _dot_cum(sp[:, KB:], tri_rev)], axis=1)
            return cum, 1.0 - one_minus_sig

        def weights(w, cum, da, left_sp):
            right_l = row_total - left_sp - cum[:, 0:1]
            right_r = right_l - cum[:, KB:KB + 1]
            a = jnp.concatenate([jnp.exp2(w[:, :KB] - cum[:, :KB] - right_l),
                                 jnp.exp2(w[:, KB:] - cum[:, KB:] - right_r)], axis=1)
            p = da * a
            cp = jnp.concatenate([_dot(p[:, :KB], tri_fwd), _dot(p[:, KB:], tri_fwd)], axis=1)
            return a.astype(BF16), p, cp, row_total - right_r

        def score_grads(p, cp, sig, left_p):
            cum_l = cp[:, :KB] + left_p
            cum_r = cp[:, KB:] + cum_l[:, KB - 1:KB]
            dz = p - sig * jnp.concatenate([cum_l, cum_r], axis=1)
            return dz.astype(BF16), cum_r[:, KB - 1:KB]

        def trip(m, st, ph):
            left_sp, left_p = st
            w_bufs[(ph + 2) % 3][...], da_bufs[(ph + 2) % 3][...] = scores(jnp.minimum(m + 2, n_tot - 1))
            a, p, cp, left_sp = weights(w_bufs[ph % 3][...], cum_bufs[ph % 2][...], da_bufs[ph % 3][...], left_sp)
            cum_bufs[(ph + 1) % 2][...], sig_bufs[(ph + 1) % 2][...] = softplus_sums(w_bufs[(ph + 1) % 3][...])
            dz, left_p = score_grads(p, cp, sig_bufs[ph % 2][...], left_p)
            dq_acc[...] += _dot(dz, step_rows(k_ref, m))
            dkt_acc[m] += jnp.dot(q_t, dz, preferred_element_type=F32) * scale
            dvt_acc[m] += jnp.dot(do_t, a, preferred_element_type=F32)
            return left_sp, left_p

        dq_acc[...] = jnp.zeros_like(dq_acc)
        w_bufs[0][...], da_bufs[0][...] = scores(0)
        w_bufs[1][...], da_bufs[1][...] = scores(jnp.minimum(1, n_tot - 1))
        cum_bufs[0][...], sig_bufs[0][...] = softplus_sums(w_bufs[0][...])
        zero_col = jnp.zeros((QB, 1), F32)
        _run_trips(trip, n_tot, (zero_col, zero_col))
        dq_ref[...] = (dq_acc[...] * scale).astype(BF16)

        @pl.when(i == nq - 1)
        def _():
            for s in range(n_key_steps):
                dk_ref[s * KS:(s + 1) * KS, :] = dkt_acc[s].T.astype(BF16)
                dv_ref[s * KS:(s + 1) * KS, :] = dvt_acc[s].T.astype(BF16)

    return pl.pallas_call(
        body, name=name, grid=(H, nq),
        in_specs=[pl.BlockSpec((QB, HEAD_DIM), lambda h, i: (i, h)),
                  pl.BlockSpec((T, HEAD_DIM), lambda h, i: (0, H + h)),
                  pl.BlockSpec((T, HEAD_DIM), lambda h, i: (0, 2 * H + h)),
                  pl.BlockSpec((QB, HEAD_DIM), lambda h, i: (i, h)),
                  pl.BlockSpec((1, QB, LANES), lambda h, i: (h, i, 0))],
        out_specs=[pl.BlockSpec((QB, HEAD_DIM), lambda h, i: (i, h)),
                   pl.BlockSpec((T, HEAD_DIM), lambda h, i: (0, h)),
                   pl.BlockSpec((T, HEAD_DIM), lambda h, i: (0, h))],
        out_shape=[jax.ShapeDtypeStruct((T, D), BF16)] * 3,
        scratch_shapes=[pltpu.VMEM((n_key_steps, HEAD_DIM, KS), F32), pltpu.VMEM((n_key_steps, HEAD_DIM, KS), F32),
                        pltpu.VMEM((QB, HEAD_DIM), F32)] + [pltpu.VMEM((QB, KS), F32)] * 10
                       + [pltpu.VMEM((ns + 1, QB, KS), F32)],
        compiler_params=_params(2),
    )(qkv, qkv, qkv, do, r_tot)


def _row_to_col(row):
    C = row.shape[1]
    eye = _iota2((C, C), 0) == _iota2((C, C), 1)
    return jnp.sum(jnp.where(eye, row, 0.0), axis=1, keepdims=True)


def _lane_sum(x):
    return jnp.sum(x, axis=-1, keepdims=True)


BWD_STATE_SKEW = 3


def _chunk_head_bwd(load, take_dstate, give):
    q, k, v, gc, beta, s_in, t_inv, do = load()
    C = q.shape[0]
    r, c = _iota2((C, C), 0), _iota2((C, C), 1)
    causal, strict = r >= c, r > c
    decay = jnp.where(causal, jnp.exp(jnp.where(causal, gc - _col_to_row(gc), 0.0)), 0.0)
    kb, vb = k * beta, v * beta
    eg = jnp.exp(gc)
    kbg = kb * eg
    g_last = gc[C - 1:C, :]
    e_tail = jnp.exp(g_last - gc)
    k_tail = k * e_tail
    gl = jnp.exp(g_last)
    qg = q * eg
    t_inv_t = t_inv.T
    kk = _dot_nt(kb, k)
    u = _dot(t_inv, vb)
    w = _dot(t_inv, kbg)
    qk = _dot_nt(q, k)
    d_qg = _dot_nt(do, s_in)
    ds_state = _dot_tn(qg, do)
    yield
    ds_out = take_dstate()
    low = jnp.where(strict, kk * decay, 0.0)
    attn = qk * decay
    w_state = _dot(w, s_in)
    d_vnew_intra = _dot_tn(attn, do)
    d_vnew_state = _dot(k_tail, ds_out)
    yield
    v_new = u - w_state
    d_vnew = d_vnew_intra + d_vnew_state
    d_ktail = _dot_nt(v_new, ds_out)
    d_attn_raw = _dot_nt(do, v_new)
    d_w = -_dot_nt(d_vnew, s_in)
    ds_w = _dot_tn(w, d_vnew)
    d_vb = _dot(t_inv_t, d_vnew)
    d_tinv_u = _dot_nt(d_vnew, vb)
    yield
    d_gl = _lane_sum(jnp.sum(s_in * ds_out, axis=0, keepdims=True))
    d_attn = jnp.where(causal, d_attn_raw, 0.0)
    give(ds_in=ds_out * gl + ds_state - ds_w)
    d_kbg = _dot(t_inv_t, d_w)
    d_tinv_w = _dot_nt(d_w, kbg)
    d_qk = d_attn * decay
    dq_intra = _dot(d_qk, k)
    dk_intra = _dot_tn(d_qk, q)
    yield
    inner = _dot(t_inv_t, d_tinv_u + d_tinv_w)
    yield
    d_low_raw = _dot_nt(inner, t_inv)
    yield
    d_low = jnp.where(strict, -d_low_raw, 0.0)
    d_kk = d_low * decay
    d_kb_low = _dot(d_kk, k)
    dk_low = _dot_tn(d_kk, kb)
    yield
    d_kb = d_kb_low + d_kbg * eg
    give(dq=dq_intra + d_qg * eg, dk=dk_low + dk_intra + d_ktail * e_tail + d_kb * beta, dv=d_vb * beta)
    dbeta = _lane_sum(d_kb * k + d_vb * v)
    m = d_low * low + d_attn * attn
    tail = d_ktail * k_tail
    d_g_last = d_gl * gl + _lane_sum(jnp.sum(tail, axis=0, keepdims=True))
    dgc = (_lane_sum(m) - _row_to_col(jnp.sum(m, axis=0, keepdims=True))
           + _lane_sum(d_qg * qg + d_kbg * kbg - tail))
    return dgc + jnp.where(_iota2((C, 1), 0) == C - 1, d_g_last, 0.0), dbeta


def _chunk_bwd(qkv, beta, gc, s_all, t_all, do, d_model, name):
    T = qkv.shape[0]
    D = d_model
    H = D // HEAD_DIM
    N = T // CHUNK
    G = CHUNKS_PER_STEP
    R = G * CHUNK
    n_steps = N // G

    def body(q_ref, k_ref, v_ref, beta_ref, gc_ref, s_ref, t_ref, do_ref, dqkv_ref, dbeta_ref, dg_ref, ds_ref):
        @pl.when(pl.program_id(0) == 0)
        def _():
            ds_ref[...] = jnp.zeros_like(ds_ref)

        lane = _iota2((CHUNK, LANES), 1)
        heads = [slice(h * HEAD_DIM, (h + 1) * HEAD_DIM) for h in range(H)]
        d_states = [[ds_ref[h] for h in range(H)]] + [[None] * H for _ in range(G)]

        def head(g, h):
            sub = G - 1 - g
            rows = slice(sub * CHUNK, (sub + 1) * CHUNK)

            def give(ds_in=None, dq=None, dk=None, dv=None):
                if ds_in is not None:
                    d_states[g + 1][h] = ds_in
                if dq is not None:
                    dqkv_ref[rows, h * HEAD_DIM:(h + 1) * HEAD_DIM] = dq
                    dqkv_ref[rows, D + h * HEAD_DIM:D + (h + 1) * HEAD_DIM] = dk
                    dqkv_ref[rows, 2 * D + h * HEAD_DIM:2 * D + (h + 1) * HEAD_DIM] = dv

            load = lambda: (q_ref[rows, heads[h]], k_ref[rows, heads[h]], v_ref[rows, heads[h]], gc_ref[rows, h:h + 1],
                            beta_ref[rows, h:h + 1], s_ref[sub, h], t_ref[sub, h], do_ref[rows, heads[h]])
            return _chunk_head_bwd(load, lambda: d_states[g][h], give)

        def finish(g, results):
            rows = slice((G - 1 - g) * CHUNK, (G - g) * CHUNK)
            dgc_all = jnp.zeros((CHUNK, LANES), F32)
            dbeta_all = jnp.zeros((CHUNK, LANES), F32)
            for h, (dgc, dbeta) in enumerate(results):
                dgc_all = jnp.where(lane == h, dgc, dgc_all)
                dbeta_all = jnp.where(lane == h, dbeta, dbeta_all)
            dbeta_ref[rows, :] = dbeta_all
            dg_ref[rows, :] = _dot_mask(_chunk_tri(CHUNK, upper=True), dgc_all)

        _lockstep([[head(g, h) for h in range(H)] for g in range(G)], BWD_STATE_SKEW, finish)
        for h in range(H):
            ds_ref[h] = d_states[G][h]

    rev = lambda n: n_steps - 1 - n
    return pl.pallas_call(
        body, name=name, grid=(n_steps,),
        in_specs=[pl.BlockSpec((R, D), lambda n: (rev(n), 0)), pl.BlockSpec((R, D), lambda n: (rev(n), 1)),
                  pl.BlockSpec((R, D), lambda n: (rev(n), 2)),
                  pl.BlockSpec((R, LANES), lambda n: (rev(n), 0)), pl.BlockSpec((R, LANES), lambda n: (rev(n), 0)),
                  pl.BlockSpec((G, H, HEAD_DIM, HEAD_DIM), lambda n: (rev(n), 0, 0, 0)),
                  pl.BlockSpec((G, H, CHUNK, CHUNK), lambda n: (rev(n), 0, 0, 0)),
                  pl.BlockSpec((R, D), lambda n: (rev(n), 0))],
        out_specs=[pl.BlockSpec((R, 3 * D), lambda n: (rev(n), 0)),
                   pl.BlockSpec((R, LANES), lambda n: (rev(n), 0)), pl.BlockSpec((R, LANES), lambda n: (rev(n), 0))],
        out_shape=[jax.ShapeDtypeStruct((T, 3 * D), F32), jax.ShapeDtypeStruct((T, LANES), F32),
                   jax.ShapeDtypeStruct((T, LANES), F32)],
        scratch_shapes=[pltpu.VMEM((H, HEAD_DIM, HEAD_DIM), F32)], compiler_params=_params(1),
    )(qkv, qkv, qkv, beta, gc, s_all, t_all, do)


def _gates_bwd(p, gate_params, col0, dbeta, dg, name):
    T = p.shape[0]
    tm = _pick(T, (256, 128, 64))

    def body(pb_ref, pa_ref, gp_ref, dbeta_ref, dg_ref, dp_ref, dgp_ref):
        gp = gp_ref[...]
        _, vjp = jax.vjp(_gates_math, pb_ref[...], pa_ref[...], gp[0:1, :], gp[1:2, :])
        dpb, dpa, d_alog, d_dt = vjp((dbeta_ref[...], dg_ref[...]))
        dp_ref[:, 0:LANES] = dpb.astype(BF16)
        dp_ref[:, LANES:2 * LANES] = dpa.astype(BF16)

        @pl.when(pl.program_id(0) == 0)
        def _():
            dgp_ref[...] = jnp.zeros_like(dgp_ref)

        dgp_ref[0:1, :] += d_alog
        dgp_ref[1:2, :] += d_dt

    return pl.pallas_call(
        body, name=name, grid=(T // tm,),
        in_specs=[pl.BlockSpec((tm, LANES), lambda i: (i, col0)), pl.BlockSpec((tm, LANES), lambda i: (i, col0 + 1)),
                  pl.BlockSpec((8, LANES), lambda i: (0, 0)),
                  pl.BlockSpec((tm, LANES), lambda i: (i, 0)), pl.BlockSpec((tm, LANES), lambda i: (i, 0))],
        out_specs=[pl.BlockSpec((tm, 2 * LANES), lambda i: (i, 0)), pl.BlockSpec((8, LANES), lambda i: (0, 0))],
        out_shape=[jax.ShapeDtypeStruct((T, 2 * LANES), BF16), jax.ShapeDtypeStruct((8, LANES), F32)],
        compiler_params=_params(1),
    )(p, p, gate_params, dbeta, dg)


def _conv_bwd_act(p, conv_w, dqkv, d_model, name):
    T = p.shape[0]
    D = d_model
    H = D // HEAD_DIM
    tm = _pick(T, (256, 128, 64))

    def columns(j, prev_outs):
        def body(cur_ref, prev_ref, w_ref, dout_ref, *rest):
            dc_ref, dw_ref = rest[-2:]
            i = pl.program_id(0)
            prev = prev_ref[...] * (i > 0).astype(F32)
            taps = _conv_taps(cur_ref[...], prev)
            w = w_ref[...]
            c = sum(taps[k] * w[k:k + 1, :] for k in range(4))
            _, vjp = jax.vjp(lambda cc: _qkv_post(cc, j, H), c)
            (dc,) = vjp(dout_ref[...])
            dc_ref[...] = dc

            @pl.when(i == 0)
            def _():
                dw_ref[...] = jnp.zeros_like(dw_ref)

            for k in range(4):
                dw_ref[k:k + 1, :] += jnp.sum(dc * taps[k], axis=0, keepdims=True)

        specs, operands, aliases = _alias_previous(prev_outs, 4)
        return pl.pallas_call(
            body, name="%s%d" % (name, j), grid=(T // tm,),
            in_specs=[pl.BlockSpec((tm, D), lambda i: (i, j)),
                      pl.BlockSpec((8, D), lambda i: (jnp.maximum(i * (tm // 8) - 1, 0), j)),
                      pl.BlockSpec((4, D), lambda i: (0, j)),
                      pl.BlockSpec((tm, D), lambda i: (i, j))] + specs,
            out_specs=[pl.BlockSpec((tm, D), lambda i: (i, j)), pl.BlockSpec((4, D), lambda i: (0, j))],
            out_shape=[jax.ShapeDtypeStruct((T, 3 * D), F32), jax.ShapeDtypeStruct((4, 3 * D), F32)],
            input_output_aliases=aliases, compiler_params=_params(1),
        )(p, p, conv_w, dqkv, *operands)

    return _column_calls(columns)


def _conv_bwd_input(dc, conv_w, name):
    T, D3 = dc.shape
    D = D3 // 3
    tm = _pick(T, (256, 128, 64))
    n_t = T // tm

    def body(cur_ref, next_ref, w_ref, dp_ref):
        i = pl.program_id(0)
        cur = cur_ref[...]
        nxt = next_ref[...] * (i < n_t - 1).astype(F32)
        ext = jnp.concatenate([cur, nxt], axis=0)
        w = w_ref[...]
        acc = cur * w[3:4, :]
        for s in (1, 2, 3):
            acc = acc + pltpu.roll(ext, tm + 8 - s, 0)[0:tm] * w[3 - s:4 - s, :]
        dp_ref[...] = acc.astype(BF16)

    return pl.pallas_call(
        body, name=name, grid=(n_t, 3),
        in_specs=[pl.BlockSpec((tm, D), lambda i, j: (i, j)),
                  pl.BlockSpec((8, D), lambda i, j: (jnp.minimum((i + 1) * (tm // 8), T // 8 - 1), j)),
                  pl.BlockSpec((4, D), lambda i, j: (0, j))],
        out_specs=pl.BlockSpec((tm, D), lambda i, j: (i, j)),
        out_shape=jax.ShapeDtypeStruct((T, D3), BF16), compiler_params=_params(2),
    )(dc, dc, conv_w)


def _comm_call(body, arrays, out_shape, n_remote, n_local, name):
    any_spec = pl.BlockSpec(memory_space=pl.ANY)
    return pl.pallas_call(
        body, name=name, in_specs=[any_spec] * len(arrays), out_specs=[any_spec] * len(out_shape), out_shape=out_shape,
        scratch_shapes=[pltpu.SemaphoreType.DMA((n_remote,)), pltpu.SemaphoreType.DMA((n_remote,)),
                        pltpu.SemaphoreType.DMA((n_local,))],
        compiler_params=pltpu.CompilerParams(has_side_effects=True),
    )(*arrays)


def _gather_shards(arrays, name):
    n = len(arrays)

    def body(*refs):
        ins, outs = refs[:n], refs[n:2 * n]
        send_sems, recv_sems, local_sems = refs[2 * n:]
        x, y, c = lax.axis_index("x"), lax.axis_index("y"), lax.axis_index("c")
        me, sibling = (x, y, c), (x, y, 1 - c)
        chips = [(1 - x, y), (x, 1 - y), (1 - x, 1 - y)]

        def copy(a, k, block, to, src=None):
            dst = outs[a].at[4 * block[0] + 2 * block[1] + block[2]]
            return pltpu.make_async_remote_copy(
                src_ref=dst if src is None else src, dst_ref=dst, send_sem=send_sems.at[a * 7 + k],
                recv_sem=recv_sems.at[a * 7 + k], device_id=to, device_id_type=MESH_ID)

        local = [pltpu.make_async_copy(ins[a], outs[a].at[4 * x + 2 * y + c], local_sems.at[a]) for a in range(n)]
        first = [copy(a, 0, me, sibling, src=ins[a]) for a in range(n)]
        first += [copy(a, 1 + j, me, (*chip, c), src=ins[a]) for j, chip in enumerate(chips) for a in range(n)]
        for cp in local + first:
            cp.start()
        passed = []
        for j, chip in enumerate(chips):
            for a in range(n):
                copy(a, 1 + j, (*chip, c), me).wait_recv()
                passed.append(copy(a, 4 + j, (*chip, c), sibling))
                passed[-1].start()
        for a in range(n):
            copy(a, 0, sibling, me).wait_recv()
            for j, chip in enumerate(chips):
                copy(a, 4 + j, (*chip, 1 - c), me).wait_recv()
        for cp in first + passed:
            cp.wait_send()
        for cp in local:
            cp.wait()

    out_shape = [jax.ShapeDtypeStruct((N_DEV,) + a.shape, a.dtype) for a in arrays]
    return _comm_call(body, arrays, out_shape, 7 * n, n, name)


def _pair_exchange(arrays, name):
    n = len(arrays)

    def body(*refs):
        ins, pair = refs[:n], refs[n:2 * n]
        send_sems, recv_sems, _ = refs[2 * n:]
        x, y, c = lax.axis_index("x"), lax.axis_index("y"), lax.axis_index("c")
        sends = [pltpu.make_async_remote_copy(
            src_ref=ins[a].at[1 - c], dst_ref=pair[a], send_sem=send_sems.at[a], recv_sem=recv_sems.at[a],
            device_id=(x, y, 1 - c), device_id_type=MESH_ID) for a in range(n)]
        for cp in sends:
            cp.start()
        for cp in sends:
            cp.wait_recv()
        for cp in sends:
            cp.wait_send()

    out_shape = [jax.ShapeDtypeStruct(a.shape[1:], a.dtype) for a in arrays]
    return _comm_call(body, arrays, out_shape, n, 1, name)


def _pair_add(own, pair, name, out_dtype):
    _, R, C = own.shape
    tr = next((t for t in (256, 128, 64, 32, 16) if R % t == 0), R)

    def body(a_ref, b_ref, o_ref):
        o_ref[...] = (a_ref[...].astype(F32) + b_ref[...].astype(F32)).astype(out_dtype)

    spec = pl.BlockSpec((1, tr, C), lambda q, i: (q, i, 0))
    return pl.pallas_call(body, name=name, grid=(4, R // tr), in_specs=[spec, spec], out_specs=spec,
                          out_shape=jax.ShapeDtypeStruct(own.shape, out_dtype), compiler_params=_params(2))(own, pair)


def _chip_exchange(arrays, name):
    n = len(arrays)

    def body(*refs):
        ins, outs = refs[:n], refs[n:2 * n]
        send_sems, recv_sems, local_sems = refs[2 * n:]
        x, y, c = lax.axis_index("x"), lax.axis_index("y"), lax.axis_index("c")
        my_chip = 2 * x + y
        chips = [(1 - x, y), (x, 1 - y), (1 - x, 1 - y)]
        local = [pltpu.make_async_copy(ins[a].at[my_chip], outs[a].at[my_chip], local_sems.at[a]) for a in range(n)]
        sends = [pltpu.make_async_remote_copy(
            src_ref=ins[a].at[2 * px + py], dst_ref=outs[a].at[my_chip], send_sem=send_sems.at[a * 3 + j],
            recv_sem=recv_sems.at[a * 3 + j], device_id=(px, py, c), device_id_type=MESH_ID)
            for j, (px, py) in enumerate(chips) for a in range(n)]
        arrivals = [pltpu.make_async_remote_copy(
            src_ref=ins[a].at[my_chip], dst_ref=outs[a].at[2 * px + py], send_sem=send_sems.at[a * 3 + j],
            recv_sem=recv_sems.at[a * 3 + j], device_id=(px, py, c), device_id_type=MESH_ID)
            for j, (px, py) in enumerate(chips) for a in range(n)]
        for cp in local + sends:
            cp.start()
        for cp in arrivals:
            cp.wait_recv()
        for cp in sends:
            cp.wait_send()
        for cp in local:
            cp.wait()

    out_shape = [jax.ShapeDtypeStruct(a.shape, a.dtype) for a in arrays]
    return _comm_call(body, arrays, out_shape, 3 * n, n, name)


def _reduce_adamw(recv, w, m, v, name):
    S, R, C = recv.shape
    tr = next((t for t in (256, 128, 64, 32, 16, 8) if R % t == 0), R)
    c1 = 1.0 - ADAM_B1 ** ADAM_STEP
    c2 = 1.0 - ADAM_B2 ** ADAM_STEP

    def body(r_ref, w_ref, m_ref, v_ref, g_ref, d_ref, nm_ref, nv_ref):
        g = r_ref[0].astype(F32)
        for s in range(1, S):
            g = g + r_ref[s].astype(F32)
        nm = ADAM_B1 * m_ref[...] + (1.0 - ADAM_B1) * g
        nv = ADAM_B2 * v_ref[...] + (1.0 - ADAM_B2) * (g * g)
        g_ref[...] = g
        nm_ref[...] = nm
        nv_ref[...] = nv
        d_ref[...] = -ADAM_LR * ((nm / c1) / (jnp.sqrt(nv / c2) + ADAM_EPS) + ADAM_WD * w_ref[...])

    spec = pl.BlockSpec((tr, C), lambda i: (i, 0))
    return pl.pallas_call(
        body, name=name, grid=(R // tr,),
        in_specs=[pl.BlockSpec((S, tr, C), lambda i: (0, i, 0)), spec, spec, spec], out_specs=[spec] * 4,
        out_shape=[jax.ShapeDtypeStruct((R, C), F32)] * 4, compiler_params=_params(1),
    )(recv, w, m, v)


def _forward_local(x, target, nw0, nw1, fw, wa_in, conv_w, gate_params, o_norm, wa_out, wb_in, wb_out):
    T, D = x.shape
    nD = D // LANES
    sv = {}
    sv["u0"], sv["pa"] = _mm_nn_from(_rmsnorm_head, [x, nw0], [0, None], wa_in, "a_in_proj", [(wa_in.shape[1], F32)])
    sv["qkv_a"] = _conv_fwd(sv["pa"], conv_w, D, "a_conv_fwd")
    sv["beta"], sv["gc"] = _gates_fwd(sv["pa"], gate_params, 4 * nD, "a_gates_fwd")
    sv["o_a"], sv["s_all"], sv["t_all"] = _chunk_fwd(sv["qkv_a"], sv["beta"], sv["gc"], D, "a_chunk_fwd")
    sv["y_a"], sv["h1"] = _mm_nn_from(_onorm_gate_head, [sv["o_a"], sv["pa"], o_norm], [0, 3, None], wa_out,
                                      "a_out_proj", [(D, F32)], add=x)
    sv["u1"], sv["qkv_b"], sv["gate_b"] = _mm_nn_from(_rmsnorm_head, [sv["h1"], nw1], [0, None], wb_in, "b_in_proj",
                                                      [(3 * D, BF16), (D, F32)])
    sv["o_b"], sv["r_b"] = _sb_fwd(sv["qkv_b"], D, "b_attn_fwd")
    sv["y_b"], sv["h2"] = _mm_nn_from(_gate_mul_head, [sv["o_b"], sv["gate_b"]], [0, 0], wb_out, "b_out_proj",
                                      [(D, F32)], add=sv["h1"])
    sv["dh2"], sv["loss"], sv["dfw"] = _final_loss(sv["h2"], fw, target, "final_loss")
    return sv


def _accumulate(acc_ref, value):
    @pl.when(pl.program_id(0) == 0)
    def _():
        acc_ref[...] = jnp.zeros_like(acc_ref)

    acc_ref[...] += jnp.broadcast_to(value, acc_ref.shape)


def _gate_mul_bwd_tail(dy, ins, outs):
    o_ref, g_ref = ins
    do_ref, dg_ref = outs
    g = g_ref[...]
    s = jax.nn.sigmoid(g)
    do_ref[...] = dy * (g * s)
    dg_ref[...] = (dy * o_ref[...] * (s + g * s * (1.0 - s))).astype(BF16)


def _rmsnorm_bwd_tail(du, ins, outs):
    x_ref, w_ref, dres_ref = ins
    dx_ref, dw_ref = outs
    dx, dw = _rms_bwd_math(x_ref[...], w_ref[...], du)
    dx_ref[...] = dres_ref[...] + dx
    _accumulate(dw_ref, dw)


def _onorm_gate_bwd_tail(dy, ins, outs):
    o_ref, z_ref, w_ref = ins
    do_ref, dz_ref, dw_ref = outs
    n_heads = o_ref.shape[1] // HEAD_DIM
    _, vjp = jax.vjp(functools.partial(_onorm_gate_math, n_heads=n_heads), o_ref[...], z_ref[...], w_ref[...])
    do, dz, dw = vjp(dy)
    do_ref[...] = do
    dz_ref[...] = dz.astype(BF16)
    _accumulate(dw_ref, dw)


def _backward_local(sv, x, nw0, nw1, wa_in, conv_w, gate_params, o_norm, wa_out, wb_in, wb_out):
    T, D = x.shape
    nD = D // LANES
    g = {}
    dh2 = sv["dh2"]
    act = lambda dtype: (jax.ShapeDtypeStruct((T, D), dtype), True)
    acc = lambda width: (jax.ShapeDtypeStruct((8, width), F32), False)
    g["wb_out"] = _mm_tn(sv["y_b"], dh2, "b_out_proj_dw")
    do_b, dgate_b = _mm_nt_then([(dh2, wb_out)], "b_out_proj_dx", [sv["o_b"], sv["gate_b"]], [0, 0],
                                _gate_mul_bwd_tail, [act(F32), act(BF16)])
    dq_b, dk_b, dv_b = _sb_bwd(sv["qkv_b"], do_b, sv["r_b"], D, "b_attn_bwd")
    dp_b = [dq_b, dk_b, dv_b, dgate_b]
    g["wb_in"] = jnp.concatenate([_mm_tn(sv["u1"], dp, "b_in_proj_dw%d" % c) for c, dp in enumerate(dp_b)], axis=1)
    dh1, g["nw1"] = _mm_nt_then([(dp, wb_in[:, c * D:(c + 1) * D]) for c, dp in enumerate(dp_b)], "b_in_proj_dx",
                                [sv["h1"], nw1, dh2], [0, None, 0], _rmsnorm_bwd_tail, [act(F32), acc(D)])
    g["wa_out"] = _mm_tn(sv["y_a"], dh1, "a_out_proj_dw")
    do_a, dz_a, g["o_norm"] = _mm_nt_then([(dh1, wa_out)], "a_out_proj_dx", [sv["o_a"], sv["pa"], o_norm], [0, 3, None],
                                          _onorm_gate_bwd_tail, [act(F32), act(BF16), acc(HEAD_DIM)])
    dqkv_a, dbeta, dg = _chunk_bwd(sv["qkv_a"], sv["beta"], sv["gc"], sv["s_all"], sv["t_all"], do_a, D, "a_chunk_bwd")
    dp_gates, g["gate_params"] = _gates_bwd(sv["pa"], gate_params, 4 * nD, dbeta, dg, "a_gates_bwd")
    dc, g["conv_w"] = _conv_bwd_act(sv["pa"], conv_w, dqkv_a, D, "a_conv_bwd_act")
    dp_qkv = _conv_bwd_input(dc, conv_w, "a_conv_bwd_input")
    dp_a = [(dp_qkv, 0, 3 * D), (dz_a, 3 * D, 4 * D), (dp_gates, 4 * D, 4 * D + 2 * LANES)]
    g["wa_in"] = jnp.concatenate([_mm_tn(sv["u0"], dp, "a_in_proj_dw%d" % c) for c, (dp, _, _) in enumerate(dp_a)], axis=1)
    g["x"], g["nw0"] = _mm_nt_then([(dp, wa_in[:, lo:hi]) for dp, lo, hi in dp_a], "a_in_proj_dx",
                                   [x, nw0, dh1], [0, None, 0], _rmsnorm_bwd_tail, [act(F32), acc(D)])
    g["fw"] = sv["dfw"]
    return g


def kernel(x, norm_w, a_w_in, a_conv_w, a_a_log, a_dt_bias, a_o_norm, a_w_out, b_w_in, b_w_out, final_norm_w, loss_target, m_norm_w, m_a_w_in, m_a_conv_w, m_a_a_log, m_a_dt_bias, m_a_o_norm, m_a_w_out, m_b_w_in, m_b_w_out, m_final_norm_w, v_norm_w, v_a_w_in, v_a_conv_w, v_a_a_log, v_a_dt_bias, v_a_o_norm, v_a_w_out, v_b_w_in, v_b_w_out, v_final_norm_w):
    D = x.shape[-1]
    H = D // HEAD_DIM
    shards = [a_w_in[0].astype(BF16), a_w_out[0].astype(BF16), b_w_in[0].astype(BF16), b_w_out[0].astype(BF16), a_conv_w[0]]
    ga_in, ga_out, gb_in, gb_out, g_conv = _gather_shards(shards, "weights_gather")
    wa = ga_in.transpose(1, 0, 2).reshape(D, -1)
    pad = lambda w: jnp.pad(w, ((0, 0), (0, LANES - w.shape[1])))
    wa_in = jnp.concatenate([wa[:, :4 * D], pad(wa[:, 4 * D:4 * D + H]), pad(wa[:, 4 * D + H:])], axis=1)
    wa_out = ga_out.reshape(D, D)
    wb_in = gb_in.transpose(1, 0, 2).reshape(D, 4 * D)
    wb_out = gb_out.reshape(D, D)
    conv_w = g_conv.transpose(1, 0, 2).reshape(4, 3 * D)
    gate_params = jnp.zeros((8, LANES), F32).at[0, :H].set(a_a_log[0]).at[1, :H].set(a_dt_bias[0])
    nw0, nw1, fw = norm_w[0:1], norm_w[1:2], final_norm_w[None]

    sv = _forward_local(x[0], loss_target[0], nw0, nw1, fw, wa_in, conv_w, gate_params, a_o_norm, wa_out, wb_in, wb_out)
    g = _backward_local(sv, x[0], nw0, nw1, wa_in, conv_w, gate_params, a_o_norm, wa_out, wb_in, wb_out)

    gwa = g["wa_in"]
    gwa = jnp.concatenate([gwa[:, :4 * D], gwa[:, 4 * D:4 * D + H], gwa[:, 4 * D + LANES:4 * D + LANES + H]], axis=1)
    row = lambda v: jnp.pad(v.reshape(1, -1), ((0, 0), (0, D - v.size)))
    small = jnp.concatenate([g["nw0"][0:1], g["nw1"][0:1], g["fw"][0:1], row(g["gate_params"][0, :H]),
                             row(g["gate_params"][1, :H]), row(g["o_norm"][0]), row(sv["loss"][0, 0:1]),
                             jnp.zeros((1, D), F32)], axis=0)
    cols = lambda a: a.astype(BF16).reshape(a.shape[0], 4, 2, -1).transpose(2, 1, 0, 3)
    rows = lambda a: a.astype(BF16).reshape(4, 2, -1, a.shape[1]).transpose(1, 0, 2, 3)
    contribs = [cols(gwa), rows(g["wa_out"]), cols(g["wb_in"]), rows(g["wb_out"]), cols(g["conv_w"]),
                jnp.broadcast_to(small[None, None], (2, 4, 8, D))]
    pair = _pair_exchange(contribs, "grads_pair_exchange")
    my_core = lax.axis_index("c")
    own = [lax.dynamic_index_in_dim(a, my_core, axis=0, keepdims=False) for a in contribs]
    partial = [_pair_add(o, p, "grads_pair_add%d" % k, o.dtype) for k, (o, p) in enumerate(zip(own, pair))]
    ra_in, ra_out, rb_in, rb_out, r_conv, r_small = _chip_exchange(partial, "grads_chip_exchange")

    outs = {}
    for nm, recv, w, m, v in (("a_w_in", ra_in, a_w_in, m_a_w_in, v_a_w_in), ("a_w_out", ra_out, a_w_out, m_a_w_out, v_a_w_out),
                              ("b_w_in", rb_in, b_w_in, m_b_w_in, v_b_w_in), ("b_w_out", rb_out, b_w_out, m_b_w_out, v_b_w_out),
                              ("a_conv_w", r_conv, a_conv_w, m_a_conv_w, v_a_conv_w)):
        outs[nm] = tuple(o[None] for o in _reduce_adamw(recv, w[0], m[0], v[0], "adamw_" + nm))

    def pack(nw, alog, dt, onorm, fnw):
        return jnp.concatenate([nw, fnw.reshape(1, D), row(alog), row(dt), row(onorm), jnp.zeros((2, D), F32)], axis=0)

    s_g, s_d, s_m, s_v = _reduce_adamw(
        r_small, pack(norm_w, a_a_log, a_dt_bias, a_o_norm, final_norm_w),
        pack(m_norm_w, m_a_a_log, m_a_dt_bias, m_a_o_norm, m_final_norm_w),
        pack(v_norm_w, v_a_a_log, v_a_dt_bias, v_a_o_norm, v_final_norm_w), "adamw_small")
    loss = s_g[6, 0]
    for i, s in enumerate((s_g, s_d, s_m, s_v)):
        outs.setdefault("norm_w", [None] * 4)[i] = s[0:2]
        outs.setdefault("final_norm_w", [None] * 4)[i] = s[2]
        outs.setdefault("a_a_log", [None] * 4)[i] = s[3:4, :H]
        outs.setdefault("a_dt_bias", [None] * 4)[i] = s[4:5, :H]
        outs.setdefault("a_o_norm", [None] * 4)[i] = s[5:6, :HEAD_DIM]
    names = ("norm_w", "a_w_in", "a_conv_w", "a_a_log", "a_dt_bias", "a_o_norm", "a_w_out", "b_w_in", "b_w_out", "final_norm_w")
    return (loss, g["x"][None]) + tuple(outs[n][i] for i in range(4) for n in names)
```

```python
import functools

import jax
import jax.numpy as jnp
from jax import lax
from jax.experimental import pallas as pl
from jax.experimental.pallas import tpu as pltpu

F32 = jnp.float32
BF16 = jnp.bfloat16
EPS = 1e-6
LOG2_E = 1.4426950408889634
MASKED_SCORE = -1e30
HEAD_DIM = 128
CHUNK = 64
CHUNKS_PER_STEP = 4
ATTN_Q_BLOCKS_FWD = (512, 256)
ATTN_Q_BLOCKS_BWD = (512, 256)
ATTN_K_BLOCK = 128
LANES = 128
N_DEV = 8
VMEM_LIMIT_BYTES = 48 * 1024 * 1024
ADAM_LR, ADAM_B1, ADAM_B2, ADAM_EPS, ADAM_WD, ADAM_STEP = 0.001, 0.9, 0.999, 1e-08, 0.01, 10
MESH_ID = pl.DeviceIdType.MESH


def _pick(n, candidates):
    for c in candidates:
        if n % c == 0:
            return c
    raise ValueError(f"no tile for {n} in {candidates}")


def _params(n_grid_axes):
    return pltpu.CompilerParams(dimension_semantics=("arbitrary",) * n_grid_axes, vmem_limit_bytes=VMEM_LIMIT_BYTES)


def _dot(a, b):
    return jnp.dot(a.astype(BF16), b.astype(BF16), preferred_element_type=F32)


def _dot_nt(a, b):
    return lax.dot_general(a.astype(BF16), b.astype(BF16), (((1,), (1,)), ((), ())), preferred_element_type=F32)


def _dot_tn(a, b):
    return lax.dot_general(a.astype(BF16), b.astype(BF16), (((0,), (0,)), ((), ())), preferred_element_type=F32)


def _split2(x):
    hi = x.astype(BF16)
    lo = (x - hi.astype(F32)).astype(BF16)
    return hi, lo


def _split3(x):
    hi = x.astype(BF16)
    r = x - hi.astype(F32)
    mid = r.astype(BF16)
    lo = (r - mid.astype(F32)).astype(BF16)
    return hi, mid, lo


def _dot3(a, b):
    a_hi, a_lo = _split2(a)
    b_hi, b_lo = _split2(b)
    d = functools.partial(jnp.dot, preferred_element_type=F32)
    return d(a_hi, b_hi) + (d(a_hi, b_lo) + d(a_lo, b_hi))


def _silu(x):
    return x * jax.nn.sigmoid(x)


def _softplus(x):
    return jnp.maximum(x, 0.0) + jnp.log1p(jnp.exp(-jnp.abs(x)))


def _iota2(shape, axis):
    return lax.broadcasted_iota(jnp.int32, shape, axis)


def _rms_bwd_math(x, w, dy):
    r = lax.rsqrt(jnp.mean(x * x, axis=-1, keepdims=True) + EPS)
    xhat = x * r
    dxhat = dy * w
    dx = r * (dxhat - xhat * jnp.mean(dxhat * xhat, axis=-1, keepdims=True))
    dw = jnp.sum(dy * xhat, axis=0, keepdims=True)
    return dx, dw


def _rmsnorm_head(ins):
    x_ref, w_ref = ins
    xf = x_ref[...]
    r = lax.rsqrt(jnp.mean(xf * xf, axis=-1, keepdims=True) + EPS)
    return (xf * r * w_ref[...]).astype(BF16)


def _mm_nn_from(head, extras, extra_cols, b, name, outs, add=None):
    M = extras[0].shape[0]
    K, N = b.shape
    e = len(extras)
    tm = _pick(M, (256, 128))

    def body(*refs):
        ins, b_ref = refs[:e], refs[e]
        add_ref = refs[e + 1] if add is not None else None
        a_ref, out_refs = refs[e + 1 + (add is not None)], refs[e + 2 + (add is not None):]
        a = head(ins)
        a_ref[...] = a
        acc = jnp.dot(a, b_ref[...], preferred_element_type=F32)
        if add is not None:
            acc = acc + add_ref[...]
        col = 0
        for o_ref, (width, dtype) in zip(out_refs, outs):
            o_ref[...] = acc[:, col:col + width].astype(dtype)
            col += width

    def extra_spec(x, col):
        if col is None:
            return pl.BlockSpec(x.shape, lambda i: (0,) * x.ndim)
        return pl.BlockSpec((tm, K), lambda i: (i, col))

    in_specs = [extra_spec(x, col) for x, col in zip(extras, extra_cols)] + [pl.BlockSpec((K, N), lambda i: (0, 0))]
    args = list(extras) + [b]
    if add is not None:
        in_specs.append(pl.BlockSpec((tm, N), lambda i: (i, 0)))
        args.append(add)
    return pl.pallas_call(
        body, name=name, grid=(M // tm,), in_specs=in_specs,
        out_specs=[pl.BlockSpec((tm, K), lambda i: (i, 0))] + [pl.BlockSpec((tm, w), lambda i: (i, 0)) for w, _ in outs],
        out_shape=[jax.ShapeDtypeStruct((M, K), BF16)] + [jax.ShapeDtypeStruct((M, w), dt) for w, dt in outs],
        compiler_params=_params(1),
    )(*args)


def _mm_nt(pairs, name):
    def write(acc, ins, outs):
        outs[0][...] = acc

    return _mm_nt_then(pairs, name, [], [], write,
                       [(jax.ShapeDtypeStruct((pairs[0][0].shape[0], pairs[0][1].shape[0]), F32), True)])


def _mm_nt_then(pairs, name, extras, extra_cols, tail, outs):
    M = pairs[0][0].shape[0]
    n, e = len(pairs), len(extras)
    tm = _pick(M, (256, 128)) if e else _pick(M, (512, 256, 128))

    def body(*refs):
        acc = _dot_nt(refs[0][...], refs[n][...])
        for p in range(1, n):
            acc = acc + _dot_nt(refs[p][...], refs[n + p][...])
        tail(acc, refs[2 * n:2 * n + e], refs[2 * n + e:])

    def extra_spec(x, col):
        if col is None:
            return pl.BlockSpec(x.shape, lambda i: (0,) * x.ndim)
        return pl.BlockSpec((tm, outs[0][0].shape[1]), lambda i: (i, col))

    in_specs = ([pl.BlockSpec((tm, a.shape[1]), lambda i: (i, 0)) for a, _ in pairs]
                + [pl.BlockSpec(b.shape, lambda i: (0, 0)) for _, b in pairs]
                + [extra_spec(x, col) for x, col in zip(extras, extra_cols)])
    out_specs = [pl.BlockSpec((tm, s.shape[1]), lambda i: (i, 0)) if tiled else pl.BlockSpec(s.shape, lambda i: (0, 0))
                 for s, tiled in outs]
    res = pl.pallas_call(
        body, name=name, grid=(M // tm,), in_specs=in_specs, out_specs=out_specs,
        out_shape=[s for s, _ in outs], compiler_params=_params(1),
    )(*[a for a, _ in pairs], *[b for _, b in pairs], *extras)
    return res[0] if len(outs) == 1 else res


def _mm_tn(a, b, name):
    R, M = a.shape
    _, N = b.shape
    tn = _pick(N, (1536, 1024, 512, 256, 128))
    tr = _pick(R, (512, 256, 128))

    def body(a_ref, b_ref, o_ref):
        @pl.when(pl.program_id(1) == 0)
        def _():
            o_ref[...] = jnp.zeros_like(o_ref)

        o_ref[...] += _dot_tn(a_ref[...], b_ref[...])

    return pl.pallas_call(
        body, name=name, grid=(N // tn, R // tr),
        in_specs=[pl.BlockSpec((tr, M), lambda j, r: (r, 0)), pl.BlockSpec((tr, tn), lambda j, r: (r, j))],
        out_specs=pl.BlockSpec((M, tn), lambda j, r: (0, j)),
        out_shape=jax.ShapeDtypeStruct((M, N), F32), compiler_params=_params(2),
    )(a, b)


def _qkv_post(c, j, n_heads):
    s = _silu(c)
    if j == 2:
        return s
    parts = []
    for h in range(n_heads):
        sh = s[:, h * HEAD_DIM:(h + 1) * HEAD_DIM]
        parts.append(sh * lax.rsqrt(jnp.sum(sh * sh, axis=-1, keepdims=True) + EPS))
    n = jnp.concatenate(parts, axis=-1)
    return n * (HEAD_DIM ** -0.5) if j == 0 else n


def _column_calls(make_call, n_cols=3):
    outs = None
    for j in range(n_cols):
        outs = make_call(j, outs)
    return outs


def _alias_previous(prev, n_inputs):
    if prev is None:
        return [], [], {}
    prev = list(prev) if isinstance(prev, (list, tuple)) else [prev]
    return ([pl.BlockSpec(memory_space=pl.ANY)] * len(prev), prev, {n_inputs + k: k for k in range(len(prev))})


def _conv_taps(cur, halo_prev):
    tm = cur.shape[0]
    ext = jnp.concatenate([halo_prev, cur], axis=0)
    taps = [pltpu.roll(ext, s, 0)[8:8 + tm] for s in (3, 2, 1)]
    return taps + [cur]


def _conv_fwd(p, conv_w, d_model, name):
    T = p.shape[0]
    D = d_model
    H = D // HEAD_DIM
    tm = _pick(T, (256, 128, 64))

    def columns(j, prev_out):
        def body(cur_ref, prev_ref, w_ref, *rest):
            o_ref = rest[-1]
            prev = prev_ref[...] * (pl.program_id(0) > 0).astype(F32)
            taps = _conv_taps(cur_ref[...], prev)
            w = w_ref[...]
            c = sum(taps[k] * w[k:k + 1, :] for k in range(4))
            o_ref[...] = _qkv_post(c, j, H)

        specs, operands, aliases = _alias_previous(prev_out, 3)
        return pl.pallas_call(
            body, name="%s%d" % (name, j), grid=(T // tm,),
            in_specs=[pl.BlockSpec((tm, D), lambda i: (i, j)),
                      pl.BlockSpec((8, D), lambda i: (jnp.maximum(i * (tm // 8) - 1, 0), j)),
                      pl.BlockSpec((4, D), lambda i: (0, j))] + specs,
            out_specs=pl.BlockSpec((tm, D), lambda i: (i, j)),
            out_shape=jax.ShapeDtypeStruct((T, 3 * D), F32), input_output_aliases=aliases, compiler_params=_params(1),
        )(p, p, conv_w, *operands)

    return _column_calls(columns)


def _chunk_tri(tm, upper):
    r, c = _iota2((tm, tm), 0), _iota2((tm, tm), 1)
    same = (r // CHUNK) == (c // CHUNK)
    tri = (c >= r) if upper else (c <= r)
    return jnp.where(same & tri, 1.0, 0.0).astype(BF16)


def _dot_mask(mask_bf16, x):
    hi, mid, lo = _split3(x)
    d = functools.partial(jnp.dot, preferred_element_type=F32)
    return d(mask_bf16, hi) + (d(mask_bf16, mid) + d(mask_bf16, lo))


def _gates_math(pb, pa, a_log, dt_bias):
    beta = jax.nn.sigmoid(pb)
    g = -jnp.exp(a_log) * _softplus(pa + dt_bias)
    return beta, g


def _gates_fwd(p, gate_params, col0, name):
    T = p.shape[0]
    tm = _pick(T, (256, 128, 64))

    def body(pb_ref, pa_ref, gp_ref, beta_ref, gc_ref):
        gp = gp_ref[...]
        beta, g = _gates_math(pb_ref[...], pa_ref[...], gp[0:1, :], gp[1:2, :])
        beta_ref[...] = beta
        gc_ref[...] = _dot_mask(_chunk_tri(tm, upper=False), g)

    return pl.pallas_call(
        body, name=name, grid=(T // tm,),
        in_specs=[pl.BlockSpec((tm, LANES), lambda i: (i, col0)), pl.BlockSpec((tm, LANES), lambda i: (i, col0 + 1)),
                  pl.BlockSpec((8, LANES), lambda i: (0, 0))],
        out_specs=[pl.BlockSpec((tm, LANES), lambda i: (i, 0))] * 2,
        out_shape=[jax.ShapeDtypeStruct((T, LANES), F32)] * 2, compiler_params=_params(1),
    )(p, p, gate_params)


def _col_to_row(col):
    C = col.shape[0]
    eye = _iota2((C, C), 0) == _iota2((C, C), 1)
    return jnp.sum(jnp.where(eye, col, 0.0), axis=0, keepdims=True)


def _lockstep(groups, skew, finish):
    groups = [list(g) for g in groups]
    results = [[None] * len(g) for g in groups]
    left = [len(g) for g in groups]
    rnd = 0
    while any(left):
        for gi, gens in enumerate(groups):
            if rnd < gi * skew or not left[gi]:
                continue
            for idx, gen in enumerate(gens):
                if gen is None:
                    continue
                try:
                    next(gen)
                except StopIteration as done:
                    results[gi][idx] = done.value
                    gens[idx] = None
                    left[gi] -= 1
            if not left[gi]:
                finish(gi, results[gi])
        rnd += 1


def _unit_lower_inverse(low):
    C = low.shape[0]
    eye = (_iota2((C, C), 0) == _iota2((C, C), 1)).astype(F32)
    t = eye - low
    p = _dot3(low, low)
    yield
    n = 2
    while True:
        tp = _dot3(t, p)
        n *= 2
        if n < C:
            p = _dot3(p, p)
        yield
        t = t + tp
        if n >= C:
            return t


FWD_STATE_SKEW = 2


def _chunk_head_fwd(load, take_state, give):
    q, k, v, gc, beta = load()
    C = q.shape[0]
    r, c = _iota2((C, C), 0), _iota2((C, C), 1)
    causal, strict = r >= c, r > c
    decay = jnp.where(causal, jnp.exp(jnp.where(causal, gc - _col_to_row(gc), 0.0)), 0.0)
    kb, vb = k * beta, v * beta
    eg = jnp.exp(gc)
    kk = _dot_nt(kb, k)
    qk = _dot_nt(q, k)
    yield
    t_inv = yield from _unit_lower_inverse(jnp.where(strict, kk * decay, 0.0))
    give(t_inv=t_inv)
    u = _dot(t_inv, vb)
    w = _dot(t_inv, kb * eg)
    yield
    s_in = take_state()
    give(s_in=s_in)
    o_state = _dot(q * eg, s_in)
    w_state = _dot(w, s_in)
    yield
    v_new = u - w_state
    g_last = gc[C - 1:C, :]
    o_intra = _dot(qk * decay, v_new)
    s_add = _dot_tn(k * jnp.exp(g_last - gc), v_new)
    yield
    give(o=o_state + o_intra, s_out=s_in * jnp.exp(g_last) + s_add)


def _chunk_fwd(qkv, beta, gc, d_model, name):
    T = qkv.shape[0]
    D = d_model
    H = D // HEAD_DIM
    N = T // CHUNK
    G = CHUNKS_PER_STEP
    R = G * CHUNK

    def body(q_ref, k_ref, v_ref, beta_ref, gc_ref, o_ref, s_all_ref, t_all_ref, s_ref):
        @pl.when(pl.program_id(0) == 0)
        def _():
            s_ref[...] = jnp.zeros_like(s_ref)

        heads = [slice(h * HEAD_DIM, (h + 1) * HEAD_DIM) for h in range(H)]
        states = [[s_ref[h] for h in range(H)]] + [[None] * H for _ in range(G)]

        def head(sub, h):
            rows = slice(sub * CHUNK, (sub + 1) * CHUNK)

            def give(t_inv=None, s_in=None, o=None, s_out=None):
                if t_inv is not None:
                    t_all_ref[sub, h] = t_inv
                if s_in is not None:
                    s_all_ref[sub, h] = s_in
                if o is not None:
                    o_ref[rows, heads[h]] = o
                    states[sub + 1][h] = s_out

            load = lambda: (q_ref[rows, heads[h]], k_ref[rows, heads[h]], v_ref[rows, heads[h]],
                            gc_ref[rows, h:h + 1], beta_ref[rows, h:h + 1])
            return _chunk_head_fwd(load, lambda: states[sub][h], give)

        _lockstep([[head(sub, h) for h in range(H)] for sub in range(G)], FWD_STATE_SKEW, lambda sub, results: None)
        for h in range(H):
            s_ref[h] = states[G][h]

    return pl.pallas_call(
        body, name=name, grid=(N // G,),
        in_specs=[pl.BlockSpec((R, D), lambda n: (n, 0)), pl.BlockSpec((R, D), lambda n: (n, 1)),
                  pl.BlockSpec((R, D), lambda n: (n, 2)),
                  pl.BlockSpec((R, LANES), lambda n: (n, 0)), pl.BlockSpec((R, LANES), lambda n: (n, 0))],
        out_specs=[pl.BlockSpec((R, D), lambda n: (n, 0)),
                   pl.BlockSpec((G, H, HEAD_DIM, HEAD_DIM), lambda n: (n, 0, 0, 0)),
                   pl.BlockSpec((G, H, CHUNK, CHUNK), lambda n: (n, 0, 0, 0))],
        out_shape=[jax.ShapeDtypeStruct((T, D), F32), jax.ShapeDtypeStruct((N, H, HEAD_DIM, HEAD_DIM), F32),
                   jax.ShapeDtypeStruct((N, H, CHUNK, CHUNK), F32)],
        scratch_shapes=[pltpu.VMEM((H, HEAD_DIM, HEAD_DIM), F32)], compiler_params=_params(1),
    )(qkv, qkv, qkv, beta, gc)


def _onorm_gate_math(o, z, w, n_heads):
    parts = []
    for h in range(n_heads):
        hs = slice(h * HEAD_DIM, (h + 1) * HEAD_DIM)
        oh = o[:, hs]
        y = oh * lax.rsqrt(jnp.mean(oh * oh, axis=-1, keepdims=True) + EPS) * w
        parts.append(y * _silu(z[:, hs]))
    return jnp.concatenate(parts, axis=-1)


def _onorm_gate_head(ins):
    o_ref, z_ref, w_ref = ins
    return _onorm_gate_math(o_ref[...], z_ref[...], w_ref[...], o_ref.shape[1] // HEAD_DIM).astype(BF16)


def _diag_mask(qb, kb, d):
    return _iota2((qb, kb), 0) > _iota2((qb, kb), 1) + d * kb


def _run_trips(trip, n, state):
    def six(j, st):
        for u in range(6):
            st = trip(6 * j + u, st, u)
        return st

    state = lax.fori_loop(0, n // 6, six, state)
    base = (n // 6) * 6
    for u in (0, 2):
        pair = lambda st, u=u: trip(base + u + 1, trip(base + u, st, u), u + 1)
        state = lax.cond(n - base > u, pair, lambda st: st, state)
    return state


def _fill_score_masks(mask_buf, qb, kb, ns):
    mask_buf[0] = jnp.zeros(mask_buf.shape[1:], F32)
    for d in range(ns):
        for half in range(2):
            mask_buf[d + 1, :, half * kb:(half + 1) * kb] = jnp.where(_diag_mask(qb, kb, 2 * d + half), 0.0, MASKED_SCORE)


def _softplus_bits(w):
    u = 1.0 + jnp.exp2(jnp.minimum(w, 64.0))
    return jnp.maximum(w, jnp.log2(u)), 1.0 / u


def _incl_lower(n):
    return jnp.where((_iota2((2 * n, n), 0) & (n - 1)) >= _iota2((2 * n, n), 1), 1.0, 0.0).astype(BF16)


def _dot_cum(x, tri_bf16):
    hi, lo = _split2(x)
    return jnp.dot(jnp.concatenate([hi, lo], axis=1), tri_bf16, preferred_element_type=F32)


def _sb_fwd(qkv, d_model, name):
    T = qkv.shape[0]
    D = d_model
    H = D // HEAD_DIM
    QB = _pick(T, ATTN_Q_BLOCKS_FWD)
    KB = ATTN_K_BLOCK
    KS = 2 * KB
    ns = QB // KS
    nq = T // QB
    scale = HEAD_DIM ** -0.5

    def body(q_ref, k_ref, v_ref, o_ref, r_ref, w0, w1, w2, cum0, cum1, mask_buf):
        w_bufs, cum_bufs = (w0, w1, w2), (cum0, cum1)
        i = pl.program_id(1)

        @pl.when(i == 0)
        def _():
            _fill_score_masks(mask_buf, QB, KB, ns)

        q = q_ref[...]
        tri = _incl_lower(KB)
        n_tot = (i + 1) * ns

        def key_step(m):
            return jnp.maximum(n_tot - 1 - m, 0)

        def rows(ref, s):
            return ref[pl.ds(pl.multiple_of(s * KS, KS), KS), :]

        def scores(s, may_be_diagonal):
            w = _dot_nt(q, rows(k_ref, s)) * (scale * LOG2_E)
            return w + mask_buf[jnp.maximum(s - i * ns + 1, 0)] if may_be_diagonal else w

        def cums(w):
            sp = _softplus_bits(w)[0]
            return jnp.concatenate([_dot_cum(sp[:, :KB], tri), _dot_cum(sp[:, KB:], tri)], axis=1)

        def weights(w, cum, carry):
            a_r = jnp.exp2(w[:, KB:] - cum[:, KB:] - carry)
            carry = carry + cum[:, KB:KB + 1]
            a_l = jnp.exp2(w[:, :KB] - cum[:, :KB] - carry)
            return jnp.concatenate([a_l, a_r], axis=1).astype(BF16), carry + cum[:, 0:1]

        def trip(m, carry, ph):
            w_bufs[(ph + 2) % 3][...] = scores(key_step(m + 2), False)
            a, carry = weights(w_bufs[ph % 3][...], cum_bufs[ph % 2][...], carry)
            o_ref[...] += _dot(a, rows(v_ref, key_step(m)))
            cum_bufs[(ph + 1) % 2][...] = cums(w_bufs[(ph + 1) % 3][...])
            return carry

        o_ref[...] = jnp.zeros_like(o_ref)
        assert ns == 2
        w_bufs[0][...] = scores(key_step(0), True)
        w_bufs[1][...] = scores(key_step(1), True)
        cum_bufs[0][...] = cums(w_bufs[0][...])
        carry = _run_trips(trip, n_tot, jnp.zeros((QB, 1), F32))
        r_ref[0] = jnp.broadcast_to(carry, (QB, LANES))

    return pl.pallas_call(
        body, name=name, grid=(H, nq),
        in_specs=[pl.BlockSpec((QB, HEAD_DIM), lambda h, i: (i, h)),
                  pl.BlockSpec((T, HEAD_DIM), lambda h, i: (0, H + h)),
                  pl.BlockSpec((T, HEAD_DIM), lambda h, i: (0, 2 * H + h))],
        out_specs=[pl.BlockSpec((QB, HEAD_DIM), lambda h, i: (i, h)),
                   pl.BlockSpec((1, QB, LANES), lambda h, i: (h, i, 0))],
        out_shape=[jax.ShapeDtypeStruct((T, D), F32), jax.ShapeDtypeStruct((H, T, LANES), F32)],
        scratch_shapes=[pltpu.VMEM((QB, KS), F32)] * 5 + [pltpu.VMEM((ns + 1, QB, KS), F32)],
        compiler_params=_params(2),
    )(qkv, qkv, qkv)


def _gate_mul_head(ins):
    o_ref, g_ref = ins
    return (o_ref[...] * _silu(g_ref[...])).astype(BF16)


def _final_loss(h, w, target, name):
    T, D = h.shape
    tm = _pick(T, (512, 256, 128))

    def body(h_ref, w_ref, t_ref, dh_ref, loss_ref, dw_ref):
        x, w = h_ref[...], w_ref[...]
        r = lax.rsqrt(jnp.mean(x * x, axis=-1, keepdims=True) + EPS)
        err = x * r * w - t_ref[...]
        part = 0.5 * jnp.sum(jnp.mean(err * err, axis=-1, keepdims=True), axis=0, keepdims=True)
        dx, dw = _rms_bwd_math(x, w, err * (1.0 / D))
        dh_ref[...] = dx

        @pl.when(pl.program_id(0) == 0)
        def _():
            loss_ref[...] = jnp.zeros_like(loss_ref)
            dw_ref[...] = jnp.zeros_like(dw_ref)

        loss_ref[...] += jnp.broadcast_to(part, loss_ref.shape)
        dw_ref[...] += jnp.broadcast_to(dw, dw_ref.shape)

    return pl.pallas_call(
        body, name=name, grid=(T // tm,),
        in_specs=[pl.BlockSpec((tm, D), lambda i: (i, 0)), pl.BlockSpec((1, D), lambda i: (0, 0)),
                  pl.BlockSpec((tm, D), lambda i: (i, 0))],
        out_specs=[pl.BlockSpec((tm, D), lambda i: (i, 0)), pl.BlockSpec((8, LANES), lambda i: (0, 0)),
                   pl.BlockSpec((8, D), lambda i: (0, 0))],
        out_shape=[jax.ShapeDtypeStruct((T, D), F32), jax.ShapeDtypeStruct((8, LANES), F32),
                   jax.ShapeDtypeStruct((8, D), F32)],
        compiler_params=_params(1),
    )(h, w, target)


def _sb_bwd(qkv, do, r_tot, d_model, name):
    T = qkv.shape[0]
    D = d_model
    H = D // HEAD_DIM
    QB = _pick(T, ATTN_Q_BLOCKS_BWD)
    KB = ATTN_K_BLOCK
    KS = 2 * KB
    ns = QB // KS
    nq = T // QB
    n_key_steps = T // KS
    scale = HEAD_DIM ** -0.5

    def body(q_ref, k_ref, v_ref, do_ref, r_ref, dq_ref, dk_ref, dv_ref,
             dkt_acc, dvt_acc, dq_acc, w0, w1, w2, da0, da1, da2, cum0, cum1, sig0, sig1, mask_buf):
        w_bufs, da_bufs, cum_bufs, sig_bufs = (w0, w1, w2), (da0, da1, da2), (cum0, cum1), (sig0, sig1)
        i = pl.program_id(1)

        @pl.when(i == 0)
        def _():
            dkt_acc[...] = jnp.zeros_like(dkt_acc)
            dvt_acc[...] = jnp.zeros_like(dvt_acc)
            _fill_score_masks(mask_buf, QB, KB, ns)

        q = q_ref[...]
        do_blk = do_ref[...].astype(BF16)
        q_t = q.astype(F32).T.astype(BF16)
        do_t = do_ref[...].T.astype(BF16)
        row_total = r_ref[0][:, 0:1]
        tri_rev = _incl_lower(KB)
        tri_fwd = jnp.where(_iota2((KB, KB), 0) <= _iota2((KB, KB), 1), 1.0, 0.0).astype(BF16)
        n_tot = (i + 1) * ns

        def step_rows(ref, s):
            return ref[pl.ds(pl.multiple_of(s * KS, KS), KS), :]

        def scores(s):
            w = _dot_nt(q, step_rows(k_ref, s)) * (scale * LOG2_E) + mask_buf[jnp.maximum(s - i * ns + 1, 0)]
            return w, _dot_nt(do_blk, step_rows(v_ref, s))

        def softplus_sums(w):
            sp, one_minus_sig = _softplus_bits(w)
            cum = jnp.concatenate([_dot_cum(sp[:, :KB], tri_rev), _dot_cum(sp[:, KB:], tri_rev)], axis=1)
            return cum, 1.0 - one_minus_sig

        def weights(w, cum, da, left_sp):
            right_l = row_total - left_sp - cum[:, 0:1]
            right_r = right_l - cum[:, KB:KB + 1]
            a = jnp.concatenate([jnp.exp2(w[:, :KB] - cum[:, :KB] - right_l),
                                 jnp.exp2(w[:, KB:] - cum[:, KB:] - right_r)], axis=1)
            p = da * a
            cp = jnp.concatenate([_dot(p[:, :KB], tri_fwd), _dot(p[:, KB:], tri_fwd)], axis=1)
            return a.astype(BF16), p, cp, row_total - right_r

        def score_grads(p, cp, sig, left_p):
            cum_l = cp[:, :KB] + left_p
            cum_r = cp[:, KB:] + cum_l[:, KB - 1:KB]
            dz = p - sig * jnp.concatenate([cum_l, cum_r], axis=1)
            return dz.astype(BF16), cum_r[:, KB - 1:KB]

        def trip(m, st, ph):
            left_sp, left_p = st
            w_bufs[(ph + 2) % 3][...], da_bufs[(ph + 2) % 3][...] = scores(jnp.minimum(m + 2, n_tot - 1))
            a, p, cp, left_sp = weights(w_bufs[ph % 3][...], cum_bufs[ph % 2][...], da_bufs[ph % 3][...], left_sp)
            cum_bufs[(ph + 1) % 2][...], sig_bufs[(ph + 1) % 2][...] = softplus_sums(w_bufs[(ph + 1) % 3][...])
            dz, left_p = score_grads(p, cp, sig_bufs[ph % 2][...], left_p)
            dq_acc[...] += _dot(dz, step_rows(k_ref, m))
            dkt_acc[m] += jnp.dot(q_t, dz, preferred_element_type=F32) * scale
            dvt_acc[m] += jnp.dot(do_t, a, preferred_element_type=F32)
            return left_sp, left_p

        dq_acc[...] = jnp.zeros_like(dq_acc)
        w_bufs[0][...], da_bufs[0][...] = scores(0)
        w_bufs[1][...], da_bufs[1][...] = scores(jnp.minimum(1, n_tot - 1))
        cum_bufs[0][...], sig_bufs[0][...] = softplus_sums(w_bufs[0][...])
        zero_col = jnp.zeros((QB, 1), F32)
        _run_trips(trip, n_tot, (zero_col, zero_col))
        dq_ref[...] = (dq_acc[...] * scale).astype(BF16)

        @pl.when(i == nq - 1)
        def _():
            for s in range(n_key_steps):
                dk_ref[s * KS:(s + 1) * KS, :] = dkt_acc[s].T.astype(BF16)
                dv_ref[s * KS:(s + 1) * KS, :] = dvt_acc[s].T.astype(BF16)

    return pl.pallas_call(
        body, name=name, grid=(H, nq),
        in_specs=[pl.BlockSpec((QB, HEAD_DIM), lambda h, i: (i, h)),
                  pl.BlockSpec((T, HEAD_DIM), lambda h, i: (0, H + h)),
                  pl.BlockSpec((T, HEAD_DIM), lambda h, i: (0, 2 * H + h)),
                  pl.BlockSpec((QB, HEAD_DIM), lambda h, i: (i, h)),
                  pl.BlockSpec((1, QB, LANES), lambda h, i: (h, i, 0))],
        out_specs=[pl.BlockSpec((QB, HEAD_DIM), lambda h, i: (i, h)),
                   pl.BlockSpec((T, HEAD_DIM), lambda h, i: (0, h)),
                   pl.BlockSpec((T, HEAD_DIM), lambda h, i: (0, h))],
        out_shape=[jax.ShapeDtypeStruct((T, D), BF16)] * 3,
        scratch_shapes=[pltpu.VMEM((n_key_steps, HEAD_DIM, KS), F32), pltpu.VMEM((n_key_steps, HEAD_DIM, KS), F32),
                        pltpu.VMEM((QB, HEAD_DIM), F32)] + [pltpu.VMEM((QB, KS), F32)] * 10
                       + [pltpu.VMEM((ns + 1, QB, KS), F32)],
        compiler_params=_params(2),
    )(qkv, qkv, qkv, do, r_tot)


def _row_to_col(row):
    C = row.shape[1]
    eye = _iota2((C, C), 0) == _iota2((C, C), 1)
    return jnp.sum(jnp.where(eye, row, 0.0), axis=1, keepdims=True)


def _lane_sum(x):
    return jnp.sum(x, axis=-1, keepdims=True)


BWD_STATE_SKEW = 2


def _chunk_head_bwd(load, take_dstate, give):
    q, k, v, gc, beta, s_in, t_inv, do = load()
    C = q.shape[0]
    r, c = _iota2((C, C), 0), _iota2((C, C), 1)
    causal, strict = r >= c, r > c
    decay = jnp.where(causal, jnp.exp(jnp.where(causal, gc - _col_to_row(gc), 0.0)), 0.0)
    kb, vb = k * beta, v * beta
    eg = jnp.exp(gc)
    kbg = kb * eg
    g_last = gc[C - 1:C, :]
    e_tail = jnp.exp(g_last - gc)
    k_tail = k * e_tail
    gl = jnp.exp(g_last)
    qg = q * eg
    t_inv_t = t_inv.T
    kk = _dot_nt(kb, k)
    u = _dot(t_inv, vb)
    w = _dot(t_inv, kbg)
    qk = _dot_nt(q, k)
    d_qg = _dot_nt(do, s_in)
    ds_state = _dot_tn(qg, do)
    yield
    ds_out = take_dstate()
    low = jnp.where(strict, kk * decay, 0.0)
    attn = qk * decay
    w_state = _dot(w, s_in)
    d_vnew_intra = _dot_tn(attn, do)
    d_vnew_state = _dot(k_tail, ds_out)
    yield
    v_new = u - w_state
    d_vnew = d_vnew_intra + d_vnew_state
    d_ktail = _dot_nt(v_new, ds_out)
    d_attn_raw = _dot_nt(do, v_new)
    d_w = -_dot_nt(d_vnew, s_in)
    ds_w = _dot_tn(w, d_vnew)
    d_vb = _dot(t_inv_t, d_vnew)
    d_tinv_u = _dot_nt(d_vnew, vb)
    yield
    d_gl = _lane_sum(jnp.sum(s_in * ds_out, axis=0, keepdims=True))
    d_attn = jnp.where(causal, d_attn_raw, 0.0)
    give(ds_in=ds_out * gl + ds_state - ds_w)
    d_kbg = _dot(t_inv_t, d_w)
    d_tinv_w = _dot_nt(d_w, kbg)
    d_qk = d_attn * decay
    dq_intra = _dot(d_qk, k)
    dk_intra = _dot_tn(d_qk, q)
    yield
    inner = _dot(t_inv_t, d_tinv_u + d_tinv_w)
    yield
    d_low_raw = _dot_nt(inner, t_inv)
    yield
    d_low = jnp.where(strict, -d_low_raw, 0.0)
    d_kk = d_low * decay
    d_kb_low = _dot(d_kk, k)
    dk_low = _dot_tn(d_kk, kb)
    yield
    d_kb = d_kb_low + d_kbg * eg
    give(dq=dq_intra + d_qg * eg, dk=dk_low + dk_intra + d_ktail * e_tail + d_kb * beta, dv=d_vb * beta)
    dbeta = _lane_sum(d_kb * k + d_vb * v)
    m = d_low * low + d_attn * attn
    tail = d_ktail * k_tail
    d_g_last = d_gl * gl + _lane_sum(jnp.sum(tail, axis=0, keepdims=True))
    dgc = (_lane_sum(m) - _row_to_col(jnp.sum(m, axis=0, keepdims=True))
           + _lane_sum(d_qg * qg + d_kbg * kbg - tail))
    return dgc + jnp.where(_iota2((C, 1), 0) == C - 1, d_g_last, 0.0), dbeta


def _chunk_bwd(qkv, beta, gc, s_all, t_all, do, d_model, name):
    T = qkv.shape[0]
    D = d_model
    H = D // HEAD_DIM
    N = T // CHUNK
    G = CHUNKS_PER_STEP
    R = G * CHUNK
    n_steps = N // G

    def body(q_ref, k_ref, v_ref, beta_ref, gc_ref, s_ref, t_ref, do_ref, dqkv_ref, dbeta_ref, dg_ref, ds_ref):
        @pl.when(pl.program_id(0) == 0)
        def _():
            ds_ref[...] = jnp.zeros_like(ds_ref)

        lane = _iota2((CHUNK, LANES), 1)
        heads = [slice(h * HEAD_DIM, (h + 1) * HEAD_DIM) for h in range(H)]
        d_states = [[ds_ref[h] for h in range(H)]] + [[None] * H for _ in range(G)]

        def head(g, h):
            sub = G - 1 - g
            rows = slice(sub * CHUNK, (sub + 1) * CHUNK)

            def give(ds_in=None, dq=None, dk=None, dv=None):
                if ds_in is not None:
                    d_states[g + 1][h] = ds_in
                if dq is not None:
                    dqkv_ref[rows, h * HEAD_DIM:(h + 1) * HEAD_DIM] = dq
                    dqkv_ref[rows, D + h * HEAD_DIM:D + (h + 1) * HEAD_DIM] = dk
                    dqkv_ref[rows, 2 * D + h * HEAD_DIM:2 * D + (h + 1) * HEAD_DIM] = dv

            load = lambda: (q_ref[rows, heads[h]], k_ref[rows, heads[h]], v_ref[rows, heads[h]], gc_ref[rows, h:h + 1],
                            beta_ref[rows, h:h + 1], s_ref[sub, h], t_ref[sub, h], do_ref[rows, heads[h]])
            return _chunk_head_bwd(load, lambda: d_states[g][h], give)

        def finish(g, results):
            rows = slice((G - 1 - g) * CHUNK, (G - g) * CHUNK)
            dgc_all = jnp.zeros((CHUNK, LANES), F32)
            dbeta_all = jnp.zeros((CHUNK, LANES), F32)
            for h, (dgc, dbeta) in enumerate(results):
                dgc_all = jnp.where(lane == h, dgc, dgc_all)
                dbeta_all = jnp.where(lane == h, dbeta, dbeta_all)
            dbeta_ref[rows, :] = dbeta_all
            dg_ref[rows, :] = _dot_mask(_chunk_tri(CHUNK, upper=True), dgc_all)

        _lockstep([[head(g, h) for h in range(H)] for g in range(G)], BWD_STATE_SKEW, finish)
        for h in range(H):
            ds_ref[h] = d_states[G][h]

    rev = lambda n: n_steps - 1 - n
    return pl.pallas_call(
        body, name=name, grid=(n_steps,),
        in_specs=[pl.BlockSpec((R, D), lambda n: (rev(n), 0)), pl.BlockSpec((R, D), lambda n: (rev(n), 1)),
                  pl.BlockSpec((R, D), lambda n: (rev(n), 2)),
                  pl.BlockSpec((R, LANES), lambda n: (rev(n), 0)), pl.BlockSpec((R, LANES), lambda n: (rev(n), 0)),
                  pl.BlockSpec((G, H, HEAD_DIM, HEAD_DIM), lambda n: (rev(n), 0, 0, 0)),
                  pl.BlockSpec((G, H, CHUNK, CHUNK), lambda n: (rev(n), 0, 0, 0)),
                  pl.BlockSpec((R, D), lambda n: (rev(n), 0))],
        out_specs=[pl.BlockSpec((R, 3 * D), lambda n: (rev(n), 0)),
                   pl.BlockSpec((R, LANES), lambda n: (rev(n), 0)), pl.BlockSpec((R, LANES), lambda n: (rev(n), 0))],
        out_shape=[jax.ShapeDtypeStruct((T, 3 * D), F32), jax.ShapeDtypeStruct((T, LANES), F32),
                   jax.ShapeDtypeStruct((T, LANES), F32)],
        scratch_shapes=[pltpu.VMEM((H, HEAD_DIM, HEAD_DIM), F32)], compiler_params=_params(1),
    )(qkv, qkv, qkv, beta, gc, s_all, t_all, do)


def _gates_bwd(p, gate_params, col0, dbeta, dg, name):
    T = p.shape[0]
    tm = _pick(T, (256, 128, 64))

    def body(pb_ref, pa_ref, gp_ref, dbeta_ref, dg_ref, dp_ref, dgp_ref):
        gp = gp_ref[...]
        _, vjp = jax.vjp(_gates_math, pb_ref[...], pa_ref[...], gp[0:1, :], gp[1:2, :])
        dpb, dpa, d_alog, d_dt = vjp((dbeta_ref[...], dg_ref[...]))
        dp_ref[:, 0:LANES] = dpb.astype(BF16)
        dp_ref[:, LANES:2 * LANES] = dpa.astype(BF16)

        @pl.when(pl.program_id(0) == 0)
        def _():
            dgp_ref[...] = jnp.zeros_like(dgp_ref)

        dgp_ref[0:1, :] += d_alog
        dgp_ref[1:2, :] += d_dt

    return pl.pallas_call(
        body, name=name, grid=(T // tm,),
        in_specs=[pl.BlockSpec((tm, LANES), lambda i: (i, col0)), pl.BlockSpec((tm, LANES), lambda i: (i, col0 + 1)),
                  pl.BlockSpec((8, LANES), lambda i: (0, 0)),
                  pl.BlockSpec((tm, LANES), lambda i: (i, 0)), pl.BlockSpec((tm, LANES), lambda i: (i, 0))],
        out_specs=[pl.BlockSpec((tm, 2 * LANES), lambda i: (i, 0)), pl.BlockSpec((8, LANES), lambda i: (0, 0))],
        out_shape=[jax.ShapeDtypeStruct((T, 2 * LANES), BF16), jax.ShapeDtypeStruct((8, LANES), F32)],
        compiler_params=_params(1),
    )(p, p, gate_params, dbeta, dg)


def _conv_bwd_act(p, conv_w, dqkv, d_model, name):
    T = p.shape[0]
    D = d_model
    H = D // HEAD_DIM
    tm = _pick(T, (256, 128, 64))

    def columns(j, prev_outs):
        def body(cur_ref, prev_ref, w_ref, dout_ref, *rest):
            dc_ref, dw_ref = rest[-2:]
            i = pl.program_id(0)
            prev = prev_ref[...] * (i > 0).astype(F32)
            taps = _conv_taps(cur_ref[...], prev)
            w = w_ref[...]
            c = sum(taps[k] * w[k:k + 1, :] for k in range(4))
            _, vjp = jax.vjp(lambda cc: _qkv_post(cc, j, H), c)
            (dc,) = vjp(dout_ref[...])
            dc_ref[...] = dc

            @pl.when(i == 0)
            def _():
                dw_ref[...] = jnp.zeros_like(dw_ref)

            for k in range(4):
                dw_ref[k:k + 1, :] += jnp.sum(dc * taps[k], axis=0, keepdims=True)

        specs, operands, aliases = _alias_previous(prev_outs, 4)
        return pl.pallas_call(
            body, name="%s%d" % (name, j), grid=(T // tm,),
            in_specs=[pl.BlockSpec((tm, D), lambda i: (i, j)),
                      pl.BlockSpec((8, D), lambda i: (jnp.maximum(i * (tm // 8) - 1, 0), j)),
                      pl.BlockSpec((4, D), lambda i: (0, j)),
                      pl.BlockSpec((tm, D), lambda i: (i, j))] + specs,
            out_specs=[pl.BlockSpec((tm, D), lambda i: (i, j)), pl.BlockSpec((4, D), lambda i: (0, j))],
            out_shape=[jax.ShapeDtypeStruct((T, 3 * D), F32), jax.ShapeDtypeStruct((4, 3 * D), F32)],
            input_output_aliases=aliases, compiler_params=_params(1),
        )(p, p, conv_w, dqkv, *operands)

    return _column_calls(columns)


def _conv_bwd_input(dc, conv_w, name):
    T, D3 = dc.shape
    D = D3 // 3
    tm = _pick(T, (256, 128, 64))
    n_t = T // tm

    def body(cur_ref, next_ref, w_ref, dp_ref):
        i = pl.program_id(0)
        cur = cur_ref[...]
        nxt = next_ref[...] * (i < n_t - 1).astype(F32)
        ext = jnp.concatenate([cur, nxt], axis=0)
        w = w_ref[...]
        acc = cur * w[3:4, :]
        for s in (1, 2, 3):
            acc = acc + pltpu.roll(ext, tm + 8 - s, 0)[0:tm] * w[3 - s:4 - s, :]
        dp_ref[...] = acc.astype(BF16)

    return pl.pallas_call(
        body, name=name, grid=(n_t, 3),
        in_specs=[pl.BlockSpec((tm, D), lambda i, j: (i, j)),
                  pl.BlockSpec((8, D), lambda i, j: (jnp.minimum((i + 1) * (tm // 8), T // 8 - 1), j)),
                  pl.BlockSpec((4, D), lambda i, j: (0, j))],
        out_specs=pl.BlockSpec((tm, D), lambda i, j: (i, j)),
        out_shape=jax.ShapeDtypeStruct((T, D3), BF16), compiler_params=_params(2),
    )(dc, dc, conv_w)


def _comm_call(body, arrays, out_shape, n_remote, n_local, name):
    any_spec = pl.BlockSpec(memory_space=pl.ANY)
    return pl.pallas_call(
        body, name=name, in_specs=[any_spec] * len(arrays), out_specs=[any_spec] * len(out_shape), out_shape=out_shape,
        scratch_shapes=[pltpu.SemaphoreType.DMA((n_remote,)), pltpu.SemaphoreType.DMA((n_remote,)),
                        pltpu.SemaphoreType.DMA((n_local,))],
        compiler_params=pltpu.CompilerParams(has_side_effects=True),
    )(*arrays)


def _gather_shards(arrays, name):
    n = len(arrays)

    def body(*refs):
        ins, outs = refs[:n], refs[n:2 * n]
        send_sems, recv_sems, local_sems = refs[2 * n:]
        x, y, c = lax.axis_index("x"), lax.axis_index("y"), lax.axis_index("c")
        me, sibling = (x, y, c), (x, y, 1 - c)
        chips = [(1 - x, y), (x, 1 - y), (1 - x, 1 - y)]

        def copy(a, k, block, to, src=None):
            dst = outs[a].at[4 * block[0] + 2 * block[1] + block[2]]
            return pltpu.make_async_remote_copy(
                src_ref=dst if src is None else src, dst_ref=dst, send_sem=send_sems.at[a * 7 + k],
                recv_sem=recv_sems.at[a * 7 + k], device_id=to, device_id_type=MESH_ID)

        local = [pltpu.make_async_copy(ins[a], outs[a].at[4 * x + 2 * y + c], local_sems.at[a]) for a in range(n)]
        first = [copy(a, 0, me, sibling, src=ins[a]) for a in range(n)]
        first += [copy(a, 1 + j, me, (*chip, c), src=ins[a]) for j, chip in enumerate(chips) for a in range(n)]
        for cp in local + first:
            cp.start()
        passed = []
        for j, chip in enumerate(chips):
            for a in range(n):
                copy(a, 1 + j, (*chip, c), me).wait_recv()
                passed.append(copy(a, 4 + j, (*chip, c), sibling))
                passed[-1].start()
        for a in range(n):
            copy(a, 0, sibling, me).wait_recv()
            for j, chip in enumerate(chips):
                copy(a, 4 + j, (*chip, 1 - c), me).wait_recv()
        for cp in first + passed:
            cp.wait_send()
        for cp in local:
            cp.wait()

    out_shape = [jax.ShapeDtypeStruct((N_DEV,) + a.shape, a.dtype) for a in arrays]
    return _comm_call(body, arrays, out_shape, 7 * n, n, name)


def _pair_exchange(arrays, name):
    n = len(arrays)

    def body(*refs):
        ins, pair = refs[:n], refs[n:2 * n]
        send_sems, recv_sems, _ = refs[2 * n:]
        x, y, c = lax.axis_index("x"), lax.axis_index("y"), lax.axis_index("c")
        sends = [pltpu.make_async_remote_copy(
            src_ref=ins[a].at[1 - c], dst_ref=pair[a], send_sem=send_sems.at[a], recv_sem=recv_sems.at[a],
            device_id=(x, y, 1 - c), device_id_type=MESH_ID) for a in range(n)]
        for cp in sends:
            cp.start()
        for cp in sends:
            cp.wait_recv()
        for cp in sends:
            cp.wait_send()

    out_shape = [jax.ShapeDtypeStruct(a.shape[1:], a.dtype) for a in arrays]
    return _comm_call(body, arrays, out_shape, n, 1, name)


def _pair_add(own, pair, name, out_dtype):
    _, R, C = own.shape
    tr = next((t for t in (256, 128, 64, 32, 16) if R % t == 0), R)

    def body(a_ref, b_ref, o_ref):
        o_ref[...] = (a_ref[...].astype(F32) + b_ref[...].astype(F32)).astype(out_dtype)

    spec = pl.BlockSpec((1, tr, C), lambda q, i: (q, i, 0))
    return pl.pallas_call(body, name=name, grid=(4, R // tr), in_specs=[spec, spec], out_specs=spec,
                          out_shape=jax.ShapeDtypeStruct(own.shape, out_dtype), compiler_params=_params(2))(own, pair)


def _chip_exchange(arrays, name):
    n = len(arrays)

    def body(*refs):
        ins, outs = refs[:n], refs[n:2 * n]
        send_sems, recv_sems, local_sems = refs[2 * n:]
        x, y, c = lax.axis_index("x"), lax.axis_index("y"), lax.axis_index("c")
        my_chip = 2 * x + y
        chips = [(1 - x, y), (x, 1 - y), (1 - x, 1 - y)]
        local = [pltpu.make_async_copy(ins[a].at[my_chip], outs[a].at[my_chip], local_sems.at[a]) for a in range(n)]
        sends = [pltpu.make_async_remote_copy(
            src_ref=ins[a].at[2 * px + py], dst_ref=outs[a].at[my_chip], send_sem=send_sems.at[a * 3 + j],
            recv_sem=recv_sems.at[a * 3 + j], device_id=(px, py, c), device_id_type=MESH_ID)
            for j, (px, py) in enumerate(chips) for a in range(n)]
        arrivals = [pltpu.make_async_remote_copy(
            src_ref=ins[a].at[my_chip], dst_ref=outs[a].at[2 * px + py], send_sem=send_sems.at[a * 3 + j],
            recv_sem=recv_sems.at[a * 3 + j], device_id=(px, py, c), device_id_type=MESH_ID)
            for j, (px, py) in enumerate(chips) for a in range(n)]
        for cp in local + sends:
            cp.start()
        for cp in arrivals:
            cp.wait_recv()
        for cp in sends:
            cp.wait_send()
        for cp in local:
            cp.wait()

    out_shape = [jax.ShapeDtypeStruct(a.shape, a.dtype) for a in arrays]
    return _comm_call(body, arrays, out_shape, 3 * n, n, name)


def _reduce_adamw(recv, w, m, v, name):
    S, R, C = recv.shape
    tr = next((t for t in (256, 128, 64, 32, 16, 8) if R % t == 0), R)
    c1 = 1.0 - ADAM_B1 ** ADAM_STEP
    c2 = 1.0 - ADAM_B2 ** ADAM_STEP

    def body(r_ref, w_ref, m_ref, v_ref, g_ref, d_ref, nm_ref, nv_ref):
        g = r_ref[0].astype(F32)
        for s in range(1, S):
            g = g + r_ref[s].astype(F32)
        nm = ADAM_B1 * m_ref[...] + (1.0 - ADAM_B1) * g
        nv = ADAM_B2 * v_ref[...] + (1.0 - ADAM_B2) * (g * g)
        g_ref[...] = g
        nm_ref[...] = nm
        nv_ref[...] = nv
        d_ref[...] = -ADAM_LR * ((nm / c1) / (jnp.sqrt(nv / c2) + ADAM_EPS) + ADAM_WD * w_ref[...])

    spec = pl.BlockSpec((tr, C), lambda i: (i, 0))
    return pl.pallas_call(
        body, name=name, grid=(R // tr,),
        in_specs=[pl.BlockSpec((S, tr, C), lambda i: (0, i, 0)), spec, spec, spec], out_specs=[spec] * 4,
        out_shape=[jax.ShapeDtypeStruct((R, C), F32)] * 4, compiler_params=_params(1),
    )(recv, w, m, v)


def _forward_local(x, target, nw0, nw1, fw, wa_in, conv_w, gate_params, o_norm, wa_out, wb_in, wb_out):
    T, D = x.shape
    nD = D // LANES
    sv = {}
    sv["u0"], sv["pa"] = _mm_nn_from(_rmsnorm_head, [x, nw0], [0, None], wa_in, "a_in_proj", [(wa_in.shape[1], F32)])
    sv["qkv_a"] = _conv_fwd(sv["pa"], conv_w, D, "a_conv_fwd")
    sv["beta"], sv["gc"] = _gates_fwd(sv["pa"], gate_params, 4 * nD, "a_gates_fwd")
    sv["o_a"], sv["s_all"], sv["t_all"] = _chunk_fwd(sv["qkv_a"], sv["beta"], sv["gc"], D, "a_chunk_fwd")
    sv["y_a"], sv["h1"] = _mm_nn_from(_onorm_gate_head, [sv["o_a"], sv["pa"], o_norm], [0, 3, None], wa_out,
                                      "a_out_proj", [(D, F32)], add=x)
    sv["u1"], sv["qkv_b"], sv["gate_b"] = _mm_nn_from(_rmsnorm_head, [sv["h1"], nw1], [0, None], wb_in, "b_in_proj",
                                                      [(3 * D, BF16), (D, F32)])
    sv["o_b"], sv["r_b"] = _sb_fwd(sv["qkv_b"], D, "b_attn_fwd")
    sv["y_b"], sv["h2"] = _mm_nn_from(_gate_mul_head, [sv["o_b"], sv["gate_b"]], [0, 0], wb_out, "b_out_proj",
                                      [(D, F32)], add=sv["h1"])
    sv["dh2"], sv["loss"], sv["dfw"] = _final_loss(sv["h2"], fw, target, "final_loss")
    return sv


def _accumulate(acc_ref, value):
    @pl.when(pl.program_id(0) == 0)
    def _():
        acc_ref[...] = jnp.zeros_like(acc_ref)

    acc_ref[...] += jnp.broadcast_to(value, acc_ref.shape)


def _gate_mul_bwd_tail(dy, ins, outs):
    o_ref, g_ref = ins
    do_ref, dg_ref = outs
    g = g_ref[...]
    s = jax.nn.sigmoid(g)
    do_ref[...] = dy * (g * s)
    dg_ref[...] = (dy * o_ref[...] * (s + g * s * (1.0 - s))).astype(BF16)


def _rmsnorm_bwd_tail(du, ins, outs):
    x_ref, w_ref, dres_ref = ins
    dx_ref, dw_ref = outs
    dx, dw = _rms_bwd_math(x_ref[...], w_ref[...], du)
    dx_ref[...] = dres_ref[...] + dx
    _accumulate(dw_ref, dw)


def _onorm_gate_bwd_tail(dy, ins, outs):
    o_ref, z_ref, w_ref = ins
    do_ref, dz_ref, dw_ref = outs
    n_heads = o_ref.shape[1] // HEAD_DIM
    _, vjp = jax.vjp(functools.partial(_onorm_gate_math, n_heads=n_heads), o_ref[...], z_ref[...], w_ref[...])
    do, dz, dw = vjp(dy)
    do_ref[...] = do
    dz_ref[...] = dz.astype(BF16)
    _accumulate(dw_ref, dw)


def _backward_local(sv, x, nw0, nw1, wa_in, conv_w, gate_params, o_norm, wa_out, wb_in, wb_out):
    T, D = x.shape
    nD = D // LANES
    g = {}
    dh2 = sv["dh2"]
    act = lambda dtype: (jax.ShapeDtypeStruct((T, D), dtype), True)
    acc = lambda width: (jax.ShapeDtypeStruct((8, width), F32), False)
    g["wb_out"] = _mm_tn(sv["y_b"], dh2, "b_out_proj_dw")
    do_b, dgate_b = _mm_nt_then([(dh2, wb_out)], "b_out_proj_dx", [sv["o_b"], sv["gate_b"]], [0, 0],
                                _gate_mul_bwd_tail, [act(F32), act(BF16)])
    dq_b, dk_b, dv_b = _sb_bwd(sv["qkv_b"], do_b, sv["r_b"], D, "b_attn_bwd")
    dp_b = [dq_b, dk_b, dv_b, dgate_b]
    g["wb_in"] = jnp.concatenate([_mm_tn(sv["u1"], dp, "b_in_proj_dw%d" % c) for c, dp in enumerate(dp_b)], axis=1)
    dh1, g["nw1"] = _mm_nt_then([(dp, wb_in[:, c * D:(c + 1) * D]) for c, dp in enumerate(dp_b)], "b_in_proj_dx",
                                [sv["h1"], nw1, dh2], [0, None, 0], _rmsnorm_bwd_tail, [act(F32), acc(D)])
    g["wa_out"] = _mm_tn(sv["y_a"], dh1, "a_out_proj_dw")
    do_a, dz_a, g["o_norm"] = _mm_nt_then([(dh1, wa_out)], "a_out_proj_dx", [sv["o_a"], sv["pa"], o_norm], [0, 3, None],
                                          _onorm_gate_bwd_tail, [act(F32), act(BF16), acc(HEAD_DIM)])
    dqkv_a, dbeta, dg = _chunk_bwd(sv["qkv_a"], sv["beta"], sv["gc"], sv["s_all"], sv["t_all"], do_a, D, "a_chunk_bwd")
    dp_gates, g["gate_params"] = _gates_bwd(sv["pa"], gate_params, 4 * nD, dbeta, dg, "a_gates_bwd")
    dc, g["conv_w"] = _conv_bwd_act(sv["pa"], conv_w, dqkv_a, D, "a_conv_bwd_act")
    dp_qkv = _conv_bwd_input(dc, conv_w, "a_conv_bwd_input")
    dp_a = [(dp_qkv, 0, 3 * D), (dz_a, 3 * D, 4 * D), (dp_gates, 4 * D, 4 * D + 2 * LANES)]
    g["wa_in"] = jnp.concatenate([_mm_tn(sv["u0"], dp, "a_in_proj_dw%d" % c) for c, (dp, _, _) in enumerate(dp_a)], axis=1)
    g["x"], g["nw0"] = _mm_nt_then([(dp, wa_in[:, lo:hi]) for dp, lo, hi in dp_a], "a_in_proj_dx",
                                   [x, nw0, dh1], [0, None, 0], _rmsnorm_bwd_tail, [act(F32), acc(D)])
    g["fw"] = sv["dfw"]
    return g


def kernel(x, norm_w, a_w_in, a_conv_w, a_a_log, a_dt_bias, a_o_norm, a_w_out, b_w_in, b_w_out, final_norm_w, loss_target, m_norm_w, m_a_w_in, m_a_conv_w, m_a_a_log, m_a_dt_bias, m_a_o_norm, m_a_w_out, m_b_w_in, m_b_w_out, m_final_norm_w, v_norm_w, v_a_w_in, v_a_conv_w, v_a_a_log, v_a_dt_bias, v_a_o_norm, v_a_w_out, v_b_w_in, v_b_w_out, v_final_norm_w):
    D = x.shape[-1]
    H = D // HEAD_DIM
    shards = [a_w_in[0].astype(BF16), a_w_out[0].astype(BF16), b_w_in[0].astype(BF16), b_w_out[0].astype(BF16), a_conv_w[0]]
    ga_in, ga_out, gb_in, gb_out, g_conv = _gather_shards(shards, "weights_gather")
    wa = ga_in.transpose(1, 0, 2).reshape(D, -1)
    pad = lambda w: jnp.pad(w, ((0, 0), (0, LANES - w.shape[1])))
    wa_in = jnp.concatenate([wa[:, :4 * D], pad(wa[:, 4 * D:4 * D + H]), pad(wa[:, 4 * D + H:])], axis=1)
    wa_out = ga_out.reshape(D, D)
    wb_in = gb_in.transpose(1, 0, 2).reshape(D, 4 * D)
    wb_out = gb_out.reshape(D, D)
    conv_w = g_conv.transpose(1, 0, 2).reshape(4, 3 * D)
    gate_params = jnp.zeros((8, LANES), F32).at[0, :H].set(a_a_log[0]).at[1, :H].set(a_dt_bias[0])
    nw0, nw1, fw = norm_w[0:1], norm_w[1:2], final_norm_w[None]

    sv = _forward_local(x[0], loss_target[0], nw0, nw1, fw, wa_in, conv_w, gate_params, a_o_norm, wa_out, wb_in, wb_out)
    g = _backward_local(sv, x[0], nw0, nw1, wa_in, conv_w, gate_params, a_o_norm, wa_out, wb_in, wb_out)

    gwa = g["wa_in"]
    gwa = jnp.concatenate([gwa[:, :4 * D], gwa[:, 4 * D:4 * D + H], gwa[:, 4 * D + LANES:4 * D + LANES + H]], axis=1)
    row = lambda v: jnp.pad(v.reshape(1, -1), ((0, 0), (0, D - v.size)))
    small = jnp.concatenate([g["nw0"][0:1], g["nw1"][0:1], g["fw"][0:1], row(g["gate_params"][0, :H]),
                             row(g["gate_params"][1, :H]), row(g["o_norm"][0]), row(sv["loss"][0, 0:1]),
                             jnp.zeros((1, D), F32)], axis=0)
    cols = lambda a: a.astype(BF16).reshape(a.shape[0], 4, 2, -1).transpose(2, 1, 0, 3)
    rows = lambda a: a.astype(BF16).reshape(4, 2, -1, a.shape[1]).transpose(1, 0, 2, 3)
    contribs = [cols(gwa), rows(g["wa_out"]), cols(g["wb_in"]), rows(g["wb_out"]), cols(g["conv_w"]),
                jnp.broadcast_to(small[None, None], (2, 4, 8, D))]
    pair = _pair_exchange(contribs, "grads_pair_exchange")
    my_core = lax.axis_index("c")
    own = [lax.dynamic_index_in_dim(a, my_core, axis=0, keepdims=False) for a in contribs]
    partial = [_pair_add(o, p, "grads_pair_add%d" % k, o.dtype) for k, (o, p) in enumerate(zip(own, pair))]
    ra_in, ra_out, rb_in, rb_out, r_conv, r_small = _chip_exchange(partial, "grads_chip_exchange")

    outs = {}
    for nm, recv, w, m, v in (("a_w_in", ra_in, a_w_in, m_a_w_in, v_a_w_in), ("a_w_out", ra_out, a_w_out, m_a_w_out, v_a_w_out),
                              ("b_w_in", rb_in, b_w_in, m_b_w_in, v_b_w_in), ("b_w_out", rb_out, b_w_out, m_b_w_out, v_b_w_out),
                              ("a_conv_w", r_conv, a_conv_w, m_a_conv_w, v_a_conv_w)):
        outs[nm] = tuple(o[None] for o in _reduce_adamw(recv, w[0], m[0], v[0], "adamw_" + nm))

    def pack(nw, alog, dt, onorm, fnw):
        return jnp.concatenate([nw, fnw.reshape(1, D), row(alog), row(dt), row(onorm), jnp.zeros((2, D), F32)], axis=0)

    s_g, s_d, s_m, s_v = _reduce_adamw(
        r_small, pack(norm_w, a_a_log, a_dt_bias, a_o_norm, final_norm_w),
        pack(m_norm_w, m_a_a_log, m_a_dt_bias, m_a_o_norm, m_final_norm_w),
        pack(v_norm_w, v_a_a_log, v_a_dt_bias, v_a_o_norm, v_final_norm_w), "adamw_small")
    loss = s_g[6, 0]
    for i, s in enumerate((s_g, s_d, s_m, s_v)):
        outs.setdefault("norm_w", [None] * 4)[i] = s[0:2]
        outs.setdefault("final_norm_w", [None] * 4)[i] = s[2]
        outs.setdefault("a_a_log", [None] * 4)[i] = s[3:4, :H]
        outs.setdefault("a_dt_bias", [None] * 4)[i] = s[4:5, :H]
        outs.setdefault("a_o_norm", [None] * 4)[i] = s[5:6, :HEAD_DIM]
    names = ("norm_w", "a_w_in", "a_conv_w", "a_a_log", "a_dt_bias", "a_o_norm", "a_w_out", "b_w_in", "b_w_out", "final_norm_w")
    return (loss, g["x"][None]) + tuple(outs[n][i] for i in range(4) for n in names)
```

```python
import functools

import jax
import jax.numpy as jnp
from jax import lax
from jax.experimental import pallas as pl
from jax.experimental.pallas import tpu as pltpu

F32 = jnp.float32
BF16 = jnp.bfloat16
EPS = 1e-6
LOG2_E = 1.4426950408889634
MASKED_SCORE = -1e30
HEAD_DIM = 128
CHUNK = 64
CHUNKS_PER_STEP = 4
ATTN_Q_BLOCKS_FWD = (512, 256)
ATTN_Q_BLOCKS_BWD = (512, 256)
ATTN_K_BLOCK = 128
LANES = 128
N_DEV = 8
VMEM_LIMIT_BYTES = 48 * 1024 * 1024
ADAM_LR, ADAM_B1, ADAM_B2, ADAM_EPS, ADAM_WD, ADAM_STEP = 0.001, 0.9, 0.999, 1e-08, 0.01, 10
MESH_ID = pl.DeviceIdType.MESH


def _pick(n, candidates):
    for c in candidates:
        if n % c == 0:
            return c
    raise ValueError(f"no tile for {n} in {candidates}")


def _params(n_grid_axes):
    return pltpu.CompilerParams(dimension_semantics=("arbitrary",) * n_grid_axes, vmem_limit_bytes=VMEM_LIMIT_BYTES)


def _dot(a, b):
    return jnp.dot(a.astype(BF16), b.astype(BF16), preferred_element_type=F32)


def _dot_nt(a, b):
    return lax.dot_general(a.astype(BF16), b.astype(BF16), (((1,), (1,)), ((), ())), preferred_element_type=F32)


def _dot_tn(a, b):
    return lax.dot_general(a.astype(BF16), b.astype(BF16), (((0,), (0,)), ((), ())), preferred_element_type=F32)


def _split2(x):
    hi = x.astype(BF16)
    lo = (x - hi.astype(F32)).astype(BF16)
    return hi, lo


def _split3(x):
    hi = x.astype(BF16)
    r = x - hi.astype(F32)
    mid = r.astype(BF16)
    lo = (r - mid.astype(F32)).astype(BF16)
    return hi, mid, lo


def _dot3(a, b):
    a_hi, a_lo = _split2(a)
    b_hi, b_lo = _split2(b)
    d = functools.partial(jnp.dot, preferred_element_type=F32)
    return d(a_hi, b_hi) + (d(a_hi, b_lo) + d(a_lo, b_hi))


def _silu(x):
    return x * jax.nn.sigmoid(x)


def _softplus(x):
    return jnp.maximum(x, 0.0) + jnp.log1p(jnp.exp(-jnp.abs(x)))


def _iota2(shape, axis):
    return lax.broadcasted_iota(jnp.int32, shape, axis)


def _rms_bwd_math(x, w, dy):
    r = lax.rsqrt(jnp.mean(x * x, axis=-1, keepdims=True) + EPS)
    xhat = x * r
    dxhat = dy * w
    dx = r * (dxhat - xhat * jnp.mean(dxhat * xhat, axis=-1, keepdims=True))
    dw = jnp.sum(dy * xhat, axis=0, keepdims=True)
    return dx, dw


def _rmsnorm_head(ins):
    x_ref, w_ref = ins
    xf = x_ref[...]
    r = lax.rsqrt(jnp.mean(xf * xf, axis=-1, keepdims=True) + EPS)
    return (xf * r * w_ref[...]).astype(BF16)


def _mm_nn_from(head, extras, extra_cols, b, name, outs, add=None):
    M = extras[0].shape[0]
    K, N = b.shape
    e = len(extras)
    tm = _pick(M, (256, 128))

    def body(*refs):
        ins, b_ref = refs[:e], refs[e]
        add_ref = refs[e + 1] if add is not None else None
        a_ref, out_refs = refs[e + 1 + (add is not None)], refs[e + 2 + (add is not None):]
        a = head(ins)
        a_ref[...] = a
        acc = jnp.dot(a, b_ref[...], preferred_element_type=F32)
        if add is not None:
            acc = acc + add_ref[...]
        col = 0
        for o_ref, (width, dtype) in zip(out_refs, outs):
            o_ref[...] = acc[:, col:col + width].astype(dtype)
            col += width

    def extra_spec(x, col):
        if col is None:
            return pl.BlockSpec(x.shape, lambda i: (0,) * x.ndim)
        return pl.BlockSpec((tm, K), lambda i: (i, col))

    in_specs = [extra_spec(x, col) for x, col in zip(extras, extra_cols)] + [pl.BlockSpec((K, N), lambda i: (0, 0))]
    args = list(extras) + [b]
    if add is not None:
        in_specs.append(pl.BlockSpec((tm, N), lambda i: (i, 0)))
        args.append(add)
    return pl.pallas_call(
        body, name=name, grid=(M // tm,), in_specs=in_specs,
        out_specs=[pl.BlockSpec((tm, K), lambda i: (i, 0))] + [pl.BlockSpec((tm, w), lambda i: (i, 0)) for w, _ in outs],
        out_shape=[jax.ShapeDtypeStruct((M, K), BF16)] + [jax.ShapeDtypeStruct((M, w), dt) for w, dt in outs],
        compiler_params=_params(1),
    )(*args)


def _mm_nt(pairs, name):
    def write(acc, ins, outs):
        outs[0][...] = acc

    return _mm_nt_then(pairs, name, [], [], write,
                       [(jax.ShapeDtypeStruct((pairs[0][0].shape[0], pairs[0][1].shape[0]), F32), True)])


def _mm_nt_then(pairs, name, extras, extra_cols, tail, outs):
    M = pairs[0][0].shape[0]
    n, e = len(pairs), len(extras)
    tm = _pick(M, (256, 128)) if e else _pick(M, (512, 256, 128))

    def body(*refs):
        acc = _dot_nt(refs[0][...], refs[n][...])
        for p in range(1, n):
            acc = acc + _dot_nt(refs[p][...], refs[n + p][...])
        tail(acc, refs[2 * n:2 * n + e], refs[2 * n + e:])

    def extra_spec(x, col):
        if col is None:
            return pl.BlockSpec(x.shape, lambda i: (0,) * x.ndim)
        return pl.BlockSpec((tm, outs[0][0].shape[1]), lambda i: (i, col))

    in_specs = ([pl.BlockSpec((tm, a.shape[1]), lambda i: (i, 0)) for a, _ in pairs]
                + [pl.BlockSpec(b.shape, lambda i: (0, 0)) for _, b in pairs]
                + [extra_spec(x, col) for x, col in zip(extras, extra_cols)])
    out_specs = [pl.BlockSpec((tm, s.shape[1]), lambda i: (i, 0)) if tiled else pl.BlockSpec(s.shape, lambda i: (0, 0))
                 for s, tiled in outs]
    res = pl.pallas_call(
        body, name=name, grid=(M // tm,), in_specs=in_specs, out_specs=out_specs,
        out_shape=[s for s, _ in outs], compiler_params=_params(1),
    )(*[a for a, _ in pairs], *[b for _, b in pairs], *extras)
    return res[0] if len(outs) == 1 else res


def _mm_tn(a, b, name):
    R, M = a.shape
    _, N = b.shape
    tn = _pick(N, (1536, 1024, 512, 256, 128))
    tr = _pick(R, (1024, 512, 256, 128))

    def body(a_ref, b_ref, o_ref):
        @pl.when(pl.program_id(1) == 0)
        def _():
            o_ref[...] = jnp.zeros_like(o_ref)

        o_ref[...] += _dot_tn(a_ref[...], b_ref[...])

    return pl.pallas_call(
        body, name=name, grid=(N // tn, R // tr),
        in_specs=[pl.BlockSpec((tr, M), lambda j, r: (r, 0)), pl.BlockSpec((tr, tn), lambda j, r: (r, j))],
        out_specs=pl.BlockSpec((M, tn), lambda j, r: (0, j)),
        out_shape=jax.ShapeDtypeStruct((M, N), F32), compiler_params=_params(2),
    )(a, b)


def _qkv_post(c, j, n_heads):
    s = _silu(c)
    if j == 2:
        return s
    parts = []
    for h in range(n_heads):
        sh = s[:, h * HEAD_DIM:(h + 1) * HEAD_DIM]
        parts.append(sh * lax.rsqrt(jnp.sum(sh * sh, axis=-1, keepdims=True) + EPS))
    n = jnp.concatenate(parts, axis=-1)
    return n * (HEAD_DIM ** -0.5) if j == 0 else n


def _column_calls(make_call, n_cols=3):
    outs = None
    for j in range(n_cols):
        outs = make_call(j, outs)
    return outs


def _alias_previous(prev, n_inputs):
    if prev is None:
        return [], [], {}
    prev = list(prev) if isinstance(prev, (list, tuple)) else [prev]
    return ([pl.BlockSpec(memory_space=pl.ANY)] * len(prev), prev, {n_inputs + k: k for k in range(len(prev))})


def _conv_taps(cur, halo_prev):
    tm = cur.shape[0]
    ext = jnp.concatenate([halo_prev, cur], axis=0)
    taps = [pltpu.roll(ext, s, 0)[8:8 + tm] for s in (3, 2, 1)]
    return taps + [cur]


def _conv_fwd(p, conv_w, d_model, name):
    T = p.shape[0]
    D = d_model
    H = D // HEAD_DIM
    tm = _pick(T, (256, 128, 64))

    def columns(j, prev_out):
        def body(cur_ref, prev_ref, w_ref, *rest):
            o_ref = rest[-1]
            prev = prev_ref[...] * (pl.program_id(0) > 0).astype(F32)
            taps = _conv_taps(cur_ref[...], prev)
            w = w_ref[...]
            c = sum(taps[k] * w[k:k + 1, :] for k in range(4))
            o_ref[...] = _qkv_post(c, j, H)

        specs, operands, aliases = _alias_previous(prev_out, 3)
        return pl.pallas_call(
            body, name="%s%d" % (name, j), grid=(T // tm,),
            in_specs=[pl.BlockSpec((tm, D), lambda i: (i, j)),
                      pl.BlockSpec((8, D), lambda i: (jnp.maximum(i * (tm // 8) - 1, 0), j)),
                      pl.BlockSpec((4, D), lambda i: (0, j))] + specs,
            out_specs=pl.BlockSpec((tm, D), lambda i: (i, j)),
            out_shape=jax.ShapeDtypeStruct((T, 3 * D), F32), input_output_aliases=aliases, compiler_params=_params(1),
        )(p, p, conv_w, *operands)

    return _column_calls(columns)


def _chunk_tri(tm, upper):
    r, c = _iota2((tm, tm), 0), _iota2((tm, tm), 1)
    same = (r // CHUNK) == (c // CHUNK)
    tri = (c >= r) if upper else (c <= r)
    return jnp.where(same & tri, 1.0, 0.0).astype(BF16)


def _dot_mask(mask_bf16, x):
    hi, mid, lo = _split3(x)
    d = functools.partial(jnp.dot, preferred_element_type=F32)
    return d(mask_bf16, hi) + (d(mask_bf16, mid) + d(mask_bf16, lo))


def _gates_math(pb, pa, a_log, dt_bias):
    beta = jax.nn.sigmoid(pb)
    g = -jnp.exp(a_log) * _softplus(pa + dt_bias)
    return beta, g


def _gates_fwd(p, gate_params, col0, name):
    T = p.shape[0]
    tm = _pick(T, (256, 128, 64))

    def body(pb_ref, pa_ref, gp_ref, beta_ref, gc_ref):
        gp = gp_ref[...]
        beta, g = _gates_math(pb_ref[...], pa_ref[...], gp[0:1, :], gp[1:2, :])
        beta_ref[...] = beta
        gc_ref[...] = _dot_mask(_chunk_tri(tm, upper=False), g)

    return pl.pallas_call(
        body, name=name, grid=(T // tm,),
        in_specs=[pl.BlockSpec((tm, LANES), lambda i: (i, col0)), pl.BlockSpec((tm, LANES), lambda i: (i, col0 + 1)),
                  pl.BlockSpec((8, LANES), lambda i: (0, 0))],
        out_specs=[pl.BlockSpec((tm, LANES), lambda i: (i, 0))] * 2,
        out_shape=[jax.ShapeDtypeStruct((T, LANES), F32)] * 2, compiler_params=_params(1),
    )(p, p, gate_params)


def _col_to_row(col):
    C = col.shape[0]
    eye = _iota2((C, C), 0) == _iota2((C, C), 1)
    return jnp.sum(jnp.where(eye, col, 0.0), axis=0, keepdims=True)


def _lockstep(groups, skew, finish):
    groups = [list(g) for g in groups]
    results = [[None] * len(g) for g in groups]
    left = [len(g) for g in groups]
    rnd = 0
    while any(left):
        for gi, gens in enumerate(groups):
            if rnd < gi * skew or not left[gi]:
                continue
            for idx, gen in enumerate(gens):
                if gen is None:
                    continue
                try:
                    next(gen)
                except StopIteration as done:
                    results[gi][idx] = done.value
                    gens[idx] = None
                    left[gi] -= 1
            if not left[gi]:
                finish(gi, results[gi])
        rnd += 1


def _unit_lower_inverse(low):
    C = low.shape[0]
    eye = (_iota2((C, C), 0) == _iota2((C, C), 1)).astype(F32)
    t = eye - low
    p = _dot3(low, low)
    yield
    n = 2
    while True:
        tp = _dot3(t, p)
        n *= 2
        if n < C:
            p = _dot3(p, p)
        yield
        t = t + tp
        if n >= C:
            return t


FWD_STATE_SKEW = 2


def _chunk_head_fwd(load, take_state, give):
    q, k, v, gc, beta = load()
    C = q.shape[0]
    r, c = _iota2((C, C), 0), _iota2((C, C), 1)
    causal, strict = r >= c, r > c
    decay = jnp.where(causal, jnp.exp(jnp.where(causal, gc - _col_to_row(gc), 0.0)), 0.0)
    kb, vb = k * beta, v * beta
    eg = jnp.exp(gc)
    kk = _dot_nt(kb, k)
    qk = _dot_nt(q, k)
    yield
    t_inv = yield from _unit_lower_inverse(jnp.where(strict, kk * decay, 0.0))
    give(t_inv=t_inv)
    u = _dot(t_inv, vb)
    w = _dot(t_inv, kb * eg)
    yield
    s_in = take_state()
    give(s_in=s_in)
    o_state = _dot(q * eg, s_in)
    w_state = _dot(w, s_in)
    yield
    v_new = u - w_state
    g_last = gc[C - 1:C, :]
    o_intra = _dot(qk * decay, v_new)
    s_add = _dot_tn(k * jnp.exp(g_last - gc), v_new)
    yield
    give(o=o_state + o_intra, s_out=s_in * jnp.exp(g_last) + s_add)


def _chunk_fwd(qkv, beta, gc, d_model, name):
    T = qkv.shape[0]
    D = d_model
    H = D // HEAD_DIM
    N = T // CHUNK
    G = CHUNKS_PER_STEP
    R = G * CHUNK

    def body(q_ref, k_ref, v_ref, beta_ref, gc_ref, o_ref, s_all_ref, t_all_ref, s_ref):
        @pl.when(pl.program_id(0) == 0)
        def _():
            s_ref[...] = jnp.zeros_like(s_ref)

        heads = [slice(h * HEAD_DIM, (h + 1) * HEAD_DIM) for h in range(H)]
        states = [[s_ref[h] for h in range(H)]] + [[None] * H for _ in range(G)]

        def head(sub, h):
            rows = slice(sub * CHUNK, (sub + 1) * CHUNK)

            def give(t_inv=None, s_in=None, o=None, s_out=None):
                if t_inv is not None:
                    t_all_ref[sub, h] = t_inv
                if s_in is not None:
                    s_all_ref[sub, h] = s_in
                if o is not None:
                    o_ref[rows, heads[h]] = o
                    states[sub + 1][h] = s_out

            load = lambda: (q_ref[rows, heads[h]], k_ref[rows, heads[h]], v_ref[rows, heads[h]],
                            gc_ref[rows, h:h + 1], beta_ref[rows, h:h + 1])
            return _chunk_head_fwd(load, lambda: states[sub][h], give)

        _lockstep([[head(sub, h) for h in range(H)] for sub in range(G)], FWD_STATE_SKEW, lambda sub, results: None)
        for h in range(H):
            s_ref[h] = states[G][h]

    return pl.pallas_call(
        body, name=name, grid=(N // G,),
        in_specs=[pl.BlockSpec((R, D), lambda n: (n, 0)), pl.BlockSpec((R, D), lambda n: (n, 1)),
                  pl.BlockSpec((R, D), lambda n: (n, 2)),
                  pl.BlockSpec((R, LANES), lambda n: (n, 0)), pl.BlockSpec((R, LANES), lambda n: (n, 0))],
        out_specs=[pl.BlockSpec((R, D), lambda n: (n, 0)),
                   pl.BlockSpec((G, H, HEAD_DIM, HEAD_DIM), lambda n: (n, 0, 0, 0)),
                   pl.BlockSpec((G, H, CHUNK, CHUNK), lambda n: (n, 0, 0, 0))],
        out_shape=[jax.ShapeDtypeStruct((T, D), F32), jax.ShapeDtypeStruct((N, H, HEAD_DIM, HEAD_DIM), F32),
                   jax.ShapeDtypeStruct((N, H, CHUNK, CHUNK), F32)],
        scratch_shapes=[pltpu.VMEM((H, HEAD_DIM, HEAD_DIM), F32)], compiler_params=_params(1),
    )(qkv, qkv, qkv, beta, gc)


def _onorm_gate_math(o, z, w, n_heads):
    parts = []
    for h in range(n_heads):
        hs = slice(h * HEAD_DIM, (h + 1) * HEAD_DIM)
        oh = o[:, hs]
        y = oh * lax.rsqrt(jnp.mean(oh * oh, axis=-1, keepdims=True) + EPS) * w
        parts.append(y * _silu(z[:, hs]))
    return jnp.concatenate(parts, axis=-1)


def _onorm_gate_head(ins):
    o_ref, z_ref, w_ref = ins
    return _onorm_gate_math(o_ref[...], z_ref[...], w_ref[...], o_ref.shape[1] // HEAD_DIM).astype(BF16)


def _diag_mask(qb, kb, d):
    return _iota2((qb, kb), 0) > _iota2((qb, kb), 1) + d * kb


def _run_trips(trip, n, state):
    def six(j, st):
        for u in range(6):
            st = trip(6 * j + u, st, u)
        return st

    state = lax.fori_loop(0, n // 6, six, state)
    base = (n // 6) * 6
    for u in (0, 2):
        pair = lambda st, u=u: trip(base + u + 1, trip(base + u, st, u), u + 1)
        state = lax.cond(n - base > u, pair, lambda st: st, state)
    return state


def _fill_score_masks(mask_buf, qb, kb, ns):
    mask_buf[0] = jnp.zeros(mask_buf.shape[1:], F32)
    for d in range(ns):
        for half in range(2):
            mask_buf[d + 1, :, half * kb:(half + 1) * kb] = jnp.where(_diag_mask(qb, kb, 2 * d + half), 0.0, MASKED_SCORE)


def _softplus_bits(w):
    u = 1.0 + jnp.exp2(jnp.minimum(w, 64.0))
    return jnp.maximum(w, jnp.log2(u)), 1.0 / u


def _incl_lower(n):
    return jnp.where((_iota2((2 * n, n), 0) & (n - 1)) >= _iota2((2 * n, n), 1), 1.0, 0.0).astype(BF16)


def _dot_cum(x, tri_bf16):
    hi, lo = _split2(x)
    return jnp.dot(jnp.concatenate([hi, lo], axis=1), tri_bf16, preferred_element_type=F32)


def _sb_fwd(qkv, d_model, name):
    T = qkv.shape[0]
    D = d_model
    H = D // HEAD_DIM
    QB = _pick(T, ATTN_Q_BLOCKS_FWD)
    KB = ATTN_K_BLOCK
    KS = 2 * KB
    ns = QB // KS
    nq = T // QB
    scale = HEAD_DIM ** -0.5

    def body(q_ref, k_ref, v_ref, o_ref, r_ref, w0, w1, w2, cum0, cum1, mask_buf):
        w_bufs, cum_bufs = (w0, w1, w2), (cum0, cum1)
        i = pl.program_id(1)

        @pl.when(i == 0)
        def _():
            _fill_score_masks(mask_buf, QB, KB, ns)

        q = q_ref[...]
        tri = _incl_lower(KB)
        n_tot = (i + 1) * ns

        def key_step(m):
            return jnp.maximum(n_tot - 1 - m, 0)

        def rows(ref, s):
            return ref[pl.ds(pl.multiple_of(s * KS, KS), KS), :]

        def scores(s, may_be_diagonal):
            w = _dot_nt(q, rows(k_ref, s)) * (scale * LOG2_E)
            return w + mask_buf[jnp.maximum(s - i * ns + 1, 0)] if may_be_diagonal else w

        def cums(w):
            sp = _softplus_bits(w)[0]
            return jnp.concatenate([_dot_cum(sp[:, :KB], tri), _dot_cum(sp[:, KB:], tri)], axis=1)

        def weights(w, cum, carry):
            a_r = jnp.exp2(w[:, KB:] - cum[:, KB:] - carry)
            carry = carry + cum[:, KB:KB + 1]
            a_l = jnp.exp2(w[:, :KB] - cum[:, :KB] - carry)
            return jnp.concatenate([a_l, a_r], axis=1).astype(BF16), carry + cum[:, 0:1]

        def trip(m, carry, ph):
            w_bufs[(ph + 2) % 3][...] = scores(key_step(m + 2), False)
            a, carry = weights(w_bufs[ph % 3][...], cum_bufs[ph % 2][...], carry)
            o_ref[...] += _dot(a, rows(v_ref, key_step(m)))
            cum_bufs[(ph + 1) % 2][...] = cums(w_bufs[(ph + 1) % 3][...])
            return carry

        o_ref[...] = jnp.zeros_like(o_ref)
        assert ns == 2
        w_bufs[0][...] = scores(key_step(0), True)
        w_bufs[1][...] = scores(key_step(1), True)
        cum_bufs[0][...] = cums(w_bufs[0][...])
        carry = _run_trips(trip, n_tot, jnp.zeros((QB, 1), F32))
        r_ref[0] = jnp.broadcast_to(carry, (QB, LANES))

    return pl.pallas_call(
        body, name=name, grid=(H, nq),
        in_specs=[pl.BlockSpec((QB, HEAD_DIM), lambda h, i: (i, h)),
                  pl.BlockSpec((T, HEAD_DIM), lambda h, i: (0, H + h)),
                  pl.BlockSpec((T, HEAD_DIM), lambda h, i: (0, 2 * H + h))],
        out_specs=[pl.BlockSpec((QB, HEAD_DIM), lambda h, i: (i, h)),
                   pl.BlockSpec((1, QB, LANES), lambda h, i: (h, i, 0))],
        out_shape=[jax.ShapeDtypeStruct((T, D), F32), jax.ShapeDtypeStruct((H, T, LANES), F32)],
        scratch_shapes=[pltpu.VMEM((QB, KS), F32)] * 5 + [pltpu.VMEM((ns + 1, QB, KS), F32)],
        compiler_params=_params(2),
    )(qkv, qkv, qkv)


def _gate_mul_head(ins):
    o_ref, g_ref = ins
    return (o_ref[...] * _silu(g_ref[...])).astype(BF16)


def _final_loss(h, w, target, name):
    T, D = h.shape
    tm = _pick(T, (512, 256, 128))

    def body(h_ref, w_ref, t_ref, dh_ref, loss_ref, dw_ref):
        x, w = h_ref[...], w_ref[...]
        r = lax.rsqrt(jnp.mean(x * x, axis=-1, keepdims=True) + EPS)
        err = x * r * w - t_ref[...]
        part = 0.5 * jnp.sum(jnp.mean(err * err, axis=-1, keepdims=True), axis=0, keepdims=True)
        dx, dw = _rms_bwd_math(x, w, err * (1.0 / D))
        dh_ref[...] = dx

        @pl.when(pl.program_id(0) == 0)
        def _():
            loss_ref[...] = jnp.zeros_like(loss_ref)
            dw_ref[...] = jnp.zeros_like(dw_ref)

        loss_ref[...] += jnp.broadcast_to(part, loss_ref.shape)
        dw_ref[...] += jnp.broadcast_to(dw, dw_ref.shape)

    return pl.pallas_call(
        body, name=name, grid=(T // tm,),
        in_specs=[pl.BlockSpec((tm, D), lambda i: (i, 0)), pl.BlockSpec((1, D), lambda i: (0, 0)),
                  pl.BlockSpec((tm, D), lambda i: (i, 0))],
        out_specs=[pl.BlockSpec((tm, D), lambda i: (i, 0)), pl.BlockSpec((8, LANES), lambda i: (0, 0)),
                   pl.BlockSpec((8, D), lambda i: (0, 0))],
        out_shape=[jax.ShapeDtypeStruct((T, D), F32), jax.ShapeDtypeStruct((8, LANES), F32),
                   jax.ShapeDtypeStruct((8, D), F32)],
        compiler_params=_params(1),
    )(h, w, target)


def _sb_bwd(qkv, do, r_tot, d_model, name):
    T = qkv.shape[0]
    D = d_model
    H = D // HEAD_DIM
    QB = _pick(T, ATTN_Q_BLOCKS_BWD)
    KB = ATTN_K_BLOCK
    KS = 2 * KB
    ns = QB // KS
    nq = T // QB
    n_key_steps = T // KS
    scale = HEAD_DIM ** -0.5

    def body(q_ref, k_ref, v_ref, do_ref, r_ref, dq_ref, dk_ref, dv_ref,
             dkt_acc, dvt_acc, dq_acc, w0, w1, w2, da0, da1, da2, cum0, cum1, sig0, sig1, mask_buf):
        w_bufs, da_bufs, cum_bufs, sig_bufs = (w0, w1, w2), (da0, da1, da2), (cum0, cum1), (sig0, sig1)
        i = pl.program_id(1)

        @pl.when(i == 0)
        def _():
            dkt_acc[...] = jnp.zeros_like(dkt_acc)
            dvt_acc[...] = jnp.zeros_like(dvt_acc)
            _fill_score_masks(mask_buf, QB, KB, ns)

        q = q_ref[...]
        do_blk = do_ref[...].astype(BF16)
        q_t = q.astype(F32).T.astype(BF16)
        do_t = do_ref[...].T.astype(BF16)
        row_total = r_ref[0][:, 0:1]
        tri_rev = _incl_lower(KB)
        tri_fwd = jnp.where(_iota2((KB, KB), 0) <= _iota2((KB, KB), 1), 1.0, 0.0).astype(BF16)
        n_tot = (i + 1) * ns

        def step_rows(ref, s):
            return ref[pl.ds(pl.multiple_of(s * KS, KS), KS), :]

        def scores(s):
            w = _dot_nt(q, step_rows(k_ref, s)) * (scale * LOG2_E) + mask_buf[jnp.maximum(s - i * ns + 1, 0)]
            return w, _dot_nt(do_blk, step_rows(v_ref, s))

        def softplus_sums(w):
            sp, one_minus_sig = _softplus_bits(w)
            cum = jnp.concatenate([_dot_cum(sp[:, :KB], tri_rev), _dot_cum(sp[:, KB:], tri_rev)], axis=1)
            return cum, 1.0 - one_minus_sig

        def weights(w, cum, da, left_sp):
            right_l = row_total - left_sp - cum[:, 0:1]
            right_r = right_l - cum[:, KB:KB + 1]
            a = jnp.concatenate([jnp.exp2(w[:, :KB] - cum[:, :KB] - right_l),
                                 jnp.exp2(w[:, KB:] - cum[:, KB:] - right_r)], axis=1)
            p = da * a
            cp = jnp.concatenate([_dot(p[:, :KB], tri_fwd), _dot(p[:, KB:], tri_fwd)], axis=1)
            return a.astype(BF16), p, cp, row_total - right_r

        def score_grads(p, cp, sig, left_p):
            cum_l = cp[:, :KB] + left_p
            cum_r = cp[:, KB:] + cum_l[:, KB - 1:KB]
            dz = p - sig * jnp.concatenate([cum_l, cum_r], axis=1)
            return dz.astype(BF16), cum_r[:, KB - 1:KB]

        def trip(m, st, ph):
            left_sp, left_p = st
            w_bufs[(ph + 2) % 3][...], da_bufs[(ph + 2) % 3][...] = scores(jnp.minimum(m + 2, n_tot - 1))
            a, p, cp, left_sp = weights(w_bufs[ph % 3][...], cum_bufs[ph % 2][...], da_bufs[ph % 3][...], left_sp)
            cum_bufs[(ph + 1) % 2][...], sig_bufs[(ph + 1) % 2][...] = softplus_sums(w_bufs[(ph + 1) % 3][...])
            dz, left_p = score_grads(p, cp, sig_bufs[ph % 2][...], left_p)
            dq_acc[...] += _dot(dz, step_rows(k_ref, m))
            dkt_acc[m] += jnp.dot(q_t, dz, preferred_element_type=F32) * scale
            dvt_acc[m] += jnp.dot(do_t, a, preferred_element_type=F32)
            return left_sp, left_p

        dq_acc[...] = jnp.zeros_like(dq_acc)
        w_bufs[0][...], da_bufs[0][...] = scores(0)
        w_bufs[1][...], da_bufs[1][...] = scores(jnp.minimum(1, n_tot - 1))
        cum_bufs[0][...], sig_bufs[0][...] = softplus_sums(w_bufs[0][...])
        zero_col = jnp.zeros((QB, 1), F32)
        _run_trips(trip, n_tot, (zero_col, zero_col))
        dq_ref[...] = (dq_acc[...] * scale).astype(BF16)

        @pl.when(i == nq - 1)
        def _():
            for s in range(n_key_steps):
                dk_ref[s * KS:(s + 1) * KS, :] = dkt_acc[s].T.astype(BF16)
                dv_ref[s * KS:(s + 1) * KS, :] = dvt_acc[s].T.astype(BF16)

    return pl.pallas_call(
        body, name=name, grid=(H, nq),
        in_specs=[pl.BlockSpec((QB, HEAD_DIM), lambda h, i: (i, h)),
                  pl.BlockSpec((T, HEAD_DIM), lambda h, i: (0, H + h)),
                  pl.BlockSpec((T, HEAD_DIM), lambda h, i: (0, 2 * H + h)),
                  pl.BlockSpec((QB, HEAD_DIM), lambda h, i: (i, h)),
                  pl.BlockSpec((1, QB, LANES), lambda h, i: (h, i, 0))],
        out_specs=[pl.BlockSpec((QB, HEAD_DIM), lambda h, i: (i, h)),
                   pl.BlockSpec((T, HEAD_DIM), lambda h, i: (0, h)),
                   pl.BlockSpec((T, HEAD_DIM), lambda h, i: (0, h))],
        out_shape=[jax.ShapeDtypeStruct((T, D), BF16)] * 3,
        scratch_shapes=[pltpu.VMEM((n_key_steps, HEAD_DIM, KS), F32), pltpu.VMEM((n_key_steps, HEAD_DIM, KS), F32),
                        pltpu.VMEM((QB, HEAD_DIM), F32)] + [pltpu.VMEM((QB, KS), F32)] * 10
                       + [pltpu.VMEM((ns + 1, QB, KS), F32)],
        compiler_params=_params(2),
    )(qkv, qkv, qkv, do, r_tot)


def _row_to_col(row):
    C = row.shape[1]
    eye = _iota2((C, C), 0) == _iota2((C, C), 1)
    return jnp.sum(jnp.where(eye, row, 0.0), axis=1, keepdims=True)


def _lane_sum(x):
    return jnp.sum(x, axis=-1, keepdims=True)


BWD_STATE_SKEW = 2


def _chunk_head_bwd(load, take_dstate, give):
    q, k, v, gc, beta, s_in, t_inv, do = load()
    C = q.shape[0]
    r, c = _iota2((C, C), 0), _iota2((C, C), 1)
    causal, strict = r >= c, r > c
    decay = jnp.where(causal, jnp.exp(jnp.where(causal, gc - _col_to_row(gc), 0.0)), 0.0)
    kb, vb = k * beta, v * beta
    eg = jnp.exp(gc)
    kbg = kb * eg
    g_last = gc[C - 1:C, :]
    e_tail = jnp.exp(g_last - gc)
    k_tail = k * e_tail
    gl = jnp.exp(g_last)
    qg = q * eg
    t_inv_t = t_inv.T
    kk = _dot_nt(kb, k)
    u = _dot(t_inv, vb)
    w = _dot(t_inv, kbg)
    qk = _dot_nt(q, k)
    d_qg = _dot_nt(do, s_in)
    ds_state = _dot_tn(qg, do)
    yield
    ds_out = take_dstate()
    low = jnp.where(strict, kk * decay, 0.0)
    attn = qk * decay
    w_state = _dot(w, s_in)
    d_vnew_intra = _dot_tn(attn, do)
    d_vnew_state = _dot(k_tail, ds_out)
    yield
    v_new = u - w_state
    d_vnew = d_vnew_intra + d_vnew_state
    d_ktail = _dot_nt(v_new, ds_out)
    d_attn_raw = _dot_nt(do, v_new)
    d_w = -_dot_nt(d_vnew, s_in)
    ds_w = _dot_tn(w, d_vnew)
    d_vb = _dot(t_inv_t, d_vnew)
    d_tinv_u = _dot_nt(d_vnew, vb)
    yield
    d_gl = _lane_sum(jnp.sum(s_in * ds_out, axis=0, keepdims=True))
    d_attn = jnp.where(causal, d_attn_raw, 0.0)
    give(ds_in=ds_out * gl + ds_state - ds_w)
    d_kbg = _dot(t_inv_t, d_w)
    d_tinv_w = _dot_nt(d_w, kbg)
    d_qk = d_attn * decay
    dq_intra = _dot(d_qk, k)
    dk_intra = _dot_tn(d_qk, q)
    yield
    inner = _dot(t_inv_t, d_tinv_u + d_tinv_w)
    yield
    d_low_raw = _dot_nt(inner, t_inv)
    yield
    d_low = jnp.where(strict, -d_low_raw, 0.0)
    d_kk = d_low * decay
    d_kb_low = _dot(d_kk, k)
    dk_low = _dot_tn(d_kk, kb)
    yield
    d_kb = d_kb_low + d_kbg * eg
    give(dq=dq_intra + d_qg * eg, dk=dk_low + dk_intra + d_ktail * e_tail + d_kb * beta, dv=d_vb * beta)
    dbeta = _lane_sum(d_kb * k + d_vb * v)
    m = d_low * low + d_attn * attn
    tail = d_ktail * k_tail
    d_g_last = d_gl * gl + _lane_sum(jnp.sum(tail, axis=0, keepdims=True))
    dgc = (_lane_sum(m) - _row_to_col(jnp.sum(m, axis=0, keepdims=True))
           + _lane_sum(d_qg * qg + d_kbg * kbg - tail))
    return dgc + jnp.where(_iota2((C, 1), 0) == C - 1, d_g_last, 0.0), dbeta


def _chunk_bwd(qkv, beta, gc, s_all, t_all, do, d_model, name):
    T = qkv.shape[0]
    D = d_model
    H = D // HEAD_DIM
    N = T // CHUNK
    G = CHUNKS_PER_STEP
    R = G * CHUNK
    n_steps = N // G

    def body(q_ref, k_ref, v_ref, beta_ref, gc_ref, s_ref, t_ref, do_ref, dqkv_ref, dbeta_ref, dg_ref, ds_ref):
        @pl.when(pl.program_id(0) == 0)
        def _():
            ds_ref[...] = jnp.zeros_like(ds_ref)

        lane = _iota2((CHUNK, LANES), 1)
        heads = [slice(h * HEAD_DIM, (h + 1) * HEAD_DIM) for h in range(H)]
        d_states = [[ds_ref[h] for h in range(H)]] + [[None] * H for _ in range(G)]

        def head(g, h):
            sub = G - 1 - g
            rows = slice(sub * CHUNK, (sub + 1) * CHUNK)

            def give(ds_in=None, dq=None, dk=None, dv=None):
                if ds_in is not None:
                    d_states[g + 1][h] = ds_in
                if dq is not None:
                    dqkv_ref[rows, h * HEAD_DIM:(h + 1) * HEAD_DIM] = dq
                    dqkv_ref[rows, D + h * HEAD_DIM:D + (h + 1) * HEAD_DIM] = dk
                    dqkv_ref[rows, 2 * D + h * HEAD_DIM:2 * D + (h + 1) * HEAD_DIM] = dv

            load = lambda: (q_ref[rows, heads[h]], k_ref[rows, heads[h]], v_ref[rows, heads[h]], gc_ref[rows, h:h + 1],
                            beta_ref[rows, h:h + 1], s_ref[sub, h], t_ref[sub, h], do_ref[rows, heads[h]])
            return _chunk_head_bwd(load, lambda: d_states[g][h], give)

        def finish(g, results):
            rows = slice((G - 1 - g) * CHUNK, (G - g) * CHUNK)
            dgc_all = jnp.zeros((CHUNK, LANES), F32)
            dbeta_all = jnp.zeros((CHUNK, LANES), F32)
            for h, (dgc, dbeta) in enumerate(results):
                dgc_all = jnp.where(lane == h, dgc, dgc_all)
                dbeta_all = jnp.where(lane == h, dbeta, dbeta_all)
            dbeta_ref[rows, :] = dbeta_all
            dg_ref[rows, :] = _dot_mask(_chunk_tri(CHUNK, upper=True), dgc_all)

        _lockstep([[head(g, h) for h in range(H)] for g in range(G)], BWD_STATE_SKEW, finish)
        for h in range(H):
            ds_ref[h] = d_states[G][h]

    rev = lambda n: n_steps - 1 - n
    return pl.pallas_call(
        body, name=name, grid=(n_steps,),
        in_specs=[pl.BlockSpec((R, D), lambda n: (rev(n), 0)), pl.BlockSpec((R, D), lambda n: (rev(n), 1)),
                  pl.BlockSpec((R, D), lambda n: (rev(n), 2)),
                  pl.BlockSpec((R, LANES), lambda n: (rev(n), 0)), pl.BlockSpec((R, LANES), lambda n: (rev(n), 0)),
                  pl.BlockSpec((G, H, HEAD_DIM, HEAD_DIM), lambda n: (rev(n), 0, 0, 0)),
                  pl.BlockSpec((G, H, CHUNK, CHUNK), lambda n: (rev(n), 0, 0, 0)),
                  pl.BlockSpec((R, D), lambda n: (rev(n), 0))],
        out_specs=[pl.BlockSpec((R, 3 * D), lambda n: (rev(n), 0)),
                   pl.BlockSpec((R, LANES), lambda n: (rev(n), 0)), pl.BlockSpec((R, LANES), lambda n: (rev(n), 0))],
        out_shape=[jax.ShapeDtypeStruct((T, 3 * D), F32), jax.ShapeDtypeStruct((T, LANES), F32),
                   jax.ShapeDtypeStruct((T, LANES), F32)],
        scratch_shapes=[pltpu.VMEM((H, HEAD_DIM, HEAD_DIM), F32)], compiler_params=_params(1),
    )(qkv, qkv, qkv, beta, gc, s_all, t_all, do)


def _gates_bwd(p, gate_params, col0, dbeta, dg, name):
    T = p.shape[0]
    tm = _pick(T, (256, 128, 64))

    def body(pb_ref, pa_ref, gp_ref, dbeta_ref, dg_ref, dp_ref, dgp_ref):
        gp = gp_ref[...]
        _, vjp = jax.vjp(_gates_math, pb_ref[...], pa_ref[...], gp[0:1, :], gp[1:2, :])
        dpb, dpa, d_alog, d_dt = vjp((dbeta_ref[...], dg_ref[...]))
        dp_ref[:, 0:LANES] = dpb.astype(BF16)
        dp_ref[:, LANES:2 * LANES] = dpa.astype(BF16)

        @pl.when(pl.program_id(0) == 0)
        def _():
            dgp_ref[...] = jnp.zeros_like(dgp_ref)

        dgp_ref[0:1, :] += d_alog
        dgp_ref[1:2, :] += d_dt

    return pl.pallas_call(
        body, name=name, grid=(T // tm,),
        in_specs=[pl.BlockSpec((tm, LANES), lambda i: (i, col0)), pl.BlockSpec((tm, LANES), lambda i: (i, col0 + 1)),
                  pl.BlockSpec((8, LANES), lambda i: (0, 0)),
                  pl.BlockSpec((tm, LANES), lambda i: (i, 0)), pl.BlockSpec((tm, LANES), lambda i: (i, 0))],
        out_specs=[pl.BlockSpec((tm, 2 * LANES), lambda i: (i, 0)), pl.BlockSpec((8, LANES), lambda i: (0, 0))],
        out_shape=[jax.ShapeDtypeStruct((T, 2 * LANES), BF16), jax.ShapeDtypeStruct((8, LANES), F32)],
        compiler_params=_params(1),
    )(p, p, gate_params, dbeta, dg)


def _conv_bwd_act(p, conv_w, dqkv, d_model, name):
    T = p.shape[0]
    D = d_model
    H = D // HEAD_DIM
    tm = _pick(T, (256, 128, 64))

    def columns(j, prev_outs):
        def body(cur_ref, prev_ref, w_ref, dout_ref, *rest):
            dc_ref, dw_ref = rest[-2:]
            i = pl.program_id(0)
            prev = prev_ref[...] * (i > 0).astype(F32)
            taps = _conv_taps(cur_ref[...], prev)
            w = w_ref[...]
            c = sum(taps[k] * w[k:k + 1, :] for k in range(4))
            _, vjp = jax.vjp(lambda cc: _qkv_post(cc, j, H), c)
            (dc,) = vjp(dout_ref[...])
            dc_ref[...] = dc

            @pl.when(i == 0)
            def _():
                dw_ref[...] = jnp.zeros_like(dw_ref)

            for k in range(4):
                dw_ref[k:k + 1, :] += jnp.sum(dc * taps[k], axis=0, keepdims=True)

        specs, operands, aliases = _alias_previous(prev_outs, 4)
        return pl.pallas_call(
            body, name="%s%d" % (name, j), grid=(T // tm,),
            in_specs=[pl.BlockSpec((tm, D), lambda i: (i, j)),
                      pl.BlockSpec((8, D), lambda i: (jnp.maximum(i * (tm // 8) - 1, 0), j)),
                      pl.BlockSpec((4, D), lambda i: (0, j)),
                      pl.BlockSpec((tm, D), lambda i: (i, j))] + specs,
            out_specs=[pl.BlockSpec((tm, D), lambda i: (i, j)), pl.BlockSpec((4, D), lambda i: (0, j))],
            out_shape=[jax.ShapeDtypeStruct((T, 3 * D), F32), jax.ShapeDtypeStruct((4, 3 * D), F32)],
            input_output_aliases=aliases, compiler_params=_params(1),
        )(p, p, conv_w, dqkv, *operands)

    return _column_calls(columns)


def _conv_bwd_input(dc, conv_w, name):
    T, D3 = dc.shape
    D = D3 // 3
    tm = _pick(T, (256, 128, 64))
    n_t = T // tm

    def body(cur_ref, next_ref, w_ref, dp_ref):
        i = pl.program_id(0)
        cur = cur_ref[...]
        nxt = next_ref[...] * (i < n_t - 1).astype(F32)
        ext = jnp.concatenate([cur, nxt], axis=0)
        w = w_ref[...]
        acc = cur * w[3:4, :]
        for s in (1, 2, 3):
            acc = acc + pltpu.roll(ext, tm + 8 - s, 0)[0:tm] * w[3 - s:4 - s, :]
        dp_ref[...] = acc.astype(BF16)

    return pl.pallas_call(
        body, name=name, grid=(n_t, 3),
        in_specs=[pl.BlockSpec((tm, D), lambda i, j: (i, j)),
                  pl.BlockSpec((8, D), lambda i, j: (jnp.minimum((i + 1) * (tm // 8), T // 8 - 1), j)),
                  pl.BlockSpec((4, D), lambda i, j: (0, j))],
        out_specs=pl.BlockSpec((tm, D), lambda i, j: (i, j)),
        out_shape=jax.ShapeDtypeStruct((T, D3), BF16), compiler_params=_params(2),
    )(dc, dc, conv_w)


def _comm_call(body, arrays, out_shape, n_remote, n_local, name):
    any_spec = pl.BlockSpec(memory_space=pl.ANY)
    return pl.pallas_call(
        body, name=name, in_specs=[any_spec] * len(arrays), out_specs=[any_spec] * len(out_shape), out_shape=out_shape,
        scratch_shapes=[pltpu.SemaphoreType.DMA((n_remote,)), pltpu.SemaphoreType.DMA((n_remote,)),
                        pltpu.SemaphoreType.DMA((n_local,))],
        compiler_params=pltpu.CompilerParams(has_side_effects=True),
    )(*arrays)


def _gather_shards(arrays, name):
    n = len(arrays)

    def body(*refs):
        ins, outs = refs[:n], refs[n:2 * n]
        send_sems, recv_sems, local_sems = refs[2 * n:]
        x, y, c = lax.axis_index("x"), lax.axis_index("y"), lax.axis_index("c")
        me, sibling = (x, y, c), (x, y, 1 - c)
        chips = [(1 - x, y), (x, 1 - y), (1 - x, 1 - y)]

        def copy(a, k, block, to, src=None):
            dst = outs[a].at[4 * block[0] + 2 * block[1] + block[2]]
            return pltpu.make_async_remote_copy(
                src_ref=dst if src is None else src, dst_ref=dst, send_sem=send_sems.at[a * 7 + k],
                recv_sem=recv_sems.at[a * 7 + k], device_id=to, device_id_type=MESH_ID)

        local = [pltpu.make_async_copy(ins[a], outs[a].at[4 * x + 2 * y + c], local_sems.at[a]) for a in range(n)]
        first = [copy(a, 0, me, sibling, src=ins[a]) for a in range(n)]
        first += [copy(a, 1 + j, me, (*chip, c), src=ins[a]) for j, chip in enumerate(chips) for a in range(n)]
        for cp in local + first:
            cp.start()
        passed = []
        for j, chip in enumerate(chips):
            for a in range(n):
                copy(a, 1 + j, (*chip, c), me).wait_recv()
                passed.append(copy(a, 4 + j, (*chip, c), sibling))
                passed[-1].start()
        for a in range(n):
            copy(a, 0, sibling, me).wait_recv()
            for j, chip in enumerate(chips):
                copy(a, 4 + j, (*chip, 1 - c), me).wait_recv()
        for cp in first + passed:
            cp.wait_send()
        for cp in local:
            cp.wait()

    out_shape = [jax.ShapeDtypeStruct((N_DEV,) + a.shape, a.dtype) for a in arrays]
    return _comm_call(body, arrays, out_shape, 7 * n, n, name)


def _pair_exchange(arrays, name):
    n = len(arrays)

    def body(*refs):
        ins, pair = refs[:n], refs[n:2 * n]
        send_sems, recv_sems, _ = refs[2 * n:]
        x, y, c = lax.axis_index("x"), lax.axis_index("y"), lax.axis_index("c")
        sends = [pltpu.make_async_remote_copy(
            src_ref=ins[a].at[1 - c], dst_ref=pair[a], send_sem=send_sems.at[a], recv_sem=recv_sems.at[a],
            device_id=(x, y, 1 - c), device_id_type=MESH_ID) for a in range(n)]
        for cp in sends:
            cp.start()
        for cp in sends:
            cp.wait_recv()
        for cp in sends:
            cp.wait_send()

    out_shape = [jax.ShapeDtypeStruct(a.shape[1:], a.dtype) for a in arrays]
    return _comm_call(body, arrays, out_shape, n, 1, name)


def _pair_add(own, pair, name, out_dtype):
    _, R, C = own.shape
    tr = next((t for t in (256, 128, 64, 32, 16) if R % t == 0), R)

    def body(a_ref, b_ref, o_ref):
        o_ref[...] = (a_ref[...].astype(F32) + b_ref[...].astype(F32)).astype(out_dtype)

    spec = pl.BlockSpec((1, tr, C), lambda q, i: (q, i, 0))
    return pl.pallas_call(body, name=name, grid=(4, R // tr), in_specs=[spec, spec], out_specs=spec,
                          out_shape=jax.ShapeDtypeStruct(own.shape, out_dtype), compiler_params=_params(2))(own, pair)


def _chip_exchange(arrays, name):
    n = len(arrays)

    def body(*refs):
        ins, outs = refs[:n], refs[n:2 * n]
        send_sems, recv_sems, local_sems = refs[2 * n:]
        x, y, c = lax.axis_index("x"), lax.axis_index("y"), lax.axis_index("c")
        my_chip = 2 * x + y
        chips = [(1 - x, y), (x, 1 - y), (1 - x, 1 - y)]
        local = [pltpu.make_async_copy(ins[a].at[my_chip], outs[a].at[my_chip], local_sems.at[a]) for a in range(n)]
        sends = [pltpu.make_async_remote_copy(
            src_ref=ins[a].at[2 * px + py], dst_ref=outs[a].at[my_chip], send_sem=send_sems.at[a * 3 + j],
            recv_sem=recv_sems.at[a * 3 + j], device_id=(px, py, c), device_id_type=MESH_ID)
            for j, (px, py) in enumerate(chips) for a in range(n)]
        arrivals = [pltpu.make_async_remote_copy(
            src_ref=ins[a].at[my_chip], dst_ref=outs[a].at[2 * px + py], send_sem=send_sems.at[a * 3 + j],
            recv_sem=recv_sems.at[a * 3 + j], device_id=(px, py, c), device_id_type=MESH_ID)
            for j, (px, py) in enumerate(chips) for a in range(n)]
        for cp in local + sends:
            cp.start()
        for cp in arrivals:
            cp.wait_recv()
        for cp in sends:
            cp.wait_send()
        for cp in local:
            cp.wait()

    out_shape = [jax.ShapeDtypeStruct(a.shape, a.dtype) for a in arrays]
    return _comm_call(body, arrays, out_shape, 3 * n, n, name)


def _reduce_adamw(recv, w, m, v, name):
    S, R, C = recv.shape
    tr = next((t for t in (256, 128, 64, 32, 16, 8) if R % t == 0), R)
    c1 = 1.0 - ADAM_B1 ** ADAM_STEP
    c2 = 1.0 - ADAM_B2 ** ADAM_STEP

    def body(r_ref, w_ref, m_ref, v_ref, g_ref, d_ref, nm_ref, nv_ref):
        g = r_ref[0].astype(F32)
        for s in range(1, S):
            g = g + r_ref[s].astype(F32)
        nm = ADAM_B1 * m_ref[...] + (1.0 - ADAM_B1) * g
        nv = ADAM_B2 * v_ref[...] + (1.0 - ADAM_B2) * (g * g)
        g_ref[...] = g
        nm_ref[...] = nm
        nv_ref[...] = nv
        d_ref[...] = -ADAM_LR * ((nm / c1) / (jnp.sqrt(nv / c2) + ADAM_EPS) + ADAM_WD * w_ref[...])

    spec = pl.BlockSpec((tr, C), lambda i: (i, 0))
    return pl.pallas_call(
        body, name=name, grid=(R // tr,),
        in_specs=[pl.BlockSpec((S, tr, C), lambda i: (0, i, 0)), spec, spec, spec], out_specs=[spec] * 4,
        out_shape=[jax.ShapeDtypeStruct((R, C), F32)] * 4, compiler_params=_params(1),
    )(recv, w, m, v)


def _forward_local(x, target, nw0, nw1, fw, wa_in, conv_w, gate_params, o_norm, wa_out, wb_in, wb_out):
    T, D = x.shape
    nD = D // LANES
    sv = {}
    sv["u0"], sv["pa"] = _mm_nn_from(_rmsnorm_head, [x, nw0], [0, None], wa_in, "a_in_proj", [(wa_in.shape[1], F32)])
    sv["qkv_a"] = _conv_fwd(sv["pa"], conv_w, D, "a_conv_fwd")
    sv["beta"], sv["gc"] = _gates_fwd(sv["pa"], gate_params, 4 * nD, "a_gates_fwd")
    sv["o_a"], sv["s_all"], sv["t_all"] = _chunk_fwd(sv["qkv_a"], sv["beta"], sv["gc"], D, "a_chunk_fwd")
    sv["y_a"], sv["h1"] = _mm_nn_from(_onorm_gate_head, [sv["o_a"], sv["pa"], o_norm], [0, 3, None], wa_out,
                                      "a_out_proj", [(D, F32)], add=x)
    sv["u1"], sv["qkv_b"], sv["gate_b"] = _mm_nn_from(_rmsnorm_head, [sv["h1"], nw1], [0, None], wb_in, "b_in_proj",
                                                      [(3 * D, BF16), (D, F32)])
    sv["o_b"], sv["r_b"] = _sb_fwd(sv["qkv_b"], D, "b_attn_fwd")
    sv["y_b"], sv["h2"] = _mm_nn_from(_gate_mul_head, [sv["o_b"], sv["gate_b"]], [0, 0], wb_out, "b_out_proj",
                                      [(D, F32)], add=sv["h1"])
    sv["dh2"], sv["loss"], sv["dfw"] = _final_loss(sv["h2"], fw, target, "final_loss")
    return sv


def _accumulate(acc_ref, value):
    @pl.when(pl.program_id(0) == 0)
    def _():
        acc_ref[...] = jnp.zeros_like(acc_ref)

    acc_ref[...] += jnp.broadcast_to(value, acc_ref.shape)


def _gate_mul_bwd_tail(dy, ins, outs):
    o_ref, g_ref = ins
    do_ref, dg_ref = outs
    g = g_ref[...]
    s = jax.nn.sigmoid(g)
    do_ref[...] = dy * (g * s)
    dg_ref[...] = (dy * o_ref[...] * (s + g * s * (1.0 - s))).astype(BF16)


def _rmsnorm_bwd_tail(du, ins, outs):
    x_ref, w_ref, dres_ref = ins
    dx_ref, dw_ref = outs
    dx, dw = _rms_bwd_math(x_ref[...], w_ref[...], du)
    dx_ref[...] = dres_ref[...] + dx
    _accumulate(dw_ref, dw)


def _onorm_gate_bwd_tail(dy, ins, outs):
    o_ref, z_ref, w_ref = ins
    do_ref, dz_ref, dw_ref = outs
    n_heads = o_ref.shape[1] // HEAD_DIM
    _, vjp = jax.vjp(functools.partial(_onorm_gate_math, n_heads=n_heads), o_ref[...], z_ref[...], w_ref[...])
    do, dz, dw = vjp(dy)
    do_ref[...] = do
    dz_ref[...] = dz.astype(BF16)
    _accumulate(dw_ref, dw)


def _backward_local(sv, x, nw0, nw1, wa_in, conv_w, gate_params, o_norm, wa_out, wb_in, wb_out):
    T, D = x.shape
    nD = D // LANES
    g = {}
    dh2 = sv["dh2"]
    act = lambda dtype: (jax.ShapeDtypeStruct((T, D), dtype), True)
    acc = lambda width: (jax.ShapeDtypeStruct((8, width), F32), False)
    g["wb_out"] = _mm_tn(sv["y_b"], dh2, "b_out_proj_dw")
    do_b, dgate_b = _mm_nt_then([(dh2, wb_out)], "b_out_proj_dx", [sv["o_b"], sv["gate_b"]], [0, 0],
                                _gate_mul_bwd_tail, [act(F32), act(BF16)])
    dq_b, dk_b, dv_b = _sb_bwd(sv["qkv_b"], do_b, sv["r_b"], D, "b_attn_bwd")
    dp_b = [dq_b, dk_b, dv_b, dgate_b]
    g["wb_in"] = jnp.concatenate([_mm_tn(sv["u1"], dp, "b_in_proj_dw%d" % c) for c, dp in enumerate(dp_b)], axis=1)
    dh1, g["nw1"] = _mm_nt_then([(dp, wb_in[:, c * D:(c + 1) * D]) for c, dp in enumerate(dp_b)], "b_in_proj_dx",
                                [sv["h1"], nw1, dh2], [0, None, 0], _rmsnorm_bwd_tail, [act(F32), acc(D)])
    g["wa_out"] = _mm_tn(sv["y_a"], dh1, "a_out_proj_dw")
    do_a, dz_a, g["o_norm"] = _mm_nt_then([(dh1, wa_out)], "a_out_proj_dx", [sv["o_a"], sv["pa"], o_norm], [0, 3, None],
                                          _onorm_gate_bwd_tail, [act(F32), act(BF16), acc(HEAD_DIM)])
    dqkv_a, dbeta, dg = _chunk_bwd(sv["qkv_a"], sv["beta"], sv["gc"], sv["s_all"], sv["t_all"], do_a, D, "a_chunk_bwd")
    dp_gates, g["gate_params"] = _gates_bwd(sv["pa"], gate_params, 4 * nD, dbeta, dg, "a_gates_bwd")
    dc, g["conv_w"] = _conv_bwd_act(sv["pa"], conv_w, dqkv_a, D, "a_conv_bwd_act")
    dp_qkv = _conv_bwd_input(dc, conv_w, "a_conv_bwd_input")
    dp_a = [(dp_qkv, 0, 3 * D), (dz_a, 3 * D, 4 * D), (dp_gates, 4 * D, 4 * D + 2 * LANES)]
    g["wa_in"] = jnp.concatenate([_mm_tn(sv["u0"], dp, "a_in_proj_dw%d" % c) for c, (dp, _, _) in enumerate(dp_a)], axis=1)
    g["x"], g["nw0"] = _mm_nt_then([(dp, wa_in[:, lo:hi]) for dp, lo, hi in dp_a], "a_in_proj_dx",
                                   [x, nw0, dh1], [0, None, 0], _rmsnorm_bwd_tail, [act(F32), acc(D)])
    g["fw"] = sv["dfw"]
    return g


def kernel(x, norm_w, a_w_in, a_conv_w, a_a_log, a_dt_bias, a_o_norm, a_w_out, b_w_in, b_w_out, final_norm_w, loss_target, m_norm_w, m_a_w_in, m_a_conv_w, m_a_a_log, m_a_dt_bias, m_a_o_norm, m_a_w_out, m_b_w_in, m_b_w_out, m_final_norm_w, v_norm_w, v_a_w_in, v_a_conv_w, v_a_a_log, v_a_dt_bias, v_a_o_norm, v_a_w_out, v_b_w_in, v_b_w_out, v_final_norm_w):
    D = x.shape[-1]
    H = D // HEAD_DIM
    shards = [a_w_in[0].astype(BF16), a_w_out[0].astype(BF16), b_w_in[0].astype(BF16), b_w_out[0].astype(BF16), a_conv_w[0]]
    ga_in, ga_out, gb_in, gb_out, g_conv = _gather_shards(shards, "weights_gather")
    wa = ga_in.transpose(1, 0, 2).reshape(D, -1)
    pad = lambda w: jnp.pad(w, ((0, 0), (0, LANES - w.shape[1])))
    wa_in = jnp.concatenate([wa[:, :4 * D], pad(wa[:, 4 * D:4 * D + H]), pad(wa[:, 4 * D + H:])], axis=1)
    wa_out = ga_out.reshape(D, D)
    wb_in = gb_in.transpose(1, 0, 2).reshape(D, 4 * D)
    wb_out = gb_out.reshape(D, D)
    conv_w = g_conv.transpose(1, 0, 2).reshape(4, 3 * D)
    gate_params = jnp.zeros((8, LANES), F32).at[0, :H].set(a_a_log[0]).at[1, :H].set(a_dt_bias[0])
    nw0, nw1, fw = norm_w[0:1], norm_w[1:2], final_norm_w[None]

    sv = _forward_local(x[0], loss_target[0], nw0, nw1, fw, wa_in, conv_w, gate_params, a_o_norm, wa_out, wb_in, wb_out)
    g = _backward_local(sv, x[0], nw0, nw1, wa_in, conv_w, gate_params, a_o_norm, wa_out, wb_in, wb_out)

    gwa = g["wa_in"]
    gwa = jnp.concatenate([gwa[:, :4 * D], gwa[:, 4 * D:4 * D + H], gwa[:, 4 * D + LANES:4 * D + LANES + H]], axis=1)
    row = lambda v: jnp.pad(v.reshape(1, -1), ((0, 0), (0, D - v.size)))
    small = jnp.concatenate([g["nw0"][0:1], g["nw1"][0:1], g["fw"][0:1], row(g["gate_params"][0, :H]),
                             row(g["gate_params"][1, :H]), row(g["o_norm"][0]), row(sv["loss"][0, 0:1]),
                             jnp.zeros((1, D), F32)], axis=0)
    cols = lambda a: a.astype(BF16).reshape(a.shape[0], 4, 2, -1).transpose(2, 1, 0, 3)
    rows = lambda a: a.astype(BF16).reshape(4, 2, -1, a.shape[1]).transpose(1, 0, 2, 3)
    contribs = [cols(gwa), rows(g["wa_out"]), cols(g["wb_in"]), rows(g["wb_out"]), cols(g["conv_w"]),
                jnp.broadcast_to(small[None, None], (2, 4, 8, D))]
    pair = _pair_exchange(contribs, "grads_pair_exchange")
    my_core = lax.axis_index("c")
    own = [lax.dynamic_index_in_dim(a, my_core, axis=0, keepdims=False) for a in contribs]
    partial = [_pair_add(o, p, "grads_pair_add%d" % k, o.dtype) for k, (o, p) in enumerate(zip(own, pair))]
    ra_in, ra_out, rb_in, rb_out, r_conv, r_small = _chip_exchange(partial, "grads_chip_exchange")

    outs = {}
    for nm, recv, w, m, v in (("a_w_in", ra_in, a_w_in, m_a_w_in, v_a_w_in), ("a_w_out", ra_out, a_w_out, m_a_w_out, v_a_w_out),
                              ("b_w_in", rb_in, b_w_in, m_b_w_in, v_b_w_in), ("b_w_out", rb_out, b_w_out, m_b_w_out, v_b_w_out),
                              ("a_conv_w", r_conv, a_conv_w, m_a_conv_w, v_a_conv_w)):
        outs[nm] = tuple(o[None] for o in _reduce_adamw(recv, w[0], m[0], v[0], "adamw_" + nm))

    def pack(nw, alog, dt, onorm, fnw):
        return jnp.concatenate([nw, fnw.reshape(1, D), row(alog), row(dt), row(onorm), jnp.zeros((2, D), F32)], axis=0)

    s_g, s_d, s_m, s_v = _reduce_adamw(
        r_small, pack(norm_w, a_a_log, a_dt_bias, a_o_norm, final_norm_w),
        pack(m_norm_w, m_a_a_log, m_a_dt_bias, m_a_o_norm, m_final_norm_w),
        pack(v_norm_w, v_a_a_log, v_a_dt_bias, v_a_o_norm, v_final_norm_w), "adamw_small")
    loss = s_g[6, 0]
    for i, s in enumerate((s_g, s_d, s_m, s_v)):
        outs.setdefault("norm_w", [None] * 4)[i] = s[0:2]
        outs.setdefault("final_norm_w", [None] * 4)[i] = s[2]
        outs.setdefault("a_a_log", [None] * 4)[i] = s[3:4, :H]
        outs.setdefault("a_dt_bias", [None] * 4)[i] = s[4:5, :H]
        outs.setdefault("a_o_norm", [None] * 4)[i] = s[5:6, :HEAD_DIM]
    names = ("norm_w", "a_w_in", "a_conv_w", "a_a_log", "a_dt_bias", "a_o_norm", "a_w_out", "b_w_in", "b_w_out", "final_norm_w")
    return (loss, g["x"][None]) + tuple(outs[n][i] for i in range(4) for n in names)
```

```python
import functools

import jax
import jax.numpy as jnp
from jax import lax
from jax.experimental import pallas as pl
from jax.experimental.pallas import tpu as pltpu

F32 = jnp.float32
BF16 = jnp.bfloat16
EPS = 1e-6
LOG2_E = 1.4426950408889634
MASKED_SCORE = -1e30
HEAD_DIM = 128
CHUNK = 64
CHUNKS_PER_STEP = 4
ATTN_Q_BLOCKS_FWD = (512, 256)
ATTN_Q_BLOCKS_BWD = (512, 256)
ATTN_K_BLOCK = 128
LANES = 128
N_DEV = 8
VMEM_LIMIT_BYTES = 48 * 1024 * 1024
ADAM_LR, ADAM_B1, ADAM_B2, ADAM_EPS, ADAM_WD, ADAM_STEP = 0.001, 0.9, 0.999, 1e-08, 0.01, 10
MESH_ID = pl.DeviceIdType.MESH


def _pick(n, candidates):
    for c in candidates:
        if n % c == 0:
            return c
    raise ValueError(f"no tile for {n} in {candidates}")


def _params(n_grid_axes):
    return pltpu.CompilerParams(dimension_semantics=("arbitrary",) * n_grid_axes, vmem_limit_bytes=VMEM_LIMIT_BYTES)


def _dot(a, b):
    return jnp.dot(a.astype(BF16), b.astype(BF16), preferred_element_type=F32)


def _dot_nt(a, b):
    return lax.dot_general(a.astype(BF16), b.astype(BF16), (((1,), (1,)), ((), ())), preferred_element_type=F32)


def _dot_tn(a, b):
    return lax.dot_general(a.astype(BF16), b.astype(BF16), (((0,), (0,)), ((), ())), preferred_element_type=F32)


def _split2(x):
    hi = x.astype(BF16)
    lo = (x - hi.astype(F32)).astype(BF16)
    return hi, lo


def _split3(x):
    hi = x.astype(BF16)
    r = x - hi.astype(F32)
    mid = r.astype(BF16)
    lo = (r - mid.astype(F32)).astype(BF16)
    return hi, mid, lo


def _dot3(a, b):
    a_hi, a_lo = _split2(a)
    b_hi, b_lo = _split2(b)
    d = functools.partial(jnp.dot, preferred_element_type=F32)
    return d(a_hi, b_hi) + (d(a_hi, b_lo) + d(a_lo, b_hi))


def _silu(x):
    return x * jax.nn.sigmoid(x)


def _softplus(x):
    return jnp.maximum(x, 0.0) + jnp.log1p(jnp.exp(-jnp.abs(x)))


def _iota2(shape, axis):
    return lax.broadcasted_iota(jnp.int32, shape, axis)


def _rms_bwd_math(x, w, dy):
    r = lax.rsqrt(jnp.mean(x * x, axis=-1, keepdims=True) + EPS)
    xhat = x * r
    dxhat = dy * w
    dx = r * (dxhat - xhat * jnp.mean(dxhat * xhat, axis=-1, keepdims=True))
    dw = jnp.sum(dy * xhat, axis=0, keepdims=True)
    return dx, dw


def _rmsnorm_head(ins):
    x_ref, w_ref = ins
    xf = x_ref[...]
    r = lax.rsqrt(jnp.mean(xf * xf, axis=-1, keepdims=True) + EPS)
    return (xf * r * w_ref[...]).astype(BF16)


def _mm_nn_from(head, extras, extra_cols, b, name, outs, add=None):
    M = extras[0].shape[0]
    K, N = b.shape
    e = len(extras)
    tm = _pick(M, (256, 128))

    def body(*refs):
        ins, b_ref = refs[:e], refs[e]
        add_ref = refs[e + 1] if add is not None else None
        a_ref, out_refs = refs[e + 1 + (add is not None)], refs[e + 2 + (add is not None):]
        a = head(ins)
        a_ref[...] = a
        acc = jnp.dot(a, b_ref[...], preferred_element_type=F32)
        if add is not None:
            acc = acc + add_ref[...]
        col = 0
        for o_ref, (width, dtype) in zip(out_refs, outs):
            o_ref[...] = acc[:, col:col + width].astype(dtype)
            col += width

    def extra_spec(x, col):
        if col is None:
            return pl.BlockSpec(x.shape, lambda i: (0,) * x.ndim)
        return pl.BlockSpec((tm, K), lambda i: (i, col))

    in_specs = [extra_spec(x, col) for x, col in zip(extras, extra_cols)] + [pl.BlockSpec((K, N), lambda i: (0, 0))]
    args = list(extras) + [b]
    if add is not None:
        in_specs.append(pl.BlockSpec((tm, N), lambda i: (i, 0)))
        args.append(add)
    return pl.pallas_call(
        body, name=name, grid=(M // tm,), in_specs=in_specs,
        out_specs=[pl.BlockSpec((tm, K), lambda i: (i, 0))] + [pl.BlockSpec((tm, w), lambda i: (i, 0)) for w, _ in outs],
        out_shape=[jax.ShapeDtypeStruct((M, K), BF16)] + [jax.ShapeDtypeStruct((M, w), dt) for w, dt in outs],
        compiler_params=_params(1),
    )(*args)


def _mm_nt(pairs, name):
    def write(acc, ins, outs):
        outs[0][...] = acc

    return _mm_nt_then(pairs, name, [], [], write,
                       [(jax.ShapeDtypeStruct((pairs[0][0].shape[0], pairs[0][1].shape[0]), F32), True)])


def _mm_nt_then(pairs, name, extras, extra_cols, tail, outs):
    M = pairs[0][0].shape[0]
    n, e = len(pairs), len(extras)
    tm = _pick(M, (256, 128)) if e else _pick(M, (512, 256, 128))

    def body(*refs):
        acc = _dot_nt(refs[0][...], refs[n][...])
        for p in range(1, n):
            acc = acc + _dot_nt(refs[p][...], refs[n + p][...])
        tail(acc, refs[2 * n:2 * n + e], refs[2 * n + e:])

    def extra_spec(x, col):
        if col is None:
            return pl.BlockSpec(x.shape, lambda i: (0,) * x.ndim)
        return pl.BlockSpec((tm, outs[0][0].shape[1]), lambda i: (i, col))

    in_specs = ([pl.BlockSpec((tm, a.shape[1]), lambda i: (i, 0)) for a, _ in pairs]
                + [pl.BlockSpec(b.shape, lambda i: (0, 0)) for _, b in pairs]
                + [extra_spec(x, col) for x, col in zip(extras, extra_cols)])
    out_specs = [pl.BlockSpec((tm, s.shape[1]), lambda i: (i, 0)) if tiled else pl.BlockSpec(s.shape, lambda i: (0, 0))
                 for s, tiled in outs]
    res = pl.pallas_call(
        body, name=name, grid=(M // tm,), in_specs=in_specs, out_specs=out_specs,
        out_shape=[s for s, _ in outs], compiler_params=_params(1),
    )(*[a for a, _ in pairs], *[b for _, b in pairs], *extras)
    return res[0] if len(outs) == 1 else res


def _mm_tn(a, b, name):
    R, M = a.shape
    _, N = b.shape
    tn = _pick(N, (1536, 1024, 512, 256, 128))
    tr = _pick(R, (1024, 512, 256, 128))

    def body(a_ref, b_ref, o_ref):
        @pl.when(pl.program_id(1) == 0)
        def _():
            o_ref[...] = jnp.zeros_like(o_ref)

        o_ref[...] += _dot_tn(a_ref[...], b_ref[...])

    return pl.pallas_call(
        body, name=name, grid=(N // tn, R // tr),
        in_specs=[pl.BlockSpec((tr, M), lambda j, r: (r, 0)), pl.BlockSpec((tr, tn), lambda j, r: (r, j))],
        out_specs=pl.BlockSpec((M, tn), lambda j, r: (0, j)),
        out_shape=jax.ShapeDtypeStruct((M, N), F32), compiler_params=_params(2),
    )(a, b)


def _qkv_post(c, j, n_heads):
    s = _silu(c)
    if j == 2:
        return s
    parts = []
    for h in range(n_heads):
        sh = s[:, h * HEAD_DIM:(h + 1) * HEAD_DIM]
        parts.append(sh * lax.rsqrt(jnp.sum(sh * sh, axis=-1, keepdims=True) + EPS))
    n = jnp.concatenate(parts, axis=-1)
    return n * (HEAD_DIM ** -0.5) if j == 0 else n


def _column_calls(make_call, n_cols=3):
    outs = None
    for j in range(n_cols):
        outs = make_call(j, outs)
    return outs


def _alias_previous(prev, n_inputs):
    if prev is None:
        return [], [], {}
    prev = list(prev) if isinstance(prev, (list, tuple)) else [prev]
    return ([pl.BlockSpec(memory_space=pl.ANY)] * len(prev), prev, {n_inputs + k: k for k in range(len(prev))})


def _conv_taps(cur, halo_prev):
    tm = cur.shape[0]
    ext = jnp.concatenate([halo_prev, cur], axis=0)
    taps = [pltpu.roll(ext, s, 0)[8:8 + tm] for s in (3, 2, 1)]
    return taps + [cur]


def _conv_fwd(p, conv_w, d_model, name):
    T = p.shape[0]
    D = d_model
    H = D // HEAD_DIM
    tm = _pick(T, (256, 128, 64))

    def columns(j, prev_out):
        def body(cur_ref, prev_ref, w_ref, *rest):
            o_ref = rest[-1]
            first = (pl.program_id(0) > 0).astype(F32)
            for h in range(H):
                hs = slice(h * HEAD_DIM, (h + 1) * HEAD_DIM)
                taps = _conv_taps(cur_ref[:, hs], prev_ref[:, hs] * first)
                w = w_ref[:, hs]
                c = sum(taps[k] * w[k:k + 1, :] for k in range(4))
                o_ref[:, hs] = _qkv_post(c, j, 1)

        specs, operands, aliases = _alias_previous(prev_out, 3)
        return pl.pallas_call(
            body, name="%s%d" % (name, j), grid=(T // tm,),
            in_specs=[pl.BlockSpec((tm, D), lambda i: (i, j)),
                      pl.BlockSpec((8, D), lambda i: (jnp.maximum(i * (tm // 8) - 1, 0), j)),
                      pl.BlockSpec((4, D), lambda i: (0, j))] + specs,
            out_specs=pl.BlockSpec((tm, D), lambda i: (i, j)),
            out_shape=jax.ShapeDtypeStruct((T, 3 * D), F32), input_output_aliases=aliases, compiler_params=_params(1),
        )(p, p, conv_w, *operands)

    return _column_calls(columns)


def _chunk_tri(tm, upper):
    r, c = _iota2((tm, tm), 0), _iota2((tm, tm), 1)
    same = (r // CHUNK) == (c // CHUNK)
    tri = (c >= r) if upper else (c <= r)
    return jnp.where(same & tri, 1.0, 0.0).astype(BF16)


def _dot_mask(mask_bf16, x):
    hi, mid, lo = _split3(x)
    d = functools.partial(jnp.dot, preferred_element_type=F32)
    return d(mask_bf16, hi) + (d(mask_bf16, mid) + d(mask_bf16, lo))


def _gates_math(pb, pa, a_log, dt_bias):
    beta = jax.nn.sigmoid(pb)
    g = -jnp.exp(a_log) * _softplus(pa + dt_bias)
    return beta, g


def _gates_fwd(p, gate_params, col0, name):
    T = p.shape[0]
    tm = _pick(T, (256, 128, 64))

    def body(pb_ref, pa_ref, gp_ref, beta_ref, gc_ref):
        gp = gp_ref[...]
        beta, g = _gates_math(pb_ref[...], pa_ref[...], gp[0:1, :], gp[1:2, :])
        beta_ref[...] = beta
        gc_ref[...] = _dot_mask(_chunk_tri(tm, upper=False), g)

    return pl.pallas_call(
        body, name=name, grid=(T // tm,),
        in_specs=[pl.BlockSpec((tm, LANES), lambda i: (i, col0)), pl.BlockSpec((tm, LANES), lambda i: (i, col0 + 1)),
                  pl.BlockSpec((8, LANES), lambda i: (0, 0))],
        out_specs=[pl.BlockSpec((tm, LANES), lambda i: (i, 0))] * 2,
        out_shape=[jax.ShapeDtypeStruct((T, LANES), F32)] * 2, compiler_params=_params(1),
    )(p, p, gate_params)


def _col_to_row(col):
    C = col.shape[0]
    eye = _iota2((C, C), 0) == _iota2((C, C), 1)
    return jnp.sum(jnp.where(eye, col, 0.0), axis=0, keepdims=True)


def _lockstep(groups, skew, finish):
    groups = [list(g) for g in groups]
    results = [[None] * len(g) for g in groups]
    left = [len(g) for g in groups]
    rnd = 0
    while any(left):
        for gi, gens in enumerate(groups):
            if rnd < gi * skew or not left[gi]:
                continue
            for idx, gen in enumerate(gens):
                if gen is None:
                    continue
                try:
                    next(gen)
                except StopIteration as done:
                    results[gi][idx] = done.value
                    gens[idx] = None
                    left[gi] -= 1
            if not left[gi]:
                finish(gi, results[gi])
        rnd += 1


def _unit_lower_inverse(low):
    C = low.shape[0]
    eye = (_iota2((C, C), 0) == _iota2((C, C), 1)).astype(F32)
    t = eye - low
    p = _dot3(low, low)
    yield
    n = 2
    while True:
        tp = _dot3(t, p)
        n *= 2
        if n < C:
            p = _dot3(p, p)
        yield
        t = t + tp
        if n >= C:
            return t


FWD_STATE_SKEW = 2


def _chunk_head_fwd(load, take_state, give):
    q, k, v, gc, beta = load()
    C = q.shape[0]
    r, c = _iota2((C, C), 0), _iota2((C, C), 1)
    causal, strict = r >= c, r > c
    decay = jnp.where(causal, jnp.exp(jnp.where(causal, gc - _col_to_row(gc), 0.0)), 0.0)
    kb, vb = k * beta, v * beta
    eg = jnp.exp(gc)
    kk = _dot_nt(kb, k)
    qk = _dot_nt(q, k)
    yield
    t_inv = yield from _unit_lower_inverse(jnp.where(strict, kk * decay, 0.0))
    give(t_inv=t_inv)
    u = _dot(t_inv, vb)
    w = _dot(t_inv, kb * eg)
    yield
    s_in = take_state()
    give(s_in=s_in)
    o_state = _dot(q * eg, s_in)
    w_state = _dot(w, s_in)
    yield
    v_new = u - w_state
    g_last = gc[C - 1:C, :]
    o_intra = _dot(qk * decay, v_new)
    s_add = _dot_tn(k * jnp.exp(g_last - gc), v_new)
    yield
    give(o=o_state + o_intra, s_out=s_in * jnp.exp(g_last) + s_add)


def _chunk_fwd(qkv, beta, gc, d_model, name):
    T = qkv.shape[0]
    D = d_model
    H = D // HEAD_DIM
    N = T // CHUNK
    G = CHUNKS_PER_STEP
    R = G * CHUNK

    def body(q_ref, k_ref, v_ref, beta_ref, gc_ref, o_ref, s_all_ref, t_all_ref, s_ref):
        @pl.when(pl.program_id(0) == 0)
        def _():
            s_ref[...] = jnp.zeros_like(s_ref)

        heads = [slice(h * HEAD_DIM, (h + 1) * HEAD_DIM) for h in range(H)]
        states = [[s_ref[h] for h in range(H)]] + [[None] * H for _ in range(G)]

        def head(sub, h):
            rows = slice(sub * CHUNK, (sub + 1) * CHUNK)

            def give(t_inv=None, s_in=None, o=None, s_out=None):
                if t_inv is not None:
                    t_all_ref[sub, h] = t_inv
                if s_in is not None:
                    s_all_ref[sub, h] = s_in
                if o is not None:
                    o_ref[rows, heads[h]] = o
                    states[sub + 1][h] = s_out

            load = lambda: (q_ref[rows, heads[h]], k_ref[rows, heads[h]], v_ref[rows, heads[h]],
                            gc_ref[rows, h:h + 1], beta_ref[rows, h:h + 1])
            return _chunk_head_fwd(load, lambda: states[sub][h], give)

        _lockstep([[head(sub, h) for h in range(H)] for sub in range(G)], FWD_STATE_SKEW, lambda sub, results: None)
        for h in range(H):
            s_ref[h] = states[G][h]

    return pl.pallas_call(
        body, name=name, grid=(N // G,),
        in_specs=[pl.BlockSpec((R, D), lambda n: (n, 0)), pl.BlockSpec((R, D), lambda n: (n, 1)),
                  pl.BlockSpec((R, D), lambda n: (n, 2)),
                  pl.BlockSpec((R, LANES), lambda n: (n, 0)), pl.BlockSpec((R, LANES), lambda n: (n, 0))],
        out_specs=[pl.BlockSpec((R, D), lambda n: (n, 0)),
                   pl.BlockSpec((G, H, HEAD_DIM, HEAD_DIM), lambda n: (n, 0, 0, 0)),
                   pl.BlockSpec((G, H, CHUNK, CHUNK), lambda n: (n, 0, 0, 0))],
        out_shape=[jax.ShapeDtypeStruct((T, D), F32), jax.ShapeDtypeStruct((N, H, HEAD_DIM, HEAD_DIM), F32),
                   jax.ShapeDtypeStruct((N, H, CHUNK, CHUNK), F32)],
        scratch_shapes=[pltpu.VMEM((H, HEAD_DIM, HEAD_DIM), F32)], compiler_params=_params(1),
    )(qkv, qkv, qkv, beta, gc)


def _onorm_gate_math(o, z, w, n_heads):
    parts = []
    for h in range(n_heads):
        hs = slice(h * HEAD_DIM, (h + 1) * HEAD_DIM)
        oh = o[:, hs]
        y = oh * lax.rsqrt(jnp.mean(oh * oh, axis=-1, keepdims=True) + EPS) * w
        parts.append(y * _silu(z[:, hs]))
    return jnp.concatenate(parts, axis=-1)


def _onorm_gate_head(ins):
    o_ref, z_ref, w_ref = ins
    return _onorm_gate_math(o_ref[...], z_ref[...], w_ref[...], o_ref.shape[1] // HEAD_DIM).astype(BF16)


def _diag_mask(qb, kb, d):
    return _iota2((qb, kb), 0) > _iota2((qb, kb), 1) + d * kb


def _run_trips(trip, n, state):
    def six(j, st):
        for u in range(6):
            st = trip(6 * j + u, st, u)
        return st

    state = lax.fori_loop(0, n // 6, six, state)
    base = (n // 6) * 6
    for u in (0, 2):
        pair = lambda st, u=u: trip(base + u + 1, trip(base + u, st, u), u + 1)
        state = lax.cond(n - base > u, pair, lambda st: st, state)
    return state


def _fill_score_masks(mask_buf, qb, kb, ns):
    mask_buf[0] = jnp.zeros(mask_buf.shape[1:], F32)
    for d in range(ns):
        for half in range(2):
            mask_buf[d + 1, :, half * kb:(half + 1) * kb] = jnp.where(_diag_mask(qb, kb, 2 * d + half), 0.0, MASKED_SCORE)


def _softplus_bits(w):
    u = 1.0 + jnp.exp2(jnp.minimum(w, 64.0))
    return jnp.maximum(w, jnp.log2(u)), 1.0 / u


def _incl_lower(n):
    return jnp.where((_iota2((2 * n, n), 0) & (n - 1)) >= _iota2((2 * n, n), 1), 1.0, 0.0).astype(BF16)


def _dot_cum(x, tri_bf16):
    hi, lo = _split2(x)
    return jnp.dot(jnp.concatenate([hi, lo], axis=1), tri_bf16, preferred_element_type=F32)


def _sb_fwd(qkv, d_model, name):
    T = qkv.shape[0]
    D = d_model
    H = D // HEAD_DIM
    QB = _pick(T, ATTN_Q_BLOCKS_FWD)
    KB = ATTN_K_BLOCK
    KS = 2 * KB
    ns = QB // KS
    nq = T // QB
    scale = HEAD_DIM ** -0.5

    def body(q_ref, k_ref, v_ref, o_ref, r_ref, w0, w1, w2, cum0, cum1, mask_buf):
        w_bufs, cum_bufs = (w0, w1, w2), (cum0, cum1)
        i = pl.program_id(1)

        @pl.when(i == 0)
        def _():
            _fill_score_masks(mask_buf, QB, KB, ns)

        q = q_ref[...]
        tri = _incl_lower(KB)
        n_tot = (i + 1) * ns

        def key_step(m):
            return jnp.maximum(n_tot - 1 - m, 0)

        def rows(ref, s):
            return ref[pl.ds(pl.multiple_of(s * KS, KS), KS), :]

        def scores(s, may_be_diagonal):
            w = _dot_nt(q, rows(k_ref, s)) * (scale * LOG2_E)
            return w + mask_buf[jnp.maximum(s - i * ns + 1, 0)] if may_be_diagonal else w

        def cums(w):
            sp = _softplus_bits(w)[0]
            return jnp.concatenate([_dot_cum(sp[:, :KB], tri), _dot_cum(sp[:, KB:], tri)], axis=1)

        def weights(w, cum, carry):
            a_r = jnp.exp2(w[:, KB:] - cum[:, KB:] - carry)
            carry = carry + cum[:, KB:KB + 1]
            a_l = jnp.exp2(w[:, :KB] - cum[:, :KB] - carry)
            return jnp.concatenate([a_l, a_r], axis=1).astype(BF16), carry + cum[:, 0:1]

        def trip(m, carry, ph):
            w_bufs[(ph + 2) % 3][...] = scores(key_step(m + 2), False)
            a, carry = weights(w_bufs[ph % 3][...], cum_bufs[ph % 2][...], carry)
            o_ref[...] += _dot(a, rows(v_ref, key_step(m)))
            cum_bufs[(ph + 1) % 2][...] = cums(w_bufs[(ph + 1) % 3][...])
            return carry

        o_ref[...] = jnp.zeros_like(o_ref)
        assert ns == 2
        w_bufs[0][...] = scores(key_step(0), True)
        w_bufs[1][...] = scores(key_step(1), True)
        cum_bufs[0][...] = cums(w_bufs[0][...])
        carry = _run_trips(trip, n_tot, jnp.zeros((QB, 1), F32))
        r_ref[0] = jnp.broadcast_to(carry, (QB, LANES))

    return pl.pallas_call(
        body, name=name, grid=(H, nq),
        in_specs=[pl.BlockSpec((QB, HEAD_DIM), lambda h, i: (i, h)),
                  pl.BlockSpec((T, HEAD_DIM), lambda h, i: (0, H + h)),
                  pl.BlockSpec((T, HEAD_DIM), lambda h, i: (0, 2 * H + h))],
        out_specs=[pl.BlockSpec((QB, HEAD_DIM), lambda h, i: (i, h)),
                   pl.BlockSpec((1, QB, LANES), lambda h, i: (h, i, 0))],
        out_shape=[jax.ShapeDtypeStruct((T, D), F32), jax.ShapeDtypeStruct((H, T, LANES), F32)],
        scratch_shapes=[pltpu.VMEM((QB, KS), F32)] * 5 + [pltpu.VMEM((ns + 1, QB, KS), F32)],
        compiler_params=_params(2),
    )(qkv, qkv, qkv)


def _gate_mul_head(ins):
    o_ref, g_ref = ins
    return (o_ref[...] * _silu(g_ref[...])).astype(BF16)


def _final_loss(h, w, target, name):
    T, D = h.shape
    tm = _pick(T, (512, 256, 128))

    def body(h_ref, w_ref, t_ref, dh_ref, loss_ref, dw_ref):
        x, w = h_ref[...], w_ref[...]
        r = lax.rsqrt(jnp.mean(x * x, axis=-1, keepdims=True) + EPS)
        err = x * r * w - t_ref[...]
        part = 0.5 * jnp.sum(jnp.mean(err * err, axis=-1, keepdims=True), axis=0, keepdims=True)
        dx, dw = _rms_bwd_math(x, w, err * (1.0 / D))
        dh_ref[...] = dx

        @pl.when(pl.program_id(0) == 0)
        def _():
            loss_ref[...] = jnp.zeros_like(loss_ref)
            dw_ref[...] = jnp.zeros_like(dw_ref)

        loss_ref[...] += jnp.broadcast_to(part, loss_ref.shape)
        dw_ref[...] += jnp.broadcast_to(dw, dw_ref.shape)

    return pl.pallas_call(
        body, name=name, grid=(T // tm,),
        in_specs=[pl.BlockSpec((tm, D), lambda i: (i, 0)), pl.BlockSpec((1, D), lambda i: (0, 0)),
                  pl.BlockSpec((tm, D), lambda i: (i, 0))],
        out_specs=[pl.BlockSpec((tm, D), lambda i: (i, 0)), pl.BlockSpec((8, LANES), lambda i: (0, 0)),
                   pl.BlockSpec((8, D), lambda i: (0, 0))],
        out_shape=[jax.ShapeDtypeStruct((T, D), F32), jax.ShapeDtypeStruct((8, LANES), F32),
                   jax.ShapeDtypeStruct((8, D), F32)],
        compiler_params=_params(1),
    )(h, w, target)


def _sb_bwd(qkv, do, r_tot, d_model, name):
    T = qkv.shape[0]
    D = d_model
    H = D // HEAD_DIM
    QB = _pick(T, ATTN_Q_BLOCKS_BWD)
    KB = ATTN_K_BLOCK
    KS = 2 * KB
    ns = QB // KS
    nq = T // QB
    n_key_steps = T // KS
    scale = HEAD_DIM ** -0.5

    def body(q_ref, k_ref, v_ref, do_ref, r_ref, dq_ref, dk_ref, dv_ref,
             dkt_acc, dvt_acc, dq_acc, w0, w1, w2, da0, da1, da2, cum0, cum1, sig0, sig1, mask_buf):
        w_bufs, da_bufs, cum_bufs, sig_bufs = (w0, w1, w2), (da0, da1, da2), (cum0, cum1), (sig0, sig1)
        i = pl.program_id(1)

        @pl.when(i == 0)
        def _():
            dkt_acc[...] = jnp.zeros_like(dkt_acc)
            dvt_acc[...] = jnp.zeros_like(dvt_acc)
            _fill_score_masks(mask_buf, QB, KB, ns)

        q = q_ref[...]
        do_blk = do_ref[...].astype(BF16)
        q_t = q.astype(F32).T.astype(BF16)
        do_t = do_ref[...].T.astype(BF16)
        row_total = r_ref[0][:, 0:1]
        tri_rev = _incl_lower(KB)
        tri_fwd = jnp.where(_iota2((KB, KB), 0) <= _iota2((KB, KB), 1), 1.0, 0.0).astype(BF16)
        n_tot = (i + 1) * ns

        def step_rows(ref, s):
            return ref[pl.ds(pl.multiple_of(s * KS, KS), KS), :]

        def scores(s):
            w = _dot_nt(q, step_rows(k_ref, s)) * (scale * LOG2_E) + mask_buf[jnp.maximum(s - i * ns + 1, 0)]
            return w, _dot_nt(do_blk, step_rows(v_ref, s))

        def softplus_sums(w):
            sp, one_minus_sig = _softplus_bits(w)
            cum = jnp.concatenate([_dot_cum(sp[:, :KB], tri_rev), _dot_cum(sp[:, KB:], tri_rev)], axis=1)
            return cum, 1.0 - one_minus_sig

        def weights(w, cum, da, left_sp):
            right_l = row_total - left_sp - cum[:, 0:1]
            right_r = right_l - cum[:, KB:KB + 1]
            a = jnp.concatenate([jnp.exp2(w[:, :KB] - cum[:, :KB] - right_l),
                                 jnp.exp2(w[:, KB:] - cum[:, KB:] - right_r)], axis=1)
            p = da * a
            cp = jnp.concatenate([_dot(p[:, :KB], tri_fwd), _dot(p[:, KB:], tri_fwd)], axis=1)
            return a.astype(BF16), p, cp, row_total - right_r

        def score_grads(p, cp, sig, left_p):
            cum_l = cp[:, :KB] + left_p
            cum_r = cp[:, KB:] + cum_l[:, KB - 1:KB]
            dz = p - sig * jnp.concatenate([cum_l, cum_r], axis=1)
            return dz.astype(BF16), cum_r[:, KB - 1:KB]

        def trip(m, st, ph):
            left_sp, left_p = st
            w_bufs[(ph + 2) % 3][...], da_bufs[(ph + 2) % 3][...] = scores(jnp.minimum(m + 2, n_tot - 1))
            a, p, cp, left_sp = weights(w_bufs[ph % 3][...], cum_bufs[ph % 2][...], da_bufs[ph % 3][...], left_sp)
            cum_bufs[(ph + 1) % 2][...], sig_bufs[(ph + 1) % 2][...] = softplus_sums(w_bufs[(ph + 1) % 3][...])
            dz, left_p = score_grads(p, cp, sig_bufs[ph % 2][...], left_p)
            dq_acc[...] += _dot(dz, step_rows(k_ref, m))
            dkt_acc[m] += jnp.dot(q_t, dz, preferred_element_type=F32) * scale
            dvt_acc[m] += jnp.dot(do_t, a, preferred_element_type=F32)
            return left_sp, left_p

        dq_acc[...] = jnp.zeros_like(dq_acc)
        w_bufs[0][...], da_bufs[0][...] = scores(0)
        w_bufs[1][...], da_bufs[1][...] = scores(jnp.minimum(1, n_tot - 1))
        cum_bufs[0][...], sig_bufs[0][...] = softplus_sums(w_bufs[0][...])
        zero_col = jnp.zeros((QB, 1), F32)
        _run_trips(trip, n_tot, (zero_col, zero_col))
        dq_ref[...] = (dq_acc[...] * scale).astype(BF16)

        @pl.when(i == nq - 1)
        def _():
            for s in range(n_key_steps):
                dk_ref[s * KS:(s + 1) * KS, :] = dkt_acc[s].T.astype(BF16)
                dv_ref[s * KS:(s + 1) * KS, :] = dvt_acc[s].T.astype(BF16)

    return pl.pallas_call(
        body, name=name, grid=(H, nq),
        in_specs=[pl.BlockSpec((QB, HEAD_DIM), lambda h, i: (i, h)),
                  pl.BlockSpec((T, HEAD_DIM), lambda h, i: (0, H + h)),
                  pl.BlockSpec((T, HEAD_DIM), lambda h, i: (0, 2 * H + h)),
                  pl.BlockSpec((QB, HEAD_DIM), lambda h, i: (i, h)),
                  pl.BlockSpec((1, QB, LANES), lambda h, i: (h, i, 0))],
        out_specs=[pl.BlockSpec((QB, HEAD_DIM), lambda h, i: (i, h)),
                   pl.BlockSpec((T, HEAD_DIM), lambda h, i: (0, h)),
                   pl.BlockSpec((T, HEAD_DIM), lambda h, i: (0, h))],
        out_shape=[jax.ShapeDtypeStruct((T, D), BF16)] * 3,
        scratch_shapes=[pltpu.VMEM((n_key_steps, HEAD_DIM, KS), F32), pltpu.VMEM((n_key_steps, HEAD_DIM, KS), F32),
                        pltpu.VMEM((QB, HEAD_DIM), F32)] + [pltpu.VMEM((QB, KS), F32)] * 10
                       + [pltpu.VMEM((ns + 1, QB, KS), F32)],
        compiler_params=_params(2),
    )(qkv, qkv, qkv, do, r_tot)


def _row_to_col(row):
    C = row.shape[1]
    eye = _iota2((C, C), 0) == _iota2((C, C), 1)
    return jnp.sum(jnp.where(eye, row, 0.0), axis=1, keepdims=True)


def _lane_sum(x):
    return jnp.sum(x, axis=-1, keepdims=True)


BWD_STATE_SKEW = 2


def _chunk_head_bwd(load, take_dstate, give):
    q, k, v, gc, beta, s_in, t_inv, do = load()
    C = q.shape[0]
    r, c = _iota2((C, C), 0), _iota2((C, C), 1)
    causal, strict = r >= c, r > c
    decay = jnp.where(causal, jnp.exp(jnp.where(causal, gc - _col_to_row(gc), 0.0)), 0.0)
    kb, vb = k * beta, v * beta
    eg = jnp.exp(gc)
    kbg = kb * eg
    g_last = gc[C - 1:C, :]
    e_tail = jnp.exp(g_last - gc)
    k_tail = k * e_tail
    gl = jnp.exp(g_last)
    qg = q * eg
    t_inv_t = t_inv.T
    kk = _dot_nt(kb, k)
    u = _dot(t_inv, vb)
    w = _dot(t_inv, kbg)
    qk = _dot_nt(q, k)
    d_qg = _dot_nt(do, s_in)
    ds_state = _dot_tn(qg, do)
    yield
    ds_out = take_dstate()
    low = jnp.where(strict, kk * decay, 0.0)
    attn = qk * decay
    w_state = _dot(w, s_in)
    d_vnew_intra = _dot_tn(attn, do)
    d_vnew_state = _dot(k_tail, ds_out)
    yield
    v_new = u - w_state
    d_vnew = d_vnew_intra + d_vnew_state
    d_ktail = _dot_nt(v_new, ds_out)
    d_attn_raw = _dot_nt(do, v_new)
    d_w = -_dot_nt(d_vnew, s_in)
    ds_w = _dot_tn(w, d_vnew)
    d_vb = _dot(t_inv_t, d_vnew)
    d_tinv_u = _dot_nt(d_vnew, vb)
    yield
    d_gl = _lane_sum(jnp.sum(s_in * ds_out, axis=0, keepdims=True))
    d_attn = jnp.where(causal, d_attn_raw, 0.0)
    give(ds_in=ds_out * gl + ds_state - ds_w)
    d_kbg = _dot(t_inv_t, d_w)
    d_tinv_w = _dot_nt(d_w, kbg)
    d_qk = d_attn * decay
    dq_intra = _dot(d_qk, k)
    dk_intra = _dot_tn(d_qk, q)
    yield
    inner = _dot(t_inv_t, d_tinv_u + d_tinv_w)
    yield
    d_low_raw = _dot_nt(inner, t_inv)
    yield
    d_low = jnp.where(strict, -d_low_raw, 0.0)
    d_kk = d_low * decay
    d_kb_low = _dot(d_kk, k)
    dk_low = _dot_tn(d_kk, kb)
    yield
    d_kb = d_kb_low + d_kbg * eg
    give(dq=dq_intra + d_qg * eg, dk=dk_low + dk_intra + d_ktail * e_tail + d_kb * beta, dv=d_vb * beta)
    dbeta = _lane_sum(d_kb * k + d_vb * v)
    m = d_low * low + d_attn * attn
    tail = d_ktail * k_tail
    d_g_last = d_gl * gl + _lane_sum(jnp.sum(tail, axis=0, keepdims=True))
    dgc = (_lane_sum(m) - _row_to_col(jnp.sum(m, axis=0, keepdims=True))
           + _lane_sum(d_qg * qg + d_kbg * kbg - tail))
    return dgc + jnp.where(_iota2((C, 1), 0) == C - 1, d_g_last, 0.0), dbeta


def _chunk_bwd(qkv, beta, gc, s_all, t_all, do, d_model, name):
    T = qkv.shape[0]
    D = d_model
    H = D // HEAD_DIM
    N = T // CHUNK
    G = CHUNKS_PER_STEP
    R = G * CHUNK
    n_steps = N // G

    def body(q_ref, k_ref, v_ref, beta_ref, gc_ref, s_ref, t_ref, do_ref, dqkv_ref, dbeta_ref, dg_ref, ds_ref):
        @pl.when(pl.program_id(0) == 0)
        def _():
            ds_ref[...] = jnp.zeros_like(ds_ref)

        lane = _iota2((CHUNK, LANES), 1)
        heads = [slice(h * HEAD_DIM, (h + 1) * HEAD_DIM) for h in range(H)]
        d_states = [[ds_ref[h] for h in range(H)]] + [[None] * H for _ in range(G)]

        def head(g, h):
            sub = G - 1 - g
            rows = slice(sub * CHUNK, (sub + 1) * CHUNK)

            def give(ds_in=None, dq=None, dk=None, dv=None):
                if ds_in is not None:
                    d_states[g + 1][h] = ds_in
                if dq is not None:
                    dqkv_ref[rows, h * HEAD_DIM:(h + 1) * HEAD_DIM] = dq
                    dqkv_ref[rows, D + h * HEAD_DIM:D + (h + 1) * HEAD_DIM] = dk
                    dqkv_ref[rows, 2 * D + h * HEAD_DIM:2 * D + (h + 1) * HEAD_DIM] = dv

            load = lambda: (q_ref[rows, heads[h]], k_ref[rows, heads[h]], v_ref[rows, heads[h]], gc_ref[rows, h:h + 1],
                            beta_ref[rows, h:h + 1], s_ref[sub, h], t_ref[sub, h], do_ref[rows, heads[h]])
            return _chunk_head_bwd(load, lambda: d_states[g][h], give)

        def finish(g, results):
            rows = slice((G - 1 - g) * CHUNK, (G - g) * CHUNK)
            dgc_all = jnp.zeros((CHUNK, LANES), F32)
            dbeta_all = jnp.zeros((CHUNK, LANES), F32)
            for h, (dgc, dbeta) in enumerate(results):
                dgc_all = jnp.where(lane == h, dgc, dgc_all)
                dbeta_all = jnp.where(lane == h, dbeta, dbeta_all)
            dbeta_ref[rows, :] = dbeta_all
            dg_ref[rows, :] = _dot_mask(_chunk_tri(CHUNK, upper=True), dgc_all)

        _lockstep([[head(g, h) for h in range(H)] for g in range(G)], BWD_STATE_SKEW, finish)
        for h in range(H):
            ds_ref[h] = d_states[G][h]

    rev = lambda n: n_steps - 1 - n
    return pl.pallas_call(
        body, name=name, grid=(n_steps,),
        in_specs=[pl.BlockSpec((R, D), lambda n: (rev(n), 0)), pl.BlockSpec((R, D), lambda n: (rev(n), 1)),
                  pl.BlockSpec((R, D), lambda n: (rev(n), 2)),
                  pl.BlockSpec((R, LANES), lambda n: (rev(n), 0)), pl.BlockSpec((R, LANES), lambda n: (rev(n), 0)),
                  pl.BlockSpec((G, H, HEAD_DIM, HEAD_DIM), lambda n: (rev(n), 0, 0, 0)),
                  pl.BlockSpec((G, H, CHUNK, CHUNK), lambda n: (rev(n), 0, 0, 0)),
                  pl.BlockSpec((R, D), lambda n: (rev(n), 0))],
        out_specs=[pl.BlockSpec((R, 3 * D), lambda n: (rev(n), 0)),
                   pl.BlockSpec((R, LANES), lambda n: (rev(n), 0)), pl.BlockSpec((R, LANES), lambda n: (rev(n), 0))],
        out_shape=[jax.ShapeDtypeStruct((T, 3 * D), F32), jax.ShapeDtypeStruct((T, LANES), F32),
                   jax.ShapeDtypeStruct((T, LANES), F32)],
        scratch_shapes=[pltpu.VMEM((H, HEAD_DIM, HEAD_DIM), F32)], compiler_params=_params(1),
    )(qkv, qkv, qkv, beta, gc, s_all, t_all, do)


def _gates_bwd(p, gate_params, col0, dbeta, dg, name):
    T = p.shape[0]
    tm = _pick(T, (256, 128, 64))

    def body(pb_ref, pa_ref, gp_ref, dbeta_ref, dg_ref, dp_ref, dgp_ref):
        gp = gp_ref[...]
        _, vjp = jax.vjp(_gates_math, pb_ref[...], pa_ref[...], gp[0:1, :], gp[1:2, :])
        dpb, dpa, d_alog, d_dt = vjp((dbeta_ref[...], dg_ref[...]))
        dp_ref[:, 0:LANES] = dpb.astype(BF16)
        dp_ref[:, LANES:2 * LANES] = dpa.astype(BF16)

        @pl.when(pl.program_id(0) == 0)
        def _():
            dgp_ref[...] = jnp.zeros_like(dgp_ref)

        dgp_ref[0:1, :] += d_alog
        dgp_ref[1:2, :] += d_dt

    return pl.pallas_call(
        body, name=name, grid=(T // tm,),
        in_specs=[pl.BlockSpec((tm, LANES), lambda i: (i, col0)), pl.BlockSpec((tm, LANES), lambda i: (i, col0 + 1)),
                  pl.BlockSpec((8, LANES), lambda i: (0, 0)),
                  pl.BlockSpec((tm, LANES), lambda i: (i, 0)), pl.BlockSpec((tm, LANES), lambda i: (i, 0))],
        out_specs=[pl.BlockSpec((tm, 2 * LANES), lambda i: (i, 0)), pl.BlockSpec((8, LANES), lambda i: (0, 0))],
        out_shape=[jax.ShapeDtypeStruct((T, 2 * LANES), BF16), jax.ShapeDtypeStruct((8, LANES), F32)],
        compiler_params=_params(1),
    )(p, p, gate_params, dbeta, dg)


def _conv_bwd_act(p, conv_w, dqkv, d_model, name):
    T = p.shape[0]
    D = d_model
    H = D // HEAD_DIM
    tm = _pick(T, (256, 128, 64))

    def columns(j, prev_outs):
        def body(cur_ref, prev_ref, w_ref, dout_ref, *rest):
            dc_ref, dw_ref = rest[-2:]
            i = pl.program_id(0)

            @pl.when(i == 0)
            def _():
                dw_ref[...] = jnp.zeros_like(dw_ref)

            first = (i > 0).astype(F32)
            for h in range(H):
                hs = slice(h * HEAD_DIM, (h + 1) * HEAD_DIM)
                taps = _conv_taps(cur_ref[:, hs], prev_ref[:, hs] * first)
                w = w_ref[:, hs]
                c = sum(taps[k] * w[k:k + 1, :] for k in range(4))
                _, vjp = jax.vjp(lambda cc: _qkv_post(cc, j, 1), c)
                (dc,) = vjp(dout_ref[:, hs])
                dc_ref[:, hs] = dc
                for k in range(4):
                    dw_ref[k:k + 1, hs] += jnp.sum(dc * taps[k], axis=0, keepdims=True)

        specs, operands, aliases = _alias_previous(prev_outs, 4)
        return pl.pallas_call(
            body, name="%s%d" % (name, j), grid=(T // tm,),
            in_specs=[pl.BlockSpec((tm, D), lambda i: (i, j)),
                      pl.BlockSpec((8, D), lambda i: (jnp.maximum(i * (tm // 8) - 1, 0), j)),
                      pl.BlockSpec((4, D), lambda i: (0, j)),
                      pl.BlockSpec((tm, D), lambda i: (i, j))] + specs,
            out_specs=[pl.BlockSpec((tm, D), lambda i: (i, j)), pl.BlockSpec((4, D), lambda i: (0, j))],
            out_shape=[jax.ShapeDtypeStruct((T, 3 * D), F32), jax.ShapeDtypeStruct((4, 3 * D), F32)],
            input_output_aliases=aliases, compiler_params=_params(1),
        )(p, p, conv_w, dqkv, *operands)

    return _column_calls(columns)


def _conv_bwd_input(dc, conv_w, name):
    T, D3 = dc.shape
    D = D3 // 3
    tm = _pick(T, (256, 128, 64))
    n_t = T // tm

    def body(cur_ref, next_ref, w_ref, dp_ref):
        i = pl.program_id(0)
        cur = cur_ref[...]
        nxt = next_ref[...] * (i < n_t - 1).astype(F32)
        ext = jnp.concatenate([cur, nxt], axis=0)
        w = w_ref[...]
        acc = cur * w[3:4, :]
        for s in (1, 2, 3):
            acc = acc + pltpu.roll(ext, tm + 8 - s, 0)[0:tm] * w[3 - s:4 - s, :]
        dp_ref[...] = acc.astype(BF16)

    return pl.pallas_call(
        body, name=name, grid=(n_t, 3),
        in_specs=[pl.BlockSpec((tm, D), lambda i, j: (i, j)),
                  pl.BlockSpec((8, D), lambda i, j: (jnp.minimum((i + 1) * (tm // 8), T // 8 - 1), j)),
                  pl.BlockSpec((4, D), lambda i, j: (0, j))],
        out_specs=pl.BlockSpec((tm, D), lambda i, j: (i, j)),
        out_shape=jax.ShapeDtypeStruct((T, D3), BF16), compiler_params=_params(2),
    )(dc, dc, conv_w)


def _comm_call(body, arrays, out_shape, n_remote, n_local, name):
    any_spec = pl.BlockSpec(memory_space=pl.ANY)
    return pl.pallas_call(
        body, name=name, in_specs=[any_spec] * len(arrays), out_specs=[any_spec] * len(out_shape), out_shape=out_shape,
        scratch_shapes=[pltpu.SemaphoreType.DMA((n_remote,)), pltpu.SemaphoreType.DMA((n_remote,)),
                        pltpu.SemaphoreType.DMA((n_local,))],
        compiler_params=pltpu.CompilerParams(has_side_effects=True),
    )(*arrays)


def _gather_shards(arrays, name):
    n = len(arrays)

    def body(*refs):
        ins, outs = refs[:n], refs[n:2 * n]
        send_sems, recv_sems, local_sems = refs[2 * n:]
        x, y, c = lax.axis_index("x"), lax.axis_index("y"), lax.axis_index("c")
        me, sibling = (x, y, c), (x, y, 1 - c)
        chips = [(1 - x, y), (x, 1 - y), (1 - x, 1 - y)]

        def copy(a, k, block, to, src=None):
            dst = outs[a].at[4 * block[0] + 2 * block[1] + block[2]]
            return pltpu.make_async_remote_copy(
                src_ref=dst if src is None else src, dst_ref=dst, send_sem=send_sems.at[a * 7 + k],
                recv_sem=recv_sems.at[a * 7 + k], device_id=to, device_id_type=MESH_ID)

        local = [pltpu.make_async_copy(ins[a], outs[a].at[4 * x + 2 * y + c], local_sems.at[a]) for a in range(n)]
        first = [copy(a, 0, me, sibling, src=ins[a]) for a in range(n)]
        first += [copy(a, 1 + j, me, (*chip, c), src=ins[a]) for j, chip in enumerate(chips) for a in range(n)]
        for cp in local + first:
            cp.start()
        passed = []
        for j, chip in enumerate(chips):
            for a in range(n):
                copy(a, 1 + j, (*chip, c), me).wait_recv()
                passed.append(copy(a, 4 + j, (*chip, c), sibling))
                passed[-1].start()
        for a in range(n):
            copy(a, 0, sibling, me).wait_recv()
            for j, chip in enumerate(chips):
                copy(a, 4 + j, (*chip, 1 - c), me).wait_recv()
        for cp in first + passed:
            cp.wait_send()
        for cp in local:
            cp.wait()

    out_shape = [jax.ShapeDtypeStruct((N_DEV,) + a.shape, a.dtype) for a in arrays]
    return _comm_call(body, arrays, out_shape, 7 * n, n, name)


def _pair_exchange(arrays, name):
    n = len(arrays)

    def body(*refs):
        ins, pair = refs[:n], refs[n:2 * n]
        send_sems, recv_sems, _ = refs[2 * n:]
        x, y, c = lax.axis_index("x"), lax.axis_index("y"), lax.axis_index("c")
        sends = [pltpu.make_async_remote_copy(
            src_ref=ins[a].at[1 - c], dst_ref=pair[a], send_sem=send_sems.at[a], recv_sem=recv_sems.at[a],
            device_id=(x, y, 1 - c), device_id_type=MESH_ID) for a in range(n)]
        for cp in sends:
            cp.start()
        for cp in sends:
            cp.wait_recv()
        for cp in sends:
            cp.wait_send()

    out_shape = [jax.ShapeDtypeStruct(a.shape[1:], a.dtype) for a in arrays]
    return _comm_call(body, arrays, out_shape, n, 1, name)


def _pair_add(own, pair, name, out_dtype):
    _, R, C = own.shape
    tr = next((t for t in (256, 128, 64, 32, 16) if R % t == 0), R)

    def body(a_ref, b_ref, o_ref):
        o_ref[...] = (a_ref[...].astype(F32) + b_ref[...].astype(F32)).astype(out_dtype)

    spec = pl.BlockSpec((1, tr, C), lambda q, i: (q, i, 0))
    return pl.pallas_call(body, name=name, grid=(4, R // tr), in_specs=[spec, spec], out_specs=spec,
                          out_shape=jax.ShapeDtypeStruct(own.shape, out_dtype), compiler_params=_params(2))(own, pair)


def _chip_exchange(arrays, name):
    n = len(arrays)

    def body(*refs):
        ins, outs = refs[:n], refs[n:2 * n]
        send_sems, recv_sems, local_sems = refs[2 * n:]
        x, y, c = lax.axis_index("x"), lax.axis_index("y"), lax.axis_index("c")
        my_chip = 2 * x + y
        chips = [(1 - x, y), (x, 1 - y), (1 - x, 1 - y)]
        local = [pltpu.make_async_copy(ins[a].at[my_chip], outs[a].at[my_chip], local_sems.at[a]) for a in range(n)]
        sends = [pltpu.make_async_remote_copy(
            src_ref=ins[a].at[2 * px + py], dst_ref=outs[a].at[my_chip], send_sem=send_sems.at[a * 3 + j],
            recv_sem=recv_sems.at[a * 3 + j], device_id=(px, py, c), device_id_type=MESH_ID)
            for j, (px, py) in enumerate(chips) for a in range(n)]
        arrivals = [pltpu.make_async_remote_copy(
            src_ref=ins[a].at[my_chip], dst_ref=outs[a].at[2 * px + py], send_sem=send_sems.at[a * 3 + j],
            recv_sem=recv_sems.at[a * 3 + j], device_id=(px, py, c), device_id_type=MESH_ID)
            for j, (px, py) in enumerate(chips) for a in range(n)]
        for cp in local + sends:
            cp.start()
        for cp in arrivals:
            cp.wait_recv()
        for cp in sends:
            cp.wait_send()
        for cp in local:
            cp.wait()

    out_shape = [jax.ShapeDtypeStruct(a.shape, a.dtype) for a in arrays]
    return _comm_call(body, arrays, out_shape, 3 * n, n, name)


def _reduce_adamw(recv, w, m, v, name):
    S, R, C = recv.shape
    tr = next((t for t in (256, 128, 64, 32, 16, 8) if R % t == 0), R)
    c1 = 1.0 - ADAM_B1 ** ADAM_STEP
    c2 = 1.0 - ADAM_B2 ** ADAM_STEP

    def body(r_ref, w_ref, m_ref, v_ref, g_ref, d_ref, nm_ref, nv_ref):
        g = r_ref[0].astype(F32)
        for s in range(1, S):
            g = g + r_ref[s].astype(F32)
        nm = ADAM_B1 * m_ref[...] + (1.0 - ADAM_B1) * g
        nv = ADAM_B2 * v_ref[...] + (1.0 - ADAM_B2) * (g * g)
        g_ref[...] = g
        nm_ref[...] = nm
        nv_ref[...] = nv
        d_ref[...] = -ADAM_LR * ((nm / c1) / (jnp.sqrt(nv / c2) + ADAM_EPS) + ADAM_WD * w_ref[...])

    spec = pl.BlockSpec((tr, C), lambda i: (i, 0))
    return pl.pallas_call(
        body, name=name, grid=(R // tr,),
        in_specs=[pl.BlockSpec((S, tr, C), lambda i: (0, i, 0)), spec, spec, spec], out_specs=[spec] * 4,
        out_shape=[jax.ShapeDtypeStruct((R, C), F32)] * 4, compiler_params=_params(1),
    )(recv, w, m, v)


def _forward_local(x, target, nw0, nw1, fw, wa_in, conv_w, gate_params, o_norm, wa_out, wb_in, wb_out):
    T, D = x.shape
    nD = D // LANES
    sv = {}
    sv["u0"], sv["pa"] = _mm_nn_from(_rmsnorm_head, [x, nw0], [0, None], wa_in, "a_in_proj", [(wa_in.shape[1], F32)])
    sv["qkv_a"] = _conv_fwd(sv["pa"], conv_w, D, "a_conv_fwd")
    sv["beta"], sv["gc"] = _gates_fwd(sv["pa"], gate_params, 4 * nD, "a_gates_fwd")
    sv["o_a"], sv["s_all"], sv["t_all"] = _chunk_fwd(sv["qkv_a"], sv["beta"], sv["gc"], D, "a_chunk_fwd")
    sv["y_a"], sv["h1"] = _mm_nn_from(_onorm_gate_head, [sv["o_a"], sv["pa"], o_norm], [0, 3, None], wa_out,
                                      "a_out_proj", [(D, F32)], add=x)
    sv["u1"], sv["qkv_b"], sv["gate_b"] = _mm_nn_from(_rmsnorm_head, [sv["h1"], nw1], [0, None], wb_in, "b_in_proj",
                                                      [(3 * D, BF16), (D, F32)])
    sv["o_b"], sv["r_b"] = _sb_fwd(sv["qkv_b"], D, "b_attn_fwd")
    sv["y_b"], sv["h2"] = _mm_nn_from(_gate_mul_head, [sv["o_b"], sv["gate_b"]], [0, 0], wb_out, "b_out_proj",
                                      [(D, F32)], add=sv["h1"])
    sv["dh2"], sv["loss"], sv["dfw"] = _final_loss(sv["h2"], fw, target, "final_loss")
    return sv


def _accumulate(acc_ref, value):
    @pl.when(pl.program_id(0) == 0)
    def _():
        acc_ref[...] = jnp.zeros_like(acc_ref)

    acc_ref[...] += jnp.broadcast_to(value, acc_ref.shape)


def _gate_mul_bwd_tail(dy, ins, outs):
    o_ref, g_ref = ins
    do_ref, dg_ref = outs
    g = g_ref[...]
    s = jax.nn.sigmoid(g)
    do_ref[...] = dy * (g * s)
    dg_ref[...] = (dy * o_ref[...] * (s + g * s * (1.0 - s))).astype(BF16)


def _rmsnorm_bwd_tail(du, ins, outs):
    x_ref, w_ref, dres_ref = ins
    dx_ref, dw_ref = outs
    dx, dw = _rms_bwd_math(x_ref[...], w_ref[...], du)
    dx_ref[...] = dres_ref[...] + dx
    _accumulate(dw_ref, dw)


def _onorm_gate_bwd_tail(dy, ins, outs):
    o_ref, z_ref, w_ref = ins
    do_ref, dz_ref, dw_ref = outs
    n_heads = o_ref.shape[1] // HEAD_DIM
    _, vjp = jax.vjp(functools.partial(_onorm_gate_math, n_heads=n_heads), o_ref[...], z_ref[...], w_ref[...])
    do, dz, dw = vjp(dy)
    do_ref[...] = do
    dz_ref[...] = dz.astype(BF16)
    _accumulate(dw_ref, dw)


def _backward_local(sv, x, nw0, nw1, wa_in, conv_w, gate_params, o_norm, wa_out, wb_in, wb_out):
    T, D = x.shape
    nD = D // LANES
    g = {}
    dh2 = sv["dh2"]
    act = lambda dtype: (jax.ShapeDtypeStruct((T, D), dtype), True)
    acc = lambda width: (jax.ShapeDtypeStruct((8, width), F32), False)
    g["wb_out"] = _mm_tn(sv["y_b"], dh2, "b_out_proj_dw")
    do_b, dgate_b = _mm_nt_then([(dh2, wb_out)], "b_out_proj_dx", [sv["o_b"], sv["gate_b"]], [0, 0],
                                _gate_mul_bwd_tail, [act(F32), act(BF16)])
    dq_b, dk_b, dv_b = _sb_bwd(sv["qkv_b"], do_b, sv["r_b"], D, "b_attn_bwd")
    dp_b = [dq_b, dk_b, dv_b, dgate_b]
    g["wb_in"] = jnp.concatenate([_mm_tn(sv["u1"], dp, "b_in_proj_dw%d" % c) for c, dp in enumerate(dp_b)], axis=1)
    dh1, g["nw1"] = _mm_nt_then([(dp, wb_in[:, c * D:(c + 1) * D]) for c, dp in enumerate(dp_b)], "b_in_proj_dx",
                                [sv["h1"], nw1, dh2], [0, None, 0], _rmsnorm_bwd_tail, [act(F32), acc(D)])
    g["wa_out"] = _mm_tn(sv["y_a"], dh1, "a_out_proj_dw")
    do_a, dz_a, g["o_norm"] = _mm_nt_then([(dh1, wa_out)], "a_out_proj_dx", [sv["o_a"], sv["pa"], o_norm], [0, 3, None],
                                          _onorm_gate_bwd_tail, [act(F32), act(BF16), acc(HEAD_DIM)])
    dqkv_a, dbeta, dg = _chunk_bwd(sv["qkv_a"], sv["beta"], sv["gc"], sv["s_all"], sv["t_all"], do_a, D, "a_chunk_bwd")
    dp_gates, g["gate_params"] = _gates_bwd(sv["pa"], gate_params, 4 * nD, dbeta, dg, "a_gates_bwd")
    dc, g["conv_w"] = _conv_bwd_act(sv["pa"], conv_w, dqkv_a, D, "a_conv_bwd_act")
    dp_qkv = _conv_bwd_input(dc, conv_w, "a_conv_bwd_input")
    dp_a = [(dp_qkv, 0, 3 * D), (dz_a, 3 * D, 4 * D), (dp_gates, 4 * D, 4 * D + 2 * LANES)]
    g["wa_in"] = jnp.concatenate([_mm_tn(sv["u0"], dp, "a_in_proj_dw%d" % c) for c, (dp, _, _) in enumerate(dp_a)], axis=1)
    g["x"], g["nw0"] = _mm_nt_then([(dp, wa_in[:, lo:hi]) for dp, lo, hi in dp_a], "a_in_proj_dx",
                                   [x, nw0, dh1], [0, None, 0], _rmsnorm_bwd_tail, [act(F32), acc(D)])
    g["fw"] = sv["dfw"]
    return g


def kernel(x, norm_w, a_w_in, a_conv_w, a_a_log, a_dt_bias, a_o_norm, a_w_out, b_w_in, b_w_out, final_norm_w, loss_target, m_norm_w, m_a_w_in, m_a_conv_w, m_a_a_log, m_a_dt_bias, m_a_o_norm, m_a_w_out, m_b_w_in, m_b_w_out, m_final_norm_w, v_norm_w, v_a_w_in, v_a_conv_w, v_a_a_log, v_a_dt_bias, v_a_o_norm, v_a_w_out, v_b_w_in, v_b_w_out, v_final_norm_w):
    D = x.shape[-1]
    H = D // HEAD_DIM
    shards = [a_w_in[0].astype(BF16), a_w_out[0].astype(BF16), b_w_in[0].astype(BF16), b_w_out[0].astype(BF16), a_conv_w[0]]
    ga_in, ga_out, gb_in, gb_out, g_conv = _gather_shards(shards, "weights_gather")
    wa = ga_in.transpose(1, 0, 2).reshape(D, -1)
    pad = lambda w: jnp.pad(w, ((0, 0), (0, LANES - w.shape[1])))
    wa_in = jnp.concatenate([wa[:, :4 * D], pad(wa[:, 4 * D:4 * D + H]), pad(wa[:, 4 * D + H:])], axis=1)
    wa_out = ga_out.reshape(D, D)
    wb_in = gb_in.transpose(1, 0, 2).reshape(D, 4 * D)
    wb_out = gb_out.reshape(D, D)
    conv_w = g_conv.transpose(1, 0, 2).reshape(4, 3 * D)
    gate_params = jnp.zeros((8, LANES), F32).at[0, :H].set(a_a_log[0]).at[1, :H].set(a_dt_bias[0])
    nw0, nw1, fw = norm_w[0:1], norm_w[1:2], final_norm_w[None]

    sv = _forward_local(x[0], loss_target[0], nw0, nw1, fw, wa_in, conv_w, gate_params, a_o_norm, wa_out, wb_in, wb_out)
    g = _backward_local(sv, x[0], nw0, nw1, wa_in, conv_w, gate_params, a_o_norm, wa_out, wb_in, wb_out)

    gwa = g["wa_in"]
    gwa = jnp.concatenate([gwa[:, :4 * D], gwa[:, 4 * D:4 * D + H], gwa[:, 4 * D + LANES:4 * D + LANES + H]], axis=1)
    row = lambda v: jnp.pad(v.reshape(1, -1), ((0, 0), (0, D - v.size)))
    small = jnp.concatenate([g["nw0"][0:1], g["nw1"][0:1], g["fw"][0:1], row(g["gate_params"][0, :H]),
                             row(g["gate_params"][1, :H]), row(g["o_norm"][0]), row(sv["loss"][0, 0:1]),
                             jnp.zeros((1, D), F32)], axis=0)
    cols = lambda a: a.astype(BF16).reshape(a.shape[0], 4, 2, -1).transpose(2, 1, 0, 3)
    rows = lambda a: a.astype(BF16).reshape(4, 2, -1, a.shape[1]).transpose(1, 0, 2, 3)
    contribs = [cols(gwa), rows(g["wa_out"]), cols(g["wb_in"]), rows(g["wb_out"]), cols(g["conv_w"]),
                jnp.broadcast_to(small[None, None], (2, 4, 8, D))]
    pair = _pair_exchange(contribs, "grads_pair_exchange")
    my_core = lax.axis_index("c")
    own = [lax.dynamic_index_in_dim(a, my_core, axis=0, keepdims=False) for a in contribs]
    partial = [_pair_add(o, p, "grads_pair_add%d" % k, o.dtype) for k, (o, p) in enumerate(zip(own, pair))]
    ra_in, ra_out, rb_in, rb_out, r_conv, r_small = _chip_exchange(partial, "grads_chip_exchange")

    outs = {}
    for nm, recv, w, m, v in (("a_w_in", ra_in, a_w_in, m_a_w_in, v_a_w_in), ("a_w_out", ra_out, a_w_out, m_a_w_out, v_a_w_out),
                              ("b_w_in", rb_in, b_w_in, m_b_w_in, v_b_w_in), ("b_w_out", rb_out, b_w_out, m_b_w_out, v_b_w_out),
                              ("a_conv_w", r_conv, a_conv_w, m_a_conv_w, v_a_conv_w)):
        outs[nm] = tuple(o[None] for o in _reduce_adamw(recv, w[0], m[0], v[0], "adamw_" + nm))

    def pack(nw, alog, dt, onorm, fnw):
        return jnp.concatenate([nw, fnw.reshape(1, D), row(alog), row(dt), row(onorm), jnp.zeros((2, D), F32)], axis=0)

    s_g, s_d, s_m, s_v = _reduce_adamw(
        r_small, pack(norm_w, a_a_log, a_dt_bias, a_o_norm, final_norm_w),
        pack(m_norm_w, m_a_a_log, m_a_dt_bias, m_a_o_norm, m_final_norm_w),
        pack(v_norm_w, v_a_a_log, v_a_dt_bias, v_a_o_norm, v_final_norm_w), "adamw_small")
    loss = s_g[6, 0]
    for i, s in enumerate((s_g, s_d, s_m, s_v)):
        outs.setdefault("norm_w", [None] * 4)[i] = s[0:2]
        outs.setdefault("final_norm_w", [None] * 4)[i] = s[2]
        outs.setdefault("a_a_log", [None] * 4)[i] = s[3:4, :H]
        outs.setdefault("a_dt_bias", [None] * 4)[i] = s[4:5, :H]
        outs.setdefault("a_o_norm", [None] * 4)[i] = s[5:6, :HEAD_DIM]
    names = ("norm_w", "a_w_in", "a_conv_w", "a_a_log", "a_dt_bias", "a_o_norm", "a_w_out", "b_w_in", "b_w_out", "final_norm_w")
    return (loss, g["x"][None]) + tuple(outs[n][i] for i in range(4) for n in names)
```
